```python
import math
import jax
import jax.numpy as jnp
from jax import lax
import numpy as np

D_MODEL = 1024
BATCH = 4
SEQ = 8192
DEPTH = 2

NSA_HEADS = 8
NSA_GROUPS = 2
NSA_HPG = NSA_HEADS // NSA_GROUPS
HEAD_DIM = 64
NSA_WIDTH = NSA_HEADS * HEAD_DIM
NSA_KV = NSA_GROUPS * HEAD_DIM
CMP_STRIDE = 16
CMP_LEN = 2 * CMP_STRIDE
CMP_HIDDEN = 256
SEL_BLOCK = 64
SEL_TOPN = 16
WINDOW = 512
Q_BLOCK = 128
SEL_FORCE = 1e9

RWKV_HEADS = 8
RWKV_HEAD = 64
RWKV_WIDTH = RWKV_HEADS * RWKV_HEAD
LORA_W = 64
LORA_A = 64
LORA_V = 32
LORA_G = 128
GN_EPS = 64e-5

REL_BUCKETS = 32
REL_MAX_DIST = 1024

MEM_LEN = 256
XATTN_HEADS = 4
XATTN_HEAD = 128
XATTN_WIDTH = XATTN_HEADS * XATTN_HEAD

N_EXPERTS = 16
N_EXPERT_GROUPS = 4
EXPERTS_PER_GROUP = N_EXPERTS // N_EXPERT_GROUPS
TOP_K = 2
EXPERT_FF = 512
MOE_BLOCK = 128

DN_ALPHA = (2 * DEPTH) ** 0.25
DN_BETA = (8 * DEPTH) ** -0.25
LN_EPS = 1e-5
NEG_INF = -1e30

NSA_COLS = NSA_WIDTH + 6 * NSA_KV + 3 * NSA_HEADS
RWKV_COLS = 3 * RWKV_WIDTH + LORA_W + LORA_A + LORA_G
GATE_COLS = 2 * D_MODEL
IN_COLS = NSA_COLS + RWKV_COLS + GATE_COLS

kernel_name = 'hybrid_nsa_rwkv7_grouped_moe_deepnorm'


def _split(z, sizes):
    return jnp.split(z, np.cumsum(sizes)[:-1].tolist(), axis=-1)


def layer_norm(x, g, b):
    xf = x.astype(jnp.float32)
    mu = jnp.mean(xf, axis=-1, keepdims=True)
    var = jnp.mean(jnp.square(xf - mu), axis=-1, keepdims=True)
    return ((xf - mu) * lax.rsqrt(var + LN_EPS) * g + b).astype(x.dtype)


def masked_softmax(s, mask):
    s = jnp.where(mask, s.astype(jnp.float32), NEG_INF)
    m = jnp.max(s, axis=-1, keepdims=True)
    p = jnp.where(mask, jnp.exp(s - m), 0.0)
    return p / jnp.maximum(jnp.sum(p, axis=-1, keepdims=True), 1e-30)


def t5_bucket(dist):
    n = jnp.maximum(dist, 0)
    max_exact = REL_BUCKETS // 2
    log_ratio = jnp.log(jnp.maximum(n, 1).astype(jnp.float32) / max_exact) / math.log(REL_MAX_DIST / max_exact)
    large = jnp.minimum(max_exact + (log_ratio * (REL_BUCKETS - max_exact)).astype(jnp.int32), REL_BUCKETS - 1)
    return jnp.where(n < max_exact, n, large)


def token_shift_mix(z, mu):
    z_prev = jnp.pad(z[:, :-1], ((0, 0), (1, 0), (0, 0)))
    return z + mu * (z_prev - z)


def _compress(z, pe, w1, w2):
    B, T, G, dh = z.shape
    n_cmp = T // CMP_STRIDE - 1
    ch = z.reshape(B, T // CMP_STRIDE, CMP_STRIDE, G, dh)
    blocks = jnp.concatenate([ch[:, :-1], ch[:, 1:]], axis=2) + pe[:, None, :]
    flat = jnp.moveaxis(blocks, 3, 2).reshape(B, n_cmp, G, CMP_LEN * dh)
    return jax.nn.gelu(flat @ w1) @ w2


def _gather_blocks(blocks, idx):
    return jax.vmap(jax.vmap(lambda b, i: b[i]))(blocks, idx)


def nsa_attention(q, kc, vc, ks, vs, kw, vw, gate_logits, rel_table, pe_k, pe_v, w1k, w2k, w1v, w2v):
    B, T, _ = q.shape
    G, Hg, dh = NSA_GROUPS, NSA_HPG, HEAD_DIM
    scale = HEAD_DIM ** -0.5
    f32 = jnp.float32
    q = q.reshape(B, T, G, Hg, dh)
    kv = lambda z: z.reshape(B, T, G, dh)
    k_cmp = _compress(kv(kc), pe_k, w1k, w2k)
    v_cmp = _compress(kv(vc), pe_v, w1v, w2v)
    n_cmp = k_cmp.shape[1]
    n_sel = T // SEL_BLOCK
    n_top = min(SEL_TOPN, n_sel)
    k_blk = kv(ks).reshape(B, n_sel, SEL_BLOCK, G, dh).transpose(0, 3, 1, 2, 4)
    v_blk = kv(vs).reshape(B, n_sel, SEL_BLOCK, G, dh).transpose(0, 3, 1, 2, 4)
    pad = ((0, 0), (WINDOW, 0), (0, 0), (0, 0))
    k_win = jnp.pad(kv(kw), pad)
    v_win = jnp.pad(kv(vw), pad)
    gates = jax.nn.sigmoid(gate_logits.astype(f32)).reshape(B, T, G, Hg, 3)
    cmp_start = jnp.arange(n_cmp) * CMP_STRIDE
    cmp_end = cmp_start + CMP_LEN - 1
    sel_start = jnp.arange(n_sel) * SEL_BLOCK
    overlap = ((cmp_start[:, None] <= sel_start[None, :] + SEL_BLOCK - 1)
               & (cmp_end[:, None] >= sel_start[None, :])).astype(f32)
    tbl_g = rel_table.reshape(REL_BUCKETS, G, Hg).transpose(1, 0, 2).astype(f32)

    def bias_2d(dist):
        return rel_table[t5_bucket(dist)].reshape(dist.shape + (G, Hg)).transpose(2, 3, 0, 1).astype(f32)

    def block(c):
        t0 = c * Q_BLOCK
        tq = t0 + jnp.arange(Q_BLOCK)
        qc = lax.dynamic_slice_in_dim(q, t0, Q_BLOCK, axis=1)
        gc = lax.dynamic_slice_in_dim(gates, t0, Q_BLOCK, axis=1)
        mask_c = cmp_end[None, :] <= tq[:, None]
        s_c = jnp.einsum('bqghd,bngd->bghqn', qc, k_cmp).astype(f32) * scale + bias_2d(tq[:, None] - cmp_end[None, :])
        p_c = masked_softmax(s_c, mask_c)
        o_c = jnp.einsum('bghqn,bngd->bqghd', p_c, v_cmp.astype(f32))
        imp = jnp.einsum('bghqn,nj->bgqj', p_c, overlap)
        jq = tq // SEL_BLOCK
        jb = jnp.arange(n_sel)
        valid = jb[None, :] <= jq[:, None]
        forced = (jb[None, :] == 0) | (jb[None, :] == jq[:, None]) | (jb[None, :] == jq[:, None] - 1)
        imp = jnp.where(forced, SEL_FORCE, jnp.where(valid, imp, -SEL_FORCE))
        top_v, top_i = lax.top_k(imp, n_top)
        sel_ok = top_v >= 0.0
        k_sel = _gather_blocks(k_blk, top_i).reshape(B, G, Q_BLOCK, n_top * SEL_BLOCK, dh)
        v_sel = _gather_blocks(v_blk, top_i).reshape(B, G, Q_BLOCK, n_top * SEL_BLOCK, dh)
        kpos = (top_i[..., None] * SEL_BLOCK + jnp.arange(SEL_BLOCK)).reshape(B, G, Q_BLOCK, n_top * SEL_BLOCK)
        mask_s = (kpos <= tq[:, None]) & jnp.repeat(sel_ok, SEL_BLOCK, axis=-1)
        bias_s = jax.vmap(lambda t, b: t[b], in_axes=(0, 1), out_axes=1)(tbl_g, t5_bucket(tq[:, None] - kpos))
        bias_s = jnp.moveaxis(bias_s, -1, 2)
        s_s = jnp.einsum('bqghd,bgqkd->bghqk', qc, k_sel).astype(f32) * scale + bias_s
        p_s = masked_softmax(s_s, mask_s[:, :, None])
        o_s = jnp.einsum('bghqk,bgqkd->bqghd', p_s, v_sel.astype(f32))
        kwc = lax.dynamic_slice_in_dim(k_win, t0, Q_BLOCK + WINDOW, axis=1)
        vwc = lax.dynamic_slice_in_dim(v_win, t0, Q_BLOCK + WINDOW, axis=1)
        kpos_w = t0 - WINDOW + jnp.arange(Q_BLOCK + WINDOW)
        dist_w = tq[:, None] - kpos_w[None, :]
        mask_w = (dist_w >= 0) & (dist_w < WINDOW) & (kpos_w[None, :] >= 0)
        s_w = jnp.einsum('bqghd,bkgd->bghqk', qc, kwc).astype(f32) * scale + bias_2d(dist_w)
        p_w = masked_softmax(s_w, mask_w)
        o_w = jnp.einsum('bghqk,bkgd->bqghd', p_w, vwc.astype(f32))
        o = gc[..., 0:1] * o_c + gc[..., 1:2] * o_s + gc[..., 2:3] * o_w
        return o.reshape(B, Q_BLOCK, NSA_WIDTH).astype(q.dtype)

    out = lax.map(block, jnp.arange(T // Q_BLOCK))
    return jnp.moveaxis(out, 0, 1).reshape(B, T, NSA_WIDTH)


def rwkv7_time_mix(r, k, v, zw, za, zg, w0, w2, a0, a2, g2, k_k, k_a, r_k, gn_g, gn_b):
    B, T, C = r.shape
    H, N = RWKV_HEADS, RWKV_HEAD
    f32 = jnp.float32
    r, k, v = r.astype(f32), k.astype(f32), v.astype(f32)
    log_w = -jax.nn.softplus(-(w0 + jnp.tanh(zw) @ w2).astype(f32)) - 0.5
    decay = jnp.exp(-jnp.exp(log_w))
    a = jax.nn.sigmoid((a0 + za @ a2).astype(f32))
    g = (jax.nn.sigmoid(zg) @ g2).astype(f32)
    kk = (k * k_k).reshape(B, T, H, N)
    kk = kk / jnp.maximum(jnp.sqrt(jnp.sum(kk * kk, axis=-1, keepdims=True)), 1e-12)
    k = k * (1.0 + (a - 1.0) * k_a)
    heads = lambda t: jnp.moveaxis(t.reshape(B, T, H, N), 1, 0)

    def step(S, inp):
        r_t, w_t, k_t, v_t, kk_t, a_t = inp
        S = (S * w_t[:, :, None, :]
             - jnp.einsum('bhij,bhj->bhi', S, kk_t)[..., None] * (kk_t * a_t)[:, :, None, :]
             + v_t[..., None] * k_t[:, :, None, :])
        return S, jnp.einsum('bhij,bhj->bhi', S, r_t)

    S0 = jnp.zeros((B, H, N, N), f32)
    _, o = lax.scan(step, S0, (heads(r), heads(decay), heads(k), heads(v), jnp.moveaxis(kk, 1, 0), heads(a)))
    o = jnp.moveaxis(o, 0, 1)
    mu = jnp.mean(o, axis=-1, keepdims=True)
    var = jnp.mean(jnp.square(o - mu), axis=-1, keepdims=True)
    o = ((o - mu) * lax.rsqrt(var + GN_EPS)).reshape(B, T, C) * gn_g + gn_b
    bonus = jnp.sum((r * k).reshape(B, T, H, N) * r_k, axis=-1, keepdims=True) * v.reshape(B, T, H, N)
    return (o + bonus.reshape(B, T, C)) * g


def memory_cross_attention(x, mem, wq, wk, wv, wo):
    B, T, _ = x.shape
    M = mem.shape[1]
    q = (x @ wq).reshape(B, T, XATTN_HEADS, XATTN_HEAD)
    k = (mem @ wk).reshape(B, M, XATTN_HEADS, XATTN_HEAD)
    v = (mem @ wv).reshape(B, M, XATTN_HEADS, XATTN_HEAD)
    s = jnp.einsum('bqhd,bkhd->bhqk', q, k).astype(jnp.float32) * XATTN_HEAD ** -0.5
    p = jax.nn.softmax(s, axis=-1)
    o = jnp.einsum('bhqk,bkhd->bqhd', p, v.astype(jnp.float32)).astype(x.dtype)
    return o.reshape(B, T, XATTN_WIDTH) @ wo


def grouped_moe(x, router_w, router_bias, w_gate, w_up, w_down):
    B, T, D = x.shape
    n_tok = B * T
    xf = x.reshape(n_tok, D)
    s = jax.nn.sigmoid((xf @ router_w).astype(jnp.float32))
    s_sel = s + router_bias.astype(jnp.float32)
    group_score = jnp.sum(lax.top_k(s_sel.reshape(n_tok, N_EXPERT_GROUPS, EXPERTS_PER_GROUP), 2)[0], axis=-1)
    group = jnp.argmax(group_score, axis=-1)
    in_group = (jnp.arange(N_EXPERTS) // EXPERTS_PER_GROUP)[None, :] == group[:, None]
    _, e_idx = lax.top_k(jnp.where(in_group, s_sel, NEG_INF), TOP_K)
    s_top = jnp.take_along_axis(s, e_idx, axis=-1)
    gate = s_top / jnp.sum(s_top, axis=-1, keepdims=True)
    n_asg = n_tok * TOP_K
    e_flat = e_idx.reshape(-1)
    order = jnp.argsort(e_flat)
    e_sorted = e_flat[order]
    tok_sorted = order // TOP_K
    gate_sorted = gate.reshape(-1)[order]
    counts = jnp.bincount(e_flat, length=N_EXPERTS)
    padded = (counts + MOE_BLOCK - 1) // MOE_BLOCK * MOE_BLOCK
    dest = (jnp.cumsum(padded) - padded)[e_sorted] + jnp.arange(n_asg) - (jnp.cumsum(counts) - counts)[e_sorted]
    n_rows = n_asg + N_EXPERTS * MOE_BLOCK
    n_blocks = n_rows // MOE_BLOCK
    rows = jnp.zeros((n_rows, D), x.dtype).at[dest].set(xf[tok_sorted])
    block_expert = jnp.minimum(jnp.searchsorted(jnp.cumsum(padded), jnp.arange(n_blocks) * MOE_BLOCK, side='right'), N_EXPERTS - 1)

    def expert_block(args):
        xb, e = args
        return (jax.nn.silu(xb @ w_gate[e]) * (xb @ w_up[e])) @ w_down[e]

    y_rows = lax.map(expert_block, (rows.reshape(n_blocks, MOE_BLOCK, D), block_expert)).reshape(n_rows, D)
    y = jax.ops.segment_sum(y_rows[dest] * gate_sorted[:, None], tok_sorted, num_segments=n_tok)
    return y.astype(x.dtype).reshape(B, T, D)


def setup_inputs(seed: int = 0) -> dict:
    key = jax.random.key(seed)
    keys = iter(jax.random.split(key, 64))
    f32 = jnp.float32
    L, D = DEPTH, D_MODEL

    def nrm(shape, scale):
        return jax.random.normal(next(keys), shape, f32) * scale

    def gain(shape):
        return 1.0 + nrm(shape, 0.02)

    def unif(shape, lo, hi):
        return jax.random.uniform(next(keys), shape, f32, lo, hi)

    return {
        'x': nrm((BATCH, SEQ, D), 1.0),
        'mem': nrm((BATCH, MEM_LEN, D), 1.0),
        'rel_table': nrm((REL_BUCKETS, NSA_HEADS), 0.5),
        'router_w': nrm((D, N_EXPERTS), D ** -0.5),
        'router_bias': nrm((N_EXPERTS,), 0.01),
        'w_in': nrm((L, D, IN_COLS), D ** -0.5),
        'cmp_pe_k': nrm((L, CMP_LEN, HEAD_DIM), 0.1),
        'cmp_pe_v': nrm((L, CMP_LEN, HEAD_DIM), 0.1),
        'cmp_w1k': nrm((L, CMP_LEN * HEAD_DIM, CMP_HIDDEN), (CMP_LEN * HEAD_DIM) ** -0.5),
        'cmp_w2k': nrm((L, CMP_HIDDEN, HEAD_DIM), CMP_HIDDEN ** -0.5),
        'cmp_w1v': nrm((L, CMP_LEN * HEAD_DIM, CMP_HIDDEN), (CMP_LEN * HEAD_DIM) ** -0.5),
        'cmp_w2v': nrm((L, CMP_HIDDEN, HEAD_DIM), CMP_HIDDEN ** -0.5),
        'rwkv_mu': unif((L, RWKV_COLS), 0.0, 1.0),
        'rwkv_w0': unif((L, RWKV_WIDTH), -6.0, -1.0),
        'rwkv_w2': nrm((L, LORA_W, RWKV_WIDTH), 0.1),
        'rwkv_a0': nrm((L, RWKV_WIDTH), 0.1),
        'rwkv_a2': nrm((L, LORA_A, RWKV_WIDTH), LORA_A ** -0.5),
        'rwkv_g2': nrm((L, LORA_G, RWKV_WIDTH), LORA_G ** -0.5),
        'rwkv_kk': 0.85 + nrm((L, RWKV_WIDTH), 0.02),
        'rwkv_ka': gain((L, RWKV_WIDTH)),
        'rwkv_rk': nrm((L, RWKV_HEADS, RWKV_HEAD), 0.1),
        'rwkv_gn_g': gain((L, RWKV_WIDTH)),
        'rwkv_gn_b': nrm((L, RWKV_WIDTH), 0.02),
        'rwkv_v0': nrm((L - 1, RWKV_WIDTH), 0.1),
        'rwkv_v1': nrm((L - 1, RWKV_WIDTH, LORA_V), RWKV_WIDTH ** -0.5),
        'rwkv_v2': nrm((L - 1, LORA_V, RWKV_WIDTH), LORA_V ** -0.5),
        'p_nsa': nrm((L, NSA_WIDTH, D), NSA_WIDTH ** -0.5),
        'p_rwkv': nrm((L, RWKV_WIDTH, D), RWKV_WIDTH ** -0.5),
        'w_out': nrm((L, D, D), D ** -0.5 * DN_BETA),
        'ln1_g': gain((L, D)),
        'ln1_b': nrm((L, D), 0.02),
        'xq_w': nrm((L, D, XATTN_WIDTH), D ** -0.5),
        'xk_w': nrm((L, D, XATTN_WIDTH), D ** -0.5),
        'xv_w': nrm((L, D, XATTN_WIDTH), D ** -0.5 * DN_BETA),
        'xo_w': nrm((L, XATTN_WIDTH, D), XATTN_WIDTH ** -0.5 * DN_BETA),
        'ln2_g': gain((L, D)),
        'ln2_b': nrm((L, D), 0.02),
        'moe_w_gate': nrm((L, N_EXPERTS, D, EXPERT_FF), D ** -0.5),
        'moe_w_up': nrm((L, N_EXPERTS, D, EXPERT_FF), D ** -0.5),
        'moe_w_down': nrm((L, N_EXPERTS, EXPERT_FF, D), EXPERT_FF ** -0.5 * DN_BETA),
        'ln3_g': gain((L, D)),
        'ln3_b': nrm((L, D), 0.02),
    }


def reference(x, mem, rel_table, router_w, router_bias, w_in, cmp_pe_k, cmp_pe_v, cmp_w1k, cmp_w2k, cmp_w1v, cmp_w2v,
              rwkv_mu, rwkv_w0, rwkv_w2, rwkv_a0, rwkv_a2, rwkv_g2, rwkv_kk, rwkv_ka, rwkv_rk, rwkv_gn_g, rwkv_gn_b,
              rwkv_v0, rwkv_v1, rwkv_v2, p_nsa, p_rwkv, w_out, ln1_g, ln1_b, xq_w, xk_w, xv_w, xo_w, ln2_g, ln2_b,
              moe_w_gate, moe_w_up, moe_w_down, ln3_g, ln3_b):
    v_first = None
    for l in range(DEPTH):
        z_nsa, z_rwkv, z_gate = _split(x @ w_in[l], (NSA_COLS, RWKV_COLS, GATE_COLS))
        q, kc, vc, ks, vs, kw, vw, g_nsa = _split(z_nsa, (NSA_WIDTH,) + (NSA_KV,) * 6 + (3 * NSA_HEADS,))
        o_nsa = nsa_attention(q, kc, vc, ks, vs, kw, vw, g_nsa, rel_table, cmp_pe_k[l], cmp_pe_v[l],
                              cmp_w1k[l], cmp_w2k[l], cmp_w1v[l], cmp_w2v[l])
        r, k, v, zw, za, zg = _split(token_shift_mix(z_rwkv, rwkv_mu[l]), (RWKV_WIDTH,) * 3 + (LORA_W, LORA_A, LORA_G))
        if l == 0:
            v_first = v
        else:
            v = v + (v_first - v) * jax.nn.sigmoid(rwkv_v0[l - 1] + (v @ rwkv_v1[l - 1]) @ rwkv_v2[l - 1])
        o_rwkv = rwkv7_time_mix(r, k, v, zw, za, zg, rwkv_w0[l], rwkv_w2[l], rwkv_a0[l], rwkv_a2[l], rwkv_g2[l],
                                rwkv_kk[l], rwkv_ka[l], rwkv_rk[l], rwkv_gn_g[l], rwkv_gn_b[l]).astype(x.dtype)
        gate_nsa, gate_rwkv = _split(jax.nn.sigmoid(z_gate), (D_MODEL, D_MODEL))
        y = (gate_nsa * (o_nsa @ p_nsa[l]) + gate_rwkv * (o_rwkv @ p_rwkv[l])) @ w_out[l]
        x = layer_norm(DN_ALPHA * x + y, ln1_g[l], ln1_b[l])
        y = memory_cross_attention(x, mem, xq_w[l], xk_w[l], xv_w[l], xo_w[l])
        x = layer_norm(DN_ALPHA * x + y, ln2_g[l], ln2_b[l])
        y = grouped_moe(x, router_w, router_bias, moe_w_gate[l], moe_w_up[l], moe_w_down[l])
        x = layer_norm(DN_ALPHA * x + y, ln3_g[l], ln3_b[l])
    return x
```

```python
import functools
import math

import numpy as np
import jax
import jax.numpy as jnp
from jax import lax
from jax.experimental import pallas as pl
from jax.experimental.pallas import tpu as pltpu

F32 = jnp.float32
BF16 = jnp.bfloat16

D_MODEL = 1024
DEPTH = 2
NSA_HEADS = 8
NSA_GROUPS = 2
NSA_HPG = NSA_HEADS // NSA_GROUPS
HEAD_DIM = 64
NSA_WIDTH = NSA_HEADS * HEAD_DIM
NSA_KV = NSA_GROUPS * HEAD_DIM
CMP_STRIDE = 16
CMP_LEN = 2 * CMP_STRIDE
CMP_HIDDEN = 256
SEL_BLOCK = 64
SEL_TOPN = 16
WINDOW = 512
Q_BLOCK = 128
SEL_FORCE = 1e9
RWKV_HEADS = 8
RWKV_HEAD = 64
RWKV_WIDTH = RWKV_HEADS * RWKV_HEAD
LORA_W = 64
LORA_A = 64
LORA_V = 32
LORA_G = 128
GN_EPS = 64e-5
REL_BUCKETS = 32
REL_MAX_DIST = 1024
XATTN_HEADS = 4
XATTN_HEAD = 128
XATTN_WIDTH = XATTN_HEADS * XATTN_HEAD
N_EXPERTS = 16
N_EXPERT_GROUPS = 4
EXPERTS_PER_GROUP = N_EXPERTS // N_EXPERT_GROUPS
TOP_K = 2
EXPERT_FF = 512
DN_ALPHA = (2 * DEPTH) ** 0.25
LN_EPS = 1e-5
NEG_INF = -1e30
NSA_COLS = NSA_WIDTH + 6 * NSA_KV + 3 * NSA_HEADS
RWKV_COLS = 3 * RWKV_WIDTH + LORA_W + LORA_A + LORA_G

LANE = 128
VMEM_LIMIT = 56 * 1024 * 1024

C_Q = 0
C_GN = 1024
C_GR = 2048
C_R = 3072
C_K = 3584
C_V = 4096
C_KV = 4608
C_G3 = 5376
C_WA = 5632
C_ZG = 5760
IN_PAD = 6144

MOE_BLK = 256
SCAN_TT = 128
FAR_BIAS_DIST = 1280


def _cparams(sem):
    return pltpu.CompilerParams(dimension_semantics=sem, vmem_limit_bytes=VMEM_LIMIT)


def _sigmoid(x):
    return 1.0 / (1.0 + jnp.exp(-x))


def _layer_norm(v, g, b):
    mu = jnp.mean(v, axis=-1, keepdims=True)
    d = v - mu
    var = jnp.mean(d * d, axis=-1, keepdims=True)
    return d * lax.rsqrt(var + LN_EPS) * g + b


def _dot(a, b):
    return jnp.dot(a, b, preferred_element_type=F32)


def _dot_nt(a, b):
    return lax.dot_general(a, b, (((1,), (1,)), ((), ())), preferred_element_type=F32)


def _dot_split(x, w):
    hi = x.astype(BF16)
    lo = (x - hi.astype(F32)).astype(BF16)
    return _dot(hi, w) + _dot(lo, w)


def _mm_kernel(x_ref, w_ref, o_ref, xb_ref):
    @pl.when(pl.program_id(1) == 0)
    def _():
        xb_ref[...] = x_ref[...].astype(BF16)

    o_ref[...] = _dot(xb_ref[...], w_ref[...]).astype(o_ref.dtype)


def _matmul(x, w, out_dtype, tm, tn):
    n, k = x.shape
    m = w.shape[1]
    return pl.pallas_call(
        _mm_kernel,
        grid=(n // tm, m // tn),
        in_specs=[pl.BlockSpec((tm, k), lambda i, j: (i, 0)),
                  pl.BlockSpec((k, tn), lambda i, j: (0, j))],
        out_specs=pl.BlockSpec((tm, tn), lambda i, j: (i, j)),
        out_shape=jax.ShapeDtypeStruct((n, m), out_dtype),
        scratch_shapes=[pltpu.VMEM((tm, k), BF16)],
        compiler_params=_cparams(("parallel", "arbitrary")),
        name="matmul",
    )(x, w)


def _compress_kernel(u_ref, pe_ref, w1_ref, w2_ref, o_ref):
    u = u_ref[0, 0]
    a = _dot((u + pe_ref[0:1, :]).astype(BF16), w1_ref[0])
    b = _dot((u + pe_ref[1:2, :]).astype(BF16), w1_ref[1])
    nc = u.shape[0]
    h = a + pltpu.roll(b, nc - 1, 0)
    h = 0.5 * h * (1.0 + jnp.tanh(math.sqrt(2.0 / math.pi) * (h + 0.044715 * (h * h * h))))
    y = _dot(h.astype(BF16), w2_ref[...])
    row = lax.broadcasted_iota(jnp.int32, y.shape, 0)
    o_ref[0, 0] = jnp.where(row < nc - 1, y, 0.0)


def _compress(u, pe2, w1, w2p):
    b, g, nc, _ = u.shape
    return pl.pallas_call(
        _compress_kernel,
        grid=(b, g),
        in_specs=[pl.BlockSpec((1, 1, nc, 1024), lambda i, j: (i, j, 0, 0)),
                  pl.BlockSpec((2, 1024), lambda i, j: (0, 0)),
                  pl.BlockSpec((2, 1024, CMP_HIDDEN), lambda i, j: (0, 0, 0)),
                  pl.BlockSpec((CMP_HIDDEN, LANE), lambda i, j: (0, 0))],
        out_specs=pl.BlockSpec((1, 1, nc, LANE), lambda i, j: (i, j, 0, 0)),
        out_shape=jax.ShapeDtypeStruct((b, g, nc, LANE), F32),
        compiler_params=_cparams(("parallel", "parallel")),
        name="nsa_compress",
    )(u, pe2, w1, w2p)


def _stack_heads(q_ref):
    q = q_ref[...] * (HEAD_DIM ** -0.5)
    return jnp.concatenate([q[:, h * LANE:(h + 1) * LANE] for h in range(NSA_HPG)], axis=0).astype(BF16)


def _store_heads(o_ref, o, gates, branch):
    for h in range(NSA_HPG):
        c = branch * NSA_HPG + h
        o_ref[0, h] = (o[h * Q_BLOCK:(h + 1) * Q_BLOCK, :] * gates[:, c:c + 1])[:, :HEAD_DIM]


def _cmp_kernel(q_ref, zg_ref, kc_ref, vc_ref, pc_ref, ov_ref, o_ref, m_ref):
    c = pl.program_id(2)
    nc = kc_ref.shape[2]
    nsel = ov_ref.shape[1]
    qs = _stack_heads(q_ref)
    s = _dot_nt(qs, kc_ref[0, 0].astype(BF16))
    off = (nc - 8) - 8 * c
    bias = pltpu.roll(pc_ref[0], (2 * nc - off) % (2 * nc), 1)[:, :nc]
    s = s + bias
    m = jnp.max(s, axis=-1, keepdims=True)
    p = jnp.exp(s - m)
    l = jnp.sum(p, axis=-1, keepdims=True)
    ql = lax.broadcasted_iota(jnp.int32, (NSA_HPG * Q_BLOCK, 1), 0) & (Q_BLOCK - 1)
    any_key = (c * Q_BLOCK + ql) >= (CMP_LEN - 1)
    p = p * jnp.where(any_key, 1.0 / l, 0.0)
    gates = _sigmoid(zg_ref[...])
    _store_heads(o_ref, _dot(p.astype(BF16), vc_ref[0, 0].astype(BF16)), gates, 0)

    psum = p[0:Q_BLOCK] + p[Q_BLOCK:2 * Q_BLOCK] + p[2 * Q_BLOCK:3 * Q_BLOCK] + p[3 * Q_BLOCK:4 * Q_BLOCK]
    imp = _dot_split(psum, ov_ref[...])
    jb = lax.broadcasted_iota(jnp.int32, (Q_BLOCK, nsel), 1)
    tq = c * Q_BLOCK + lax.broadcasted_iota(jnp.int32, (Q_BLOCK, nsel), 0)
    jq = jnp.right_shift(tq, int(math.log2(SEL_BLOCK)))
    forced = (jb == 0) | (jb == jq) | (jb == jq - 1)
    val = jnp.where(forced, SEL_FORCE, jnp.where(jb <= jq, imp, -SEL_FORCE))
    sel = jnp.zeros((Q_BLOCK, nsel), F32)
    jbf = jb.astype(F32)
    for _ in range(SEL_TOPN):
        mx = jnp.max(val, axis=-1, keepdims=True)
        first = jnp.min(jnp.where(val == mx, jbf, float(nsel)), axis=-1, keepdims=True)
        hit = jbf == first
        sel = jnp.where(hit, jnp.where(mx >= 0.0, 1.0, 0.0), sel)
        val = jnp.where(hit, -3e38, val)
    m_ref[0, 0] = (sel - 1.0) * (-NEG_INF)


def _cmp_branch(z, k_cmp, v_cmp, pc_tab, overlap, bsz, t):
    nq = t // Q_BLOCK
    nc = k_cmp.shape[2]
    nsel = overlap.shape[1]
    return pl.pallas_call(
        _cmp_kernel,
        grid=(bsz, NSA_GROUPS, nq),
        in_specs=[pl.BlockSpec((Q_BLOCK, NSA_HPG * LANE), lambda b, g, c: (b * nq + c, g)),
                  pl.BlockSpec((Q_BLOCK, LANE), lambda b, g, c: (b * nq + c, C_G3 // LANE + g)),
                  pl.BlockSpec((1, 1, nc, LANE), lambda b, g, c: (b, g, 0, 0)),
                  pl.BlockSpec((1, 1, nc, LANE), lambda b, g, c: (b, g, 0, 0)),
                  pl.BlockSpec((1, NSA_HPG * Q_BLOCK, 2 * nc), lambda b, g, c: (g, 0, 0)),
                  pl.BlockSpec((nc, nsel), lambda b, g, c: (0, 0))],
        out_specs=[pl.BlockSpec((1, NSA_HPG, Q_BLOCK, HEAD_DIM), lambda b, g, c: (b, g, c, 0)),
                   pl.BlockSpec((1, 1, Q_BLOCK, nsel), lambda b, g, c: (b, g, c, 0))],
        out_shape=[jax.ShapeDtypeStruct((bsz, NSA_HEADS, t, HEAD_DIM), F32),
                   jax.ShapeDtypeStruct((bsz, NSA_GROUPS, t, nsel), F32)],
        compiler_params=_cparams(("parallel", "parallel", "arbitrary")),
        name="nsa_cmp_select",
    )(z, z, k_cmp, v_cmp, pc_tab, overlap)


def _win_kernel(q_ref, zg_ref, k_ref, v_ref, wb_ref, o_ref):
    c = pl.program_id(2)
    start = pl.multiple_of(jnp.maximum(c * Q_BLOCK - WINDOW, 0), Q_BLOCK)
    qs = _stack_heads(q_ref)
    kw = k_ref[0, 0, pl.ds(start, WINDOW + Q_BLOCK), :]
    vw = v_ref[0, 0, pl.ds(start, WINDOW + Q_BLOCK), :]
    s = _dot_nt(qs, kw) + wb_ref[0, 0]
    m = jnp.max(s, axis=-1, keepdims=True)
    p = jnp.exp(s - m)
    l = jnp.sum(p, axis=-1, keepdims=True)
    o = _dot(p.astype(BF16), vw) * (1.0 / l)
    _store_heads(o_ref, o, _sigmoid(zg_ref[...]), 2)


def _win_branch(z, kw, vw, wb_tab, bsz, t):
    nq = t // Q_BLOCK
    nvar = wb_tab.shape[1]
    return pl.pallas_call(
        _win_kernel,
        grid=(bsz, NSA_GROUPS, nq),
        in_specs=[pl.BlockSpec((Q_BLOCK, NSA_HPG * LANE), lambda b, g, c: (b * nq + c, g)),
                  pl.BlockSpec((Q_BLOCK, LANE), lambda b, g, c: (b * nq + c, C_G3 // LANE + g)),
                  pl.BlockSpec((1, 1, t, LANE), lambda b, g, c: (b, g, 0, 0)),
                  pl.BlockSpec((1, 1, t, LANE), lambda b, g, c: (b, g, 0, 0)),
                  pl.BlockSpec((1, 1, NSA_HPG * Q_BLOCK, WINDOW + Q_BLOCK),
                               lambda b, g, c: (g, jnp.minimum(c, nvar - 1), 0, 0))],
        out_specs=pl.BlockSpec((1, NSA_HPG, Q_BLOCK, HEAD_DIM), lambda b, g, c: (b, g, c, 0)),
        out_shape=jax.ShapeDtypeStruct((bsz, NSA_HEADS, t, HEAD_DIM), F32),
        compiler_params=_cparams(("parallel", "parallel", "arbitrary")),
        name="nsa_window",
    )(z, z, kw, vw, wb_tab)


SEL_TK = 512


def _sel_kernel(q_ref, zg_ref, k_ref, v_ref, mt_ref, lt_ref, e0_ref, o_ref, m_s, l_s, acc_s):
    c = pl.program_id(2)
    qs = _stack_heads(q_ref)
    mt = mt_ref[0, 0]
    nsel = mt.shape[1]
    m_s[...] = jnp.full(m_s.shape, NEG_INF, F32)
    l_s[...] = jnp.zeros(l_s.shape, F32)
    acc_s[...] = jnp.zeros(acc_s.shape, F32)
    blocks_per_tile = SEL_TK // SEL_BLOCK

    def body(i, carry):
        k0 = pl.multiple_of(i * SEL_TK, SEL_TK)
        kt = k_ref[0, 0, pl.ds(k0, SEL_TK), :]
        vt = v_ref[0, 0, pl.ds(k0, SEL_TK), :]
        s = _dot_nt(qs, kt)
        mrow = pltpu.roll(mt, (nsel - blocks_per_tile * i) % nsel, 1)
        mb = _dot(mrow.astype(BF16), e0_ref[...])
        delta = c * Q_BLOCK - i * SEL_TK
        off = pl.multiple_of(jnp.maximum(FAR_BIAS_DIST + SEL_TK - delta, 0), LANE)
        s = s + lt_ref[0, :, pl.ds(off, SEL_TK)]
        s = (s.reshape(NSA_HPG, Q_BLOCK, SEL_TK) + mb[None]).reshape(NSA_HPG * Q_BLOCK, SEL_TK)
        m_old = m_s[...]
        m_new = jnp.maximum(m_old, jnp.max(s, axis=-1, keepdims=True))
        alpha = jnp.exp(m_old - m_new)
        p = jnp.exp(s - m_new)
        l_s[...] = alpha * l_s[...] + jnp.sum(p, axis=-1, keepdims=True)
        acc_s[...] = alpha * acc_s[...] + _dot(p.astype(BF16), vt)
        m_s[...] = m_new
        return carry

    lax.fori_loop(0, lax.div(c, SEL_TK // Q_BLOCK) + 1, body, 0)
    o = acc_s[...] * (1.0 / l_s[...])
    _store_heads(o_ref, o, _sigmoid(zg_ref[...]), 1)


def _sel_branch(z, ks, vs, mt, lt_tab, e0, bsz, t):
    nq = t // Q_BLOCK
    nsel = mt.shape[-1]
    return pl.pallas_call(
        _sel_kernel,
        grid=(bsz, NSA_GROUPS, nq),
        in_specs=[pl.BlockSpec((Q_BLOCK, NSA_HPG * LANE), lambda b, g, c: (b * nq + c, g)),
                  pl.BlockSpec((Q_BLOCK, LANE), lambda b, g, c: (b * nq + c, C_G3 // LANE + g)),
                  pl.BlockSpec((1, 1, t, LANE), lambda b, g, c: (b, g, 0, 0)),
                  pl.BlockSpec((1, 1, t, LANE), lambda b, g, c: (b, g, 0, 0)),
                  pl.BlockSpec((1, 1, Q_BLOCK, nsel), lambda b, g, c: (b, g, c, 0)),
                  pl.BlockSpec((1, NSA_HPG * Q_BLOCK, lt_tab.shape[2]), lambda b, g, c: (g, 0, 0)),
                  pl.BlockSpec((nsel, SEL_TK), lambda b, g, c: (0, 0))],
        out_specs=pl.BlockSpec((1, NSA_HPG, Q_BLOCK, HEAD_DIM), lambda b, g, c: (b, g, c, 0)),
        out_shape=jax.ShapeDtypeStruct((bsz, NSA_HEADS, t, HEAD_DIM), F32),
        scratch_shapes=[pltpu.VMEM((NSA_HPG * Q_BLOCK, 1), F32),
                        pltpu.VMEM((NSA_HPG * Q_BLOCK, 1), F32),
                        pltpu.VMEM((NSA_HPG * Q_BLOCK, LANE), F32)],
        compiler_params=_cparams(("parallel", "parallel", "arbitrary")),
        name="nsa_selected",
    )(z, z, ks, vs, mt, lt_tab, e0)


def _shift_mix(z_ref, prev_ref, mu, first):
    z = z_ref[...]
    prev = jnp.where(first, 0.0, prev_ref[7:8, :])
    row = lax.broadcasted_iota(jnp.int32, z.shape, 0)
    zp = jnp.where(row == 0, prev, pltpu.roll(z, 1, 0))
    return z + mu * (zp - z)


def _rwkv_prep_kernel(has_vres, tiles_per_seq, *refs):
    (zr, zk, zv, zwa, zg, pr, pk, pv, pwa, pg, mu_ref, vec_ref, w2_ref, a2_ref, g2_ref, hsum_ref) = refs[:16]
    pos = 16
    if has_vres:
        vf_ref, v0_ref, v1_ref, v2_ref = refs[pos:pos + 4]
        pos += 4
    r_o, w_o, k_o, v_o, kk_o, b_o, g_o, bonus_o, vt_o = refs[pos:]
    first = lax.rem(pl.program_id(0), tiles_per_seq) == 0
    r = _shift_mix(zr, pr, mu_ref[0:1, :], first)
    k = _shift_mix(zk, pk, mu_ref[1:2, :], first)
    v = _shift_mix(zv, pv, mu_ref[2:3, :], first)
    wa = _shift_mix(zwa, pwa, mu_ref[3:4, 0:LANE], first)
    zg_s = _shift_mix(zg, pg, mu_ref[4:5, 0:LANE], first)
    w0, a0, k_k, k_a, r_k = (vec_ref[i:i + 1, :] for i in range(5))
    if has_vres:
        lora = _dot(_dot(v.astype(BF16), v1_ref[...]).astype(BF16), v2_ref[...])
        v = v + (vf_ref[...] - v) * _sigmoid(v0_ref[...] + lora)
    u = w0 + _dot(jnp.tanh(wa).astype(BF16), w2_ref[...])
    decay = jnp.exp(-math.exp(-0.5) * _sigmoid(u))
    a = _sigmoid(a0 + _dot(wa.astype(BF16), a2_ref[...]))
    g = _dot(_sigmoid(zg_s).astype(BF16), g2_ref[...])
    kk = k * k_k
    kk = kk / jnp.maximum(jnp.sqrt(_dot_split(kk * kk, hsum_ref[...])), 1e-12)
    k = k * (1.0 + (a - 1.0) * k_a)
    r_o[...] = r
    w_o[...] = decay
    k_o[...] = k
    v_o[...] = v
    kk_o[...] = kk
    b_o[...] = kk * a
    g_o[...] = g
    bonus_o[...] = _dot_split(r * k * r_k, hsum_ref[...]) * v
    vt_o[0] = v.T


def _rwkv_prep(z, mu5, vecs, w2p, a2p, g2, hsum, vres, bsz, t, tm=256):
    n = bsz * t
    nt = n // tm
    tps = t // tm
    w = RWKV_WIDTH

    def cur(width, col):
        return pl.BlockSpec((tm, width), lambda i: (i, col // width))

    def prev(width, col):
        return pl.BlockSpec((8, width), lambda i: (jnp.maximum(i * (tm // 8) - 1, 0), col // width))

    def full(shape):
        return pl.BlockSpec(shape, lambda i: (0,) * len(shape))

    cols = [(w, C_R), (w, C_K), (w, C_V), (LANE, C_WA), (LANE, C_ZG)]
    in_specs = [cur(*c) for c in cols] + [prev(*c) for c in cols]
    in_specs += [full(mu5.shape), full(vecs.shape), full(w2p.shape), full(a2p.shape), full(g2.shape), full(hsum.shape)]
    args = [z] * 10 + [mu5, vecs, w2p, a2p, g2, hsum]
    if vres is not None:
        v_first, v0, v1p, v2p = vres
        in_specs += [pl.BlockSpec((tm, w), lambda i: (i, 0)), full(v0.shape), full(v1p.shape), full(v2p.shape)]
        args += [v_first, v0, v1p, v2p]
    tok = pl.BlockSpec((tm, w), lambda i: (i, 0))
    out_specs = [tok] * 8 + [pl.BlockSpec((1, w, tm), lambda i: (i // tps, 0, i % tps))]
    out_shape = [jax.ShapeDtypeStruct((n, w), F32)] * 8 + [jax.ShapeDtypeStruct((bsz, w, t), F32)]
    return pl.pallas_call(
        functools.partial(_rwkv_prep_kernel, vres is not None, tps),
        grid=(nt,),
        in_specs=in_specs,
        out_specs=out_specs,
        out_shape=out_shape,
        compiler_params=_cparams(("parallel",)),
        name="rwkv_prep",
    )(*args)


N_PAIRS = RWKV_HEADS // 2


def _scan_kernel(r_ref, w_ref, k_ref, kk_ref, b_ref, vt_ref, ones_ref, o_ref, s_ref, kbd_ref):
    bsz = r_ref.shape[0]
    tt = r_ref.shape[1]

    @pl.when(pl.program_id(0) == 0)
    def _():
        s_ref[...] = jnp.zeros(s_ref.shape, F32)

    lane = lax.broadcasted_iota(jnp.int32, (tt, LANE), 1)
    for bi in range(bsz):
        for p in range(N_PAIRS):
            kp = k_ref[bi, :, p * LANE:(p + 1) * LANE]
            kbd_ref[bi * N_PAIRS + p] = jnp.concatenate(
                [jnp.where(lane < RWKV_HEAD, kp, 0.0), jnp.where(lane >= RWKV_HEAD, kp, 0.0)], axis=0).astype(BF16)

    step_lane = lax.broadcasted_iota(jnp.int32, (RWKV_HEAD, 2 * tt), 1) & (tt - 1)
    row8 = lax.broadcasted_iota(jnp.int32, (16, LANE), 0)
    lane8 = lax.broadcasted_iota(jnp.int32, (16, LANE), 1)
    r_mask = ((row8 == 0) & (lane8 < RWKV_HEAD)) | ((row8 == 1) & (lane8 >= RWKV_HEAD))
    ones_bd = ones_ref[...]

    def group(tg, carry):
        t0 = pl.multiple_of(tg * 8, 8)
        for bi in range(bsz):
            for p in range(N_PAIRS):
                ln = slice(p * LANE, (p + 1) * LANE)
                w8 = w_ref[bi, pl.ds(t0, 8), ln]
                kk8 = kk_ref[bi, pl.ds(t0, 8), ln]
                b8 = b_ref[bi, pl.ds(t0, 8), ln]
                r8 = r_ref[bi, pl.ds(t0, 8), ln]
                vt_pair = jnp.concatenate([vt_ref[bi, p * LANE:p * LANE + RWKV_HEAD, :],
                                           vt_ref[bi, p * LANE + RWKV_HEAD:(p + 1) * LANE, :]], axis=1)
                c = bi * N_PAIRS + p
                s = s_ref[c]
                for u in range(8):
                    sk = _dot((s * kk8[u:u + 1, :]).astype(BF16), ones_bd)
                    v_sel = jnp.where(step_lane == t0 + u, vt_pair, 0.0).astype(BF16)
                    vk = _dot(v_sel, kbd_ref[c])
                    s = s * w8[u:u + 1, :] - sk * b8[u:u + 1, :] + vk
                    r_lhs = jnp.where(r_mask, r8[u:u + 1, :], 0.0).astype(BF16)
                    o = _dot_nt(r_lhs, s.astype(BF16))
                    o_ref[t0 + u, bi, 2 * p:2 * p + 2, :] = o[0:2, :]
                s_ref[c] = s
        return carry

    lax.fori_loop(0, tt // 8, group, 0)


def _rwkv_scan(r, w, k, kk, b, vt, ones_bd, bsz, t):
    tt = SCAN_TT
    wd = RWKV_WIDTH
    tok = pl.BlockSpec((bsz, tt, wd), lambda i: (0, i, 0))
    return pl.pallas_call(
        _scan_kernel,
        grid=(t // tt,),
        in_specs=[tok, tok, tok, tok, tok,
                  pl.BlockSpec((bsz, wd, tt), lambda i: (0, 0, i)),
                  pl.BlockSpec((LANE, LANE), lambda i: (0, 0))],
        out_specs=pl.BlockSpec((tt, bsz, RWKV_HEADS, RWKV_HEAD), lambda i: (i, 0, 0, 0)),
        out_shape=jax.ShapeDtypeStruct((t, bsz, RWKV_HEADS, RWKV_HEAD), F32),
        scratch_shapes=[pltpu.VMEM((bsz * N_PAIRS, RWKV_HEAD, LANE), F32),
                        pltpu.VMEM((bsz * N_PAIRS, 2 * tt, LANE), BF16)],
        compiler_params=_cparams(("arbitrary",)),
        name="rwkv_scan",
    )(r, w, k, kk, b, vt, ones_bd)


def _merge_kernel(x_ref, oc_ref, os_ref, ow_ref, orw_ref, bonus_ref, g_ref, zgn_ref, zgr_ref, gn_ref, hsum_ref,
                  pn_ref, pr_ref, wo_ref, ln_ref, o_ref):
    y_n = jnp.zeros(o_ref.shape, F32)
    for h in range(NSA_HEADS):
        o_h = (oc_ref[0, h] + os_ref[0, h] + ow_ref[0, h]).astype(BF16)
        y_n = y_n + _dot(o_h, pn_ref[h * HEAD_DIM:(h + 1) * HEAD_DIM, :])
    o = orw_ref[...]
    mu = _dot_split(o, hsum_ref[...]) * (1.0 / RWKV_HEAD)
    d = o - mu
    var = _dot_split(d * d, hsum_ref[...]) * (1.0 / RWKV_HEAD)
    o = d * lax.rsqrt(var + GN_EPS) * gn_ref[0:1, :] + gn_ref[1:2, :]
    o = ((o + bonus_ref[...]) * g_ref[...]).astype(BF16)
    y1 = _sigmoid(zgn_ref[...]) * y_n + _sigmoid(zgr_ref[...]) * _dot(o, pr_ref[...])
    y = _dot(y1.astype(BF16), wo_ref[...])
    o_ref[...] = _layer_norm(DN_ALPHA * x_ref[...] + y, ln_ref[0:1, :], ln_ref[1:2, :])


def _merge(x, o_c, o_s, o_w, o_rw, bonus, g, z, gn, hsum, p_nsa, p_rwkv, w_out, ln, bsz, t, tm=256):
    n = bsz * t
    tps = t // tm
    d = D_MODEL
    w = RWKV_WIDTH
    head = pl.BlockSpec((1, NSA_HEADS, tm, HEAD_DIM), lambda i: (i // tps, 0, i % tps, 0))
    tok = pl.BlockSpec((tm, w), lambda i: (i, 0))

    def full(a):
        return pl.BlockSpec(a.shape, lambda i: (0,) * a.ndim)

    return pl.pallas_call(
        _merge_kernel,
        grid=(n // tm,),
        in_specs=[pl.BlockSpec((tm, d), lambda i: (i, 0)), head, head, head, tok, tok, tok,
                  pl.BlockSpec((tm, d), lambda i: (i, C_GN // d)),
                  pl.BlockSpec((tm, d), lambda i: (i, C_GR // d)),
                  full(gn), full(hsum), full(p_nsa), full(p_rwkv), full(w_out), full(ln)],
        out_specs=pl.BlockSpec((tm, d), lambda i: (i, 0)),
        out_shape=jax.ShapeDtypeStruct((n, d), F32),
        compiler_params=_cparams(("parallel",)),
        name="mix_merge_ln",
    )(x, o_c, o_s, o_w, o_rw, bonus, g, z, z, gn, hsum, p_nsa, p_rwkv, w_out, ln)


def _xattn_kernel(x_ref, kt_ref, v_ref, wq_ref, wo_ref, ln_ref, rw_ref, o_ref, s_ref):
    x = x_ref[...]
    q = _dot(x.astype(BF16), wq_ref[...]).astype(BF16)
    outs = []
    for h in range(XATTN_HEADS):
        hs = slice(h * XATTN_HEAD, (h + 1) * XATTN_HEAD)
        s = _dot(q[:, hs], kt_ref[0, hs, :]) * (XATTN_HEAD ** -0.5)
        m = jnp.max(s, axis=-1, keepdims=True)
        p = jnp.exp(s - m)
        l = jnp.sum(p, axis=-1, keepdims=True)
        outs.append(_dot(p.astype(BF16), v_ref[0, :, hs]) * (1.0 / l))
    o = jnp.concatenate(outs, axis=1).astype(BF16)
    x2 = _layer_norm(DN_ALPHA * x + _dot(o, wo_ref[...]), ln_ref[0:1, :], ln_ref[1:2, :])
    o_ref[...] = x2
    hi = x2.astype(BF16)
    lo = (x2 - hi.astype(F32)).astype(BF16)
    logits = _dot(hi, rw_ref[0]) + (_dot(lo, rw_ref[0]) + _dot(hi, rw_ref[1]))
    s_ref[...] = _sigmoid(logits)


def _xattn(x, kt, v, wq, wo, ln, rw2, bsz, t, tm=256):
    n = bsz * t
    tps = t // tm
    d = D_MODEL

    def full(a):
        return pl.BlockSpec(a.shape, lambda i: (0,) * a.ndim)

    return pl.pallas_call(
        _xattn_kernel,
        grid=(n // tm,),
        in_specs=[pl.BlockSpec((tm, d), lambda i: (i, 0)),
                  pl.BlockSpec((1,) + kt.shape[1:], lambda i: (i // tps, 0, 0)),
                  pl.BlockSpec((1,) + v.shape[1:], lambda i: (i // tps, 0, 0)),
                  full(wq), full(wo), full(ln), full(rw2)],
        out_specs=[pl.BlockSpec((tm, d), lambda i: (i, 0)), pl.BlockSpec((tm, LANE), lambda i: (i, 0))],
        out_shape=[jax.ShapeDtypeStruct((n, d), F32), jax.ShapeDtypeStruct((n, LANE), F32)],
        compiler_params=_cparams(("parallel",)),
        name="xattn_ln_router",
    )(x, kt, v, wq, wo, ln, rw2)


def _expert_kernel(be_ref, x_ref, wg_ref, wu_ref, wd_ref, o_ref):
    x = x_ref[...]
    hg = _dot(x, wg_ref[0])
    h = hg * _sigmoid(hg) * _dot(x, wu_ref[0])
    o_ref[...] = _dot(h.astype(BF16), wd_ref[0])


def _experts(block_expert, rows, w_gate, w_up, w_down):
    n_rows, d = rows.shape
    ff = w_gate.shape[2]
    return pl.pallas_call(
        _expert_kernel,
        grid_spec=pltpu.PrefetchScalarGridSpec(
            num_scalar_prefetch=1,
            grid=(n_rows // MOE_BLK,),
            in_specs=[pl.BlockSpec((MOE_BLK, d), lambda i, be: (i, 0)),
                      pl.BlockSpec((1, d, ff), lambda i, be: (be[i], 0, 0)),
                      pl.BlockSpec((1, d, ff), lambda i, be: (be[i], 0, 0)),
                      pl.BlockSpec((1, ff, d), lambda i, be: (be[i], 0, 0))],
            out_specs=pl.BlockSpec((MOE_BLK, d), lambda i, be: (i, 0))),
        out_shape=jax.ShapeDtypeStruct((n_rows, d), F32),
        compiler_params=_cparams(("arbitrary",)),
        name="moe_experts",
    )(block_expert, rows, w_gate, w_up, w_down)


def _combine_kernel(x_ref, ya_ref, yb_ref, gate_ref, ln_ref, o_ref):
    y = gate_ref[:, 0:1] * ya_ref[...] + gate_ref[:, 1:2] * yb_ref[...]
    o_ref[...] = _layer_norm(DN_ALPHA * x_ref[...] + y, ln_ref[0:1, :], ln_ref[1:2, :])


def _combine(x, ya, yb, gate, ln, tm=512):
    n, d = x.shape
    tok = pl.BlockSpec((tm, d), lambda i: (i, 0))
    return pl.pallas_call(
        _combine_kernel,
        grid=(n // tm,),
        in_specs=[tok, tok, tok, pl.BlockSpec((tm, TOP_K), lambda i: (i, 0)),
                  pl.BlockSpec(ln.shape, lambda i: (0, 0))],
        out_specs=tok,
        out_shape=jax.ShapeDtypeStruct((n, d), F32),
        compiler_params=_cparams(("parallel",)),
        name="moe_combine_ln",
    )(x, ya, yb, gate, ln)


def _t5_bucket(dist):
    n = jnp.maximum(dist, 0)
    max_exact = REL_BUCKETS // 2
    log_ratio = jnp.log(jnp.maximum(n, 1).astype(F32) / max_exact) / math.log(REL_MAX_DIST / max_exact)
    large = jnp.minimum(max_exact + (log_ratio * (REL_BUCKETS - max_exact)).astype(jnp.int32), REL_BUCKETS - 1)
    return jnp.where(n < max_exact, n, large)


def _bias_tables(rel_table, t):
    nc = t // CMP_STRIDE
    f = rel_table[_t5_bucket(jnp.arange(t + 2048))].astype(F32).T
    ql = jnp.arange(Q_BLOCK)[:, None]

    def lookup(dist, ok):
        vals = jnp.take(f, jnp.clip(dist, 0, f.shape[-1] - 1), axis=1)
        vals = jnp.where(ok[None], vals, NEG_INF)
        return vals.reshape(NSA_GROUPS, NSA_HPG * dist.shape[0], dist.shape[1])

    j = jnp.arange(2 * nc)[None, :]
    dist_c = CMP_STRIDE * (nc - 8 - j) + ql - (CMP_LEN - 1)
    pc = lookup(dist_c, dist_c >= 0)
    nvar = WINDOW // Q_BLOCK + 1
    kl = jnp.arange(WINDOW + Q_BLOCK)[None, :]
    wb = jnp.stack([lookup(Q_BLOCK * v + ql - kl, (Q_BLOCK * v + ql - kl >= 0) & (Q_BLOCK * v + ql - kl < WINDOW))
                    for v in range(nvar)], axis=1)
    j = jnp.arange(FAR_BIAS_DIST + 2 * SEL_TK)[None, :]
    dist_s = FAR_BIAS_DIST + SEL_TK + ql - j
    lt = lookup(dist_s, dist_s >= 0)
    return pc, wb, lt


def _in_proj_perm():
    perm = np.full((IN_PAD,), -1, np.int64)
    for h in range(NSA_HEADS):
        perm[C_Q + h * LANE:C_Q + h * LANE + HEAD_DIM] = np.arange(HEAD_DIM) + h * HEAD_DIM
    rw0 = NSA_COLS
    g0 = NSA_COLS + RWKV_COLS
    perm[C_GN:C_GN + D_MODEL] = g0 + np.arange(D_MODEL)
    perm[C_GR:C_GR + D_MODEL] = g0 + D_MODEL + np.arange(D_MODEL)
    perm[C_R:C_R + RWKV_WIDTH] = rw0 + np.arange(RWKV_WIDTH)
    perm[C_K:C_K + RWKV_WIDTH] = rw0 + RWKV_WIDTH + np.arange(RWKV_WIDTH)
    perm[C_V:C_V + RWKV_WIDTH] = rw0 + 2 * RWKV_WIDTH + np.arange(RWKV_WIDTH)
    perm[C_KV:C_KV + 6 * NSA_KV] = NSA_WIDTH + np.arange(6 * NSA_KV)
    for g in range(NSA_GROUPS):
        for br in range(3):
            for h in range(NSA_HPG):
                perm[C_G3 + g * LANE + br * NSA_HPG + h] = NSA_WIDTH + 6 * NSA_KV + (g * NSA_HPG + h) * 3 + br
    perm[C_WA:C_WA + LORA_W + LORA_A] = rw0 + 3 * RWKV_WIDTH + np.arange(LORA_W + LORA_A)
    perm[C_ZG:C_ZG + LORA_G] = rw0 + 3 * RWKV_WIDTH + LORA_W + LORA_A + np.arange(LORA_G)
    return perm


_PERM = _in_proj_perm()


def _pad_to(a, shape):
    return jnp.pad(a, [(0, s - d) for d, s in zip(a.shape, shape)])


def _route(s, router_bias, n_tok):
    s16 = s[:, :N_EXPERTS]
    s_sel = s16 + router_bias.astype(F32)
    group_score = jnp.sum(lax.top_k(s_sel.reshape(n_tok, N_EXPERT_GROUPS, EXPERTS_PER_GROUP), 2)[0], axis=-1)
    group = jnp.argmax(group_score, axis=-1)
    in_group = (jnp.arange(N_EXPERTS) // EXPERTS_PER_GROUP)[None, :] == group[:, None]
    _, e_idx = lax.top_k(jnp.where(in_group, s_sel, NEG_INF), TOP_K)
    s_top = jnp.take_along_axis(s16, e_idx, axis=-1)
    gate = s_top / jnp.sum(s_top, axis=-1, keepdims=True)
    e_flat = e_idx.reshape(-1)
    onehot = (e_flat[:, None] == jnp.arange(N_EXPERTS)[None, :]).astype(jnp.int32)
    rank = jnp.take_along_axis(jnp.cumsum(onehot, axis=0) - onehot, e_flat[:, None], axis=1)[:, 0]
    counts = jnp.sum(onehot, axis=0)
    padded = (counts + MOE_BLK - 1) // MOE_BLK * MOE_BLK
    ends = jnp.cumsum(padded)
    dest = (ends - padded)[e_flat] + rank
    n_rows = n_tok * TOP_K + N_EXPERTS * MOE_BLK
    tok_of_row = jnp.zeros((n_rows,), jnp.int32).at[dest].set(jnp.arange(n_tok * TOP_K, dtype=jnp.int32) // TOP_K)
    block_expert = jnp.minimum(
        jnp.searchsorted(ends, jnp.arange(n_rows // MOE_BLK) * MOE_BLK, side='right'), N_EXPERTS - 1).astype(jnp.int32)
    return gate, dest.reshape(n_tok, TOP_K), tok_of_row, block_expert


def kernel(x, mem, rel_table, router_w, router_bias, w_in, cmp_pe_k, cmp_pe_v, cmp_w1k, cmp_w2k, cmp_w1v, cmp_w2v, rwkv_mu, rwkv_w0, rwkv_w2, rwkv_a0, rwkv_a2, rwkv_g2, rwkv_kk, rwkv_ka, rwkv_rk, rwkv_gn_g, rwkv_gn_b, rwkv_v0, rwkv_v1, rwkv_v2, p_nsa, p_rwkv, w_out, ln1_g, ln1_b, xq_w, xk_w, xv_w, xo_w, ln2_g, ln2_b, moe_w_gate, moe_w_up, moe_w_down, ln3_g, ln3_b):
    bsz, t, d = x.shape
    n = bsz * t
    depth = w_in.shape[0]
    nc = t // CMP_STRIDE
    nsel = t // SEL_BLOCK
    g, hd = NSA_GROUPS, HEAD_DIM

    pc_tab, wb_tab, lt_tab = _bias_tables(rel_table, t)
    cmp_start = np.arange(nc) * CMP_STRIDE
    sel_start = np.arange(nsel) * SEL_BLOCK
    overlap = ((cmp_start[:, None] <= sel_start[None, :] + SEL_BLOCK - 1)
               & (cmp_start[:, None] + CMP_LEN - 1 >= sel_start[None, :]) & (cmp_start[:, None] < (nc - 1) * CMP_STRIDE))
    overlap = jnp.asarray(overlap, BF16)
    e0 = jnp.asarray(np.arange(nsel)[:, None] == (np.arange(SEL_TK)[None, :] // SEL_BLOCK), BF16)
    lane_head = np.arange(RWKV_WIDTH) // RWKV_HEAD
    hsum = jnp.asarray(lane_head[:, None] == lane_head[None, :], BF16)
    ones_bd = jnp.asarray((np.arange(LANE) // RWKV_HEAD)[:, None] == (np.arange(LANE) // RWKV_HEAD)[None, :], BF16)
    rw_f = _pad_to(router_w.astype(F32), (d, LANE))
    rw_hi = rw_f.astype(BF16)
    rw2 = jnp.stack([rw_hi, (rw_f - rw_hi.astype(F32)).astype(BF16)])
    perm = jnp.asarray(np.maximum(_PERM, 0))
    perm_ok = jnp.asarray(_PERM >= 0)

    xf = x.reshape(n, d)
    mem_f = mem.reshape(bsz * mem.shape[1], d)
    v_first = None
    for l in range(depth):
        w_in_l = jnp.where(perm_ok[None, :], jnp.take(w_in[l], perm, axis=1), 0.0).astype(BF16)
        z = _matmul(xf, w_in_l, F32, 512, 1024)

        kv = z[:, C_KV:C_KV + 6 * NSA_KV].reshape(bsz, t, 6, g, hd)
        kc, vc, ks, vs, kw, vw = (kv[:, :, i] for i in range(6))

        def chunks(a):
            return a.reshape(bsz, nc, CMP_STRIDE, g, hd).transpose(0, 3, 1, 2, 4).reshape(bsz, g, nc, CMP_STRIDE * hd)

        def keys(a):
            return _pad_to(a.transpose(0, 2, 1, 3), (bsz, g, t, LANE)).astype(BF16)

        def cmp_w(pe, w1, w2):
            return (pe.reshape(2, CMP_STRIDE * hd), w1.reshape(2, CMP_STRIDE * hd, CMP_HIDDEN).astype(BF16),
                    _pad_to(w2, (CMP_HIDDEN, LANE)).astype(BF16))

        k_cmp = _compress(chunks(kc), *cmp_w(cmp_pe_k[l], cmp_w1k[l], cmp_w2k[l]))
        v_cmp = _compress(chunks(vc), *cmp_w(cmp_pe_v[l], cmp_w1v[l], cmp_w2v[l]))
        o_c, sel_mask = _cmp_branch(z, k_cmp, v_cmp, pc_tab, overlap, bsz, t)
        o_s = _sel_branch(z, keys(ks), keys(vs), sel_mask, lt_tab, e0, bsz, t)
        o_w = _win_branch(z, keys(kw), keys(vw), wb_tab, bsz, t)

        mu = rwkv_mu[l]
        w3 = 3 * RWKV_WIDTH
        mu5 = jnp.stack([mu[0:512], mu[512:1024], mu[1024:1536], _pad_to(mu[w3:w3 + 128], (512,)),
                         _pad_to(mu[w3 + 128:w3 + 256], (512,))])
        vecs = jnp.stack([rwkv_w0[l], rwkv_a0[l], rwkv_kk[l], rwkv_ka[l], rwkv_rk[l].reshape(-1)])
        w2p = jnp.concatenate([rwkv_w2[l], jnp.zeros_like(rwkv_a2[l])], axis=0).astype(BF16)
        a2p = jnp.concatenate([jnp.zeros_like(rwkv_w2[l]), rwkv_a2[l]], axis=0).astype(BF16)
        vres = None
        if l > 0:
            vres = (v_first, rwkv_v0[l - 1][None, :], _pad_to(rwkv_v1[l - 1], (RWKV_WIDTH, LANE)).astype(BF16),
                    _pad_to(rwkv_v2[l - 1], (LANE, RWKV_WIDTH)).astype(BF16))
        r, w, k, v, kk, b, g_out, bonus, vt = _rwkv_prep(z, mu5, vecs, w2p, a2p, rwkv_g2[l].astype(BF16), hsum, vres,
                                                          bsz, t)
        if l == 0:
            v_first = v
        seq = lambda a: a.reshape(bsz, t, RWKV_WIDTH)
        o_rw = _rwkv_scan(seq(r), seq(w), seq(k), seq(kk), seq(b), vt, ones_bd, bsz, t)
        o_rw = o_rw.transpose(1, 0, 2, 3).reshape(n, RWKV_WIDTH)

        xf = _merge(xf, o_c, o_s, o_w, o_rw, bonus, g_out, z, jnp.stack([rwkv_gn_g[l], rwkv_gn_b[l]]), hsum,
                    p_nsa[l].astype(BF16), p_rwkv[l].astype(BF16), w_out[l].astype(BF16),
                    jnp.stack([ln1_g[l], ln1_b[l]]), bsz, t)

        mlen = mem.shape[1]
        mk = _matmul(mem_f, xk_w[l].astype(BF16), BF16, mlen, XATTN_WIDTH)
        mv = _matmul(mem_f, xv_w[l].astype(BF16), BF16, mlen, XATTN_WIDTH)
        kt = mk.reshape(bsz, mlen, XATTN_WIDTH).transpose(0, 2, 1)
        xf, scores = _xattn(xf, kt, mv.reshape(bsz, mlen, XATTN_WIDTH), xq_w[l].astype(BF16), xo_w[l].astype(BF16),
                            jnp.stack([ln2_g[l], ln2_b[l]]), rw2, bsz, t)

        gate, dest, tok_of_row, block_expert = _route(scores, router_bias, n)
        rows = jnp.take(xf.astype(BF16), tok_of_row, axis=0)
        y_rows = _experts(block_expert, rows, moe_w_gate[l].astype(BF16), moe_w_up[l].astype(BF16),
                          moe_w_down[l].astype(BF16))
        xf = _combine(xf, jnp.take(y_rows, dest[:, 0], axis=0), jnp.take(y_rows, dest[:, 1], axis=0), gate,
                      jnp.stack([ln3_g[l], ln3_b[l]]))
    return xf.reshape(bsz, t, d)
```

```python
import functools
import math

import numpy as np
import jax
import jax.numpy as jnp
from jax import lax
from jax.experimental import pallas as pl
from jax.experimental.pallas import tpu as pltpu

F32 = jnp.float32
BF16 = jnp.bfloat16

D_MODEL = 1024
DEPTH = 2
NSA_HEADS = 8
NSA_GROUPS = 2
NSA_HPG = NSA_HEADS // NSA_GROUPS
HEAD_DIM = 64
NSA_WIDTH = NSA_HEADS * HEAD_DIM
NSA_KV = NSA_GROUPS * HEAD_DIM
CMP_STRIDE = 16
CMP_LEN = 2 * CMP_STRIDE
CMP_HIDDEN = 256
SEL_BLOCK = 64
SEL_TOPN = 16
WINDOW = 512
Q_BLOCK = 128
SEL_FORCE = 1e9
RWKV_HEADS = 8
RWKV_HEAD = 64
RWKV_WIDTH = RWKV_HEADS * RWKV_HEAD
LORA_W = 64
LORA_A = 64
LORA_V = 32
LORA_G = 128
GN_EPS = 64e-5
REL_BUCKETS = 32
REL_MAX_DIST = 1024
XATTN_HEADS = 4
XATTN_HEAD = 128
XATTN_WIDTH = XATTN_HEADS * XATTN_HEAD
N_EXPERTS = 16
N_EXPERT_GROUPS = 4
EXPERTS_PER_GROUP = N_EXPERTS // N_EXPERT_GROUPS
TOP_K = 2
EXPERT_FF = 512
DN_ALPHA = (2 * DEPTH) ** 0.25
LN_EPS = 1e-5
NEG_INF = -1e30
NSA_COLS = NSA_WIDTH + 6 * NSA_KV + 3 * NSA_HEADS
RWKV_COLS = 3 * RWKV_WIDTH + LORA_W + LORA_A + LORA_G

LANE = 128
VMEM_LIMIT = 56 * 1024 * 1024

C_Q = 0
C_GN = 1024
C_GR = 2048
C_R = 3072
C_K = 3584
C_V = 4096
C_KV = 4608
C_G3 = 5376
C_WA = 5632
C_ZG = 5760
IN_PAD = 6144

MOE_BLK = 256
SCAN_TT = 128
FAR_BIAS_DIST = 1280


def _cparams(sem):
    return pltpu.CompilerParams(dimension_semantics=sem, vmem_limit_bytes=VMEM_LIMIT)


def _sigmoid(x):
    return 1.0 / (1.0 + jnp.exp(-x))


def _layer_norm(v, g, b):
    mu = jnp.mean(v, axis=-1, keepdims=True)
    d = v - mu
    var = jnp.mean(d * d, axis=-1, keepdims=True)
    return d * lax.rsqrt(var + LN_EPS) * g + b


def _dot(a, b):
    return jnp.dot(a, b, preferred_element_type=F32)


def _dot_nt(a, b):
    return lax.dot_general(a, b, (((1,), (1,)), ((), ())), preferred_element_type=F32)


def _dot_split(x, w):
    hi = x.astype(BF16)
    lo = (x - hi.astype(F32)).astype(BF16)
    return _dot(hi, w) + _dot(lo, w)


def _mm_kernel(x_ref, w_ref, o_ref, xb_ref):
    @pl.when(pl.program_id(1) == 0)
    def _():
        xb_ref[...] = x_ref[...].astype(BF16)

    o_ref[...] = _dot(xb_ref[...], w_ref[...]).astype(o_ref.dtype)


def _matmul(x, w, out_dtype, tm, tn):
    n, k = x.shape
    m = w.shape[1]
    return pl.pallas_call(
        _mm_kernel,
        grid=(n // tm, m // tn),
        in_specs=[pl.BlockSpec((tm, k), lambda i, j: (i, 0)),
                  pl.BlockSpec((k, tn), lambda i, j: (0, j))],
        out_specs=pl.BlockSpec((tm, tn), lambda i, j: (i, j)),
        out_shape=jax.ShapeDtypeStruct((n, m), out_dtype),
        scratch_shapes=[pltpu.VMEM((tm, k), BF16)],
        compiler_params=_cparams(("parallel", "arbitrary")),
        name="matmul",
    )(x, w)


def _compress_kernel(u_ref, pe_ref, w1_ref, w2_ref, o_ref):
    u = u_ref[0, 0]
    a = _dot((u + pe_ref[0:1, :]).astype(BF16), w1_ref[0])
    b = _dot((u + pe_ref[1:2, :]).astype(BF16), w1_ref[1])
    nc = u.shape[0]
    h = a + pltpu.roll(b, nc - 1, 0)
    h = 0.5 * h * (1.0 + jnp.tanh(math.sqrt(2.0 / math.pi) * (h + 0.044715 * (h * h * h))))
    y = _dot(h.astype(BF16), w2_ref[...])
    row = lax.broadcasted_iota(jnp.int32, y.shape, 0)
    o_ref[0, 0] = jnp.where(row < nc - 1, y, 0.0)


def _compress(u, pe2, w1, w2p):
    b, g, nc, _ = u.shape
    return pl.pallas_call(
        _compress_kernel,
        grid=(b, g),
        in_specs=[pl.BlockSpec((1, 1, nc, 1024), lambda i, j: (i, j, 0, 0)),
                  pl.BlockSpec((2, 1024), lambda i, j: (0, 0)),
                  pl.BlockSpec((2, 1024, CMP_HIDDEN), lambda i, j: (0, 0, 0)),
                  pl.BlockSpec((CMP_HIDDEN, LANE), lambda i, j: (0, 0))],
        out_specs=pl.BlockSpec((1, 1, nc, LANE), lambda i, j: (i, j, 0, 0)),
        out_shape=jax.ShapeDtypeStruct((b, g, nc, LANE), F32),
        compiler_params=_cparams(("parallel", "parallel")),
        name="nsa_compress",
    )(u, pe2, w1, w2p)


def _stack_heads(q_ref):
    q = q_ref[...] * (HEAD_DIM ** -0.5)
    return jnp.concatenate([q[:, h * LANE:(h + 1) * LANE] for h in range(NSA_HPG)], axis=0).astype(BF16)


def _store_heads(o_ref, o, gates, branch):
    for h in range(NSA_HPG):
        c = branch * NSA_HPG + h
        o_ref[0, h] = (o[h * Q_BLOCK:(h + 1) * Q_BLOCK, :] * gates[:, c:c + 1])[:, :HEAD_DIM]


def _cmp_kernel(q_ref, zg_ref, kc_ref, vc_ref, pc_ref, ov_ref, o_ref, m_ref):
    c = pl.program_id(2)
    nc = kc_ref.shape[2]
    nsel = ov_ref.shape[1]
    qs = _stack_heads(q_ref)
    s = _dot_nt(qs, kc_ref[0, 0].astype(BF16))
    off = (nc - 8) - 8 * c
    bias = pltpu.roll(pc_ref[0], (2 * nc - off) % (2 * nc), 1)[:, :nc]
    s = s + bias
    m = jnp.max(s, axis=-1, keepdims=True)
    p = jnp.exp(s - m)
    l = jnp.sum(p, axis=-1, keepdims=True)
    ql = lax.broadcasted_iota(jnp.int32, (NSA_HPG * Q_BLOCK, 1), 0) & (Q_BLOCK - 1)
    any_key = (c * Q_BLOCK + ql) >= (CMP_LEN - 1)
    p = p * jnp.where(any_key, 1.0 / l, 0.0)
    gates = _sigmoid(zg_ref[...])
    _store_heads(o_ref, _dot(p.astype(BF16), vc_ref[0, 0].astype(BF16)), gates, 0)

    psum = p[0:Q_BLOCK] + p[Q_BLOCK:2 * Q_BLOCK] + p[2 * Q_BLOCK:3 * Q_BLOCK] + p[3 * Q_BLOCK:4 * Q_BLOCK]
    imp = _dot_split(psum, ov_ref[...])
    jb = lax.broadcasted_iota(jnp.int32, (Q_BLOCK, nsel), 1)
    tq = c * Q_BLOCK + lax.broadcasted_iota(jnp.int32, (Q_BLOCK, nsel), 0)
    jq = jnp.right_shift(tq, int(math.log2(SEL_BLOCK)))
    forced = (jb == 0) | (jb == jq) | (jb == jq - 1)
    val = jnp.where(forced, SEL_FORCE, jnp.where(jb <= jq, imp, -SEL_FORCE))
    sel = jnp.zeros((Q_BLOCK, nsel), F32)
    jbf = jb.astype(F32)
    for _ in range(SEL_TOPN):
        mx = jnp.max(val, axis=-1, keepdims=True)
        first = jnp.min(jnp.where(val == mx, jbf, float(nsel)), axis=-1, keepdims=True)
        hit = jbf == first
        sel = jnp.where(hit, jnp.where(mx >= 0.0, 1.0, 0.0), sel)
        val = jnp.where(hit, -3e38, val)
    m_ref[0, 0] = (sel - 1.0) * (-NEG_INF)


def _cmp_branch(z, k_cmp, v_cmp, pc_tab, overlap, bsz, t):
    nq = t // Q_BLOCK
    nc = k_cmp.shape[2]
    nsel = overlap.shape[1]
    return pl.pallas_call(
        _cmp_kernel,
        grid=(bsz, NSA_GROUPS, nq),
        in_specs=[pl.BlockSpec((Q_BLOCK, NSA_HPG * LANE), lambda b, g, c: (b * nq + c, g)),
                  pl.BlockSpec((Q_BLOCK, LANE), lambda b, g, c: (b * nq + c, C_G3 // LANE + g)),
                  pl.BlockSpec((1, 1, nc, LANE), lambda b, g, c: (b, g, 0, 0)),
                  pl.BlockSpec((1, 1, nc, LANE), lambda b, g, c: (b, g, 0, 0)),
                  pl.BlockSpec((1, NSA_HPG * Q_BLOCK, 2 * nc), lambda b, g, c: (g, 0, 0)),
                  pl.BlockSpec((nc, nsel), lambda b, g, c: (0, 0))],
        out_specs=[pl.BlockSpec((1, NSA_HPG, Q_BLOCK, HEAD_DIM), lambda b, g, c: (b, g, c, 0)),
                   pl.BlockSpec((1, 1, Q_BLOCK, nsel), lambda b, g, c: (b, g, c, 0))],
        out_shape=[jax.ShapeDtypeStruct((bsz, NSA_HEADS, t, HEAD_DIM), F32),
                   jax.ShapeDtypeStruct((bsz, NSA_GROUPS, t, nsel), F32)],
        compiler_params=_cparams(("parallel", "parallel", "arbitrary")),
        name="nsa_cmp_select",
    )(z, z, k_cmp, v_cmp, pc_tab, overlap)


def _win_kernel(q_ref, zg_ref, k_ref, v_ref, wb_ref, o_ref):
    c = pl.program_id(2)
    start = pl.multiple_of(jnp.maximum(c * Q_BLOCK - WINDOW, 0), Q_BLOCK)
    qs = _stack_heads(q_ref)
    kw = k_ref[0, 0, pl.ds(start, WINDOW + Q_BLOCK), :]
    vw = v_ref[0, 0, pl.ds(start, WINDOW + Q_BLOCK), :]
    s = _dot_nt(qs, kw) + wb_ref[0, 0]
    m = jnp.max(s, axis=-1, keepdims=True)
    p = jnp.exp(s - m)
    l = jnp.sum(p, axis=-1, keepdims=True)
    o = _dot(p.astype(BF16), vw) * (1.0 / l)
    _store_heads(o_ref, o, _sigmoid(zg_ref[...]), 2)


def _win_branch(z, kw, vw, wb_tab, bsz, t):
    nq = t // Q_BLOCK
    nvar = wb_tab.shape[1]
    return pl.pallas_call(
        _win_kernel,
        grid=(bsz, NSA_GROUPS, nq),
        in_specs=[pl.BlockSpec((Q_BLOCK, NSA_HPG * LANE), lambda b, g, c: (b * nq + c, g)),
                  pl.BlockSpec((Q_BLOCK, LANE), lambda b, g, c: (b * nq + c, C_G3 // LANE + g)),
                  pl.BlockSpec((1, 1, t, LANE), lambda b, g, c: (b, g, 0, 0)),
                  pl.BlockSpec((1, 1, t, LANE), lambda b, g, c: (b, g, 0, 0)),
                  pl.BlockSpec((1, 1, NSA_HPG * Q_BLOCK, WINDOW + Q_BLOCK),
                               lambda b, g, c: (g, jnp.minimum(c, nvar - 1), 0, 0))],
        out_specs=pl.BlockSpec((1, NSA_HPG, Q_BLOCK, HEAD_DIM), lambda b, g, c: (b, g, c, 0)),
        out_shape=jax.ShapeDtypeStruct((bsz, NSA_HEADS, t, HEAD_DIM), F32),
        compiler_params=_cparams(("parallel", "parallel", "arbitrary")),
        name="nsa_window",
    )(z, z, kw, vw, wb_tab)


SEL_TK = 512


SEL_ROWS = 64
SEL_NEAR = FAR_BIAS_DIST + Q_BLOCK


def _sel_kernel(q_ref, zg_ref, k_ref, v_ref, mt_ref, lt_ref, cq_ref, o_ref, m_s, l_s, acc_s, s_s, p_s, a_s):
    c = pl.program_id(2)
    q = q_ref[...] * (HEAD_DIM ** -0.5)
    qs = jnp.concatenate([q[:, h * LANE:(h + 1) * LANE] for h in range(NSA_HPG)], axis=0) + cq_ref[0]
    mt = mt_ref[0, 0]
    q_aug = jnp.concatenate([qs, jnp.concatenate([mt] * NSA_HPG, axis=0)], axis=1).astype(BF16)
    m_s[...] = jnp.full(m_s.shape, NEG_INF, F32)
    l_s[...] = jnp.zeros(l_s.shape, F32)
    acc_s[...] = jnp.zeros(acc_s.shape, F32)
    ncol = SEL_TK // LANE

    def tile(i, near):
        k0 = pl.multiple_of(i * SEL_TK, SEL_TK)
        s_s[...] = _dot_nt(q_aug, k_ref[0, 0, pl.ds(k0, SEL_TK), :])
        off = pl.multiple_of(SEL_NEAR - (c * Q_BLOCK - i * SEL_TK), LANE)
        for r in range(NSA_HPG * Q_BLOCK // SEL_ROWS):
            rows = slice(r * SEL_ROWS, (r + 1) * SEL_ROWS)
            s = s_s[rows, :]
            if near:
                s = s + lt_ref[0, rows, pl.ds(off, SEL_TK)]
            cols = [s[:, j * LANE:(j + 1) * LANE] for j in range(ncol)]
            mx = functools.reduce(jnp.maximum, cols)
            m_old = m_s[rows, :]
            m_new = jnp.maximum(m_old, jnp.max(mx, axis=-1, keepdims=True))
            alpha = jnp.exp(m_old - m_new)
            ps = [jnp.exp(col - m_new) for col in cols]
            l_s[rows, :] = alpha * l_s[rows, :] + jnp.sum(functools.reduce(jnp.add, ps), axis=-1, keepdims=True)
            m_s[rows, :] = m_new
            a_s[rows, :] = alpha
            p_s[rows, :] = jnp.concatenate(ps, axis=1).astype(BF16)
        acc_s[...] = a_s[...] * acc_s[...] + _dot(p_s[...], v_ref[0, 0, pl.ds(k0, SEL_TK), :])

    n_tiles = lax.div(c, SEL_TK // Q_BLOCK) + 1
    n_far = lax.div(jnp.maximum(c - 8, 0), SEL_TK // Q_BLOCK)
    lax.fori_loop(0, n_far, lambda i, carry: (tile(i, False), carry)[1], 0)
    lax.fori_loop(n_far, n_tiles, lambda i, carry: (tile(i, True), carry)[1], 0)
    o = acc_s[...] * (1.0 / l_s[...])
    _store_heads(o_ref, o, _sigmoid(zg_ref[...]), 1)


def _sel_branch(z, ks_aug, vs, mt, lt_tab, cq_tab, bsz, t):
    nq = t // Q_BLOCK
    nsel = mt.shape[-1]
    rows = NSA_HPG * Q_BLOCK
    return pl.pallas_call(
        _sel_kernel,
        grid=(bsz, NSA_GROUPS, nq),
        in_specs=[pl.BlockSpec((Q_BLOCK, NSA_HPG * LANE), lambda b, g, c: (b * nq + c, g)),
                  pl.BlockSpec((Q_BLOCK, LANE), lambda b, g, c: (b * nq + c, C_G3 // LANE + g)),
                  pl.BlockSpec((1, 1, t, 2 * LANE), lambda b, g, c: (b, g, 0, 0)),
                  pl.BlockSpec((1, 1, t, LANE), lambda b, g, c: (b, g, 0, 0)),
                  pl.BlockSpec((1, 1, Q_BLOCK, nsel), lambda b, g, c: (b, g, c, 0)),
                  pl.BlockSpec((1, rows, lt_tab.shape[2]), lambda b, g, c: (g, 0, 0)),
                  pl.BlockSpec((1, rows, LANE), lambda b, g, c: (g, 0, 0))],
        out_specs=pl.BlockSpec((1, NSA_HPG, Q_BLOCK, HEAD_DIM), lambda b, g, c: (b, g, c, 0)),
        out_shape=jax.ShapeDtypeStruct((bsz, NSA_HEADS, t, HEAD_DIM), F32),
        scratch_shapes=[pltpu.VMEM((rows, LANE), F32),
                        pltpu.VMEM((rows, LANE), F32),
                        pltpu.VMEM((rows, LANE), F32),
                        pltpu.VMEM((rows, SEL_TK), F32),
                        pltpu.VMEM((rows, SEL_TK), BF16),
                        pltpu.VMEM((rows, LANE), F32)],
        compiler_params=_cparams(("parallel", "parallel", "arbitrary")),
        name="nsa_selected",
    )(z, z, ks_aug, vs, mt, lt_tab, cq_tab)


def _shift_mix(z_ref, prev_ref, mu, first):
    z = z_ref[...]
    prev = jnp.where(first, 0.0, prev_ref[7:8, :])
    row = lax.broadcasted_iota(jnp.int32, z.shape, 0)
    zp = jnp.where(row == 0, prev, pltpu.roll(z, 1, 0))
    return z + mu * (zp - z)


def _rwkv_prep_kernel(has_vres, tiles_per_seq, *refs):
    (zr, zk, zv, zwa, zg, pr, pk, pv, pwa, pg, mu_ref, vec_ref, w2_ref, a2_ref, g2_ref, hsum_ref) = refs[:16]
    pos = 16
    if has_vres:
        vf_ref, v0_ref, v1_ref, v2_ref = refs[pos:pos + 4]
        pos += 4
    r_o, w_o, k_o, v_o, kk_o, b_o, g_o, bonus_o, vt_o = refs[pos:]
    first = lax.rem(pl.program_id(0), tiles_per_seq) == 0
    r = _shift_mix(zr, pr, mu_ref[0:1, :], first)
    k = _shift_mix(zk, pk, mu_ref[1:2, :], first)
    v = _shift_mix(zv, pv, mu_ref[2:3, :], first)
    wa = _shift_mix(zwa, pwa, mu_ref[3:4, 0:LANE], first)
    zg_s = _shift_mix(zg, pg, mu_ref[4:5, 0:LANE], first)
    w0, a0, k_k, k_a, r_k = (vec_ref[i:i + 1, :] for i in range(5))
    if has_vres:
        lora = _dot(_dot(v.astype(BF16), v1_ref[...]).astype(BF16), v2_ref[...])
        v = v + (vf_ref[...] - v) * _sigmoid(v0_ref[...] + lora)
    u = w0 + _dot(jnp.tanh(wa).astype(BF16), w2_ref[...])
    decay = jnp.exp(-math.exp(-0.5) * _sigmoid(u))
    a = _sigmoid(a0 + _dot(wa.astype(BF16), a2_ref[...]))
    g = _dot(_sigmoid(zg_s).astype(BF16), g2_ref[...])
    kk = k * k_k
    kk = kk / jnp.maximum(jnp.sqrt(_dot_split(kk * kk, hsum_ref[...])), 1e-12)
    k = k * (1.0 + (a - 1.0) * k_a)
    r_o[...] = r
    w_o[...] = decay
    k_o[...] = k
    v_o[...] = v
    kk_o[...] = kk
    b_o[...] = kk * a
    g_o[...] = g
    bonus_o[...] = _dot_split(r * k * r_k, hsum_ref[...]) * v
    vt_o[0] = v.T.astype(BF16)


def _rwkv_prep(z, mu5, vecs, w2p, a2p, g2, hsum, vres, bsz, t, tm=256):
    n = bsz * t
    nt = n // tm
    tps = t // tm
    w = RWKV_WIDTH

    def cur(width, col):
        return pl.BlockSpec((tm, width), lambda i: (i, col // width))

    def prev(width, col):
        return pl.BlockSpec((8, width), lambda i: (jnp.maximum(i * (tm // 8) - 1, 0), col // width))

    def full(shape):
        return pl.BlockSpec(shape, lambda i: (0,) * len(shape))

    cols = [(w, C_R), (w, C_K), (w, C_V), (LANE, C_WA), (LANE, C_ZG)]
    in_specs = [cur(*c) for c in cols] + [prev(*c) for c in cols]
    in_specs += [full(mu5.shape), full(vecs.shape), full(w2p.shape), full(a2p.shape), full(g2.shape), full(hsum.shape)]
    args = [z] * 10 + [mu5, vecs, w2p, a2p, g2, hsum]
    if vres is not None:
        v_first, v0, v1p, v2p = vres
        in_specs += [pl.BlockSpec((tm, w), lambda i: (i, 0)), full(v0.shape), full(v1p.shape), full(v2p.shape)]
        args += [v_first, v0, v1p, v2p]
    tok = pl.BlockSpec((tm, w), lambda i: (i, 0))
    out_specs = [tok] * 8 + [pl.BlockSpec((1, w, tm), lambda i: (i // tps, 0, i % tps))]
    out_shape = [jax.ShapeDtypeStruct((n, w), F32)] * 8 + [jax.ShapeDtypeStruct((bsz, w, t), BF16)]
    return pl.pallas_call(
        functools.partial(_rwkv_prep_kernel, vres is not None, tps),
        grid=(nt,),
        in_specs=in_specs,
        out_specs=out_specs,
        out_shape=out_shape,
        compiler_params=_cparams(("parallel",)),
        name="rwkv_prep",
    )(*args)


N_PAIRS = RWKV_HEADS // 2


def _scan_kernel(r_ref, w_ref, k_ref, kk_ref, b_ref, vt_ref, ones_ref, o_ref, s_ref, kbd_ref):
    bsz = r_ref.shape[0]
    tt = r_ref.shape[1]

    @pl.when(pl.program_id(0) == 0)
    def _():
        s_ref[...] = jnp.zeros(s_ref.shape, F32)

    lane = lax.broadcasted_iota(jnp.int32, (tt, LANE), 1)
    for bi in range(bsz):
        for p in range(N_PAIRS):
            kp = k_ref[bi, :, p * LANE:(p + 1) * LANE]
            kbd_ref[bi * N_PAIRS + p] = jnp.concatenate(
                [jnp.where(lane < RWKV_HEAD, kp, 0.0), jnp.where(lane >= RWKV_HEAD, kp, 0.0)], axis=0).astype(BF16)

    step_lane = lax.broadcasted_iota(jnp.int32, (RWKV_HEAD, 2 * tt), 1) & (tt - 1)
    row8 = lax.broadcasted_iota(jnp.int32, (16, LANE), 0)
    lane8 = lax.broadcasted_iota(jnp.int32, (16, LANE), 1)
    r_mask = ((row8 == 0) & (lane8 < RWKV_HEAD)) | ((row8 == 1) & (lane8 >= RWKV_HEAD))
    ones_bd = ones_ref[...]
    chains = [(bi, p) for bi in range(bsz) for p in range(N_PAIRS)]

    def group(tg, carry):
        t0 = pl.multiple_of(tg * 8, 8)
        rows = []
        for bi, p in chains:
            ln = slice(p * LANE, (p + 1) * LANE)
            rows.append(tuple(ref[bi, pl.ds(t0, 8), ln] for ref in (w_ref, kk_ref, b_ref, r_ref)))
        state = [s_ref[c] for c in range(len(chains))]
        for u in range(8):
            onehot = jnp.where(step_lane == t0 + u, 1.0, 0.0).astype(BF16)
            sk = [_dot((state[c] * rows[c][1][u:u + 1, :]).astype(BF16), ones_bd) for c in range(len(chains))]
            vk = []
            for c, (bi, p) in enumerate(chains):
                vt_pair = jnp.concatenate([vt_ref[bi, p * LANE:p * LANE + RWKV_HEAD, :],
                                           vt_ref[bi, p * LANE + RWKV_HEAD:(p + 1) * LANE, :]], axis=1)
                vk.append(_dot(vt_pair * onehot, kbd_ref[c]))
            for c, (bi, p) in enumerate(chains):
                w8, _, b8, r8 = rows[c]
                state[c] = state[c] * w8[u:u + 1, :] - sk[c] * b8[u:u + 1, :] + vk[c]
                r_lhs = jnp.where(r_mask, r8[u:u + 1, :], 0.0).astype(BF16)
                o = _dot_nt(r_lhs, state[c].astype(BF16))
                o_ref[t0 + u, bi, 2 * p:2 * p + 2, :] = o[0:2, :]
        for c in range(len(chains)):
            s_ref[c] = state[c]
        return carry

    lax.fori_loop(0, tt // 8, group, 0)


def _rwkv_scan(r, w, k, kk, b, vt, ones_bd, bsz, t):
    tt = SCAN_TT
    wd = RWKV_WIDTH
    tok = pl.BlockSpec((bsz, tt, wd), lambda i: (0, i, 0))
    return pl.pallas_call(
        _scan_kernel,
        grid=(t // tt,),
        in_specs=[tok, tok, tok, tok, tok,
                  pl.BlockSpec((bsz, wd, tt), lambda i: (0, 0, i)),
                  pl.BlockSpec((LANE, LANE), lambda i: (0, 0))],
        out_specs=pl.BlockSpec((tt, bsz, RWKV_HEADS, RWKV_HEAD), lambda i: (i, 0, 0, 0)),
        out_shape=jax.ShapeDtypeStruct((t, bsz, RWKV_HEADS, RWKV_HEAD), F32),
        scratch_shapes=[pltpu.VMEM((bsz * N_PAIRS, RWKV_HEAD, LANE), F32),
                        pltpu.VMEM((bsz * N_PAIRS, 2 * tt, LANE), BF16)],
        compiler_params=_cparams(("arbitrary",)),
        name="rwkv_scan",
    )(r, w, k, kk, b, vt, ones_bd)


def _merge_kernel(x_ref, oc_ref, os_ref, ow_ref, orw_ref, bonus_ref, g_ref, zgn_ref, zgr_ref, gn_ref, hsum_ref,
                  pn_ref, pr_ref, wo_ref, ln_ref, o_ref):
    y_n = jnp.zeros(o_ref.shape, F32)
    for h in range(NSA_HEADS):
        o_h = (oc_ref[0, h] + os_ref[0, h] + ow_ref[0, h]).astype(BF16)
        y_n = y_n + _dot(o_h, pn_ref[h * HEAD_DIM:(h + 1) * HEAD_DIM, :])
    o = orw_ref[...]
    mu = _dot_split(o, hsum_ref[...]) * (1.0 / RWKV_HEAD)
    d = o - mu
    var = _dot_split(d * d, hsum_ref[...]) * (1.0 / RWKV_HEAD)
    o = d * lax.rsqrt(var + GN_EPS) * gn_ref[0:1, :] + gn_ref[1:2, :]
    o = ((o + bonus_ref[...]) * g_ref[...]).astype(BF16)
    y1 = _sigmoid(zgn_ref[...]) * y_n + _sigmoid(zgr_ref[...]) * _dot(o, pr_ref[...])
    y = _dot(y1.astype(BF16), wo_ref[...])
    o_ref[...] = _layer_norm(DN_ALPHA * x_ref[...] + y, ln_ref[0:1, :], ln_ref[1:2, :])


def _merge(x, o_c, o_s, o_w, o_rw, bonus, g, z, gn, hsum, p_nsa, p_rwkv, w_out, ln, bsz, t, tm=256):
    n = bsz * t
    tps = t // tm
    d = D_MODEL
    w = RWKV_WIDTH
    head = pl.BlockSpec((1, NSA_HEADS, tm, HEAD_DIM), lambda i: (i // tps, 0, i % tps, 0))
    tok = pl.BlockSpec((tm, w), lambda i: (i, 0))

    def full(a):
        return pl.BlockSpec(a.shape, lambda i: (0,) * a.ndim)

    return pl.pallas_call(
        _merge_kernel,
        grid=(n // tm,),
        in_specs=[pl.BlockSpec((tm, d), lambda i: (i, 0)), head, head, head, tok, tok, tok,
                  pl.BlockSpec((tm, d), lambda i: (i, C_GN // d)),
                  pl.BlockSpec((tm, d), lambda i: (i, C_GR // d)),
                  full(gn), full(hsum), full(p_nsa), full(p_rwkv), full(w_out), full(ln)],
        out_specs=pl.BlockSpec((tm, d), lambda i: (i, 0)),
        out_shape=jax.ShapeDtypeStruct((n, d), F32),
        compiler_params=_cparams(("parallel",)),
        name="mix_merge_ln",
    )(x, o_c, o_s, o_w, o_rw, bonus, g, z, z, gn, hsum, p_nsa, p_rwkv, w_out, ln)


def _xattn_kernel(x_ref, kt_ref, v_ref, wq_ref, wo_ref, ln_ref, rw_ref, o_ref, s_ref):
    x = x_ref[...]
    q = _dot(x.astype(BF16), wq_ref[...]).astype(BF16)
    outs = []
    for h in range(XATTN_HEADS):
        hs = slice(h * XATTN_HEAD, (h + 1) * XATTN_HEAD)
        s = _dot(q[:, hs], kt_ref[0, hs, :]) * (XATTN_HEAD ** -0.5)
        m = jnp.max(s, axis=-1, keepdims=True)
        p = jnp.exp(s - m)
        l = jnp.sum(p, axis=-1, keepdims=True)
        outs.append(_dot(p.astype(BF16), v_ref[0, :, hs]) * (1.0 / l))
    o = jnp.concatenate(outs, axis=1).astype(BF16)
    x2 = _layer_norm(DN_ALPHA * x + _dot(o, wo_ref[...]), ln_ref[0:1, :], ln_ref[1:2, :])
    o_ref[...] = x2
    hi = x2.astype(BF16)
    lo = (x2 - hi.astype(F32)).astype(BF16)
    logits = _dot(hi, rw_ref[0]) + (_dot(lo, rw_ref[0]) + _dot(hi, rw_ref[1]))
    s_ref[...] = _sigmoid(logits)


def _xattn(x, kt, v, wq, wo, ln, rw2, bsz, t, tm=256):
    n = bsz * t
    tps = t // tm
    d = D_MODEL

    def full(a):
        return pl.BlockSpec(a.shape, lambda i: (0,) * a.ndim)

    return pl.pallas_call(
        _xattn_kernel,
        grid=(n // tm,),
        in_specs=[pl.BlockSpec((tm, d), lambda i: (i, 0)),
                  pl.BlockSpec((1,) + kt.shape[1:], lambda i: (i // tps, 0, 0)),
                  pl.BlockSpec((1,) + v.shape[1:], lambda i: (i // tps, 0, 0)),
                  full(wq), full(wo), full(ln), full(rw2)],
        out_specs=[pl.BlockSpec((tm, d), lambda i: (i, 0)), pl.BlockSpec((tm, LANE), lambda i: (i, 0))],
        out_shape=[jax.ShapeDtypeStruct((n, d), F32), jax.ShapeDtypeStruct((n, LANE), F32)],
        compiler_params=_cparams(("parallel",)),
        name="xattn_ln_router",
    )(x, kt, v, wq, wo, ln, rw2)


def _expert_kernel(be_ref, x_ref, wg_ref, wu_ref, wd_ref, o_ref):
    x = x_ref[...]
    hg = _dot(x, wg_ref[0])
    h = hg * _sigmoid(hg) * _dot(x, wu_ref[0])
    o_ref[...] = _dot(h.astype(BF16), wd_ref[0])


def _experts(block_expert, rows, w_gate, w_up, w_down):
    n_rows, d = rows.shape
    ff = w_gate.shape[2]
    return pl.pallas_call(
        _expert_kernel,
        grid_spec=pltpu.PrefetchScalarGridSpec(
            num_scalar_prefetch=1,
            grid=(n_rows // MOE_BLK,),
            in_specs=[pl.BlockSpec((MOE_BLK, d), lambda i, be: (i, 0)),
                      pl.BlockSpec((1, d, ff), lambda i, be: (be[i], 0, 0)),
                      pl.BlockSpec((1, d, ff), lambda i, be: (be[i], 0, 0)),
                      pl.BlockSpec((1, ff, d), lambda i, be: (be[i], 0, 0))],
            out_specs=pl.BlockSpec((MOE_BLK, d), lambda i, be: (i, 0))),
        out_shape=jax.ShapeDtypeStruct((n_rows, d), F32),
        compiler_params=_cparams(("arbitrary",)),
        name="moe_experts",
    )(block_expert, rows, w_gate, w_up, w_down)


def _combine_kernel(x_ref, ya_ref, yb_ref, gate_ref, ln_ref, o_ref):
    y = gate_ref[:, 0:1] * ya_ref[...] + gate_ref[:, 1:2] * yb_ref[...]
    o_ref[...] = _layer_norm(DN_ALPHA * x_ref[...] + y, ln_ref[0:1, :], ln_ref[1:2, :])


def _combine(x, ya, yb, gate, ln, tm=512):
    n, d = x.shape
    tok = pl.BlockSpec((tm, d), lambda i: (i, 0))
    return pl.pallas_call(
        _combine_kernel,
        grid=(n // tm,),
        in_specs=[tok, tok, tok, pl.BlockSpec((tm, TOP_K), lambda i: (i, 0)),
                  pl.BlockSpec(ln.shape, lambda i: (0, 0))],
        out_specs=tok,
        out_shape=jax.ShapeDtypeStruct((n, d), F32),
        compiler_params=_cparams(("parallel",)),
        name="moe_combine_ln",
    )(x, ya, yb, gate, ln)


def _t5_bucket(dist):
    n = jnp.maximum(dist, 0)
    max_exact = REL_BUCKETS // 2
    log_ratio = jnp.log(jnp.maximum(n, 1).astype(F32) / max_exact) / math.log(REL_MAX_DIST / max_exact)
    large = jnp.minimum(max_exact + (log_ratio * (REL_BUCKETS - max_exact)).astype(jnp.int32), REL_BUCKETS - 1)
    return jnp.where(n < max_exact, n, large)


def _bias_tables(rel_table, t):
    nc = t // CMP_STRIDE
    f = rel_table[_t5_bucket(jnp.arange(t + 2048))].astype(F32).T
    ql = jnp.arange(Q_BLOCK)[:, None]

    def lookup(dist, ok):
        vals = jnp.take(f, jnp.clip(dist, 0, f.shape[-1] - 1), axis=1)
        vals = jnp.where(ok[None], vals, NEG_INF)
        return vals.reshape(NSA_GROUPS, NSA_HPG * dist.shape[0], dist.shape[1])

    j = jnp.arange(2 * nc)[None, :]
    dist_c = CMP_STRIDE * (nc - 8 - j) + ql - (CMP_LEN - 1)
    pc = lookup(dist_c, dist_c >= 0)
    nvar = WINDOW // Q_BLOCK + 1
    kl = jnp.arange(WINDOW + Q_BLOCK)[None, :]
    wb = jnp.stack([lookup(Q_BLOCK * v + ql - kl, (Q_BLOCK * v + ql - kl >= 0) & (Q_BLOCK * v + ql - kl < WINDOW))
                    for v in range(nvar)], axis=1)
    far = rel_table[REL_BUCKETS - 1].astype(F32)
    j = jnp.arange(SEL_NEAR + SEL_TK)[None, :]
    dist_s = SEL_NEAR + ql - j
    vals = jnp.take(f, jnp.clip(dist_s, 0, f.shape[-1] - 1), axis=1) - far[:, None, None]
    lt = jnp.where((dist_s >= 0)[None], vals, NEG_INF).reshape(NSA_GROUPS, NSA_HPG * Q_BLOCK, -1)
    far_hi = far.astype(BF16).astype(F32)
    lane = jnp.arange(LANE)[None, :]
    cq = jnp.where(lane == HEAD_DIM, far_hi[:, None], jnp.where(lane == HEAD_DIM + 1, (far - far_hi)[:, None], 0.0))
    cq = jnp.broadcast_to(cq[:, None, :], (NSA_HEADS, Q_BLOCK, LANE)).reshape(NSA_GROUPS, NSA_HPG * Q_BLOCK, LANE)
    return pc, wb, lt, cq


def _in_proj_perm():
    perm = np.full((IN_PAD,), -1, np.int64)
    for h in range(NSA_HEADS):
        perm[C_Q + h * LANE:C_Q + h * LANE + HEAD_DIM] = np.arange(HEAD_DIM) + h * HEAD_DIM
    rw0 = NSA_COLS
    g0 = NSA_COLS + RWKV_COLS
    perm[C_GN:C_GN + D_MODEL] = g0 + np.arange(D_MODEL)
    perm[C_GR:C_GR + D_MODEL] = g0 + D_MODEL + np.arange(D_MODEL)
    perm[C_R:C_R + RWKV_WIDTH] = rw0 + np.arange(RWKV_WIDTH)
    perm[C_K:C_K + RWKV_WIDTH] = rw0 + RWKV_WIDTH + np.arange(RWKV_WIDTH)
    perm[C_V:C_V + RWKV_WIDTH] = rw0 + 2 * RWKV_WIDTH + np.arange(RWKV_WIDTH)
    perm[C_KV:C_KV + 6 * NSA_KV] = NSA_WIDTH + np.arange(6 * NSA_KV)
    for g in range(NSA_GROUPS):
        for br in range(3):
            for h in range(NSA_HPG):
                perm[C_G3 + g * LANE + br * NSA_HPG + h] = NSA_WIDTH + 6 * NSA_KV + (g * NSA_HPG + h) * 3 + br
    perm[C_WA:C_WA + LORA_W + LORA_A] = rw0 + 3 * RWKV_WIDTH + np.arange(LORA_W + LORA_A)
    perm[C_ZG:C_ZG + LORA_G] = rw0 + 3 * RWKV_WIDTH + LORA_W + LORA_A + np.arange(LORA_G)
    return perm


_PERM = _in_proj_perm()


def _pad_to(a, shape):
    return jnp.pad(a, [(0, s - d) for d, s in zip(a.shape, shape)])


def _route(s, router_bias, n_tok):
    s16 = s[:, :N_EXPERTS]
    s_sel = s16 + router_bias.astype(F32)

    def top2(a):
        i1 = jnp.argmax(a, axis=-1)
        rest = jnp.where(jnp.arange(a.shape[-1]) == i1[..., None], -jnp.inf, a)
        i2 = jnp.argmax(rest, axis=-1)
        return jnp.max(a, axis=-1) + jnp.max(rest, axis=-1), jnp.stack([i1, i2], axis=-1)

    group_score, _ = top2(s_sel.reshape(n_tok, N_EXPERT_GROUPS, EXPERTS_PER_GROUP))
    group = jnp.argmax(group_score, axis=-1)
    in_group = (jnp.arange(N_EXPERTS) // EXPERTS_PER_GROUP)[None, :] == group[:, None]
    _, e_idx = top2(jnp.where(in_group, s_sel, NEG_INF))
    s_top = jnp.take_along_axis(s16, e_idx, axis=-1)
    gate = s_top / jnp.sum(s_top, axis=-1, keepdims=True)
    e_flat = e_idx.reshape(-1)
    onehot = (e_flat[:, None] == jnp.arange(N_EXPERTS)[None, :]).astype(jnp.int32)
    rank = jnp.take_along_axis(jnp.cumsum(onehot, axis=0) - onehot, e_flat[:, None], axis=1)[:, 0]
    counts = jnp.sum(onehot, axis=0)
    padded = (counts + MOE_BLK - 1) // MOE_BLK * MOE_BLK
    ends = jnp.cumsum(padded)
    dest = (ends - padded)[e_flat] + rank
    n_rows = n_tok * TOP_K + N_EXPERTS * MOE_BLK
    tok_of_row = jnp.zeros((n_rows,), jnp.int32).at[dest].set(jnp.arange(n_tok * TOP_K, dtype=jnp.int32) // TOP_K)
    block_expert = jnp.minimum(
        jnp.searchsorted(ends, jnp.arange(n_rows // MOE_BLK) * MOE_BLK, side='right'), N_EXPERTS - 1).astype(jnp.int32)
    return gate, dest.reshape(n_tok, TOP_K), tok_of_row, block_expert


def kernel(x, mem, rel_table, router_w, router_bias, w_in, cmp_pe_k, cmp_pe_v, cmp_w1k, cmp_w2k, cmp_w1v, cmp_w2v, rwkv_mu, rwkv_w0, rwkv_w2, rwkv_a0, rwkv_a2, rwkv_g2, rwkv_kk, rwkv_ka, rwkv_rk, rwkv_gn_g, rwkv_gn_b, rwkv_v0, rwkv_v1, rwkv_v2, p_nsa, p_rwkv, w_out, ln1_g, ln1_b, xq_w, xk_w, xv_w, xo_w, ln2_g, ln2_b, moe_w_gate, moe_w_up, moe_w_down, ln3_g, ln3_b):
    bsz, t, d = x.shape
    n = bsz * t
    depth = w_in.shape[0]
    nc = t // CMP_STRIDE
    nsel = t // SEL_BLOCK
    g, hd = NSA_GROUPS, HEAD_DIM

    pc_tab, wb_tab, lt_tab, cq_tab = _bias_tables(rel_table, t)
    cmp_start = np.arange(nc) * CMP_STRIDE
    sel_start = np.arange(nsel) * SEL_BLOCK
    overlap = ((cmp_start[:, None] <= sel_start[None, :] + SEL_BLOCK - 1)
               & (cmp_start[:, None] + CMP_LEN - 1 >= sel_start[None, :]) & (cmp_start[:, None] < (nc - 1) * CMP_STRIDE))
    overlap = jnp.asarray(overlap, BF16)
    key_tail = np.zeros((t, 2 * LANE - HEAD_DIM), np.float32)
    key_tail[:, 0:2] = 1.0
    key_tail[np.arange(t), LANE - HEAD_DIM + np.arange(t) // SEL_BLOCK] = 1.0
    key_tail = jnp.asarray(key_tail, BF16)
    lane_head = np.arange(RWKV_WIDTH) // RWKV_HEAD
    hsum = jnp.asarray(lane_head[:, None] == lane_head[None, :], BF16)
    ones_bd = jnp.asarray((np.arange(LANE) // RWKV_HEAD)[:, None] == (np.arange(LANE) // RWKV_HEAD)[None, :], BF16)
    rw_f = _pad_to(router_w.astype(F32), (d, LANE))
    rw_hi = rw_f.astype(BF16)
    rw2 = jnp.stack([rw_hi, (rw_f - rw_hi.astype(F32)).astype(BF16)])
    perm = jnp.asarray(np.maximum(_PERM, 0))
    perm_ok = jnp.asarray(_PERM >= 0)

    xf = x.reshape(n, d)
    mem_f = mem.reshape(bsz * mem.shape[1], d)
    v_first = None
    for l in range(depth):
        w_in_l = jnp.where(perm_ok[None, :], jnp.take(w_in[l], perm, axis=1), 0.0).astype(BF16)
        z = _matmul(xf, w_in_l, F32, 512, 1024)

        kv = z[:, C_KV:C_KV + 6 * NSA_KV].reshape(bsz, t, 6, g, hd)
        kc, vc, ks, vs, kw, vw = (kv[:, :, i] for i in range(6))

        def chunks(a):
            return a.reshape(bsz, nc, CMP_STRIDE, g, hd).transpose(0, 3, 1, 2, 4).reshape(bsz, g, nc, CMP_STRIDE * hd)

        def keys(a):
            return _pad_to(a.transpose(0, 2, 1, 3), (bsz, g, t, LANE)).astype(BF16)

        def cmp_w(pe, w1, w2):
            return (pe.reshape(2, CMP_STRIDE * hd), w1.reshape(2, CMP_STRIDE * hd, CMP_HIDDEN).astype(BF16),
                    _pad_to(w2, (CMP_HIDDEN, LANE)).astype(BF16))

        k_cmp = _compress(chunks(kc), *cmp_w(cmp_pe_k[l], cmp_w1k[l], cmp_w2k[l]))
        v_cmp = _compress(chunks(vc), *cmp_w(cmp_pe_v[l], cmp_w1v[l], cmp_w2v[l]))
        o_c, sel_mask = _cmp_branch(z, k_cmp, v_cmp, pc_tab, overlap, bsz, t)
        ks_aug = jnp.concatenate([ks.transpose(0, 2, 1, 3).astype(BF16),
                                  jnp.broadcast_to(key_tail, (bsz, g, t, key_tail.shape[1]))], axis=-1)
        o_s = _sel_branch(z, ks_aug, keys(vs), sel_mask, lt_tab, cq_tab, bsz, t)
        o_w = _win_branch(z, keys(kw), keys(vw), wb_tab, bsz, t)

        mu = rwkv_mu[l]
        w3 = 3 * RWKV_WIDTH
        mu5 = jnp.stack([mu[0:512], mu[512:1024], mu[1024:1536], _pad_to(mu[w3:w3 + 128], (512,)),
                         _pad_to(mu[w3 + 128:w3 + 256], (512,))])
        vecs = jnp.stack([rwkv_w0[l], rwkv_a0[l], rwkv_kk[l], rwkv_ka[l], rwkv_rk[l].reshape(-1)])
        w2p = jnp.concatenate([rwkv_w2[l], jnp.zeros_like(rwkv_a2[l])], axis=0).astype(BF16)
        a2p = jnp.concatenate([jnp.zeros_like(rwkv_w2[l]), rwkv_a2[l]], axis=0).astype(BF16)
        vres = None
        if l > 0:
            vres = (v_first, rwkv_v0[l - 1][None, :], _pad_to(rwkv_v1[l - 1], (RWKV_WIDTH, LANE)).astype(BF16),
                    _pad_to(rwkv_v2[l - 1], (LANE, RWKV_WIDTH)).astype(BF16))
        r, w, k, v, kk, b, g_out, bonus, vt = _rwkv_prep(z, mu5, vecs, w2p, a2p, rwkv_g2[l].astype(BF16), hsum, vres,
                                                          bsz, t)
        if l == 0:
            v_first = v
        seq = lambda a: a.reshape(bsz, t, RWKV_WIDTH)
        o_rw = _rwkv_scan(seq(r), seq(w), seq(k), seq(kk), seq(b), vt, ones_bd, bsz, t)
        o_rw = o_rw.transpose(1, 0, 2, 3).reshape(n, RWKV_WIDTH)

        xf = _merge(xf, o_c, o_s, o_w, o_rw, bonus, g_out, z, jnp.stack([rwkv_gn_g[l], rwkv_gn_b[l]]), hsum,
                    p_nsa[l].astype(BF16), p_rwkv[l].astype(BF16), w_out[l].astype(BF16),
                    jnp.stack([ln1_g[l], ln1_b[l]]), bsz, t)

        mlen = mem.shape[1]
        mk = _matmul(mem_f, xk_w[l].astype(BF16), BF16, mlen, XATTN_WIDTH)
        mv = _matmul(mem_f, xv_w[l].astype(BF16), BF16, mlen, XATTN_WIDTH)
        kt = mk.reshape(bsz, mlen, XATTN_WIDTH).transpose(0, 2, 1)
        xf, scores = _xattn(xf, kt, mv.reshape(bsz, mlen, XATTN_WIDTH), xq_w[l].astype(BF16), xo_w[l].astype(BF16),
                            jnp.stack([ln2_g[l], ln2_b[l]]), rw2, bsz, t)

        gate, dest, tok_of_row, block_expert = _route(scores, router_bias, n)
        rows = jnp.take(xf.astype(BF16), tok_of_row, axis=0)
        y_rows = _experts(block_expert, rows, moe_w_gate[l].astype(BF16), moe_w_up[l].astype(BF16),
                          moe_w_down[l].astype(BF16))
        xf = _combine(xf, jnp.take(y_rows, dest[:, 0], axis=0), jnp.take(y_rows, dest[:, 1], axis=0), gate,
                      jnp.stack([ln3_g[l], ln3_b[l]]))
    return xf.reshape(bsz, t, d)
```

```python
import functools
import math

import numpy as np
import jax
import jax.numpy as jnp
from jax import lax
from jax.experimental import pallas as pl
from jax.experimental.pallas import tpu as pltpu

F32 = jnp.float32
BF16 = jnp.bfloat16

D_MODEL = 1024
DEPTH = 2
NSA_HEADS = 8
NSA_GROUPS = 2
NSA_HPG = NSA_HEADS // NSA_GROUPS
HEAD_DIM = 64
NSA_WIDTH = NSA_HEADS * HEAD_DIM
NSA_KV = NSA_GROUPS * HEAD_DIM
CMP_STRIDE = 16
CMP_LEN = 2 * CMP_STRIDE
CMP_HIDDEN = 256
SEL_BLOCK = 64
SEL_TOPN = 16
WINDOW = 512
Q_BLOCK = 128
SEL_FORCE = 1e9
RWKV_HEADS = 8
RWKV_HEAD = 64
RWKV_WIDTH = RWKV_HEADS * RWKV_HEAD
LORA_W = 64
LORA_A = 64
LORA_V = 32
LORA_G = 128
GN_EPS = 64e-5
REL_BUCKETS = 32
REL_MAX_DIST = 1024
XATTN_HEADS = 4
XATTN_HEAD = 128
XATTN_WIDTH = XATTN_HEADS * XATTN_HEAD
N_EXPERTS = 16
N_EXPERT_GROUPS = 4
EXPERTS_PER_GROUP = N_EXPERTS // N_EXPERT_GROUPS
TOP_K = 2
EXPERT_FF = 512
DN_ALPHA = (2 * DEPTH) ** 0.25
LN_EPS = 1e-5
NEG_INF = -1e30
NSA_COLS = NSA_WIDTH + 6 * NSA_KV + 3 * NSA_HEADS
RWKV_COLS = 3 * RWKV_WIDTH + LORA_W + LORA_A + LORA_G

LANE = 128
VMEM_LIMIT = 56 * 1024 * 1024

C_Q = 0
C_GN = 1024
C_GR = 2048
C_R = 3072
C_K = 3584
C_V = 4096
C_KV = 4608
C_G3 = 5376
C_WA = 5632
C_ZG = 5760
IN_PAD = 6144

MOE_BLK = 256
SCAN_TT = 128
FAR_BIAS_DIST = 1280


def _cparams(sem):
    return pltpu.CompilerParams(dimension_semantics=sem, vmem_limit_bytes=VMEM_LIMIT)


def _sigmoid(x):
    return 1.0 / (1.0 + jnp.exp(-x))


def _layer_norm(v, g, b):
    mu = jnp.mean(v, axis=-1, keepdims=True)
    d = v - mu
    var = jnp.mean(d * d, axis=-1, keepdims=True)
    return d * lax.rsqrt(var + LN_EPS) * g + b


def _dot(a, b):
    return jnp.dot(a, b, preferred_element_type=F32)


def _dot_nt(a, b):
    return lax.dot_general(a, b, (((1,), (1,)), ((), ())), preferred_element_type=F32)


def _dot_split(x, w):
    hi = x.astype(BF16)
    lo = (x - hi.astype(F32)).astype(BF16)
    return _dot(hi, w) + _dot(lo, w)


def _mm_kernel(x_ref, w_ref, o_ref, xb_ref):
    @pl.when(pl.program_id(1) == 0)
    def _():
        xb_ref[...] = x_ref[...].astype(BF16)

    o_ref[...] = _dot(xb_ref[...], w_ref[...]).astype(o_ref.dtype)


def _matmul(x, w, out_dtype, tm, tn):
    n, k = x.shape
    m = w.shape[1]
    return pl.pallas_call(
        _mm_kernel,
        grid=(n // tm, m // tn),
        in_specs=[pl.BlockSpec((tm, k), lambda i, j: (i, 0)),
                  pl.BlockSpec((k, tn), lambda i, j: (0, j))],
        out_specs=pl.BlockSpec((tm, tn), lambda i, j: (i, j)),
        out_shape=jax.ShapeDtypeStruct((n, m), out_dtype),
        scratch_shapes=[pltpu.VMEM((tm, k), BF16)],
        compiler_params=_cparams(("parallel", "arbitrary")),
        name="matmul",
    )(x, w)


def _compress_kernel(u_ref, pe_ref, w1_ref, w2_ref, o_ref):
    u = u_ref[0, 0]
    a = _dot((u + pe_ref[0:1, :]).astype(BF16), w1_ref[0])
    b = _dot((u + pe_ref[1:2, :]).astype(BF16), w1_ref[1])
    nc = u.shape[0]
    h = a + pltpu.roll(b, nc - 1, 0)
    h = 0.5 * h * (1.0 + jnp.tanh(math.sqrt(2.0 / math.pi) * (h + 0.044715 * (h * h * h))))
    y = _dot(h.astype(BF16), w2_ref[...])
    row = lax.broadcasted_iota(jnp.int32, y.shape, 0)
    o_ref[0, 0] = jnp.where(row < nc - 1, y, 0.0)


def _compress(u, pe2, w1, w2p):
    b, g, nc, _ = u.shape
    return pl.pallas_call(
        _compress_kernel,
        grid=(b, g),
        in_specs=[pl.BlockSpec((1, 1, nc, 1024), lambda i, j: (i, j, 0, 0)),
                  pl.BlockSpec((2, 1024), lambda i, j: (0, 0)),
                  pl.BlockSpec((2, 1024, CMP_HIDDEN), lambda i, j: (0, 0, 0)),
                  pl.BlockSpec((CMP_HIDDEN, LANE), lambda i, j: (0, 0))],
        out_specs=pl.BlockSpec((1, 1, nc, LANE), lambda i, j: (i, j, 0, 0)),
        out_shape=jax.ShapeDtypeStruct((b, g, nc, LANE), F32),
        compiler_params=_cparams(("parallel", "parallel")),
        name="nsa_compress",
    )(u, pe2, w1, w2p)


def _stack_heads(q_ref):
    q = q_ref[...] * (HEAD_DIM ** -0.5)
    return jnp.concatenate([q[:, h * LANE:(h + 1) * LANE] for h in range(NSA_HPG)], axis=0).astype(BF16)


def _store_heads(o_ref, o, gates, branch):
    for h in range(NSA_HPG):
        c = branch * NSA_HPG + h
        o_ref[0, h] = (o[h * Q_BLOCK:(h + 1) * Q_BLOCK, :] * gates[:, c:c + 1])[:, :HEAD_DIM]


def _cmp_kernel(q_ref, zg_ref, kc_ref, vc_ref, pc_ref, ov_ref, o_ref, m_ref):
    c = pl.program_id(2)
    nc = kc_ref.shape[2]
    nsel = ov_ref.shape[0]
    qs = _stack_heads(q_ref)
    s = _dot_nt(qs, kc_ref[0, 0].astype(BF16))
    off = (nc - 8) - 8 * c
    bias = pltpu.roll(pc_ref[0], (2 * nc - off) % (2 * nc), 1)[:, :nc]
    s = s + bias
    m = jnp.max(s, axis=-1, keepdims=True)
    p = jnp.exp(s - m)
    l = jnp.sum(p, axis=-1, keepdims=True)
    ql = lax.broadcasted_iota(jnp.int32, (NSA_HPG * Q_BLOCK, 1), 0) & (Q_BLOCK - 1)
    any_key = (c * Q_BLOCK + ql) >= (CMP_LEN - 1)
    p = p * jnp.where(any_key, 1.0 / l, 0.0)
    gates = _sigmoid(zg_ref[...])
    _store_heads(o_ref, _dot(p.astype(BF16), vc_ref[0, 0].astype(BF16)), gates, 0)

    psum = p[0:Q_BLOCK] + p[Q_BLOCK:2 * Q_BLOCK] + p[2 * Q_BLOCK:3 * Q_BLOCK] + p[3 * Q_BLOCK:4 * Q_BLOCK]
    hi = psum.astype(BF16)
    lo = (psum - hi.astype(F32)).astype(BF16)
    imp = _dot_nt(ov_ref[...], hi) + _dot_nt(ov_ref[...], lo)
    jb = lax.broadcasted_iota(jnp.int32, (nsel, Q_BLOCK), 0)
    tq = c * Q_BLOCK + lax.broadcasted_iota(jnp.int32, (nsel, Q_BLOCK), 1)
    jq = jnp.right_shift(tq, int(math.log2(SEL_BLOCK)))
    forced = (jb == 0) | (jb == jq) | (jb == jq - 1)
    val = jnp.where(forced, SEL_FORCE, jnp.where(jb <= jq, imp, -SEL_FORCE))
    sel = jnp.zeros((nsel, Q_BLOCK), F32)
    jbf = jb.astype(F32)
    for _ in range(SEL_TOPN):
        mx = jnp.max(val, axis=0, keepdims=True)
        first = jnp.min(jnp.where(val == mx, jbf, float(nsel)), axis=0, keepdims=True)
        hit = jbf == first
        sel = jnp.where(hit, jnp.where(mx >= 0.0, 1.0, 0.0), sel)
        val = jnp.where(hit, -3e38, val)
    m_ref[0, 0] = (sel.T - 1.0) * (-NEG_INF)


def _cmp_branch(z, k_cmp, v_cmp, pc_tab, overlap, bsz, t):
    nq = t // Q_BLOCK
    nc = k_cmp.shape[2]
    nsel = overlap.shape[0]
    return pl.pallas_call(
        _cmp_kernel,
        grid=(bsz, NSA_GROUPS, nq),
        in_specs=[pl.BlockSpec((Q_BLOCK, NSA_HPG * LANE), lambda b, g, c: (b * nq + c, g)),
                  pl.BlockSpec((Q_BLOCK, LANE), lambda b, g, c: (b * nq + c, C_G3 // LANE + g)),
                  pl.BlockSpec((1, 1, nc, LANE), lambda b, g, c: (b, g, 0, 0)),
                  pl.BlockSpec((1, 1, nc, LANE), lambda b, g, c: (b, g, 0, 0)),
                  pl.BlockSpec((1, NSA_HPG * Q_BLOCK, 2 * nc), lambda b, g, c: (g, 0, 0)),
                  pl.BlockSpec((nsel, nc), lambda b, g, c: (0, 0))],
        out_specs=[pl.BlockSpec((1, NSA_HPG, Q_BLOCK, HEAD_DIM), lambda b, g, c: (b, g, c, 0)),
                   pl.BlockSpec((1, 1, Q_BLOCK, nsel), lambda b, g, c: (b, g, c, 0))],
        out_shape=[jax.ShapeDtypeStruct((bsz, NSA_HEADS, t, HEAD_DIM), F32),
                   jax.ShapeDtypeStruct((bsz, NSA_GROUPS, t, nsel), F32)],
        compiler_params=_cparams(("parallel", "parallel", "arbitrary")),
        name="nsa_cmp_select",
    )(z, z, k_cmp, v_cmp, pc_tab, overlap)


def _win_kernel(q_ref, zg_ref, k_ref, v_ref, wb_ref, o_ref):
    c = pl.program_id(2)
    start = pl.multiple_of(jnp.maximum(c * Q_BLOCK - WINDOW, 0), Q_BLOCK)
    qs = _stack_heads(q_ref)
    kw = k_ref[0, 0, pl.ds(start, WINDOW + Q_BLOCK), :]
    vw = v_ref[0, 0, pl.ds(start, WINDOW + Q_BLOCK), :]
    s = _dot_nt(qs, kw) + wb_ref[0, 0]
    m = jnp.max(s, axis=-1, keepdims=True)
    p = jnp.exp(s - m)
    l = jnp.sum(p, axis=-1, keepdims=True)
    o = _dot(p.astype(BF16), vw) * (1.0 / l)
    _store_heads(o_ref, o, _sigmoid(zg_ref[...]), 2)


def _win_branch(z, kw, vw, wb_tab, bsz, t):
    nq = t // Q_BLOCK
    nvar = wb_tab.shape[1]
    return pl.pallas_call(
        _win_kernel,
        grid=(bsz, NSA_GROUPS, nq),
        in_specs=[pl.BlockSpec((Q_BLOCK, NSA_HPG * LANE), lambda b, g, c: (b * nq + c, g)),
                  pl.BlockSpec((Q_BLOCK, LANE), lambda b, g, c: (b * nq + c, C_G3 // LANE + g)),
                  pl.BlockSpec((1, 1, t, LANE), lambda b, g, c: (b, g, 0, 0)),
                  pl.BlockSpec((1, 1, t, LANE), lambda b, g, c: (b, g, 0, 0)),
                  pl.BlockSpec((1, 1, NSA_HPG * Q_BLOCK, WINDOW + Q_BLOCK),
                               lambda b, g, c: (g, jnp.minimum(c, nvar - 1), 0, 0))],
        out_specs=pl.BlockSpec((1, NSA_HPG, Q_BLOCK, HEAD_DIM), lambda b, g, c: (b, g, c, 0)),
        out_shape=jax.ShapeDtypeStruct((bsz, NSA_HEADS, t, HEAD_DIM), F32),
        compiler_params=_cparams(("parallel", "parallel", "arbitrary")),
        name="nsa_window",
    )(z, z, kw, vw, wb_tab)


SEL_TK = 512


SEL_ROWS = 64
SEL_NEAR = FAR_BIAS_DIST + Q_BLOCK


def _sel_kernel(q_ref, zg_ref, k_ref, v_ref, mt_ref, lt_ref, cq_ref, o_ref, m_s, l_s, acc_s, s_s, p_s, a_s):
    c = pl.program_id(2)
    q = q_ref[...] * (HEAD_DIM ** -0.5)
    qs = jnp.concatenate([q[:, h * LANE:(h + 1) * LANE] for h in range(NSA_HPG)], axis=0) + cq_ref[0]
    mt = mt_ref[0, 0]
    q_aug = jnp.concatenate([qs, jnp.concatenate([mt] * NSA_HPG, axis=0)], axis=1).astype(BF16)
    m_s[...] = jnp.full(m_s.shape, NEG_INF, F32)
    l_s[...] = jnp.zeros(l_s.shape, F32)
    acc_s[...] = jnp.zeros(acc_s.shape, F32)
    ncol = SEL_TK // LANE

    def tile(i, near):
        k0 = pl.multiple_of(i * SEL_TK, SEL_TK)
        s_s[...] = _dot_nt(q_aug, k_ref[0, 0, pl.ds(k0, SEL_TK), :])
        off = pl.multiple_of(SEL_NEAR - (c * Q_BLOCK - i * SEL_TK), LANE)
        for r in range(NSA_HPG * Q_BLOCK // SEL_ROWS):
            rows = slice(r * SEL_ROWS, (r + 1) * SEL_ROWS)
            s = s_s[rows, :]
            if near:
                s = s + lt_ref[0, rows, pl.ds(off, SEL_TK)]
            cols = [s[:, j * LANE:(j + 1) * LANE] for j in range(ncol)]
            mx = functools.reduce(jnp.maximum, cols)
            m_old = m_s[rows, :]
            m_new = jnp.maximum(m_old, jnp.max(mx, axis=-1, keepdims=True))
            alpha = jnp.exp(m_old - m_new)
            ps = [jnp.exp(col - m_new) for col in cols]
            l_s[rows, :] = alpha * l_s[rows, :] + jnp.sum(functools.reduce(jnp.add, ps), axis=-1, keepdims=True)
            m_s[rows, :] = m_new
            a_s[rows, :] = alpha
            p_s[rows, :] = jnp.concatenate(ps, axis=1).astype(BF16)
        acc_s[...] = a_s[...] * acc_s[...] + _dot(p_s[...], v_ref[0, 0, pl.ds(k0, SEL_TK), :])

    n_tiles = lax.div(c, SEL_TK // Q_BLOCK) + 1
    n_far = lax.div(jnp.maximum(c - 8, 0), SEL_TK // Q_BLOCK)
    lax.fori_loop(0, n_far, lambda i, carry: (tile(i, False), carry)[1], 0)
    lax.fori_loop(n_far, n_tiles, lambda i, carry: (tile(i, True), carry)[1], 0)
    o = acc_s[...] * (1.0 / l_s[...])
    _store_heads(o_ref, o, _sigmoid(zg_ref[...]), 1)


def _sel_branch(z, ks_aug, vs, mt, lt_tab, cq_tab, bsz, t):
    nq = t // Q_BLOCK
    nsel = mt.shape[-1]
    rows = NSA_HPG * Q_BLOCK
    return pl.pallas_call(
        _sel_kernel,
        grid=(bsz, NSA_GROUPS, nq),
        in_specs=[pl.BlockSpec((Q_BLOCK, NSA_HPG * LANE), lambda b, g, c: (b * nq + c, g)),
                  pl.BlockSpec((Q_BLOCK, LANE), lambda b, g, c: (b * nq + c, C_G3 // LANE + g)),
                  pl.BlockSpec((1, 1, t, 2 * LANE), lambda b, g, c: (b, g, 0, 0)),
                  pl.BlockSpec((1, 1, t, LANE), lambda b, g, c: (b, g, 0, 0)),
                  pl.BlockSpec((1, 1, Q_BLOCK, nsel), lambda b, g, c: (b, g, c, 0)),
                  pl.BlockSpec((1, rows, lt_tab.shape[2]), lambda b, g, c: (g, 0, 0)),
                  pl.BlockSpec((1, rows, LANE), lambda b, g, c: (g, 0, 0))],
        out_specs=pl.BlockSpec((1, NSA_HPG, Q_BLOCK, HEAD_DIM), lambda b, g, c: (b, g, c, 0)),
        out_shape=jax.ShapeDtypeStruct((bsz, NSA_HEADS, t, HEAD_DIM), F32),
        scratch_shapes=[pltpu.VMEM((rows, LANE), F32),
                        pltpu.VMEM((rows, LANE), F32),
                        pltpu.VMEM((rows, LANE), F32),
                        pltpu.VMEM((rows, SEL_TK), F32),
                        pltpu.VMEM((rows, SEL_TK), BF16),
                        pltpu.VMEM((rows, LANE), F32)],
        compiler_params=_cparams(("parallel", "parallel", "arbitrary")),
        name="nsa_selected",
    )(z, z, ks_aug, vs, mt, lt_tab, cq_tab)


def _shift_mix(z_ref, prev_ref, mu, first):
    z = z_ref[...]
    prev = jnp.where(first, 0.0, prev_ref[7:8, :])
    row = lax.broadcasted_iota(jnp.int32, z.shape, 0)
    zp = jnp.where(row == 0, prev, pltpu.roll(z, 1, 0))
    return z + mu * (zp - z)


def _rwkv_prep_kernel(has_vres, tiles_per_seq, *refs):
    (zr, zk, zv, zwa, zg, pr, pk, pv, pwa, pg, mu_ref, vec_ref, w2_ref, a2_ref, g2_ref, hsum_ref) = refs[:16]
    pos = 16
    if has_vres:
        vf_ref, v0_ref, v1_ref, v2_ref = refs[pos:pos + 4]
        pos += 4
    r_o, w_o, k_o, v_o, kk_o, b_o, g_o, bonus_o, vt_o = refs[pos:]
    first = lax.rem(pl.program_id(0), tiles_per_seq) == 0
    r = _shift_mix(zr, pr, mu_ref[0:1, :], first)
    k = _shift_mix(zk, pk, mu_ref[1:2, :], first)
    v = _shift_mix(zv, pv, mu_ref[2:3, :], first)
    wa = _shift_mix(zwa, pwa, mu_ref[3:4, 0:LANE], first)
    zg_s = _shift_mix(zg, pg, mu_ref[4:5, 0:LANE], first)
    w0, a0, k_k, k_a, r_k = (vec_ref[i:i + 1, :] for i in range(5))
    if has_vres:
        lora = _dot(_dot(v.astype(BF16), v1_ref[...]).astype(BF16), v2_ref[...])
        v = v + (vf_ref[...] - v) * _sigmoid(v0_ref[...] + lora)
    u = w0 + _dot(jnp.tanh(wa).astype(BF16), w2_ref[...])
    decay = jnp.exp(-math.exp(-0.5) * _sigmoid(u))
    a = _sigmoid(a0 + _dot(wa.astype(BF16), a2_ref[...]))
    g = _dot(_sigmoid(zg_s).astype(BF16), g2_ref[...])
    kk = k * k_k
    kk = kk / jnp.maximum(jnp.sqrt(_dot_split(kk * kk, hsum_ref[...])), 1e-12)
    k = k * (1.0 + (a - 1.0) * k_a)
    r_o[...] = r
    w_o[...] = decay
    k_o[...] = k
    v_o[...] = v
    kk_o[...] = kk
    b_o[...] = kk * a
    g_o[...] = g
    bonus_o[...] = _dot_split(r * k * r_k, hsum_ref[...]) * v
    vt_o[0] = v.T.astype(BF16)


def _rwkv_prep(z, mu5, vecs, w2p, a2p, g2, hsum, vres, bsz, t, tm=256):
    n = bsz * t
    nt = n // tm
    tps = t // tm
    w = RWKV_WIDTH

    def cur(width, col):
        return pl.BlockSpec((tm, width), lambda i: (i, col // width))

    def prev(width, col):
        return pl.BlockSpec((8, width), lambda i: (jnp.maximum(i * (tm // 8) - 1, 0), col // width))

    def full(shape):
        return pl.BlockSpec(shape, lambda i: (0,) * len(shape))

    cols = [(w, C_R), (w, C_K), (w, C_V), (LANE, C_WA), (LANE, C_ZG)]
    in_specs = [cur(*c) for c in cols] + [prev(*c) for c in cols]
    in_specs += [full(mu5.shape), full(vecs.shape), full(w2p.shape), full(a2p.shape), full(g2.shape), full(hsum.shape)]
    args = [z] * 10 + [mu5, vecs, w2p, a2p, g2, hsum]
    if vres is not None:
        v_first, v0, v1p, v2p = vres
        in_specs += [pl.BlockSpec((tm, w), lambda i: (i, 0)), full(v0.shape), full(v1p.shape), full(v2p.shape)]
        args += [v_first, v0, v1p, v2p]
    tok = pl.BlockSpec((tm, w), lambda i: (i, 0))
    out_specs = [tok] * 8 + [pl.BlockSpec((1, w, tm), lambda i: (i // tps, 0, i % tps))]
    out_shape = [jax.ShapeDtypeStruct((n, w), F32)] * 8 + [jax.ShapeDtypeStruct((bsz, w, t), BF16)]
    return pl.pallas_call(
        functools.partial(_rwkv_prep_kernel, vres is not None, tps),
        grid=(nt,),
        in_specs=in_specs,
        out_specs=out_specs,
        out_shape=out_shape,
        compiler_params=_cparams(("parallel",)),
        name="rwkv_prep",
    )(*args)


N_PAIRS = RWKV_HEADS // 2


def _scan_kernel(r_ref, w_ref, k_ref, kk_ref, b_ref, vt_ref, ones_ref, o_ref, s_ref, kbd_ref, vk_ref):
    bsz = r_ref.shape[0]
    tt = r_ref.shape[1]

    @pl.when(pl.program_id(0) == 0)
    def _():
        s_ref[...] = jnp.zeros(s_ref.shape, F32)

    lane = lax.broadcasted_iota(jnp.int32, (tt, LANE), 1)
    for bi in range(bsz):
        for p in range(N_PAIRS):
            kp = k_ref[bi, :, p * LANE:(p + 1) * LANE]
            kbd_ref[bi * N_PAIRS + p] = jnp.concatenate(
                [jnp.where(lane < RWKV_HEAD, kp, 0.0), jnp.where(lane >= RWKV_HEAD, kp, 0.0)], axis=0).astype(BF16)

    step_lane = lax.broadcasted_iota(jnp.int32, (RWKV_HEAD, 2 * tt), 1) & (tt - 1)
    row8 = lax.broadcasted_iota(jnp.int32, (16, LANE), 0)
    lane8 = lax.broadcasted_iota(jnp.int32, (16, LANE), 1)
    r_mask = ((row8 == 0) & (lane8 < RWKV_HEAD)) | ((row8 == 1) & (lane8 >= RWKV_HEAD))
    ones_bd = ones_ref[...]
    chains = [(bi, p) for bi in range(bsz) for p in range(N_PAIRS)]

    nh = RWKV_HEAD

    def group(tg, carry):
        t0 = pl.multiple_of(tg * 8, 8)
        rows = []
        for bi, p in chains:
            ln = slice(p * LANE, (p + 1) * LANE)
            rows.append(tuple(ref[bi, pl.ds(t0, 8), ln] for ref in (w_ref, kk_ref, b_ref, r_ref)))
        onehots = [jnp.where(step_lane == t0 + u, 1.0, 0.0).astype(BF16) for u in range(8)]
        for c, (bi, p) in enumerate(chains):
            vt_pair = jnp.concatenate([vt_ref[bi, p * LANE:p * LANE + nh, :],
                                       vt_ref[bi, p * LANE + nh:(p + 1) * LANE, :]], axis=1)
            vk_ref[c] = _dot(jnp.concatenate([vt_pair * oh for oh in onehots], axis=0), kbd_ref[c])
        state = [s_ref[c] for c in range(len(chains))]
        for u in range(8):
            sk = _dot(jnp.concatenate([(state[c] * rows[c][1][u:u + 1, :]).astype(BF16)
                                       for c in range(len(chains))], axis=0), ones_bd)
            for c, (bi, p) in enumerate(chains):
                w8, _, b8, r8 = rows[c]
                state[c] = (state[c] * w8[u:u + 1, :] - sk[c * nh:(c + 1) * nh, :] * b8[u:u + 1, :]
                            + vk_ref[c, u * nh:(u + 1) * nh, :])
                r_lhs = jnp.where(r_mask, r8[u:u + 1, :], 0.0).astype(BF16)
                o = _dot_nt(r_lhs, state[c].astype(BF16))
                o_ref[t0 + u, bi, 2 * p:2 * p + 2, :] = o[0:2, :]
        for c in range(len(chains)):
            s_ref[c] = state[c]
        return carry

    lax.fori_loop(0, tt // 8, group, 0)


def _rwkv_scan(r, w, k, kk, b, vt, ones_bd, bsz, t):
    tt = SCAN_TT
    wd = RWKV_WIDTH
    tok = pl.BlockSpec((bsz, tt, wd), lambda i: (0, i, 0))
    return pl.pallas_call(
        _scan_kernel,
        grid=(t // tt,),
        in_specs=[tok, tok, tok, tok, tok,
                  pl.BlockSpec((bsz, wd, tt), lambda i: (0, 0, i)),
                  pl.BlockSpec((LANE, LANE), lambda i: (0, 0))],
        out_specs=pl.BlockSpec((tt, bsz, RWKV_HEADS, RWKV_HEAD), lambda i: (i, 0, 0, 0)),
        out_shape=jax.ShapeDtypeStruct((t, bsz, RWKV_HEADS, RWKV_HEAD), F32),
        scratch_shapes=[pltpu.VMEM((bsz * N_PAIRS, RWKV_HEAD, LANE), F32),
                        pltpu.VMEM((bsz * N_PAIRS, 2 * tt, LANE), BF16),
                        pltpu.VMEM((bsz * N_PAIRS, 8 * RWKV_HEAD, LANE), F32)],
        compiler_params=_cparams(("arbitrary",)),
        name="rwkv_scan",
    )(r, w, k, kk, b, vt, ones_bd)


def _merge_kernel(x_ref, oc_ref, os_ref, ow_ref, orw_ref, bonus_ref, g_ref, zgn_ref, zgr_ref, gn_ref, hsum_ref,
                  pn_ref, pr_ref, wo_ref, ln_ref, o_ref):
    y_n = jnp.zeros(o_ref.shape, F32)
    for h in range(NSA_HEADS):
        o_h = (oc_ref[0, h] + os_ref[0, h] + ow_ref[0, h]).astype(BF16)
        y_n = y_n + _dot(o_h, pn_ref[h * HEAD_DIM:(h + 1) * HEAD_DIM, :])
    o = orw_ref[...]
    mu = _dot_split(o, hsum_ref[...]) * (1.0 / RWKV_HEAD)
    d = o - mu
    var = _dot_split(d * d, hsum_ref[...]) * (1.0 / RWKV_HEAD)
    o = d * lax.rsqrt(var + GN_EPS) * gn_ref[0:1, :] + gn_ref[1:2, :]
    o = ((o + bonus_ref[...]) * g_ref[...]).astype(BF16)
    y1 = _sigmoid(zgn_ref[...]) * y_n + _sigmoid(zgr_ref[...]) * _dot(o, pr_ref[...])
    y = _dot(y1.astype(BF16), wo_ref[...])
    o_ref[...] = _layer_norm(DN_ALPHA * x_ref[...] + y, ln_ref[0:1, :], ln_ref[1:2, :])


def _merge(x, o_c, o_s, o_w, o_rw, bonus, g, z, gn, hsum, p_nsa, p_rwkv, w_out, ln, bsz, t, tm=256):
    n = bsz * t
    tps = t // tm
    d = D_MODEL
    w = RWKV_WIDTH
    head = pl.BlockSpec((1, NSA_HEADS, tm, HEAD_DIM), lambda i: (i // tps, 0, i % tps, 0))
    tok = pl.BlockSpec((tm, w), lambda i: (i, 0))

    def full(a):
        return pl.BlockSpec(a.shape, lambda i: (0,) * a.ndim)

    return pl.pallas_call(
        _merge_kernel,
        grid=(n // tm,),
        in_specs=[pl.BlockSpec((tm, d), lambda i: (i, 0)), head, head, head, tok, tok, tok,
                  pl.BlockSpec((tm, d), lambda i: (i, C_GN // d)),
                  pl.BlockSpec((tm, d), lambda i: (i, C_GR // d)),
                  full(gn), full(hsum), full(p_nsa), full(p_rwkv), full(w_out), full(ln)],
        out_specs=pl.BlockSpec((tm, d), lambda i: (i, 0)),
        out_shape=jax.ShapeDtypeStruct((n, d), F32),
        compiler_params=_cparams(("parallel",)),
        name="mix_merge_ln",
    )(x, o_c, o_s, o_w, o_rw, bonus, g, z, z, gn, hsum, p_nsa, p_rwkv, w_out, ln)


def _xattn_kernel(x_ref, kt_ref, v_ref, wq_ref, wo_ref, ln_ref, rw_ref, o_ref, s_ref):
    x = x_ref[...]
    q = _dot(x.astype(BF16), wq_ref[...]).astype(BF16)
    outs = []
    for h in range(XATTN_HEADS):
        hs = slice(h * XATTN_HEAD, (h + 1) * XATTN_HEAD)
        s = _dot(q[:, hs], kt_ref[0, hs, :]) * (XATTN_HEAD ** -0.5)
        m = jnp.max(s, axis=-1, keepdims=True)
        p = jnp.exp(s - m)
        l = jnp.sum(p, axis=-1, keepdims=True)
        outs.append(_dot(p.astype(BF16), v_ref[0, :, hs]) * (1.0 / l))
    o = jnp.concatenate(outs, axis=1).astype(BF16)
    x2 = _layer_norm(DN_ALPHA * x + _dot(o, wo_ref[...]), ln_ref[0:1, :], ln_ref[1:2, :])
    o_ref[...] = x2
    hi = x2.astype(BF16)
    lo = (x2 - hi.astype(F32)).astype(BF16)
    logits = _dot(hi, rw_ref[0]) + (_dot(lo, rw_ref[0]) + _dot(hi, rw_ref[1]))
    s_ref[...] = _sigmoid(logits)


def _xattn(x, kt, v, wq, wo, ln, rw2, bsz, t, tm=256):
    n = bsz * t
    tps = t // tm
    d = D_MODEL

    def full(a):
        return pl.BlockSpec(a.shape, lambda i: (0,) * a.ndim)

    return pl.pallas_call(
        _xattn_kernel,
        grid=(n // tm,),
        in_specs=[pl.BlockSpec((tm, d), lambda i: (i, 0)),
                  pl.BlockSpec((1,) + kt.shape[1:], lambda i: (i // tps, 0, 0)),
                  pl.BlockSpec((1,) + v.shape[1:], lambda i: (i // tps, 0, 0)),
                  full(wq), full(wo), full(ln), full(rw2)],
        out_specs=[pl.BlockSpec((tm, d), lambda i: (i, 0)), pl.BlockSpec((tm, LANE), lambda i: (i, 0))],
        out_shape=[jax.ShapeDtypeStruct((n, d), F32), jax.ShapeDtypeStruct((n, LANE), F32)],
        compiler_params=_cparams(("parallel",)),
        name="xattn_ln_router",
    )(x, kt, v, wq, wo, ln, rw2)


def _expert_kernel(be_ref, x_ref, wg_ref, wu_ref, wd_ref, o_ref):
    x = x_ref[...]
    hg = _dot(x, wg_ref[0])
    h = hg * _sigmoid(hg) * _dot(x, wu_ref[0])
    o_ref[...] = _dot(h.astype(BF16), wd_ref[0]).astype(o_ref.dtype)


def _experts(block_expert, rows, w_gate, w_up, w_down):
    n_rows, d = rows.shape
    ff = w_gate.shape[2]
    return pl.pallas_call(
        _expert_kernel,
        grid_spec=pltpu.PrefetchScalarGridSpec(
            num_scalar_prefetch=1,
            grid=(n_rows // MOE_BLK,),
            in_specs=[pl.BlockSpec((MOE_BLK, d), lambda i, be: (i, 0)),
                      pl.BlockSpec((1, d, ff), lambda i, be: (be[i], 0, 0)),
                      pl.BlockSpec((1, d, ff), lambda i, be: (be[i], 0, 0)),
                      pl.BlockSpec((1, ff, d), lambda i, be: (be[i], 0, 0))],
            out_specs=pl.BlockSpec((MOE_BLK, d), lambda i, be: (i, 0))),
        out_shape=jax.ShapeDtypeStruct((n_rows, d), BF16),
        compiler_params=_cparams(("arbitrary",)),
        name="moe_experts",
    )(block_expert, rows, w_gate, w_up, w_down)


def _combine_kernel(x_ref, ya_ref, yb_ref, gate_ref, ln_ref, o_ref):
    y = gate_ref[:, 0:1] * ya_ref[...] + gate_ref[:, 1:2] * yb_ref[...]
    o_ref[...] = _layer_norm(DN_ALPHA * x_ref[...] + y, ln_ref[0:1, :], ln_ref[1:2, :])


def _combine(x, ya, yb, gate, ln, tm=512):
    n, d = x.shape
    tok = pl.BlockSpec((tm, d), lambda i: (i, 0))
    return pl.pallas_call(
        _combine_kernel,
        grid=(n // tm,),
        in_specs=[tok, tok, tok, pl.BlockSpec((tm, TOP_K), lambda i: (i, 0)),
                  pl.BlockSpec(ln.shape, lambda i: (0, 0))],
        out_specs=tok,
        out_shape=jax.ShapeDtypeStruct((n, d), F32),
        compiler_params=_cparams(("parallel",)),
        name="moe_combine_ln",
    )(x, ya, yb, gate, ln)


def _t5_bucket(dist):
    n = jnp.maximum(dist, 0)
    max_exact = REL_BUCKETS // 2
    log_ratio = jnp.log(jnp.maximum(n, 1).astype(F32) / max_exact) / math.log(REL_MAX_DIST / max_exact)
    large = jnp.minimum(max_exact + (log_ratio * (REL_BUCKETS - max_exact)).astype(jnp.int32), REL_BUCKETS - 1)
    return jnp.where(n < max_exact, n, large)


def _bias_tables(rel_table, t):
    nc = t // CMP_STRIDE
    f = rel_table[_t5_bucket(jnp.arange(t + 2048))].astype(F32).T
    ql = jnp.arange(Q_BLOCK)[:, None]

    def lookup(dist, ok):
        vals = jnp.take(f, jnp.clip(dist, 0, f.shape[-1] - 1), axis=1)
        vals = jnp.where(ok[None], vals, NEG_INF)
        return vals.reshape(NSA_GROUPS, NSA_HPG * dist.shape[0], dist.shape[1])

    j = jnp.arange(2 * nc)[None, :]
    dist_c = CMP_STRIDE * (nc - 8 - j) + ql - (CMP_LEN - 1)
    pc = lookup(dist_c, dist_c >= 0)
    nvar = WINDOW // Q_BLOCK + 1

    def toeplitz(base, width, hi, shift):
        period = width + Q_BLOCK
        m = np.arange(period)
        dist = base - np.where(m < width, m, m - period)
        row = jnp.take(f, np.clip(dist, 0, f.shape[-1] - 1), axis=1) - shift
        row = jnp.where(((dist >= 0) & (dist < hi))[None], row, NEG_INF)
        flat = jnp.tile(row, (1, Q_BLOCK))[:, :Q_BLOCK * (period - 1)]
        return flat.reshape(NSA_HEADS, Q_BLOCK, period - 1)[:, :, :width]

    wb = jnp.stack([toeplitz(Q_BLOCK * v, WINDOW + Q_BLOCK, WINDOW, 0.0) for v in range(nvar)], axis=1)
    wb = wb.reshape(NSA_GROUPS, NSA_HPG, nvar, Q_BLOCK, -1).transpose(0, 2, 1, 3, 4)
    wb = wb.reshape(NSA_GROUPS, nvar, NSA_HPG * Q_BLOCK, -1)
    far = rel_table[REL_BUCKETS - 1].astype(F32)
    lt = toeplitz(SEL_NEAR, SEL_NEAR + SEL_TK, f.shape[-1], far[:, None])
    lt = lt.reshape(NSA_GROUPS, NSA_HPG * Q_BLOCK, -1)
    far_hi = far.astype(BF16).astype(F32)
    lane = jnp.arange(LANE)[None, :]
    cq = jnp.where(lane == HEAD_DIM, far_hi[:, None], jnp.where(lane == HEAD_DIM + 1, (far - far_hi)[:, None], 0.0))
    cq = jnp.broadcast_to(cq[:, None, :], (NSA_HEADS, Q_BLOCK, LANE)).reshape(NSA_GROUPS, NSA_HPG * Q_BLOCK, LANE)
    return pc, wb, lt, cq


def _in_proj_perm():
    perm = np.full((IN_PAD,), -1, np.int64)
    for h in range(NSA_HEADS):
        perm[C_Q + h * LANE:C_Q + h * LANE + HEAD_DIM] = np.arange(HEAD_DIM) + h * HEAD_DIM
    rw0 = NSA_COLS
    g0 = NSA_COLS + RWKV_COLS
    perm[C_GN:C_GN + D_MODEL] = g0 + np.arange(D_MODEL)
    perm[C_GR:C_GR + D_MODEL] = g0 + D_MODEL + np.arange(D_MODEL)
    perm[C_R:C_R + RWKV_WIDTH] = rw0 + np.arange(RWKV_WIDTH)
    perm[C_K:C_K + RWKV_WIDTH] = rw0 + RWKV_WIDTH + np.arange(RWKV_WIDTH)
    perm[C_V:C_V + RWKV_WIDTH] = rw0 + 2 * RWKV_WIDTH + np.arange(RWKV_WIDTH)
    perm[C_KV:C_KV + 6 * NSA_KV] = NSA_WIDTH + np.arange(6 * NSA_KV)
    for g in range(NSA_GROUPS):
        for br in range(3):
            for h in range(NSA_HPG):
                perm[C_G3 + g * LANE + br * NSA_HPG + h] = NSA_WIDTH + 6 * NSA_KV + (g * NSA_HPG + h) * 3 + br
    perm[C_WA:C_WA + LORA_W + LORA_A] = rw0 + 3 * RWKV_WIDTH + np.arange(LORA_W + LORA_A)
    perm[C_ZG:C_ZG + LORA_G] = rw0 + 3 * RWKV_WIDTH + LORA_W + LORA_A + np.arange(LORA_G)
    return perm


_PERM = _in_proj_perm()


def _permute_in_proj(w):
    pieces = []
    start = 0
    while start < IN_PAD:
        stop = start + 1
        if _PERM[start] < 0:
            while stop < IN_PAD and _PERM[stop] < 0:
                stop += 1
            pieces.append(jnp.zeros((w.shape[0], stop - start), BF16))
        else:
            while stop < IN_PAD and _PERM[stop] == _PERM[stop - 1] + 1:
                stop += 1
            pieces.append(w[:, int(_PERM[start]):int(_PERM[stop - 1]) + 1].astype(BF16))
        start = stop
    return jnp.concatenate(pieces, axis=1)


def _pad_to(a, shape):
    return jnp.pad(a, [(0, s - d) for d, s in zip(a.shape, shape)])


def _route(s, router_bias, n_tok):
    s16 = s[:, :N_EXPERTS]
    s_sel = s16 + router_bias.astype(F32)

    def top2(a):
        i1 = jnp.argmax(a, axis=-1)
        rest = jnp.where(jnp.arange(a.shape[-1]) == i1[..., None], -jnp.inf, a)
        i2 = jnp.argmax(rest, axis=-1)
        return jnp.max(a, axis=-1) + jnp.max(rest, axis=-1), jnp.stack([i1, i2], axis=-1)

    group_score, _ = top2(s_sel.reshape(n_tok, N_EXPERT_GROUPS, EXPERTS_PER_GROUP))
    group = jnp.argmax(group_score, axis=-1)
    in_group = (jnp.arange(N_EXPERTS) // EXPERTS_PER_GROUP)[None, :] == group[:, None]
    _, e_idx = top2(jnp.where(in_group, s_sel, NEG_INF))
    s_top = jnp.take_along_axis(s16, e_idx, axis=-1)
    gate = s_top / jnp.sum(s_top, axis=-1, keepdims=True)
    e_flat = e_idx.reshape(-1)
    onehot = (e_flat[:, None] == jnp.arange(N_EXPERTS)[None, :]).astype(jnp.int32)
    rank = jnp.take_along_axis(jnp.cumsum(onehot, axis=0) - onehot, e_flat[:, None], axis=1)[:, 0]
    counts = jnp.sum(onehot, axis=0)
    padded = (counts + MOE_BLK - 1) // MOE_BLK * MOE_BLK
    ends = jnp.cumsum(padded)
    dest = (ends - padded)[e_flat] + rank
    n_rows = n_tok * TOP_K + N_EXPERTS * MOE_BLK
    tok_of_row = jnp.zeros((n_rows,), jnp.int32).at[dest].set(jnp.arange(n_tok * TOP_K, dtype=jnp.int32) // TOP_K)
    block_expert = jnp.minimum(
        jnp.searchsorted(ends, jnp.arange(n_rows // MOE_BLK) * MOE_BLK, side='right'), N_EXPERTS - 1).astype(jnp.int32)
    return gate, dest.reshape(n_tok, TOP_K), tok_of_row, block_expert


def kernel(x, mem, rel_table, router_w, router_bias, w_in, cmp_pe_k, cmp_pe_v, cmp_w1k, cmp_w2k, cmp_w1v, cmp_w2v, rwkv_mu, rwkv_w0, rwkv_w2, rwkv_a0, rwkv_a2, rwkv_g2, rwkv_kk, rwkv_ka, rwkv_rk, rwkv_gn_g, rwkv_gn_b, rwkv_v0, rwkv_v1, rwkv_v2, p_nsa, p_rwkv, w_out, ln1_g, ln1_b, xq_w, xk_w, xv_w, xo_w, ln2_g, ln2_b, moe_w_gate, moe_w_up, moe_w_down, ln3_g, ln3_b):
    bsz, t, d = x.shape
    n = bsz * t
    depth = w_in.shape[0]
    nc = t // CMP_STRIDE
    nsel = t // SEL_BLOCK
    g, hd = NSA_GROUPS, HEAD_DIM

    pc_tab, wb_tab, lt_tab, cq_tab = _bias_tables(rel_table, t)
    cmp_start = np.arange(nc) * CMP_STRIDE
    sel_start = np.arange(nsel) * SEL_BLOCK
    overlap = ((cmp_start[:, None] <= sel_start[None, :] + SEL_BLOCK - 1)
               & (cmp_start[:, None] + CMP_LEN - 1 >= sel_start[None, :]) & (cmp_start[:, None] < (nc - 1) * CMP_STRIDE))
    overlap = jnp.asarray(overlap.T, BF16)
    key_tail = np.zeros((t, 2 * LANE - HEAD_DIM), np.float32)
    key_tail[:, 0:2] = 1.0
    key_tail[np.arange(t), LANE - HEAD_DIM + np.arange(t) // SEL_BLOCK] = 1.0
    key_tail = jnp.asarray(key_tail, BF16)
    lane_head = np.arange(RWKV_WIDTH) // RWKV_HEAD
    hsum = jnp.asarray(lane_head[:, None] == lane_head[None, :], BF16)
    ones_bd = jnp.asarray((np.arange(LANE) // RWKV_HEAD)[:, None] == (np.arange(LANE) // RWKV_HEAD)[None, :], BF16)
    rw_f = _pad_to(router_w.astype(F32), (d, LANE))
    rw_hi = rw_f.astype(BF16)
    rw2 = jnp.stack([rw_hi, (rw_f - rw_hi.astype(F32)).astype(BF16)])

    xf = x.reshape(n, d)
    mem_f = mem.reshape(bsz * mem.shape[1], d)
    v_first = None
    for l in range(depth):
        w_in_l = _permute_in_proj(w_in[l])
        z = _matmul(xf, w_in_l, F32, 512, 1024)

        kv = z[:, C_KV:C_KV + 6 * NSA_KV].reshape(bsz, t, 6, g, hd)
        kc, vc, ks, vs, kw, vw = (kv[:, :, i] for i in range(6))

        def chunks(a):
            return a.reshape(bsz, nc, CMP_STRIDE, g, hd).transpose(0, 3, 1, 2, 4).reshape(bsz, g, nc, CMP_STRIDE * hd)

        def keys(a):
            return _pad_to(a.transpose(0, 2, 1, 3), (bsz, g, t, LANE)).astype(BF16)

        def cmp_w(pe, w1, w2):
            return (pe.reshape(2, CMP_STRIDE * hd), w1.reshape(2, CMP_STRIDE * hd, CMP_HIDDEN).astype(BF16),
                    _pad_to(w2, (CMP_HIDDEN, LANE)).astype(BF16))

        k_cmp = _compress(chunks(kc), *cmp_w(cmp_pe_k[l], cmp_w1k[l], cmp_w2k[l]))
        v_cmp = _compress(chunks(vc), *cmp_w(cmp_pe_v[l], cmp_w1v[l], cmp_w2v[l]))
        o_c, sel_mask = _cmp_branch(z, k_cmp, v_cmp, pc_tab, overlap, bsz, t)
        ks_aug = jnp.concatenate([ks.transpose(0, 2, 1, 3).astype(BF16),
                                  jnp.broadcast_to(key_tail, (bsz, g, t, key_tail.shape[1]))], axis=-1)
        o_s = _sel_branch(z, ks_aug, keys(vs), sel_mask, lt_tab, cq_tab, bsz, t)
        o_w = _win_branch(z, keys(kw), keys(vw), wb_tab, bsz, t)

        mu = rwkv_mu[l]
        w3 = 3 * RWKV_WIDTH
        mu5 = jnp.stack([mu[0:512], mu[512:1024], mu[1024:1536], _pad_to(mu[w3:w3 + 128], (512,)),
                         _pad_to(mu[w3 + 128:w3 + 256], (512,))])
        vecs = jnp.stack([rwkv_w0[l], rwkv_a0[l], rwkv_kk[l], rwkv_ka[l], rwkv_rk[l].reshape(-1)])
        w2p = jnp.concatenate([rwkv_w2[l], jnp.zeros_like(rwkv_a2[l])], axis=0).astype(BF16)
        a2p = jnp.concatenate([jnp.zeros_like(rwkv_w2[l]), rwkv_a2[l]], axis=0).astype(BF16)
        vres = None
        if l > 0:
            vres = (v_first, rwkv_v0[l - 1][None, :], _pad_to(rwkv_v1[l - 1], (RWKV_WIDTH, LANE)).astype(BF16),
                    _pad_to(rwkv_v2[l - 1], (LANE, RWKV_WIDTH)).astype(BF16))
        r, w, k, v, kk, b, g_out, bonus, vt = _rwkv_prep(z, mu5, vecs, w2p, a2p, rwkv_g2[l].astype(BF16), hsum, vres,
                                                          bsz, t)
        if l == 0:
            v_first = v
        seq = lambda a: a.reshape(bsz, t, RWKV_WIDTH)
        o_rw = _rwkv_scan(seq(r), seq(w), seq(k), seq(kk), seq(b), vt, ones_bd, bsz, t)
        o_rw = o_rw.transpose(1, 0, 2, 3).reshape(n, RWKV_WIDTH)

        xf = _merge(xf, o_c, o_s, o_w, o_rw, bonus, g_out, z, jnp.stack([rwkv_gn_g[l], rwkv_gn_b[l]]), hsum,
                    p_nsa[l].astype(BF16), p_rwkv[l].astype(BF16), w_out[l].astype(BF16),
                    jnp.stack([ln1_g[l], ln1_b[l]]), bsz, t)

        mlen = mem.shape[1]
        mk = _matmul(mem_f, xk_w[l].astype(BF16), BF16, mlen, XATTN_WIDTH)
        mv = _matmul(mem_f, xv_w[l].astype(BF16), BF16, mlen, XATTN_WIDTH)
        kt = mk.reshape(bsz, mlen, XATTN_WIDTH).transpose(0, 2, 1)
        xf, scores = _xattn(xf, kt, mv.reshape(bsz, mlen, XATTN_WIDTH), xq_w[l].astype(BF16), xo_w[l].astype(BF16),
                            jnp.stack([ln2_g[l], ln2_b[l]]), rw2, bsz, t)

        gate, dest, tok_of_row, block_expert = _route(scores, router_bias, n)
        rows = jnp.take(xf.astype(BF16), tok_of_row, axis=0)
        y_rows = _experts(block_expert, rows, moe_w_gate[l].astype(BF16), moe_w_up[l].astype(BF16),
                          moe_w_down[l].astype(BF16))
        xf = _combine(xf, jnp.take(y_rows, dest[:, 0], axis=0), jnp.take(y_rows, dest[:, 1], axis=0), gate,
                      jnp.stack([ln3_g[l], ln3_b[l]]))
    return xf.reshape(bsz, t, d)
```

```python
import functools
import math

import numpy as np
import jax
import jax.numpy as jnp
from jax import lax
from jax.experimental import pallas as pl
from jax.experimental.pallas import tpu as pltpu

F32 = jnp.float32
BF16 = jnp.bfloat16

D_MODEL = 1024
DEPTH = 2
NSA_HEADS = 8
NSA_GROUPS = 2
NSA_HPG = NSA_HEADS // NSA_GROUPS
HEAD_DIM = 64
NSA_WIDTH = NSA_HEADS * HEAD_DIM
NSA_KV = NSA_GROUPS * HEAD_DIM
CMP_STRIDE = 16
CMP_LEN = 2 * CMP_STRIDE
CMP_HIDDEN = 256
SEL_BLOCK = 64
SEL_TOPN = 16
WINDOW = 512
Q_BLOCK = 128
SEL_FORCE = 1e9
RWKV_HEADS = 8
RWKV_HEAD = 64
RWKV_WIDTH = RWKV_HEADS * RWKV_HEAD
LORA_W = 64
LORA_A = 64
LORA_V = 32
LORA_G = 128
GN_EPS = 64e-5
REL_BUCKETS = 32
REL_MAX_DIST = 1024
XATTN_HEADS = 4
XATTN_HEAD = 128
XATTN_WIDTH = XATTN_HEADS * XATTN_HEAD
N_EXPERTS = 16
N_EXPERT_GROUPS = 4
EXPERTS_PER_GROUP = N_EXPERTS // N_EXPERT_GROUPS
TOP_K = 2
EXPERT_FF = 512
DN_ALPHA = (2 * DEPTH) ** 0.25
LN_EPS = 1e-5
NEG_INF = -1e30
NSA_COLS = NSA_WIDTH + 6 * NSA_KV + 3 * NSA_HEADS
RWKV_COLS = 3 * RWKV_WIDTH + LORA_W + LORA_A + LORA_G

LANE = 128
VMEM_LIMIT = 56 * 1024 * 1024

C_Q = 0
C_GN = 1024
C_GR = 2048
C_R = 3072
C_K = 3584
C_V = 4096
C_KV = 4608
C_G3 = 5376
C_WA = 5632
C_ZG = 5760
IN_PAD = 6144

MOE_BLK = 256
SCAN_TT = 128
FAR_BIAS_DIST = 1280


def _cparams(sem):
    return pltpu.CompilerParams(dimension_semantics=sem, vmem_limit_bytes=VMEM_LIMIT)


def _sigmoid(x):
    return 1.0 / (1.0 + jnp.exp(-x))


def _layer_norm(v, g, b):
    mu = jnp.mean(v, axis=-1, keepdims=True)
    d = v - mu
    var = jnp.mean(d * d, axis=-1, keepdims=True)
    return d * lax.rsqrt(var + LN_EPS) * g + b


def _dot(a, b):
    return jnp.dot(a, b, preferred_element_type=F32)


def _dot_nt(a, b):
    return lax.dot_general(a, b, (((1,), (1,)), ((), ())), preferred_element_type=F32)


def _dot_split(x, w):
    hi = x.astype(BF16)
    lo = (x - hi.astype(F32)).astype(BF16)
    return _dot(hi, w) + _dot(lo, w)


def _mm_kernel(x_ref, w_ref, o_ref, xb_ref):
    @pl.when(pl.program_id(1) == 0)
    def _():
        xb_ref[...] = x_ref[...].astype(BF16)

    o_ref[...] = _dot(xb_ref[...], w_ref[...]).astype(o_ref.dtype)


def _matmul(x, w, out_dtype, tm, tn):
    n, k = x.shape
    m = w.shape[1]
    return pl.pallas_call(
        _mm_kernel,
        grid=(n // tm, m // tn),
        in_specs=[pl.BlockSpec((tm, k), lambda i, j: (i, 0)),
                  pl.BlockSpec((k, tn), lambda i, j: (0, j))],
        out_specs=pl.BlockSpec((tm, tn), lambda i, j: (i, j)),
        out_shape=jax.ShapeDtypeStruct((n, m), out_dtype),
        scratch_shapes=[pltpu.VMEM((tm, k), BF16)],
        compiler_params=_cparams(("parallel", "arbitrary")),
        name="matmul",
    )(x, w)


def _compress_kernel(u_ref, pe_ref, w1_ref, w2_ref, o_ref):
    u = u_ref[0, 0]
    a = _dot((u + pe_ref[0:1, :]).astype(BF16), w1_ref[0])
    b = _dot((u + pe_ref[1:2, :]).astype(BF16), w1_ref[1])
    nc = u.shape[0]
    h = a + pltpu.roll(b, nc - 1, 0)
    h = 0.5 * h * (1.0 + jnp.tanh(math.sqrt(2.0 / math.pi) * (h + 0.044715 * (h * h * h))))
    y = _dot(h.astype(BF16), w2_ref[...])
    row = lax.broadcasted_iota(jnp.int32, y.shape, 0)
    o_ref[0, 0] = jnp.where(row < nc - 1, y, 0.0)


def _compress(u, pe2, w1, w2p):
    b, g, nc, _ = u.shape
    return pl.pallas_call(
        _compress_kernel,
        grid=(b, g),
        in_specs=[pl.BlockSpec((1, 1, nc, 1024), lambda i, j: (i, j, 0, 0)),
                  pl.BlockSpec((2, 1024), lambda i, j: (0, 0)),
                  pl.BlockSpec((2, 1024, CMP_HIDDEN), lambda i, j: (0, 0, 0)),
                  pl.BlockSpec((CMP_HIDDEN, LANE), lambda i, j: (0, 0))],
        out_specs=pl.BlockSpec((1, 1, nc, LANE), lambda i, j: (i, j, 0, 0)),
        out_shape=jax.ShapeDtypeStruct((b, g, nc, LANE), F32),
        compiler_params=_cparams(("parallel", "parallel")),
        name="nsa_compress",
    )(u, pe2, w1, w2p)


def _stack_heads(q_ref):
    q = q_ref[...] * (HEAD_DIM ** -0.5)
    return jnp.concatenate([q[:, h * LANE:(h + 1) * LANE] for h in range(NSA_HPG)], axis=0).astype(BF16)


def _store_heads(o_ref, o, gates, branch):
    for h in range(NSA_HPG):
        c = branch * NSA_HPG + h
        o_ref[0, h] = (o[h * Q_BLOCK:(h + 1) * Q_BLOCK, :] * gates[:, c:c + 1])[:, :HEAD_DIM]


def _cmp_kernel(q_ref, zg_ref, kc_ref, vc_ref, pc_ref, ov_ref, o_ref, m_ref):
    c = pl.program_id(2)
    nc = kc_ref.shape[2]
    nsel = ov_ref.shape[0]
    qs = _stack_heads(q_ref)
    s = _dot_nt(qs, kc_ref[0, 0].astype(BF16))
    off = (nc - 8) - 8 * c
    bias = pltpu.roll(pc_ref[0], (2 * nc - off) % (2 * nc), 1)[:, :nc]
    s = s + bias
    m = jnp.max(s, axis=-1, keepdims=True)
    p = jnp.exp(s - m)
    l = jnp.sum(p, axis=-1, keepdims=True)
    ql = lax.broadcasted_iota(jnp.int32, (NSA_HPG * Q_BLOCK, 1), 0) & (Q_BLOCK - 1)
    any_key = (c * Q_BLOCK + ql) >= (CMP_LEN - 1)
    p = p * jnp.where(any_key, 1.0 / l, 0.0)
    gates = _sigmoid(zg_ref[...])
    _store_heads(o_ref, _dot(p.astype(BF16), vc_ref[0, 0].astype(BF16)), gates, 0)

    psum = p[0:Q_BLOCK] + p[Q_BLOCK:2 * Q_BLOCK] + p[2 * Q_BLOCK:3 * Q_BLOCK] + p[3 * Q_BLOCK:4 * Q_BLOCK]
    hi = psum.astype(BF16)
    lo = (psum - hi.astype(F32)).astype(BF16)
    imp = _dot_nt(ov_ref[...], hi) + _dot_nt(ov_ref[...], lo)
    jb = lax.broadcasted_iota(jnp.int32, (nsel, Q_BLOCK), 0)
    tq = c * Q_BLOCK + lax.broadcasted_iota(jnp.int32, (nsel, Q_BLOCK), 1)
    jq = jnp.right_shift(tq, int(math.log2(SEL_BLOCK)))
    forced = (jb == 0) | (jb == jq) | (jb == jq - 1)
    val = jnp.where(forced, SEL_FORCE, jnp.where(jb <= jq, imp, -SEL_FORCE))
    sel = jnp.zeros((nsel, Q_BLOCK), F32)
    jbf = jb.astype(F32)
    for _ in range(SEL_TOPN):
        mx = jnp.max(val, axis=0, keepdims=True)
        first = jnp.min(jnp.where(val == mx, jbf, float(nsel)), axis=0, keepdims=True)
        hit = jbf == first
        sel = jnp.where(hit, jnp.where(mx >= 0.0, 1.0, 0.0), sel)
        val = jnp.where(hit, -3e38, val)
    m_ref[0, 0] = (sel.T - 1.0) * (-NEG_INF)


def _cmp_branch(z, k_cmp, v_cmp, pc_tab, overlap, bsz, t):
    nq = t // Q_BLOCK
    nc = k_cmp.shape[2]
    nsel = overlap.shape[0]
    return pl.pallas_call(
        _cmp_kernel,
        grid=(bsz, NSA_GROUPS, nq),
        in_specs=[pl.BlockSpec((Q_BLOCK, NSA_HPG * LANE), lambda b, g, c: (b * nq + c, g)),
                  pl.BlockSpec((Q_BLOCK, LANE), lambda b, g, c: (b * nq + c, C_G3 // LANE + g)),
                  pl.BlockSpec((1, 1, nc, LANE), lambda b, g, c: (b, g, 0, 0)),
                  pl.BlockSpec((1, 1, nc, LANE), lambda b, g, c: (b, g, 0, 0)),
                  pl.BlockSpec((1, NSA_HPG * Q_BLOCK, 2 * nc), lambda b, g, c: (g, 0, 0)),
                  pl.BlockSpec((nsel, nc), lambda b, g, c: (0, 0))],
        out_specs=[pl.BlockSpec((1, NSA_HPG, Q_BLOCK, HEAD_DIM), lambda b, g, c: (b, g, c, 0)),
                   pl.BlockSpec((1, 1, Q_BLOCK, nsel), lambda b, g, c: (b, g, c, 0))],
        out_shape=[jax.ShapeDtypeStruct((bsz, NSA_HEADS, t, HEAD_DIM), F32),
                   jax.ShapeDtypeStruct((bsz, NSA_GROUPS, t, nsel), F32)],
        compiler_params=_cparams(("parallel", "parallel", "arbitrary")),
        name="nsa_cmp_select",
    )(z, z, k_cmp, v_cmp, pc_tab, overlap)


def _win_kernel(q_ref, zg_ref, k_ref, v_ref, wb_ref, o_ref, s_s, p_s):
    c = pl.program_id(2)
    start = pl.multiple_of(jnp.maximum(c * Q_BLOCK - WINDOW, 0), Q_BLOCK)
    qs = _stack_heads(q_ref)
    s_s[...] = _dot_nt(qs, k_ref[0, 0, pl.ds(start, WINDOW + Q_BLOCK), :])
    for r in range(NSA_HPG * Q_BLOCK // SEL_ROWS):
        rows = slice(r * SEL_ROWS, (r + 1) * SEL_ROWS)
        s = s_s[rows, :] + wb_ref[0, 0, rows, :]
        p_s[rows, :] = jnp.exp(s - jnp.max(s, axis=-1, keepdims=True)).astype(BF16)
    o = _dot(p_s[...], v_ref[0, 0, pl.ds(start, WINDOW + Q_BLOCK), :])
    o = o * (1.0 / o[:, HEAD_DIM:HEAD_DIM + 1])
    _store_heads(o_ref, o, _sigmoid(zg_ref[...]), 2)


def _win_branch(z, kw, vw, wb_tab, bsz, t):
    nq = t // Q_BLOCK
    nvar = wb_tab.shape[1]
    return pl.pallas_call(
        _win_kernel,
        grid=(bsz, NSA_GROUPS, nq),
        in_specs=[pl.BlockSpec((Q_BLOCK, NSA_HPG * LANE), lambda b, g, c: (b * nq + c, g)),
                  pl.BlockSpec((Q_BLOCK, LANE), lambda b, g, c: (b * nq + c, C_G3 // LANE + g)),
                  pl.BlockSpec((1, 1, t, LANE), lambda b, g, c: (b, g, 0, 0)),
                  pl.BlockSpec((1, 1, t, LANE), lambda b, g, c: (b, g, 0, 0)),
                  pl.BlockSpec((1, 1, NSA_HPG * Q_BLOCK, WINDOW + Q_BLOCK),
                               lambda b, g, c: (g, jnp.minimum(c, nvar - 1), 0, 0))],
        out_specs=pl.BlockSpec((1, NSA_HPG, Q_BLOCK, HEAD_DIM), lambda b, g, c: (b, g, c, 0)),
        out_shape=jax.ShapeDtypeStruct((bsz, NSA_HEADS, t, HEAD_DIM), F32),
        scratch_shapes=[pltpu.VMEM((NSA_HPG * Q_BLOCK, WINDOW + Q_BLOCK), F32),
                        pltpu.VMEM((NSA_HPG * Q_BLOCK, WINDOW + Q_BLOCK), BF16)],
        compiler_params=_cparams(("parallel", "parallel", "arbitrary")),
        name="nsa_window",
    )(z, z, kw, vw, wb_tab)


SEL_TK = 512


SEL_ROWS = 64
SEL_NEAR = FAR_BIAS_DIST + Q_BLOCK


def _sel_kernel(q_ref, zg_ref, k_ref, v_ref, mt_ref, lt_ref, cq_ref, o_ref, m_s, acc_s, s0_s, s1_s, p_s, a_s):
    c = pl.program_id(2)
    q = q_ref[...] * (HEAD_DIM ** -0.5)
    qs = jnp.concatenate([q[:, h * LANE:(h + 1) * LANE] for h in range(NSA_HPG)], axis=0) + cq_ref[0]
    mt = mt_ref[0, 0]
    q_aug = jnp.concatenate([qs, jnp.concatenate([mt] * NSA_HPG, axis=0)], axis=1).astype(BF16)
    m_s[...] = jnp.full(m_s.shape, NEG_INF, F32)
    acc_s[...] = jnp.zeros(acc_s.shape, F32)
    ncol = SEL_TK // LANE
    n_tiles = lax.div(c, SEL_TK // Q_BLOCK) + 1

    def scores(i, dst):
        i = jnp.minimum(i, n_tiles - 1)
        dst[...] = _dot_nt(q_aug, k_ref[0, 0, pl.ds(pl.multiple_of(i * SEL_TK, SEL_TK), SEL_TK), :])

    def tile(i, cur, nxt):
        scores(i + 1, nxt)
        off = pl.multiple_of(jnp.maximum(SEL_NEAR + SEL_TK - (c * Q_BLOCK - i * SEL_TK), 0), LANE)
        for r in range(NSA_HPG * Q_BLOCK // SEL_ROWS):
            rows = slice(r * SEL_ROWS, (r + 1) * SEL_ROWS)
            s = cur[rows, :] + lt_ref[0, rows, pl.ds(off, SEL_TK)]
            cols = [s[:, j * LANE:(j + 1) * LANE] for j in range(ncol)]
            mx = functools.reduce(jnp.maximum, cols)
            m_old = m_s[rows, :]
            m_new = jnp.maximum(m_old, jnp.max(mx, axis=-1, keepdims=True))
            m_s[rows, :] = m_new
            a_s[rows, :] = jnp.exp(m_old - m_new)
            p_s[rows, :] = jnp.concatenate([jnp.exp(col - m_new) for col in cols], axis=1).astype(BF16)
        k0 = pl.multiple_of(i * SEL_TK, SEL_TK)
        acc_s[...] = a_s[...] * acc_s[...] + _dot(p_s[...], v_ref[0, 0, pl.ds(k0, SEL_TK), :])

    scores(0, s0_s)

    def pair(j, carry):
        tile(2 * j, s0_s, s1_s)

        @pl.when(2 * j + 1 < n_tiles)
        def _():
            tile(2 * j + 1, s1_s, s0_s)

        return carry

    lax.fori_loop(0, lax.div(n_tiles + 1, 2), pair, 0)
    acc = acc_s[...]
    o = acc * (1.0 / acc[:, HEAD_DIM:HEAD_DIM + 1])
    _store_heads(o_ref, o, _sigmoid(zg_ref[...]), 1)


def _sel_branch(z, ks_aug, vs, mt, lt_tab, cq_tab, bsz, t):
    nq = t // Q_BLOCK
    nsel = mt.shape[-1]
    rows = NSA_HPG * Q_BLOCK
    return pl.pallas_call(
        _sel_kernel,
        grid=(bsz, NSA_GROUPS, nq),
        in_specs=[pl.BlockSpec((Q_BLOCK, NSA_HPG * LANE), lambda b, g, c: (b * nq + c, g)),
                  pl.BlockSpec((Q_BLOCK, LANE), lambda b, g, c: (b * nq + c, C_G3 // LANE + g)),
                  pl.BlockSpec((1, 1, t, 2 * LANE), lambda b, g, c: (b, g, 0, 0)),
                  pl.BlockSpec((1, 1, t, LANE), lambda b, g, c: (b, g, 0, 0)),
                  pl.BlockSpec((1, 1, Q_BLOCK, nsel), lambda b, g, c: (b, g, c, 0)),
                  pl.BlockSpec((1, rows, lt_tab.shape[2]), lambda b, g, c: (g, 0, 0)),
                  pl.BlockSpec((1, rows, LANE), lambda b, g, c: (g, 0, 0))],
        out_specs=pl.BlockSpec((1, NSA_HPG, Q_BLOCK, HEAD_DIM), lambda b, g, c: (b, g, c, 0)),
        out_shape=jax.ShapeDtypeStruct((bsz, NSA_HEADS, t, HEAD_DIM), F32),
        scratch_shapes=[pltpu.VMEM((rows, LANE), F32),
                        pltpu.VMEM((rows, LANE), F32),
                        pltpu.VMEM((rows, SEL_TK), F32),
                        pltpu.VMEM((rows, SEL_TK), F32),
                        pltpu.VMEM((rows, SEL_TK), BF16),
                        pltpu.VMEM((rows, LANE), F32)],
        compiler_params=_cparams(("parallel", "parallel", "arbitrary")),
        name="nsa_selected",
    )(z, z, ks_aug, vs, mt, lt_tab, cq_tab)


def _shift_mix(z_ref, prev_ref, mu, first):
    z = z_ref[...]
    prev = jnp.where(first, 0.0, prev_ref[7:8, :])
    row = lax.broadcasted_iota(jnp.int32, z.shape, 0)
    zp = jnp.where(row == 0, prev, pltpu.roll(z, 1, 0))
    return z + mu * (zp - z)


def _rwkv_prep_kernel(has_vres, tiles_per_seq, *refs):
    (zr, zk, zv, zwa, zg, pr, pk, pv, pwa, pg, mu_ref, vec_ref, w2_ref, a2_ref, g2_ref, hsum_ref) = refs[:16]
    pos = 16
    if has_vres:
        vf_ref, v0_ref, v1_ref, v2_ref = refs[pos:pos + 4]
        pos += 4
    r_o, w_o, k_o, v_o, kk_o, b_o, g_o, bonus_o, vt_o = refs[pos:]
    first = lax.rem(pl.program_id(0), tiles_per_seq) == 0
    r = _shift_mix(zr, pr, mu_ref[0:1, :], first)
    k = _shift_mix(zk, pk, mu_ref[1:2, :], first)
    v = _shift_mix(zv, pv, mu_ref[2:3, :], first)
    wa = _shift_mix(zwa, pwa, mu_ref[3:4, 0:LANE], first)
    zg_s = _shift_mix(zg, pg, mu_ref[4:5, 0:LANE], first)
    w0, a0, k_k, k_a, r_k = (vec_ref[i:i + 1, :] for i in range(5))
    if has_vres:
        lora = _dot(_dot(v.astype(BF16), v1_ref[...]).astype(BF16), v2_ref[...])
        v = v + (vf_ref[...] - v) * _sigmoid(v0_ref[...] + lora)
    u = w0 + _dot(jnp.tanh(wa).astype(BF16), w2_ref[...])
    decay = jnp.exp(-math.exp(-0.5) * _sigmoid(u))
    a = _sigmoid(a0 + _dot(wa.astype(BF16), a2_ref[...]))
    g = _dot(_sigmoid(zg_s).astype(BF16), g2_ref[...])
    kk = k * k_k
    kk = kk / jnp.maximum(jnp.sqrt(_dot_split(kk * kk, hsum_ref[...])), 1e-12)
    k = k * (1.0 + (a - 1.0) * k_a)
    r_o[...] = r
    w_o[...] = decay
    k_o[...] = k
    v_o[...] = v
    kk_o[...] = kk
    b_o[...] = kk * a
    g_o[...] = g
    bonus_o[...] = _dot_split(r * k * r_k, hsum_ref[...]) * v
    vt_o[0] = v.T.astype(BF16)


def _rwkv_prep(z, mu5, vecs, w2p, a2p, g2, hsum, vres, bsz, t, tm=256):
    n = bsz * t
    nt = n // tm
    tps = t // tm
    w = RWKV_WIDTH

    def cur(width, col):
        return pl.BlockSpec((tm, width), lambda i: (i, col // width))

    def prev(width, col):
        return pl.BlockSpec((8, width), lambda i: (jnp.maximum(i * (tm // 8) - 1, 0), col // width))

    def full(shape):
        return pl.BlockSpec(shape, lambda i: (0,) * len(shape))

    cols = [(w, C_R), (w, C_K), (w, C_V), (LANE, C_WA), (LANE, C_ZG)]
    in_specs = [cur(*c) for c in cols] + [prev(*c) for c in cols]
    in_specs += [full(mu5.shape), full(vecs.shape), full(w2p.shape), full(a2p.shape), full(g2.shape), full(hsum.shape)]
    args = [z] * 10 + [mu5, vecs, w2p, a2p, g2, hsum]
    if vres is not None:
        v_first, v0, v1p, v2p = vres
        in_specs += [pl.BlockSpec((tm, w), lambda i: (i, 0)), full(v0.shape), full(v1p.shape), full(v2p.shape)]
        args += [v_first, v0, v1p, v2p]
    tok = pl.BlockSpec((tm, w), lambda i: (i, 0))
    out_specs = [tok] * 8 + [pl.BlockSpec((1, w, tm), lambda i: (i // tps, 0, i % tps))]
    out_shape = [jax.ShapeDtypeStruct((n, w), F32)] * 8 + [jax.ShapeDtypeStruct((bsz, w, t), BF16)]
    return pl.pallas_call(
        functools.partial(_rwkv_prep_kernel, vres is not None, tps),
        grid=(nt,),
        in_specs=in_specs,
        out_specs=out_specs,
        out_shape=out_shape,
        compiler_params=_cparams(("parallel",)),
        name="rwkv_prep",
    )(*args)


N_PAIRS = RWKV_HEADS // 2


def _scan_kernel(r_ref, w_ref, k_ref, kk_ref, b_ref, vt_ref, ones_ref, o_ref, s_ref, kbd_ref, vk_ref):
    bsz = r_ref.shape[0]
    tt = r_ref.shape[1]

    @pl.when(pl.program_id(0) == 0)
    def _():
        s_ref[...] = jnp.zeros(s_ref.shape, F32)

    lane = lax.broadcasted_iota(jnp.int32, (tt, LANE), 1)
    for bi in range(bsz):
        for p in range(N_PAIRS):
            kp = k_ref[bi, :, p * LANE:(p + 1) * LANE]
            kbd_ref[bi * N_PAIRS + p] = jnp.concatenate(
                [jnp.where(lane < RWKV_HEAD, kp, 0.0), jnp.where(lane >= RWKV_HEAD, kp, 0.0)], axis=0).astype(BF16)

    step_lane = lax.broadcasted_iota(jnp.int32, (RWKV_HEAD, 2 * tt), 1) & (tt - 1)
    row8 = lax.broadcasted_iota(jnp.int32, (16, LANE), 0)
    lane8 = lax.broadcasted_iota(jnp.int32, (16, LANE), 1)
    r_mask = ((row8 == 0) & (lane8 < RWKV_HEAD)) | ((row8 == 1) & (lane8 >= RWKV_HEAD))
    ones_bd = ones_ref[...]
    chains = [(bi, p) for bi in range(bsz) for p in range(N_PAIRS)]

    nh = RWKV_HEAD

    def group(tg, carry):
        t0 = pl.multiple_of(tg * 8, 8)
        rows = []
        for bi, p in chains:
            ln = slice(p * LANE, (p + 1) * LANE)
            rows.append(tuple(ref[bi, pl.ds(t0, 8), ln] for ref in (w_ref, kk_ref, b_ref, r_ref)))
        onehots = [jnp.where(step_lane == t0 + u, 1.0, 0.0).astype(BF16) for u in range(8)]
        for c, (bi, p) in enumerate(chains):
            vt_pair = jnp.concatenate([vt_ref[bi, p * LANE:p * LANE + nh, :],
                                       vt_ref[bi, p * LANE + nh:(p + 1) * LANE, :]], axis=1)
            vk_ref[c] = _dot(jnp.concatenate([vt_pair * oh for oh in onehots], axis=0), kbd_ref[c])
        state = [s_ref[c] for c in range(len(chains))]
        for u in range(8):
            sk = _dot(jnp.concatenate([(state[c] * rows[c][1][u:u + 1, :]).astype(BF16)
                                       for c in range(len(chains))], axis=0), ones_bd)
            for c, (bi, p) in enumerate(chains):
                w8, _, b8, r8 = rows[c]
                state[c] = (state[c] * w8[u:u + 1, :] - sk[c * nh:(c + 1) * nh, :] * b8[u:u + 1, :]
                            + vk_ref[c, u * nh:(u + 1) * nh, :])
                r_lhs = jnp.where(r_mask, r8[u:u + 1, :], 0.0).astype(BF16)
                o = _dot_nt(r_lhs, state[c].astype(BF16))
                o_ref[t0 + u, bi, 2 * p:2 * p + 2, :] = o[0:2, :]
        for c in range(len(chains)):
            s_ref[c] = state[c]
        return carry

    lax.fori_loop(0, tt // 8, group, 0)


def _rwkv_scan(r, w, k, kk, b, vt, ones_bd, bsz, t):
    tt = SCAN_TT
    wd = RWKV_WIDTH
    tok = pl.BlockSpec((bsz, tt, wd), lambda i: (0, i, 0))
    return pl.pallas_call(
        _scan_kernel,
        grid=(t // tt,),
        in_specs=[tok, tok, tok, tok, tok,
                  pl.BlockSpec((bsz, wd, tt), lambda i: (0, 0, i)),
                  pl.BlockSpec((LANE, LANE), lambda i: (0, 0))],
        out_specs=pl.BlockSpec((tt, bsz, RWKV_HEADS, RWKV_HEAD), lambda i: (i, 0, 0, 0)),
        out_shape=jax.ShapeDtypeStruct((t, bsz, RWKV_HEADS, RWKV_HEAD), F32),
        scratch_shapes=[pltpu.VMEM((bsz * N_PAIRS, RWKV_HEAD, LANE), F32),
                        pltpu.VMEM((bsz * N_PAIRS, 2 * tt, LANE), BF16),
                        pltpu.VMEM((bsz * N_PAIRS, 8 * RWKV_HEAD, LANE), F32)],
        compiler_params=_cparams(("arbitrary",)),
        name="rwkv_scan",
    )(r, w, k, kk, b, vt, ones_bd)


def _merge_kernel(x_ref, oc_ref, os_ref, ow_ref, orw_ref, bonus_ref, g_ref, zgn_ref, zgr_ref, gn_ref, hsum_ref,
                  pn_ref, pr_ref, wo_ref, ln_ref, o_ref):
    y_n = jnp.zeros(o_ref.shape, F32)
    for h in range(NSA_HEADS):
        o_h = (oc_ref[0, h] + os_ref[0, h] + ow_ref[0, h]).astype(BF16)
        y_n = y_n + _dot(o_h, pn_ref[h * HEAD_DIM:(h + 1) * HEAD_DIM, :])
    o = orw_ref[...]
    mu = _dot_split(o, hsum_ref[...]) * (1.0 / RWKV_HEAD)
    d = o - mu
    var = _dot_split(d * d, hsum_ref[...]) * (1.0 / RWKV_HEAD)
    o = d * lax.rsqrt(var + GN_EPS) * gn_ref[0:1, :] + gn_ref[1:2, :]
    o = ((o + bonus_ref[...]) * g_ref[...]).astype(BF16)
    y1 = _sigmoid(zgn_ref[...]) * y_n + _sigmoid(zgr_ref[...]) * _dot(o, pr_ref[...])
    y = _dot(y1.astype(BF16), wo_ref[...])
    o_ref[...] = _layer_norm(DN_ALPHA * x_ref[...] + y, ln_ref[0:1, :], ln_ref[1:2, :])


def _merge(x, o_c, o_s, o_w, o_rw, bonus, g, z, gn, hsum, p_nsa, p_rwkv, w_out, ln, bsz, t, tm=256):
    n = bsz * t
    tps = t // tm
    d = D_MODEL
    w = RWKV_WIDTH
    head = pl.BlockSpec((1, NSA_HEADS, tm, HEAD_DIM), lambda i: (i // tps, 0, i % tps, 0))
    tok = pl.BlockSpec((tm, w), lambda i: (i, 0))

    def full(a):
        return pl.BlockSpec(a.shape, lambda i: (0,) * a.ndim)

    return pl.pallas_call(
        _merge_kernel,
        grid=(n // tm,),
        in_specs=[pl.BlockSpec((tm, d), lambda i: (i, 0)), head, head, head, tok, tok, tok,
                  pl.BlockSpec((tm, d), lambda i: (i, C_GN // d)),
                  pl.BlockSpec((tm, d), lambda i: (i, C_GR // d)),
                  full(gn), full(hsum), full(p_nsa), full(p_rwkv), full(w_out), full(ln)],
        out_specs=pl.BlockSpec((tm, d), lambda i: (i, 0)),
        out_shape=jax.ShapeDtypeStruct((n, d), F32),
        compiler_params=_cparams(("parallel",)),
        name="mix_merge_ln",
    )(x, o_c, o_s, o_w, o_rw, bonus, g, z, z, gn, hsum, p_nsa, p_rwkv, w_out, ln)


def _xattn_kernel(x_ref, kt_ref, v_ref, wq_ref, wo_ref, ln_ref, rw_ref, o_ref, s_ref):
    x = x_ref[...]
    q = _dot(x.astype(BF16), wq_ref[...]).astype(BF16)
    outs = []
    for h in range(XATTN_HEADS):
        hs = slice(h * XATTN_HEAD, (h + 1) * XATTN_HEAD)
        s = _dot(q[:, hs], kt_ref[0, hs, :]) * (XATTN_HEAD ** -0.5)
        m = jnp.max(s, axis=-1, keepdims=True)
        p = jnp.exp(s - m)
        l = jnp.sum(p, axis=-1, keepdims=True)
        outs.append(_dot(p.astype(BF16), v_ref[0, :, hs]) * (1.0 / l))
    o = jnp.concatenate(outs, axis=1).astype(BF16)
    x2 = _layer_norm(DN_ALPHA * x + _dot(o, wo_ref[...]), ln_ref[0:1, :], ln_ref[1:2, :])
    o_ref[...] = x2
    hi = x2.astype(BF16)
    lo = (x2 - hi.astype(F32)).astype(BF16)
    logits = _dot(hi, rw_ref[0]) + (_dot(lo, rw_ref[0]) + _dot(hi, rw_ref[1]))
    s_ref[...] = _sigmoid(logits)


def _xattn(x, kt, v, wq, wo, ln, rw2, bsz, t, tm=256):
    n = bsz * t
    tps = t // tm
    d = D_MODEL

    def full(a):
        return pl.BlockSpec(a.shape, lambda i: (0,) * a.ndim)

    return pl.pallas_call(
        _xattn_kernel,
        grid=(n // tm,),
        in_specs=[pl.BlockSpec((tm, d), lambda i: (i, 0)),
                  pl.BlockSpec((1,) + kt.shape[1:], lambda i: (i // tps, 0, 0)),
                  pl.BlockSpec((1,) + v.shape[1:], lambda i: (i // tps, 0, 0)),
                  full(wq), full(wo), full(ln), full(rw2)],
        out_specs=[pl.BlockSpec((tm, d), lambda i: (i, 0)), pl.BlockSpec((tm, LANE), lambda i: (i, 0))],
        out_shape=[jax.ShapeDtypeStruct((n, d), F32), jax.ShapeDtypeStruct((n, LANE), F32)],
        compiler_params=_cparams(("parallel",)),
        name="xattn_ln_router",
    )(x, kt, v, wq, wo, ln, rw2)


def _expert_kernel(be_ref, x_ref, wg_ref, wu_ref, wd_ref, o_ref, wg_s, wu_s, wd_s):
    i = pl.program_id(0)

    @pl.when((i == 0) | (be_ref[i] != be_ref[jnp.maximum(i - 1, 0)]))
    def _():
        wg_s[...] = wg_ref[0].astype(BF16)
        wu_s[...] = wu_ref[0].astype(BF16)
        wd_s[...] = wd_ref[0].astype(BF16)

    x = x_ref[...]
    hg = _dot(x, wg_s[...])
    h = hg * _sigmoid(hg) * _dot(x, wu_s[...])
    o_ref[...] = _dot(h.astype(BF16), wd_s[...]).astype(o_ref.dtype)


def _experts(block_expert, rows, w_gate, w_up, w_down):
    n_rows, d = rows.shape
    ff = w_gate.shape[2]
    return pl.pallas_call(
        _expert_kernel,
        grid_spec=pltpu.PrefetchScalarGridSpec(
            num_scalar_prefetch=1,
            grid=(n_rows // MOE_BLK,),
            in_specs=[pl.BlockSpec((MOE_BLK, d), lambda i, be: (i, 0)),
                      pl.BlockSpec((1, d, ff), lambda i, be: (be[i], 0, 0)),
                      pl.BlockSpec((1, d, ff), lambda i, be: (be[i], 0, 0)),
                      pl.BlockSpec((1, ff, d), lambda i, be: (be[i], 0, 0))],
            out_specs=pl.BlockSpec((MOE_BLK, d), lambda i, be: (i, 0)),
            scratch_shapes=[pltpu.VMEM((d, ff), BF16), pltpu.VMEM((d, ff), BF16), pltpu.VMEM((ff, d), BF16)]),
        out_shape=jax.ShapeDtypeStruct((n_rows, d), BF16),
        compiler_params=_cparams(("arbitrary",)),
        name="moe_experts",
    )(block_expert, rows, w_gate, w_up, w_down)


def _combine_kernel(x_ref, ya_ref, yb_ref, gate_ref, ln_ref, o_ref):
    y = gate_ref[:, 0:1] * ya_ref[...] + gate_ref[:, 1:2] * yb_ref[...]
    o_ref[...] = _layer_norm(DN_ALPHA * x_ref[...] + y, ln_ref[0:1, :], ln_ref[1:2, :])


def _combine(x, ya, yb, gate, ln, tm=512):
    n, d = x.shape
    tok = pl.BlockSpec((tm, d), lambda i: (i, 0))
    return pl.pallas_call(
        _combine_kernel,
        grid=(n // tm,),
        in_specs=[tok, tok, tok, pl.BlockSpec((tm, TOP_K), lambda i: (i, 0)),
                  pl.BlockSpec(ln.shape, lambda i: (0, 0))],
        out_specs=tok,
        out_shape=jax.ShapeDtypeStruct((n, d), F32),
        compiler_params=_cparams(("parallel",)),
        name="moe_combine_ln",
    )(x, ya, yb, gate, ln)


def _t5_bucket(dist):
    n = jnp.maximum(dist, 0)
    max_exact = REL_BUCKETS // 2
    log_ratio = jnp.log(jnp.maximum(n, 1).astype(F32) / max_exact) / math.log(REL_MAX_DIST / max_exact)
    large = jnp.minimum(max_exact + (log_ratio * (REL_BUCKETS - max_exact)).astype(jnp.int32), REL_BUCKETS - 1)
    return jnp.where(n < max_exact, n, large)


def _bias_tables(rel_table, t):
    nc = t // CMP_STRIDE
    f = rel_table[_t5_bucket(jnp.arange(t + 2048))].astype(F32).T
    ql = jnp.arange(Q_BLOCK)[:, None]

    def lookup(dist, ok):
        vals = jnp.take(f, jnp.clip(dist, 0, f.shape[-1] - 1), axis=1)
        vals = jnp.where(ok[None], vals, NEG_INF)
        return vals.reshape(NSA_GROUPS, NSA_HPG * dist.shape[0], dist.shape[1])

    j = jnp.arange(2 * nc)[None, :]
    dist_c = CMP_STRIDE * (nc - 8 - j) + ql - (CMP_LEN - 1)
    pc = lookup(dist_c, dist_c >= 0)
    nvar = WINDOW // Q_BLOCK + 1

    def toeplitz(base, width, hi, shift):
        period = width + Q_BLOCK
        m = np.arange(period)
        dist = base - np.where(m < width, m, m - period)
        row = jnp.take(f, np.clip(dist, 0, f.shape[-1] - 1), axis=1) - shift
        row = jnp.where(((dist >= 0) & (dist < hi))[None], row, NEG_INF)
        flat = jnp.tile(row, (1, Q_BLOCK))[:, :Q_BLOCK * (period - 1)]
        return flat.reshape(NSA_HEADS, Q_BLOCK, period - 1)[:, :, :width]

    wb = jnp.stack([toeplitz(Q_BLOCK * v, WINDOW + Q_BLOCK, WINDOW, 0.0) for v in range(nvar)], axis=1)
    wb = wb.reshape(NSA_GROUPS, NSA_HPG, nvar, Q_BLOCK, -1).transpose(0, 2, 1, 3, 4)
    wb = wb.reshape(NSA_GROUPS, nvar, NSA_HPG * Q_BLOCK, -1)
    far = rel_table[REL_BUCKETS - 1].astype(F32)
    lt = toeplitz(SEL_NEAR + SEL_TK, SEL_NEAR + 2 * SEL_TK, f.shape[-1], far[:, None])
    lt = lt.reshape(NSA_GROUPS, NSA_HPG * Q_BLOCK, -1)
    far_hi = far.astype(BF16).astype(F32)
    lane = jnp.arange(LANE)[None, :]
    cq = jnp.where(lane == HEAD_DIM, far_hi[:, None], jnp.where(lane == HEAD_DIM + 1, (far - far_hi)[:, None], 0.0))
    cq = jnp.broadcast_to(cq[:, None, :], (NSA_HEADS, Q_BLOCK, LANE)).reshape(NSA_GROUPS, NSA_HPG * Q_BLOCK, LANE)
    return pc, wb, lt, cq


def _in_proj_perm():
    perm = np.full((IN_PAD,), -1, np.int64)
    for h in range(NSA_HEADS):
        perm[C_Q + h * LANE:C_Q + h * LANE + HEAD_DIM] = np.arange(HEAD_DIM) + h * HEAD_DIM
    rw0 = NSA_COLS
    g0 = NSA_COLS + RWKV_COLS
    perm[C_GN:C_GN + D_MODEL] = g0 + np.arange(D_MODEL)
    perm[C_GR:C_GR + D_MODEL] = g0 + D_MODEL + np.arange(D_MODEL)
    perm[C_R:C_R + RWKV_WIDTH] = rw0 + np.arange(RWKV_WIDTH)
    perm[C_K:C_K + RWKV_WIDTH] = rw0 + RWKV_WIDTH + np.arange(RWKV_WIDTH)
    perm[C_V:C_V + RWKV_WIDTH] = rw0 + 2 * RWKV_WIDTH + np.arange(RWKV_WIDTH)
    perm[C_KV:C_KV + 6 * NSA_KV] = NSA_WIDTH + np.arange(6 * NSA_KV)
    for g in range(NSA_GROUPS):
        for br in range(3):
            for h in range(NSA_HPG):
                perm[C_G3 + g * LANE + br * NSA_HPG + h] = NSA_WIDTH + 6 * NSA_KV + (g * NSA_HPG + h) * 3 + br
    perm[C_WA:C_WA + LORA_W + LORA_A] = rw0 + 3 * RWKV_WIDTH + np.arange(LORA_W + LORA_A)
    perm[C_ZG:C_ZG + LORA_G] = rw0 + 3 * RWKV_WIDTH + LORA_W + LORA_A + np.arange(LORA_G)
    return perm


_PERM = _in_proj_perm()


def _permute_in_proj(w):
    pieces = []
    start = 0
    while start < IN_PAD:
        stop = start + 1
        if _PERM[start] < 0:
            while stop < IN_PAD and _PERM[stop] < 0:
                stop += 1
            pieces.append(jnp.zeros((w.shape[0], stop - start), BF16))
        else:
            while stop < IN_PAD and _PERM[stop] == _PERM[stop - 1] + 1:
                stop += 1
            pieces.append(w[:, int(_PERM[start]):int(_PERM[stop - 1]) + 1].astype(BF16))
        start = stop
    return jnp.concatenate(pieces, axis=1)


def _pad_to(a, shape):
    return jnp.pad(a, [(0, s - d) for d, s in zip(a.shape, shape)])


def _route(s, router_bias, n_tok):
    s16 = s[:, :N_EXPERTS]
    s_sel = s16 + router_bias.astype(F32)

    def top2(a):
        i1 = jnp.argmax(a, axis=-1)
        rest = jnp.where(jnp.arange(a.shape[-1]) == i1[..., None], -jnp.inf, a)
        i2 = jnp.argmax(rest, axis=-1)
        return jnp.max(a, axis=-1) + jnp.max(rest, axis=-1), jnp.stack([i1, i2], axis=-1)

    group_score, _ = top2(s_sel.reshape(n_tok, N_EXPERT_GROUPS, EXPERTS_PER_GROUP))
    group = jnp.argmax(group_score, axis=-1)
    in_group = (jnp.arange(N_EXPERTS) // EXPERTS_PER_GROUP)[None, :] == group[:, None]
    _, e_idx = top2(jnp.where(in_group, s_sel, NEG_INF))
    s_top = jnp.take_along_axis(s16, e_idx, axis=-1)
    gate = s_top / jnp.sum(s_top, axis=-1, keepdims=True)
    e_flat = e_idx.reshape(-1)
    onehot = (e_flat[:, None] == jnp.arange(N_EXPERTS)[None, :]).astype(jnp.int32)
    rank = jnp.take_along_axis(jnp.cumsum(onehot, axis=0) - onehot, e_flat[:, None], axis=1)[:, 0]
    counts = jnp.sum(onehot, axis=0)
    padded = (counts + MOE_BLK - 1) // MOE_BLK * MOE_BLK
    ends = jnp.cumsum(padded)
    dest = (ends - padded)[e_flat] + rank
    n_rows = n_tok * TOP_K + N_EXPERTS * MOE_BLK
    tok_of_row = jnp.zeros((n_rows,), jnp.int32).at[dest].set(jnp.arange(n_tok * TOP_K, dtype=jnp.int32) // TOP_K)
    block_expert = jnp.minimum(
        jnp.searchsorted(ends, jnp.arange(n_rows // MOE_BLK) * MOE_BLK, side='right'), N_EXPERTS - 1).astype(jnp.int32)
    return gate, dest.reshape(n_tok, TOP_K), tok_of_row, block_expert


def kernel(x, mem, rel_table, router_w, router_bias, w_in, cmp_pe_k, cmp_pe_v, cmp_w1k, cmp_w2k, cmp_w1v, cmp_w2v, rwkv_mu, rwkv_w0, rwkv_w2, rwkv_a0, rwkv_a2, rwkv_g2, rwkv_kk, rwkv_ka, rwkv_rk, rwkv_gn_g, rwkv_gn_b, rwkv_v0, rwkv_v1, rwkv_v2, p_nsa, p_rwkv, w_out, ln1_g, ln1_b, xq_w, xk_w, xv_w, xo_w, ln2_g, ln2_b, moe_w_gate, moe_w_up, moe_w_down, ln3_g, ln3_b):
    bsz, t, d = x.shape
    n = bsz * t
    depth = w_in.shape[0]
    nc = t // CMP_STRIDE
    nsel = t // SEL_BLOCK
    g, hd = NSA_GROUPS, HEAD_DIM

    pc_tab, wb_tab, lt_tab, cq_tab = _bias_tables(rel_table, t)
    cmp_start = np.arange(nc) * CMP_STRIDE
    sel_start = np.arange(nsel) * SEL_BLOCK
    overlap = ((cmp_start[:, None] <= sel_start[None, :] + SEL_BLOCK - 1)
               & (cmp_start[:, None] + CMP_LEN - 1 >= sel_start[None, :]) & (cmp_start[:, None] < (nc - 1) * CMP_STRIDE))
    overlap = jnp.asarray(overlap.T, BF16)
    key_tail = np.zeros((t, 2 * LANE - HEAD_DIM), np.float32)
    key_tail[:, 0:2] = 1.0
    key_tail[np.arange(t), LANE - HEAD_DIM + np.arange(t) // SEL_BLOCK] = 1.0
    key_tail = jnp.asarray(key_tail, BF16)
    lane_head = np.arange(RWKV_WIDTH) // RWKV_HEAD
    hsum = jnp.asarray(lane_head[:, None] == lane_head[None, :], BF16)
    ones_bd = jnp.asarray((np.arange(LANE) // RWKV_HEAD)[:, None] == (np.arange(LANE) // RWKV_HEAD)[None, :], BF16)
    rw_f = _pad_to(router_w.astype(F32), (d, LANE))
    rw_hi = rw_f.astype(BF16)
    rw2 = jnp.stack([rw_hi, (rw_f - rw_hi.astype(F32)).astype(BF16)])

    xf = x.reshape(n, d)
    mem_f = mem.reshape(bsz * mem.shape[1], d)
    v_first = None
    for l in range(depth):
        w_in_l = _permute_in_proj(w_in[l])
        z = _matmul(xf, w_in_l, F32, 512, 1024)

        kv = z[:, C_KV:C_KV + 6 * NSA_KV].reshape(bsz, t, 6, g, hd)
        kc, vc, ks, vs, kw, vw = (kv[:, :, i] for i in range(6))

        def chunks(a):
            return a.reshape(bsz, nc, CMP_STRIDE, g, hd).transpose(0, 3, 1, 2, 4).reshape(bsz, g, nc, CMP_STRIDE * hd)

        def keys(a):
            return _pad_to(a.transpose(0, 2, 1, 3), (bsz, g, t, LANE)).astype(BF16)

        def cmp_w(pe, w1, w2):
            return (pe.reshape(2, CMP_STRIDE * hd), w1.reshape(2, CMP_STRIDE * hd, CMP_HIDDEN).astype(BF16),
                    _pad_to(w2, (CMP_HIDDEN, LANE)).astype(BF16))

        k_cmp = _compress(chunks(kc), *cmp_w(cmp_pe_k[l], cmp_w1k[l], cmp_w2k[l]))
        v_cmp = _compress(chunks(vc), *cmp_w(cmp_pe_v[l], cmp_w1v[l], cmp_w2v[l]))
        o_c, sel_mask = _cmp_branch(z, k_cmp, v_cmp, pc_tab, overlap, bsz, t)
        ks_aug = jnp.concatenate([ks.transpose(0, 2, 1, 3).astype(BF16),
                                  jnp.broadcast_to(key_tail, (bsz, g, t, key_tail.shape[1]))], axis=-1)

        def values(a):
            return jnp.concatenate([a.transpose(0, 2, 1, 3), jnp.ones((bsz, g, t, 1), F32),
                                    jnp.zeros((bsz, g, t, LANE - hd - 1), F32)], axis=-1).astype(BF16)

        o_s = _sel_branch(z, ks_aug, values(vs), sel_mask, lt_tab, cq_tab, bsz, t)
        o_w = _win_branch(z, keys(kw), values(vw), wb_tab, bsz, t)

        mu = rwkv_mu[l]
        w3 = 3 * RWKV_WIDTH
        mu5 = jnp.stack([mu[0:512], mu[512:1024], mu[1024:1536], _pad_to(mu[w3:w3 + 128], (512,)),
                         _pad_to(mu[w3 + 128:w3 + 256], (512,))])
        vecs = jnp.stack([rwkv_w0[l], rwkv_a0[l], rwkv_kk[l], rwkv_ka[l], rwkv_rk[l].reshape(-1)])
        w2p = jnp.concatenate([rwkv_w2[l], jnp.zeros_like(rwkv_a2[l])], axis=0).astype(BF16)
        a2p = jnp.concatenate([jnp.zeros_like(rwkv_w2[l]), rwkv_a2[l]], axis=0).astype(BF16)
        vres = None
        if l > 0:
            vres = (v_first, rwkv_v0[l - 1][None, :], _pad_to(rwkv_v1[l - 1], (RWKV_WIDTH, LANE)).astype(BF16),
                    _pad_to(rwkv_v2[l - 1], (LANE, RWKV_WIDTH)).astype(BF16))
        r, w, k, v, kk, b, g_out, bonus, vt = _rwkv_prep(z, mu5, vecs, w2p, a2p, rwkv_g2[l].astype(BF16), hsum, vres,
                                                          bsz, t)
        if l == 0:
            v_first = v
        seq = lambda a: a.reshape(bsz, t, RWKV_WIDTH)
        o_rw = _rwkv_scan(seq(r), seq(w), seq(k), seq(kk), seq(b), vt, ones_bd, bsz, t)
        o_rw = o_rw.transpose(1, 0, 2, 3).reshape(n, RWKV_WIDTH)

        xf = _merge(xf, o_c, o_s, o_w, o_rw, bonus, g_out, z, jnp.stack([rwkv_gn_g[l], rwkv_gn_b[l]]), hsum,
                    p_nsa[l].astype(BF16), p_rwkv[l].astype(BF16), w_out[l].astype(BF16),
                    jnp.stack([ln1_g[l], ln1_b[l]]), bsz, t)

        mlen = mem.shape[1]
        mk = _matmul(mem_f, xk_w[l].astype(BF16), BF16, mlen, XATTN_WIDTH)
        mv = _matmul(mem_f, xv_w[l].astype(BF16), BF16, mlen, XATTN_WIDTH)
        kt = mk.reshape(bsz, mlen, XATTN_WIDTH).transpose(0, 2, 1)
        xf, scores = _xattn(xf, kt, mv.reshape(bsz, mlen, XATTN_WIDTH), xq_w[l].astype(BF16), xo_w[l].astype(BF16),
                            jnp.stack([ln2_g[l], ln2_b[l]]), rw2, bsz, t)

        gate, dest, tok_of_row, block_expert = _route(scores, router_bias, n)
        rows = jnp.take(xf.astype(BF16), tok_of_row, axis=0)
        y_rows = _experts(block_expert, rows, moe_w_gate[l], moe_w_up[l], moe_w_down[l])
        xf = _combine(xf, jnp.take(y_rows, dest[:, 0], axis=0), jnp.take(y_rows, dest[:, 1], axis=0), gate,
                      jnp.stack([ln3_g[l], ln3_b[l]]))
    return xf.reshape(bsz, t, d)
```

```python
import functools
import math

import numpy as np
import jax
import jax.numpy as jnp
from jax import lax
from jax.experimental import pallas as pl
from jax.experimental.pallas import tpu as pltpu

F32 = jnp.float32
BF16 = jnp.bfloat16

D_MODEL = 1024
DEPTH = 2
NSA_HEADS = 8
NSA_GROUPS = 2
NSA_HPG = NSA_HEADS // NSA_GROUPS
HEAD_DIM = 64
NSA_WIDTH = NSA_HEADS * HEAD_DIM
NSA_KV = NSA_GROUPS * HEAD_DIM
CMP_STRIDE = 16
CMP_LEN = 2 * CMP_STRIDE
CMP_HIDDEN = 256
SEL_BLOCK = 64
SEL_TOPN = 16
WINDOW = 512
Q_BLOCK = 128
SEL_FORCE = 1e9
RWKV_HEADS = 8
RWKV_HEAD = 64
RWKV_WIDTH = RWKV_HEADS * RWKV_HEAD
LORA_W = 64
LORA_A = 64
LORA_V = 32
LORA_G = 128
GN_EPS = 64e-5
REL_BUCKETS = 32
REL_MAX_DIST = 1024
XATTN_HEADS = 4
XATTN_HEAD = 128
XATTN_WIDTH = XATTN_HEADS * XATTN_HEAD
N_EXPERTS = 16
N_EXPERT_GROUPS = 4
EXPERTS_PER_GROUP = N_EXPERTS // N_EXPERT_GROUPS
TOP_K = 2
EXPERT_FF = 512
DN_ALPHA = (2 * DEPTH) ** 0.25
LN_EPS = 1e-5
NEG_INF = -1e30
NSA_COLS = NSA_WIDTH + 6 * NSA_KV + 3 * NSA_HEADS
RWKV_COLS = 3 * RWKV_WIDTH + LORA_W + LORA_A + LORA_G

LANE = 128
VMEM_LIMIT = 56 * 1024 * 1024

C_Q = 0
C_GN = 1024
C_GR = 2048
C_R = 3072
C_K = 3584
C_V = 4096
C_G3 = 4608
C_WA = 4864
C_ZG = 4992
IN_PAD = 5120
KV_ARRAYS = 6

MOE_BLK = 256
SCAN_TT = 128
FAR_BIAS_DIST = 1280


def _cparams(sem):
    return pltpu.CompilerParams(dimension_semantics=sem, vmem_limit_bytes=VMEM_LIMIT)


def _sigmoid(x):
    return 1.0 / (1.0 + jnp.exp(-x))


def _layer_norm(v, g, b):
    mu = jnp.mean(v, axis=-1, keepdims=True)
    d = v - mu
    var = jnp.mean(d * d, axis=-1, keepdims=True)
    return d * lax.rsqrt(var + LN_EPS) * g + b


def _dot(a, b):
    return jnp.dot(a, b, preferred_element_type=F32)


def _dot_nt(a, b):
    return lax.dot_general(a, b, (((1,), (1,)), ((), ())), preferred_element_type=F32)


def _dot_split(x, w):
    hi = x.astype(BF16)
    lo = (x - hi.astype(F32)).astype(BF16)
    return _dot(hi, w) + _dot(lo, w)


def _mm_kernel(x_ref, w_ref, o_ref, xb_ref):
    @pl.when(pl.program_id(1) == 0)
    def _():
        xb_ref[...] = x_ref[...].astype(BF16)

    o_ref[...] = _dot(xb_ref[...], w_ref[...]).astype(o_ref.dtype)


def _matmul(x, w, out_dtype, tm, tn):
    n, k = x.shape
    m = w.shape[1]
    return pl.pallas_call(
        _mm_kernel,
        grid=(n // tm, m // tn),
        in_specs=[pl.BlockSpec((tm, k), lambda i, j: (i, 0)),
                  pl.BlockSpec((k, tn), lambda i, j: (0, j))],
        out_specs=pl.BlockSpec((tm, tn), lambda i, j: (i, j)),
        out_shape=jax.ShapeDtypeStruct((n, m), out_dtype),
        scratch_shapes=[pltpu.VMEM((tm, k), BF16)],
        compiler_params=_cparams(("parallel", "arbitrary")),
        name="matmul",
    )(x, w)


def _kv_proj_kernel(tiles_per_seq, x_ref, w_ref, kc_ref, vc_ref, ks_ref, vs_ref, kw_ref, vw_ref, xb_ref):
    @pl.when(pl.program_id(1) == 0)
    def _():
        xb_ref[...] = x_ref[...].astype(BF16)

    y = _dot(xb_ref[...], w_ref[0])
    tm = y.shape[0]
    slot = lambda i: y[:, i * LANE:(i + 1) * LANE]
    lane = lax.broadcasted_iota(jnp.int32, (tm, LANE), 1)
    one_lane = jnp.where(lane == HEAD_DIM, 1.0, 0.0)
    t0 = lax.rem(pl.program_id(0), tiles_per_seq) * tm
    block = jnp.right_shift(t0 + lax.broadcasted_iota(jnp.int32, (tm, LANE), 0), int(math.log2(SEL_BLOCK)))
    kc_ref[0, 0] = slot(0)[:, :HEAD_DIM]
    vc_ref[0, 0] = slot(1)[:, :HEAD_DIM]
    ks_ref[0, 0] = jnp.concatenate([slot(2) + jnp.where(lane == HEAD_DIM + 1, 1.0, one_lane),
                                    jnp.where(lane == block, 1.0, 0.0)], axis=1).astype(BF16)
    vs_ref[0, 0] = (slot(3) + one_lane).astype(BF16)
    kw_ref[0, 0] = slot(4).astype(BF16)
    vw_ref[0, 0] = (slot(5) + one_lane).astype(BF16)


def _kv_proj(x, w_kv, bsz, t, tm=512):
    n, d = x.shape
    tps = t // tm
    g = NSA_GROUPS

    def out(width, dtype):
        return (pl.BlockSpec((1, 1, tm, width), lambda i, j: (i // tps, j, i % tps, 0)),
                jax.ShapeDtypeStruct((bsz, g, t, width), dtype))

    outs = [out(HEAD_DIM, F32), out(HEAD_DIM, F32), out(2 * LANE, BF16), out(LANE, BF16), out(LANE, BF16),
            out(LANE, BF16)]
    return pl.pallas_call(
        functools.partial(_kv_proj_kernel, tps),
        grid=(n // tm, g),
        in_specs=[pl.BlockSpec((tm, d), lambda i, j: (i, 0)),
                  pl.BlockSpec((1, d, KV_ARRAYS * LANE), lambda i, j: (j, 0, 0))],
        out_specs=[o[0] for o in outs],
        out_shape=[o[1] for o in outs],
        scratch_shapes=[pltpu.VMEM((tm, d), BF16)],
        compiler_params=_cparams(("parallel", "arbitrary")),
        name="nsa_kv_proj",
    )(x, w_kv)


def _compress_kernel(u_ref, pe_ref, w1_ref, w2_ref, o_ref):
    u = u_ref[0, 0]
    a = _dot((u + pe_ref[0:1, :]).astype(BF16), w1_ref[0])
    b = _dot((u + pe_ref[1:2, :]).astype(BF16), w1_ref[1])
    nc = u.shape[0]
    h = a + pltpu.roll(b, nc - 1, 0)
    h = 0.5 * h * (1.0 + jnp.tanh(math.sqrt(2.0 / math.pi) * (h + 0.044715 * (h * h * h))))
    y = _dot(h.astype(BF16), w2_ref[...])
    row = lax.broadcasted_iota(jnp.int32, y.shape, 0)
    o_ref[0, 0] = jnp.where(row < nc - 1, y, 0.0)


def _compress(u, pe2, w1, w2p):
    b, g, nc, _ = u.shape
    return pl.pallas_call(
        _compress_kernel,
        grid=(b, g),
        in_specs=[pl.BlockSpec((1, 1, nc, 1024), lambda i, j: (i, j, 0, 0)),
                  pl.BlockSpec((2, 1024), lambda i, j: (0, 0)),
                  pl.BlockSpec((2, 1024, CMP_HIDDEN), lambda i, j: (0, 0, 0)),
                  pl.BlockSpec((CMP_HIDDEN, LANE), lambda i, j: (0, 0))],
        out_specs=pl.BlockSpec((1, 1, nc, LANE), lambda i, j: (i, j, 0, 0)),
        out_shape=jax.ShapeDtypeStruct((b, g, nc, LANE), F32),
        compiler_params=_cparams(("parallel", "parallel")),
        name="nsa_compress",
    )(u, pe2, w1, w2p)


def _stack_heads(q_ref):
    q = q_ref[...] * (HEAD_DIM ** -0.5)
    return jnp.concatenate([q[:, h * LANE:(h + 1) * LANE] for h in range(NSA_HPG)], axis=0).astype(BF16)


def _store_heads(o_ref, o, gates, branch):
    for h in range(NSA_HPG):
        c = branch * NSA_HPG + h
        o_ref[0, h] = (o[h * Q_BLOCK:(h + 1) * Q_BLOCK, :] * gates[:, c:c + 1])[:, :HEAD_DIM]


def _cmp_kernel(q_ref, zg_ref, kc_ref, vc_ref, pc_ref, ov_ref, o_ref, m_ref):
    c = pl.program_id(2)
    nc = kc_ref.shape[2]
    nsel = ov_ref.shape[0]
    qs = _stack_heads(q_ref)
    s = _dot_nt(qs, kc_ref[0, 0].astype(BF16))
    off = (nc - 8) - 8 * c
    bias = pltpu.roll(pc_ref[0], (2 * nc - off) % (2 * nc), 1)[:, :nc]
    s = s + bias
    m = jnp.max(s, axis=-1, keepdims=True)
    p = jnp.exp(s - m)
    l = jnp.sum(p, axis=-1, keepdims=True)
    ql = lax.broadcasted_iota(jnp.int32, (NSA_HPG * Q_BLOCK, 1), 0) & (Q_BLOCK - 1)
    any_key = (c * Q_BLOCK + ql) >= (CMP_LEN - 1)
    p = p * jnp.where(any_key, 1.0 / l, 0.0)
    gates = _sigmoid(zg_ref[...])
    _store_heads(o_ref, _dot(p.astype(BF16), vc_ref[0, 0].astype(BF16)), gates, 0)

    psum = p[0:Q_BLOCK] + p[Q_BLOCK:2 * Q_BLOCK] + p[2 * Q_BLOCK:3 * Q_BLOCK] + p[3 * Q_BLOCK:4 * Q_BLOCK]
    hi = psum.astype(BF16)
    lo = (psum - hi.astype(F32)).astype(BF16)
    imp = _dot_nt(ov_ref[...], hi) + _dot_nt(ov_ref[...], lo)
    jb = lax.broadcasted_iota(jnp.int32, (nsel, Q_BLOCK), 0)
    tq = c * Q_BLOCK + lax.broadcasted_iota(jnp.int32, (nsel, Q_BLOCK), 1)
    jq = jnp.right_shift(tq, int(math.log2(SEL_BLOCK)))
    forced = (jb == 0) | (jb == jq) | (jb == jq - 1)
    val = jnp.where(forced, SEL_FORCE, jnp.where(jb <= jq, imp, -SEL_FORCE))
    sel = jnp.zeros((nsel, Q_BLOCK), F32)
    jbf = jb.astype(F32)
    for _ in range(SEL_TOPN):
        mx = jnp.max(val, axis=0, keepdims=True)
        first = jnp.min(jnp.where(val == mx, jbf, float(nsel)), axis=0, keepdims=True)
        hit = jbf == first
        sel = jnp.where(hit, jnp.where(mx >= 0.0, 1.0, 0.0), sel)
        val = jnp.where(hit, -3e38, val)
    m_ref[0, 0] = (sel.T - 1.0) * (-NEG_INF)


def _cmp_branch(z, k_cmp, v_cmp, pc_tab, overlap, bsz, t):
    nq = t // Q_BLOCK
    nc = k_cmp.shape[2]
    nsel = overlap.shape[0]
    return pl.pallas_call(
        _cmp_kernel,
        grid=(bsz, NSA_GROUPS, nq),
        in_specs=[pl.BlockSpec((Q_BLOCK, NSA_HPG * LANE), lambda b, g, c: (b * nq + c, g)),
                  pl.BlockSpec((Q_BLOCK, LANE), lambda b, g, c: (b * nq + c, C_G3 // LANE + g)),
                  pl.BlockSpec((1, 1, nc, LANE), lambda b, g, c: (b, g, 0, 0)),
                  pl.BlockSpec((1, 1, nc, LANE), lambda b, g, c: (b, g, 0, 0)),
                  pl.BlockSpec((1, NSA_HPG * Q_BLOCK, 2 * nc), lambda b, g, c: (g, 0, 0)),
                  pl.BlockSpec((nsel, nc), lambda b, g, c: (0, 0))],
        out_specs=[pl.BlockSpec((1, NSA_HPG, Q_BLOCK, HEAD_DIM), lambda b, g, c: (b, g, c, 0)),
                   pl.BlockSpec((1, 1, Q_BLOCK, nsel), lambda b, g, c: (b, g, c, 0))],
        out_shape=[jax.ShapeDtypeStruct((bsz, NSA_HEADS, t, HEAD_DIM), F32),
                   jax.ShapeDtypeStruct((bsz, NSA_GROUPS, t, nsel), F32)],
        compiler_params=_cparams(("parallel", "parallel", "arbitrary")),
        name="nsa_cmp_select",
    )(z, z, k_cmp, v_cmp, pc_tab, overlap)


def _win_kernel(q_ref, zg_ref, k_ref, v_ref, wb_ref, o_ref, s_s, p_s):
    c = pl.program_id(2)
    start = pl.multiple_of(jnp.maximum(c * Q_BLOCK - WINDOW, 0), Q_BLOCK)
    qs = _stack_heads(q_ref)
    s_s[...] = _dot_nt(qs, k_ref[0, 0, pl.ds(start, WINDOW + Q_BLOCK), :])
    for r in range(NSA_HPG * Q_BLOCK // SEL_ROWS):
        rows = slice(r * SEL_ROWS, (r + 1) * SEL_ROWS)
        s = s_s[rows, :] + wb_ref[0, 0, rows, :]
        p_s[rows, :] = jnp.exp(s - jnp.max(s, axis=-1, keepdims=True)).astype(BF16)
    o = _dot(p_s[...], v_ref[0, 0, pl.ds(start, WINDOW + Q_BLOCK), :])
    o = o * (1.0 / o[:, HEAD_DIM:HEAD_DIM + 1])
    _store_heads(o_ref, o, _sigmoid(zg_ref[...]), 2)


def _win_branch(z, kw, vw, wb_tab, bsz, t):
    nq = t // Q_BLOCK
    nvar = wb_tab.shape[1]
    return pl.pallas_call(
        _win_kernel,
        grid=(bsz, NSA_GROUPS, nq),
        in_specs=[pl.BlockSpec((Q_BLOCK, NSA_HPG * LANE), lambda b, g, c: (b * nq + c, g)),
                  pl.BlockSpec((Q_BLOCK, LANE), lambda b, g, c: (b * nq + c, C_G3 // LANE + g)),
                  pl.BlockSpec((1, 1, t, LANE), lambda b, g, c: (b, g, 0, 0)),
                  pl.BlockSpec((1, 1, t, LANE), lambda b, g, c: (b, g, 0, 0)),
                  pl.BlockSpec((1, 1, NSA_HPG * Q_BLOCK, WINDOW + Q_BLOCK),
                               lambda b, g, c: (g, jnp.minimum(c, nvar - 1), 0, 0))],
        out_specs=pl.BlockSpec((1, NSA_HPG, Q_BLOCK, HEAD_DIM), lambda b, g, c: (b, g, c, 0)),
        out_shape=jax.ShapeDtypeStruct((bsz, NSA_HEADS, t, HEAD_DIM), F32),
        scratch_shapes=[pltpu.VMEM((NSA_HPG * Q_BLOCK, WINDOW + Q_BLOCK), F32),
                        pltpu.VMEM((NSA_HPG * Q_BLOCK, WINDOW + Q_BLOCK), BF16)],
        compiler_params=_cparams(("parallel", "parallel", "arbitrary")),
        name="nsa_window",
    )(z, z, kw, vw, wb_tab)


SEL_TK = 512


SEL_ROWS = 64
SEL_NEAR = FAR_BIAS_DIST + Q_BLOCK


def _sel_kernel(q_ref, zg_ref, k_ref, v_ref, mt_ref, lt_ref, cq_ref, o_ref, m_s, acc_s, s0_s, s1_s, p_s, a_s):
    c = pl.program_id(2)
    q = q_ref[...] * (HEAD_DIM ** -0.5)
    qs = jnp.concatenate([q[:, h * LANE:(h + 1) * LANE] for h in range(NSA_HPG)], axis=0) + cq_ref[0]
    mt = mt_ref[0, 0]
    q_aug = jnp.concatenate([qs, jnp.concatenate([mt] * NSA_HPG, axis=0)], axis=1).astype(BF16)
    m_s[...] = jnp.full(m_s.shape, NEG_INF, F32)
    acc_s[...] = jnp.zeros(acc_s.shape, F32)
    ncol = SEL_TK // LANE
    n_tiles = lax.div(c, SEL_TK // Q_BLOCK) + 1

    def scores(i, dst):
        i = jnp.minimum(i, n_tiles - 1)
        dst[...] = _dot_nt(q_aug, k_ref[0, 0, pl.ds(pl.multiple_of(i * SEL_TK, SEL_TK), SEL_TK), :])

    def tile(i, cur, nxt):
        scores(i + 1, nxt)
        off = pl.multiple_of(jnp.maximum(SEL_NEAR + SEL_TK - (c * Q_BLOCK - i * SEL_TK), 0), LANE)
        for r in range(NSA_HPG * Q_BLOCK // SEL_ROWS):
            rows = slice(r * SEL_ROWS, (r + 1) * SEL_ROWS)
            s = cur[rows, :] + lt_ref[0, rows, pl.ds(off, SEL_TK)]
            cols = [s[:, j * LANE:(j + 1) * LANE] for j in range(ncol)]
            mx = functools.reduce(jnp.maximum, cols)
            m_old = m_s[rows, :]
            m_new = jnp.maximum(m_old, jnp.max(mx, axis=-1, keepdims=True))
            m_s[rows, :] = m_new
            a_s[rows, :] = jnp.exp(m_old - m_new)
            p_s[rows, :] = jnp.concatenate([jnp.exp(col - m_new) for col in cols], axis=1).astype(BF16)
        k0 = pl.multiple_of(i * SEL_TK, SEL_TK)
        acc_s[...] = a_s[...] * acc_s[...] + _dot(p_s[...], v_ref[0, 0, pl.ds(k0, SEL_TK), :])

    scores(0, s0_s)

    def pair(j, carry):
        tile(2 * j, s0_s, s1_s)

        @pl.when(2 * j + 1 < n_tiles)
        def _():
            tile(2 * j + 1, s1_s, s0_s)

        return carry

    lax.fori_loop(0, lax.div(n_tiles + 1, 2), pair, 0)
    acc = acc_s[...]
    o = acc * (1.0 / acc[:, HEAD_DIM:HEAD_DIM + 1])
    _store_heads(o_ref, o, _sigmoid(zg_ref[...]), 1)


def _sel_branch(z, ks_aug, vs, mt, lt_tab, cq_tab, bsz, t):
    nq = t // Q_BLOCK
    nsel = mt.shape[-1]
    rows = NSA_HPG * Q_BLOCK
    return pl.pallas_call(
        _sel_kernel,
        grid=(bsz, NSA_GROUPS, nq),
        in_specs=[pl.BlockSpec((Q_BLOCK, NSA_HPG * LANE), lambda b, g, c: (b * nq + c, g)),
                  pl.BlockSpec((Q_BLOCK, LANE), lambda b, g, c: (b * nq + c, C_G3 // LANE + g)),
                  pl.BlockSpec((1, 1, t, 2 * LANE), lambda b, g, c: (b, g, 0, 0)),
                  pl.BlockSpec((1, 1, t, LANE), lambda b, g, c: (b, g, 0, 0)),
                  pl.BlockSpec((1, 1, Q_BLOCK, nsel), lambda b, g, c: (b, g, c, 0)),
                  pl.BlockSpec((1, rows, lt_tab.shape[2]), lambda b, g, c: (g, 0, 0)),
                  pl.BlockSpec((1, rows, LANE), lambda b, g, c: (g, 0, 0))],
        out_specs=pl.BlockSpec((1, NSA_HPG, Q_BLOCK, HEAD_DIM), lambda b, g, c: (b, g, c, 0)),
        out_shape=jax.ShapeDtypeStruct((bsz, NSA_HEADS, t, HEAD_DIM), F32),
        scratch_shapes=[pltpu.VMEM((rows, LANE), F32),
                        pltpu.VMEM((rows, LANE), F32),
                        pltpu.VMEM((rows, SEL_TK), F32),
                        pltpu.VMEM((rows, SEL_TK), F32),
                        pltpu.VMEM((rows, SEL_TK), BF16),
                        pltpu.VMEM((rows, LANE), F32)],
        compiler_params=_cparams(("parallel", "parallel", "arbitrary")),
        name="nsa_selected",
    )(z, z, ks_aug, vs, mt, lt_tab, cq_tab)


def _shift_mix(z_ref, prev_ref, mu, first):
    z = z_ref[...]
    prev = jnp.where(first, 0.0, prev_ref[7:8, :])
    row = lax.broadcasted_iota(jnp.int32, z.shape, 0)
    zp = jnp.where(row == 0, prev, pltpu.roll(z, 1, 0))
    return z + mu * (zp - z)


def _rwkv_prep_kernel(has_vres, tiles_per_seq, *refs):
    (zr, zk, zv, zwa, zg, pr, pk, pv, pwa, pg, mu_ref, vec_ref, w2_ref, a2_ref, g2_ref, hsum_ref) = refs[:16]
    pos = 16
    if has_vres:
        vf_ref, v0_ref, v1_ref, v2_ref = refs[pos:pos + 4]
        pos += 4
    r_o, w_o, k_o, v_o, kk_o, b_o, g_o, bonus_o, vt_o = refs[pos:]
    first = lax.rem(pl.program_id(0), tiles_per_seq) == 0
    r = _shift_mix(zr, pr, mu_ref[0:1, :], first)
    k = _shift_mix(zk, pk, mu_ref[1:2, :], first)
    v = _shift_mix(zv, pv, mu_ref[2:3, :], first)
    wa = _shift_mix(zwa, pwa, mu_ref[3:4, 0:LANE], first)
    zg_s = _shift_mix(zg, pg, mu_ref[4:5, 0:LANE], first)
    w0, a0, k_k, k_a, r_k = (vec_ref[i:i + 1, :] for i in range(5))
    if has_vres:
        lora = _dot(_dot(v.astype(BF16), v1_ref[...]).astype(BF16), v2_ref[...])
        v = v + (vf_ref[...] - v) * _sigmoid(v0_ref[...] + lora)
    u = w0 + _dot(jnp.tanh(wa).astype(BF16), w2_ref[...])
    decay = jnp.exp(-math.exp(-0.5) * _sigmoid(u))
    a = _sigmoid(a0 + _dot(wa.astype(BF16), a2_ref[...]))
    g = _dot(_sigmoid(zg_s).astype(BF16), g2_ref[...])
    kk = k * k_k
    kk = kk / jnp.maximum(jnp.sqrt(_dot_split(kk * kk, hsum_ref[...])), 1e-12)
    k = k * (1.0 + (a - 1.0) * k_a)
    r_o[...] = r
    w_o[...] = decay
    k_o[...] = k
    v_o[...] = v
    kk_o[...] = kk
    b_o[...] = kk * a
    g_o[...] = g
    bonus_o[...] = _dot_split(r * k * r_k, hsum_ref[...]) * v
    vt_o[0] = v.T.astype(BF16)


def _rwkv_prep(z, mu5, vecs, w2p, a2p, g2, hsum, vres, bsz, t, tm=256):
    n = bsz * t
    nt = n // tm
    tps = t // tm
    w = RWKV_WIDTH

    def cur(width, col):
        return pl.BlockSpec((tm, width), lambda i: (i, col // width))

    def prev(width, col):
        return pl.BlockSpec((8, width), lambda i: (jnp.maximum(i * (tm // 8) - 1, 0), col // width))

    def full(shape):
        return pl.BlockSpec(shape, lambda i: (0,) * len(shape))

    cols = [(w, C_R), (w, C_K), (w, C_V), (LANE, C_WA), (LANE, C_ZG)]
    in_specs = [cur(*c) for c in cols] + [prev(*c) for c in cols]
    in_specs += [full(mu5.shape), full(vecs.shape), full(w2p.shape), full(a2p.shape), full(g2.shape), full(hsum.shape)]
    args = [z] * 10 + [mu5, vecs, w2p, a2p, g2, hsum]
    if vres is not None:
        v_first, v0, v1p, v2p = vres
        in_specs += [pl.BlockSpec((tm, w), lambda i: (i, 0)), full(v0.shape), full(v1p.shape), full(v2p.shape)]
        args += [v_first, v0, v1p, v2p]
    tok = pl.BlockSpec((tm, w), lambda i: (i, 0))
    out_specs = [tok] * 8 + [pl.BlockSpec((1, w, tm), lambda i: (i // tps, 0, i % tps))]
    out_shape = [jax.ShapeDtypeStruct((n, w), F32)] * 8 + [jax.ShapeDtypeStruct((bsz, w, t), BF16)]
    return pl.pallas_call(
        functools.partial(_rwkv_prep_kernel, vres is not None, tps),
        grid=(nt,),
        in_specs=in_specs,
        out_specs=out_specs,
        out_shape=out_shape,
        compiler_params=_cparams(("parallel",)),
        name="rwkv_prep",
    )(*args)


N_PAIRS = RWKV_HEADS // 2


def _scan_kernel(r_ref, w_ref, k_ref, kk_ref, b_ref, vt_ref, ones_ref, o_ref, s_ref, kbd_ref, vk_ref):
    bsz = r_ref.shape[0]
    tt = r_ref.shape[1]

    @pl.when(pl.program_id(0) == 0)
    def _():
        s_ref[...] = jnp.zeros(s_ref.shape, F32)

    lane = lax.broadcasted_iota(jnp.int32, (tt, LANE), 1)
    for bi in range(bsz):
        for p in range(N_PAIRS):
            kp = k_ref[bi, :, p * LANE:(p + 1) * LANE]
            kbd_ref[bi * N_PAIRS + p] = jnp.concatenate(
                [jnp.where(lane < RWKV_HEAD, kp, 0.0), jnp.where(lane >= RWKV_HEAD, kp, 0.0)], axis=0).astype(BF16)

    step_lane = lax.broadcasted_iota(jnp.int32, (RWKV_HEAD, 2 * tt), 1) & (tt - 1)
    row8 = lax.broadcasted_iota(jnp.int32, (16, LANE), 0)
    lane8 = lax.broadcasted_iota(jnp.int32, (16, LANE), 1)
    r_mask = ((row8 == 0) & (lane8 < RWKV_HEAD)) | ((row8 == 1) & (lane8 >= RWKV_HEAD))
    ones_bd = ones_ref[...]
    chains = [(bi, p) for bi in range(bsz) for p in range(N_PAIRS)]

    nh = RWKV_HEAD

    def group(tg, carry):
        t0 = pl.multiple_of(tg * 8, 8)
        rows = []
        for bi, p in chains:
            ln = slice(p * LANE, (p + 1) * LANE)
            rows.append(tuple(ref[bi, pl.ds(t0, 8), ln] for ref in (w_ref, kk_ref, b_ref, r_ref)))
        onehots = [jnp.where(step_lane == t0 + u, 1.0, 0.0).astype(BF16) for u in range(8)]
        for c, (bi, p) in enumerate(chains):
            vt_pair = jnp.concatenate([vt_ref[bi, p * LANE:p * LANE + nh, :],
                                       vt_ref[bi, p * LANE + nh:(p + 1) * LANE, :]], axis=1)
            vk_ref[c] = _dot(jnp.concatenate([vt_pair * oh for oh in onehots], axis=0), kbd_ref[c])
        state = [s_ref[c] for c in range(len(chains))]
        for u in range(8):
            sk = _dot(jnp.concatenate([(state[c] * rows[c][1][u:u + 1, :]).astype(BF16)
                                       for c in range(len(chains))], axis=0), ones_bd)
            for c, (bi, p) in enumerate(chains):
                w8, _, b8, r8 = rows[c]
                state[c] = (state[c] * w8[u:u + 1, :] - sk[c * nh:(c + 1) * nh, :] * b8[u:u + 1, :]
                            + vk_ref[c, u * nh:(u + 1) * nh, :])
                r_lhs = jnp.where(r_mask, r8[u:u + 1, :], 0.0).astype(BF16)
                o = _dot_nt(r_lhs, state[c].astype(BF16))
                o_ref[t0 + u, bi, 2 * p:2 * p + 2, :] = o[0:2, :]
        for c in range(len(chains)):
            s_ref[c] = state[c]
        return carry

    lax.fori_loop(0, tt // 8, group, 0)


def _rwkv_scan(r, w, k, kk, b, vt, ones_bd, bsz, t):
    tt = SCAN_TT
    wd = RWKV_WIDTH
    tok = pl.BlockSpec((bsz, tt, wd), lambda i: (0, i, 0))
    return pl.pallas_call(
        _scan_kernel,
        grid=(t // tt,),
        in_specs=[tok, tok, tok, tok, tok,
                  pl.BlockSpec((bsz, wd, tt), lambda i: (0, 0, i)),
                  pl.BlockSpec((LANE, LANE), lambda i: (0, 0))],
        out_specs=pl.BlockSpec((tt, bsz, RWKV_HEADS, RWKV_HEAD), lambda i: (i, 0, 0, 0)),
        out_shape=jax.ShapeDtypeStruct((t, bsz, RWKV_HEADS, RWKV_HEAD), F32),
        scratch_shapes=[pltpu.VMEM((bsz * N_PAIRS, RWKV_HEAD, LANE), F32),
                        pltpu.VMEM((bsz * N_PAIRS, 2 * tt, LANE), BF16),
                        pltpu.VMEM((bsz * N_PAIRS, 8 * RWKV_HEAD, LANE), F32)],
        compiler_params=_cparams(("arbitrary",)),
        name="rwkv_scan",
    )(r, w, k, kk, b, vt, ones_bd)


def _merge_kernel(x_ref, oc_ref, os_ref, ow_ref, orw_ref, bonus_ref, g_ref, zgn_ref, zgr_ref, gn_ref, hsum_ref,
                  pn_ref, pr_ref, wo_ref, ln_ref, o_ref):
    y_n = jnp.zeros(o_ref.shape, F32)
    for h in range(NSA_HEADS):
        o_h = (oc_ref[0, h] + os_ref[0, h] + ow_ref[0, h]).astype(BF16)
        y_n = y_n + _dot(o_h, pn_ref[h * HEAD_DIM:(h + 1) * HEAD_DIM, :])
    o = orw_ref[...]
    mu = _dot_split(o, hsum_ref[...]) * (1.0 / RWKV_HEAD)
    d = o - mu
    var = _dot_split(d * d, hsum_ref[...]) * (1.0 / RWKV_HEAD)
    o = d * lax.rsqrt(var + GN_EPS) * gn_ref[0:1, :] + gn_ref[1:2, :]
    o = ((o + bonus_ref[...]) * g_ref[...]).astype(BF16)
    y1 = _sigmoid(zgn_ref[...]) * y_n + _sigmoid(zgr_ref[...]) * _dot(o, pr_ref[...])
    y = _dot(y1.astype(BF16), wo_ref[...])
    o_ref[...] = _layer_norm(DN_ALPHA * x_ref[...] + y, ln_ref[0:1, :], ln_ref[1:2, :])


def _merge(x, o_c, o_s, o_w, o_rw, bonus, g, z, gn, hsum, p_nsa, p_rwkv, w_out, ln, bsz, t, tm=256):
    n = bsz * t
    tps = t // tm
    d = D_MODEL
    w = RWKV_WIDTH
    head = pl.BlockSpec((1, NSA_HEADS, tm, HEAD_DIM), lambda i: (i // tps, 0, i % tps, 0))
    tok = pl.BlockSpec((tm, w), lambda i: (i, 0))

    def full(a):
        return pl.BlockSpec(a.shape, lambda i: (0,) * a.ndim)

    return pl.pallas_call(
        _merge_kernel,
        grid=(n // tm,),
        in_specs=[pl.BlockSpec((tm, d), lambda i: (i, 0)), head, head, head, tok, tok, tok,
                  pl.BlockSpec((tm, d), lambda i: (i, C_GN // d)),
                  pl.BlockSpec((tm, d), lambda i: (i, C_GR // d)),
                  full(gn), full(hsum), full(p_nsa), full(p_rwkv), full(w_out), full(ln)],
        out_specs=pl.BlockSpec((tm, d), lambda i: (i, 0)),
        out_shape=jax.ShapeDtypeStruct((n, d), F32),
        compiler_params=_cparams(("parallel",)),
        name="mix_merge_ln",
    )(x, o_c, o_s, o_w, o_rw, bonus, g, z, z, gn, hsum, p_nsa, p_rwkv, w_out, ln)


def _xattn_kernel(x_ref, kt_ref, v_ref, wq_ref, wo_ref, ln_ref, rw_ref, o_ref, s_ref):
    x = x_ref[...]
    q = _dot(x.astype(BF16), wq_ref[...]).astype(BF16)
    outs = []
    for h in range(XATTN_HEADS):
        hs = slice(h * XATTN_HEAD, (h + 1) * XATTN_HEAD)
        s = _dot(q[:, hs], kt_ref[0, hs, :]) * (XATTN_HEAD ** -0.5)
        m = jnp.max(s, axis=-1, keepdims=True)
        p = jnp.exp(s - m)
        l = jnp.sum(p, axis=-1, keepdims=True)
        outs.append(_dot(p.astype(BF16), v_ref[0, :, hs]) * (1.0 / l))
    o = jnp.concatenate(outs, axis=1).astype(BF16)
    x2 = _layer_norm(DN_ALPHA * x + _dot(o, wo_ref[...]), ln_ref[0:1, :], ln_ref[1:2, :])
    o_ref[...] = x2
    hi = x2.astype(BF16)
    lo = (x2 - hi.astype(F32)).astype(BF16)
    logits = _dot(hi, rw_ref[0]) + (_dot(lo, rw_ref[0]) + _dot(hi, rw_ref[1]))
    s_ref[...] = _sigmoid(logits)


def _xattn(x, kt, v, wq, wo, ln, rw2, bsz, t, tm=256):
    n = bsz * t
    tps = t // tm
    d = D_MODEL

    def full(a):
        return pl.BlockSpec(a.shape, lambda i: (0,) * a.ndim)

    return pl.pallas_call(
        _xattn_kernel,
        grid=(n // tm,),
        in_specs=[pl.BlockSpec((tm, d), lambda i: (i, 0)),
                  pl.BlockSpec((1,) + kt.shape[1:], lambda i: (i // tps, 0, 0)),
                  pl.BlockSpec((1,) + v.shape[1:], lambda i: (i // tps, 0, 0)),
                  full(wq), full(wo), full(ln), full(rw2)],
        out_specs=[pl.BlockSpec((tm, d), lambda i: (i, 0)), pl.BlockSpec((tm, LANE), lambda i: (i, 0))],
        out_shape=[jax.ShapeDtypeStruct((n, d), F32), jax.ShapeDtypeStruct((n, LANE), F32)],
        compiler_params=_cparams(("parallel",)),
        name="xattn_ln_router",
    )(x, kt, v, wq, wo, ln, rw2)


def _expert_kernel(be_ref, x_ref, wg_ref, wu_ref, wd_ref, o_ref, wg_s, wu_s, wd_s):
    i = pl.program_id(0)

    @pl.when((i == 0) | (be_ref[i] != be_ref[jnp.maximum(i - 1, 0)]))
    def _():
        wg_s[...] = wg_ref[0].astype(BF16)
        wu_s[...] = wu_ref[0].astype(BF16)
        wd_s[...] = wd_ref[0].astype(BF16)

    x = x_ref[...]
    hg = _dot(x, wg_s[...])
    h = hg * _sigmoid(hg) * _dot(x, wu_s[...])
    o_ref[...] = _dot(h.astype(BF16), wd_s[...]).astype(o_ref.dtype)


def _experts(block_expert, rows, w_gate, w_up, w_down):
    n_rows, d = rows.shape
    ff = w_gate.shape[2]
    return pl.pallas_call(
        _expert_kernel,
        grid_spec=pltpu.PrefetchScalarGridSpec(
            num_scalar_prefetch=1,
            grid=(n_rows // MOE_BLK,),
            in_specs=[pl.BlockSpec((MOE_BLK, d), lambda i, be: (i, 0)),
                      pl.BlockSpec((1, d, ff), lambda i, be: (be[i], 0, 0)),
                      pl.BlockSpec((1, d, ff), lambda i, be: (be[i], 0, 0)),
                      pl.BlockSpec((1, ff, d), lambda i, be: (be[i], 0, 0))],
            out_specs=pl.BlockSpec((MOE_BLK, d), lambda i, be: (i, 0)),
            scratch_shapes=[pltpu.VMEM((d, ff), BF16), pltpu.VMEM((d, ff), BF16), pltpu.VMEM((ff, d), BF16)]),
        out_shape=jax.ShapeDtypeStruct((n_rows, d), BF16),
        compiler_params=_cparams(("arbitrary",)),
        name="moe_experts",
    )(block_expert, rows, w_gate, w_up, w_down)


def _combine_kernel(x_ref, ya_ref, yb_ref, gate_ref, ln_ref, o_ref):
    y = gate_ref[:, 0:1] * ya_ref[...] + gate_ref[:, 1:2] * yb_ref[...]
    o_ref[...] = _layer_norm(DN_ALPHA * x_ref[...] + y, ln_ref[0:1, :], ln_ref[1:2, :])


def _combine(x, ya, yb, gate, ln, tm=512):
    n, d = x.shape
    tok = pl.BlockSpec((tm, d), lambda i: (i, 0))
    return pl.pallas_call(
        _combine_kernel,
        grid=(n // tm,),
        in_specs=[tok, tok, tok, pl.BlockSpec((tm, TOP_K), lambda i: (i, 0)),
                  pl.BlockSpec(ln.shape, lambda i: (0, 0))],
        out_specs=tok,
        out_shape=jax.ShapeDtypeStruct((n, d), F32),
        compiler_params=_cparams(("parallel",)),
        name="moe_combine_ln",
    )(x, ya, yb, gate, ln)


def _t5_bucket(dist):
    n = jnp.maximum(dist, 0)
    max_exact = REL_BUCKETS // 2
    log_ratio = jnp.log(jnp.maximum(n, 1).astype(F32) / max_exact) / math.log(REL_MAX_DIST / max_exact)
    large = jnp.minimum(max_exact + (log_ratio * (REL_BUCKETS - max_exact)).astype(jnp.int32), REL_BUCKETS - 1)
    return jnp.where(n < max_exact, n, large)


def _bias_tables(rel_table, t):
    nc = t // CMP_STRIDE
    f = rel_table[_t5_bucket(jnp.arange(t + 2048))].astype(F32).T
    ql = jnp.arange(Q_BLOCK)[:, None]

    def lookup(dist, ok):
        vals = jnp.take(f, jnp.clip(dist, 0, f.shape[-1] - 1), axis=1)
        vals = jnp.where(ok[None], vals, NEG_INF)
        return vals.reshape(NSA_GROUPS, NSA_HPG * dist.shape[0], dist.shape[1])

    j = jnp.arange(2 * nc)[None, :]
    dist_c = CMP_STRIDE * (nc - 8 - j) + ql - (CMP_LEN - 1)
    pc = lookup(dist_c, dist_c >= 0)
    nvar = WINDOW // Q_BLOCK + 1

    def toeplitz(base, width, hi, shift):
        period = width + Q_BLOCK
        m = np.arange(period)
        dist = base - np.where(m < width, m, m - period)
        row = jnp.take(f, np.clip(dist, 0, f.shape[-1] - 1), axis=1) - shift
        row = jnp.where(((dist >= 0) & (dist < hi))[None], row, NEG_INF)
        flat = jnp.tile(row, (1, Q_BLOCK))[:, :Q_BLOCK * (period - 1)]
        return flat.reshape(NSA_HEADS, Q_BLOCK, period - 1)[:, :, :width]

    wb = jnp.stack([toeplitz(Q_BLOCK * v, WINDOW + Q_BLOCK, WINDOW, 0.0) for v in range(nvar)], axis=1)
    wb = wb.reshape(NSA_GROUPS, NSA_HPG, nvar, Q_BLOCK, -1).transpose(0, 2, 1, 3, 4)
    wb = wb.reshape(NSA_GROUPS, nvar, NSA_HPG * Q_BLOCK, -1)
    far = rel_table[REL_BUCKETS - 1].astype(F32)
    lt = toeplitz(SEL_NEAR + SEL_TK, SEL_NEAR + 2 * SEL_TK, f.shape[-1], far[:, None])
    lt = lt.reshape(NSA_GROUPS, NSA_HPG * Q_BLOCK, -1)
    far_hi = far.astype(BF16).astype(F32)
    lane = jnp.arange(LANE)[None, :]
    cq = jnp.where(lane == HEAD_DIM, far_hi[:, None], jnp.where(lane == HEAD_DIM + 1, (far - far_hi)[:, None], 0.0))
    cq = jnp.broadcast_to(cq[:, None, :], (NSA_HEADS, Q_BLOCK, LANE)).reshape(NSA_GROUPS, NSA_HPG * Q_BLOCK, LANE)
    return pc, wb, lt, cq


def _in_proj_perm():
    perm = np.full((IN_PAD,), -1, np.int64)
    for h in range(NSA_HEADS):
        perm[C_Q + h * LANE:C_Q + h * LANE + HEAD_DIM] = np.arange(HEAD_DIM) + h * HEAD_DIM
    rw0 = NSA_COLS
    g0 = NSA_COLS + RWKV_COLS
    perm[C_GN:C_GN + D_MODEL] = g0 + np.arange(D_MODEL)
    perm[C_GR:C_GR + D_MODEL] = g0 + D_MODEL + np.arange(D_MODEL)
    perm[C_R:C_R + RWKV_WIDTH] = rw0 + np.arange(RWKV_WIDTH)
    perm[C_K:C_K + RWKV_WIDTH] = rw0 + RWKV_WIDTH + np.arange(RWKV_WIDTH)
    perm[C_V:C_V + RWKV_WIDTH] = rw0 + 2 * RWKV_WIDTH + np.arange(RWKV_WIDTH)
    for g in range(NSA_GROUPS):
        for br in range(3):
            for h in range(NSA_HPG):
                perm[C_G3 + g * LANE + br * NSA_HPG + h] = NSA_WIDTH + 6 * NSA_KV + (g * NSA_HPG + h) * 3 + br
    perm[C_WA:C_WA + LORA_W + LORA_A] = rw0 + 3 * RWKV_WIDTH + np.arange(LORA_W + LORA_A)
    perm[C_ZG:C_ZG + LORA_G] = rw0 + 3 * RWKV_WIDTH + LORA_W + LORA_A + np.arange(LORA_G)
    return perm


_PERM = _in_proj_perm()


def _permute_in_proj(w):
    pieces = []
    start = 0
    while start < IN_PAD:
        stop = start + 1
        if _PERM[start] < 0:
            while stop < IN_PAD and _PERM[stop] < 0:
                stop += 1
            pieces.append(jnp.zeros((w.shape[0], stop - start), BF16))
        else:
            while stop < IN_PAD and _PERM[stop] == _PERM[stop - 1] + 1:
                stop += 1
            pieces.append(w[:, int(_PERM[start]):int(_PERM[stop - 1]) + 1].astype(BF16))
        start = stop
    return jnp.concatenate(pieces, axis=1)


def _pad_to(a, shape):
    return jnp.pad(a, [(0, s - d) for d, s in zip(a.shape, shape)])


def _route(s, router_bias, n_tok):
    s16 = s[:, :N_EXPERTS]
    s_sel = s16 + router_bias.astype(F32)

    def top2(a):
        i1 = jnp.argmax(a, axis=-1)
        rest = jnp.where(jnp.arange(a.shape[-1]) == i1[..., None], -jnp.inf, a)
        i2 = jnp.argmax(rest, axis=-1)
        return jnp.max(a, axis=-1) + jnp.max(rest, axis=-1), jnp.stack([i1, i2], axis=-1)

    group_score, _ = top2(s_sel.reshape(n_tok, N_EXPERT_GROUPS, EXPERTS_PER_GROUP))
    group = jnp.argmax(group_score, axis=-1)
    in_group = (jnp.arange(N_EXPERTS) // EXPERTS_PER_GROUP)[None, :] == group[:, None]
    _, e_idx = top2(jnp.where(in_group, s_sel, NEG_INF))
    s_top = jnp.take_along_axis(s16, e_idx, axis=-1)
    gate = s_top / jnp.sum(s_top, axis=-1, keepdims=True)
    e_flat = e_idx.reshape(-1)
    onehot = (e_flat[:, None] == jnp.arange(N_EXPERTS)[None, :]).astype(jnp.int32)
    rank = jnp.take_along_axis(jnp.cumsum(onehot, axis=0) - onehot, e_flat[:, None], axis=1)[:, 0]
    counts = jnp.sum(onehot, axis=0)
    padded = (counts + MOE_BLK - 1) // MOE_BLK * MOE_BLK
    ends = jnp.cumsum(padded)
    dest = (ends - padded)[e_flat] + rank
    n_rows = n_tok * TOP_K + N_EXPERTS * MOE_BLK
    tok_of_row = jnp.zeros((n_rows,), jnp.int32).at[dest].set(jnp.arange(n_tok * TOP_K, dtype=jnp.int32) // TOP_K)
    block_expert = jnp.minimum(
        jnp.searchsorted(ends, jnp.arange(n_rows // MOE_BLK) * MOE_BLK, side='right'), N_EXPERTS - 1).astype(jnp.int32)
    return gate, dest.reshape(n_tok, TOP_K), tok_of_row, block_expert


def kernel(x, mem, rel_table, router_w, router_bias, w_in, cmp_pe_k, cmp_pe_v, cmp_w1k, cmp_w2k, cmp_w1v, cmp_w2v, rwkv_mu, rwkv_w0, rwkv_w2, rwkv_a0, rwkv_a2, rwkv_g2, rwkv_kk, rwkv_ka, rwkv_rk, rwkv_gn_g, rwkv_gn_b, rwkv_v0, rwkv_v1, rwkv_v2, p_nsa, p_rwkv, w_out, ln1_g, ln1_b, xq_w, xk_w, xv_w, xo_w, ln2_g, ln2_b, moe_w_gate, moe_w_up, moe_w_down, ln3_g, ln3_b):
    bsz, t, d = x.shape
    n = bsz * t
    depth = w_in.shape[0]
    nc = t // CMP_STRIDE
    nsel = t // SEL_BLOCK
    g, hd = NSA_GROUPS, HEAD_DIM

    pc_tab, wb_tab, lt_tab, cq_tab = _bias_tables(rel_table, t)
    cmp_start = np.arange(nc) * CMP_STRIDE
    sel_start = np.arange(nsel) * SEL_BLOCK
    overlap = ((cmp_start[:, None] <= sel_start[None, :] + SEL_BLOCK - 1)
               & (cmp_start[:, None] + CMP_LEN - 1 >= sel_start[None, :]) & (cmp_start[:, None] < (nc - 1) * CMP_STRIDE))
    overlap = jnp.asarray(overlap.T, BF16)
    lane_head = np.arange(RWKV_WIDTH) // RWKV_HEAD
    hsum = jnp.asarray(lane_head[:, None] == lane_head[None, :], BF16)
    ones_bd = jnp.asarray((np.arange(LANE) // RWKV_HEAD)[:, None] == (np.arange(LANE) // RWKV_HEAD)[None, :], BF16)
    rw_f = _pad_to(router_w.astype(F32), (d, LANE))
    rw_hi = rw_f.astype(BF16)
    rw2 = jnp.stack([rw_hi, (rw_f - rw_hi.astype(F32)).astype(BF16)])

    xf = x.reshape(n, d)
    mem_f = mem.reshape(bsz * mem.shape[1], d)
    v_first = None
    for l in range(depth):
        w_in_l = _permute_in_proj(w_in[l])
        z = _matmul(xf, w_in_l, F32, 512, 1024)

        kv0 = NSA_WIDTH
        zeros64 = jnp.zeros((d, LANE - hd), BF16)
        w_kv = jnp.stack([jnp.concatenate(
            [piece for i in range(KV_ARRAYS)
             for piece in (w_in[l][:, kv0 + i * NSA_KV + gi * hd:kv0 + i * NSA_KV + (gi + 1) * hd].astype(BF16), zeros64)],
            axis=1) for gi in range(g)])
        kc, vc, ks_aug, vs_aug, kw_pad, vw_aug = _kv_proj(xf, w_kv, bsz, t)

        def cmp_w(pe, w1, w2):
            return (pe.reshape(2, CMP_STRIDE * hd), w1.reshape(2, CMP_STRIDE * hd, CMP_HIDDEN).astype(BF16),
                    _pad_to(w2, (CMP_HIDDEN, LANE)).astype(BF16))

        chunks = lambda a: a.reshape(bsz, g, nc, CMP_STRIDE * hd)
        k_cmp = _compress(chunks(kc), *cmp_w(cmp_pe_k[l], cmp_w1k[l], cmp_w2k[l]))
        v_cmp = _compress(chunks(vc), *cmp_w(cmp_pe_v[l], cmp_w1v[l], cmp_w2v[l]))
        o_c, sel_mask = _cmp_branch(z, k_cmp, v_cmp, pc_tab, overlap, bsz, t)
        o_s = _sel_branch(z, ks_aug, vs_aug, sel_mask, lt_tab, cq_tab, bsz, t)
        o_w = _win_branch(z, kw_pad, vw_aug, wb_tab, bsz, t)

        mu = rwkv_mu[l]
        w3 = 3 * RWKV_WIDTH
        mu5 = jnp.stack([mu[0:512], mu[512:1024], mu[1024:1536], _pad_to(mu[w3:w3 + 128], (512,)),
                         _pad_to(mu[w3 + 128:w3 + 256], (512,))])
        vecs = jnp.stack([rwkv_w0[l], rwkv_a0[l], rwkv_kk[l], rwkv_ka[l], rwkv_rk[l].reshape(-1)])
        w2p = jnp.concatenate([rwkv_w2[l], jnp.zeros_like(rwkv_a2[l])], axis=0).astype(BF16)
        a2p = jnp.concatenate([jnp.zeros_like(rwkv_w2[l]), rwkv_a2[l]], axis=0).astype(BF16)
        vres = None
        if l > 0:
            vres = (v_first, rwkv_v0[l - 1][None, :], _pad_to(rwkv_v1[l - 1], (RWKV_WIDTH, LANE)).astype(BF16),
                    _pad_to(rwkv_v2[l - 1], (LANE, RWKV_WIDTH)).astype(BF16))
        r, w, k, v, kk, b, g_out, bonus, vt = _rwkv_prep(z, mu5, vecs, w2p, a2p, rwkv_g2[l].astype(BF16), hsum, vres,
                                                          bsz, t)
        if l == 0:
            v_first = v
        seq = lambda a: a.reshape(bsz, t, RWKV_WIDTH)
        o_rw = _rwkv_scan(seq(r), seq(w), seq(k), seq(kk), seq(b), vt, ones_bd, bsz, t)
        o_rw = o_rw.transpose(1, 0, 2, 3).reshape(n, RWKV_WIDTH)

        xf = _merge(xf, o_c, o_s, o_w, o_rw, bonus, g_out, z, jnp.stack([rwkv_gn_g[l], rwkv_gn_b[l]]), hsum,
                    p_nsa[l].astype(BF16), p_rwkv[l].astype(BF16), w_out[l].astype(BF16),
                    jnp.stack([ln1_g[l], ln1_b[l]]), bsz, t)

        mlen = mem.shape[1]
        mk = _matmul(mem_f, xk_w[l].astype(BF16), BF16, mlen, XATTN_WIDTH)
        mv = _matmul(mem_f, xv_w[l].astype(BF16), BF16, mlen, XATTN_WIDTH)
        kt = mk.reshape(bsz, mlen, XATTN_WIDTH).transpose(0, 2, 1)
        xf, scores = _xattn(xf, kt, mv.reshape(bsz, mlen, XATTN_WIDTH), xq_w[l].astype(BF16), xo_w[l].astype(BF16),
                            jnp.stack([ln2_g[l], ln2_b[l]]), rw2, bsz, t)

        gate, dest, tok_of_row, block_expert = _route(scores, router_bias, n)
        rows = jnp.take(xf.astype(BF16), tok_of_row, axis=0)
        y_rows = _experts(block_expert, rows, moe_w_gate[l], moe_w_up[l], moe_w_down[l])
        xf = _combine(xf, jnp.take(y_rows, dest[:, 0], axis=0), jnp.take(y_rows, dest[:, 1], axis=0), gate,
                      jnp.stack([ln3_g[l], ln3_b[l]]))
    return xf.reshape(bsz, t, d)
```

```python
import functools
import math

import numpy as np
import jax
import jax.numpy as jnp
from jax import lax
from jax.experimental import pallas as pl
from jax.experimental.pallas import tpu as pltpu

F32 = jnp.float32
BF16 = jnp.bfloat16

D_MODEL = 1024
DEPTH = 2
NSA_HEADS = 8
NSA_GROUPS = 2
NSA_HPG = NSA_HEADS // NSA_GROUPS
HEAD_DIM = 64
NSA_WIDTH = NSA_HEADS * HEAD_DIM
NSA_KV = NSA_GROUPS * HEAD_DIM
CMP_STRIDE = 16
CMP_LEN = 2 * CMP_STRIDE
CMP_HIDDEN = 256
SEL_BLOCK = 64
SEL_TOPN = 16
WINDOW = 512
Q_BLOCK = 128
SEL_FORCE = 1e9
RWKV_HEADS = 8
RWKV_HEAD = 64
RWKV_WIDTH = RWKV_HEADS * RWKV_HEAD
LORA_W = 64
LORA_A = 64
LORA_V = 32
LORA_G = 128
GN_EPS = 64e-5
REL_BUCKETS = 32
REL_MAX_DIST = 1024
XATTN_HEADS = 4
XATTN_HEAD = 128
XATTN_WIDTH = XATTN_HEADS * XATTN_HEAD
N_EXPERTS = 16
N_EXPERT_GROUPS = 4
EXPERTS_PER_GROUP = N_EXPERTS // N_EXPERT_GROUPS
TOP_K = 2
EXPERT_FF = 512
DN_ALPHA = (2 * DEPTH) ** 0.25
LN_EPS = 1e-5
NEG_INF = -1e30
NSA_COLS = NSA_WIDTH + 6 * NSA_KV + 3 * NSA_HEADS
RWKV_COLS = 3 * RWKV_WIDTH + LORA_W + LORA_A + LORA_G

LANE = 128
VMEM_LIMIT = 56 * 1024 * 1024

C_Q = 0
C_GN = 1024
C_GR = 2048
C_R = 3072
C_K = 3584
C_V = 4096
C_G3 = 4608
C_WA = 4864
C_ZG = 4992
IN_PAD = 5120
KV_ARRAYS = 6

MOE_BLK = 256
SCAN_TT = 128
FAR_BIAS_DIST = 1280


def _cparams(sem):
    return pltpu.CompilerParams(dimension_semantics=sem, vmem_limit_bytes=VMEM_LIMIT)


def _sigmoid(x):
    return 1.0 / (1.0 + jnp.exp(-x))


def _layer_norm(v, g, b):
    mu = jnp.mean(v, axis=-1, keepdims=True)
    d = v - mu
    var = jnp.mean(d * d, axis=-1, keepdims=True)
    return d * lax.rsqrt(var + LN_EPS) * g + b


def _dot(a, b):
    return jnp.dot(a, b, preferred_element_type=F32)


def _dot_nt(a, b):
    return lax.dot_general(a, b, (((1,), (1,)), ((), ())), preferred_element_type=F32)


def _dot_split(x, w):
    hi = x.astype(BF16)
    lo = (x - hi.astype(F32)).astype(BF16)
    return _dot(hi, w) + _dot(lo, w)


def _mm_kernel(x_ref, w_ref, o_ref, xb_ref):
    @pl.when(pl.program_id(1) == 0)
    def _():
        xb_ref[...] = x_ref[...].astype(BF16)

    o_ref[...] = _dot(xb_ref[...], w_ref[...]).astype(o_ref.dtype)


def _matmul(x, w, out_dtype, tm, tn):
    n, k = x.shape
    m = w.shape[1]
    return pl.pallas_call(
        _mm_kernel,
        grid=(n // tm, m // tn),
        in_specs=[pl.BlockSpec((tm, k), lambda i, j: (i, 0)),
                  pl.BlockSpec((k, tn), lambda i, j: (0, j))],
        out_specs=pl.BlockSpec((tm, tn), lambda i, j: (i, j)),
        out_shape=jax.ShapeDtypeStruct((n, m), out_dtype),
        scratch_shapes=[pltpu.VMEM((tm, k), BF16)],
        compiler_params=_cparams(("parallel", "arbitrary")),
        name="matmul",
    )(x, w)


def _kv_proj_kernel(tiles_per_seq, x_ref, w_ref, kc_ref, vc_ref, ks_ref, vs_ref, kw_ref, vw_ref, xb_ref):
    @pl.when(pl.program_id(1) == 0)
    def _():
        xb_ref[...] = x_ref[...].astype(BF16)

    y = _dot(xb_ref[...], w_ref[0])
    tm = y.shape[0]
    slot = lambda i: y[:, i * LANE:(i + 1) * LANE]
    lane = lax.broadcasted_iota(jnp.int32, (tm, LANE), 1)
    one_lane = jnp.where(lane == HEAD_DIM, 1.0, 0.0)
    t0 = lax.rem(pl.program_id(0), tiles_per_seq) * tm
    block = jnp.right_shift(t0 + lax.broadcasted_iota(jnp.int32, (tm, LANE), 0), int(math.log2(SEL_BLOCK)))
    kc_ref[0, 0] = slot(0)[:, :HEAD_DIM]
    vc_ref[0, 0] = slot(1)[:, :HEAD_DIM]
    ks_ref[0, 0] = jnp.concatenate([slot(2) + jnp.where(lane == HEAD_DIM + 1, 1.0, one_lane),
                                    jnp.where(lane == block, 1.0, 0.0)], axis=1).astype(BF16)
    vs_ref[0, 0] = (slot(3) + one_lane).astype(BF16)
    kw_ref[0, 0] = slot(4).astype(BF16)
    vw_ref[0, 0] = (slot(5) + one_lane).astype(BF16)


def _kv_proj(x, w_kv, bsz, t, tm=512):
    n, d = x.shape
    tps = t // tm
    g = NSA_GROUPS

    def out(width, dtype):
        return (pl.BlockSpec((1, 1, tm, width), lambda i, j: (i // tps, j, i % tps, 0)),
                jax.ShapeDtypeStruct((bsz, g, t, width), dtype))

    outs = [out(HEAD_DIM, F32), out(HEAD_DIM, F32), out(2 * LANE, BF16), out(LANE, BF16), out(LANE, BF16),
            out(LANE, BF16)]
    return pl.pallas_call(
        functools.partial(_kv_proj_kernel, tps),
        grid=(n // tm, g),
        in_specs=[pl.BlockSpec((tm, d), lambda i, j: (i, 0)),
                  pl.BlockSpec((1, d, KV_ARRAYS * LANE), lambda i, j: (j, 0, 0))],
        out_specs=[o[0] for o in outs],
        out_shape=[o[1] for o in outs],
        scratch_shapes=[pltpu.VMEM((tm, d), BF16)],
        compiler_params=_cparams(("parallel", "arbitrary")),
        name="nsa_kv_proj",
    )(x, w_kv)


def _compress_kernel(u_ref, pe_ref, w1_ref, w2_ref, o_ref):
    u = u_ref[0, 0]
    a = _dot((u + pe_ref[0:1, :]).astype(BF16), w1_ref[0])
    b = _dot((u + pe_ref[1:2, :]).astype(BF16), w1_ref[1])
    nc = u.shape[0]
    h = a + pltpu.roll(b, nc - 1, 0)
    h = 0.5 * h * (1.0 + jnp.tanh(math.sqrt(2.0 / math.pi) * (h + 0.044715 * (h * h * h))))
    y = _dot(h.astype(BF16), w2_ref[...])
    row = lax.broadcasted_iota(jnp.int32, y.shape, 0)
    o_ref[0, 0] = jnp.where(row < nc - 1, y, 0.0)


def _compress(u, pe2, w1, w2p):
    b, g, nc, _ = u.shape
    return pl.pallas_call(
        _compress_kernel,
        grid=(b, g),
        in_specs=[pl.BlockSpec((1, 1, nc, 1024), lambda i, j: (i, j, 0, 0)),
                  pl.BlockSpec((2, 1024), lambda i, j: (0, 0)),
                  pl.BlockSpec((2, 1024, CMP_HIDDEN), lambda i, j: (0, 0, 0)),
                  pl.BlockSpec((CMP_HIDDEN, LANE), lambda i, j: (0, 0))],
        out_specs=pl.BlockSpec((1, 1, nc, LANE), lambda i, j: (i, j, 0, 0)),
        out_shape=jax.ShapeDtypeStruct((b, g, nc, LANE), F32),
        compiler_params=_cparams(("parallel", "parallel")),
        name="nsa_compress",
    )(u, pe2, w1, w2p)


def _stack_heads(q_ref):
    q = q_ref[...] * (HEAD_DIM ** -0.5)
    return jnp.concatenate([q[:, h * LANE:(h + 1) * LANE] for h in range(NSA_HPG)], axis=0).astype(BF16)


def _store_heads(o_ref, o, gates, branch):
    for h in range(NSA_HPG):
        c = branch * NSA_HPG + h
        o_ref[0, h] = (o[h * Q_BLOCK:(h + 1) * Q_BLOCK, :] * gates[:, c:c + 1])[:, :HEAD_DIM]


def _cmp_kernel(q_ref, zg_ref, kc_ref, vc_ref, pc_ref, ov_ref, o_ref, m_ref):
    c = pl.program_id(2)
    nc = kc_ref.shape[2]
    nsel = ov_ref.shape[0]
    qs = _stack_heads(q_ref)
    s = _dot_nt(qs, kc_ref[0, 0].astype(BF16))
    off = (nc - 8) - 8 * c
    bias = pltpu.roll(pc_ref[0], (2 * nc - off) % (2 * nc), 1)[:, :nc]
    s = s + bias
    m = jnp.max(s, axis=-1, keepdims=True)
    p = jnp.exp(s - m)
    l = jnp.sum(p, axis=-1, keepdims=True)
    ql = lax.broadcasted_iota(jnp.int32, (NSA_HPG * Q_BLOCK, 1), 0) & (Q_BLOCK - 1)
    any_key = (c * Q_BLOCK + ql) >= (CMP_LEN - 1)
    p = p * jnp.where(any_key, 1.0 / l, 0.0)
    gates = _sigmoid(zg_ref[...])
    _store_heads(o_ref, _dot(p.astype(BF16), vc_ref[0, 0].astype(BF16)), gates, 0)

    psum = p[0:Q_BLOCK] + p[Q_BLOCK:2 * Q_BLOCK] + p[2 * Q_BLOCK:3 * Q_BLOCK] + p[3 * Q_BLOCK:4 * Q_BLOCK]
    hi = psum.astype(BF16)
    lo = (psum - hi.astype(F32)).astype(BF16)
    imp = _dot_nt(ov_ref[...], hi) + _dot_nt(ov_ref[...], lo)
    jb = lax.broadcasted_iota(jnp.int32, (nsel, Q_BLOCK), 0)
    tq = c * Q_BLOCK + lax.broadcasted_iota(jnp.int32, (nsel, Q_BLOCK), 1)
    jq = jnp.right_shift(tq, int(math.log2(SEL_BLOCK)))
    forced = (jb == 0) | (jb == jq) | (jb == jq - 1)
    val = jnp.where(forced, SEL_FORCE, jnp.where(jb <= jq, imp, -SEL_FORCE))
    sel = jnp.zeros((nsel, Q_BLOCK), F32)
    jbf = jb.astype(F32)
    for _ in range(SEL_TOPN):
        mx = jnp.max(val, axis=0, keepdims=True)
        first = jnp.min(jnp.where(val == mx, jbf, float(nsel)), axis=0, keepdims=True)
        hit = jbf == first
        sel = jnp.where(hit, jnp.where(mx >= 0.0, 1.0, 0.0), sel)
        val = jnp.where(hit, -3e38, val)
    m_ref[0, 0] = (sel.T - 1.0) * (-NEG_INF)


def _cmp_branch(z, k_cmp, v_cmp, pc_tab, overlap, bsz, t):
    nq = t // Q_BLOCK
    nc = k_cmp.shape[2]
    nsel = overlap.shape[0]
    return pl.pallas_call(
        _cmp_kernel,
        grid=(bsz, NSA_GROUPS, nq),
        in_specs=[pl.BlockSpec((Q_BLOCK, NSA_HPG * LANE), lambda b, g, c: (b * nq + c, g)),
                  pl.BlockSpec((Q_BLOCK, LANE), lambda b, g, c: (b * nq + c, C_G3 // LANE + g)),
                  pl.BlockSpec((1, 1, nc, LANE), lambda b, g, c: (b, g, 0, 0)),
                  pl.BlockSpec((1, 1, nc, LANE), lambda b, g, c: (b, g, 0, 0)),
                  pl.BlockSpec((1, NSA_HPG * Q_BLOCK, 2 * nc), lambda b, g, c: (g, 0, 0)),
                  pl.BlockSpec((nsel, nc), lambda b, g, c: (0, 0))],
        out_specs=[pl.BlockSpec((1, NSA_HPG, Q_BLOCK, HEAD_DIM), lambda b, g, c: (b, g, c, 0)),
                   pl.BlockSpec((1, 1, Q_BLOCK, nsel), lambda b, g, c: (b, g, c, 0))],
        out_shape=[jax.ShapeDtypeStruct((bsz, NSA_HEADS, t, HEAD_DIM), F32),
                   jax.ShapeDtypeStruct((bsz, NSA_GROUPS, t, nsel), F32)],
        compiler_params=_cparams(("parallel", "parallel", "arbitrary")),
        name="nsa_cmp_select",
    )(z, z, k_cmp, v_cmp, pc_tab, overlap)


def _win_kernel(q_ref, zg_ref, k_ref, v_ref, wb_ref, o_ref, s_s, p_s):
    c = pl.program_id(2)
    start = pl.multiple_of(jnp.maximum(c * Q_BLOCK - WINDOW, 0), Q_BLOCK)
    qs = _stack_heads(q_ref)
    s_s[...] = _dot_nt(qs, k_ref[0, 0, pl.ds(start, WINDOW + Q_BLOCK), :])
    for r in range(NSA_HPG * Q_BLOCK // SEL_ROWS):
        rows = slice(r * SEL_ROWS, (r + 1) * SEL_ROWS)
        s = s_s[rows, :] + wb_ref[0, 0, rows, :]
        p_s[rows, :] = jnp.exp(s - jnp.max(s, axis=-1, keepdims=True)).astype(BF16)
    o = _dot(p_s[...], v_ref[0, 0, pl.ds(start, WINDOW + Q_BLOCK), :])
    o = o * (1.0 / o[:, HEAD_DIM:HEAD_DIM + 1])
    _store_heads(o_ref, o, _sigmoid(zg_ref[...]), 2)


def _win_branch(z, kw, vw, wb_tab, bsz, t):
    nq = t // Q_BLOCK
    nvar = wb_tab.shape[1]
    return pl.pallas_call(
        _win_kernel,
        grid=(bsz, NSA_GROUPS, nq),
        in_specs=[pl.BlockSpec((Q_BLOCK, NSA_HPG * LANE), lambda b, g, c: (b * nq + c, g)),
                  pl.BlockSpec((Q_BLOCK, LANE), lambda b, g, c: (b * nq + c, C_G3 // LANE + g)),
                  pl.BlockSpec((1, 1, t, LANE), lambda b, g, c: (b, g, 0, 0)),
                  pl.BlockSpec((1, 1, t, LANE), lambda b, g, c: (b, g, 0, 0)),
                  pl.BlockSpec((1, 1, NSA_HPG * Q_BLOCK, WINDOW + Q_BLOCK),
                               lambda b, g, c: (g, jnp.minimum(c, nvar - 1), 0, 0))],
        out_specs=pl.BlockSpec((1, NSA_HPG, Q_BLOCK, HEAD_DIM), lambda b, g, c: (b, g, c, 0)),
        out_shape=jax.ShapeDtypeStruct((bsz, NSA_HEADS, t, HEAD_DIM), F32),
        scratch_shapes=[pltpu.VMEM((NSA_HPG * Q_BLOCK, WINDOW + Q_BLOCK), F32),
                        pltpu.VMEM((NSA_HPG * Q_BLOCK, WINDOW + Q_BLOCK), BF16)],
        compiler_params=_cparams(("parallel", "parallel", "arbitrary")),
        name="nsa_window",
    )(z, z, kw, vw, wb_tab)


SEL_TK = 512


SEL_ROWS = 64
SEL_NEAR = FAR_BIAS_DIST + Q_BLOCK


def _sel_kernel(q_ref, zg_ref, k_ref, v_ref, mt_ref, lt_ref, cq_ref, o_ref, m_s, acc_s, s0_s, s1_s, p_s, a_s):
    c = pl.program_id(2)
    q = q_ref[...] * (HEAD_DIM ** -0.5)
    qs = jnp.concatenate([q[:, h * LANE:(h + 1) * LANE] for h in range(NSA_HPG)], axis=0) + cq_ref[0]
    mt = mt_ref[0, 0]
    q_aug = jnp.concatenate([qs, jnp.concatenate([mt] * NSA_HPG, axis=0)], axis=1).astype(BF16)
    m_s[...] = jnp.full(m_s.shape, NEG_INF, F32)
    acc_s[...] = jnp.zeros(acc_s.shape, F32)
    ncol = SEL_TK // LANE
    n_tiles = lax.div(c, SEL_TK // Q_BLOCK) + 1

    def scores(i, dst):
        i = jnp.minimum(i, n_tiles - 1)
        dst[...] = _dot_nt(q_aug, k_ref[0, 0, pl.ds(pl.multiple_of(i * SEL_TK, SEL_TK), SEL_TK), :])

    def tile(i, cur, nxt):
        scores(i + 1, nxt)
        off = pl.multiple_of(jnp.maximum(SEL_NEAR + SEL_TK - (c * Q_BLOCK - i * SEL_TK), 0), LANE)
        for r in range(NSA_HPG * Q_BLOCK // SEL_ROWS):
            rows = slice(r * SEL_ROWS, (r + 1) * SEL_ROWS)
            s = cur[rows, :] + lt_ref[0, rows, pl.ds(off, SEL_TK)]
            cols = [s[:, j * LANE:(j + 1) * LANE] for j in range(ncol)]
            mx = functools.reduce(jnp.maximum, cols)
            m_old = m_s[rows, :]
            m_new = jnp.maximum(m_old, jnp.max(mx, axis=-1, keepdims=True))
            m_s[rows, :] = m_new
            a_s[rows, :] = jnp.exp(m_old - m_new)
            p_s[rows, :] = jnp.concatenate([jnp.exp(col - m_new) for col in cols], axis=1).astype(BF16)
        k0 = pl.multiple_of(i * SEL_TK, SEL_TK)
        acc_s[...] = a_s[...] * acc_s[...] + _dot(p_s[...], v_ref[0, 0, pl.ds(k0, SEL_TK), :])

    scores(0, s0_s)

    def pair(j, carry):
        tile(2 * j, s0_s, s1_s)

        @pl.when(2 * j + 1 < n_tiles)
        def _():
            tile(2 * j + 1, s1_s, s0_s)

        return carry

    lax.fori_loop(0, lax.div(n_tiles + 1, 2), pair, 0)
    acc = acc_s[...]
    o = acc * (1.0 / acc[:, HEAD_DIM:HEAD_DIM + 1])
    _store_heads(o_ref, o, _sigmoid(zg_ref[...]), 1)


def _sel_branch(z, ks_aug, vs, mt, lt_tab, cq_tab, bsz, t):
    nq = t // Q_BLOCK
    nsel = mt.shape[-1]
    rows = NSA_HPG * Q_BLOCK
    return pl.pallas_call(
        _sel_kernel,
        grid=(bsz, NSA_GROUPS, nq),
        in_specs=[pl.BlockSpec((Q_BLOCK, NSA_HPG * LANE), lambda b, g, c: (b * nq + c, g)),
                  pl.BlockSpec((Q_BLOCK, LANE), lambda b, g, c: (b * nq + c, C_G3 // LANE + g)),
                  pl.BlockSpec((1, 1, t, 2 * LANE), lambda b, g, c: (b, g, 0, 0)),
                  pl.BlockSpec((1, 1, t, LANE), lambda b, g, c: (b, g, 0, 0)),
                  pl.BlockSpec((1, 1, Q_BLOCK, nsel), lambda b, g, c: (b, g, c, 0)),
                  pl.BlockSpec((1, rows, lt_tab.shape[2]), lambda b, g, c: (g, 0, 0)),
                  pl.BlockSpec((1, rows, LANE), lambda b, g, c: (g, 0, 0))],
        out_specs=pl.BlockSpec((1, NSA_HPG, Q_BLOCK, HEAD_DIM), lambda b, g, c: (b, g, c, 0)),
        out_shape=jax.ShapeDtypeStruct((bsz, NSA_HEADS, t, HEAD_DIM), F32),
        scratch_shapes=[pltpu.VMEM((rows, LANE), F32),
                        pltpu.VMEM((rows, LANE), F32),
                        pltpu.VMEM((rows, SEL_TK), F32),
                        pltpu.VMEM((rows, SEL_TK), F32),
                        pltpu.VMEM((rows, SEL_TK), BF16),
                        pltpu.VMEM((rows, LANE), F32)],
        compiler_params=_cparams(("parallel", "parallel", "arbitrary")),
        name="nsa_selected",
    )(z, z, ks_aug, vs, mt, lt_tab, cq_tab)


def _shift_mix(z_ref, prev_ref, mu, first):
    z = z_ref[...]
    prev = jnp.where(first, 0.0, prev_ref[7:8, :])
    row = lax.broadcasted_iota(jnp.int32, z.shape, 0)
    zp = jnp.where(row == 0, prev, pltpu.roll(z, 1, 0))
    return z + mu * (zp - z)


def _rwkv_prep_kernel(has_vres, tiles_per_seq, *refs):
    (zr, zk, zv, zwa, zg, pr, pk, pv, pwa, pg, mu_ref, vec_ref, w2_ref, a2_ref, g2_ref, hsum_ref) = refs[:16]
    pos = 16
    if has_vres:
        vf_ref, v0_ref, v1_ref, v2_ref = refs[pos:pos + 4]
        pos += 4
    r_o, w_o, k_o, v_o, kk_o, b_o, g_o, bonus_o, vt_o = refs[pos:]
    first = lax.rem(pl.program_id(0), tiles_per_seq) == 0
    r = _shift_mix(zr, pr, mu_ref[0:1, :], first)
    k = _shift_mix(zk, pk, mu_ref[1:2, :], first)
    v = _shift_mix(zv, pv, mu_ref[2:3, :], first)
    wa = _shift_mix(zwa, pwa, mu_ref[3:4, 0:LANE], first)
    zg_s = _shift_mix(zg, pg, mu_ref[4:5, 0:LANE], first)
    w0, a0, k_k, k_a, r_k = (vec_ref[i:i + 1, :] for i in range(5))
    if has_vres:
        lora = _dot(_dot(v.astype(BF16), v1_ref[...]).astype(BF16), v2_ref[...])
        v = v + (vf_ref[...] - v) * _sigmoid(v0_ref[...] + lora)
    u = w0 + _dot(jnp.tanh(wa).astype(BF16), w2_ref[...])
    decay = jnp.exp(-math.exp(-0.5) * _sigmoid(u))
    a = _sigmoid(a0 + _dot(wa.astype(BF16), a2_ref[...]))
    g = _dot(_sigmoid(zg_s).astype(BF16), g2_ref[...])
    kk = k * k_k
    kk = kk / jnp.maximum(jnp.sqrt(_dot_split(kk * kk, hsum_ref[...])), 1e-12)
    k = k * (1.0 + (a - 1.0) * k_a)
    r_o[...] = r
    w_o[...] = decay
    k_o[...] = k
    v_o[...] = v
    kk_o[...] = kk
    b_o[...] = kk * a
    g_o[...] = g
    bonus_o[...] = _dot_split(r * k * r_k, hsum_ref[...]) * v
    vt_o[0] = v.T.astype(BF16)


def _rwkv_prep(z, mu5, vecs, w2p, a2p, g2, hsum, vres, bsz, t, tm=256):
    n = bsz * t
    nt = n // tm
    tps = t // tm
    w = RWKV_WIDTH

    def cur(width, col):
        return pl.BlockSpec((tm, width), lambda i: (i, col // width))

    def prev(width, col):
        return pl.BlockSpec((8, width), lambda i: (jnp.maximum(i * (tm // 8) - 1, 0), col // width))

    def full(shape):
        return pl.BlockSpec(shape, lambda i: (0,) * len(shape))

    cols = [(w, C_R), (w, C_K), (w, C_V), (LANE, C_WA), (LANE, C_ZG)]
    in_specs = [cur(*c) for c in cols] + [prev(*c) for c in cols]
    in_specs += [full(mu5.shape), full(vecs.shape), full(w2p.shape), full(a2p.shape), full(g2.shape), full(hsum.shape)]
    args = [z] * 10 + [mu5, vecs, w2p, a2p, g2, hsum]
    if vres is not None:
        v_first, v0, v1p, v2p = vres
        in_specs += [pl.BlockSpec((tm, w), lambda i: (i, 0)), full(v0.shape), full(v1p.shape), full(v2p.shape)]
        args += [v_first, v0, v1p, v2p]
    tok = pl.BlockSpec((tm, w), lambda i: (i, 0))
    out_specs = [tok] * 8 + [pl.BlockSpec((1, w, tm), lambda i: (i // tps, 0, i % tps))]
    out_shape = [jax.ShapeDtypeStruct((n, w), F32)] * 8 + [jax.ShapeDtypeStruct((bsz, w, t), BF16)]
    return pl.pallas_call(
        functools.partial(_rwkv_prep_kernel, vres is not None, tps),
        grid=(nt,),
        in_specs=in_specs,
        out_specs=out_specs,
        out_shape=out_shape,
        compiler_params=_cparams(("parallel",)),
        name="rwkv_prep",
    )(*args)


QUAD = 4
QUAD_LANES = QUAD * RWKV_HEAD
SUB_STEPS = LANE // QUAD


def _scan_kernel(r_ref, w_ref, k_ref, kk_ref, b_ref, vt_ref, ones_ref, place_ref, o_ref, s_ref, vq_ref, kq_ref, vk_ref):
    bsz = r_ref.shape[0]
    tt = r_ref.shape[1]
    nh = RWKV_HEAD
    quads = [(bi, hf) for bi in range(bsz) for hf in range(RWKV_HEADS // QUAD)]

    @pl.when(pl.program_id(0) == 0)
    def _():
        s_ref[...] = jnp.zeros(s_ref.shape, F32)

    sub_shift, head_shift = int(math.log2(SUB_STEPS)), int(math.log2(nh))
    sub_lane = lax.broadcasted_iota(jnp.int32, (nh, LANE), 1) & (SUB_STEPS - 1)
    k_row_head = jnp.right_shift(lax.broadcasted_iota(jnp.int32, (LANE, QUAD_LANES), 0), sub_shift)
    k_lane_head = jnp.right_shift(lax.broadcasted_iota(jnp.int32, (LANE, QUAD_LANES), 1), head_shift)
    r_mask = (lax.broadcasted_iota(jnp.int32, (16, QUAD_LANES), 0)
              == jnp.right_shift(lax.broadcasted_iota(jnp.int32, (16, QUAD_LANES), 1), head_shift))
    ones_bd = ones_ref[...]

    def sub_block(sb, carry):
        s0 = pl.multiple_of(sb * SUB_STEPS, SUB_STEPS)
        for q, (bi, hf) in enumerate(quads):
            ln = slice(hf * QUAD_LANES, (hf + 1) * QUAD_LANES)
            vq = _dot(vt_ref[bi, (QUAD * hf) * nh:(QUAD * hf + 1) * nh, :], place_ref[sb * QUAD])
            for h in range(1, QUAD):
                vq = vq + _dot(vt_ref[bi, (QUAD * hf + h) * nh:(QUAD * hf + h + 1) * nh, :], place_ref[sb * QUAD + h])
            vq_ref[q] = vq.astype(BF16)
            k_sub = k_ref[bi, pl.ds(s0, SUB_STEPS), ln]
            kq_ref[q] = jnp.where(k_row_head == k_lane_head, jnp.concatenate([k_sub] * QUAD, axis=0), 0.0).astype(BF16)

        def group(tg, carry):
            t0 = pl.multiple_of(s0 + tg * 8, 8)
            rows = []
            for bi, hf in quads:
                ln = slice(hf * QUAD_LANES, (hf + 1) * QUAD_LANES)
                rows.append(tuple(ref[bi, pl.ds(t0, 8), ln] for ref in (w_ref, kk_ref, b_ref, r_ref)))
            onehots = [jnp.where(sub_lane == tg * 8 + u, 1.0, 0.0).astype(BF16) for u in range(8)]
            for q in range(len(quads)):
                vq = vq_ref[q]
                vk_ref[q] = _dot(jnp.concatenate([vq * oh for oh in onehots], axis=0), kq_ref[q])
            state = [s_ref[q] for q in range(len(quads))]
            for u in range(8):
                sk = _dot(jnp.concatenate([(state[q] * rows[q][1][u:u + 1, :]).astype(BF16)
                                           for q in range(len(quads))], axis=0), ones_bd)
                for q, (bi, hf) in enumerate(quads):
                    w8, _, b8, r8 = rows[q]
                    state[q] = (state[q] * w8[u:u + 1, :] - sk[q * nh:(q + 1) * nh, :] * b8[u:u + 1, :]
                                + vk_ref[q, u * nh:(u + 1) * nh, :])
                    r_lhs = jnp.where(r_mask, r8[u:u + 1, :], 0.0).astype(BF16)
                    o = _dot_nt(r_lhs, state[q].astype(BF16))
                    o_ref[t0 + u, bi, QUAD * hf:QUAD * (hf + 1), :] = o[0:QUAD, :]
            for q in range(len(quads)):
                s_ref[q] = state[q]
            return carry

        return lax.fori_loop(0, SUB_STEPS // 8, group, carry)

    lax.fori_loop(0, tt // SUB_STEPS, sub_block, 0)


def _rwkv_scan(r, w, k, kk, b, vt, ones_bd, place, bsz, t):
    tt = SCAN_TT
    wd = RWKV_WIDTH
    nquad = bsz * RWKV_HEADS // QUAD
    tok = pl.BlockSpec((bsz, tt, wd), lambda i: (0, i, 0))
    return pl.pallas_call(
        _scan_kernel,
        grid=(t // tt,),
        in_specs=[tok, tok, tok, tok, tok,
                  pl.BlockSpec((bsz, wd, tt), lambda i: (0, 0, i)),
                  pl.BlockSpec(ones_bd.shape, lambda i: (0, 0)),
                  pl.BlockSpec(place.shape, lambda i: (0, 0, 0))],
        out_specs=pl.BlockSpec((tt, bsz, RWKV_HEADS, RWKV_HEAD), lambda i: (i, 0, 0, 0)),
        out_shape=jax.ShapeDtypeStruct((t, bsz, RWKV_HEADS, RWKV_HEAD), F32),
        scratch_shapes=[pltpu.VMEM((nquad, RWKV_HEAD, QUAD_LANES), F32),
                        pltpu.VMEM((nquad, RWKV_HEAD, LANE), BF16),
                        pltpu.VMEM((nquad, LANE, QUAD_LANES), BF16),
                        pltpu.VMEM((nquad, 8 * RWKV_HEAD, QUAD_LANES), F32)],
        compiler_params=_cparams(("arbitrary",)),
        name="rwkv_scan",
    )(r, w, k, kk, b, vt, ones_bd, place)


def _merge_kernel(x_ref, oc_ref, os_ref, ow_ref, orw_ref, bonus_ref, g_ref, zgn_ref, zgr_ref, gn_ref, hsum_ref,
                  pn_ref, pr_ref, wo_ref, ln_ref, o_ref):
    y_n = jnp.zeros(o_ref.shape, F32)
    for h in range(NSA_HEADS):
        o_h = (oc_ref[0, h] + os_ref[0, h] + ow_ref[0, h]).astype(BF16)
        y_n = y_n + _dot(o_h, pn_ref[h * HEAD_DIM:(h + 1) * HEAD_DIM, :])
    o = orw_ref[...]
    mu = _dot_split(o, hsum_ref[...]) * (1.0 / RWKV_HEAD)
    d = o - mu
    var = _dot_split(d * d, hsum_ref[...]) * (1.0 / RWKV_HEAD)
    o = d * lax.rsqrt(var + GN_EPS) * gn_ref[0:1, :] + gn_ref[1:2, :]
    o = ((o + bonus_ref[...]) * g_ref[...]).astype(BF16)
    y1 = _sigmoid(zgn_ref[...]) * y_n + _sigmoid(zgr_ref[...]) * _dot(o, pr_ref[...])
    y = _dot(y1.astype(BF16), wo_ref[...])
    o_ref[...] = _layer_norm(DN_ALPHA * x_ref[...] + y, ln_ref[0:1, :], ln_ref[1:2, :])


def _merge(x, o_c, o_s, o_w, o_rw, bonus, g, z, gn, hsum, p_nsa, p_rwkv, w_out, ln, bsz, t, tm=256):
    n = bsz * t
    tps = t // tm
    d = D_MODEL
    w = RWKV_WIDTH
    head = pl.BlockSpec((1, NSA_HEADS, tm, HEAD_DIM), lambda i: (i // tps, 0, i % tps, 0))
    tok = pl.BlockSpec((tm, w), lambda i: (i, 0))

    def full(a):
        return pl.BlockSpec(a.shape, lambda i: (0,) * a.ndim)

    return pl.pallas_call(
        _merge_kernel,
        grid=(n // tm,),
        in_specs=[pl.BlockSpec((tm, d), lambda i: (i, 0)), head, head, head, tok, tok, tok,
                  pl.BlockSpec((tm, d), lambda i: (i, C_GN // d)),
                  pl.BlockSpec((tm, d), lambda i: (i, C_GR // d)),
                  full(gn), full(hsum), full(p_nsa), full(p_rwkv), full(w_out), full(ln)],
        out_specs=pl.BlockSpec((tm, d), lambda i: (i, 0)),
        out_shape=jax.ShapeDtypeStruct((n, d), F32),
        compiler_params=_cparams(("parallel",)),
        name="mix_merge_ln",
    )(x, o_c, o_s, o_w, o_rw, bonus, g, z, z, gn, hsum, p_nsa, p_rwkv, w_out, ln)


def _xattn_kernel(x_ref, kt_ref, v_ref, wq_ref, wo_ref, ln_ref, rw_ref, o_ref, s_ref):
    x = x_ref[...]
    q = _dot(x.astype(BF16), wq_ref[...]).astype(BF16)
    outs = []
    for h in range(XATTN_HEADS):
        hs = slice(h * XATTN_HEAD, (h + 1) * XATTN_HEAD)
        s = _dot(q[:, hs], kt_ref[0, hs, :]) * (XATTN_HEAD ** -0.5)
        m = jnp.max(s, axis=-1, keepdims=True)
        p = jnp.exp(s - m)
        l = jnp.sum(p, axis=-1, keepdims=True)
        outs.append(_dot(p.astype(BF16), v_ref[0, :, hs]) * (1.0 / l))
    o = jnp.concatenate(outs, axis=1).astype(BF16)
    x2 = _layer_norm(DN_ALPHA * x + _dot(o, wo_ref[...]), ln_ref[0:1, :], ln_ref[1:2, :])
    o_ref[...] = x2
    hi = x2.astype(BF16)
    lo = (x2 - hi.astype(F32)).astype(BF16)
    logits = _dot(hi, rw_ref[0]) + (_dot(lo, rw_ref[0]) + _dot(hi, rw_ref[1]))
    s_ref[...] = _sigmoid(logits)


def _xattn(x, kt, v, wq, wo, ln, rw2, bsz, t, tm=256):
    n = bsz * t
    tps = t // tm
    d = D_MODEL

    def full(a):
        return pl.BlockSpec(a.shape, lambda i: (0,) * a.ndim)

    return pl.pallas_call(
        _xattn_kernel,
        grid=(n // tm,),
        in_specs=[pl.BlockSpec((tm, d), lambda i: (i, 0)),
                  pl.BlockSpec((1,) + kt.shape[1:], lambda i: (i // tps, 0, 0)),
                  pl.BlockSpec((1,) + v.shape[1:], lambda i: (i // tps, 0, 0)),
                  full(wq), full(wo), full(ln), full(rw2)],
        out_specs=[pl.BlockSpec((tm, d), lambda i: (i, 0)), pl.BlockSpec((tm, LANE), lambda i: (i, 0))],
        out_shape=[jax.ShapeDtypeStruct((n, d), F32), jax.ShapeDtypeStruct((n, LANE), F32)],
        compiler_params=_cparams(("parallel",)),
        name="xattn_ln_router",
    )(x, kt, v, wq, wo, ln, rw2)


def _expert_kernel(be_ref, x_ref, wg_ref, wu_ref, wd_ref, o_ref, wg_s, wu_s, wd_s):
    i = pl.program_id(0)

    @pl.when((i == 0) | (be_ref[i] != be_ref[jnp.maximum(i - 1, 0)]))
    def _():
        wg_s[...] = wg_ref[0].astype(BF16)
        wu_s[...] = wu_ref[0].astype(BF16)
        wd_s[...] = wd_ref[0].astype(BF16)

    x = x_ref[...]
    hg = _dot(x, wg_s[...])
    h = hg * _sigmoid(hg) * _dot(x, wu_s[...])
    o_ref[...] = _dot(h.astype(BF16), wd_s[...]).astype(o_ref.dtype)


def _experts(block_expert, rows, w_gate, w_up, w_down):
    n_rows, d = rows.shape
    ff = w_gate.shape[2]
    return pl.pallas_call(
        _expert_kernel,
        grid_spec=pltpu.PrefetchScalarGridSpec(
            num_scalar_prefetch=1,
            grid=(n_rows // MOE_BLK,),
            in_specs=[pl.BlockSpec((MOE_BLK, d), lambda i, be: (i, 0)),
                      pl.BlockSpec((1, d, ff), lambda i, be: (be[i], 0, 0)),
                      pl.BlockSpec((1, d, ff), lambda i, be: (be[i], 0, 0)),
                      pl.BlockSpec((1, ff, d), lambda i, be: (be[i], 0, 0))],
            out_specs=pl.BlockSpec((MOE_BLK, d), lambda i, be: (i, 0)),
            scratch_shapes=[pltpu.VMEM((d, ff), BF16), pltpu.VMEM((d, ff), BF16), pltpu.VMEM((ff, d), BF16)]),
        out_shape=jax.ShapeDtypeStruct((n_rows, d), BF16),
        compiler_params=_cparams(("arbitrary",)),
        name="moe_experts",
    )(block_expert, rows, w_gate, w_up, w_down)


def _combine_kernel(x_ref, ya_ref, yb_ref, gate_ref, ln_ref, o_ref):
    y = gate_ref[:, 0:1] * ya_ref[...] + gate_ref[:, 1:2] * yb_ref[...]
    o_ref[...] = _layer_norm(DN_ALPHA * x_ref[...] + y, ln_ref[0:1, :], ln_ref[1:2, :])


def _combine(x, ya, yb, gate, ln, tm=512):
    n, d = x.shape
    tok = pl.BlockSpec((tm, d), lambda i: (i, 0))
    return pl.pallas_call(
        _combine_kernel,
        grid=(n // tm,),
        in_specs=[tok, tok, tok, pl.BlockSpec((tm, TOP_K), lambda i: (i, 0)),
                  pl.BlockSpec(ln.shape, lambda i: (0, 0))],
        out_specs=tok,
        out_shape=jax.ShapeDtypeStruct((n, d), F32),
        compiler_params=_cparams(("parallel",)),
        name="moe_combine_ln",
    )(x, ya, yb, gate, ln)


def _t5_bucket(dist):
    n = jnp.maximum(dist, 0)
    max_exact = REL_BUCKETS // 2
    log_ratio = jnp.log(jnp.maximum(n, 1).astype(F32) / max_exact) / math.log(REL_MAX_DIST / max_exact)
    large = jnp.minimum(max_exact + (log_ratio * (REL_BUCKETS - max_exact)).astype(jnp.int32), REL_BUCKETS - 1)
    return jnp.where(n < max_exact, n, large)


def _bias_tables(rel_table, t):
    nc = t // CMP_STRIDE
    f = rel_table[_t5_bucket(jnp.arange(t + 2048))].astype(F32).T
    ql = jnp.arange(Q_BLOCK)[:, None]

    def lookup(dist, ok):
        vals = jnp.take(f, jnp.clip(dist, 0, f.shape[-1] - 1), axis=1)
        vals = jnp.where(ok[None], vals, NEG_INF)
        return vals.reshape(NSA_GROUPS, NSA_HPG * dist.shape[0], dist.shape[1])

    j = jnp.arange(2 * nc)[None, :]
    dist_c = CMP_STRIDE * (nc - 8 - j) + ql - (CMP_LEN - 1)
    pc = lookup(dist_c, dist_c >= 0)
    nvar = WINDOW // Q_BLOCK + 1

    def toeplitz(base, width, hi, shift):
        period = width + Q_BLOCK
        m = np.arange(period)
        dist = base - np.where(m < width, m, m - period)
        row = jnp.take(f, np.clip(dist, 0, f.shape[-1] - 1), axis=1) - shift
        row = jnp.where(((dist >= 0) & (dist < hi))[None], row, NEG_INF)
        flat = jnp.tile(row, (1, Q_BLOCK))[:, :Q_BLOCK * (period - 1)]
        return flat.reshape(NSA_HEADS, Q_BLOCK, period - 1)[:, :, :width]

    wb = jnp.stack([toeplitz(Q_BLOCK * v, WINDOW + Q_BLOCK, WINDOW, 0.0) for v in range(nvar)], axis=1)
    wb = wb.reshape(NSA_GROUPS, NSA_HPG, nvar, Q_BLOCK, -1).transpose(0, 2, 1, 3, 4)
    wb = wb.reshape(NSA_GROUPS, nvar, NSA_HPG * Q_BLOCK, -1)
    far = rel_table[REL_BUCKETS - 1].astype(F32)
    lt = toeplitz(SEL_NEAR + SEL_TK, SEL_NEAR + 2 * SEL_TK, f.shape[-1], far[:, None])
    lt = lt.reshape(NSA_GROUPS, NSA_HPG * Q_BLOCK, -1)
    far_hi = far.astype(BF16).astype(F32)
    lane = jnp.arange(LANE)[None, :]
    cq = jnp.where(lane == HEAD_DIM, far_hi[:, None], jnp.where(lane == HEAD_DIM + 1, (far - far_hi)[:, None], 0.0))
    cq = jnp.broadcast_to(cq[:, None, :], (NSA_HEADS, Q_BLOCK, LANE)).reshape(NSA_GROUPS, NSA_HPG * Q_BLOCK, LANE)
    return pc, wb, lt, cq


def _in_proj_perm():
    perm = np.full((IN_PAD,), -1, np.int64)
    for h in range(NSA_HEADS):
        perm[C_Q + h * LANE:C_Q + h * LANE + HEAD_DIM] = np.arange(HEAD_DIM) + h * HEAD_DIM
    rw0 = NSA_COLS
    g0 = NSA_COLS + RWKV_COLS
    perm[C_GN:C_GN + D_MODEL] = g0 + np.arange(D_MODEL)
    perm[C_GR:C_GR + D_MODEL] = g0 + D_MODEL + np.arange(D_MODEL)
    perm[C_R:C_R + RWKV_WIDTH] = rw0 + np.arange(RWKV_WIDTH)
    perm[C_K:C_K + RWKV_WIDTH] = rw0 + RWKV_WIDTH + np.arange(RWKV_WIDTH)
    perm[C_V:C_V + RWKV_WIDTH] = rw0 + 2 * RWKV_WIDTH + np.arange(RWKV_WIDTH)
    for g in range(NSA_GROUPS):
        for br in range(3):
            for h in range(NSA_HPG):
                perm[C_G3 + g * LANE + br * NSA_HPG + h] = NSA_WIDTH + 6 * NSA_KV + (g * NSA_HPG + h) * 3 + br
    perm[C_WA:C_WA + LORA_W + LORA_A] = rw0 + 3 * RWKV_WIDTH + np.arange(LORA_W + LORA_A)
    perm[C_ZG:C_ZG + LORA_G] = rw0 + 3 * RWKV_WIDTH + LORA_W + LORA_A + np.arange(LORA_G)
    return perm


_PERM = _in_proj_perm()


def _permute_in_proj(w):
    pieces = []
    start = 0
    while start < IN_PAD:
        stop = start + 1
        if _PERM[start] < 0:
            while stop < IN_PAD and _PERM[stop] < 0:
                stop += 1
            pieces.append(jnp.zeros((w.shape[0], stop - start), BF16))
        else:
            while stop < IN_PAD and _PERM[stop] == _PERM[stop - 1] + 1:
                stop += 1
            pieces.append(w[:, int(_PERM[start]):int(_PERM[stop - 1]) + 1].astype(BF16))
        start = stop
    return jnp.concatenate(pieces, axis=1)


def _pad_to(a, shape):
    return jnp.pad(a, [(0, s - d) for d, s in zip(a.shape, shape)])


def _route(s, router_bias, n_tok):
    s16 = s[:, :N_EXPERTS]
    s_sel = s16 + router_bias.astype(F32)

    def top2(a):
        i1 = jnp.argmax(a, axis=-1)
        rest = jnp.where(jnp.arange(a.shape[-1]) == i1[..., None], -jnp.inf, a)
        i2 = jnp.argmax(rest, axis=-1)
        return jnp.max(a, axis=-1) + jnp.max(rest, axis=-1), jnp.stack([i1, i2], axis=-1)

    group_score, _ = top2(s_sel.reshape(n_tok, N_EXPERT_GROUPS, EXPERTS_PER_GROUP))
    group = jnp.argmax(group_score, axis=-1)
    in_group = (jnp.arange(N_EXPERTS) // EXPERTS_PER_GROUP)[None, :] == group[:, None]
    _, e_idx = top2(jnp.where(in_group, s_sel, NEG_INF))
    s_top = jnp.take_along_axis(s16, e_idx, axis=-1)
    gate = s_top / jnp.sum(s_top, axis=-1, keepdims=True)
    e_flat = e_idx.reshape(-1)
    onehot = (e_flat[:, None] == jnp.arange(N_EXPERTS)[None, :]).astype(jnp.int32)
    rank = jnp.take_along_axis(jnp.cumsum(onehot, axis=0) - onehot, e_flat[:, None], axis=1)[:, 0]
    counts = jnp.sum(onehot, axis=0)
    padded = (counts + MOE_BLK - 1) // MOE_BLK * MOE_BLK
    ends = jnp.cumsum(padded)
    dest = (ends - padded)[e_flat] + rank
    n_rows = n_tok * TOP_K + N_EXPERTS * MOE_BLK
    tok_of_row = jnp.zeros((n_rows,), jnp.int32).at[dest].set(jnp.arange(n_tok * TOP_K, dtype=jnp.int32) // TOP_K)
    block_expert = jnp.minimum(
        jnp.searchsorted(ends, jnp.arange(n_rows // MOE_BLK) * MOE_BLK, side='right'), N_EXPERTS - 1).astype(jnp.int32)
    return gate, dest.reshape(n_tok, TOP_K), tok_of_row, block_expert


def kernel(x, mem, rel_table, router_w, router_bias, w_in, cmp_pe_k, cmp_pe_v, cmp_w1k, cmp_w2k, cmp_w1v, cmp_w2v, rwkv_mu, rwkv_w0, rwkv_w2, rwkv_a0, rwkv_a2, rwkv_g2, rwkv_kk, rwkv_ka, rwkv_rk, rwkv_gn_g, rwkv_gn_b, rwkv_v0, rwkv_v1, rwkv_v2, p_nsa, p_rwkv, w_out, ln1_g, ln1_b, xq_w, xk_w, xv_w, xo_w, ln2_g, ln2_b, moe_w_gate, moe_w_up, moe_w_down, ln3_g, ln3_b):
    bsz, t, d = x.shape
    n = bsz * t
    depth = w_in.shape[0]
    nc = t // CMP_STRIDE
    nsel = t // SEL_BLOCK
    g, hd = NSA_GROUPS, HEAD_DIM

    pc_tab, wb_tab, lt_tab, cq_tab = _bias_tables(rel_table, t)
    cmp_start = np.arange(nc) * CMP_STRIDE
    sel_start = np.arange(nsel) * SEL_BLOCK
    overlap = ((cmp_start[:, None] <= sel_start[None, :] + SEL_BLOCK - 1)
               & (cmp_start[:, None] + CMP_LEN - 1 >= sel_start[None, :]) & (cmp_start[:, None] < (nc - 1) * CMP_STRIDE))
    overlap = jnp.asarray(overlap.T, BF16)
    lane_head = np.arange(RWKV_WIDTH) // RWKV_HEAD
    hsum = jnp.asarray(lane_head[:, None] == lane_head[None, :], BF16)
    quad_head = np.arange(QUAD_LANES) // RWKV_HEAD
    ones_bd = jnp.asarray(quad_head[:, None] == quad_head[None, :], BF16)
    step = np.arange(SCAN_TT)[None, :, None]
    lane = np.arange(LANE)[None, None, :]
    sbh = np.arange(SCAN_TT // SUB_STEPS * QUAD)[:, None, None]
    place = jnp.asarray((lane // SUB_STEPS == sbh % QUAD) & (step == (sbh // QUAD) * SUB_STEPS + lane % SUB_STEPS), BF16)
    rw_f = _pad_to(router_w.astype(F32), (d, LANE))
    rw_hi = rw_f.astype(BF16)
    rw2 = jnp.stack([rw_hi, (rw_f - rw_hi.astype(F32)).astype(BF16)])

    xf = x.reshape(n, d)
    mem_f = mem.reshape(bsz * mem.shape[1], d)
    v_first = None
    for l in range(depth):
        w_in_l = _permute_in_proj(w_in[l])
        z = _matmul(xf, w_in_l, F32, 512, 1024)

        kv0 = NSA_WIDTH
        zeros64 = jnp.zeros((d, LANE - hd), BF16)
        w_kv = jnp.stack([jnp.concatenate(
            [piece for i in range(KV_ARRAYS)
             for piece in (w_in[l][:, kv0 + i * NSA_KV + gi * hd:kv0 + i * NSA_KV + (gi + 1) * hd].astype(BF16), zeros64)],
            axis=1) for gi in range(g)])
        kc, vc, ks_aug, vs_aug, kw_pad, vw_aug = _kv_proj(xf, w_kv, bsz, t)

        def cmp_w(pe, w1, w2):
            return (pe.reshape(2, CMP_STRIDE * hd), w1.reshape(2, CMP_STRIDE * hd, CMP_HIDDEN).astype(BF16),
                    _pad_to(w2, (CMP_HIDDEN, LANE)).astype(BF16))

        chunks = lambda a: a.reshape(bsz, g, nc, CMP_STRIDE * hd)
        k_cmp = _compress(chunks(kc), *cmp_w(cmp_pe_k[l], cmp_w1k[l], cmp_w2k[l]))
        v_cmp = _compress(chunks(vc), *cmp_w(cmp_pe_v[l], cmp_w1v[l], cmp_w2v[l]))
        o_c, sel_mask = _cmp_branch(z, k_cmp, v_cmp, pc_tab, overlap, bsz, t)
        o_s = _sel_branch(z, ks_aug, vs_aug, sel_mask, lt_tab, cq_tab, bsz, t)
        o_w = _win_branch(z, kw_pad, vw_aug, wb_tab, bsz, t)

        mu = rwkv_mu[l]
        w3 = 3 * RWKV_WIDTH
        mu5 = jnp.stack([mu[0:512], mu[512:1024], mu[1024:1536], _pad_to(mu[w3:w3 + 128], (512,)),
                         _pad_to(mu[w3 + 128:w3 + 256], (512,))])
        vecs = jnp.stack([rwkv_w0[l], rwkv_a0[l], rwkv_kk[l], rwkv_ka[l], rwkv_rk[l].reshape(-1)])
        w2p = jnp.concatenate([rwkv_w2[l], jnp.zeros_like(rwkv_a2[l])], axis=0).astype(BF16)
        a2p = jnp.concatenate([jnp.zeros_like(rwkv_w2[l]), rwkv_a2[l]], axis=0).astype(BF16)
        vres = None
        if l > 0:
            vres = (v_first, rwkv_v0[l - 1][None, :], _pad_to(rwkv_v1[l - 1], (RWKV_WIDTH, LANE)).astype(BF16),
                    _pad_to(rwkv_v2[l - 1], (LANE, RWKV_WIDTH)).astype(BF16))
        r, w, k, v, kk, b, g_out, bonus, vt = _rwkv_prep(z, mu5, vecs, w2p, a2p, rwkv_g2[l].astype(BF16), hsum, vres,
                                                          bsz, t)
        if l == 0:
            v_first = v
        seq = lambda a: a.reshape(bsz, t, RWKV_WIDTH)
        o_rw = _rwkv_scan(seq(r), seq(w), seq(k), seq(kk), seq(b), vt, ones_bd, place, bsz, t)
        o_rw = o_rw.transpose(1, 0, 2, 3).reshape(n, RWKV_WIDTH)

        xf = _merge(xf, o_c, o_s, o_w, o_rw, bonus, g_out, z, jnp.stack([rwkv_gn_g[l], rwkv_gn_b[l]]), hsum,
                    p_nsa[l].astype(BF16), p_rwkv[l].astype(BF16), w_out[l].astype(BF16),
                    jnp.stack([ln1_g[l], ln1_b[l]]), bsz, t)

        mlen = mem.shape[1]
        mk = _matmul(mem_f, xk_w[l].astype(BF16), BF16, mlen, XATTN_WIDTH)
        mv = _matmul(mem_f, xv_w[l].astype(BF16), BF16, mlen, XATTN_WIDTH)
        kt = mk.reshape(bsz, mlen, XATTN_WIDTH).transpose(0, 2, 1)
        xf, scores = _xattn(xf, kt, mv.reshape(bsz, mlen, XATTN_WIDTH), xq_w[l].astype(BF16), xo_w[l].astype(BF16),
                            jnp.stack([ln2_g[l], ln2_b[l]]), rw2, bsz, t)

        gate, dest, tok_of_row, block_expert = _route(scores, router_bias, n)
        rows = jnp.take(xf.astype(BF16), tok_of_row, axis=0)
        y_rows = _experts(block_expert, rows, moe_w_gate[l], moe_w_up[l], moe_w_down[l])
        xf = _combine(xf, jnp.take(y_rows, dest[:, 0], axis=0), jnp.take(y_rows, dest[:, 1], axis=0), gate,
                      jnp.stack([ln3_g[l], ln3_b[l]]))
    return xf.reshape(bsz, t, d)
```

```python
import functools
import math

import numpy as np
import jax
import jax.numpy as jnp
from jax import lax
from jax.experimental import pallas as pl
from jax.experimental.pallas import tpu as pltpu

F32 = jnp.float32
BF16 = jnp.bfloat16

D_MODEL = 1024
DEPTH = 2
NSA_HEADS = 8
NSA_GROUPS = 2
NSA_HPG = NSA_HEADS // NSA_GROUPS
HEAD_DIM = 64
NSA_WIDTH = NSA_HEADS * HEAD_DIM
NSA_KV = NSA_GROUPS * HEAD_DIM
CMP_STRIDE = 16
CMP_LEN = 2 * CMP_STRIDE
CMP_HIDDEN = 256
SEL_BLOCK = 64
SEL_TOPN = 16
WINDOW = 512
Q_BLOCK = 128
SEL_FORCE = 1e9
RWKV_HEADS = 8
RWKV_HEAD = 64
RWKV_WIDTH = RWKV_HEADS * RWKV_HEAD
LORA_W = 64
LORA_A = 64
LORA_V = 32
LORA_G = 128
GN_EPS = 64e-5
REL_BUCKETS = 32
REL_MAX_DIST = 1024
XATTN_HEADS = 4
XATTN_HEAD = 128
XATTN_WIDTH = XATTN_HEADS * XATTN_HEAD
N_EXPERTS = 16
N_EXPERT_GROUPS = 4
EXPERTS_PER_GROUP = N_EXPERTS // N_EXPERT_GROUPS
TOP_K = 2
EXPERT_FF = 512
DN_ALPHA = (2 * DEPTH) ** 0.25
LN_EPS = 1e-5
NEG_INF = -1e30
NSA_COLS = NSA_WIDTH + 6 * NSA_KV + 3 * NSA_HEADS
RWKV_COLS = 3 * RWKV_WIDTH + LORA_W + LORA_A + LORA_G

LANE = 128
VMEM_LIMIT = 56 * 1024 * 1024

C_Q = 0
C_GN = 1024
C_GR = 2048
C_R = 3072
C_K = 3584
C_V = 4096
C_G3 = 4608
C_WA = 4864
C_ZG = 4992
IN_PAD = 5120
KV_ARRAYS = 6

MOE_BLK = 256
SCAN_TT = 128
FAR_BIAS_DIST = 1280


def _cparams(sem):
    return pltpu.CompilerParams(dimension_semantics=sem, vmem_limit_bytes=VMEM_LIMIT)


def _sigmoid(x):
    return 1.0 / (1.0 + jnp.exp(-x))


def _layer_norm(v, g, b):
    mu = jnp.mean(v, axis=-1, keepdims=True)
    d = v - mu
    var = jnp.mean(d * d, axis=-1, keepdims=True)
    return d * lax.rsqrt(var + LN_EPS) * g + b


def _dot(a, b):
    return jnp.dot(a, b, preferred_element_type=F32)


def _dot_nt(a, b):
    return lax.dot_general(a, b, (((1,), (1,)), ((), ())), preferred_element_type=F32)


def _dot_split(x, w):
    hi = x.astype(BF16)
    lo = (x - hi.astype(F32)).astype(BF16)
    return _dot(hi, w) + _dot(lo, w)


def _mm_kernel(x_ref, w_ref, o_ref, xb_ref):
    @pl.when(pl.program_id(1) == 0)
    def _():
        xb_ref[...] = x_ref[...].astype(BF16)

    o_ref[...] = _dot(xb_ref[...], w_ref[...]).astype(o_ref.dtype)


def _matmul(x, w, out_dtype, tm, tn):
    n, k = x.shape
    m = w.shape[1]
    return pl.pallas_call(
        _mm_kernel,
        grid=(n // tm, m // tn),
        in_specs=[pl.BlockSpec((tm, k), lambda i, j: (i, 0)),
                  pl.BlockSpec((k, tn), lambda i, j: (0, j))],
        out_specs=pl.BlockSpec((tm, tn), lambda i, j: (i, j)),
        out_shape=jax.ShapeDtypeStruct((n, m), out_dtype),
        scratch_shapes=[pltpu.VMEM((tm, k), BF16)],
        compiler_params=_cparams(("parallel", "arbitrary")),
        name="matmul",
    )(x, w)


def _kv_proj_kernel(tiles_per_seq, x_ref, w_ref, kc_ref, vc_ref, ks_ref, vs_ref, kw_ref, vw_ref, xb_ref):
    @pl.when(pl.program_id(1) == 0)
    def _():
        xb_ref[...] = x_ref[...].astype(BF16)

    y = _dot(xb_ref[...], w_ref[0])
    tm = y.shape[0]
    slot = lambda i: y[:, i * LANE:(i + 1) * LANE]
    lane = lax.broadcasted_iota(jnp.int32, (tm, LANE), 1)
    one_lane = jnp.where(lane == HEAD_DIM, 1.0, 0.0)
    t0 = lax.rem(pl.program_id(0), tiles_per_seq) * tm
    block = jnp.right_shift(t0 + lax.broadcasted_iota(jnp.int32, (tm, LANE), 0), int(math.log2(SEL_BLOCK)))
    kc_ref[0, 0] = slot(0)[:, :HEAD_DIM]
    vc_ref[0, 0] = slot(1)[:, :HEAD_DIM]
    ks_ref[0, 0] = jnp.concatenate([slot(2) + jnp.where(lane == HEAD_DIM + 1, 1.0, one_lane),
                                    jnp.where(lane == block, 1.0, 0.0)], axis=1).astype(BF16)
    vs_ref[0, 0] = (slot(3) + one_lane).astype(BF16)
    kw_ref[0, 0] = slot(4).astype(BF16)
    vw_ref[0, 0] = (slot(5) + one_lane).astype(BF16)


def _kv_proj(x, w_kv, bsz, t, tm=512):
    n, d = x.shape
    tps = t // tm
    g = NSA_GROUPS

    def out(width, dtype):
        return (pl.BlockSpec((1, 1, tm, width), lambda i, j: (i // tps, j, i % tps, 0)),
                jax.ShapeDtypeStruct((bsz, g, t, width), dtype))

    outs = [out(HEAD_DIM, F32), out(HEAD_DIM, F32), out(2 * LANE, BF16), out(LANE, BF16), out(LANE, BF16),
            out(LANE, BF16)]
    return pl.pallas_call(
        functools.partial(_kv_proj_kernel, tps),
        grid=(n // tm, g),
        in_specs=[pl.BlockSpec((tm, d), lambda i, j: (i, 0)),
                  pl.BlockSpec((1, d, KV_ARRAYS * LANE), lambda i, j: (j, 0, 0))],
        out_specs=[o[0] for o in outs],
        out_shape=[o[1] for o in outs],
        scratch_shapes=[pltpu.VMEM((tm, d), BF16)],
        compiler_params=_cparams(("parallel", "arbitrary")),
        name="nsa_kv_proj",
    )(x, w_kv)


def _compress_kernel(u_ref, pe_ref, w1_ref, w2_ref, o_ref):
    u = u_ref[0, 0]
    a = _dot((u + pe_ref[0:1, :]).astype(BF16), w1_ref[0])
    b = _dot((u + pe_ref[1:2, :]).astype(BF16), w1_ref[1])
    nc = u.shape[0]
    h = a + pltpu.roll(b, nc - 1, 0)
    h = 0.5 * h * (1.0 + jnp.tanh(math.sqrt(2.0 / math.pi) * (h + 0.044715 * (h * h * h))))
    y = _dot(h.astype(BF16), w2_ref[...])
    row = lax.broadcasted_iota(jnp.int32, y.shape, 0)
    o_ref[0, 0] = jnp.where(row < nc - 1, y, 0.0)


def _compress(u, pe2, w1, w2p):
    b, g, nc, _ = u.shape
    return pl.pallas_call(
        _compress_kernel,
        grid=(b, g),
        in_specs=[pl.BlockSpec((1, 1, nc, 1024), lambda i, j: (i, j, 0, 0)),
                  pl.BlockSpec((2, 1024), lambda i, j: (0, 0)),
                  pl.BlockSpec((2, 1024, CMP_HIDDEN), lambda i, j: (0, 0, 0)),
                  pl.BlockSpec((CMP_HIDDEN, LANE), lambda i, j: (0, 0))],
        out_specs=pl.BlockSpec((1, 1, nc, LANE), lambda i, j: (i, j, 0, 0)),
        out_shape=jax.ShapeDtypeStruct((b, g, nc, LANE), F32),
        compiler_params=_cparams(("parallel", "parallel")),
        name="nsa_compress",
    )(u, pe2, w1, w2p)


def _stack_heads(q_ref):
    q = q_ref[...] * (HEAD_DIM ** -0.5)
    return jnp.concatenate([q[:, h * LANE:(h + 1) * LANE] for h in range(NSA_HPG)], axis=0).astype(BF16)


def _store_heads(o_ref, o, gates, branch):
    for h in range(NSA_HPG):
        c = branch * NSA_HPG + h
        o_ref[0, h] = (o[h * Q_BLOCK:(h + 1) * Q_BLOCK, :] * gates[:, c:c + 1])[:, :HEAD_DIM]


def _cmp_kernel(q_ref, zg_ref, kc_ref, vc_ref, pc_ref, ov_ref, o_ref, m_ref):
    c = pl.program_id(2)
    nc = kc_ref.shape[2]
    nsel = ov_ref.shape[0]
    qs = _stack_heads(q_ref)
    s = _dot_nt(qs, kc_ref[0, 0].astype(BF16))
    off = (nc - 8) - 8 * c
    bias = pltpu.roll(pc_ref[0], (2 * nc - off) % (2 * nc), 1)[:, :nc]
    s = s + bias
    m = jnp.max(s, axis=-1, keepdims=True)
    p = jnp.exp(s - m)
    l = jnp.sum(p, axis=-1, keepdims=True)
    ql = lax.broadcasted_iota(jnp.int32, (NSA_HPG * Q_BLOCK, 1), 0) & (Q_BLOCK - 1)
    any_key = (c * Q_BLOCK + ql) >= (CMP_LEN - 1)
    p = p * jnp.where(any_key, 1.0 / l, 0.0)
    gates = _sigmoid(zg_ref[...])
    _store_heads(o_ref, _dot(p.astype(BF16), vc_ref[0, 0].astype(BF16)), gates, 0)

    psum = p[0:Q_BLOCK] + p[Q_BLOCK:2 * Q_BLOCK] + p[2 * Q_BLOCK:3 * Q_BLOCK] + p[3 * Q_BLOCK:4 * Q_BLOCK]
    hi = psum.astype(BF16)
    lo = (psum - hi.astype(F32)).astype(BF16)
    imp = _dot_nt(ov_ref[...], hi) + _dot_nt(ov_ref[...], lo)
    jb = lax.broadcasted_iota(jnp.int32, (nsel, Q_BLOCK), 0)
    tq = c * Q_BLOCK + lax.broadcasted_iota(jnp.int32, (nsel, Q_BLOCK), 1)
    jq = jnp.right_shift(tq, int(math.log2(SEL_BLOCK)))
    forced = (jb == 0) | (jb == jq) | (jb == jq - 1)
    val = jnp.where(forced, SEL_FORCE, jnp.where(jb <= jq, imp, -SEL_FORCE))
    sel = jnp.zeros((nsel, Q_BLOCK), F32)
    jbf = jb.astype(F32)
    for _ in range(SEL_TOPN):
        mx = jnp.max(val, axis=0, keepdims=True)
        first = jnp.min(jnp.where(val == mx, jbf, float(nsel)), axis=0, keepdims=True)
        hit = jbf == first
        sel = jnp.where(hit, jnp.where(mx >= 0.0, 1.0, 0.0), sel)
        val = jnp.where(hit, -3e38, val)
    m_ref[0, 0] = (sel.T - 1.0) * (-NEG_INF)


def _cmp_branch(z, k_cmp, v_cmp, pc_tab, overlap, bsz, t):
    nq = t // Q_BLOCK
    nc = k_cmp.shape[2]
    nsel = overlap.shape[0]
    return pl.pallas_call(
        _cmp_kernel,
        grid=(bsz, NSA_GROUPS, nq),
        in_specs=[pl.BlockSpec((Q_BLOCK, NSA_HPG * LANE), lambda b, g, c: (b * nq + c, g)),
                  pl.BlockSpec((Q_BLOCK, LANE), lambda b, g, c: (b * nq + c, C_G3 // LANE + g)),
                  pl.BlockSpec((1, 1, nc, LANE), lambda b, g, c: (b, g, 0, 0)),
                  pl.BlockSpec((1, 1, nc, LANE), lambda b, g, c: (b, g, 0, 0)),
                  pl.BlockSpec((1, NSA_HPG * Q_BLOCK, 2 * nc), lambda b, g, c: (g, 0, 0)),
                  pl.BlockSpec((nsel, nc), lambda b, g, c: (0, 0))],
        out_specs=[pl.BlockSpec((1, NSA_HPG, Q_BLOCK, HEAD_DIM), lambda b, g, c: (b, g, c, 0)),
                   pl.BlockSpec((1, 1, Q_BLOCK, nsel), lambda b, g, c: (b, g, c, 0))],
        out_shape=[jax.ShapeDtypeStruct((bsz, NSA_HEADS, t, HEAD_DIM), F32),
                   jax.ShapeDtypeStruct((bsz, NSA_GROUPS, t, nsel), F32)],
        compiler_params=_cparams(("parallel", "parallel", "arbitrary")),
        name="nsa_cmp_select",
    )(z, z, k_cmp, v_cmp, pc_tab, overlap)


def _win_kernel(q_ref, zg_ref, k_ref, v_ref, wb_ref, o_ref, s_s, p_s):
    c = pl.program_id(2)
    start = pl.multiple_of(jnp.maximum(c * Q_BLOCK - WINDOW, 0), Q_BLOCK)
    qs = _stack_heads(q_ref)
    s_s[...] = _dot_nt(qs, k_ref[0, 0, pl.ds(start, WINDOW + Q_BLOCK), :])
    for r in range(NSA_HPG * Q_BLOCK // SEL_ROWS):
        rows = slice(r * SEL_ROWS, (r + 1) * SEL_ROWS)
        s = s_s[rows, :] + wb_ref[0, 0, rows, :]
        p_s[rows, :] = jnp.exp(s - jnp.max(s, axis=-1, keepdims=True)).astype(BF16)
    o = _dot(p_s[...], v_ref[0, 0, pl.ds(start, WINDOW + Q_BLOCK), :])
    o = o * (1.0 / o[:, HEAD_DIM:HEAD_DIM + 1])
    _store_heads(o_ref, o, _sigmoid(zg_ref[...]), 2)


def _win_branch(z, kw, vw, wb_tab, bsz, t):
    nq = t // Q_BLOCK
    nvar = wb_tab.shape[1]
    return pl.pallas_call(
        _win_kernel,
        grid=(bsz, NSA_GROUPS, nq),
        in_specs=[pl.BlockSpec((Q_BLOCK, NSA_HPG * LANE), lambda b, g, c: (b * nq + c, g)),
                  pl.BlockSpec((Q_BLOCK, LANE), lambda b, g, c: (b * nq + c, C_G3 // LANE + g)),
                  pl.BlockSpec((1, 1, t, LANE), lambda b, g, c: (b, g, 0, 0)),
                  pl.BlockSpec((1, 1, t, LANE), lambda b, g, c: (b, g, 0, 0)),
                  pl.BlockSpec((1, 1, NSA_HPG * Q_BLOCK, WINDOW + Q_BLOCK),
                               lambda b, g, c: (g, jnp.minimum(c, nvar - 1), 0, 0))],
        out_specs=pl.BlockSpec((1, NSA_HPG, Q_BLOCK, HEAD_DIM), lambda b, g, c: (b, g, c, 0)),
        out_shape=jax.ShapeDtypeStruct((bsz, NSA_HEADS, t, HEAD_DIM), F32),
        scratch_shapes=[pltpu.VMEM((NSA_HPG * Q_BLOCK, WINDOW + Q_BLOCK), F32),
                        pltpu.VMEM((NSA_HPG * Q_BLOCK, WINDOW + Q_BLOCK), BF16)],
        compiler_params=_cparams(("parallel", "parallel", "arbitrary")),
        name="nsa_window",
    )(z, z, kw, vw, wb_tab)


SEL_TK = 512


SEL_ROWS = 64
SEL_NEAR = FAR_BIAS_DIST + Q_BLOCK


def _sel_kernel(q_ref, zg_ref, k_ref, v_ref, mt_ref, lt_ref, cq_ref, o_ref, m_s, acc_s, s0_s, s1_s, p0_s, p1_s,
                a0_s, a1_s):
    c = pl.program_id(2)
    q = q_ref[...] * (HEAD_DIM ** -0.5)
    qs = jnp.concatenate([q[:, h * LANE:(h + 1) * LANE] for h in range(NSA_HPG)], axis=0) + cq_ref[0]
    mt = mt_ref[0, 0]
    q_aug = jnp.concatenate([qs, jnp.concatenate([mt] * NSA_HPG, axis=0)], axis=1).astype(BF16)
    m_s[...] = jnp.full(m_s.shape, NEG_INF, F32)
    acc_s[...] = jnp.zeros(acc_s.shape, F32)
    ncol = SEL_TK // LANE
    n_tiles = lax.div(c, SEL_TK // Q_BLOCK) + 1

    def key_rows(i):
        return pl.ds(pl.multiple_of(jnp.minimum(i, n_tiles - 1) * SEL_TK, SEL_TK), SEL_TK)

    def scores(i, dst):
        dst[...] = _dot_nt(q_aug, k_ref[0, 0, key_rows(i), :])

    def tile(i, cur, nxt, p_buf, a_buf):
        scores(i + 1, nxt)
        off = jnp.where(i < n_tiles, jnp.maximum(SEL_NEAR + SEL_TK - (c * Q_BLOCK - i * SEL_TK), 0),
                        SEL_NEAR + SEL_TK + Q_BLOCK)
        off = pl.multiple_of(off, LANE)
        for r in range(NSA_HPG * Q_BLOCK // SEL_ROWS):
            rows = slice(r * SEL_ROWS, (r + 1) * SEL_ROWS)
            s = cur[rows, :] + lt_ref[0, rows, pl.ds(off, SEL_TK)]
            cols = [s[:, j * LANE:(j + 1) * LANE] for j in range(ncol)]
            mx = functools.reduce(jnp.maximum, cols)
            m_old = m_s[rows, :]
            m_new = jnp.maximum(m_old, jnp.max(mx, axis=-1, keepdims=True))
            m_s[rows, :] = m_new
            a_buf[rows, :] = jnp.exp(m_old - m_new)
            p_buf[rows, :] = jnp.concatenate([jnp.exp(col - m_new) for col in cols], axis=1).astype(BF16)
        acc_s[...] = a_buf[...] * acc_s[...] + _dot(p_buf[...], v_ref[0, 0, key_rows(i), :])

    scores(0, s0_s)

    def pair(j, carry):
        tile(2 * j, s0_s, s1_s, p0_s, a0_s)
        tile(2 * j + 1, s1_s, s0_s, p1_s, a1_s)
        return carry

    lax.fori_loop(0, lax.div(n_tiles + 1, 2), pair, 0)
    acc = acc_s[...]
    o = acc * (1.0 / acc[:, HEAD_DIM:HEAD_DIM + 1])
    _store_heads(o_ref, o, _sigmoid(zg_ref[...]), 1)


def _sel_branch(z, ks_aug, vs, mt, lt_tab, cq_tab, bsz, t):
    nq = t // Q_BLOCK
    nsel = mt.shape[-1]
    rows = NSA_HPG * Q_BLOCK
    return pl.pallas_call(
        _sel_kernel,
        grid=(bsz, NSA_GROUPS, nq),
        in_specs=[pl.BlockSpec((Q_BLOCK, NSA_HPG * LANE), lambda b, g, c: (b * nq + c, g)),
                  pl.BlockSpec((Q_BLOCK, LANE), lambda b, g, c: (b * nq + c, C_G3 // LANE + g)),
                  pl.BlockSpec((1, 1, t, 2 * LANE), lambda b, g, c: (b, g, 0, 0)),
                  pl.BlockSpec((1, 1, t, LANE), lambda b, g, c: (b, g, 0, 0)),
                  pl.BlockSpec((1, 1, Q_BLOCK, nsel), lambda b, g, c: (b, g, c, 0)),
                  pl.BlockSpec((1, rows, lt_tab.shape[2]), lambda b, g, c: (g, 0, 0)),
                  pl.BlockSpec((1, rows, LANE), lambda b, g, c: (g, 0, 0))],
        out_specs=pl.BlockSpec((1, NSA_HPG, Q_BLOCK, HEAD_DIM), lambda b, g, c: (b, g, c, 0)),
        out_shape=jax.ShapeDtypeStruct((bsz, NSA_HEADS, t, HEAD_DIM), F32),
        scratch_shapes=[pltpu.VMEM((rows, LANE), F32),
                        pltpu.VMEM((rows, LANE), F32),
                        pltpu.VMEM((rows, SEL_TK), F32),
                        pltpu.VMEM((rows, SEL_TK), F32),
                        pltpu.VMEM((rows, SEL_TK), BF16),
                        pltpu.VMEM((rows, SEL_TK), BF16),
                        pltpu.VMEM((rows, LANE), F32),
                        pltpu.VMEM((rows, LANE), F32)],
        compiler_params=_cparams(("parallel", "parallel", "arbitrary")),
        name="nsa_selected",
    )(z, z, ks_aug, vs, mt, lt_tab, cq_tab)


def _shift_mix(z_ref, prev_ref, mu, first):
    z = z_ref[...]
    prev = jnp.where(first, 0.0, prev_ref[7:8, :])
    row = lax.broadcasted_iota(jnp.int32, z.shape, 0)
    zp = jnp.where(row == 0, prev, pltpu.roll(z, 1, 0))
    return z + mu * (zp - z)


def _rwkv_prep_kernel(has_vres, tiles_per_seq, *refs):
    (zr, zk, zv, zwa, zg, pr, pk, pv, pwa, pg, mu_ref, vec_ref, w2_ref, a2_ref, g2_ref, hsum_ref) = refs[:16]
    pos = 16
    if has_vres:
        vf_ref, v0_ref, v1_ref, v2_ref = refs[pos:pos + 4]
        pos += 4
    r_o, w_o, k_o, v_o, kk_o, b_o, g_o, bonus_o, vt_o = refs[pos:]
    first = lax.rem(pl.program_id(0), tiles_per_seq) == 0
    r = _shift_mix(zr, pr, mu_ref[0:1, :], first)
    k = _shift_mix(zk, pk, mu_ref[1:2, :], first)
    v = _shift_mix(zv, pv, mu_ref[2:3, :], first)
    wa = _shift_mix(zwa, pwa, mu_ref[3:4, 0:LANE], first)
    zg_s = _shift_mix(zg, pg, mu_ref[4:5, 0:LANE], first)
    w0, a0, k_k, k_a, r_k = (vec_ref[i:i + 1, :] for i in range(5))
    if has_vres:
        lora = _dot(_dot(v.astype(BF16), v1_ref[...]).astype(BF16), v2_ref[...])
        v = v + (vf_ref[...] - v) * _sigmoid(v0_ref[...] + lora)
    u = w0 + _dot(jnp.tanh(wa).astype(BF16), w2_ref[...])
    decay = jnp.exp(-math.exp(-0.5) * _sigmoid(u))
    a = _sigmoid(a0 + _dot(wa.astype(BF16), a2_ref[...]))
    g = _dot(_sigmoid(zg_s).astype(BF16), g2_ref[...])
    kk = k * k_k
    kk = kk / jnp.maximum(jnp.sqrt(_dot_split(kk * kk, hsum_ref[...])), 1e-12)
    k = k * (1.0 + (a - 1.0) * k_a)
    r_o[...] = r
    w_o[...] = decay
    k_o[...] = k
    v_o[...] = v
    kk_o[...] = kk
    b_o[...] = kk * a
    g_o[...] = g
    bonus_o[...] = _dot_split(r * k * r_k, hsum_ref[...]) * v
    vt_o[0] = v.T.astype(BF16)


def _rwkv_prep(z, mu5, vecs, w2p, a2p, g2, hsum, vres, bsz, t, tm=256):
    n = bsz * t
    nt = n // tm
    tps = t // tm
    w = RWKV_WIDTH

    def cur(width, col):
        return pl.BlockSpec((tm, width), lambda i: (i, col // width))

    def prev(width, col):
        return pl.BlockSpec((8, width), lambda i: (jnp.maximum(i * (tm // 8) - 1, 0), col // width))

    def full(shape):
        return pl.BlockSpec(shape, lambda i: (0,) * len(shape))

    cols = [(w, C_R), (w, C_K), (w, C_V), (LANE, C_WA), (LANE, C_ZG)]
    in_specs = [cur(*c) for c in cols] + [prev(*c) for c in cols]
    in_specs += [full(mu5.shape), full(vecs.shape), full(w2p.shape), full(a2p.shape), full(g2.shape), full(hsum.shape)]
    args = [z] * 10 + [mu5, vecs, w2p, a2p, g2, hsum]
    if vres is not None:
        v_first, v0, v1p, v2p = vres
        in_specs += [pl.BlockSpec((tm, w), lambda i: (i, 0)), full(v0.shape), full(v1p.shape), full(v2p.shape)]
        args += [v_first, v0, v1p, v2p]
    tok = pl.BlockSpec((tm, w), lambda i: (i, 0))
    out_specs = [tok] * 8 + [pl.BlockSpec((1, w, tm), lambda i: (i // tps, 0, i % tps))]
    out_shape = [jax.ShapeDtypeStruct((n, w), F32)] * 8 + [jax.ShapeDtypeStruct((bsz, w, t), BF16)]
    return pl.pallas_call(
        functools.partial(_rwkv_prep_kernel, vres is not None, tps),
        grid=(nt,),
        in_specs=in_specs,
        out_specs=out_specs,
        out_shape=out_shape,
        compiler_params=_cparams(("parallel",)),
        name="rwkv_prep",
    )(*args)


QUAD = 4
QUAD_LANES = QUAD * RWKV_HEAD
SUB_STEPS = LANE // QUAD


def _scan_kernel(r_ref, w_ref, k_ref, kk_ref, b_ref, vt_ref, ones_ref, place_ref, o_ref, s_ref, vq_ref, kq_ref, vk_ref):
    bsz = r_ref.shape[0]
    tt = r_ref.shape[1]
    nh = RWKV_HEAD
    quads = [(bi, hf) for bi in range(bsz) for hf in range(RWKV_HEADS // QUAD)]

    @pl.when(pl.program_id(0) == 0)
    def _():
        s_ref[...] = jnp.zeros(s_ref.shape, F32)

    sub_shift, head_shift = int(math.log2(SUB_STEPS)), int(math.log2(nh))
    sub_lane = lax.broadcasted_iota(jnp.int32, (nh, LANE), 1) & (SUB_STEPS - 1)
    k_row_head = jnp.right_shift(lax.broadcasted_iota(jnp.int32, (LANE, QUAD_LANES), 0), sub_shift)
    k_lane_head = jnp.right_shift(lax.broadcasted_iota(jnp.int32, (LANE, QUAD_LANES), 1), head_shift)
    r_mask = (lax.broadcasted_iota(jnp.int32, (16, QUAD_LANES), 0)
              == jnp.right_shift(lax.broadcasted_iota(jnp.int32, (16, QUAD_LANES), 1), head_shift))
    ones_bd = ones_ref[...]

    def sub_block(sb, carry):
        s0 = pl.multiple_of(sb * SUB_STEPS, SUB_STEPS)
        for q, (bi, hf) in enumerate(quads):
            ln = slice(hf * QUAD_LANES, (hf + 1) * QUAD_LANES)
            vq = _dot(vt_ref[bi, (QUAD * hf) * nh:(QUAD * hf + 1) * nh, :], place_ref[sb * QUAD])
            for h in range(1, QUAD):
                vq = vq + _dot(vt_ref[bi, (QUAD * hf + h) * nh:(QUAD * hf + h + 1) * nh, :], place_ref[sb * QUAD + h])
            vq_ref[q] = vq.astype(BF16)
            k_sub = k_ref[bi, pl.ds(s0, SUB_STEPS), ln]
            kq_ref[q] = jnp.where(k_row_head == k_lane_head, jnp.concatenate([k_sub] * QUAD, axis=0), 0.0).astype(BF16)

        def group(tg, carry):
            t0 = pl.multiple_of(s0 + tg * 8, 8)
            rows = []
            for bi, hf in quads:
                ln = slice(hf * QUAD_LANES, (hf + 1) * QUAD_LANES)
                rows.append(tuple(ref[bi, pl.ds(t0, 8), ln] for ref in (w_ref, kk_ref, b_ref, r_ref)))
            onehots = [jnp.where(sub_lane == tg * 8 + u, 1.0, 0.0).astype(BF16) for u in range(8)]
            for q in range(len(quads)):
                vq = vq_ref[q]
                vk_ref[q] = _dot(jnp.concatenate([vq * oh for oh in onehots], axis=0), kq_ref[q])
            state = [s_ref[q] for q in range(len(quads))]
            for u in range(8):
                sk = _dot(jnp.concatenate([(state[q] * rows[q][1][u:u + 1, :]).astype(BF16)
                                           for q in range(len(quads))], axis=0), ones_bd)
                for q, (bi, hf) in enumerate(quads):
                    w8, _, b8, r8 = rows[q]
                    state[q] = (state[q] * w8[u:u + 1, :] - sk[q * nh:(q + 1) * nh, :] * b8[u:u + 1, :]
                                + vk_ref[q, u * nh:(u + 1) * nh, :])
                    r_lhs = jnp.where(r_mask, r8[u:u + 1, :], 0.0).astype(BF16)
                    o = _dot_nt(r_lhs, state[q].astype(BF16))
                    o_ref[t0 + u, bi, QUAD * hf:QUAD * (hf + 1), :] = o[0:QUAD, :]
            for q in range(len(quads)):
                s_ref[q] = state[q]
            return carry

        return lax.fori_loop(0, SUB_STEPS // 8, group, carry)

    lax.fori_loop(0, tt // SUB_STEPS, sub_block, 0)


def _rwkv_scan(r, w, k, kk, b, vt, ones_bd, place, bsz, t):
    tt = SCAN_TT
    wd = RWKV_WIDTH
    nquad = bsz * RWKV_HEADS // QUAD
    tok = pl.BlockSpec((bsz, tt, wd), lambda i: (0, i, 0))
    return pl.pallas_call(
        _scan_kernel,
        grid=(t // tt,),
        in_specs=[tok, tok, tok, tok, tok,
                  pl.BlockSpec((bsz, wd, tt), lambda i: (0, 0, i)),
                  pl.BlockSpec(ones_bd.shape, lambda i: (0, 0)),
                  pl.BlockSpec(place.shape, lambda i: (0, 0, 0))],
        out_specs=pl.BlockSpec((tt, bsz, RWKV_HEADS, RWKV_HEAD), lambda i: (i, 0, 0, 0)),
        out_shape=jax.ShapeDtypeStruct((t, bsz, RWKV_HEADS, RWKV_HEAD), F32),
        scratch_shapes=[pltpu.VMEM((nquad, RWKV_HEAD, QUAD_LANES), F32),
                        pltpu.VMEM((nquad, RWKV_HEAD, LANE), BF16),
                        pltpu.VMEM((nquad, LANE, QUAD_LANES), BF16),
                        pltpu.VMEM((nquad, 8 * RWKV_HEAD, QUAD_LANES), F32)],
        compiler_params=_cparams(("arbitrary",)),
        name="rwkv_scan",
    )(r, w, k, kk, b, vt, ones_bd, place)


def _merge_kernel(x_ref, oc_ref, os_ref, ow_ref, orw_ref, bonus_ref, g_ref, zgn_ref, zgr_ref, gn_ref, hsum_ref,
                  pn_ref, pr_ref, wo_ref, ln_ref, o_ref):
    y_n = jnp.zeros(o_ref.shape, F32)
    for h in range(NSA_HEADS):
        o_h = (oc_ref[0, h] + os_ref[0, h] + ow_ref[0, h]).astype(BF16)
        y_n = y_n + _dot(o_h, pn_ref[h * HEAD_DIM:(h + 1) * HEAD_DIM, :])
    o = orw_ref[...]
    mu = _dot_split(o, hsum_ref[...]) * (1.0 / RWKV_HEAD)
    d = o - mu
    var = _dot_split(d * d, hsum_ref[...]) * (1.0 / RWKV_HEAD)
    o = d * lax.rsqrt(var + GN_EPS) * gn_ref[0:1, :] + gn_ref[1:2, :]
    o = ((o + bonus_ref[...]) * g_ref[...]).astype(BF16)
    y1 = _sigmoid(zgn_ref[...]) * y_n + _sigmoid(zgr_ref[...]) * _dot(o, pr_ref[...])
    y = _dot(y1.astype(BF16), wo_ref[...])
    o_ref[...] = _layer_norm(DN_ALPHA * x_ref[...] + y, ln_ref[0:1, :], ln_ref[1:2, :])


def _merge(x, o_c, o_s, o_w, o_rw, bonus, g, z, gn, hsum, p_nsa, p_rwkv, w_out, ln, bsz, t, tm=256):
    n = bsz * t
    tps = t // tm
    d = D_MODEL
    w = RWKV_WIDTH
    head = pl.BlockSpec((1, NSA_HEADS, tm, HEAD_DIM), lambda i: (i // tps, 0, i % tps, 0))
    tok = pl.BlockSpec((tm, w), lambda i: (i, 0))

    def full(a):
        return pl.BlockSpec(a.shape, lambda i: (0,) * a.ndim)

    return pl.pallas_call(
        _merge_kernel,
        grid=(n // tm,),
        in_specs=[pl.BlockSpec((tm, d), lambda i: (i, 0)), head, head, head, tok, tok, tok,
                  pl.BlockSpec((tm, d), lambda i: (i, C_GN // d)),
                  pl.BlockSpec((tm, d), lambda i: (i, C_GR // d)),
                  full(gn), full(hsum), full(p_nsa), full(p_rwkv), full(w_out), full(ln)],
        out_specs=pl.BlockSpec((tm, d), lambda i: (i, 0)),
        out_shape=jax.ShapeDtypeStruct((n, d), F32),
        compiler_params=_cparams(("parallel",)),
        name="mix_merge_ln",
    )(x, o_c, o_s, o_w, o_rw, bonus, g, z, z, gn, hsum, p_nsa, p_rwkv, w_out, ln)


def _xattn_kernel(x_ref, kt_ref, v_ref, wq_ref, wo_ref, ln_ref, rw_ref, o_ref, s_ref):
    x = x_ref[...]
    q = _dot(x.astype(BF16), wq_ref[...]).astype(BF16)
    outs = []
    for h in range(XATTN_HEADS):
        hs = slice(h * XATTN_HEAD, (h + 1) * XATTN_HEAD)
        s = _dot(q[:, hs], kt_ref[0, hs, :]) * (XATTN_HEAD ** -0.5)
        m = jnp.max(s, axis=-1, keepdims=True)
        p = jnp.exp(s - m)
        l = jnp.sum(p, axis=-1, keepdims=True)
        outs.append(_dot(p.astype(BF16), v_ref[0, :, hs]) * (1.0 / l))
    o = jnp.concatenate(outs, axis=1).astype(BF16)
    x2 = _layer_norm(DN_ALPHA * x + _dot(o, wo_ref[...]), ln_ref[0:1, :], ln_ref[1:2, :])
    o_ref[...] = x2
    hi = x2.astype(BF16)
    lo = (x2 - hi.astype(F32)).astype(BF16)
    logits = _dot(hi, rw_ref[0]) + (_dot(lo, rw_ref[0]) + _dot(hi, rw_ref[1]))
    s_ref[...] = _sigmoid(logits)


def _xattn(x, kt, v, wq, wo, ln, rw2, bsz, t, tm=256):
    n = bsz * t
    tps = t // tm
    d = D_MODEL

    def full(a):
        return pl.BlockSpec(a.shape, lambda i: (0,) * a.ndim)

    return pl.pallas_call(
        _xattn_kernel,
        grid=(n // tm,),
        in_specs=[pl.BlockSpec((tm, d), lambda i: (i, 0)),
                  pl.BlockSpec((1,) + kt.shape[1:], lambda i: (i // tps, 0, 0)),
                  pl.BlockSpec((1,) + v.shape[1:], lambda i: (i // tps, 0, 0)),
                  full(wq), full(wo), full(ln), full(rw2)],
        out_specs=[pl.BlockSpec((tm, d), lambda i: (i, 0)), pl.BlockSpec((tm, LANE), lambda i: (i, 0))],
        out_shape=[jax.ShapeDtypeStruct((n, d), F32), jax.ShapeDtypeStruct((n, LANE), F32)],
        compiler_params=_cparams(("parallel",)),
        name="xattn_ln_router",
    )(x, kt, v, wq, wo, ln, rw2)


def _expert_kernel(be_ref, x_ref, wg_ref, wu_ref, wd_ref, o_ref, wg_s, wu_s, wd_s):
    i = pl.program_id(0)

    @pl.when((i == 0) | (be_ref[i] != be_ref[jnp.maximum(i - 1, 0)]))
    def _():
        wg_s[...] = wg_ref[0].astype(BF16)
        wu_s[...] = wu_ref[0].astype(BF16)
        wd_s[...] = wd_ref[0].astype(BF16)

    x = x_ref[...]
    hg = _dot(x, wg_s[...])
    h = hg * _sigmoid(hg) * _dot(x, wu_s[...])
    o_ref[...] = _dot(h.astype(BF16), wd_s[...]).astype(o_ref.dtype)


def _experts(block_expert, rows, w_gate, w_up, w_down):
    n_rows, d = rows.shape
    ff = w_gate.shape[2]
    return pl.pallas_call(
        _expert_kernel,
        grid_spec=pltpu.PrefetchScalarGridSpec(
            num_scalar_prefetch=1,
            grid=(n_rows // MOE_BLK,),
            in_specs=[pl.BlockSpec((MOE_BLK, d), lambda i, be: (i, 0)),
                      pl.BlockSpec((1, d, ff), lambda i, be: (be[i], 0, 0)),
                      pl.BlockSpec((1, d, ff), lambda i, be: (be[i], 0, 0)),
                      pl.BlockSpec((1, ff, d), lambda i, be: (be[i], 0, 0))],
            out_specs=pl.BlockSpec((MOE_BLK, d), lambda i, be: (i, 0)),
            scratch_shapes=[pltpu.VMEM((d, ff), BF16), pltpu.VMEM((d, ff), BF16), pltpu.VMEM((ff, d), BF16)]),
        out_shape=jax.ShapeDtypeStruct((n_rows, d), BF16),
        compiler_params=_cparams(("arbitrary",)),
        name="moe_experts",
    )(block_expert, rows, w_gate, w_up, w_down)


def _combine_kernel(x_ref, ya_ref, yb_ref, gate_ref, ln_ref, o_ref):
    y = gate_ref[:, 0:1] * ya_ref[...] + gate_ref[:, 1:2] * yb_ref[...]
    o_ref[...] = _layer_norm(DN_ALPHA * x_ref[...] + y, ln_ref[0:1, :], ln_ref[1:2, :])


def _combine(x, ya, yb, gate, ln, tm=512):
    n, d = x.shape
    tok = pl.BlockSpec((tm, d), lambda i: (i, 0))
    return pl.pallas_call(
        _combine_kernel,
        grid=(n // tm,),
        in_specs=[tok, tok, tok, pl.BlockSpec((tm, TOP_K), lambda i: (i, 0)),
                  pl.BlockSpec(ln.shape, lambda i: (0, 0))],
        out_specs=tok,
        out_shape=jax.ShapeDtypeStruct((n, d), F32),
        compiler_params=_cparams(("parallel",)),
        name="moe_combine_ln",
    )(x, ya, yb, gate, ln)


def _t5_bucket(dist):
    n = jnp.maximum(dist, 0)
    max_exact = REL_BUCKETS // 2
    log_ratio = jnp.log(jnp.maximum(n, 1).astype(F32) / max_exact) / math.log(REL_MAX_DIST / max_exact)
    large = jnp.minimum(max_exact + (log_ratio * (REL_BUCKETS - max_exact)).astype(jnp.int32), REL_BUCKETS - 1)
    return jnp.where(n < max_exact, n, large)


def _bias_tables(rel_table, t):
    nc = t // CMP_STRIDE
    f = rel_table[_t5_bucket(jnp.arange(t + 2048))].astype(F32).T
    ql = jnp.arange(Q_BLOCK)[:, None]

    def lookup(dist, ok):
        vals = jnp.take(f, jnp.clip(dist, 0, f.shape[-1] - 1), axis=1)
        vals = jnp.where(ok[None], vals, NEG_INF)
        return vals.reshape(NSA_GROUPS, NSA_HPG * dist.shape[0], dist.shape[1])

    j = jnp.arange(2 * nc)[None, :]
    dist_c = CMP_STRIDE * (nc - 8 - j) + ql - (CMP_LEN - 1)
    pc = lookup(dist_c, dist_c >= 0)
    nvar = WINDOW // Q_BLOCK + 1

    def toeplitz(base, width, hi, shift):
        period = width + Q_BLOCK
        m = np.arange(period)
        dist = base - np.where(m < width, m, m - period)
        row = jnp.take(f, np.clip(dist, 0, f.shape[-1] - 1), axis=1) - shift
        row = jnp.where(((dist >= 0) & (dist < hi))[None], row, NEG_INF)
        flat = jnp.tile(row, (1, Q_BLOCK))[:, :Q_BLOCK * (period - 1)]
        return flat.reshape(NSA_HEADS, Q_BLOCK, period - 1)[:, :, :width]

    wb = jnp.stack([toeplitz(Q_BLOCK * v, WINDOW + Q_BLOCK, WINDOW, 0.0) for v in range(nvar)], axis=1)
    wb = wb.reshape(NSA_GROUPS, NSA_HPG, nvar, Q_BLOCK, -1).transpose(0, 2, 1, 3, 4)
    wb = wb.reshape(NSA_GROUPS, nvar, NSA_HPG * Q_BLOCK, -1)
    far = rel_table[REL_BUCKETS - 1].astype(F32)
    lt = toeplitz(SEL_NEAR + SEL_TK, SEL_NEAR + 2 * SEL_TK + Q_BLOCK, f.shape[-1], far[:, None])
    lt = lt.reshape(NSA_GROUPS, NSA_HPG * Q_BLOCK, -1)
    far_hi = far.astype(BF16).astype(F32)
    lane = jnp.arange(LANE)[None, :]
    cq = jnp.where(lane == HEAD_DIM, far_hi[:, None], jnp.where(lane == HEAD_DIM + 1, (far - far_hi)[:, None], 0.0))
    cq = jnp.broadcast_to(cq[:, None, :], (NSA_HEADS, Q_BLOCK, LANE)).reshape(NSA_GROUPS, NSA_HPG * Q_BLOCK, LANE)
    return pc, wb, lt, cq


def _in_proj_perm():
    perm = np.full((IN_PAD,), -1, np.int64)
    for h in range(NSA_HEADS):
        perm[C_Q + h * LANE:C_Q + h * LANE + HEAD_DIM] = np.arange(HEAD_DIM) + h * HEAD_DIM
    rw0 = NSA_COLS
    g0 = NSA_COLS + RWKV_COLS
    perm[C_GN:C_GN + D_MODEL] = g0 + np.arange(D_MODEL)
    perm[C_GR:C_GR + D_MODEL] = g0 + D_MODEL + np.arange(D_MODEL)
    perm[C_R:C_R + RWKV_WIDTH] = rw0 + np.arange(RWKV_WIDTH)
    perm[C_K:C_K + RWKV_WIDTH] = rw0 + RWKV_WIDTH + np.arange(RWKV_WIDTH)
    perm[C_V:C_V + RWKV_WIDTH] = rw0 + 2 * RWKV_WIDTH + np.arange(RWKV_WIDTH)
    for g in range(NSA_GROUPS):
        for br in range(3):
            for h in range(NSA_HPG):
                perm[C_G3 + g * LANE + br * NSA_HPG + h] = NSA_WIDTH + 6 * NSA_KV + (g * NSA_HPG + h) * 3 + br
    perm[C_WA:C_WA + LORA_W + LORA_A] = rw0 + 3 * RWKV_WIDTH + np.arange(LORA_W + LORA_A)
    perm[C_ZG:C_ZG + LORA_G] = rw0 + 3 * RWKV_WIDTH + LORA_W + LORA_A + np.arange(LORA_G)
    return perm


_PERM = _in_proj_perm()


def _permute_in_proj(w):
    pieces = []
    start = 0
    while start < IN_PAD:
        stop = start + 1
        if _PERM[start] < 0:
            while stop < IN_PAD and _PERM[stop] < 0:
                stop += 1
            pieces.append(jnp.zeros((w.shape[0], stop - start), BF16))
        else:
            while stop < IN_PAD and _PERM[stop] == _PERM[stop - 1] + 1:
                stop += 1
            pieces.append(w[:, int(_PERM[start]):int(_PERM[stop - 1]) + 1].astype(BF16))
        start = stop
    return jnp.concatenate(pieces, axis=1)


def _pad_to(a, shape):
    return jnp.pad(a, [(0, s - d) for d, s in zip(a.shape, shape)])


def _route(s, router_bias, n_tok):
    s16 = s[:, :N_EXPERTS]
    s_sel = s16 + router_bias.astype(F32)

    def top2(a):
        i1 = jnp.argmax(a, axis=-1)
        rest = jnp.where(jnp.arange(a.shape[-1]) == i1[..., None], -jnp.inf, a)
        i2 = jnp.argmax(rest, axis=-1)
        return jnp.max(a, axis=-1) + jnp.max(rest, axis=-1), jnp.stack([i1, i2], axis=-1)

    group_score, _ = top2(s_sel.reshape(n_tok, N_EXPERT_GROUPS, EXPERTS_PER_GROUP))
    group = jnp.argmax(group_score, axis=-1)
    in_group = (jnp.arange(N_EXPERTS) // EXPERTS_PER_GROUP)[None, :] == group[:, None]
    _, e_idx = top2(jnp.where(in_group, s_sel, NEG_INF))
    s_top = jnp.take_along_axis(s16, e_idx, axis=-1)
    gate = s_top / jnp.sum(s_top, axis=-1, keepdims=True)
    e_flat = e_idx.reshape(-1)
    onehot = (e_flat[:, None] == jnp.arange(N_EXPERTS)[None, :]).astype(jnp.int32)
    rank = jnp.take_along_axis(jnp.cumsum(onehot, axis=0) - onehot, e_flat[:, None], axis=1)[:, 0]
    counts = jnp.sum(onehot, axis=0)
    padded = (counts + MOE_BLK - 1) // MOE_BLK * MOE_BLK
    ends = jnp.cumsum(padded)
    dest = (ends - padded)[e_flat] + rank
    n_rows = n_tok * TOP_K + N_EXPERTS * MOE_BLK
    tok_of_row = jnp.zeros((n_rows,), jnp.int32).at[dest].set(jnp.arange(n_tok * TOP_K, dtype=jnp.int32) // TOP_K)
    block_expert = jnp.minimum(
        jnp.searchsorted(ends, jnp.arange(n_rows // MOE_BLK) * MOE_BLK, side='right'), N_EXPERTS - 1).astype(jnp.int32)
    return gate, dest.reshape(n_tok, TOP_K), tok_of_row, block_expert


def kernel(x, mem, rel_table, router_w, router_bias, w_in, cmp_pe_k, cmp_pe_v, cmp_w1k, cmp_w2k, cmp_w1v, cmp_w2v, rwkv_mu, rwkv_w0, rwkv_w2, rwkv_a0, rwkv_a2, rwkv_g2, rwkv_kk, rwkv_ka, rwkv_rk, rwkv_gn_g, rwkv_gn_b, rwkv_v0, rwkv_v1, rwkv_v2, p_nsa, p_rwkv, w_out, ln1_g, ln1_b, xq_w, xk_w, xv_w, xo_w, ln2_g, ln2_b, moe_w_gate, moe_w_up, moe_w_down, ln3_g, ln3_b):
    bsz, t, d = x.shape
    n = bsz * t
    depth = w_in.shape[0]
    nc = t // CMP_STRIDE
    nsel = t // SEL_BLOCK
    g, hd = NSA_GROUPS, HEAD_DIM

    pc_tab, wb_tab, lt_tab, cq_tab = _bias_tables(rel_table, t)
    cmp_start = np.arange(nc) * CMP_STRIDE
    sel_start = np.arange(nsel) * SEL_BLOCK
    overlap = ((cmp_start[:, None] <= sel_start[None, :] + SEL_BLOCK - 1)
               & (cmp_start[:, None] + CMP_LEN - 1 >= sel_start[None, :]) & (cmp_start[:, None] < (nc - 1) * CMP_STRIDE))
    overlap = jnp.asarray(overlap.T, BF16)
    lane_head = np.arange(RWKV_WIDTH) // RWKV_HEAD
    hsum = jnp.asarray(lane_head[:, None] == lane_head[None, :], BF16)
    quad_head = np.arange(QUAD_LANES) // RWKV_HEAD
    ones_bd = jnp.asarray(quad_head[:, None] == quad_head[None, :], BF16)
    step = np.arange(SCAN_TT)[None, :, None]
    lane = np.arange(LANE)[None, None, :]
    sbh = np.arange(SCAN_TT // SUB_STEPS * QUAD)[:, None, None]
    place = jnp.asarray((lane // SUB_STEPS == sbh % QUAD) & (step == (sbh // QUAD) * SUB_STEPS + lane % SUB_STEPS), BF16)
    rw_f = _pad_to(router_w.astype(F32), (d, LANE))
    rw_hi = rw_f.astype(BF16)
    rw2 = jnp.stack([rw_hi, (rw_f - rw_hi.astype(F32)).astype(BF16)])

    xf = x.reshape(n, d)
    mem_f = mem.reshape(bsz * mem.shape[1], d)
    v_first = None
    for l in range(depth):
        w_in_l = _permute_in_proj(w_in[l])
        z = _matmul(xf, w_in_l, F32, 1024, 1024)

        kv0 = NSA_WIDTH
        zeros64 = jnp.zeros((d, LANE - hd), BF16)
        w_kv = jnp.stack([jnp.concatenate(
            [piece for i in range(KV_ARRAYS)
             for piece in (w_in[l][:, kv0 + i * NSA_KV + gi * hd:kv0 + i * NSA_KV + (gi + 1) * hd].astype(BF16), zeros64)],
            axis=1) for gi in range(g)])
        kc, vc, ks_aug, vs_aug, kw_pad, vw_aug = _kv_proj(xf, w_kv, bsz, t)

        def cmp_w(pe, w1, w2):
            return (pe.reshape(2, CMP_STRIDE * hd), w1.reshape(2, CMP_STRIDE * hd, CMP_HIDDEN).astype(BF16),
                    _pad_to(w2, (CMP_HIDDEN, LANE)).astype(BF16))

        chunks = lambda a: a.reshape(bsz, g, nc, CMP_STRIDE * hd)
        k_cmp = _compress(chunks(kc), *cmp_w(cmp_pe_k[l], cmp_w1k[l], cmp_w2k[l]))
        v_cmp = _compress(chunks(vc), *cmp_w(cmp_pe_v[l], cmp_w1v[l], cmp_w2v[l]))
        o_c, sel_mask = _cmp_branch(z, k_cmp, v_cmp, pc_tab, overlap, bsz, t)
        o_s = _sel_branch(z, ks_aug, vs_aug, sel_mask, lt_tab, cq_tab, bsz, t)
        o_w = _win_branch(z, kw_pad, vw_aug, wb_tab, bsz, t)

        mu = rwkv_mu[l]
        w3 = 3 * RWKV_WIDTH
        mu5 = jnp.stack([mu[0:512], mu[512:1024], mu[1024:1536], _pad_to(mu[w3:w3 + 128], (512,)),
                         _pad_to(mu[w3 + 128:w3 + 256], (512,))])
        vecs = jnp.stack([rwkv_w0[l], rwkv_a0[l], rwkv_kk[l], rwkv_ka[l], rwkv_rk[l].reshape(-1)])
        w2p = jnp.concatenate([rwkv_w2[l], jnp.zeros_like(rwkv_a2[l])], axis=0).astype(BF16)
        a2p = jnp.concatenate([jnp.zeros_like(rwkv_w2[l]), rwkv_a2[l]], axis=0).astype(BF16)
        vres = None
        if l > 0:
            vres = (v_first, rwkv_v0[l - 1][None, :], _pad_to(rwkv_v1[l - 1], (RWKV_WIDTH, LANE)).astype(BF16),
                    _pad_to(rwkv_v2[l - 1], (LANE, RWKV_WIDTH)).astype(BF16))
        r, w, k, v, kk, b, g_out, bonus, vt = _rwkv_prep(z, mu5, vecs, w2p, a2p, rwkv_g2[l].astype(BF16), hsum, vres,
                                                          bsz, t)
        if l == 0:
            v_first = v
        seq = lambda a: a.reshape(bsz, t, RWKV_WIDTH)
        o_rw = _rwkv_scan(seq(r), seq(w), seq(k), seq(kk), seq(b), vt, ones_bd, place, bsz, t)
        o_rw = o_rw.transpose(1, 0, 2, 3).reshape(n, RWKV_WIDTH)

        xf = _merge(xf, o_c, o_s, o_w, o_rw, bonus, g_out, z, jnp.stack([rwkv_gn_g[l], rwkv_gn_b[l]]), hsum,
                    p_nsa[l].astype(BF16), p_rwkv[l].astype(BF16), w_out[l].astype(BF16),
                    jnp.stack([ln1_g[l], ln1_b[l]]), bsz, t)

        mlen = mem.shape[1]
        mk = _matmul(mem_f, xk_w[l].astype(BF16), BF16, mlen, XATTN_WIDTH)
        mv = _matmul(mem_f, xv_w[l].astype(BF16), BF16, mlen, XATTN_WIDTH)
        kt = mk.reshape(bsz, mlen, XATTN_WIDTH).transpose(0, 2, 1)
        xf, scores = _xattn(xf, kt, mv.reshape(bsz, mlen, XATTN_WIDTH), xq_w[l].astype(BF16), xo_w[l].astype(BF16),
                            jnp.stack([ln2_g[l], ln2_b[l]]), rw2, bsz, t)

        gate, dest, tok_of_row, block_expert = _route(scores, router_bias, n)
        rows = jnp.take(xf.astype(BF16), tok_of_row, axis=0)
        y_rows = _experts(block_expert, rows, moe_w_gate[l], moe_w_up[l], moe_w_down[l])
        xf = _combine(xf, jnp.take(y_rows, dest[:, 0], axis=0), jnp.take(y_rows, dest[:, 1], axis=0), gate,
                      jnp.stack([ln3_g[l], ln3_b[l]]))
    return xf.reshape(bsz, t, d)
```

```python
import functools
import math

import numpy as np
import jax
import jax.numpy as jnp
from jax import lax
from jax.experimental import pallas as pl
from jax.experimental.pallas import tpu as pltpu

F32 = jnp.float32
BF16 = jnp.bfloat16

D_MODEL = 1024
DEPTH = 2
NSA_HEADS = 8
NSA_GROUPS = 2
NSA_HPG = NSA_HEADS // NSA_GROUPS
HEAD_DIM = 64
NSA_WIDTH = NSA_HEADS * HEAD_DIM
NSA_KV = NSA_GROUPS * HEAD_DIM
CMP_STRIDE = 16
CMP_LEN = 2 * CMP_STRIDE
CMP_HIDDEN = 256
SEL_BLOCK = 64
SEL_TOPN = 16
WINDOW = 512
Q_BLOCK = 128
SEL_FORCE = 1e9
RWKV_HEADS = 8
RWKV_HEAD = 64
RWKV_WIDTH = RWKV_HEADS * RWKV_HEAD
LORA_W = 64
LORA_A = 64
LORA_V = 32
LORA_G = 128
GN_EPS = 64e-5
REL_BUCKETS = 32
REL_MAX_DIST = 1024
XATTN_HEADS = 4
XATTN_HEAD = 128
XATTN_WIDTH = XATTN_HEADS * XATTN_HEAD
N_EXPERTS = 16
N_EXPERT_GROUPS = 4
EXPERTS_PER_GROUP = N_EXPERTS // N_EXPERT_GROUPS
TOP_K = 2
EXPERT_FF = 512
DN_ALPHA = (2 * DEPTH) ** 0.25
LN_EPS = 1e-5
NEG_INF = -1e30
NSA_COLS = NSA_WIDTH + 6 * NSA_KV + 3 * NSA_HEADS
RWKV_COLS = 3 * RWKV_WIDTH + LORA_W + LORA_A + LORA_G

LANE = 128
VMEM_LIMIT = 56 * 1024 * 1024

C_Q = 0
C_GN = 1024
C_GR = 2048
C_R = 3072
C_K = 3584
C_V = 4096
C_G3 = 4608
C_WA = 4864
C_ZG = 4992
IN_PAD = 5120
KV_ARRAYS = 6

MOE_BLK = 256
SCAN_TT = 128
FAR_BIAS_DIST = 1280


def _cparams(sem):
    return pltpu.CompilerParams(dimension_semantics=sem, vmem_limit_bytes=VMEM_LIMIT)


def _sigmoid(x):
    return 1.0 / (1.0 + jnp.exp(-x))


def _layer_norm(v, g, b):
    mu = jnp.mean(v, axis=-1, keepdims=True)
    d = v - mu
    var = jnp.mean(d * d, axis=-1, keepdims=True)
    return d * lax.rsqrt(var + LN_EPS) * g + b


def _dot(a, b):
    return jnp.dot(a, b, preferred_element_type=F32)


def _dot_nt(a, b):
    return lax.dot_general(a, b, (((1,), (1,)), ((), ())), preferred_element_type=F32)


def _dot_split(x, w):
    hi = x.astype(BF16)
    lo = (x - hi.astype(F32)).astype(BF16)
    return _dot(hi, w) + _dot(lo, w)


def _mm_kernel(x_ref, w_ref, o_ref, xb_ref):
    @pl.when(pl.program_id(1) == 0)
    def _():
        xb_ref[...] = x_ref[...].astype(BF16)

    o_ref[...] = _dot(xb_ref[...], w_ref[...]).astype(o_ref.dtype)


def _matmul(x, w, out_dtype, tm, tn):
    n, k = x.shape
    m = w.shape[1]
    return pl.pallas_call(
        _mm_kernel,
        grid=(n // tm, m // tn),
        in_specs=[pl.BlockSpec((tm, k), lambda i, j: (i, 0)),
                  pl.BlockSpec((k, tn), lambda i, j: (0, j))],
        out_specs=pl.BlockSpec((tm, tn), lambda i, j: (i, j)),
        out_shape=jax.ShapeDtypeStruct((n, m), out_dtype),
        scratch_shapes=[pltpu.VMEM((tm, k), BF16)],
        compiler_params=_cparams(("parallel", "arbitrary")),
        name="matmul",
    )(x, w)


def _kv_proj_kernel(tiles_per_seq, x_ref, w_ref, kc_ref, vc_ref, ks_ref, vs_ref, kw_ref, vw_ref, xb_ref):
    @pl.when(pl.program_id(1) == 0)
    def _():
        xb_ref[...] = x_ref[...].astype(BF16)

    y = _dot(xb_ref[...], w_ref[0])
    tm = y.shape[0]
    slot = lambda i: y[:, i * LANE:(i + 1) * LANE]
    lane = lax.broadcasted_iota(jnp.int32, (tm, LANE), 1)
    one_lane = jnp.where(lane == HEAD_DIM, 1.0, 0.0)
    t0 = lax.rem(pl.program_id(0), tiles_per_seq) * tm
    block = jnp.right_shift(t0 + lax.broadcasted_iota(jnp.int32, (tm, LANE), 0), int(math.log2(SEL_BLOCK)))
    kc_ref[0, 0] = slot(0)[:, :HEAD_DIM]
    vc_ref[0, 0] = slot(1)[:, :HEAD_DIM]
    ks_ref[0, 0] = jnp.concatenate([slot(2) + jnp.where(lane == HEAD_DIM + 1, 1.0, one_lane),
                                    jnp.where(lane == block, 1.0, 0.0)], axis=1).astype(BF16)
    vs_ref[0, 0] = (slot(3) + one_lane).astype(BF16)
    kw_ref[0, 0] = slot(4).astype(BF16)
    vw_ref[0, 0] = (slot(5) + one_lane).astype(BF16)


def _kv_proj(x, w_kv, bsz, t, tm=512):
    n, d = x.shape
    tps = t // tm
    g = NSA_GROUPS

    def out(width, dtype):
        return (pl.BlockSpec((1, 1, tm, width), lambda i, j: (i // tps, j, i % tps, 0)),
                jax.ShapeDtypeStruct((bsz, g, t, width), dtype))

    outs = [out(HEAD_DIM, F32), out(HEAD_DIM, F32), out(2 * LANE, BF16), out(LANE, BF16), out(LANE, BF16),
            out(LANE, BF16)]
    return pl.pallas_call(
        functools.partial(_kv_proj_kernel, tps),
        grid=(n // tm, g),
        in_specs=[pl.BlockSpec((tm, d), lambda i, j: (i, 0)),
                  pl.BlockSpec((1, d, KV_ARRAYS * LANE), lambda i, j: (j, 0, 0))],
        out_specs=[o[0] for o in outs],
        out_shape=[o[1] for o in outs],
        scratch_shapes=[pltpu.VMEM((tm, d), BF16)],
        compiler_params=_cparams(("parallel", "arbitrary")),
        name="nsa_kv_proj",
    )(x, w_kv)


def _compress_kernel(u_ref, pe_ref, w1_ref, w2_ref, o_ref):
    u = u_ref[0, 0]
    a = _dot((u + pe_ref[0:1, :]).astype(BF16), w1_ref[0])
    b = _dot((u + pe_ref[1:2, :]).astype(BF16), w1_ref[1])
    nc = u.shape[0]
    h = a + pltpu.roll(b, nc - 1, 0)
    h = 0.5 * h * (1.0 + jnp.tanh(math.sqrt(2.0 / math.pi) * (h + 0.044715 * (h * h * h))))
    y = _dot(h.astype(BF16), w2_ref[...])
    row = lax.broadcasted_iota(jnp.int32, y.shape, 0)
    o_ref[0, 0] = jnp.where(row < nc - 1, y, 0.0)


def _compress(u, pe2, w1, w2p):
    b, g, nc, _ = u.shape
    return pl.pallas_call(
        _compress_kernel,
        grid=(b, g),
        in_specs=[pl.BlockSpec((1, 1, nc, 1024), lambda i, j: (i, j, 0, 0)),
                  pl.BlockSpec((2, 1024), lambda i, j: (0, 0)),
                  pl.BlockSpec((2, 1024, CMP_HIDDEN), lambda i, j: (0, 0, 0)),
                  pl.BlockSpec((CMP_HIDDEN, LANE), lambda i, j: (0, 0))],
        out_specs=pl.BlockSpec((1, 1, nc, LANE), lambda i, j: (i, j, 0, 0)),
        out_shape=jax.ShapeDtypeStruct((b, g, nc, LANE), F32),
        compiler_params=_cparams(("parallel", "parallel")),
        name="nsa_compress",
    )(u, pe2, w1, w2p)


def _stack_heads(q_ref, row0=0):
    q = q_ref[row0:row0 + Q_BLOCK, :] * (HEAD_DIM ** -0.5)
    return jnp.concatenate([q[:, h * LANE:(h + 1) * LANE] for h in range(NSA_HPG)], axis=0).astype(BF16)


def _store_heads(o_ref, o, gates, branch, row0=0):
    lane = lax.broadcasted_iota(jnp.int32, (Q_BLOCK, LANE), 1)
    for pair in range(NSA_HPG // 2):
        halves = []
        for h in (2 * pair, 2 * pair + 1):
            c = branch * NSA_HPG + h
            halves.append(o[h * Q_BLOCK:(h + 1) * Q_BLOCK, :] * gates[:, c:c + 1])
        o_ref[0, pair, row0:row0 + Q_BLOCK, :] = jnp.where(lane < HEAD_DIM, halves[0],
                                                           pltpu.roll(halves[1], HEAD_DIM, 1))


def _cmp_kernel(q_ref, zg_ref, kc_ref, vc_ref, pc_ref, ov_ref, o_ref, m_ref):
    c = pl.program_id(2)
    nc = kc_ref.shape[2]
    nsel = ov_ref.shape[0]
    qs = _stack_heads(q_ref)
    s = _dot_nt(qs, kc_ref[0, 0].astype(BF16))
    off = (nc - 8) - 8 * c
    s = s + pc_ref[0, 0, :, pl.ds(pl.multiple_of(lax.div(off, LANE) * LANE, LANE), nc)]
    m = jnp.max(s, axis=-1, keepdims=True)
    p = jnp.exp(s - m)
    l = jnp.sum(p, axis=-1, keepdims=True)
    ql = lax.broadcasted_iota(jnp.int32, (NSA_HPG * Q_BLOCK, 1), 0) & (Q_BLOCK - 1)
    any_key = (c * Q_BLOCK + ql) >= (CMP_LEN - 1)
    p = p * jnp.where(any_key, 1.0 / l, 0.0)
    gates = _sigmoid(zg_ref[...])
    _store_heads(o_ref, _dot(p.astype(BF16), vc_ref[0, 0].astype(BF16)), gates, 0)

    psum = p[0:Q_BLOCK] + p[Q_BLOCK:2 * Q_BLOCK] + p[2 * Q_BLOCK:3 * Q_BLOCK] + p[3 * Q_BLOCK:4 * Q_BLOCK]
    hi = psum.astype(BF16)
    lo = (psum - hi.astype(F32)).astype(BF16)
    imp = _dot_nt(ov_ref[...], hi) + _dot_nt(ov_ref[...], lo)
    jb = lax.broadcasted_iota(jnp.int32, (nsel, Q_BLOCK), 0)
    tq = c * Q_BLOCK + lax.broadcasted_iota(jnp.int32, (nsel, Q_BLOCK), 1)
    jq = jnp.right_shift(tq, int(math.log2(SEL_BLOCK)))
    forced = (jb == 0) | (jb == jq) | (jb == jq - 1)
    val = jnp.where(forced, SEL_FORCE, jnp.where(jb <= jq, imp, -SEL_FORCE))
    sel = jnp.zeros((nsel, Q_BLOCK), F32)
    jbf = jb.astype(F32)
    for _ in range(SEL_TOPN):
        mx = jnp.max(val, axis=0, keepdims=True)
        first = jnp.min(jnp.where(val == mx, jbf, float(nsel)), axis=0, keepdims=True)
        hit = jbf == first
        sel = jnp.where(hit, jnp.where(mx >= 0.0, 1.0, 0.0), sel)
        val = jnp.where(hit, -3e38, val)
    m_ref[0, 0] = (sel.T - 1.0) * (-NEG_INF)


def _cmp_branch(z, k_cmp, v_cmp, pc_tab, overlap, bsz, t):
    nq = t // Q_BLOCK
    nc = k_cmp.shape[2]
    nsel = overlap.shape[0]
    return pl.pallas_call(
        _cmp_kernel,
        grid=(bsz, NSA_GROUPS, nq),
        in_specs=[pl.BlockSpec((Q_BLOCK, NSA_HPG * LANE), lambda b, g, c: (b * nq + c, g)),
                  pl.BlockSpec((Q_BLOCK, LANE), lambda b, g, c: (b * nq + c, C_G3 // LANE + g)),
                  pl.BlockSpec((1, 1, nc, LANE), lambda b, g, c: (b, g, 0, 0)),
                  pl.BlockSpec((1, 1, nc, LANE), lambda b, g, c: (b, g, 0, 0)),
                  pl.BlockSpec((1, 1, NSA_HPG * Q_BLOCK, 2 * nc - LANE),
                               lambda b, g, c: (g, lax.rem((nc - 8) - 8 * c, LANE) // 8, 0, 0)),
                  pl.BlockSpec((nsel, nc), lambda b, g, c: (0, 0))],
        out_specs=[pl.BlockSpec((1, NSA_HPG // 2, Q_BLOCK, LANE), lambda b, g, c: (b, g, c, 0)),
                   pl.BlockSpec((1, 1, Q_BLOCK, nsel), lambda b, g, c: (b, g, c, 0))],
        out_shape=[jax.ShapeDtypeStruct((bsz, NSA_HEADS // 2, t, LANE), F32),
                   jax.ShapeDtypeStruct((bsz, NSA_GROUPS, t, nsel), F32)],
        compiler_params=_cparams(("parallel", "parallel", "arbitrary")),
        name="nsa_cmp_select",
    )(z, z, k_cmp, v_cmp, pc_tab, overlap)


WIN_QB = 2


def _win_kernel(q_ref, zg_ref, k_ref, v_ref, wb_ref, o_ref, *scratch):
    nvar = wb_ref.shape[1]
    gates = _sigmoid(zg_ref[...])
    blocks = []
    for qb in range(WIN_QB):
        c = pl.program_id(2) * WIN_QB + qb
        start = pl.multiple_of(jnp.maximum(c * Q_BLOCK - WINDOW, 0), Q_BLOCK)
        blocks.append((qb, jnp.minimum(c, nvar - 1), pl.ds(start, WINDOW + Q_BLOCK), scratch[2 * qb], scratch[2 * qb + 1]))
    for qb, _, keys, s_s, _ in blocks:
        s_s[...] = _dot_nt(_stack_heads(q_ref, qb * Q_BLOCK), k_ref[0, 0, keys, :])
    for qb, var, _, s_s, p_s in blocks:
        for r in range(NSA_HPG * Q_BLOCK // SEL_ROWS):
            rows = slice(r * SEL_ROWS, (r + 1) * SEL_ROWS)
            s = s_s[rows, :] + wb_ref[0, var, rows, :]
            p_s[rows, :] = jnp.exp(s - jnp.max(s, axis=-1, keepdims=True)).astype(BF16)
    for qb, _, keys, _, p_s in blocks:
        o = _dot(p_s[...], v_ref[0, 0, keys, :])
        o = o * (1.0 / o[:, HEAD_DIM:HEAD_DIM + 1])
        _store_heads(o_ref, o, gates[qb * Q_BLOCK:(qb + 1) * Q_BLOCK, :], 2, qb * Q_BLOCK)


def _win_branch(z, kw, vw, wb_tab, bsz, t):
    tq = WIN_QB * Q_BLOCK
    nq = t // tq
    rows, width = NSA_HPG * Q_BLOCK, WINDOW + Q_BLOCK
    return pl.pallas_call(
        _win_kernel,
        grid=(bsz, NSA_GROUPS, nq),
        in_specs=[pl.BlockSpec((tq, NSA_HPG * LANE), lambda b, g, c: (b * nq + c, g)),
                  pl.BlockSpec((tq, LANE), lambda b, g, c: (b * nq + c, C_G3 // LANE + g)),
                  pl.BlockSpec((1, 1, t, LANE), lambda b, g, c: (b, g, 0, 0)),
                  pl.BlockSpec((1, 1, t, LANE), lambda b, g, c: (b, g, 0, 0)),
                  pl.BlockSpec((1,) + wb_tab.shape[1:], lambda b, g, c: (g, 0, 0, 0))],
        out_specs=pl.BlockSpec((1, NSA_HPG // 2, tq, LANE), lambda b, g, c: (b, g, c, 0)),
        out_shape=jax.ShapeDtypeStruct((bsz, NSA_HEADS // 2, t, LANE), F32),
        scratch_shapes=[pltpu.VMEM((rows, width), F32), pltpu.VMEM((rows, width), BF16)] * WIN_QB,
        compiler_params=_cparams(("parallel", "parallel", "arbitrary")),
        name="nsa_window",
    )(z, z, kw, vw, wb_tab)


SEL_TK = 512


SEL_ROWS = 64
SEL_NEAR = FAR_BIAS_DIST + Q_BLOCK


def _sel_kernel(q_ref, zg_ref, k_ref, v_ref, mt_ref, lt_ref, cq_ref, o_ref, m_s, acc_s, s0_s, s1_s, p0_s, p1_s,
                a0_s, a1_s):
    c = pl.program_id(2)
    q = q_ref[...] * (HEAD_DIM ** -0.5)
    qs = jnp.concatenate([q[:, h * LANE:(h + 1) * LANE] for h in range(NSA_HPG)], axis=0) + cq_ref[0]
    mt = mt_ref[0, 0]
    q_aug = jnp.concatenate([qs, jnp.concatenate([mt] * NSA_HPG, axis=0)], axis=1).astype(BF16)
    m_s[...] = jnp.full(m_s.shape, NEG_INF, F32)
    acc_s[...] = jnp.zeros(acc_s.shape, F32)
    ncol = SEL_TK // LANE
    n_tiles = lax.div(c, SEL_TK // Q_BLOCK) + 1

    def key_rows(i):
        return pl.ds(pl.multiple_of(jnp.minimum(i, n_tiles - 1) * SEL_TK, SEL_TK), SEL_TK)

    def scores(i, dst):
        dst[...] = _dot_nt(q_aug, k_ref[0, 0, key_rows(i), :])

    def tile(i, cur, nxt, p_buf, a_buf):
        scores(i + 1, nxt)
        off = jnp.where(i < n_tiles, jnp.maximum(SEL_NEAR + SEL_TK - (c * Q_BLOCK - i * SEL_TK), 0),
                        SEL_NEAR + SEL_TK + Q_BLOCK)
        off = pl.multiple_of(off, LANE)
        for r in range(NSA_HPG * Q_BLOCK // SEL_ROWS):
            rows = slice(r * SEL_ROWS, (r + 1) * SEL_ROWS)
            s = cur[rows, :] + lt_ref[0, rows, pl.ds(off, SEL_TK)]
            cols = [s[:, j * LANE:(j + 1) * LANE] for j in range(ncol)]
            mx = functools.reduce(jnp.maximum, cols)
            m_old = m_s[rows, :]
            m_new = jnp.maximum(m_old, jnp.max(mx, axis=-1, keepdims=True))
            m_s[rows, :] = m_new
            a_buf[rows, :] = jnp.exp(m_old - m_new)
            p_buf[rows, :] = jnp.concatenate([jnp.exp(col - m_new) for col in cols], axis=1).astype(BF16)
        acc_s[...] = a_buf[...] * acc_s[...] + _dot(p_buf[...], v_ref[0, 0, key_rows(i), :])

    scores(0, s0_s)

    def pair(j, carry):
        tile(2 * j, s0_s, s1_s, p0_s, a0_s)
        tile(2 * j + 1, s1_s, s0_s, p1_s, a1_s)
        return carry

    lax.fori_loop(0, lax.div(n_tiles + 1, 2), pair, 0)
    acc = acc_s[...]
    o = acc * (1.0 / acc[:, HEAD_DIM:HEAD_DIM + 1])
    _store_heads(o_ref, o, _sigmoid(zg_ref[...]), 1)


def _sel_branch(z, ks_aug, vs, mt, lt_tab, cq_tab, bsz, t):
    nq = t // Q_BLOCK
    nsel = mt.shape[-1]
    rows = NSA_HPG * Q_BLOCK
    return pl.pallas_call(
        _sel_kernel,
        grid=(bsz, NSA_GROUPS, nq),
        in_specs=[pl.BlockSpec((Q_BLOCK, NSA_HPG * LANE), lambda b, g, c: (b * nq + c, g)),
                  pl.BlockSpec((Q_BLOCK, LANE), lambda b, g, c: (b * nq + c, C_G3 // LANE + g)),
                  pl.BlockSpec((1, 1, t, 2 * LANE), lambda b, g, c: (b, g, 0, 0)),
                  pl.BlockSpec((1, 1, t, LANE), lambda b, g, c: (b, g, 0, 0)),
                  pl.BlockSpec((1, 1, Q_BLOCK, nsel), lambda b, g, c: (b, g, c, 0)),
                  pl.BlockSpec((1, rows, lt_tab.shape[2]), lambda b, g, c: (g, 0, 0)),
                  pl.BlockSpec((1, rows, LANE), lambda b, g, c: (g, 0, 0))],
        out_specs=pl.BlockSpec((1, NSA_HPG // 2, Q_BLOCK, LANE), lambda b, g, c: (b, g, c, 0)),
        out_shape=jax.ShapeDtypeStruct((bsz, NSA_HEADS // 2, t, LANE), F32),
        scratch_shapes=[pltpu.VMEM((rows, LANE), F32),
                        pltpu.VMEM((rows, LANE), F32),
                        pltpu.VMEM((rows, SEL_TK), F32),
                        pltpu.VMEM((rows, SEL_TK), F32),
                        pltpu.VMEM((rows, SEL_TK), BF16),
                        pltpu.VMEM((rows, SEL_TK), BF16),
                        pltpu.VMEM((rows, LANE), F32),
                        pltpu.VMEM((rows, LANE), F32)],
        compiler_params=_cparams(("parallel", "parallel", "arbitrary")),
        name="nsa_selected",
    )(z, z, ks_aug, vs, mt, lt_tab, cq_tab)


def _shift_mix(z_ref, prev_ref, mu, first):
    z = z_ref[...]
    prev = jnp.where(first, 0.0, prev_ref[7:8, :])
    row = lax.broadcasted_iota(jnp.int32, z.shape, 0)
    zp = jnp.where(row == 0, prev, pltpu.roll(z, 1, 0))
    return z + mu * (zp - z)


def _rwkv_prep_kernel(has_vres, tiles_per_seq, *refs):
    (zr, zk, zv, zwa, zg, pr, pk, pv, pwa, pg, mu_ref, vec_ref, w2_ref, a2_ref, g2_ref, hsum_ref) = refs[:16]
    pos = 16
    if has_vres:
        vf_ref, v0_ref, v1_ref, v2_ref = refs[pos:pos + 4]
        pos += 4
    r_o, w_o, k_o, v_o, kk_o, b_o, g_o, bonus_o, vt_o = refs[pos:]
    first = lax.rem(pl.program_id(0), tiles_per_seq) == 0
    r = _shift_mix(zr, pr, mu_ref[0:1, :], first)
    k = _shift_mix(zk, pk, mu_ref[1:2, :], first)
    v = _shift_mix(zv, pv, mu_ref[2:3, :], first)
    wa = _shift_mix(zwa, pwa, mu_ref[3:4, 0:LANE], first)
    zg_s = _shift_mix(zg, pg, mu_ref[4:5, 0:LANE], first)
    w0, a0, k_k, k_a, r_k = (vec_ref[i:i + 1, :] for i in range(5))
    if has_vres:
        lora = _dot(_dot(v.astype(BF16), v1_ref[...]).astype(BF16), v2_ref[...])
        v = v + (vf_ref[...] - v) * _sigmoid(v0_ref[...] + lora)
    u = w0 + _dot(jnp.tanh(wa).astype(BF16), w2_ref[...])
    decay = jnp.exp(-math.exp(-0.5) * _sigmoid(u))
    a = _sigmoid(a0 + _dot(wa.astype(BF16), a2_ref[...]))
    g = _dot(_sigmoid(zg_s).astype(BF16), g2_ref[...])
    kk = k * k_k
    kk = kk / jnp.maximum(jnp.sqrt(_dot_split(kk * kk, hsum_ref[...])), 1e-12)
    k = k * (1.0 + (a - 1.0) * k_a)
    r_o[...] = r
    w_o[...] = decay
    k_o[...] = k
    v_o[...] = v
    kk_o[...] = kk
    b_o[...] = kk * a
    g_o[...] = g
    bonus_o[...] = _dot_split(r * k * r_k, hsum_ref[...]) * v
    vt_o[0] = v.T.astype(BF16)


def _rwkv_prep(z, mu5, vecs, w2p, a2p, g2, hsum, vres, bsz, t, tm=256):
    n = bsz * t
    nt = n // tm
    tps = t // tm
    w = RWKV_WIDTH

    def cur(width, col):
        return pl.BlockSpec((tm, width), lambda i: (i, col // width))

    def prev(width, col):
        return pl.BlockSpec((8, width), lambda i: (jnp.maximum(i * (tm // 8) - 1, 0), col // width))

    def full(shape):
        return pl.BlockSpec(shape, lambda i: (0,) * len(shape))

    cols = [(w, C_R), (w, C_K), (w, C_V), (LANE, C_WA), (LANE, C_ZG)]
    in_specs = [cur(*c) for c in cols] + [prev(*c) for c in cols]
    in_specs += [full(mu5.shape), full(vecs.shape), full(w2p.shape), full(a2p.shape), full(g2.shape), full(hsum.shape)]
    args = [z] * 10 + [mu5, vecs, w2p, a2p, g2, hsum]
    if vres is not None:
        v_first, v0, v1p, v2p = vres
        in_specs += [pl.BlockSpec((tm, w), lambda i: (i, 0)), full(v0.shape), full(v1p.shape), full(v2p.shape)]
        args += [v_first, v0, v1p, v2p]
    tok = pl.BlockSpec((tm, w), lambda i: (i, 0))
    out_specs = [tok] * 8 + [pl.BlockSpec((1, w, tm), lambda i: (i // tps, 0, i % tps))]
    out_shape = [jax.ShapeDtypeStruct((n, w), F32)] * 8 + [jax.ShapeDtypeStruct((bsz, w, t), BF16)]
    return pl.pallas_call(
        functools.partial(_rwkv_prep_kernel, vres is not None, tps),
        grid=(nt,),
        in_specs=in_specs,
        out_specs=out_specs,
        out_shape=out_shape,
        compiler_params=_cparams(("parallel",)),
        name="rwkv_prep",
    )(*args)


QUAD = 4
QUAD_LANES = QUAD * RWKV_HEAD
SUB_STEPS = LANE // QUAD


def _scan_kernel(r_ref, w_ref, k_ref, kk_ref, b_ref, vt_ref, ones_ref, place_ref, o_ref, s_ref, vq_ref, kq_ref, vk_ref):
    bsz = r_ref.shape[0]
    tt = r_ref.shape[1]
    nh = RWKV_HEAD
    quads = [(bi, hf) for bi in range(bsz) for hf in range(RWKV_HEADS // QUAD)]

    @pl.when(pl.program_id(0) == 0)
    def _():
        s_ref[...] = jnp.zeros(s_ref.shape, F32)

    sub_shift, head_shift = int(math.log2(SUB_STEPS)), int(math.log2(nh))
    sub_lane = lax.broadcasted_iota(jnp.int32, (nh, LANE), 1) & (SUB_STEPS - 1)
    k_row_head = jnp.right_shift(lax.broadcasted_iota(jnp.int32, (LANE, QUAD_LANES), 0), sub_shift)
    k_lane_head = jnp.right_shift(lax.broadcasted_iota(jnp.int32, (LANE, QUAD_LANES), 1), head_shift)
    r_mask = (lax.broadcasted_iota(jnp.int32, (16, QUAD_LANES), 0)
              == jnp.right_shift(lax.broadcasted_iota(jnp.int32, (16, QUAD_LANES), 1), head_shift))
    ones_bd = ones_ref[...]

    def sub_block(sb, carry):
        s0 = pl.multiple_of(sb * SUB_STEPS, SUB_STEPS)
        for q, (bi, hf) in enumerate(quads):
            ln = slice(hf * QUAD_LANES, (hf + 1) * QUAD_LANES)
            vq = _dot(vt_ref[bi, (QUAD * hf) * nh:(QUAD * hf + 1) * nh, :], place_ref[sb * QUAD])
            for h in range(1, QUAD):
                vq = vq + _dot(vt_ref[bi, (QUAD * hf + h) * nh:(QUAD * hf + h + 1) * nh, :], place_ref[sb * QUAD + h])
            vq_ref[q] = vq.astype(BF16)
            k_sub = k_ref[bi, pl.ds(s0, SUB_STEPS), ln]
            kq_ref[q] = jnp.where(k_row_head == k_lane_head, jnp.concatenate([k_sub] * QUAD, axis=0), 0.0).astype(BF16)

        def group(tg, carry):
            t0 = pl.multiple_of(s0 + tg * 8, 8)
            rows = []
            for bi, hf in quads:
                ln = slice(hf * QUAD_LANES, (hf + 1) * QUAD_LANES)
                rows.append(tuple(ref[bi, pl.ds(t0, 8), ln] for ref in (w_ref, kk_ref, b_ref, r_ref)))
            onehots = [jnp.where(sub_lane == tg * 8 + u, 1.0, 0.0).astype(BF16) for u in range(8)]
            for q in range(len(quads)):
                vq = vq_ref[q]
                vk_ref[q] = _dot(jnp.concatenate([vq * oh for oh in onehots], axis=0), kq_ref[q])
            state = [s_ref[q] for q in range(len(quads))]
            for u in range(8):
                sk = _dot(jnp.concatenate([(state[q] * rows[q][1][u:u + 1, :]).astype(BF16)
                                           for q in range(len(quads))], axis=0), ones_bd)
                for q, (bi, hf) in enumerate(quads):
                    w8, _, b8, r8 = rows[q]
                    state[q] = (state[q] * w8[u:u + 1, :] - sk[q * nh:(q + 1) * nh, :] * b8[u:u + 1, :]
                                + vk_ref[q, u * nh:(u + 1) * nh, :])
                    r_lhs = jnp.where(r_mask, r8[u:u + 1, :], 0.0).astype(BF16)
                    o = _dot_nt(r_lhs, state[q].astype(BF16))
                    o_ref[t0 + u, bi, QUAD * hf:QUAD * (hf + 1), :] = o[0:QUAD, :]
            for q in range(len(quads)):
                s_ref[q] = state[q]
            return carry

        return lax.fori_loop(0, SUB_STEPS // 8, group, carry)

    lax.fori_loop(0, tt // SUB_STEPS, sub_block, 0)


def _rwkv_scan(r, w, k, kk, b, vt, ones_bd, place, bsz, t):
    tt = SCAN_TT
    wd = RWKV_WIDTH
    nquad = bsz * RWKV_HEADS // QUAD
    tok = pl.BlockSpec((bsz, tt, wd), lambda i: (0, i, 0))
    return pl.pallas_call(
        _scan_kernel,
        grid=(t // tt,),
        in_specs=[tok, tok, tok, tok, tok,
                  pl.BlockSpec((bsz, wd, tt), lambda i: (0, 0, i)),
                  pl.BlockSpec(ones_bd.shape, lambda i: (0, 0)),
                  pl.BlockSpec(place.shape, lambda i: (0, 0, 0))],
        out_specs=pl.BlockSpec((tt, bsz, RWKV_HEADS, RWKV_HEAD), lambda i: (i, 0, 0, 0)),
        out_shape=jax.ShapeDtypeStruct((t, bsz, RWKV_HEADS, RWKV_HEAD), F32),
        scratch_shapes=[pltpu.VMEM((nquad, RWKV_HEAD, QUAD_LANES), F32),
                        pltpu.VMEM((nquad, RWKV_HEAD, LANE), BF16),
                        pltpu.VMEM((nquad, LANE, QUAD_LANES), BF16),
                        pltpu.VMEM((nquad, 8 * RWKV_HEAD, QUAD_LANES), F32)],
        compiler_params=_cparams(("arbitrary",)),
        name="rwkv_scan",
    )(r, w, k, kk, b, vt, ones_bd, place)


def _merge_kernel(x_ref, oc_ref, os_ref, ow_ref, orw_ref, bonus_ref, g_ref, zgn_ref, zgr_ref, gn_ref, hsum_ref,
                  pn_ref, pr_ref, wo_ref, ln_ref, o_ref):
    o_n = jnp.concatenate([oc_ref[0, p] + os_ref[0, p] + ow_ref[0, p] for p in range(NSA_HEADS // 2)], axis=1)
    y_n = _dot(o_n.astype(BF16), pn_ref[...])
    o = orw_ref[...]
    mu = _dot_split(o, hsum_ref[...]) * (1.0 / RWKV_HEAD)
    d = o - mu
    var = _dot_split(d * d, hsum_ref[...]) * (1.0 / RWKV_HEAD)
    o = d * lax.rsqrt(var + GN_EPS) * gn_ref[0:1, :] + gn_ref[1:2, :]
    o = ((o + bonus_ref[...]) * g_ref[...]).astype(BF16)
    y1 = _sigmoid(zgn_ref[...]) * y_n + _sigmoid(zgr_ref[...]) * _dot(o, pr_ref[...])
    y = _dot(y1.astype(BF16), wo_ref[...])
    o_ref[...] = _layer_norm(DN_ALPHA * x_ref[...] + y, ln_ref[0:1, :], ln_ref[1:2, :])


def _merge(x, o_c, o_s, o_w, o_rw, bonus, g, z, gn, hsum, p_nsa, p_rwkv, w_out, ln, bsz, t, tm=256):
    n = bsz * t
    tps = t // tm
    d = D_MODEL
    w = RWKV_WIDTH
    head = pl.BlockSpec((1, NSA_HEADS // 2, tm, LANE), lambda i: (i // tps, 0, i % tps, 0))
    tok = pl.BlockSpec((tm, w), lambda i: (i, 0))

    def full(a):
        return pl.BlockSpec(a.shape, lambda i: (0,) * a.ndim)

    return pl.pallas_call(
        _merge_kernel,
        grid=(n // tm,),
        in_specs=[pl.BlockSpec((tm, d), lambda i: (i, 0)), head, head, head, tok, tok, tok,
                  pl.BlockSpec((tm, d), lambda i: (i, C_GN // d)),
                  pl.BlockSpec((tm, d), lambda i: (i, C_GR // d)),
                  full(gn), full(hsum), full(p_nsa), full(p_rwkv), full(w_out), full(ln)],
        out_specs=pl.BlockSpec((tm, d), lambda i: (i, 0)),
        out_shape=jax.ShapeDtypeStruct((n, d), F32),
        compiler_params=_cparams(("parallel",)),
        name="mix_merge_ln",
    )(x, o_c, o_s, o_w, o_rw, bonus, g, z, z, gn, hsum, p_nsa, p_rwkv, w_out, ln)


def _xattn_kernel(x_ref, kt_ref, v_ref, wq_ref, wo_ref, ln_ref, rw_ref, o_ref, s_ref):
    x = x_ref[...]
    q = _dot(x.astype(BF16), wq_ref[...]).astype(BF16)
    outs = []
    for h in range(XATTN_HEADS):
        hs = slice(h * XATTN_HEAD, (h + 1) * XATTN_HEAD)
        s = _dot(q[:, hs], kt_ref[0, hs, :]) * (XATTN_HEAD ** -0.5)
        m = jnp.max(s, axis=-1, keepdims=True)
        p = jnp.exp(s - m)
        l = jnp.sum(p, axis=-1, keepdims=True)
        outs.append(_dot(p.astype(BF16), v_ref[0, :, hs]) * (1.0 / l))
    o = jnp.concatenate(outs, axis=1).astype(BF16)
    x2 = _layer_norm(DN_ALPHA * x + _dot(o, wo_ref[...]), ln_ref[0:1, :], ln_ref[1:2, :])
    o_ref[...] = x2
    hi = x2.astype(BF16)
    lo = (x2 - hi.astype(F32)).astype(BF16)
    logits = _dot(hi, rw_ref[0]) + (_dot(lo, rw_ref[0]) + _dot(hi, rw_ref[1]))
    s_ref[...] = _sigmoid(logits)


def _xattn(x, kt, v, wq, wo, ln, rw2, bsz, t, tm=256):
    n = bsz * t
    tps = t // tm
    d = D_MODEL

    def full(a):
        return pl.BlockSpec(a.shape, lambda i: (0,) * a.ndim)

    return pl.pallas_call(
        _xattn_kernel,
        grid=(n // tm,),
        in_specs=[pl.BlockSpec((tm, d), lambda i: (i, 0)),
                  pl.BlockSpec((1,) + kt.shape[1:], lambda i: (i // tps, 0, 0)),
                  pl.BlockSpec((1,) + v.shape[1:], lambda i: (i // tps, 0, 0)),
                  full(wq), full(wo), full(ln), full(rw2)],
        out_specs=[pl.BlockSpec((tm, d), lambda i: (i, 0)), pl.BlockSpec((tm, LANE), lambda i: (i, 0))],
        out_shape=[jax.ShapeDtypeStruct((n, d), F32), jax.ShapeDtypeStruct((n, LANE), F32)],
        compiler_params=_cparams(("parallel",)),
        name="xattn_ln_router",
    )(x, kt, v, wq, wo, ln, rw2)


def _expert_kernel(be_ref, x_ref, wg_ref, wu_ref, wd_ref, o_ref, wg_s, wu_s, wd_s):
    i = pl.program_id(0)

    @pl.when((i == 0) | (be_ref[i] != be_ref[jnp.maximum(i - 1, 0)]))
    def _():
        wg_s[...] = wg_ref[0].astype(BF16)
        wu_s[...] = wu_ref[0].astype(BF16)
        wd_s[...] = wd_ref[0].astype(BF16)

    x = x_ref[...]
    hg = _dot(x, wg_s[...])
    h = hg * _sigmoid(hg) * _dot(x, wu_s[...])
    o_ref[...] = _dot(h.astype(BF16), wd_s[...]).astype(o_ref.dtype)


def _experts(block_expert, rows, w_gate, w_up, w_down):
    n_rows, d = rows.shape
    ff = w_gate.shape[2]
    return pl.pallas_call(
        _expert_kernel,
        grid_spec=pltpu.PrefetchScalarGridSpec(
            num_scalar_prefetch=1,
            grid=(n_rows // MOE_BLK,),
            in_specs=[pl.BlockSpec((MOE_BLK, d), lambda i, be: (i, 0)),
                      pl.BlockSpec((1, d, ff), lambda i, be: (be[i], 0, 0)),
                      pl.BlockSpec((1, d, ff), lambda i, be: (be[i], 0, 0)),
                      pl.BlockSpec((1, ff, d), lambda i, be: (be[i], 0, 0))],
            out_specs=pl.BlockSpec((MOE_BLK, d), lambda i, be: (i, 0)),
            scratch_shapes=[pltpu.VMEM((d, ff), BF16), pltpu.VMEM((d, ff), BF16), pltpu.VMEM((ff, d), BF16)]),
        out_shape=jax.ShapeDtypeStruct((n_rows, d), BF16),
        compiler_params=_cparams(("arbitrary",)),
        name="moe_experts",
    )(block_expert, rows, w_gate, w_up, w_down)


def _combine_kernel(x_ref, ya_ref, yb_ref, gate_ref, ln_ref, o_ref):
    y = gate_ref[:, 0:1] * ya_ref[...] + gate_ref[:, 1:2] * yb_ref[...]
    o_ref[...] = _layer_norm(DN_ALPHA * x_ref[...] + y, ln_ref[0:1, :], ln_ref[1:2, :])


def _combine(x, ya, yb, gate, ln, tm=512):
    n, d = x.shape
    tok = pl.BlockSpec((tm, d), lambda i: (i, 0))
    return pl.pallas_call(
        _combine_kernel,
        grid=(n // tm,),
        in_specs=[tok, tok, tok, pl.BlockSpec((tm, TOP_K), lambda i: (i, 0)),
                  pl.BlockSpec(ln.shape, lambda i: (0, 0))],
        out_specs=tok,
        out_shape=jax.ShapeDtypeStruct((n, d), F32),
        compiler_params=_cparams(("parallel",)),
        name="moe_combine_ln",
    )(x, ya, yb, gate, ln)


def _t5_bucket(dist):
    n = jnp.maximum(dist, 0)
    max_exact = REL_BUCKETS // 2
    log_ratio = jnp.log(jnp.maximum(n, 1).astype(F32) / max_exact) / math.log(REL_MAX_DIST / max_exact)
    large = jnp.minimum(max_exact + (log_ratio * (REL_BUCKETS - max_exact)).astype(jnp.int32), REL_BUCKETS - 1)
    return jnp.where(n < max_exact, n, large)


def _bias_tables(rel_table, t):
    nc = t // CMP_STRIDE
    f = rel_table[_t5_bucket(jnp.arange(t + 2048))].astype(F32).T
    ql = jnp.arange(Q_BLOCK)[:, None]

    def lookup(dist, ok):
        vals = jnp.take(f, jnp.clip(dist, 0, f.shape[-1] - 1), axis=1)
        vals = jnp.where(ok[None], vals, NEG_INF)
        return vals.reshape(NSA_GROUPS, NSA_HPG * dist.shape[0], dist.shape[1])

    j = jnp.arange(2 * nc)[None, :]
    dist_c = CMP_STRIDE * (nc - 8 - j) + ql - (CMP_LEN - 1)
    pc = lookup(dist_c, dist_c >= 0)
    pc = jnp.stack([pc[:, :, s:s + 2 * nc - LANE] for s in range(0, LANE, 8)], axis=1)
    nvar = WINDOW // Q_BLOCK + 1

    def toeplitz(base, width, hi, shift):
        period = width + Q_BLOCK
        m = np.arange(period)
        dist = base - np.where(m < width, m, m - period)
        row = jnp.take(f, np.clip(dist, 0, f.shape[-1] - 1), axis=1) - shift
        row = jnp.where(((dist >= 0) & (dist < hi))[None], row, NEG_INF)
        flat = jnp.tile(row, (1, Q_BLOCK))[:, :Q_BLOCK * (period - 1)]
        return flat.reshape(NSA_HEADS, Q_BLOCK, period - 1)[:, :, :width]

    wb = jnp.stack([toeplitz(Q_BLOCK * v, WINDOW + Q_BLOCK, WINDOW, 0.0) for v in range(nvar)], axis=1)
    wb = wb.reshape(NSA_GROUPS, NSA_HPG, nvar, Q_BLOCK, -1).transpose(0, 2, 1, 3, 4)
    wb = wb.reshape(NSA_GROUPS, nvar, NSA_HPG * Q_BLOCK, -1)
    far = rel_table[REL_BUCKETS - 1].astype(F32)
    lt = toeplitz(SEL_NEAR + SEL_TK, SEL_NEAR + 2 * SEL_TK + Q_BLOCK, f.shape[-1], far[:, None])
    lt = lt.reshape(NSA_GROUPS, NSA_HPG * Q_BLOCK, -1)
    far_hi = far.astype(BF16).astype(F32)
    lane = jnp.arange(LANE)[None, :]
    cq = jnp.where(lane == HEAD_DIM, far_hi[:, None], jnp.where(lane == HEAD_DIM + 1, (far - far_hi)[:, None], 0.0))
    cq = jnp.broadcast_to(cq[:, None, :], (NSA_HEADS, Q_BLOCK, LANE)).reshape(NSA_GROUPS, NSA_HPG * Q_BLOCK, LANE)
    return pc, wb, lt, cq


def _in_proj_perm():
    perm = np.full((IN_PAD,), -1, np.int64)
    for h in range(NSA_HEADS):
        perm[C_Q + h * LANE:C_Q + h * LANE + HEAD_DIM] = np.arange(HEAD_DIM) + h * HEAD_DIM
    rw0 = NSA_COLS
    g0 = NSA_COLS + RWKV_COLS
    perm[C_GN:C_GN + D_MODEL] = g0 + np.arange(D_MODEL)
    perm[C_GR:C_GR + D_MODEL] = g0 + D_MODEL + np.arange(D_MODEL)
    perm[C_R:C_R + RWKV_WIDTH] = rw0 + np.arange(RWKV_WIDTH)
    perm[C_K:C_K + RWKV_WIDTH] = rw0 + RWKV_WIDTH + np.arange(RWKV_WIDTH)
    perm[C_V:C_V + RWKV_WIDTH] = rw0 + 2 * RWKV_WIDTH + np.arange(RWKV_WIDTH)
    for g in range(NSA_GROUPS):
        for br in range(3):
            for h in range(NSA_HPG):
                perm[C_G3 + g * LANE + br * NSA_HPG + h] = NSA_WIDTH + 6 * NSA_KV + (g * NSA_HPG + h) * 3 + br
    perm[C_WA:C_WA + LORA_W + LORA_A] = rw0 + 3 * RWKV_WIDTH + np.arange(LORA_W + LORA_A)
    perm[C_ZG:C_ZG + LORA_G] = rw0 + 3 * RWKV_WIDTH + LORA_W + LORA_A + np.arange(LORA_G)
    return perm


_PERM = _in_proj_perm()


def _permute_in_proj(w):
    pieces = []
    start = 0
    while start < IN_PAD:
        stop = start + 1
        if _PERM[start] < 0:
            while stop < IN_PAD and _PERM[stop] < 0:
                stop += 1
            pieces.append(jnp.zeros((w.shape[0], stop - start), BF16))
        else:
            while stop < IN_PAD and _PERM[stop] == _PERM[stop - 1] + 1:
                stop += 1
            pieces.append(w[:, int(_PERM[start]):int(_PERM[stop - 1]) + 1].astype(BF16))
        start = stop
    return jnp.concatenate(pieces, axis=1)


def _pad_to(a, shape):
    return jnp.pad(a, [(0, s - d) for d, s in zip(a.shape, shape)])


def _route(s, router_bias, n_tok):
    s16 = s[:, :N_EXPERTS]
    s_sel = s16 + router_bias.astype(F32)

    def top2(a):
        i1 = jnp.argmax(a, axis=-1)
        rest = jnp.where(jnp.arange(a.shape[-1]) == i1[..., None], -jnp.inf, a)
        i2 = jnp.argmax(rest, axis=-1)
        return jnp.max(a, axis=-1) + jnp.max(rest, axis=-1), jnp.stack([i1, i2], axis=-1)

    group_score, _ = top2(s_sel.reshape(n_tok, N_EXPERT_GROUPS, EXPERTS_PER_GROUP))
    group = jnp.argmax(group_score, axis=-1)
    in_group = (jnp.arange(N_EXPERTS) // EXPERTS_PER_GROUP)[None, :] == group[:, None]
    _, e_idx = top2(jnp.where(in_group, s_sel, NEG_INF))
    s_top = jnp.take_along_axis(s16, e_idx, axis=-1)
    gate = s_top / jnp.sum(s_top, axis=-1, keepdims=True)
    e_flat = e_idx.reshape(-1)
    onehot = (e_flat[:, None] == jnp.arange(N_EXPERTS)[None, :]).astype(jnp.int32)
    rank = jnp.take_along_axis(jnp.cumsum(onehot, axis=0) - onehot, e_flat[:, None], axis=1)[:, 0]
    counts = jnp.sum(onehot, axis=0)
    padded = (counts + MOE_BLK - 1) // MOE_BLK * MOE_BLK
    ends = jnp.cumsum(padded)
    dest = (ends - padded)[e_flat] + rank
    n_rows = n_tok * TOP_K + N_EXPERTS * MOE_BLK
    tok_of_row = jnp.zeros((n_rows,), jnp.int32).at[dest].set(jnp.arange(n_tok * TOP_K, dtype=jnp.int32) // TOP_K)
    block_expert = jnp.minimum(
        jnp.searchsorted(ends, jnp.arange(n_rows // MOE_BLK) * MOE_BLK, side='right'), N_EXPERTS - 1).astype(jnp.int32)
    return gate, dest.reshape(n_tok, TOP_K), tok_of_row, block_expert


def kernel(x, mem, rel_table, router_w, router_bias, w_in, cmp_pe_k, cmp_pe_v, cmp_w1k, cmp_w2k, cmp_w1v, cmp_w2v, rwkv_mu, rwkv_w0, rwkv_w2, rwkv_a0, rwkv_a2, rwkv_g2, rwkv_kk, rwkv_ka, rwkv_rk, rwkv_gn_g, rwkv_gn_b, rwkv_v0, rwkv_v1, rwkv_v2, p_nsa, p_rwkv, w_out, ln1_g, ln1_b, xq_w, xk_w, xv_w, xo_w, ln2_g, ln2_b, moe_w_gate, moe_w_up, moe_w_down, ln3_g, ln3_b):
    bsz, t, d = x.shape
    n = bsz * t
    depth = w_in.shape[0]
    nc = t // CMP_STRIDE
    nsel = t // SEL_BLOCK
    g, hd = NSA_GROUPS, HEAD_DIM

    pc_tab, wb_tab, lt_tab, cq_tab = _bias_tables(rel_table, t)
    cmp_start = np.arange(nc) * CMP_STRIDE
    sel_start = np.arange(nsel) * SEL_BLOCK
    overlap = ((cmp_start[:, None] <= sel_start[None, :] + SEL_BLOCK - 1)
               & (cmp_start[:, None] + CMP_LEN - 1 >= sel_start[None, :]) & (cmp_start[:, None] < (nc - 1) * CMP_STRIDE))
    overlap = jnp.asarray(overlap.T, BF16)
    lane_head = np.arange(RWKV_WIDTH) // RWKV_HEAD
    hsum = jnp.asarray(lane_head[:, None] == lane_head[None, :], BF16)
    quad_head = np.arange(QUAD_LANES) // RWKV_HEAD
    ones_bd = jnp.asarray(quad_head[:, None] == quad_head[None, :], BF16)
    step = np.arange(SCAN_TT)[None, :, None]
    lane = np.arange(LANE)[None, None, :]
    sbh = np.arange(SCAN_TT // SUB_STEPS * QUAD)[:, None, None]
    place = jnp.asarray((lane // SUB_STEPS == sbh % QUAD) & (step == (sbh // QUAD) * SUB_STEPS + lane % SUB_STEPS), BF16)
    rw_f = _pad_to(router_w.astype(F32), (d, LANE))
    rw_hi = rw_f.astype(BF16)
    rw2 = jnp.stack([rw_hi, (rw_f - rw_hi.astype(F32)).astype(BF16)])

    xf = x.reshape(n, d)
    mem_f = mem.reshape(bsz * mem.shape[1], d)
    v_first = None
    for l in range(depth):
        w_in_l = _permute_in_proj(w_in[l])
        z = _matmul(xf, w_in_l, F32, 1024, 1024)

        kv0 = NSA_WIDTH
        zeros64 = jnp.zeros((d, LANE - hd), BF16)
        w_kv = jnp.stack([jnp.concatenate(
            [piece for i in range(KV_ARRAYS)
             for piece in (w_in[l][:, kv0 + i * NSA_KV + gi * hd:kv0 + i * NSA_KV + (gi + 1) * hd].astype(BF16), zeros64)],
            axis=1) for gi in range(g)])
        kc, vc, ks_aug, vs_aug, kw_pad, vw_aug = _kv_proj(xf, w_kv, bsz, t)

        def cmp_w(pe, w1, w2):
            return (pe.reshape(2, CMP_STRIDE * hd), w1.reshape(2, CMP_STRIDE * hd, CMP_HIDDEN).astype(BF16),
                    _pad_to(w2, (CMP_HIDDEN, LANE)).astype(BF16))

        chunks = lambda a: a.reshape(bsz, g, nc, CMP_STRIDE * hd)
        k_cmp = _compress(chunks(kc), *cmp_w(cmp_pe_k[l], cmp_w1k[l], cmp_w2k[l]))
        v_cmp = _compress(chunks(vc), *cmp_w(cmp_pe_v[l], cmp_w1v[l], cmp_w2v[l]))
        o_c, sel_mask = _cmp_branch(z, k_cmp, v_cmp, pc_tab, overlap, bsz, t)
        o_s = _sel_branch(z, ks_aug, vs_aug, sel_mask, lt_tab, cq_tab, bsz, t)
        o_w = _win_branch(z, kw_pad, vw_aug, wb_tab, bsz, t)

        mu = rwkv_mu[l]
        w3 = 3 * RWKV_WIDTH
        mu5 = jnp.stack([mu[0:512], mu[512:1024], mu[1024:1536], _pad_to(mu[w3:w3 + 128], (512,)),
                         _pad_to(mu[w3 + 128:w3 + 256], (512,))])
        vecs = jnp.stack([rwkv_w0[l], rwkv_a0[l], rwkv_kk[l], rwkv_ka[l], rwkv_rk[l].reshape(-1)])
        w2p = jnp.concatenate([rwkv_w2[l], jnp.zeros_like(rwkv_a2[l])], axis=0).astype(BF16)
        a2p = jnp.concatenate([jnp.zeros_like(rwkv_w2[l]), rwkv_a2[l]], axis=0).astype(BF16)
        vres = None
        if l > 0:
            vres = (v_first, rwkv_v0[l - 1][None, :], _pad_to(rwkv_v1[l - 1], (RWKV_WIDTH, LANE)).astype(BF16),
                    _pad_to(rwkv_v2[l - 1], (LANE, RWKV_WIDTH)).astype(BF16))
        r, w, k, v, kk, b, g_out, bonus, vt = _rwkv_prep(z, mu5, vecs, w2p, a2p, rwkv_g2[l].astype(BF16), hsum, vres,
                                                          bsz, t)
        if l == 0:
            v_first = v
        seq = lambda a: a.reshape(bsz, t, RWKV_WIDTH)
        o_rw = _rwkv_scan(seq(r), seq(w), seq(k), seq(kk), seq(b), vt, ones_bd, place, bsz, t)
        o_rw = o_rw.transpose(1, 0, 2, 3).reshape(n, RWKV_WIDTH)

        xf = _merge(xf, o_c, o_s, o_w, o_rw, bonus, g_out, z, jnp.stack([rwkv_gn_g[l], rwkv_gn_b[l]]), hsum,
                    p_nsa[l].astype(BF16), p_rwkv[l].astype(BF16), w_out[l].astype(BF16),
                    jnp.stack([ln1_g[l], ln1_b[l]]), bsz, t)

        mlen = mem.shape[1]
        mk = _matmul(mem_f, xk_w[l].astype(BF16), BF16, mlen, XATTN_WIDTH)
        mv = _matmul(mem_f, xv_w[l].astype(BF16), BF16, mlen, XATTN_WIDTH)
        kt = mk.reshape(bsz, mlen, XATTN_WIDTH).transpose(0, 2, 1)
        xf, scores = _xattn(xf, kt, mv.reshape(bsz, mlen, XATTN_WIDTH), xq_w[l].astype(BF16), xo_w[l].astype(BF16),
                            jnp.stack([ln2_g[l], ln2_b[l]]), rw2, bsz, t)

        gate, dest, tok_of_row, block_expert = _route(scores, router_bias, n)
        rows = jnp.take(xf.astype(BF16), tok_of_row, axis=0)
        y_rows = _experts(block_expert, rows, moe_w_gate[l], moe_w_up[l], moe_w_down[l])
        xf = _combine(xf, jnp.take(y_rows, dest[:, 0], axis=0), jnp.take(y_rows, dest[:, 1], axis=0), gate,
                      jnp.stack([ln3_g[l], ln3_b[l]]))
    return xf.reshape(bsz, t, d)
```

```python
import functools
import math

import numpy as np
import jax
import jax.numpy as jnp
from jax import lax
from jax.experimental import pallas as pl
from jax.experimental.pallas import tpu as pltpu

F32 = jnp.float32
BF16 = jnp.bfloat16

D_MODEL = 1024
DEPTH = 2
NSA_HEADS = 8
NSA_GROUPS = 2
NSA_HPG = NSA_HEADS // NSA_GROUPS
HEAD_DIM = 64
NSA_WIDTH = NSA_HEADS * HEAD_DIM
NSA_KV = NSA_GROUPS * HEAD_DIM
CMP_STRIDE = 16
CMP_LEN = 2 * CMP_STRIDE
CMP_HIDDEN = 256
SEL_BLOCK = 64
SEL_TOPN = 16
WINDOW = 512
Q_BLOCK = 128
SEL_FORCE = 1e9
RWKV_HEADS = 8
RWKV_HEAD = 64
RWKV_WIDTH = RWKV_HEADS * RWKV_HEAD
LORA_W = 64
LORA_A = 64
LORA_V = 32
LORA_G = 128
GN_EPS = 64e-5
REL_BUCKETS = 32
REL_MAX_DIST = 1024
XATTN_HEADS = 4
XATTN_HEAD = 128
XATTN_WIDTH = XATTN_HEADS * XATTN_HEAD
N_EXPERTS = 16
N_EXPERT_GROUPS = 4
EXPERTS_PER_GROUP = N_EXPERTS // N_EXPERT_GROUPS
TOP_K = 2
EXPERT_FF = 512
DN_ALPHA = (2 * DEPTH) ** 0.25
LN_EPS = 1e-5
NEG_INF = -1e30
NSA_COLS = NSA_WIDTH + 6 * NSA_KV + 3 * NSA_HEADS
RWKV_COLS = 3 * RWKV_WIDTH + LORA_W + LORA_A + LORA_G

LANE = 128
VMEM_LIMIT = 56 * 1024 * 1024

C_Q = 0
C_GN = 1024
C_GR = 2048
C_R = 3072
C_K = 3584
C_V = 4096
C_G3 = 4608
C_WA = 4864
C_ZG = 4992
IN_PAD = 5120
KV_ARRAYS = 6

MOE_BLK = 512
SCAN_TT = 128
FAR_BIAS_DIST = 1280


def _cparams(sem):
    return pltpu.CompilerParams(dimension_semantics=sem, vmem_limit_bytes=VMEM_LIMIT)


def _sigmoid(x):
    return 1.0 / (1.0 + jnp.exp(-x))


def _layer_norm(v, g, b):
    mu = jnp.mean(v, axis=-1, keepdims=True)
    d = v - mu
    var = jnp.mean(d * d, axis=-1, keepdims=True)
    return d * lax.rsqrt(var + LN_EPS) * g + b


def _dot(a, b):
    return jnp.dot(a, b, preferred_element_type=F32)


def _dot_nt(a, b):
    return lax.dot_general(a, b, (((1,), (1,)), ((), ())), preferred_element_type=F32)


def _dot_split(x, w):
    hi = x.astype(BF16)
    lo = (x - hi.astype(F32)).astype(BF16)
    return _dot(hi, w) + _dot(lo, w)


def _mm_kernel(x_ref, w_ref, o_ref, xb_ref):
    @pl.when(pl.program_id(1) == 0)
    def _():
        xb_ref[...] = x_ref[...].astype(BF16)

    o_ref[...] = _dot(xb_ref[...], w_ref[...]).astype(o_ref.dtype)


def _matmul(x, w, out_dtype, tm, tn):
    n, k = x.shape
    m = w.shape[1]
    return pl.pallas_call(
        _mm_kernel,
        grid=(n // tm, m // tn),
        in_specs=[pl.BlockSpec((tm, k), lambda i, j: (i, 0)),
                  pl.BlockSpec((k, tn), lambda i, j: (0, j))],
        out_specs=pl.BlockSpec((tm, tn), lambda i, j: (i, j)),
        out_shape=jax.ShapeDtypeStruct((n, m), out_dtype),
        scratch_shapes=[pltpu.VMEM((tm, k), BF16)],
        compiler_params=_cparams(("parallel", "arbitrary")),
        name="matmul",
    )(x, w)


def _kv_proj_kernel(tiles_per_seq, x_ref, w_ref, kc_ref, vc_ref, ks_ref, vs_ref, kw_ref, vw_ref, xb_ref):
    @pl.when(pl.program_id(1) == 0)
    def _():
        xb_ref[...] = x_ref[...].astype(BF16)

    y = _dot(xb_ref[...], w_ref[0])
    tm = y.shape[0]
    slot = lambda i: y[:, i * LANE:(i + 1) * LANE]
    lane = lax.broadcasted_iota(jnp.int32, (tm, LANE), 1)
    one_lane = jnp.where(lane == HEAD_DIM, 1.0, 0.0)
    t0 = lax.rem(pl.program_id(0), tiles_per_seq) * tm
    block = jnp.right_shift(t0 + lax.broadcasted_iota(jnp.int32, (tm, LANE), 0), int(math.log2(SEL_BLOCK)))
    kc_ref[0, 0] = slot(0)[:, :HEAD_DIM]
    vc_ref[0, 0] = slot(1)[:, :HEAD_DIM]
    ks_ref[0, 0] = jnp.concatenate([slot(2) + jnp.where(lane == HEAD_DIM + 1, 1.0, one_lane),
                                    jnp.where(lane == block, 1.0, 0.0)], axis=1).astype(BF16)
    vs_ref[0, 0] = (slot(3) + one_lane).astype(BF16)
    kw_ref[0, 0] = slot(4).astype(BF16)
    vw_ref[0, 0] = (slot(5) + one_lane).astype(BF16)


def _kv_proj(x, w_kv, bsz, t, tm=512):
    n, d = x.shape
    tps = t // tm
    g = NSA_GROUPS

    def out(width, dtype):
        return (pl.BlockSpec((1, 1, tm, width), lambda i, j: (i // tps, j, i % tps, 0)),
                jax.ShapeDtypeStruct((bsz, g, t, width), dtype))

    outs = [out(HEAD_DIM, F32), out(HEAD_DIM, F32), out(2 * LANE, BF16), out(LANE, BF16), out(LANE, BF16),
            out(LANE, BF16)]
    return pl.pallas_call(
        functools.partial(_kv_proj_kernel, tps),
        grid=(n // tm, g),
        in_specs=[pl.BlockSpec((tm, d), lambda i, j: (i, 0)),
                  pl.BlockSpec((1, d, KV_ARRAYS * LANE), lambda i, j: (j, 0, 0))],
        out_specs=[o[0] for o in outs],
        out_shape=[o[1] for o in outs],
        scratch_shapes=[pltpu.VMEM((tm, d), BF16)],
        compiler_params=_cparams(("parallel", "arbitrary")),
        name="nsa_kv_proj",
    )(x, w_kv)


def _compress_kernel(u_ref, pe_ref, w1_ref, w2_ref, o_ref):
    u = u_ref[0, 0]
    a = _dot((u + pe_ref[0:1, :]).astype(BF16), w1_ref[0])
    b = _dot((u + pe_ref[1:2, :]).astype(BF16), w1_ref[1])
    nc = u.shape[0]
    h = a + pltpu.roll(b, nc - 1, 0)
    h = 0.5 * h * (1.0 + jnp.tanh(math.sqrt(2.0 / math.pi) * (h + 0.044715 * (h * h * h))))
    y = _dot(h.astype(BF16), w2_ref[...])
    row = lax.broadcasted_iota(jnp.int32, y.shape, 0)
    o_ref[0, 0] = jnp.where(row < nc - 1, y, 0.0)


def _compress(u, pe2, w1, w2p):
    b, g, nc, _ = u.shape
    return pl.pallas_call(
        _compress_kernel,
        grid=(b, g),
        in_specs=[pl.BlockSpec((1, 1, nc, 1024), lambda i, j: (i, j, 0, 0)),
                  pl.BlockSpec((2, 1024), lambda i, j: (0, 0)),
                  pl.BlockSpec((2, 1024, CMP_HIDDEN), lambda i, j: (0, 0, 0)),
                  pl.BlockSpec((CMP_HIDDEN, LANE), lambda i, j: (0, 0))],
        out_specs=pl.BlockSpec((1, 1, nc, LANE), lambda i, j: (i, j, 0, 0)),
        out_shape=jax.ShapeDtypeStruct((b, g, nc, LANE), F32),
        compiler_params=_cparams(("parallel", "parallel")),
        name="nsa_compress",
    )(u, pe2, w1, w2p)


def _stack_heads(q_ref, row0=0):
    q = q_ref[row0:row0 + Q_BLOCK, :] * (HEAD_DIM ** -0.5)
    return jnp.concatenate([q[:, h * LANE:(h + 1) * LANE] for h in range(NSA_HPG)], axis=0).astype(BF16)


def _store_heads(o_ref, o, gates, branch, row0=0):
    lane = lax.broadcasted_iota(jnp.int32, (Q_BLOCK, LANE), 1)
    for pair in range(NSA_HPG // 2):
        halves = []
        for h in (2 * pair, 2 * pair + 1):
            c = branch * NSA_HPG + h
            halves.append(o[h * Q_BLOCK:(h + 1) * Q_BLOCK, :] * gates[:, c:c + 1])
        o_ref[0, pair, row0:row0 + Q_BLOCK, :] = jnp.where(lane < HEAD_DIM, halves[0],
                                                           pltpu.roll(halves[1], HEAD_DIM, 1))


CMP_QB = 2


def _cmp_kernel(q_ref, zg_ref, kc_ref, vc_ref, pc0_ref, pc1_ref, ov_ref, o_ref, m_ref, *scratch):
    nc = kc_ref.shape[2]
    nsel = ov_ref.shape[0]
    kc = kc_ref[0, 0].astype(BF16)
    vc = vc_ref[0, 0].astype(BF16)
    gates = _sigmoid(zg_ref[...])
    blocks = [(qb, pl.program_id(2) * CMP_QB + qb, pc_ref, scratch[2 * qb], scratch[2 * qb + 1])
              for qb, pc_ref in enumerate((pc0_ref, pc1_ref))]
    for qb, c, _, p_s, _ in blocks:
        p_s[...] = _dot_nt(_stack_heads(q_ref, qb * Q_BLOCK), kc)
    for qb, c, pc_ref, p_s, pb_s in blocks:
        off = pl.multiple_of(lax.div((nc - 8) - 8 * c, LANE) * LANE, LANE)
        for r in range(NSA_HPG * Q_BLOCK // SEL_ROWS):
            rows = slice(r * SEL_ROWS, (r + 1) * SEL_ROWS)
            s = p_s[rows, :] + pc_ref[0, 0, rows, pl.ds(off, nc)]
            e = jnp.exp(s - jnp.max(s, axis=-1, keepdims=True))
            l = jnp.sum(e, axis=-1, keepdims=True)
            ql = (lax.broadcasted_iota(jnp.int32, (SEL_ROWS, 1), 0) + r * SEL_ROWS) & (Q_BLOCK - 1)
            p = e * jnp.where((c * Q_BLOCK + ql) >= (CMP_LEN - 1), 1.0 / l, 0.0)
            p_s[rows, :] = p
            pb_s[rows, :] = p.astype(BF16)
    vals = []
    jb = lax.broadcasted_iota(jnp.int32, (nsel, Q_BLOCK), 0)
    jbf = jb.astype(F32)
    for qb, c, _, p_s, pb_s in blocks:
        _store_heads(o_ref, _dot(pb_s[...], vc), gates[qb * Q_BLOCK:(qb + 1) * Q_BLOCK, :], 0, qb * Q_BLOCK)
        psum = ((p_s[0:Q_BLOCK, :] + p_s[Q_BLOCK:2 * Q_BLOCK, :])
                + (p_s[2 * Q_BLOCK:3 * Q_BLOCK, :] + p_s[3 * Q_BLOCK:4 * Q_BLOCK, :]))
        hi = psum.astype(BF16)
        lo = (psum - hi.astype(F32)).astype(BF16)
        imp = _dot_nt(ov_ref[...], hi) + _dot_nt(ov_ref[...], lo)
        tq = c * Q_BLOCK + lax.broadcasted_iota(jnp.int32, (nsel, Q_BLOCK), 1)
        jq = jnp.right_shift(tq, int(math.log2(SEL_BLOCK)))
        forced = (jb == 0) | (jb == jq) | (jb == jq - 1)
        vals.append(jnp.where(forced, SEL_FORCE, jnp.where(jb <= jq, imp, -SEL_FORCE)))
    sels = [jnp.zeros((nsel, Q_BLOCK), F32) for _ in blocks]
    for _ in range(SEL_TOPN):
        for qb in range(CMP_QB):
            mx = jnp.max(vals[qb], axis=0, keepdims=True)
            first = jnp.min(jnp.where(vals[qb] == mx, jbf, float(nsel)), axis=0, keepdims=True)
            hit = jbf == first
            sels[qb] = jnp.where(hit, jnp.where(mx >= 0.0, 1.0, 0.0), sels[qb])
            vals[qb] = jnp.where(hit, -3e38, vals[qb])
    for qb in range(CMP_QB):
        m_ref[0, 0, qb * Q_BLOCK:(qb + 1) * Q_BLOCK, :] = (sels[qb].T - 1.0) * (-NEG_INF)


def _cmp_branch(z, k_cmp, v_cmp, pc_tab, overlap, bsz, t):
    tq = CMP_QB * Q_BLOCK
    nq = t // tq
    nc = k_cmp.shape[2]
    nsel = overlap.shape[0]
    rows = NSA_HPG * Q_BLOCK

    def variant(qb):
        return pl.BlockSpec((1, 1, rows, 2 * nc - LANE),
                            lambda b, g, c: (g, lax.rem((nc - 8) - 8 * (CMP_QB * c + qb), LANE) // 8, 0, 0))

    return pl.pallas_call(
        _cmp_kernel,
        grid=(bsz, NSA_GROUPS, nq),
        in_specs=[pl.BlockSpec((tq, NSA_HPG * LANE), lambda b, g, c: (b * nq + c, g)),
                  pl.BlockSpec((tq, LANE), lambda b, g, c: (b * nq + c, C_G3 // LANE + g)),
                  pl.BlockSpec((1, 1, nc, LANE), lambda b, g, c: (b, g, 0, 0)),
                  pl.BlockSpec((1, 1, nc, LANE), lambda b, g, c: (b, g, 0, 0)),
                  variant(0), variant(1),
                  pl.BlockSpec((nsel, nc), lambda b, g, c: (0, 0))],
        out_specs=[pl.BlockSpec((1, NSA_HPG // 2, tq, LANE), lambda b, g, c: (b, g, c, 0)),
                   pl.BlockSpec((1, 1, tq, nsel), lambda b, g, c: (b, g, c, 0))],
        out_shape=[jax.ShapeDtypeStruct((bsz, NSA_HEADS // 2, t, LANE), F32),
                   jax.ShapeDtypeStruct((bsz, NSA_GROUPS, t, nsel), F32)],
        scratch_shapes=[pltpu.VMEM((rows, nc), F32), pltpu.VMEM((rows, nc), BF16)] * CMP_QB,
        compiler_params=_cparams(("parallel", "parallel", "arbitrary")),
        name="nsa_cmp_select",
    )(z, z, k_cmp, v_cmp, pc_tab, pc_tab, overlap)


WIN_QB = 2


def _win_kernel(q_ref, zg_ref, k_ref, v_ref, wb_ref, o_ref, *scratch):
    nvar = wb_ref.shape[1]
    gates = _sigmoid(zg_ref[...])
    blocks = []
    for qb in range(WIN_QB):
        c = pl.program_id(2) * WIN_QB + qb
        start = pl.multiple_of(jnp.maximum(c * Q_BLOCK - WINDOW, 0), Q_BLOCK)
        blocks.append((qb, jnp.minimum(c, nvar - 1), pl.ds(start, WINDOW + Q_BLOCK), scratch[2 * qb], scratch[2 * qb + 1]))
    for qb, _, keys, s_s, _ in blocks:
        s_s[...] = _dot_nt(_stack_heads(q_ref, qb * Q_BLOCK), k_ref[0, 0, keys, :])
    for qb, var, _, s_s, p_s in blocks:
        for r in range(NSA_HPG * Q_BLOCK // SEL_ROWS):
            rows = slice(r * SEL_ROWS, (r + 1) * SEL_ROWS)
            s = s_s[rows, :] + wb_ref[0, var, rows, :]
            p_s[rows, :] = jnp.exp(s - jnp.max(s, axis=-1, keepdims=True)).astype(BF16)
    for qb, _, keys, _, p_s in blocks:
        o = _dot(p_s[...], v_ref[0, 0, keys, :])
        o = o * (1.0 / o[:, HEAD_DIM:HEAD_DIM + 1])
        _store_heads(o_ref, o, gates[qb * Q_BLOCK:(qb + 1) * Q_BLOCK, :], 2, qb * Q_BLOCK)


def _win_branch(z, kw, vw, wb_tab, bsz, t):
    tq = WIN_QB * Q_BLOCK
    nq = t // tq
    rows, width = NSA_HPG * Q_BLOCK, WINDOW + Q_BLOCK
    return pl.pallas_call(
        _win_kernel,
        grid=(bsz, NSA_GROUPS, nq),
        in_specs=[pl.BlockSpec((tq, NSA_HPG * LANE), lambda b, g, c: (b * nq + c, g)),
                  pl.BlockSpec((tq, LANE), lambda b, g, c: (b * nq + c, C_G3 // LANE + g)),
                  pl.BlockSpec((1, 1, t, LANE), lambda b, g, c: (b, g, 0, 0)),
                  pl.BlockSpec((1, 1, t, LANE), lambda b, g, c: (b, g, 0, 0)),
                  pl.BlockSpec((1,) + wb_tab.shape[1:], lambda b, g, c: (g, 0, 0, 0))],
        out_specs=pl.BlockSpec((1, NSA_HPG // 2, tq, LANE), lambda b, g, c: (b, g, c, 0)),
        out_shape=jax.ShapeDtypeStruct((bsz, NSA_HEADS // 2, t, LANE), F32),
        scratch_shapes=[pltpu.VMEM((rows, width), F32), pltpu.VMEM((rows, width), BF16)] * WIN_QB,
        compiler_params=_cparams(("parallel", "parallel", "arbitrary")),
        name="nsa_window",
    )(z, z, kw, vw, wb_tab)


SEL_TK = 512


SEL_ROWS = 64
SEL_NEAR = FAR_BIAS_DIST + Q_BLOCK


def _sel_kernel(q_ref, zg_ref, k_ref, v_ref, mt_ref, lt_ref, cq_ref, o_ref, m_s, acc_s, s0_s, s1_s, p0_s, p1_s,
                a0_s, a1_s):
    c = pl.program_id(2)
    q = q_ref[...] * (HEAD_DIM ** -0.5)
    qs = jnp.concatenate([q[:, h * LANE:(h + 1) * LANE] for h in range(NSA_HPG)], axis=0) + cq_ref[0]
    mt = mt_ref[0, 0]
    q_aug = jnp.concatenate([qs, jnp.concatenate([mt] * NSA_HPG, axis=0)], axis=1).astype(BF16)
    m_s[...] = jnp.full(m_s.shape, NEG_INF, F32)
    acc_s[...] = jnp.zeros(acc_s.shape, F32)
    ncol = SEL_TK // LANE
    n_tiles = lax.div(c, SEL_TK // Q_BLOCK) + 1

    def key_rows(i):
        return pl.ds(pl.multiple_of(jnp.minimum(i, n_tiles - 1) * SEL_TK, SEL_TK), SEL_TK)

    def scores(i, dst):
        dst[...] = _dot_nt(q_aug, k_ref[0, 0, key_rows(i), :])

    def tile(i, cur, nxt, p_buf, a_buf):
        scores(i + 1, nxt)
        off = jnp.where(i < n_tiles, jnp.maximum(SEL_NEAR + SEL_TK - (c * Q_BLOCK - i * SEL_TK), 0),
                        SEL_NEAR + SEL_TK + Q_BLOCK)
        off = pl.multiple_of(off, LANE)
        for r in range(NSA_HPG * Q_BLOCK // SEL_ROWS):
            rows = slice(r * SEL_ROWS, (r + 1) * SEL_ROWS)
            s = cur[rows, :] + lt_ref[0, rows, pl.ds(off, SEL_TK)]
            cols = [s[:, j * LANE:(j + 1) * LANE] for j in range(ncol)]
            mx = functools.reduce(jnp.maximum, cols)
            m_old = m_s[rows, :]
            m_new = jnp.maximum(m_old, jnp.max(mx, axis=-1, keepdims=True))
            m_s[rows, :] = m_new
            a_buf[rows, :] = jnp.exp(m_old - m_new)
            p_buf[rows, :] = jnp.concatenate([jnp.exp(col - m_new) for col in cols], axis=1).astype(BF16)
        acc_s[...] = a_buf[...] * acc_s[...] + _dot(p_buf[...], v_ref[0, 0, key_rows(i), :])

    scores(0, s0_s)

    def pair(j, carry):
        tile(2 * j, s0_s, s1_s, p0_s, a0_s)
        tile(2 * j + 1, s1_s, s0_s, p1_s, a1_s)
        return carry

    lax.fori_loop(0, lax.div(n_tiles + 1, 2), pair, 0)
    acc = acc_s[...]
    o = acc * (1.0 / acc[:, HEAD_DIM:HEAD_DIM + 1])
    _store_heads(o_ref, o, _sigmoid(zg_ref[...]), 1)


def _sel_branch(z, ks_aug, vs, mt, lt_tab, cq_tab, bsz, t):
    nq = t // Q_BLOCK
    nsel = mt.shape[-1]
    rows = NSA_HPG * Q_BLOCK
    return pl.pallas_call(
        _sel_kernel,
        grid=(bsz, NSA_GROUPS, nq),
        in_specs=[pl.BlockSpec((Q_BLOCK, NSA_HPG * LANE), lambda b, g, c: (b * nq + c, g)),
                  pl.BlockSpec((Q_BLOCK, LANE), lambda b, g, c: (b * nq + c, C_G3 // LANE + g)),
                  pl.BlockSpec((1, 1, t, 2 * LANE), lambda b, g, c: (b, g, 0, 0)),
                  pl.BlockSpec((1, 1, t, LANE), lambda b, g, c: (b, g, 0, 0)),
                  pl.BlockSpec((1, 1, Q_BLOCK, nsel), lambda b, g, c: (b, g, c, 0)),
                  pl.BlockSpec((1, rows, lt_tab.shape[2]), lambda b, g, c: (g, 0, 0)),
                  pl.BlockSpec((1, rows, LANE), lambda b, g, c: (g, 0, 0))],
        out_specs=pl.BlockSpec((1, NSA_HPG // 2, Q_BLOCK, LANE), lambda b, g, c: (b, g, c, 0)),
        out_shape=jax.ShapeDtypeStruct((bsz, NSA_HEADS // 2, t, LANE), F32),
        scratch_shapes=[pltpu.VMEM((rows, LANE), F32),
                        pltpu.VMEM((rows, LANE), F32),
                        pltpu.VMEM((rows, SEL_TK), F32),
                        pltpu.VMEM((rows, SEL_TK), F32),
                        pltpu.VMEM((rows, SEL_TK), BF16),
                        pltpu.VMEM((rows, SEL_TK), BF16),
                        pltpu.VMEM((rows, LANE), F32),
                        pltpu.VMEM((rows, LANE), F32)],
        compiler_params=_cparams(("parallel", "parallel", "arbitrary")),
        name="nsa_selected",
    )(z, z, ks_aug, vs, mt, lt_tab, cq_tab)


def _shift_mix(z_ref, prev_ref, mu, first):
    z = z_ref[...]
    prev = jnp.where(first, 0.0, prev_ref[7:8, :])
    row = lax.broadcasted_iota(jnp.int32, z.shape, 0)
    zp = jnp.where(row == 0, prev, pltpu.roll(z, 1, 0))
    return z + mu * (zp - z)


def _rwkv_prep_kernel(has_vres, tiles_per_seq, *refs):
    (zr, zk, zv, zwa, zg, pr, pk, pv, pwa, pg, mu_ref, vec_ref, w2_ref, a2_ref, g2_ref, hsum_ref) = refs[:16]
    pos = 16
    if has_vres:
        vf_ref, v0_ref, v1_ref, v2_ref = refs[pos:pos + 4]
        pos += 4
    r_o, w_o, k_o, v_o, kk_o, b_o, g_o, bonus_o, vt_o = refs[pos:]
    first = lax.rem(pl.program_id(0), tiles_per_seq) == 0
    r = _shift_mix(zr, pr, mu_ref[0:1, :], first)
    k = _shift_mix(zk, pk, mu_ref[1:2, :], first)
    v = _shift_mix(zv, pv, mu_ref[2:3, :], first)
    wa = _shift_mix(zwa, pwa, mu_ref[3:4, 0:LANE], first)
    zg_s = _shift_mix(zg, pg, mu_ref[4:5, 0:LANE], first)
    w0, a0, k_k, k_a, r_k = (vec_ref[i:i + 1, :] for i in range(5))
    if has_vres:
        lora = _dot(_dot(v.astype(BF16), v1_ref[...]).astype(BF16), v2_ref[...])
        v = v + (vf_ref[...] - v) * _sigmoid(v0_ref[...] + lora)
    u = w0 + _dot(jnp.tanh(wa).astype(BF16), w2_ref[...])
    decay = jnp.exp(-math.exp(-0.5) * _sigmoid(u))
    a = _sigmoid(a0 + _dot(wa.astype(BF16), a2_ref[...]))
    g = _dot(_sigmoid(zg_s).astype(BF16), g2_ref[...])
    kk = k * k_k
    kk = kk / jnp.maximum(jnp.sqrt(_dot_split(kk * kk, hsum_ref[...])), 1e-12)
    k = k * (1.0 + (a - 1.0) * k_a)
    r_o[...] = r
    w_o[...] = decay
    k_o[...] = k
    v_o[...] = v
    kk_o[...] = kk
    b_o[...] = kk * a
    g_o[...] = g
    bonus_o[...] = _dot_split(r * k * r_k, hsum_ref[...]) * v
    vt_o[0] = v.T.astype(BF16)


def _rwkv_prep(z, mu5, vecs, w2p, a2p, g2, hsum, vres, bsz, t, tm=256):
    n = bsz * t
    nt = n // tm
    tps = t // tm
    w = RWKV_WIDTH

    def cur(width, col):
        return pl.BlockSpec((tm, width), lambda i: (i, col // width))

    def prev(width, col):
        return pl.BlockSpec((8, width), lambda i: (jnp.maximum(i * (tm // 8) - 1, 0), col // width))

    def full(shape):
        return pl.BlockSpec(shape, lambda i: (0,) * len(shape))

    cols = [(w, C_R), (w, C_K), (w, C_V), (LANE, C_WA), (LANE, C_ZG)]
    in_specs = [cur(*c) for c in cols] + [prev(*c) for c in cols]
    in_specs += [full(mu5.shape), full(vecs.shape), full(w2p.shape), full(a2p.shape), full(g2.shape), full(hsum.shape)]
    args = [z] * 10 + [mu5, vecs, w2p, a2p, g2, hsum]
    if vres is not None:
        v_first, v0, v1p, v2p = vres
        in_specs += [pl.BlockSpec((tm, w), lambda i: (i, 0)), full(v0.shape), full(v1p.shape), full(v2p.shape)]
        args += [v_first, v0, v1p, v2p]
    tok = pl.BlockSpec((tm, w), lambda i: (i, 0))
    out_specs = [tok] * 8 + [pl.BlockSpec((1, w, tm), lambda i: (i // tps, 0, i % tps))]
    out_shape = [jax.ShapeDtypeStruct((n, w), F32)] * 8 + [jax.ShapeDtypeStruct((bsz, w, t), BF16)]
    return pl.pallas_call(
        functools.partial(_rwkv_prep_kernel, vres is not None, tps),
        grid=(nt,),
        in_specs=in_specs,
        out_specs=out_specs,
        out_shape=out_shape,
        compiler_params=_cparams(("parallel",)),
        name="rwkv_prep",
    )(*args)


QUAD = 4
QUAD_LANES = QUAD * RWKV_HEAD
SUB_STEPS = LANE // QUAD


def _scan_kernel(r_ref, w_ref, k_ref, kk_ref, b_ref, vt_ref, ones_ref, place_ref, o_ref, s_ref, vq_ref, kq_ref, vk_ref):
    bsz = r_ref.shape[0]
    tt = r_ref.shape[1]
    nh = RWKV_HEAD
    quads = [(bi, hf) for bi in range(bsz) for hf in range(RWKV_HEADS // QUAD)]

    @pl.when(pl.program_id(0) == 0)
    def _():
        s_ref[...] = jnp.zeros(s_ref.shape, F32)

    sub_shift, head_shift = int(math.log2(SUB_STEPS)), int(math.log2(nh))
    sub_lane = lax.broadcasted_iota(jnp.int32, (nh, LANE), 1) & (SUB_STEPS - 1)
    k_row_head = jnp.right_shift(lax.broadcasted_iota(jnp.int32, (LANE, QUAD_LANES), 0), sub_shift)
    k_lane_head = jnp.right_shift(lax.broadcasted_iota(jnp.int32, (LANE, QUAD_LANES), 1), head_shift)
    r_mask = (lax.broadcasted_iota(jnp.int32, (16, QUAD_LANES), 0)
              == jnp.right_shift(lax.broadcasted_iota(jnp.int32, (16, QUAD_LANES), 1), head_shift))
    ones_bd = ones_ref[...]

    def sub_block(sb, carry):
        s0 = pl.multiple_of(sb * SUB_STEPS, SUB_STEPS)
        for q, (bi, hf) in enumerate(quads):
            ln = slice(hf * QUAD_LANES, (hf + 1) * QUAD_LANES)
            vq = _dot(vt_ref[bi, (QUAD * hf) * nh:(QUAD * hf + 1) * nh, :], place_ref[sb * QUAD])
            for h in range(1, QUAD):
                vq = vq + _dot(vt_ref[bi, (QUAD * hf + h) * nh:(QUAD * hf + h + 1) * nh, :], place_ref[sb * QUAD + h])
            vq_ref[q] = vq.astype(BF16)
            k_sub = k_ref[bi, pl.ds(s0, SUB_STEPS), ln]
            kq_ref[q] = jnp.where(k_row_head == k_lane_head, jnp.concatenate([k_sub] * QUAD, axis=0), 0.0).astype(BF16)

        def group(tg, carry):
            t0 = pl.multiple_of(s0 + tg * 8, 8)
            rows = []
            for bi, hf in quads:
                ln = slice(hf * QUAD_LANES, (hf + 1) * QUAD_LANES)
                rows.append(tuple(ref[bi, pl.ds(t0, 8), ln] for ref in (w_ref, kk_ref, b_ref, r_ref)))
            onehots = [jnp.where(sub_lane == tg * 8 + u, 1.0, 0.0).astype(BF16) for u in range(8)]
            for q in range(len(quads)):
                vq = vq_ref[q]
                vk_ref[q] = _dot(jnp.concatenate([vq * oh for oh in onehots], axis=0), kq_ref[q])
            state = [s_ref[q] for q in range(len(quads))]
            for u in range(8):
                sk = _dot(jnp.concatenate([(state[q] * rows[q][1][u:u + 1, :]).astype(BF16)
                                           for q in range(len(quads))], axis=0), ones_bd)
                for q, (bi, hf) in enumerate(quads):
                    w8, _, b8, r8 = rows[q]
                    state[q] = (state[q] * w8[u:u + 1, :] - sk[q * nh:(q + 1) * nh, :] * b8[u:u + 1, :]
                                + vk_ref[q, u * nh:(u + 1) * nh, :])
                    r_lhs = jnp.where(r_mask, r8[u:u + 1, :], 0.0).astype(BF16)
                    o = _dot_nt(r_lhs, state[q].astype(BF16))
                    o_ref[t0 + u, bi, QUAD * hf:QUAD * (hf + 1), :] = o[0:QUAD, :]
            for q in range(len(quads)):
                s_ref[q] = state[q]
            return carry

        return lax.fori_loop(0, SUB_STEPS // 8, group, carry)

    lax.fori_loop(0, tt // SUB_STEPS, sub_block, 0)


def _rwkv_scan(r, w, k, kk, b, vt, ones_bd, place, bsz, t):
    tt = SCAN_TT
    wd = RWKV_WIDTH
    nquad = bsz * RWKV_HEADS // QUAD
    tok = pl.BlockSpec((bsz, tt, wd), lambda i: (0, i, 0))
    return pl.pallas_call(
        _scan_kernel,
        grid=(t // tt,),
        in_specs=[tok, tok, tok, tok, tok,
                  pl.BlockSpec((bsz, wd, tt), lambda i: (0, 0, i)),
                  pl.BlockSpec(ones_bd.shape, lambda i: (0, 0)),
                  pl.BlockSpec(place.shape, lambda i: (0, 0, 0))],
        out_specs=pl.BlockSpec((tt, bsz, RWKV_HEADS, RWKV_HEAD), lambda i: (i, 0, 0, 0)),
        out_shape=jax.ShapeDtypeStruct((t, bsz, RWKV_HEADS, RWKV_HEAD), F32),
        scratch_shapes=[pltpu.VMEM((nquad, RWKV_HEAD, QUAD_LANES), F32),
                        pltpu.VMEM((nquad, RWKV_HEAD, LANE), BF16),
                        pltpu.VMEM((nquad, LANE, QUAD_LANES), BF16),
                        pltpu.VMEM((nquad, 8 * RWKV_HEAD, QUAD_LANES), F32)],
        compiler_params=_cparams(("arbitrary",)),
        name="rwkv_scan",
    )(r, w, k, kk, b, vt, ones_bd, place)


def _merge_kernel(x_ref, oc_ref, os_ref, ow_ref, orw_ref, bonus_ref, g_ref, zgn_ref, zgr_ref, gn_ref, hsum_ref,
                  pn_ref, pr_ref, wo_ref, ln_ref, o_ref):
    o_n = jnp.concatenate([oc_ref[0, p] + os_ref[0, p] + ow_ref[0, p] for p in range(NSA_HEADS // 2)], axis=1)
    y_n = _dot(o_n.astype(BF16), pn_ref[...])
    o = orw_ref[...]
    mu = _dot_split(o, hsum_ref[...]) * (1.0 / RWKV_HEAD)
    d = o - mu
    var = _dot_split(d * d, hsum_ref[...]) * (1.0 / RWKV_HEAD)
    o = d * lax.rsqrt(var + GN_EPS) * gn_ref[0:1, :] + gn_ref[1:2, :]
    o = ((o + bonus_ref[...]) * g_ref[...]).astype(BF16)
    y1 = _sigmoid(zgn_ref[...]) * y_n + _sigmoid(zgr_ref[...]) * _dot(o, pr_ref[...])
    y = _dot(y1.astype(BF16), wo_ref[...])
    o_ref[...] = _layer_norm(DN_ALPHA * x_ref[...] + y, ln_ref[0:1, :], ln_ref[1:2, :])


def _merge(x, o_c, o_s, o_w, o_rw, bonus, g, z, gn, hsum, p_nsa, p_rwkv, w_out, ln, bsz, t, tm=256):
    n = bsz * t
    tps = t // tm
    d = D_MODEL
    w = RWKV_WIDTH
    head = pl.BlockSpec((1, NSA_HEADS // 2, tm, LANE), lambda i: (i // tps, 0, i % tps, 0))
    tok = pl.BlockSpec((tm, w), lambda i: (i, 0))

    def full(a):
        return pl.BlockSpec(a.shape, lambda i: (0,) * a.ndim)

    return pl.pallas_call(
        _merge_kernel,
        grid=(n // tm,),
        in_specs=[pl.BlockSpec((tm, d), lambda i: (i, 0)), head, head, head, tok, tok, tok,
                  pl.BlockSpec((tm, d), lambda i: (i, C_GN // d)),
                  pl.BlockSpec((tm, d), lambda i: (i, C_GR // d)),
                  full(gn), full(hsum), full(p_nsa), full(p_rwkv), full(w_out), full(ln)],
        out_specs=pl.BlockSpec((tm, d), lambda i: (i, 0)),
        out_shape=jax.ShapeDtypeStruct((n, d), F32),
        compiler_params=_cparams(("parallel",)),
        name="mix_merge_ln",
    )(x, o_c, o_s, o_w, o_rw, bonus, g, z, z, gn, hsum, p_nsa, p_rwkv, w_out, ln)


def _xattn_kernel(x_ref, kt_ref, v_ref, wq_ref, wo_ref, ln_ref, rw_ref, o_ref, s_ref):
    x = x_ref[...]
    q = _dot(x.astype(BF16), wq_ref[...]).astype(BF16)
    outs = []
    for h in range(XATTN_HEADS):
        hs = slice(h * XATTN_HEAD, (h + 1) * XATTN_HEAD)
        s = _dot(q[:, hs], kt_ref[0, hs, :]) * (XATTN_HEAD ** -0.5)
        m = jnp.max(s, axis=-1, keepdims=True)
        p = jnp.exp(s - m)
        l = jnp.sum(p, axis=-1, keepdims=True)
        outs.append(_dot(p.astype(BF16), v_ref[0, :, hs]) * (1.0 / l))
    o = jnp.concatenate(outs, axis=1).astype(BF16)
    x2 = _layer_norm(DN_ALPHA * x + _dot(o, wo_ref[...]), ln_ref[0:1, :], ln_ref[1:2, :])
    o_ref[...] = x2
    hi = x2.astype(BF16)
    lo = (x2 - hi.astype(F32)).astype(BF16)
    logits = _dot(hi, rw_ref[0]) + (_dot(lo, rw_ref[0]) + _dot(hi, rw_ref[1]))
    s_ref[...] = _sigmoid(logits)


def _xattn(x, kt, v, wq, wo, ln, rw2, bsz, t, tm=256):
    n = bsz * t
    tps = t // tm
    d = D_MODEL

    def full(a):
        return pl.BlockSpec(a.shape, lambda i: (0,) * a.ndim)

    return pl.pallas_call(
        _xattn_kernel,
        grid=(n // tm,),
        in_specs=[pl.BlockSpec((tm, d), lambda i: (i, 0)),
                  pl.BlockSpec((1,) + kt.shape[1:], lambda i: (i // tps, 0, 0)),
                  pl.BlockSpec((1,) + v.shape[1:], lambda i: (i // tps, 0, 0)),
                  full(wq), full(wo), full(ln), full(rw2)],
        out_specs=[pl.BlockSpec((tm, d), lambda i: (i, 0)), pl.BlockSpec((tm, LANE), lambda i: (i, 0))],
        out_shape=[jax.ShapeDtypeStruct((n, d), F32), jax.ShapeDtypeStruct((n, LANE), F32)],
        compiler_params=_cparams(("parallel",)),
        name="xattn_ln_router",
    )(x, kt, v, wq, wo, ln, rw2)


def _expert_kernel(be_ref, x_ref, wg_ref, wu_ref, wd_ref, o_ref, wg_s, wu_s, wd_s):
    i = pl.program_id(0)

    @pl.when((i == 0) | (be_ref[i] != be_ref[jnp.maximum(i - 1, 0)]))
    def _():
        wg_s[...] = wg_ref[0].astype(BF16)
        wu_s[...] = wu_ref[0].astype(BF16)
        wd_s[...] = wd_ref[0].astype(BF16)

    n_used = be_ref[pl.num_programs(0)]

    @pl.when(i < n_used)
    def _():
        x = x_ref[...]
        hg = _dot(x, wg_s[...])
        h = hg * _sigmoid(hg) * _dot(x, wu_s[...])
        o_ref[...] = _dot(h.astype(BF16), wd_s[...]).astype(o_ref.dtype)

    @pl.when(i >= n_used)
    def _():
        o_ref[...] = jnp.zeros(o_ref.shape, o_ref.dtype)


def _experts(block_expert, rows, w_gate, w_up, w_down):
    n_rows, d = rows.shape
    ff = w_gate.shape[2]
    return pl.pallas_call(
        _expert_kernel,
        grid_spec=pltpu.PrefetchScalarGridSpec(
            num_scalar_prefetch=1,
            grid=(n_rows // MOE_BLK,),
            in_specs=[pl.BlockSpec((MOE_BLK, d), lambda i, be: (i, 0)),
                      pl.BlockSpec((1, d, ff), lambda i, be: (be[i], 0, 0)),
                      pl.BlockSpec((1, d, ff), lambda i, be: (be[i], 0, 0)),
                      pl.BlockSpec((1, ff, d), lambda i, be: (be[i], 0, 0))],
            out_specs=pl.BlockSpec((MOE_BLK, d), lambda i, be: (i, 0)),
            scratch_shapes=[pltpu.VMEM((d, ff), BF16), pltpu.VMEM((d, ff), BF16), pltpu.VMEM((ff, d), BF16)]),
        out_shape=jax.ShapeDtypeStruct((n_rows, d), BF16),
        compiler_params=_cparams(("arbitrary",)),
        name="moe_experts",
    )(block_expert, rows, w_gate, w_up, w_down)


def _combine_kernel(x_ref, ya_ref, yb_ref, gate_ref, ln_ref, o_ref):
    y = gate_ref[:, 0:1] * ya_ref[...] + gate_ref[:, 1:2] * yb_ref[...]
    o_ref[...] = _layer_norm(DN_ALPHA * x_ref[...] + y, ln_ref[0:1, :], ln_ref[1:2, :])


def _combine(x, ya, yb, gate, ln, tm=512):
    n, d = x.shape
    tok = pl.BlockSpec((tm, d), lambda i: (i, 0))
    return pl.pallas_call(
        _combine_kernel,
        grid=(n // tm,),
        in_specs=[tok, tok, tok, pl.BlockSpec((tm, TOP_K), lambda i: (i, 0)),
                  pl.BlockSpec(ln.shape, lambda i: (0, 0))],
        out_specs=tok,
        out_shape=jax.ShapeDtypeStruct((n, d), F32),
        compiler_params=_cparams(("parallel",)),
        name="moe_combine_ln",
    )(x, ya, yb, gate, ln)


def _t5_bucket(dist):
    n = jnp.maximum(dist, 0)
    max_exact = REL_BUCKETS // 2
    log_ratio = jnp.log(jnp.maximum(n, 1).astype(F32) / max_exact) / math.log(REL_MAX_DIST / max_exact)
    large = jnp.minimum(max_exact + (log_ratio * (REL_BUCKETS - max_exact)).astype(jnp.int32), REL_BUCKETS - 1)
    return jnp.where(n < max_exact, n, large)


def _bias_tables(rel_table, t):
    nc = t // CMP_STRIDE
    f = rel_table[_t5_bucket(jnp.arange(t + 2048))].astype(F32).T
    ql = jnp.arange(Q_BLOCK)[:, None]

    def lookup(dist, ok):
        vals = jnp.take(f, jnp.clip(dist, 0, f.shape[-1] - 1), axis=1)
        vals = jnp.where(ok[None], vals, NEG_INF)
        return vals.reshape(NSA_GROUPS, NSA_HPG * dist.shape[0], dist.shape[1])

    j = jnp.arange(2 * nc)[None, :]
    dist_c = CMP_STRIDE * (nc - 8 - j) + ql - (CMP_LEN - 1)
    pc = lookup(dist_c, dist_c >= 0)
    pc = jnp.stack([pc[:, :, s:s + 2 * nc - LANE] for s in range(0, LANE, 8)], axis=1)
    nvar = WINDOW // Q_BLOCK + 1

    def toeplitz(base, width, hi, shift):
        period = width + Q_BLOCK
        m = np.arange(period)
        dist = base - np.where(m < width, m, m - period)
        row = jnp.take(f, np.clip(dist, 0, f.shape[-1] - 1), axis=1) - shift
        row = jnp.where(((dist >= 0) & (dist < hi))[None], row, NEG_INF)
        flat = jnp.tile(row, (1, Q_BLOCK))[:, :Q_BLOCK * (period - 1)]
        return flat.reshape(NSA_HEADS, Q_BLOCK, period - 1)[:, :, :width]

    wb = jnp.stack([toeplitz(Q_BLOCK * v, WINDOW + Q_BLOCK, WINDOW, 0.0) for v in range(nvar)], axis=1)
    wb = wb.reshape(NSA_GROUPS, NSA_HPG, nvar, Q_BLOCK, -1).transpose(0, 2, 1, 3, 4)
    wb = wb.reshape(NSA_GROUPS, nvar, NSA_HPG * Q_BLOCK, -1)
    far = rel_table[REL_BUCKETS - 1].astype(F32)
    lt = toeplitz(SEL_NEAR + SEL_TK, SEL_NEAR + 2 * SEL_TK + Q_BLOCK, f.shape[-1], far[:, None])
    lt = lt.reshape(NSA_GROUPS, NSA_HPG * Q_BLOCK, -1)
    far_hi = far.astype(BF16).astype(F32)
    lane = jnp.arange(LANE)[None, :]
    cq = jnp.where(lane == HEAD_DIM, far_hi[:, None], jnp.where(lane == HEAD_DIM + 1, (far - far_hi)[:, None], 0.0))
    cq = jnp.broadcast_to(cq[:, None, :], (NSA_HEADS, Q_BLOCK, LANE)).reshape(NSA_GROUPS, NSA_HPG * Q_BLOCK, LANE)
    return pc, wb, lt, cq


def _in_proj_perm():
    perm = np.full((IN_PAD,), -1, np.int64)
    for h in range(NSA_HEADS):
        perm[C_Q + h * LANE:C_Q + h * LANE + HEAD_DIM] = np.arange(HEAD_DIM) + h * HEAD_DIM
    rw0 = NSA_COLS
    g0 = NSA_COLS + RWKV_COLS
    perm[C_GN:C_GN + D_MODEL] = g0 + np.arange(D_MODEL)
    perm[C_GR:C_GR + D_MODEL] = g0 + D_MODEL + np.arange(D_MODEL)
    perm[C_R:C_R + RWKV_WIDTH] = rw0 + np.arange(RWKV_WIDTH)
    perm[C_K:C_K + RWKV_WIDTH] = rw0 + RWKV_WIDTH + np.arange(RWKV_WIDTH)
    perm[C_V:C_V + RWKV_WIDTH] = rw0 + 2 * RWKV_WIDTH + np.arange(RWKV_WIDTH)
    for g in range(NSA_GROUPS):
        for br in range(3):
            for h in range(NSA_HPG):
                perm[C_G3 + g * LANE + br * NSA_HPG + h] = NSA_WIDTH + 6 * NSA_KV + (g * NSA_HPG + h) * 3 + br
    perm[C_WA:C_WA + LORA_W + LORA_A] = rw0 + 3 * RWKV_WIDTH + np.arange(LORA_W + LORA_A)
    perm[C_ZG:C_ZG + LORA_G] = rw0 + 3 * RWKV_WIDTH + LORA_W + LORA_A + np.arange(LORA_G)
    return perm


_PERM = _in_proj_perm()


def _permute_in_proj(w):
    pieces = []
    start = 0
    while start < IN_PAD:
        stop = start + 1
        if _PERM[start] < 0:
            while stop < IN_PAD and _PERM[stop] < 0:
                stop += 1
            pieces.append(jnp.zeros((w.shape[0], stop - start), BF16))
        else:
            while stop < IN_PAD and _PERM[stop] == _PERM[stop - 1] + 1:
                stop += 1
            pieces.append(w[:, int(_PERM[start]):int(_PERM[stop - 1]) + 1].astype(BF16))
        start = stop
    return jnp.concatenate(pieces, axis=1)


def _pad_to(a, shape):
    return jnp.pad(a, [(0, s - d) for d, s in zip(a.shape, shape)])


def _route(s, router_bias, n_tok):
    s16 = s[:, :N_EXPERTS]
    s_sel = s16 + router_bias.astype(F32)

    def top2(a):
        i1 = jnp.argmax(a, axis=-1)
        rest = jnp.where(jnp.arange(a.shape[-1]) == i1[..., None], -jnp.inf, a)
        i2 = jnp.argmax(rest, axis=-1)
        return jnp.max(a, axis=-1) + jnp.max(rest, axis=-1), jnp.stack([i1, i2], axis=-1)

    group_score, _ = top2(s_sel.reshape(n_tok, N_EXPERT_GROUPS, EXPERTS_PER_GROUP))
    group = jnp.argmax(group_score, axis=-1)
    in_group = (jnp.arange(N_EXPERTS) // EXPERTS_PER_GROUP)[None, :] == group[:, None]
    _, e_idx = top2(jnp.where(in_group, s_sel, NEG_INF))
    s_top = jnp.take_along_axis(s16, e_idx, axis=-1)
    gate = s_top / jnp.sum(s_top, axis=-1, keepdims=True)
    e_flat = e_idx.reshape(-1)
    onehot = (e_flat[:, None] == jnp.arange(N_EXPERTS)[None, :]).astype(jnp.int32)
    rank = jnp.take_along_axis(jnp.cumsum(onehot, axis=0) - onehot, e_flat[:, None], axis=1)[:, 0]
    counts = jnp.sum(onehot, axis=0)
    padded = (counts + MOE_BLK - 1) // MOE_BLK * MOE_BLK
    ends = jnp.cumsum(padded)
    dest = (ends - padded)[e_flat] + rank
    n_rows = n_tok * TOP_K + N_EXPERTS * MOE_BLK
    tok_of_row = jnp.zeros((n_rows,), jnp.int32).at[dest].set(jnp.arange(n_tok * TOP_K, dtype=jnp.int32) // TOP_K)
    block_expert = jnp.minimum(
        jnp.searchsorted(ends, jnp.arange(n_rows // MOE_BLK) * MOE_BLK, side='right'), N_EXPERTS - 1).astype(jnp.int32)
    block_expert = jnp.concatenate([block_expert, (ends[-1:] // MOE_BLK).astype(jnp.int32)])
    return gate, dest.reshape(n_tok, TOP_K), tok_of_row, block_expert


def kernel(x, mem, rel_table, router_w, router_bias, w_in, cmp_pe_k, cmp_pe_v, cmp_w1k, cmp_w2k, cmp_w1v, cmp_w2v, rwkv_mu, rwkv_w0, rwkv_w2, rwkv_a0, rwkv_a2, rwkv_g2, rwkv_kk, rwkv_ka, rwkv_rk, rwkv_gn_g, rwkv_gn_b, rwkv_v0, rwkv_v1, rwkv_v2, p_nsa, p_rwkv, w_out, ln1_g, ln1_b, xq_w, xk_w, xv_w, xo_w, ln2_g, ln2_b, moe_w_gate, moe_w_up, moe_w_down, ln3_g, ln3_b):
    bsz, t, d = x.shape
    n = bsz * t
    depth = w_in.shape[0]
    nc = t // CMP_STRIDE
    nsel = t // SEL_BLOCK
    g, hd = NSA_GROUPS, HEAD_DIM

    pc_tab, wb_tab, lt_tab, cq_tab = _bias_tables(rel_table, t)
    cmp_start = np.arange(nc) * CMP_STRIDE
    sel_start = np.arange(nsel) * SEL_BLOCK
    overlap = ((cmp_start[:, None] <= sel_start[None, :] + SEL_BLOCK - 1)
               & (cmp_start[:, None] + CMP_LEN - 1 >= sel_start[None, :]) & (cmp_start[:, None] < (nc - 1) * CMP_STRIDE))
    overlap = jnp.asarray(overlap.T, BF16)
    lane_head = np.arange(RWKV_WIDTH) // RWKV_HEAD
    hsum = jnp.asarray(lane_head[:, None] == lane_head[None, :], BF16)
    quad_head = np.arange(QUAD_LANES) // RWKV_HEAD
    ones_bd = jnp.asarray(quad_head[:, None] == quad_head[None, :], BF16)
    step = np.arange(SCAN_TT)[None, :, None]
    lane = np.arange(LANE)[None, None, :]
    sbh = np.arange(SCAN_TT // SUB_STEPS * QUAD)[:, None, None]
    place = jnp.asarray((lane // SUB_STEPS == sbh % QUAD) & (step == (sbh // QUAD) * SUB_STEPS + lane % SUB_STEPS), BF16)
    rw_f = _pad_to(router_w.astype(F32), (d, LANE))
    rw_hi = rw_f.astype(BF16)
    rw2 = jnp.stack([rw_hi, (rw_f - rw_hi.astype(F32)).astype(BF16)])

    xf = x.reshape(n, d)
    mem_f = mem.reshape(bsz * mem.shape[1], d)
    v_first = None
    for l in range(depth):
        w_in_l = _permute_in_proj(w_in[l])
        z = _matmul(xf, w_in_l, F32, 1024, 1024)

        kv0 = NSA_WIDTH
        zeros64 = jnp.zeros((d, LANE - hd), BF16)
        w_kv = jnp.stack([jnp.concatenate(
            [piece for i in range(KV_ARRAYS)
             for piece in (w_in[l][:, kv0 + i * NSA_KV + gi * hd:kv0 + i * NSA_KV + (gi + 1) * hd].astype(BF16), zeros64)],
            axis=1) for gi in range(g)])
        kc, vc, ks_aug, vs_aug, kw_pad, vw_aug = _kv_proj(xf, w_kv, bsz, t)

        def cmp_w(pe, w1, w2):
            return (pe.reshape(2, CMP_STRIDE * hd), w1.reshape(2, CMP_STRIDE * hd, CMP_HIDDEN).astype(BF16),
                    _pad_to(w2, (CMP_HIDDEN, LANE)).astype(BF16))

        chunks = lambda a: a.reshape(bsz, g, nc, CMP_STRIDE * hd)
        k_cmp = _compress(chunks(kc), *cmp_w(cmp_pe_k[l], cmp_w1k[l], cmp_w2k[l]))
        v_cmp = _compress(chunks(vc), *cmp_w(cmp_pe_v[l], cmp_w1v[l], cmp_w2v[l]))
        o_c, sel_mask = _cmp_branch(z, k_cmp, v_cmp, pc_tab, overlap, bsz, t)
        o_s = _sel_branch(z, ks_aug, vs_aug, sel_mask, lt_tab, cq_tab, bsz, t)
        o_w = _win_branch(z, kw_pad, vw_aug, wb_tab, bsz, t)

        mu = rwkv_mu[l]
        w3 = 3 * RWKV_WIDTH
        mu5 = jnp.stack([mu[0:512], mu[512:1024], mu[1024:1536], _pad_to(mu[w3:w3 + 128], (512,)),
                         _pad_to(mu[w3 + 128:w3 + 256], (512,))])
        vecs = jnp.stack([rwkv_w0[l], rwkv_a0[l], rwkv_kk[l], rwkv_ka[l], rwkv_rk[l].reshape(-1)])
        w2p = jnp.concatenate([rwkv_w2[l], jnp.zeros_like(rwkv_a2[l])], axis=0).astype(BF16)
        a2p = jnp.concatenate([jnp.zeros_like(rwkv_w2[l]), rwkv_a2[l]], axis=0).astype(BF16)
        vres = None
        if l > 0:
            vres = (v_first, rwkv_v0[l - 1][None, :], _pad_to(rwkv_v1[l - 1], (RWKV_WIDTH, LANE)).astype(BF16),
                    _pad_to(rwkv_v2[l - 1], (LANE, RWKV_WIDTH)).astype(BF16))
        r, w, k, v, kk, b, g_out, bonus, vt = _rwkv_prep(z, mu5, vecs, w2p, a2p, rwkv_g2[l].astype(BF16), hsum, vres,
                                                          bsz, t)
        if l == 0:
            v_first = v
        seq = lambda a: a.reshape(bsz, t, RWKV_WIDTH)
        o_rw = _rwkv_scan(seq(r), seq(w), seq(k), seq(kk), seq(b), vt, ones_bd, place, bsz, t)
        o_rw = o_rw.transpose(1, 0, 2, 3).reshape(n, RWKV_WIDTH)

        xf = _merge(xf, o_c, o_s, o_w, o_rw, bonus, g_out, z, jnp.stack([rwkv_gn_g[l], rwkv_gn_b[l]]), hsum,
                    p_nsa[l].astype(BF16), p_rwkv[l].astype(BF16), w_out[l].astype(BF16),
                    jnp.stack([ln1_g[l], ln1_b[l]]), bsz, t)

        mlen = mem.shape[1]
        mk = _matmul(mem_f, xk_w[l].astype(BF16), BF16, mlen, XATTN_WIDTH)
        mv = _matmul(mem_f, xv_w[l].astype(BF16), BF16, mlen, XATTN_WIDTH)
        kt = mk.reshape(bsz, mlen, XATTN_WIDTH).transpose(0, 2, 1)
        xf, scores = _xattn(xf, kt, mv.reshape(bsz, mlen, XATTN_WIDTH), xq_w[l].astype(BF16), xo_w[l].astype(BF16),
                            jnp.stack([ln2_g[l], ln2_b[l]]), rw2, bsz, t)

        gate, dest, tok_of_row, block_expert = _route(scores, router_bias, n)
        rows = jnp.take(xf.astype(BF16), tok_of_row, axis=0)
        y_rows = _experts(block_expert, rows, moe_w_gate[l], moe_w_up[l], moe_w_down[l])
        xf = _combine(xf, jnp.take(y_rows, dest[:, 0], axis=0), jnp.take(y_rows, dest[:, 1], axis=0), gate,
                      jnp.stack([ln3_g[l], ln3_b[l]]))
    return xf.reshape(bsz, t, d)
```

```python
import functools
import math

import numpy as np
import jax
import jax.numpy as jnp
from jax import lax
from jax.experimental import pallas as pl
from jax.experimental.pallas import tpu as pltpu

F32 = jnp.float32
BF16 = jnp.bfloat16

D_MODEL = 1024
DEPTH = 2
NSA_HEADS = 8
NSA_GROUPS = 2
NSA_HPG = NSA_HEADS // NSA_GROUPS
HEAD_DIM = 64
NSA_WIDTH = NSA_HEADS * HEAD_DIM
NSA_KV = NSA_GROUPS * HEAD_DIM
CMP_STRIDE = 16
CMP_LEN = 2 * CMP_STRIDE
CMP_HIDDEN = 256
SEL_BLOCK = 64
SEL_TOPN = 16
WINDOW = 512
Q_BLOCK = 128
SEL_FORCE = 1e9
RWKV_HEADS = 8
RWKV_HEAD = 64
RWKV_WIDTH = RWKV_HEADS * RWKV_HEAD
LORA_W = 64
LORA_A = 64
LORA_V = 32
LORA_G = 128
GN_EPS = 64e-5
REL_BUCKETS = 32
REL_MAX_DIST = 1024
XATTN_HEADS = 4
XATTN_HEAD = 128
XATTN_WIDTH = XATTN_HEADS * XATTN_HEAD
N_EXPERTS = 16
N_EXPERT_GROUPS = 4
EXPERTS_PER_GROUP = N_EXPERTS // N_EXPERT_GROUPS
TOP_K = 2
EXPERT_FF = 512
DN_ALPHA = (2 * DEPTH) ** 0.25
LN_EPS = 1e-5
NEG_INF = -1e30
NSA_COLS = NSA_WIDTH + 6 * NSA_KV + 3 * NSA_HEADS
RWKV_COLS = 3 * RWKV_WIDTH + LORA_W + LORA_A + LORA_G

LANE = 128
VMEM_LIMIT = 56 * 1024 * 1024

C_Q = 0
C_GN = 1024
C_GR = 2048
C_R = 3072
C_K = 3584
C_V = 4096
C_G3 = 4608
C_WA = 4864
C_ZG = 4992
IN_PAD = 5120
KV_ARRAYS = 6

MOE_BLK = 512
SCAN_TT = 128
FAR_BIAS_DIST = 1280


def _cparams(sem):
    return pltpu.CompilerParams(dimension_semantics=sem, vmem_limit_bytes=VMEM_LIMIT)


def _sigmoid(x):
    return 1.0 / (1.0 + jnp.exp(-x))


def _layer_norm(v, g, b):
    mu = jnp.mean(v, axis=-1, keepdims=True)
    d = v - mu
    var = jnp.mean(d * d, axis=-1, keepdims=True)
    return d * lax.rsqrt(var + LN_EPS) * g + b


def _dot(a, b):
    return jnp.dot(a, b, preferred_element_type=F32)


def _dot_nt(a, b):
    return lax.dot_general(a, b, (((1,), (1,)), ((), ())), preferred_element_type=F32)


def _dot_split(x, w):
    hi = x.astype(BF16)
    lo = (x - hi.astype(F32)).astype(BF16)
    return _dot(hi, w) + _dot(lo, w)


def _mm_kernel(x_ref, w_ref, o_ref, xb_ref):
    @pl.when(pl.program_id(1) == 0)
    def _():
        xb_ref[...] = x_ref[...].astype(BF16)

    o_ref[...] = _dot(xb_ref[...], w_ref[...]).astype(o_ref.dtype)


def _matmul(x, w, out_dtype, tm, tn):
    n, k = x.shape
    m = w.shape[1]
    return pl.pallas_call(
        _mm_kernel,
        grid=(n // tm, m // tn),
        in_specs=[pl.BlockSpec((tm, k), lambda i, j: (i, 0)),
                  pl.BlockSpec((k, tn), lambda i, j: (0, j))],
        out_specs=pl.BlockSpec((tm, tn), lambda i, j: (i, j)),
        out_shape=jax.ShapeDtypeStruct((n, m), out_dtype),
        scratch_shapes=[pltpu.VMEM((tm, k), BF16)],
        compiler_params=_cparams(("parallel", "arbitrary")),
        name="matmul",
    )(x, w)


def _kv_proj_kernel(tiles_per_seq, x_ref, w_ref, kc_ref, vc_ref, ks_ref, vs_ref, kw_ref, vw_ref, xb_ref):
    @pl.when(pl.program_id(1) == 0)
    def _():
        xb_ref[...] = x_ref[...].astype(BF16)

    y = _dot(xb_ref[...], w_ref[pl.program_id(1)])
    tm = y.shape[0]
    slot = lambda i: y[:, i * LANE:(i + 1) * LANE]
    lane = lax.broadcasted_iota(jnp.int32, (tm, LANE), 1)
    one_lane = jnp.where(lane == HEAD_DIM, 1.0, 0.0)
    t0 = lax.rem(pl.program_id(0), tiles_per_seq) * tm
    block = jnp.right_shift(t0 + lax.broadcasted_iota(jnp.int32, (tm, LANE), 0), int(math.log2(SEL_BLOCK)))
    kc_ref[0, 0] = slot(0)[:, :HEAD_DIM]
    vc_ref[0, 0] = slot(1)[:, :HEAD_DIM]
    ks_ref[0, 0] = jnp.concatenate([slot(2) + jnp.where(lane == HEAD_DIM + 1, 1.0, one_lane),
                                    jnp.where(lane == block, 1.0, 0.0)], axis=1).astype(BF16)
    vs_ref[0, 0] = (slot(3) + one_lane).astype(BF16)
    kw_ref[0, 0] = slot(4).astype(BF16)
    vw_ref[0, 0] = (slot(5) + one_lane).astype(BF16)


def _kv_proj(x, w_kv, bsz, t, tm=512):
    n, d = x.shape
    tps = t // tm
    g = NSA_GROUPS

    def out(width, dtype):
        return (pl.BlockSpec((1, 1, tm, width), lambda i, j: (i // tps, j, i % tps, 0)),
                jax.ShapeDtypeStruct((bsz, g, t, width), dtype))

    outs = [out(HEAD_DIM, F32), out(HEAD_DIM, F32), out(2 * LANE, BF16), out(LANE, BF16), out(LANE, BF16),
            out(LANE, BF16)]
    return pl.pallas_call(
        functools.partial(_kv_proj_kernel, tps),
        grid=(n // tm, g),
        in_specs=[pl.BlockSpec((tm, d), lambda i, j: (i, 0)),
                  pl.BlockSpec((g, d, KV_ARRAYS * LANE), lambda i, j: (0, 0, 0))],
        out_specs=[o[0] for o in outs],
        out_shape=[o[1] for o in outs],
        scratch_shapes=[pltpu.VMEM((tm, d), BF16)],
        compiler_params=_cparams(("parallel", "arbitrary")),
        name="nsa_kv_proj",
    )(x, w_kv)


def _compress_kernel(u_ref, pe_ref, w1_ref, w2_ref, o_ref):
    u = u_ref[0, 0]
    a = _dot((u + pe_ref[0:1, :]).astype(BF16), w1_ref[0])
    b = _dot((u + pe_ref[1:2, :]).astype(BF16), w1_ref[1])
    nc = u.shape[0]
    h = a + pltpu.roll(b, nc - 1, 0)
    h = 0.5 * h * (1.0 + jnp.tanh(math.sqrt(2.0 / math.pi) * (h + 0.044715 * (h * h * h))))
    y = _dot(h.astype(BF16), w2_ref[...])
    row = lax.broadcasted_iota(jnp.int32, y.shape, 0)
    o_ref[0, 0] = jnp.where(row < nc - 1, y, 0.0)


def _compress(u, pe2, w1, w2p):
    b, g, nc, _ = u.shape
    return pl.pallas_call(
        _compress_kernel,
        grid=(b, g),
        in_specs=[pl.BlockSpec((1, 1, nc, 1024), lambda i, j: (i, j, 0, 0)),
                  pl.BlockSpec((2, 1024), lambda i, j: (0, 0)),
                  pl.BlockSpec((2, 1024, CMP_HIDDEN), lambda i, j: (0, 0, 0)),
                  pl.BlockSpec((CMP_HIDDEN, LANE), lambda i, j: (0, 0))],
        out_specs=pl.BlockSpec((1, 1, nc, LANE), lambda i, j: (i, j, 0, 0)),
        out_shape=jax.ShapeDtypeStruct((b, g, nc, LANE), F32),
        compiler_params=_cparams(("parallel", "parallel")),
        name="nsa_compress",
    )(u, pe2, w1, w2p)


def _stack_heads(q_ref, row0=0):
    q = q_ref[row0:row0 + Q_BLOCK, :] * (HEAD_DIM ** -0.5)
    return jnp.concatenate([q[:, h * LANE:(h + 1) * LANE] for h in range(NSA_HPG)], axis=0).astype(BF16)


def _store_heads(o_ref, o, gates, branch, row0=0):
    lane = lax.broadcasted_iota(jnp.int32, (Q_BLOCK, LANE), 1)
    for pair in range(NSA_HPG // 2):
        halves = []
        for h in (2 * pair, 2 * pair + 1):
            c = branch * NSA_HPG + h
            halves.append(o[h * Q_BLOCK:(h + 1) * Q_BLOCK, :] * gates[:, c:c + 1])
        o_ref[0, pair, row0:row0 + Q_BLOCK, :] = jnp.where(lane < HEAD_DIM, halves[0],
                                                           pltpu.roll(halves[1], HEAD_DIM, 1)).astype(o_ref.dtype)


CMP_QB = 2


def _cmp_kernel(q_ref, zg_ref, kc_ref, vc_ref, pc0_ref, pc1_ref, ov_ref, o_ref, m_ref, *scratch):
    nc = kc_ref.shape[2]
    nsel = ov_ref.shape[0]
    kc = kc_ref[0, 0].astype(BF16)
    vc = vc_ref[0, 0].astype(BF16)
    gates = _sigmoid(zg_ref[...])
    blocks = [(qb, pl.program_id(2) * CMP_QB + qb, pc_ref, scratch[2 * qb], scratch[2 * qb + 1])
              for qb, pc_ref in enumerate((pc0_ref, pc1_ref))]
    for qb, c, _, p_s, _ in blocks:
        p_s[...] = _dot_nt(_stack_heads(q_ref, qb * Q_BLOCK), kc)
    for qb, c, pc_ref, p_s, pb_s in blocks:
        off = pl.multiple_of(lax.div((nc - 8) - 8 * c, LANE) * LANE, LANE)
        for r in range(NSA_HPG * Q_BLOCK // SEL_ROWS):
            rows = slice(r * SEL_ROWS, (r + 1) * SEL_ROWS)
            s = p_s[rows, :] + pc_ref[0, 0, rows, pl.ds(off, nc)]
            e = jnp.exp(s - jnp.max(s, axis=-1, keepdims=True))
            l = jnp.sum(e, axis=-1, keepdims=True)
            ql = (lax.broadcasted_iota(jnp.int32, (SEL_ROWS, 1), 0) + r * SEL_ROWS) & (Q_BLOCK - 1)
            p = e * jnp.where((c * Q_BLOCK + ql) >= (CMP_LEN - 1), 1.0 / l, 0.0)
            p_s[rows, :] = p
            pb_s[rows, :] = p.astype(BF16)
    vals = []
    jb = lax.broadcasted_iota(jnp.int32, (nsel, Q_BLOCK), 0)
    jbf = jb.astype(F32)
    for qb, c, _, p_s, pb_s in blocks:
        _store_heads(o_ref, _dot(pb_s[...], vc), gates[qb * Q_BLOCK:(qb + 1) * Q_BLOCK, :], 0, qb * Q_BLOCK)
        psum = ((p_s[0:Q_BLOCK, :] + p_s[Q_BLOCK:2 * Q_BLOCK, :])
                + (p_s[2 * Q_BLOCK:3 * Q_BLOCK, :] + p_s[3 * Q_BLOCK:4 * Q_BLOCK, :]))
        hi = psum.astype(BF16)
        lo = (psum - hi.astype(F32)).astype(BF16)
        imp = _dot_nt(ov_ref[...], hi) + _dot_nt(ov_ref[...], lo)
        tq = c * Q_BLOCK + lax.broadcasted_iota(jnp.int32, (nsel, Q_BLOCK), 1)
        jq = jnp.right_shift(tq, int(math.log2(SEL_BLOCK)))
        forced = (jb == 0) | (jb == jq) | (jb == jq - 1)
        vals.append(jnp.where(forced, SEL_FORCE, jnp.where(jb <= jq, imp, -SEL_FORCE)))
    sels = [jnp.zeros((nsel, Q_BLOCK), F32) for _ in blocks]
    for _ in range(SEL_TOPN):
        for qb in range(CMP_QB):
            mx = jnp.max(vals[qb], axis=0, keepdims=True)
            first = jnp.min(jnp.where(vals[qb] == mx, jbf, float(nsel)), axis=0, keepdims=True)
            hit = jbf == first
            sels[qb] = jnp.where(hit, jnp.where(mx >= 0.0, 1.0, 0.0), sels[qb])
            vals[qb] = jnp.where(hit, -3e38, vals[qb])
    for qb in range(CMP_QB):
        m_ref[0, 0, qb * Q_BLOCK:(qb + 1) * Q_BLOCK, :] = (sels[qb].T - 1.0) * (-NEG_INF)


def _cmp_branch(z, k_cmp, v_cmp, pc_tab, overlap, bsz, t):
    tq = CMP_QB * Q_BLOCK
    nq = t // tq
    nc = k_cmp.shape[2]
    nsel = overlap.shape[0]
    rows = NSA_HPG * Q_BLOCK

    def variant(qb):
        return pl.BlockSpec((1, 1, rows, 2 * nc - LANE),
                            lambda b, g, c: (g, lax.rem((nc - 8) - 8 * (CMP_QB * c + qb), LANE) // 8, 0, 0))

    return pl.pallas_call(
        _cmp_kernel,
        grid=(bsz, NSA_GROUPS, nq),
        in_specs=[pl.BlockSpec((tq, NSA_HPG * LANE), lambda b, g, c: (b * nq + c, g)),
                  pl.BlockSpec((tq, LANE), lambda b, g, c: (b * nq + c, C_G3 // LANE + g)),
                  pl.BlockSpec((1, 1, nc, LANE), lambda b, g, c: (b, g, 0, 0)),
                  pl.BlockSpec((1, 1, nc, LANE), lambda b, g, c: (b, g, 0, 0)),
                  variant(0), variant(1),
                  pl.BlockSpec((nsel, nc), lambda b, g, c: (0, 0))],
        out_specs=[pl.BlockSpec((1, NSA_HPG // 2, tq, LANE), lambda b, g, c: (b, g, c, 0)),
                   pl.BlockSpec((1, 1, tq, nsel), lambda b, g, c: (b, g, c, 0))],
        out_shape=[jax.ShapeDtypeStruct((bsz, NSA_HEADS // 2, t, LANE), BF16),
                   jax.ShapeDtypeStruct((bsz, NSA_GROUPS, t, nsel), F32)],
        scratch_shapes=[pltpu.VMEM((rows, nc), F32), pltpu.VMEM((rows, nc), BF16)] * CMP_QB,
        compiler_params=_cparams(("parallel", "parallel", "arbitrary")),
        name="nsa_cmp_select",
    )(z, z, k_cmp, v_cmp, pc_tab, pc_tab, overlap)


WIN_QB = 2


def _win_kernel(q_ref, zg_ref, k_ref, v_ref, wb_ref, o_ref, *scratch):
    nvar = wb_ref.shape[1]
    gates = _sigmoid(zg_ref[...])
    blocks = []
    for qb in range(WIN_QB):
        c = pl.program_id(2) * WIN_QB + qb
        start = pl.multiple_of(jnp.maximum(c * Q_BLOCK - WINDOW, 0), Q_BLOCK)
        blocks.append((qb, jnp.minimum(c, nvar - 1), pl.ds(start, WINDOW + Q_BLOCK), scratch[2 * qb], scratch[2 * qb + 1]))
    for qb, _, keys, s_s, _ in blocks:
        s_s[...] = _dot_nt(_stack_heads(q_ref, qb * Q_BLOCK), k_ref[0, 0, keys, :])
    for qb, var, _, s_s, p_s in blocks:
        for r in range(NSA_HPG * Q_BLOCK // SEL_ROWS):
            rows = slice(r * SEL_ROWS, (r + 1) * SEL_ROWS)
            s = s_s[rows, :] + wb_ref[0, var, rows, :]
            p_s[rows, :] = jnp.exp(s - jnp.max(s, axis=-1, keepdims=True)).astype(BF16)
    for qb, _, keys, _, p_s in blocks:
        o = _dot(p_s[...], v_ref[0, 0, keys, :])
        o = o * (1.0 / o[:, HEAD_DIM:HEAD_DIM + 1])
        _store_heads(o_ref, o, gates[qb * Q_BLOCK:(qb + 1) * Q_BLOCK, :], 2, qb * Q_BLOCK)


def _win_branch(z, kw, vw, wb_tab, bsz, t):
    tq = WIN_QB * Q_BLOCK
    nq = t // tq
    rows, width = NSA_HPG * Q_BLOCK, WINDOW + Q_BLOCK
    return pl.pallas_call(
        _win_kernel,
        grid=(bsz, NSA_GROUPS, nq),
        in_specs=[pl.BlockSpec((tq, NSA_HPG * LANE), lambda b, g, c: (b * nq + c, g)),
                  pl.BlockSpec((tq, LANE), lambda b, g, c: (b * nq + c, C_G3 // LANE + g)),
                  pl.BlockSpec((1, 1, t, LANE), lambda b, g, c: (b, g, 0, 0)),
                  pl.BlockSpec((1, 1, t, LANE), lambda b, g, c: (b, g, 0, 0)),
                  pl.BlockSpec((1,) + wb_tab.shape[1:], lambda b, g, c: (g, 0, 0, 0))],
        out_specs=pl.BlockSpec((1, NSA_HPG // 2, tq, LANE), lambda b, g, c: (b, g, c, 0)),
        out_shape=jax.ShapeDtypeStruct((bsz, NSA_HEADS // 2, t, LANE), BF16),
        scratch_shapes=[pltpu.VMEM((rows, width), F32), pltpu.VMEM((rows, width), BF16)] * WIN_QB,
        compiler_params=_cparams(("parallel", "parallel", "arbitrary")),
        name="nsa_window",
    )(z, z, kw, vw, wb_tab)


SEL_TK = 512


SEL_ROWS = 64
SEL_NEAR = FAR_BIAS_DIST + Q_BLOCK


def _sel_kernel(q_ref, zg_ref, k_ref, v_ref, mt_ref, lt_ref, cq_ref, o_ref, m_s, acc_s, s0_s, s1_s, p0_s, p1_s,
                a0_s, a1_s):
    c = pl.program_id(2)
    q = q_ref[...] * (HEAD_DIM ** -0.5)
    qs = jnp.concatenate([q[:, h * LANE:(h + 1) * LANE] for h in range(NSA_HPG)], axis=0) + cq_ref[0]
    mt = mt_ref[0, 0]
    q_aug = jnp.concatenate([qs, jnp.concatenate([mt] * NSA_HPG, axis=0)], axis=1).astype(BF16)
    m_s[...] = jnp.full(m_s.shape, NEG_INF, F32)
    acc_s[...] = jnp.zeros(acc_s.shape, F32)
    ncol = SEL_TK // LANE
    n_tiles = lax.div(c, SEL_TK // Q_BLOCK) + 1

    def key_rows(i):
        return pl.ds(pl.multiple_of(jnp.minimum(i, n_tiles - 1) * SEL_TK, SEL_TK), SEL_TK)

    def scores(i, dst):
        dst[...] = _dot_nt(q_aug, k_ref[0, 0, key_rows(i), :])

    def tile(i, cur, nxt, p_buf, a_buf):
        scores(i + 1, nxt)
        off = jnp.where(i < n_tiles, jnp.maximum(SEL_NEAR + SEL_TK - (c * Q_BLOCK - i * SEL_TK), 0),
                        SEL_NEAR + SEL_TK + Q_BLOCK)
        off = pl.multiple_of(off, LANE)
        for r in range(NSA_HPG * Q_BLOCK // SEL_ROWS):
            rows = slice(r * SEL_ROWS, (r + 1) * SEL_ROWS)
            s = cur[rows, :] + lt_ref[0, rows, pl.ds(off, SEL_TK)]
            cols = [s[:, j * LANE:(j + 1) * LANE] for j in range(ncol)]
            mx = functools.reduce(jnp.maximum, cols)
            m_old = m_s[rows, :]
            m_new = jnp.maximum(m_old, jnp.max(mx, axis=-1, keepdims=True))
            m_s[rows, :] = m_new
            a_buf[rows, :] = jnp.exp(m_old - m_new)
            p_buf[rows, :] = jnp.concatenate([jnp.exp(col - m_new) for col in cols], axis=1).astype(BF16)
        acc_s[...] = a_buf[...] * acc_s[...] + _dot(p_buf[...], v_ref[0, 0, key_rows(i), :])

    scores(0, s0_s)

    def pair(j, carry):
        tile(2 * j, s0_s, s1_s, p0_s, a0_s)
        tile(2 * j + 1, s1_s, s0_s, p1_s, a1_s)
        return carry

    lax.fori_loop(0, lax.div(n_tiles + 1, 2), pair, 0)
    acc = acc_s[...]
    o = acc * (1.0 / acc[:, HEAD_DIM:HEAD_DIM + 1])
    _store_heads(o_ref, o, _sigmoid(zg_ref[...]), 1)


def _sel_branch(z, ks_aug, vs, mt, lt_tab, cq_tab, bsz, t):
    nq = t // Q_BLOCK
    nsel = mt.shape[-1]
    rows = NSA_HPG * Q_BLOCK
    return pl.pallas_call(
        _sel_kernel,
        grid=(bsz, NSA_GROUPS, nq),
        in_specs=[pl.BlockSpec((Q_BLOCK, NSA_HPG * LANE), lambda b, g, c: (b * nq + c, g)),
                  pl.BlockSpec((Q_BLOCK, LANE), lambda b, g, c: (b * nq + c, C_G3 // LANE + g)),
                  pl.BlockSpec((1, 1, t, 2 * LANE), lambda b, g, c: (b, g, 0, 0)),
                  pl.BlockSpec((1, 1, t, LANE), lambda b, g, c: (b, g, 0, 0)),
                  pl.BlockSpec((1, 1, Q_BLOCK, nsel), lambda b, g, c: (b, g, c, 0)),
                  pl.BlockSpec((1, rows, lt_tab.shape[2]), lambda b, g, c: (g, 0, 0)),
                  pl.BlockSpec((1, rows, LANE), lambda b, g, c: (g, 0, 0))],
        out_specs=pl.BlockSpec((1, NSA_HPG // 2, Q_BLOCK, LANE), lambda b, g, c: (b, g, c, 0)),
        out_shape=jax.ShapeDtypeStruct((bsz, NSA_HEADS // 2, t, LANE), BF16),
        scratch_shapes=[pltpu.VMEM((rows, LANE), F32),
                        pltpu.VMEM((rows, LANE), F32),
                        pltpu.VMEM((rows, SEL_TK), F32),
                        pltpu.VMEM((rows, SEL_TK), F32),
                        pltpu.VMEM((rows, SEL_TK), BF16),
                        pltpu.VMEM((rows, SEL_TK), BF16),
                        pltpu.VMEM((rows, LANE), F32),
                        pltpu.VMEM((rows, LANE), F32)],
        compiler_params=_cparams(("parallel", "parallel", "arbitrary")),
        name="nsa_selected",
    )(z, z, ks_aug, vs, mt, lt_tab, cq_tab)


def _shift_mix(z_ref, prev_ref, mu, first):
    z = z_ref[...]
    prev = jnp.where(first, 0.0, prev_ref[7:8, :])
    row = lax.broadcasted_iota(jnp.int32, z.shape, 0)
    zp = jnp.where(row == 0, prev, pltpu.roll(z, 1, 0))
    return z + mu * (zp - z)


def _rwkv_prep_kernel(has_vres, tiles_per_seq, *refs):
    (zr, zk, zv, zwa, zg, pr, pk, pv, pwa, pg, mu_ref, vec_ref, w2_ref, a2_ref, g2_ref, hsum_ref) = refs[:16]
    pos = 16
    if has_vres:
        vf_ref, v0_ref, v1_ref, v2_ref = refs[pos:pos + 4]
        pos += 4
    r_o, w_o, k_o, v_o, kk_o, b_o, g_o, bonus_o, vt_o = refs[pos:]
    first = lax.rem(pl.program_id(0), tiles_per_seq) == 0
    r = _shift_mix(zr, pr, mu_ref[0:1, :], first)
    k = _shift_mix(zk, pk, mu_ref[1:2, :], first)
    v = _shift_mix(zv, pv, mu_ref[2:3, :], first)
    wa = _shift_mix(zwa, pwa, mu_ref[3:4, 0:LANE], first)
    zg_s = _shift_mix(zg, pg, mu_ref[4:5, 0:LANE], first)
    w0, a0, k_k, k_a, r_k = (vec_ref[i:i + 1, :] for i in range(5))
    if has_vres:
        lora = _dot(_dot(v.astype(BF16), v1_ref[...]).astype(BF16), v2_ref[...])
        v = v + (vf_ref[...] - v) * _sigmoid(v0_ref[...] + lora)
    u = w0 + _dot(jnp.tanh(wa).astype(BF16), w2_ref[...])
    decay = jnp.exp(-math.exp(-0.5) * _sigmoid(u))
    a = _sigmoid(a0 + _dot(wa.astype(BF16), a2_ref[...]))
    g = _dot(_sigmoid(zg_s).astype(BF16), g2_ref[...])
    kk = k * k_k
    kk = kk / jnp.maximum(jnp.sqrt(_dot_split(kk * kk, hsum_ref[...])), 1e-12)
    k = k * (1.0 + (a - 1.0) * k_a)
    r_o[...] = r
    w_o[...] = decay
    k_o[...] = k.astype(k_o.dtype)
    v_o[...] = v
    kk_o[...] = kk
    b_o[...] = kk * a
    g_o[...] = g.astype(g_o.dtype)
    bonus_o[...] = (_dot_split(r * k * r_k, hsum_ref[...]) * v).astype(bonus_o.dtype)
    vt_o[0] = v.T.astype(BF16)


def _rwkv_prep(z, mu5, vecs, w2p, a2p, g2, hsum, vres, bsz, t, tm=256):
    n = bsz * t
    nt = n // tm
    tps = t // tm
    w = RWKV_WIDTH

    def cur(width, col):
        return pl.BlockSpec((tm, width), lambda i: (i, col // width))

    def prev(width, col):
        return pl.BlockSpec((8, width), lambda i: (jnp.maximum(i * (tm // 8) - 1, 0), col // width))

    def full(shape):
        return pl.BlockSpec(shape, lambda i: (0,) * len(shape))

    cols = [(w, C_R), (w, C_K), (w, C_V), (LANE, C_WA), (LANE, C_ZG)]
    in_specs = [cur(*c) for c in cols] + [prev(*c) for c in cols]
    in_specs += [full(mu5.shape), full(vecs.shape), full(w2p.shape), full(a2p.shape), full(g2.shape), full(hsum.shape)]
    args = [z] * 10 + [mu5, vecs, w2p, a2p, g2, hsum]
    if vres is not None:
        v_first, v0, v1p, v2p = vres
        in_specs += [pl.BlockSpec((tm, w), lambda i: (i, 0)), full(v0.shape), full(v1p.shape), full(v2p.shape)]
        args += [v_first, v0, v1p, v2p]
    tok = pl.BlockSpec((tm, w), lambda i: (i, 0))
    out_specs = [tok] * 8 + [pl.BlockSpec((1, w, tm), lambda i: (i // tps, 0, i % tps))]
    tok_dtypes = (F32, F32, BF16, F32, F32, F32, BF16, BF16)
    out_shape = [jax.ShapeDtypeStruct((n, w), dt) for dt in tok_dtypes] + [jax.ShapeDtypeStruct((bsz, w, t), BF16)]
    return pl.pallas_call(
        functools.partial(_rwkv_prep_kernel, vres is not None, tps),
        grid=(nt,),
        in_specs=in_specs,
        out_specs=out_specs,
        out_shape=out_shape,
        compiler_params=_cparams(("parallel",)),
        name="rwkv_prep",
    )(*args)


QUAD = 4
QUAD_LANES = QUAD * RWKV_HEAD
SUB_STEPS = LANE // QUAD


def _scan_kernel(r_ref, w_ref, k_ref, kk_ref, b_ref, vt_ref, ones_ref, place_ref, o_ref, s_ref, vq_ref, kq_ref, vk_ref):
    bsz = r_ref.shape[0]
    tt = r_ref.shape[1]
    nh = RWKV_HEAD
    quads = [(bi, hf) for bi in range(bsz) for hf in range(RWKV_HEADS // QUAD)]

    @pl.when(pl.program_id(0) == 0)
    def _():
        s_ref[...] = jnp.zeros(s_ref.shape, F32)

    sub_shift, head_shift = int(math.log2(SUB_STEPS)), int(math.log2(nh))
    sub_lane = lax.broadcasted_iota(jnp.int32, (nh, LANE), 1) & (SUB_STEPS - 1)
    k_row_head = jnp.right_shift(lax.broadcasted_iota(jnp.int32, (LANE, QUAD_LANES), 0), sub_shift)
    k_lane_head = jnp.right_shift(lax.broadcasted_iota(jnp.int32, (LANE, QUAD_LANES), 1), head_shift)
    r_mask = (lax.broadcasted_iota(jnp.int32, (16, QUAD_LANES), 0)
              == jnp.right_shift(lax.broadcasted_iota(jnp.int32, (16, QUAD_LANES), 1), head_shift))
    ones_bd = ones_ref[...]

    def sub_block(sb, carry):
        s0 = pl.multiple_of(sb * SUB_STEPS, SUB_STEPS)
        for q, (bi, hf) in enumerate(quads):
            ln = slice(hf * QUAD_LANES, (hf + 1) * QUAD_LANES)
            vq = _dot(vt_ref[bi, (QUAD * hf) * nh:(QUAD * hf + 1) * nh, :], place_ref[sb * QUAD])
            for h in range(1, QUAD):
                vq = vq + _dot(vt_ref[bi, (QUAD * hf + h) * nh:(QUAD * hf + h + 1) * nh, :], place_ref[sb * QUAD + h])
            vq_ref[q] = vq.astype(BF16)
            k_sub = k_ref[bi, pl.ds(s0, SUB_STEPS), ln].astype(F32)
            kq_ref[q] = jnp.where(k_row_head == k_lane_head, jnp.concatenate([k_sub] * QUAD, axis=0), 0.0).astype(BF16)

        def group(tg, carry):
            t0 = pl.multiple_of(s0 + tg * 8, 8)
            rows = []
            for bi, hf in quads:
                ln = slice(hf * QUAD_LANES, (hf + 1) * QUAD_LANES)
                rows.append(tuple(ref[bi, pl.ds(t0, 8), ln] for ref in (w_ref, kk_ref, b_ref, r_ref)))
            onehots = [jnp.where(sub_lane == tg * 8 + u, 1.0, 0.0).astype(BF16) for u in range(8)]
            for q in range(len(quads)):
                vq = vq_ref[q]
                vk_ref[q] = _dot(jnp.concatenate([vq * oh for oh in onehots], axis=0), kq_ref[q])
            state = [s_ref[q] for q in range(len(quads))]
            for u in range(8):
                sk = _dot(jnp.concatenate([(state[q] * rows[q][1][u:u + 1, :]).astype(BF16)
                                           for q in range(len(quads))], axis=0), ones_bd)
                for q, (bi, hf) in enumerate(quads):
                    w8, _, b8, r8 = rows[q]
                    state[q] = (state[q] * w8[u:u + 1, :] - sk[q * nh:(q + 1) * nh, :] * b8[u:u + 1, :]
                                + vk_ref[q, u * nh:(u + 1) * nh, :])
                    r_lhs = jnp.where(r_mask, r8[u:u + 1, :], 0.0).astype(BF16)
                    o = _dot_nt(r_lhs, state[q].astype(BF16))
                    o_ref[t0 + u, bi, QUAD * hf:QUAD * (hf + 1), :] = o[0:QUAD, :]
            for q in range(len(quads)):
                s_ref[q] = state[q]
            return carry

        return lax.fori_loop(0, SUB_STEPS // 8, group, carry)

    lax.fori_loop(0, tt // SUB_STEPS, sub_block, 0)


def _rwkv_scan(r, w, k, kk, b, vt, ones_bd, place, bsz, t):
    tt = SCAN_TT
    wd = RWKV_WIDTH
    nquad = bsz * RWKV_HEADS // QUAD
    tok = pl.BlockSpec((bsz, tt, wd), lambda i: (0, i, 0))
    return pl.pallas_call(
        _scan_kernel,
        grid=(t // tt,),
        in_specs=[tok, tok, tok, tok, tok,
                  pl.BlockSpec((bsz, wd, tt), lambda i: (0, 0, i)),
                  pl.BlockSpec(ones_bd.shape, lambda i: (0, 0)),
                  pl.BlockSpec(place.shape, lambda i: (0, 0, 0))],
        out_specs=pl.BlockSpec((tt, bsz, RWKV_HEADS, RWKV_HEAD), lambda i: (i, 0, 0, 0)),
        out_shape=jax.ShapeDtypeStruct((t, bsz, RWKV_HEADS, RWKV_HEAD), F32),
        scratch_shapes=[pltpu.VMEM((nquad, RWKV_HEAD, QUAD_LANES), F32),
                        pltpu.VMEM((nquad, RWKV_HEAD, LANE), BF16),
                        pltpu.VMEM((nquad, LANE, QUAD_LANES), BF16),
                        pltpu.VMEM((nquad, 8 * RWKV_HEAD, QUAD_LANES), F32)],
        compiler_params=_cparams(("arbitrary",)),
        name="rwkv_scan",
    )(r, w, k, kk, b, vt, ones_bd, place)


def _merge_kernel(x_ref, oc_ref, os_ref, ow_ref, orw_ref, bonus_ref, g_ref, zgn_ref, zgr_ref, gn_ref, hsum_ref,
                  pn_ref, pr_ref, wo_ref, ln_ref, o_ref):
    o_n = jnp.concatenate([oc_ref[0, p].astype(F32) + os_ref[0, p].astype(F32) + ow_ref[0, p].astype(F32)
                           for p in range(NSA_HEADS // 2)], axis=1)
    y_n = _dot(o_n.astype(BF16), pn_ref[...])
    o = orw_ref[...]
    mu = _dot_split(o, hsum_ref[...]) * (1.0 / RWKV_HEAD)
    d = o - mu
    var = _dot_split(d * d, hsum_ref[...]) * (1.0 / RWKV_HEAD)
    o = d * lax.rsqrt(var + GN_EPS) * gn_ref[0:1, :] + gn_ref[1:2, :]
    o = ((o + bonus_ref[...]) * g_ref[...]).astype(BF16)
    y1 = _sigmoid(zgn_ref[...]) * y_n + _sigmoid(zgr_ref[...]) * _dot(o, pr_ref[...])
    y = _dot(y1.astype(BF16), wo_ref[...])
    o_ref[...] = _layer_norm(DN_ALPHA * x_ref[...] + y, ln_ref[0:1, :], ln_ref[1:2, :])


def _merge(x, o_c, o_s, o_w, o_rw, bonus, g, z, gn, hsum, p_nsa, p_rwkv, w_out, ln, bsz, t, tm=256):
    n = bsz * t
    tps = t // tm
    d = D_MODEL
    w = RWKV_WIDTH
    head = pl.BlockSpec((1, NSA_HEADS // 2, tm, LANE), lambda i: (i // tps, 0, i % tps, 0))
    tok = pl.BlockSpec((tm, w), lambda i: (i, 0))

    def full(a):
        return pl.BlockSpec(a.shape, lambda i: (0,) * a.ndim)

    return pl.pallas_call(
        _merge_kernel,
        grid=(n // tm,),
        in_specs=[pl.BlockSpec((tm, d), lambda i: (i, 0)), head, head, head, tok, tok, tok,
                  pl.BlockSpec((tm, d), lambda i: (i, C_GN // d)),
                  pl.BlockSpec((tm, d), lambda i: (i, C_GR // d)),
                  full(gn), full(hsum), full(p_nsa), full(p_rwkv), full(w_out), full(ln)],
        out_specs=pl.BlockSpec((tm, d), lambda i: (i, 0)),
        out_shape=jax.ShapeDtypeStruct((n, d), F32),
        compiler_params=_cparams(("parallel",)),
        name="mix_merge_ln",
    )(x, o_c, o_s, o_w, o_rw, bonus, g, z, z, gn, hsum, p_nsa, p_rwkv, w_out, ln)


def _xattn_kernel(x_ref, kt_ref, v_ref, wq_ref, wo_ref, ln_ref, rw_ref, o_ref, s_ref):
    x = x_ref[...]
    q = _dot(x.astype(BF16), wq_ref[...]).astype(BF16)
    outs = []
    for h in range(XATTN_HEADS):
        hs = slice(h * XATTN_HEAD, (h + 1) * XATTN_HEAD)
        s = _dot(q[:, hs], kt_ref[0, hs, :]) * (XATTN_HEAD ** -0.5)
        m = jnp.max(s, axis=-1, keepdims=True)
        p = jnp.exp(s - m)
        l = jnp.sum(p, axis=-1, keepdims=True)
        outs.append(_dot(p.astype(BF16), v_ref[0, :, hs]) * (1.0 / l))
    o = jnp.concatenate(outs, axis=1).astype(BF16)
    x2 = _layer_norm(DN_ALPHA * x + _dot(o, wo_ref[...]), ln_ref[0:1, :], ln_ref[1:2, :])
    o_ref[...] = x2
    hi = x2.astype(BF16)
    lo = (x2 - hi.astype(F32)).astype(BF16)
    logits = _dot(hi, rw_ref[0]) + (_dot(lo, rw_ref[0]) + _dot(hi, rw_ref[1]))
    s_ref[...] = _sigmoid(logits)


def _xattn(x, kt, v, wq, wo, ln, rw2, bsz, t, tm=256):
    n = bsz * t
    tps = t // tm
    d = D_MODEL

    def full(a):
        return pl.BlockSpec(a.shape, lambda i: (0,) * a.ndim)

    return pl.pallas_call(
        _xattn_kernel,
        grid=(n // tm,),
        in_specs=[pl.BlockSpec((tm, d), lambda i: (i, 0)),
                  pl.BlockSpec((1,) + kt.shape[1:], lambda i: (i // tps, 0, 0)),
                  pl.BlockSpec((1,) + v.shape[1:], lambda i: (i // tps, 0, 0)),
                  full(wq), full(wo), full(ln), full(rw2)],
        out_specs=[pl.BlockSpec((tm, d), lambda i: (i, 0)), pl.BlockSpec((tm, LANE), lambda i: (i, 0))],
        out_shape=[jax.ShapeDtypeStruct((n, d), F32), jax.ShapeDtypeStruct((n, LANE), F32)],
        compiler_params=_cparams(("parallel",)),
        name="xattn_ln_router",
    )(x, kt, v, wq, wo, ln, rw2)


def _expert_kernel(be_ref, x_ref, wg_ref, wu_ref, wd_ref, o_ref, wg_s, wu_s, wd_s):
    i = pl.program_id(0)

    @pl.when((i == 0) | (be_ref[i] != be_ref[jnp.maximum(i - 1, 0)]))
    def _():
        wg_s[...] = wg_ref[0].astype(BF16)
        wu_s[...] = wu_ref[0].astype(BF16)
        wd_s[...] = wd_ref[0].astype(BF16)

    n_used = be_ref[pl.num_programs(0)]

    @pl.when(i < n_used)
    def _():
        x = x_ref[...]
        hg = _dot(x, wg_s[...])
        h = hg * _sigmoid(hg) * _dot(x, wu_s[...])
        o_ref[...] = _dot(h.astype(BF16), wd_s[...]).astype(o_ref.dtype)

    @pl.when(i >= n_used)
    def _():
        o_ref[...] = jnp.zeros(o_ref.shape, o_ref.dtype)


def _experts(block_expert, rows, w_gate, w_up, w_down):
    n_rows, d = rows.shape
    ff = w_gate.shape[2]
    return pl.pallas_call(
        _expert_kernel,
        grid_spec=pltpu.PrefetchScalarGridSpec(
            num_scalar_prefetch=1,
            grid=(n_rows // MOE_BLK,),
            in_specs=[pl.BlockSpec((MOE_BLK, d), lambda i, be: (i, 0)),
                      pl.BlockSpec((1, d, ff), lambda i, be: (be[i], 0, 0)),
                      pl.BlockSpec((1, d, ff), lambda i, be: (be[i], 0, 0)),
                      pl.BlockSpec((1, ff, d), lambda i, be: (be[i], 0, 0))],
            out_specs=pl.BlockSpec((MOE_BLK, d), lambda i, be: (i, 0)),
            scratch_shapes=[pltpu.VMEM((d, ff), BF16), pltpu.VMEM((d, ff), BF16), pltpu.VMEM((ff, d), BF16)]),
        out_shape=jax.ShapeDtypeStruct((n_rows, d), BF16),
        compiler_params=_cparams(("arbitrary",)),
        name="moe_experts",
    )(block_expert, rows, w_gate, w_up, w_down)


def _combine_kernel(x_ref, ya_ref, yb_ref, gate_ref, ln_ref, o_ref):
    y = gate_ref[:, 0:1] * ya_ref[...] + gate_ref[:, 1:2] * yb_ref[...]
    o_ref[...] = _layer_norm(DN_ALPHA * x_ref[...] + y, ln_ref[0:1, :], ln_ref[1:2, :])


def _combine(x, ya, yb, gate, ln, tm=512):
    n, d = x.shape
    tok = pl.BlockSpec((tm, d), lambda i: (i, 0))
    return pl.pallas_call(
        _combine_kernel,
        grid=(n // tm,),
        in_specs=[tok, tok, tok, pl.BlockSpec((tm, TOP_K), lambda i: (i, 0)),
                  pl.BlockSpec(ln.shape, lambda i: (0, 0))],
        out_specs=tok,
        out_shape=jax.ShapeDtypeStruct((n, d), F32),
        compiler_params=_cparams(("parallel",)),
        name="moe_combine_ln",
    )(x, ya, yb, gate, ln)


def _t5_bucket(dist):
    n = jnp.maximum(dist, 0)
    max_exact = REL_BUCKETS // 2
    log_ratio = jnp.log(jnp.maximum(n, 1).astype(F32) / max_exact) / math.log(REL_MAX_DIST / max_exact)
    large = jnp.minimum(max_exact + (log_ratio * (REL_BUCKETS - max_exact)).astype(jnp.int32), REL_BUCKETS - 1)
    return jnp.where(n < max_exact, n, large)


def _bias_tables(rel_table, t):
    nc = t // CMP_STRIDE
    f = rel_table[_t5_bucket(jnp.arange(t + 2048))].astype(F32).T
    ql = jnp.arange(Q_BLOCK)[:, None]

    def lookup(dist, ok):
        vals = jnp.take(f, jnp.clip(dist, 0, f.shape[-1] - 1), axis=1)
        vals = jnp.where(ok[None], vals, NEG_INF)
        return vals.reshape(NSA_GROUPS, NSA_HPG * dist.shape[0], dist.shape[1])

    j = jnp.arange(2 * nc)[None, :]
    dist_c = CMP_STRIDE * (nc - 8 - j) + ql - (CMP_LEN - 1)
    pc = lookup(dist_c, dist_c >= 0)
    pc = jnp.stack([pc[:, :, s:s + 2 * nc - LANE] for s in range(0, LANE, 8)], axis=1)
    nvar = WINDOW // Q_BLOCK + 1

    def toeplitz(base, width, hi, shift):
        period = width + Q_BLOCK
        m = np.arange(period)
        dist = base - np.where(m < width, m, m - period)
        row = jnp.take(f, np.clip(dist, 0, f.shape[-1] - 1), axis=1) - shift
        row = jnp.where(((dist >= 0) & (dist < hi))[None], row, NEG_INF)
        flat = jnp.tile(row, (1, Q_BLOCK))[:, :Q_BLOCK * (period - 1)]
        return flat.reshape(NSA_HEADS, Q_BLOCK, period - 1)[:, :, :width]

    wb = jnp.stack([toeplitz(Q_BLOCK * v, WINDOW + Q_BLOCK, WINDOW, 0.0) for v in range(nvar)], axis=1)
    wb = wb.reshape(NSA_GROUPS, NSA_HPG, nvar, Q_BLOCK, -1).transpose(0, 2, 1, 3, 4)
    wb = wb.reshape(NSA_GROUPS, nvar, NSA_HPG * Q_BLOCK, -1)
    far = rel_table[REL_BUCKETS - 1].astype(F32)
    lt = toeplitz(SEL_NEAR + SEL_TK, SEL_NEAR + 2 * SEL_TK + Q_BLOCK, f.shape[-1], far[:, None])
    lt = lt.reshape(NSA_GROUPS, NSA_HPG * Q_BLOCK, -1)
    far_hi = far.astype(BF16).astype(F32)
    lane = jnp.arange(LANE)[None, :]
    cq = jnp.where(lane == HEAD_DIM, far_hi[:, None], jnp.where(lane == HEAD_DIM + 1, (far - far_hi)[:, None], 0.0))
    cq = jnp.broadcast_to(cq[:, None, :], (NSA_HEADS, Q_BLOCK, LANE)).reshape(NSA_GROUPS, NSA_HPG * Q_BLOCK, LANE)
    return pc, wb, lt, cq


def _in_proj_perm():
    perm = np.full((IN_PAD,), -1, np.int64)
    for h in range(NSA_HEADS):
        perm[C_Q + h * LANE:C_Q + h * LANE + HEAD_DIM] = np.arange(HEAD_DIM) + h * HEAD_DIM
    rw0 = NSA_COLS
    g0 = NSA_COLS + RWKV_COLS
    perm[C_GN:C_GN + D_MODEL] = g0 + np.arange(D_MODEL)
    perm[C_GR:C_GR + D_MODEL] = g0 + D_MODEL + np.arange(D_MODEL)
    perm[C_R:C_R + RWKV_WIDTH] = rw0 + np.arange(RWKV_WIDTH)
    perm[C_K:C_K + RWKV_WIDTH] = rw0 + RWKV_WIDTH + np.arange(RWKV_WIDTH)
    perm[C_V:C_V + RWKV_WIDTH] = rw0 + 2 * RWKV_WIDTH + np.arange(RWKV_WIDTH)
    for g in range(NSA_GROUPS):
        for br in range(3):
            for h in range(NSA_HPG):
                perm[C_G3 + g * LANE + br * NSA_HPG + h] = NSA_WIDTH + 6 * NSA_KV + (g * NSA_HPG + h) * 3 + br
    perm[C_WA:C_WA + LORA_W + LORA_A] = rw0 + 3 * RWKV_WIDTH + np.arange(LORA_W + LORA_A)
    perm[C_ZG:C_ZG + LORA_G] = rw0 + 3 * RWKV_WIDTH + LORA_W + LORA_A + np.arange(LORA_G)
    return perm


_PERM = _in_proj_perm()


def _permute_in_proj(w):
    pieces = []
    start = 0
    while start < IN_PAD:
        stop = start + 1
        if _PERM[start] < 0:
            while stop < IN_PAD and _PERM[stop] < 0:
                stop += 1
            pieces.append(jnp.zeros((w.shape[0], stop - start), BF16))
        else:
            while stop < IN_PAD and _PERM[stop] == _PERM[stop - 1] + 1:
                stop += 1
            pieces.append(w[:, int(_PERM[start]):int(_PERM[stop - 1]) + 1].astype(BF16))
        start = stop
    return jnp.concatenate(pieces, axis=1)


def _pad_to(a, shape):
    return jnp.pad(a, [(0, s - d) for d, s in zip(a.shape, shape)])


def _route(s, router_bias, n_tok):
    s16 = s[:, :N_EXPERTS]
    s_sel = s16 + router_bias.astype(F32)

    def top2(a):
        i1 = jnp.argmax(a, axis=-1)
        rest = jnp.where(jnp.arange(a.shape[-1]) == i1[..., None], -jnp.inf, a)
        i2 = jnp.argmax(rest, axis=-1)
        return jnp.max(a, axis=-1) + jnp.max(rest, axis=-1), jnp.stack([i1, i2], axis=-1)

    group_score, _ = top2(s_sel.reshape(n_tok, N_EXPERT_GROUPS, EXPERTS_PER_GROUP))
    group = jnp.argmax(group_score, axis=-1)
    in_group = (jnp.arange(N_EXPERTS) // EXPERTS_PER_GROUP)[None, :] == group[:, None]
    _, e_idx = top2(jnp.where(in_group, s_sel, NEG_INF))
    s_top = jnp.take_along_axis(s16, e_idx, axis=-1)
    gate = s_top / jnp.sum(s_top, axis=-1, keepdims=True)
    e_flat = e_idx.reshape(-1)
    onehot = (e_flat[:, None] == jnp.arange(N_EXPERTS)[None, :]).astype(jnp.int32)
    rank = jnp.take_along_axis(jnp.cumsum(onehot, axis=0) - onehot, e_flat[:, None], axis=1)[:, 0]
    counts = jnp.sum(onehot, axis=0)
    padded = (counts + MOE_BLK - 1) // MOE_BLK * MOE_BLK
    ends = jnp.cumsum(padded)
    dest = (ends - padded)[e_flat] + rank
    n_rows = n_tok * TOP_K + N_EXPERTS * MOE_BLK
    tok_of_row = jnp.zeros((n_rows,), jnp.int32).at[dest].set(jnp.arange(n_tok * TOP_K, dtype=jnp.int32) // TOP_K)
    block_expert = jnp.minimum(
        jnp.searchsorted(ends, jnp.arange(n_rows // MOE_BLK) * MOE_BLK, side='right'), N_EXPERTS - 1).astype(jnp.int32)
    block_expert = jnp.concatenate([block_expert, (ends[-1:] // MOE_BLK).astype(jnp.int32)])
    return gate, dest.reshape(n_tok, TOP_K), tok_of_row, block_expert


def kernel(x, mem, rel_table, router_w, router_bias, w_in, cmp_pe_k, cmp_pe_v, cmp_w1k, cmp_w2k, cmp_w1v, cmp_w2v, rwkv_mu, rwkv_w0, rwkv_w2, rwkv_a0, rwkv_a2, rwkv_g2, rwkv_kk, rwkv_ka, rwkv_rk, rwkv_gn_g, rwkv_gn_b, rwkv_v0, rwkv_v1, rwkv_v2, p_nsa, p_rwkv, w_out, ln1_g, ln1_b, xq_w, xk_w, xv_w, xo_w, ln2_g, ln2_b, moe_w_gate, moe_w_up, moe_w_down, ln3_g, ln3_b):
    bsz, t, d = x.shape
    n = bsz * t
    depth = w_in.shape[0]
    nc = t // CMP_STRIDE
    nsel = t // SEL_BLOCK
    g, hd = NSA_GROUPS, HEAD_DIM

    pc_tab, wb_tab, lt_tab, cq_tab = _bias_tables(rel_table, t)
    cmp_start = np.arange(nc) * CMP_STRIDE
    sel_start = np.arange(nsel) * SEL_BLOCK
    overlap = ((cmp_start[:, None] <= sel_start[None, :] + SEL_BLOCK - 1)
               & (cmp_start[:, None] + CMP_LEN - 1 >= sel_start[None, :]) & (cmp_start[:, None] < (nc - 1) * CMP_STRIDE))
    overlap = jnp.asarray(overlap.T, BF16)
    lane_head = np.arange(RWKV_WIDTH) // RWKV_HEAD
    hsum = jnp.asarray(lane_head[:, None] == lane_head[None, :], BF16)
    quad_head = np.arange(QUAD_LANES) // RWKV_HEAD
    ones_bd = jnp.asarray(quad_head[:, None] == quad_head[None, :], BF16)
    step = np.arange(SCAN_TT)[None, :, None]
    lane = np.arange(LANE)[None, None, :]
    sbh = np.arange(SCAN_TT // SUB_STEPS * QUAD)[:, None, None]
    place = jnp.asarray((lane // SUB_STEPS == sbh % QUAD) & (step == (sbh // QUAD) * SUB_STEPS + lane % SUB_STEPS), BF16)
    rw_f = _pad_to(router_w.astype(F32), (d, LANE))
    rw_hi = rw_f.astype(BF16)
    rw2 = jnp.stack([rw_hi, (rw_f - rw_hi.astype(F32)).astype(BF16)])

    xf = x.reshape(n, d)
    mem_f = mem.reshape(bsz * mem.shape[1], d)
    v_first = None
    for l in range(depth):
        w_in_l = _permute_in_proj(w_in[l])
        z = _matmul(xf, w_in_l, F32, 1024, 1024)

        kv0 = NSA_WIDTH
        zeros64 = jnp.zeros((d, LANE - hd), BF16)
        w_kv = jnp.stack([jnp.concatenate(
            [piece for i in range(KV_ARRAYS)
             for piece in (w_in[l][:, kv0 + i * NSA_KV + gi * hd:kv0 + i * NSA_KV + (gi + 1) * hd].astype(BF16), zeros64)],
            axis=1) for gi in range(g)])
        kc, vc, ks_aug, vs_aug, kw_pad, vw_aug = _kv_proj(xf, w_kv, bsz, t)

        def cmp_w(pe, w1, w2):
            return (pe.reshape(2, CMP_STRIDE * hd), w1.reshape(2, CMP_STRIDE * hd, CMP_HIDDEN).astype(BF16),
                    _pad_to(w2, (CMP_HIDDEN, LANE)).astype(BF16))

        chunks = lambda a: a.reshape(bsz, g, nc, CMP_STRIDE * hd)
        k_cmp = _compress(chunks(kc), *cmp_w(cmp_pe_k[l], cmp_w1k[l], cmp_w2k[l]))
        v_cmp = _compress(chunks(vc), *cmp_w(cmp_pe_v[l], cmp_w1v[l], cmp_w2v[l]))
        o_c, sel_mask = _cmp_branch(z, k_cmp, v_cmp, pc_tab, overlap, bsz, t)
        o_s = _sel_branch(z, ks_aug, vs_aug, sel_mask, lt_tab, cq_tab, bsz, t)
        o_w = _win_branch(z, kw_pad, vw_aug, wb_tab, bsz, t)

        mu = rwkv_mu[l]
        w3 = 3 * RWKV_WIDTH
        mu5 = jnp.stack([mu[0:512], mu[512:1024], mu[1024:1536], _pad_to(mu[w3:w3 + 128], (512,)),
                         _pad_to(mu[w3 + 128:w3 + 256], (512,))])
        vecs = jnp.stack([rwkv_w0[l], rwkv_a0[l], rwkv_kk[l], rwkv_ka[l], rwkv_rk[l].reshape(-1)])
        w2p = jnp.concatenate([rwkv_w2[l], jnp.zeros_like(rwkv_a2[l])], axis=0).astype(BF16)
        a2p = jnp.concatenate([jnp.zeros_like(rwkv_w2[l]), rwkv_a2[l]], axis=0).astype(BF16)
        vres = None
        if l > 0:
            vres = (v_first, rwkv_v0[l - 1][None, :], _pad_to(rwkv_v1[l - 1], (RWKV_WIDTH, LANE)).astype(BF16),
                    _pad_to(rwkv_v2[l - 1], (LANE, RWKV_WIDTH)).astype(BF16))
        r, w, k, v, kk, b, g_out, bonus, vt = _rwkv_prep(z, mu5, vecs, w2p, a2p, rwkv_g2[l].astype(BF16), hsum, vres,
                                                          bsz, t)
        if l == 0:
            v_first = v
        seq = lambda a: a.reshape(bsz, t, RWKV_WIDTH)
        o_rw = _rwkv_scan(seq(r), seq(w), seq(k), seq(kk), seq(b), vt, ones_bd, place, bsz, t)
        o_rw = o_rw.transpose(1, 0, 2, 3).reshape(n, RWKV_WIDTH)

        xf = _merge(xf, o_c, o_s, o_w, o_rw, bonus, g_out, z, jnp.stack([rwkv_gn_g[l], rwkv_gn_b[l]]), hsum,
                    p_nsa[l].astype(BF16), p_rwkv[l].astype(BF16), w_out[l].astype(BF16),
                    jnp.stack([ln1_g[l], ln1_b[l]]), bsz, t)

        mlen = mem.shape[1]
        mk = _matmul(mem_f, xk_w[l].astype(BF16), BF16, mlen, XATTN_WIDTH)
        mv = _matmul(mem_f, xv_w[l].astype(BF16), BF16, mlen, XATTN_WIDTH)
        kt = mk.reshape(bsz, mlen, XATTN_WIDTH).transpose(0, 2, 1)
        xf, scores = _xattn(xf, kt, mv.reshape(bsz, mlen, XATTN_WIDTH), xq_w[l].astype(BF16), xo_w[l].astype(BF16),
                            jnp.stack([ln2_g[l], ln2_b[l]]), rw2, bsz, t)

        gate, dest, tok_of_row, block_expert = _route(scores, router_bias, n)
        rows = jnp.take(xf.astype(BF16), tok_of_row, axis=0)
        y_rows = _experts(block_expert, rows, moe_w_gate[l], moe_w_up[l], moe_w_down[l])
        xf = _combine(xf, jnp.take(y_rows, dest[:, 0], axis=0), jnp.take(y_rows, dest[:, 1], axis=0), gate,
                      jnp.stack([ln3_g[l], ln3_b[l]]))
    return xf.reshape(bsz, t, d)
```

```python
import functools
import math

import numpy as np
import jax
import jax.numpy as jnp
from jax import lax
from jax.experimental import pallas as pl
from jax.experimental.pallas import tpu as pltpu

F32 = jnp.float32
BF16 = jnp.bfloat16

D_MODEL = 1024
DEPTH = 2
NSA_HEADS = 8
NSA_GROUPS = 2
NSA_HPG = NSA_HEADS // NSA_GROUPS
HEAD_DIM = 64
NSA_WIDTH = NSA_HEADS * HEAD_DIM
NSA_KV = NSA_GROUPS * HEAD_DIM
CMP_STRIDE = 16
CMP_LEN = 2 * CMP_STRIDE
CMP_HIDDEN = 256
SEL_BLOCK = 64
SEL_TOPN = 16
WINDOW = 512
Q_BLOCK = 128
SEL_FORCE = 1e9
RWKV_HEADS = 8
RWKV_HEAD = 64
RWKV_WIDTH = RWKV_HEADS * RWKV_HEAD
LORA_W = 64
LORA_A = 64
LORA_V = 32
LORA_G = 128
GN_EPS = 64e-5
REL_BUCKETS = 32
REL_MAX_DIST = 1024
XATTN_HEADS = 4
XATTN_HEAD = 128
XATTN_WIDTH = XATTN_HEADS * XATTN_HEAD
N_EXPERTS = 16
N_EXPERT_GROUPS = 4
EXPERTS_PER_GROUP = N_EXPERTS // N_EXPERT_GROUPS
TOP_K = 2
EXPERT_FF = 512
DN_ALPHA = (2 * DEPTH) ** 0.25
LN_EPS = 1e-5
NEG_INF = -1e30
NSA_COLS = NSA_WIDTH + 6 * NSA_KV + 3 * NSA_HEADS
RWKV_COLS = 3 * RWKV_WIDTH + LORA_W + LORA_A + LORA_G

LANE = 128
VMEM_LIMIT = 56 * 1024 * 1024

C_Q = 0
C_GN = 1024
C_GR = 2048
C_R = 3072
C_K = 3584
C_V = 4096
C_G3 = 4608
C_WA = 4864
C_ZG = 4992
IN_PAD = 5120
KV_ARRAYS = 6

MOE_BLK = 256
SCAN_TT = 128
FAR_BIAS_DIST = 1280


def _cparams(sem):
    return pltpu.CompilerParams(dimension_semantics=sem, vmem_limit_bytes=VMEM_LIMIT)


def _sigmoid(x):
    return 1.0 / (1.0 + jnp.exp(-x))


def _layer_norm(v, g, b):
    mu = jnp.mean(v, axis=-1, keepdims=True)
    d = v - mu
    var = jnp.mean(d * d, axis=-1, keepdims=True)
    return d * lax.rsqrt(var + LN_EPS) * g + b


def _dot(a, b):
    return jnp.dot(a, b, preferred_element_type=F32)


def _dot_nt(a, b):
    return lax.dot_general(a, b, (((1,), (1,)), ((), ())), preferred_element_type=F32)


def _dot_split(x, w):
    hi = x.astype(BF16)
    lo = (x - hi.astype(F32)).astype(BF16)
    return _dot(hi, w) + _dot(lo, w)


def _mm_kernel(x_ref, w_ref, o_ref, xb_ref):
    @pl.when(pl.program_id(1) == 0)
    def _():
        xb_ref[...] = x_ref[...].astype(BF16)

    o_ref[...] = _dot(xb_ref[...], w_ref[...]).astype(o_ref.dtype)


def _matmul(x, w, out_dtype, tm, tn):
    n, k = x.shape
    m = w.shape[1]
    return pl.pallas_call(
        _mm_kernel,
        grid=(n // tm, m // tn),
        in_specs=[pl.BlockSpec((tm, k), lambda i, j: (i, 0)),
                  pl.BlockSpec((k, tn), lambda i, j: (0, j))],
        out_specs=pl.BlockSpec((tm, tn), lambda i, j: (i, j)),
        out_shape=jax.ShapeDtypeStruct((n, m), out_dtype),
        scratch_shapes=[pltpu.VMEM((tm, k), BF16)],
        compiler_params=_cparams(("parallel", "arbitrary")),
        name="matmul",
    )(x, w)


def _kv_proj_kernel(tiles_per_seq, x_ref, w_ref, kc_ref, vc_ref, ks_ref, vs_ref, kw_ref, vw_ref, xb_ref):
    @pl.when(pl.program_id(1) == 0)
    def _():
        xb_ref[...] = x_ref[...].astype(BF16)

    y = _dot(xb_ref[...], w_ref[pl.program_id(1)])
    tm = y.shape[0]
    slot = lambda i: y[:, i * LANE:(i + 1) * LANE]
    lane = lax.broadcasted_iota(jnp.int32, (tm, LANE), 1)
    one_lane = jnp.where(lane == HEAD_DIM, 1.0, 0.0)
    t0 = lax.rem(pl.program_id(0), tiles_per_seq) * tm
    block = jnp.right_shift(t0 + lax.broadcasted_iota(jnp.int32, (tm, LANE), 0), int(math.log2(SEL_BLOCK)))
    kc_ref[0, 0] = slot(0)[:, :HEAD_DIM]
    vc_ref[0, 0] = slot(1)[:, :HEAD_DIM]
    ks_ref[0, 0] = jnp.concatenate([slot(2) + jnp.where(lane == HEAD_DIM + 1, 1.0, one_lane),
                                    jnp.where(lane == block, 1.0, 0.0)], axis=1).astype(BF16)
    vs_ref[0, 0] = (slot(3) + one_lane).astype(BF16)
    kw_ref[0, 0] = slot(4).astype(BF16)
    vw_ref[0, 0] = (slot(5) + one_lane).astype(BF16)


def _kv_proj(x, w_kv, bsz, t, tm=512):
    n, d = x.shape
    tps = t // tm
    g = NSA_GROUPS

    def out(width, dtype):
        return (pl.BlockSpec((1, 1, tm, width), lambda i, j: (i // tps, j, i % tps, 0)),
                jax.ShapeDtypeStruct((bsz, g, t, width), dtype))

    outs = [out(HEAD_DIM, F32), out(HEAD_DIM, F32), out(2 * LANE, BF16), out(LANE, BF16), out(LANE, BF16),
            out(LANE, BF16)]
    return pl.pallas_call(
        functools.partial(_kv_proj_kernel, tps),
        grid=(n // tm, g),
        in_specs=[pl.BlockSpec((tm, d), lambda i, j: (i, 0)),
                  pl.BlockSpec((g, d, KV_ARRAYS * LANE), lambda i, j: (0, 0, 0))],
        out_specs=[o[0] for o in outs],
        out_shape=[o[1] for o in outs],
        scratch_shapes=[pltpu.VMEM((tm, d), BF16)],
        compiler_params=_cparams(("parallel", "arbitrary")),
        name="nsa_kv_proj",
    )(x, w_kv)


def _compress_kernel(u_ref, pe_ref, w1_ref, w2_ref, o_ref):
    u = u_ref[0, 0]
    a = _dot((u + pe_ref[0:1, :]).astype(BF16), w1_ref[0])
    b = _dot((u + pe_ref[1:2, :]).astype(BF16), w1_ref[1])
    nc = u.shape[0]
    h = a + pltpu.roll(b, nc - 1, 0)
    h = 0.5 * h * (1.0 + jnp.tanh(math.sqrt(2.0 / math.pi) * (h + 0.044715 * (h * h * h))))
    y = _dot(h.astype(BF16), w2_ref[...])
    row = lax.broadcasted_iota(jnp.int32, y.shape, 0)
    o_ref[0, 0] = jnp.where(row < nc - 1, y, 0.0)


def _compress(u, pe2, w1, w2p):
    b, g, nc, _ = u.shape
    return pl.pallas_call(
        _compress_kernel,
        grid=(b, g),
        in_specs=[pl.BlockSpec((1, 1, nc, 1024), lambda i, j: (i, j, 0, 0)),
                  pl.BlockSpec((2, 1024), lambda i, j: (0, 0)),
                  pl.BlockSpec((2, 1024, CMP_HIDDEN), lambda i, j: (0, 0, 0)),
                  pl.BlockSpec((CMP_HIDDEN, LANE), lambda i, j: (0, 0))],
        out_specs=pl.BlockSpec((1, 1, nc, LANE), lambda i, j: (i, j, 0, 0)),
        out_shape=jax.ShapeDtypeStruct((b, g, nc, LANE), F32),
        compiler_params=_cparams(("parallel", "parallel")),
        name="nsa_compress",
    )(u, pe2, w1, w2p)


def _stack_heads(q_ref, row0=0):
    q = q_ref[row0:row0 + Q_BLOCK, :] * (HEAD_DIM ** -0.5)
    return jnp.concatenate([q[:, h * LANE:(h + 1) * LANE] for h in range(NSA_HPG)], axis=0).astype(BF16)


def _store_heads(o_ref, o, gates, branch, row0=0):
    lane = lax.broadcasted_iota(jnp.int32, (Q_BLOCK, LANE), 1)
    for pair in range(NSA_HPG // 2):
        halves = []
        for h in (2 * pair, 2 * pair + 1):
            c = branch * NSA_HPG + h
            halves.append(o[h * Q_BLOCK:(h + 1) * Q_BLOCK, :] * gates[:, c:c + 1])
        o_ref[0, pair, row0:row0 + Q_BLOCK, :] = jnp.where(lane < HEAD_DIM, halves[0],
                                                           pltpu.roll(halves[1], HEAD_DIM, 1)).astype(o_ref.dtype)


CMP_QB = 2


def _cmp_kernel(q_ref, zg_ref, kc_ref, vc_ref, pc0_ref, pc1_ref, ov_ref, o_ref, m_ref, *scratch):
    nc = kc_ref.shape[2]
    nsel = ov_ref.shape[0]
    kc = kc_ref[0, 0].astype(BF16)
    vc = vc_ref[0, 0].astype(BF16)
    gates = _sigmoid(zg_ref[...])
    blocks = [(qb, pl.program_id(2) * CMP_QB + qb, pc_ref, scratch[2 * qb], scratch[2 * qb + 1])
              for qb, pc_ref in enumerate((pc0_ref, pc1_ref))]
    for qb, c, _, p_s, _ in blocks:
        p_s[...] = _dot_nt(_stack_heads(q_ref, qb * Q_BLOCK), kc)
    for qb, c, pc_ref, p_s, pb_s in blocks:
        off = pl.multiple_of(lax.div((nc - 8) - 8 * c, LANE) * LANE, LANE)
        for r in range(NSA_HPG * Q_BLOCK // SEL_ROWS):
            rows = slice(r * SEL_ROWS, (r + 1) * SEL_ROWS)
            s = p_s[rows, :] + pc_ref[0, 0, rows, pl.ds(off, nc)]
            e = jnp.exp(s - jnp.max(s, axis=-1, keepdims=True))
            l = jnp.sum(e, axis=-1, keepdims=True)
            ql = (lax.broadcasted_iota(jnp.int32, (SEL_ROWS, 1), 0) + r * SEL_ROWS) & (Q_BLOCK - 1)
            p = e * jnp.where((c * Q_BLOCK + ql) >= (CMP_LEN - 1), 1.0 / l, 0.0)
            p_s[rows, :] = p
            pb_s[rows, :] = p.astype(BF16)
    vals = []
    jb = lax.broadcasted_iota(jnp.int32, (nsel, Q_BLOCK), 0)
    jbf = jb.astype(F32)
    for qb, c, _, p_s, pb_s in blocks:
        _store_heads(o_ref, _dot(pb_s[...], vc), gates[qb * Q_BLOCK:(qb + 1) * Q_BLOCK, :], 0, qb * Q_BLOCK)
        psum = ((p_s[0:Q_BLOCK, :] + p_s[Q_BLOCK:2 * Q_BLOCK, :])
                + (p_s[2 * Q_BLOCK:3 * Q_BLOCK, :] + p_s[3 * Q_BLOCK:4 * Q_BLOCK, :]))
        hi = psum.astype(BF16)
        lo = (psum - hi.astype(F32)).astype(BF16)
        imp = _dot_nt(ov_ref[...], hi) + _dot_nt(ov_ref[...], lo)
        tq = c * Q_BLOCK + lax.broadcasted_iota(jnp.int32, (nsel, Q_BLOCK), 1)
        jq = jnp.right_shift(tq, int(math.log2(SEL_BLOCK)))
        forced = (jb == 0) | (jb == jq) | (jb == jq - 1)
        vals.append(jnp.where(forced, SEL_FORCE, jnp.where(jb <= jq, imp, -SEL_FORCE)))
    sels = [jnp.zeros((nsel, Q_BLOCK), F32) for _ in blocks]
    for _ in range(SEL_TOPN):
        for qb in range(CMP_QB):
            mx = jnp.max(vals[qb], axis=0, keepdims=True)
            first = jnp.min(jnp.where(vals[qb] == mx, jbf, float(nsel)), axis=0, keepdims=True)
            hit = jbf == first
            sels[qb] = jnp.where(hit, jnp.where(mx >= 0.0, 1.0, 0.0), sels[qb])
            vals[qb] = jnp.where(hit, -3e38, vals[qb])
    for qb in range(CMP_QB):
        m_ref[0, 0, qb * Q_BLOCK:(qb + 1) * Q_BLOCK, :] = (sels[qb].T - 1.0) * (-NEG_INF)


def _cmp_branch(z, k_cmp, v_cmp, pc_tab, overlap, bsz, t):
    tq = CMP_QB * Q_BLOCK
    nq = t // tq
    nc = k_cmp.shape[2]
    nsel = overlap.shape[0]
    rows = NSA_HPG * Q_BLOCK

    def variant(qb):
        return pl.BlockSpec((1, 1, rows, 2 * nc - LANE),
                            lambda b, g, c: (g, lax.rem((nc - 8) - 8 * (CMP_QB * c + qb), LANE) // 8, 0, 0))

    return pl.pallas_call(
        _cmp_kernel,
        grid=(bsz, NSA_GROUPS, nq),
        in_specs=[pl.BlockSpec((tq, NSA_HPG * LANE), lambda b, g, c: (b * nq + c, g)),
                  pl.BlockSpec((tq, LANE), lambda b, g, c: (b * nq + c, C_G3 // LANE + g)),
                  pl.BlockSpec((1, 1, nc, LANE), lambda b, g, c: (b, g, 0, 0)),
                  pl.BlockSpec((1, 1, nc, LANE), lambda b, g, c: (b, g, 0, 0)),
                  variant(0), variant(1),
                  pl.BlockSpec((nsel, nc), lambda b, g, c: (0, 0))],
        out_specs=[pl.BlockSpec((1, NSA_HPG // 2, tq, LANE), lambda b, g, c: (b, g, c, 0)),
                   pl.BlockSpec((1, 1, tq, nsel), lambda b, g, c: (b, g, c, 0))],
        out_shape=[jax.ShapeDtypeStruct((bsz, NSA_HEADS // 2, t, LANE), BF16),
                   jax.ShapeDtypeStruct((bsz, NSA_GROUPS, t, nsel), F32)],
        scratch_shapes=[pltpu.VMEM((rows, nc), F32), pltpu.VMEM((rows, nc), BF16)] * CMP_QB,
        compiler_params=_cparams(("parallel", "parallel", "arbitrary")),
        name="nsa_cmp_select",
    )(z, z, k_cmp, v_cmp, pc_tab, pc_tab, overlap)


WIN_QB = 2


def _win_kernel(q_ref, zg_ref, k_ref, v_ref, wb_ref, o_ref, *scratch):
    nvar = wb_ref.shape[1]
    gates = _sigmoid(zg_ref[...])
    blocks = []
    for qb in range(WIN_QB):
        c = pl.program_id(2) * WIN_QB + qb
        start = pl.multiple_of(jnp.maximum(c * Q_BLOCK - WINDOW, 0), Q_BLOCK)
        blocks.append((qb, jnp.minimum(c, nvar - 1), pl.ds(start, WINDOW + Q_BLOCK), scratch[2 * qb], scratch[2 * qb + 1]))
    for qb, _, keys, s_s, _ in blocks:
        s_s[...] = _dot_nt(_stack_heads(q_ref, qb * Q_BLOCK), k_ref[0, 0, keys, :])
    for qb, var, _, s_s, p_s in blocks:
        for r in range(NSA_HPG * Q_BLOCK // SEL_ROWS):
            rows = slice(r * SEL_ROWS, (r + 1) * SEL_ROWS)
            s = s_s[rows, :] + wb_ref[0, var, rows, :]
            p_s[rows, :] = jnp.exp(s - jnp.max(s, axis=-1, keepdims=True)).astype(BF16)
    for qb, _, keys, _, p_s in blocks:
        o = _dot(p_s[...], v_ref[0, 0, keys, :])
        o = o * (1.0 / o[:, HEAD_DIM:HEAD_DIM + 1])
        _store_heads(o_ref, o, gates[qb * Q_BLOCK:(qb + 1) * Q_BLOCK, :], 2, qb * Q_BLOCK)


def _win_branch(z, kw, vw, wb_tab, bsz, t):
    tq = WIN_QB * Q_BLOCK
    nq = t // tq
    rows, width = NSA_HPG * Q_BLOCK, WINDOW + Q_BLOCK
    return pl.pallas_call(
        _win_kernel,
        grid=(bsz, NSA_GROUPS, nq),
        in_specs=[pl.BlockSpec((tq, NSA_HPG * LANE), lambda b, g, c: (b * nq + c, g)),
                  pl.BlockSpec((tq, LANE), lambda b, g, c: (b * nq + c, C_G3 // LANE + g)),
                  pl.BlockSpec((1, 1, t, LANE), lambda b, g, c: (b, g, 0, 0)),
                  pl.BlockSpec((1, 1, t, LANE), lambda b, g, c: (b, g, 0, 0)),
                  pl.BlockSpec((1,) + wb_tab.shape[1:], lambda b, g, c: (g, 0, 0, 0))],
        out_specs=pl.BlockSpec((1, NSA_HPG // 2, tq, LANE), lambda b, g, c: (b, g, c, 0)),
        out_shape=jax.ShapeDtypeStruct((bsz, NSA_HEADS // 2, t, LANE), BF16),
        scratch_shapes=[pltpu.VMEM((rows, width), F32), pltpu.VMEM((rows, width), BF16)] * WIN_QB,
        compiler_params=_cparams(("parallel", "parallel", "arbitrary")),
        name="nsa_window",
    )(z, z, kw, vw, wb_tab)


SEL_TK = 512


SEL_ROWS = 64
SEL_NEAR = FAR_BIAS_DIST + Q_BLOCK


def _sel_kernel(q_ref, zg_ref, k_ref, v_ref, mt_ref, lt_ref, cq_ref, o_ref, m_s, acc_s, s0_s, s1_s, p0_s, p1_s,
                a0_s, a1_s):
    c = pl.program_id(2)
    q = q_ref[...] * (HEAD_DIM ** -0.5)
    qs = jnp.concatenate([q[:, h * LANE:(h + 1) * LANE] for h in range(NSA_HPG)], axis=0) + cq_ref[0]
    mt = mt_ref[0, 0]
    q_aug = jnp.concatenate([qs, jnp.concatenate([mt] * NSA_HPG, axis=0)], axis=1).astype(BF16)
    m_s[...] = jnp.full(m_s.shape, NEG_INF, F32)
    acc_s[...] = jnp.zeros(acc_s.shape, F32)
    ncol = SEL_TK // LANE
    n_tiles = lax.div(c, SEL_TK // Q_BLOCK) + 1

    def key_rows(i):
        return pl.ds(pl.multiple_of(jnp.minimum(i, n_tiles - 1) * SEL_TK, SEL_TK), SEL_TK)

    def scores(i, dst):
        dst[...] = _dot_nt(q_aug, k_ref[0, 0, key_rows(i), :])

    def tile(i, cur, nxt, p_buf, a_buf):
        scores(i + 1, nxt)
        off = jnp.where(i < n_tiles, jnp.maximum(SEL_NEAR + SEL_TK - (c * Q_BLOCK - i * SEL_TK), 0),
                        SEL_NEAR + SEL_TK + Q_BLOCK)
        off = pl.multiple_of(off, LANE)
        for r in range(NSA_HPG * Q_BLOCK // SEL_ROWS):
            rows = slice(r * SEL_ROWS, (r + 1) * SEL_ROWS)
            s = cur[rows, :] + lt_ref[0, rows, pl.ds(off, SEL_TK)]
            cols = [s[:, j * LANE:(j + 1) * LANE] for j in range(ncol)]
            mx = functools.reduce(jnp.maximum, cols)
            m_old = m_s[rows, :]
            m_new = jnp.maximum(m_old, jnp.max(mx, axis=-1, keepdims=True))
            m_s[rows, :] = m_new
            a_buf[rows, :] = jnp.exp(m_old - m_new)
            p_buf[rows, :] = jnp.concatenate([jnp.exp(col - m_new) for col in cols], axis=1).astype(BF16)
        acc_s[...] = a_buf[...] * acc_s[...] + _dot(p_buf[...], v_ref[0, 0, key_rows(i), :])

    scores(0, s0_s)

    def pair(j, carry):
        tile(2 * j, s0_s, s1_s, p0_s, a0_s)
        tile(2 * j + 1, s1_s, s0_s, p1_s, a1_s)
        return carry

    lax.fori_loop(0, lax.div(n_tiles + 1, 2), pair, 0)
    acc = acc_s[...]
    o = acc * (1.0 / acc[:, HEAD_DIM:HEAD_DIM + 1])
    _store_heads(o_ref, o, _sigmoid(zg_ref[...]), 1)


def _sel_branch(z, ks_aug, vs, mt, lt_tab, cq_tab, bsz, t):
    nq = t // Q_BLOCK
    nsel = mt.shape[-1]
    rows = NSA_HPG * Q_BLOCK
    return pl.pallas_call(
        _sel_kernel,
        grid=(bsz, NSA_GROUPS, nq),
        in_specs=[pl.BlockSpec((Q_BLOCK, NSA_HPG * LANE), lambda b, g, c: (b * nq + c, g)),
                  pl.BlockSpec((Q_BLOCK, LANE), lambda b, g, c: (b * nq + c, C_G3 // LANE + g)),
                  pl.BlockSpec((1, 1, t, 2 * LANE), lambda b, g, c: (b, g, 0, 0)),
                  pl.BlockSpec((1, 1, t, LANE), lambda b, g, c: (b, g, 0, 0)),
                  pl.BlockSpec((1, 1, Q_BLOCK, nsel), lambda b, g, c: (b, g, c, 0)),
                  pl.BlockSpec((1, rows, lt_tab.shape[2]), lambda b, g, c: (g, 0, 0)),
                  pl.BlockSpec((1, rows, LANE), lambda b, g, c: (g, 0, 0))],
        out_specs=pl.BlockSpec((1, NSA_HPG // 2, Q_BLOCK, LANE), lambda b, g, c: (b, g, c, 0)),
        out_shape=jax.ShapeDtypeStruct((bsz, NSA_HEADS // 2, t, LANE), BF16),
        scratch_shapes=[pltpu.VMEM((rows, LANE), F32),
                        pltpu.VMEM((rows, LANE), F32),
                        pltpu.VMEM((rows, SEL_TK), F32),
                        pltpu.VMEM((rows, SEL_TK), F32),
                        pltpu.VMEM((rows, SEL_TK), BF16),
                        pltpu.VMEM((rows, SEL_TK), BF16),
                        pltpu.VMEM((rows, LANE), F32),
                        pltpu.VMEM((rows, LANE), F32)],
        compiler_params=_cparams(("parallel", "parallel", "arbitrary")),
        name="nsa_selected",
    )(z, z, ks_aug, vs, mt, lt_tab, cq_tab)


def _shift_mix(z_ref, prev_ref, mu, first):
    z = z_ref[...]
    prev = jnp.where(first, 0.0, prev_ref[7:8, :])
    row = lax.broadcasted_iota(jnp.int32, z.shape, 0)
    zp = jnp.where(row == 0, prev, pltpu.roll(z, 1, 0))
    return z + mu * (zp - z)


def _rwkv_prep_kernel(has_vres, tiles_per_seq, *refs):
    (zr, zk, zv, zwa, zg, pr, pk, pv, pwa, pg, mu_ref, vec_ref, w2_ref, a2_ref, g2_ref, hsum_ref) = refs[:16]
    pos = 16
    if has_vres:
        vf_ref, v0_ref, v1_ref, v2_ref = refs[pos:pos + 4]
        pos += 4
    r_o, w_o, k_o, v_o, kk_o, b_o, g_o, bonus_o, vt_o = refs[pos:]
    first = lax.rem(pl.program_id(0), tiles_per_seq) == 0
    r = _shift_mix(zr, pr, mu_ref[0:1, :], first)
    k = _shift_mix(zk, pk, mu_ref[1:2, :], first)
    v = _shift_mix(zv, pv, mu_ref[2:3, :], first)
    wa = _shift_mix(zwa, pwa, mu_ref[3:4, 0:LANE], first)
    zg_s = _shift_mix(zg, pg, mu_ref[4:5, 0:LANE], first)
    w0, a0, k_k, k_a, r_k = (vec_ref[i:i + 1, :] for i in range(5))
    if has_vres:
        lora = _dot(_dot(v.astype(BF16), v1_ref[...]).astype(BF16), v2_ref[...])
        v = v + (vf_ref[...] - v) * _sigmoid(v0_ref[...] + lora)
    u = w0 + _dot(jnp.tanh(wa).astype(BF16), w2_ref[...])
    decay = jnp.exp(-math.exp(-0.5) * _sigmoid(u))
    a = _sigmoid(a0 + _dot(wa.astype(BF16), a2_ref[...]))
    g = _dot(_sigmoid(zg_s).astype(BF16), g2_ref[...])
    kk = k * k_k
    kk = kk / jnp.maximum(jnp.sqrt(_dot_split(kk * kk, hsum_ref[...])), 1e-12)
    k = k * (1.0 + (a - 1.0) * k_a)
    r_o[...] = r
    w_o[...] = decay
    k_o[...] = k.astype(k_o.dtype)
    v_o[...] = v
    kk_o[...] = kk
    b_o[...] = kk * a
    g_o[...] = g.astype(g_o.dtype)
    bonus_o[...] = (_dot_split(r * k * r_k, hsum_ref[...]) * v).astype(bonus_o.dtype)
    vt_o[0] = v.T.astype(BF16)


def _rwkv_prep(z, mu5, vecs, w2p, a2p, g2, hsum, vres, bsz, t, tm=256):
    n = bsz * t
    nt = n // tm
    tps = t // tm
    w = RWKV_WIDTH

    def cur(width, col):
        return pl.BlockSpec((tm, width), lambda i: (i, col // width))

    def prev(width, col):
        return pl.BlockSpec((8, width), lambda i: (jnp.maximum(i * (tm // 8) - 1, 0), col // width))

    def full(shape):
        return pl.BlockSpec(shape, lambda i: (0,) * len(shape))

    cols = [(w, C_R), (w, C_K), (w, C_V), (LANE, C_WA), (LANE, C_ZG)]
    in_specs = [cur(*c) for c in cols] + [prev(*c) for c in cols]
    in_specs += [full(mu5.shape), full(vecs.shape), full(w2p.shape), full(a2p.shape), full(g2.shape), full(hsum.shape)]
    args = [z] * 10 + [mu5, vecs, w2p, a2p, g2, hsum]
    if vres is not None:
        v_first, v0, v1p, v2p = vres
        in_specs += [pl.BlockSpec((tm, w), lambda i: (i, 0)), full(v0.shape), full(v1p.shape), full(v2p.shape)]
        args += [v_first, v0, v1p, v2p]
    tok = pl.BlockSpec((tm, w), lambda i: (i, 0))
    out_specs = [tok] * 8 + [pl.BlockSpec((1, w, tm), lambda i: (i // tps, 0, i % tps))]
    tok_dtypes = (F32, F32, BF16, F32, F32, F32, BF16, BF16)
    out_shape = [jax.ShapeDtypeStruct((n, w), dt) for dt in tok_dtypes] + [jax.ShapeDtypeStruct((bsz, w, t), BF16)]
    return pl.pallas_call(
        functools.partial(_rwkv_prep_kernel, vres is not None, tps),
        grid=(nt,),
        in_specs=in_specs,
        out_specs=out_specs,
        out_shape=out_shape,
        compiler_params=_cparams(("parallel",)),
        name="rwkv_prep",
    )(*args)


QUAD = 4
QUAD_LANES = QUAD * RWKV_HEAD
SUB_STEPS = LANE // QUAD


def _scan_kernel(r_ref, w_ref, k_ref, kk_ref, b_ref, vt_ref, ones_ref, place_ref, o_ref, s_ref, vq_ref, kq_ref, vk_ref):
    bsz = r_ref.shape[0]
    tt = r_ref.shape[1]
    nh = RWKV_HEAD
    quads = [(bi, hf) for bi in range(bsz) for hf in range(RWKV_HEADS // QUAD)]

    @pl.when(pl.program_id(0) == 0)
    def _():
        s_ref[...] = jnp.zeros(s_ref.shape, F32)

    sub_shift, head_shift = int(math.log2(SUB_STEPS)), int(math.log2(nh))
    sub_lane = lax.broadcasted_iota(jnp.int32, (nh, LANE), 1) & (SUB_STEPS - 1)
    k_row_head = jnp.right_shift(lax.broadcasted_iota(jnp.int32, (LANE, QUAD_LANES), 0), sub_shift)
    k_lane_head = jnp.right_shift(lax.broadcasted_iota(jnp.int32, (LANE, QUAD_LANES), 1), head_shift)
    r_mask = (lax.broadcasted_iota(jnp.int32, (16, QUAD_LANES), 0)
              == jnp.right_shift(lax.broadcasted_iota(jnp.int32, (16, QUAD_LANES), 1), head_shift))
    ones_bd = ones_ref[...]

    def sub_block(sb, carry):
        s0 = pl.multiple_of(sb * SUB_STEPS, SUB_STEPS)
        for q, (bi, hf) in enumerate(quads):
            ln = slice(hf * QUAD_LANES, (hf + 1) * QUAD_LANES)
            vq = _dot(vt_ref[bi, (QUAD * hf) * nh:(QUAD * hf + 1) * nh, :], place_ref[sb * QUAD])
            for h in range(1, QUAD):
                vq = vq + _dot(vt_ref[bi, (QUAD * hf + h) * nh:(QUAD * hf + h + 1) * nh, :], place_ref[sb * QUAD + h])
            vq_ref[q] = vq.astype(BF16)
            k_sub = k_ref[bi, pl.ds(s0, SUB_STEPS), ln].astype(F32)
            kq_ref[q] = jnp.where(k_row_head == k_lane_head, jnp.concatenate([k_sub] * QUAD, axis=0), 0.0).astype(BF16)

        def group(tg, carry):
            t0 = pl.multiple_of(s0 + tg * 8, 8)
            rows = []
            for bi, hf in quads:
                ln = slice(hf * QUAD_LANES, (hf + 1) * QUAD_LANES)
                rows.append(tuple(ref[bi, pl.ds(t0, 8), ln] for ref in (w_ref, kk_ref, b_ref, r_ref)))
            onehots = [jnp.where(sub_lane == tg * 8 + u, 1.0, 0.0).astype(BF16) for u in range(8)]
            for q in range(len(quads)):
                vq = vq_ref[q]
                vk_ref[q] = _dot(jnp.concatenate([vq * oh for oh in onehots], axis=0), kq_ref[q])
            state = [s_ref[q] for q in range(len(quads))]
            for u in range(8):
                sk = _dot(jnp.concatenate([(state[q] * rows[q][1][u:u + 1, :]).astype(BF16)
                                           for q in range(len(quads))], axis=0), ones_bd)
                for q, (bi, hf) in enumerate(quads):
                    w8, _, b8, r8 = rows[q]
                    state[q] = (state[q] * w8[u:u + 1, :] - sk[q * nh:(q + 1) * nh, :] * b8[u:u + 1, :]
                                + vk_ref[q, u * nh:(u + 1) * nh, :])
                    r_lhs = jnp.where(r_mask, r8[u:u + 1, :], 0.0).astype(BF16)
                    o = _dot_nt(r_lhs, state[q].astype(BF16))
                    o_ref[t0 + u, bi, QUAD * hf:QUAD * (hf + 1), :] = o[0:QUAD, :]
            for q in range(len(quads)):
                s_ref[q] = state[q]
            return carry

        return lax.fori_loop(0, SUB_STEPS // 8, group, carry)

    lax.fori_loop(0, tt // SUB_STEPS, sub_block, 0)


def _rwkv_scan(r, w, k, kk, b, vt, ones_bd, place, bsz, t):
    tt = SCAN_TT
    wd = RWKV_WIDTH
    nquad = bsz * RWKV_HEADS // QUAD
    tok = pl.BlockSpec((bsz, tt, wd), lambda i: (0, i, 0))
    return pl.pallas_call(
        _scan_kernel,
        grid=(t // tt,),
        in_specs=[tok, tok, tok, tok, tok,
                  pl.BlockSpec((bsz, wd, tt), lambda i: (0, 0, i)),
                  pl.BlockSpec(ones_bd.shape, lambda i: (0, 0)),
                  pl.BlockSpec(place.shape, lambda i: (0, 0, 0))],
        out_specs=pl.BlockSpec((tt, bsz, RWKV_HEADS, RWKV_HEAD), lambda i: (i, 0, 0, 0)),
        out_shape=jax.ShapeDtypeStruct((t, bsz, RWKV_HEADS, RWKV_HEAD), F32),
        scratch_shapes=[pltpu.VMEM((nquad, RWKV_HEAD, QUAD_LANES), F32),
                        pltpu.VMEM((nquad, RWKV_HEAD, LANE), BF16),
                        pltpu.VMEM((nquad, LANE, QUAD_LANES), BF16),
                        pltpu.VMEM((nquad, 8 * RWKV_HEAD, QUAD_LANES), F32)],
        compiler_params=_cparams(("arbitrary",)),
        name="rwkv_scan",
    )(r, w, k, kk, b, vt, ones_bd, place)


def _merge_kernel(x_ref, oc_ref, os_ref, ow_ref, orw_ref, bonus_ref, g_ref, zgn_ref, zgr_ref, gn_ref, hsum_ref,
                  pn_ref, pr_ref, wo_ref, ln_ref, o_ref):
    o_n = jnp.concatenate([oc_ref[0, p].astype(F32) + os_ref[0, p].astype(F32) + ow_ref[0, p].astype(F32)
                           for p in range(NSA_HEADS // 2)], axis=1)
    y_n = _dot(o_n.astype(BF16), pn_ref[...])
    o = orw_ref[...]
    mu = _dot_split(o, hsum_ref[...]) * (1.0 / RWKV_HEAD)
    d = o - mu
    var = _dot_split(d * d, hsum_ref[...]) * (1.0 / RWKV_HEAD)
    o = d * lax.rsqrt(var + GN_EPS) * gn_ref[0:1, :] + gn_ref[1:2, :]
    o = ((o + bonus_ref[...]) * g_ref[...]).astype(BF16)
    y1 = _sigmoid(zgn_ref[...]) * y_n + _sigmoid(zgr_ref[...]) * _dot(o, pr_ref[...])
    y = _dot(y1.astype(BF16), wo_ref[...])
    o_ref[...] = _layer_norm(DN_ALPHA * x_ref[...] + y, ln_ref[0:1, :], ln_ref[1:2, :])


def _merge(x, o_c, o_s, o_w, o_rw, bonus, g, z, gn, hsum, p_nsa, p_rwkv, w_out, ln, bsz, t, tm=256):
    n = bsz * t
    tps = t // tm
    d = D_MODEL
    w = RWKV_WIDTH
    head = pl.BlockSpec((1, NSA_HEADS // 2, tm, LANE), lambda i: (i // tps, 0, i % tps, 0))
    tok = pl.BlockSpec((tm, w), lambda i: (i, 0))

    def full(a):
        return pl.BlockSpec(a.shape, lambda i: (0,) * a.ndim)

    return pl.pallas_call(
        _merge_kernel,
        grid=(n // tm,),
        in_specs=[pl.BlockSpec((tm, d), lambda i: (i, 0)), head, head, head, tok, tok, tok,
                  pl.BlockSpec((tm, d), lambda i: (i, C_GN // d)),
                  pl.BlockSpec((tm, d), lambda i: (i, C_GR // d)),
                  full(gn), full(hsum), full(p_nsa), full(p_rwkv), full(w_out), full(ln)],
        out_specs=pl.BlockSpec((tm, d), lambda i: (i, 0)),
        out_shape=jax.ShapeDtypeStruct((n, d), F32),
        compiler_params=_cparams(("parallel",)),
        name="mix_merge_ln",
    )(x, o_c, o_s, o_w, o_rw, bonus, g, z, z, gn, hsum, p_nsa, p_rwkv, w_out, ln)


def _xattn_kernel(x_ref, kt_ref, v_ref, wq_ref, wo_ref, ln_ref, rw_ref, o_ref, s_ref):
    x = x_ref[...]
    q = _dot(x.astype(BF16), wq_ref[...]).astype(BF16)
    outs = []
    for h in range(XATTN_HEADS):
        hs = slice(h * XATTN_HEAD, (h + 1) * XATTN_HEAD)
        s = _dot(q[:, hs], kt_ref[0, hs, :]) * (XATTN_HEAD ** -0.5)
        m = jnp.max(s, axis=-1, keepdims=True)
        p = jnp.exp(s - m)
        l = jnp.sum(p, axis=-1, keepdims=True)
        outs.append(_dot(p.astype(BF16), v_ref[0, :, hs]) * (1.0 / l))
    o = jnp.concatenate(outs, axis=1).astype(BF16)
    x2 = _layer_norm(DN_ALPHA * x + _dot(o, wo_ref[...]), ln_ref[0:1, :], ln_ref[1:2, :])
    o_ref[...] = x2
    hi = x2.astype(BF16)
    lo = (x2 - hi.astype(F32)).astype(BF16)
    logits = _dot(hi, rw_ref[0]) + (_dot(lo, rw_ref[0]) + _dot(hi, rw_ref[1]))
    s_ref[...] = _sigmoid(logits)


def _xattn(x, kt, v, wq, wo, ln, rw2, bsz, t, tm=256):
    n = bsz * t
    tps = t // tm
    d = D_MODEL

    def full(a):
        return pl.BlockSpec(a.shape, lambda i: (0,) * a.ndim)

    return pl.pallas_call(
        _xattn_kernel,
        grid=(n // tm,),
        in_specs=[pl.BlockSpec((tm, d), lambda i: (i, 0)),
                  pl.BlockSpec((1,) + kt.shape[1:], lambda i: (i // tps, 0, 0)),
                  pl.BlockSpec((1,) + v.shape[1:], lambda i: (i // tps, 0, 0)),
                  full(wq), full(wo), full(ln), full(rw2)],
        out_specs=[pl.BlockSpec((tm, d), lambda i: (i, 0)), pl.BlockSpec((tm, LANE), lambda i: (i, 0))],
        out_shape=[jax.ShapeDtypeStruct((n, d), F32), jax.ShapeDtypeStruct((n, LANE), F32)],
        compiler_params=_cparams(("parallel",)),
        name="xattn_ln_router",
    )(x, kt, v, wq, wo, ln, rw2)


def _expert_kernel(meta_ref, x_ref, gate_ref, *refs):
    w_refs, o_ref, w_scr = refs[:6], refs[6], refs[7:]
    i = pl.program_id(0)
    nb = pl.num_programs(0)
    prev = jnp.maximum(i - 1, 0)

    for side in range(2):
        @pl.when((i == 0) | (meta_ref[side * nb + i] != meta_ref[side * nb + prev]))
        def _():
            for j in range(3):
                w_scr[3 * side + j][...] = w_refs[3 * side + j][0].astype(BF16)

    n_used = meta_ref[2 * nb]

    @pl.when(i < n_used)
    def _():
        x = x_ref[...]
        y = jnp.zeros(o_ref.shape, F32)
        for side in range(2):
            wg, wu, wd = w_scr[3 * side:3 * side + 3]
            hg = _dot(x, wg[...])
            h = hg * _sigmoid(hg) * _dot(x, wu[...])
            y = y + gate_ref[:, side:side + 1] * _dot(h.astype(BF16), wd[...])
        o_ref[...] = y.astype(o_ref.dtype)

    @pl.when(i >= n_used)
    def _():
        o_ref[...] = jnp.zeros(o_ref.shape, o_ref.dtype)


def _experts(meta, rows, gates, w_gate, w_up, w_down):
    n_rows, d = rows.shape
    ff = w_gate.shape[2]
    nb = n_rows // MOE_BLK

    def weight(shape, side):
        return pl.BlockSpec((1,) + shape, lambda i, m: (m[side * nb + i], 0, 0))

    return pl.pallas_call(
        _expert_kernel,
        grid_spec=pltpu.PrefetchScalarGridSpec(
            num_scalar_prefetch=1,
            grid=(nb,),
            in_specs=[pl.BlockSpec((MOE_BLK, d), lambda i, m: (i, 0)),
                      pl.BlockSpec((MOE_BLK, TOP_K), lambda i, m: (i, 0))]
                     + [weight(s, side) for side in range(2) for s in ((d, ff), (d, ff), (ff, d))],
            out_specs=pl.BlockSpec((MOE_BLK, d), lambda i, m: (i, 0)),
            scratch_shapes=[pltpu.VMEM(s, BF16) for _ in range(2) for s in ((d, ff), (d, ff), (ff, d))]),
        out_shape=jax.ShapeDtypeStruct((n_rows, d), BF16),
        compiler_params=_cparams(("arbitrary",)),
        name="moe_experts",
    )(meta, rows, gates, w_gate, w_up, w_down, w_gate, w_up, w_down)


def _combine_kernel(x_ref, y_ref, ln_ref, o_ref):
    o_ref[...] = _layer_norm(DN_ALPHA * x_ref[...] + y_ref[...], ln_ref[0:1, :], ln_ref[1:2, :])


def _combine(x, y, ln, tm=512):
    n, d = x.shape
    tok = pl.BlockSpec((tm, d), lambda i: (i, 0))
    return pl.pallas_call(
        _combine_kernel,
        grid=(n // tm,),
        in_specs=[tok, tok, pl.BlockSpec(ln.shape, lambda i: (0, 0))],
        out_specs=tok,
        out_shape=jax.ShapeDtypeStruct((n, d), F32),
        compiler_params=_cparams(("parallel",)),
        name="moe_combine_ln",
    )(x, y, ln)


def _t5_bucket(dist):
    n = jnp.maximum(dist, 0)
    max_exact = REL_BUCKETS // 2
    log_ratio = jnp.log(jnp.maximum(n, 1).astype(F32) / max_exact) / math.log(REL_MAX_DIST / max_exact)
    large = jnp.minimum(max_exact + (log_ratio * (REL_BUCKETS - max_exact)).astype(jnp.int32), REL_BUCKETS - 1)
    return jnp.where(n < max_exact, n, large)


def _bias_tables(rel_table, t):
    nc = t // CMP_STRIDE
    f = rel_table[_t5_bucket(jnp.arange(t + 2048))].astype(F32).T
    ql = jnp.arange(Q_BLOCK)[:, None]

    def lookup(dist, ok):
        vals = jnp.take(f, jnp.clip(dist, 0, f.shape[-1] - 1), axis=1)
        vals = jnp.where(ok[None], vals, NEG_INF)
        return vals.reshape(NSA_GROUPS, NSA_HPG * dist.shape[0], dist.shape[1])

    j = jnp.arange(2 * nc)[None, :]
    dist_c = CMP_STRIDE * (nc - 8 - j) + ql - (CMP_LEN - 1)
    pc = lookup(dist_c, dist_c >= 0)
    pc = jnp.stack([pc[:, :, s:s + 2 * nc - LANE] for s in range(0, LANE, 8)], axis=1)
    nvar = WINDOW // Q_BLOCK + 1

    def toeplitz(base, width, hi, shift):
        period = width + Q_BLOCK
        m = np.arange(period)
        dist = base - np.where(m < width, m, m - period)
        row = jnp.take(f, np.clip(dist, 0, f.shape[-1] - 1), axis=1) - shift
        row = jnp.where(((dist >= 0) & (dist < hi))[None], row, NEG_INF)
        flat = jnp.tile(row, (1, Q_BLOCK))[:, :Q_BLOCK * (period - 1)]
        return flat.reshape(NSA_HEADS, Q_BLOCK, period - 1)[:, :, :width]

    wb = jnp.stack([toeplitz(Q_BLOCK * v, WINDOW + Q_BLOCK, WINDOW, 0.0) for v in range(nvar)], axis=1)
    wb = wb.reshape(NSA_GROUPS, NSA_HPG, nvar, Q_BLOCK, -1).transpose(0, 2, 1, 3, 4)
    wb = wb.reshape(NSA_GROUPS, nvar, NSA_HPG * Q_BLOCK, -1)
    far = rel_table[REL_BUCKETS - 1].astype(F32)
    lt = toeplitz(SEL_NEAR + SEL_TK, SEL_NEAR + 2 * SEL_TK + Q_BLOCK, f.shape[-1], far[:, None])
    lt = lt.reshape(NSA_GROUPS, NSA_HPG * Q_BLOCK, -1)
    far_hi = far.astype(BF16).astype(F32)
    lane = jnp.arange(LANE)[None, :]
    cq = jnp.where(lane == HEAD_DIM, far_hi[:, None], jnp.where(lane == HEAD_DIM + 1, (far - far_hi)[:, None], 0.0))
    cq = jnp.broadcast_to(cq[:, None, :], (NSA_HEADS, Q_BLOCK, LANE)).reshape(NSA_GROUPS, NSA_HPG * Q_BLOCK, LANE)
    return pc, wb, lt, cq


def _in_proj_perm():
    perm = np.full((IN_PAD,), -1, np.int64)
    for h in range(NSA_HEADS):
        perm[C_Q + h * LANE:C_Q + h * LANE + HEAD_DIM] = np.arange(HEAD_DIM) + h * HEAD_DIM
    rw0 = NSA_COLS
    g0 = NSA_COLS + RWKV_COLS
    perm[C_GN:C_GN + D_MODEL] = g0 + np.arange(D_MODEL)
    perm[C_GR:C_GR + D_MODEL] = g0 + D_MODEL + np.arange(D_MODEL)
    perm[C_R:C_R + RWKV_WIDTH] = rw0 + np.arange(RWKV_WIDTH)
    perm[C_K:C_K + RWKV_WIDTH] = rw0 + RWKV_WIDTH + np.arange(RWKV_WIDTH)
    perm[C_V:C_V + RWKV_WIDTH] = rw0 + 2 * RWKV_WIDTH + np.arange(RWKV_WIDTH)
    for g in range(NSA_GROUPS):
        for br in range(3):
            for h in range(NSA_HPG):
                perm[C_G3 + g * LANE + br * NSA_HPG + h] = NSA_WIDTH + 6 * NSA_KV + (g * NSA_HPG + h) * 3 + br
    perm[C_WA:C_WA + LORA_W + LORA_A] = rw0 + 3 * RWKV_WIDTH + np.arange(LORA_W + LORA_A)
    perm[C_ZG:C_ZG + LORA_G] = rw0 + 3 * RWKV_WIDTH + LORA_W + LORA_A + np.arange(LORA_G)
    return perm


_PERM = _in_proj_perm()


def _permute_in_proj(w):
    pieces = []
    start = 0
    while start < IN_PAD:
        stop = start + 1
        if _PERM[start] < 0:
            while stop < IN_PAD and _PERM[stop] < 0:
                stop += 1
            pieces.append(jnp.zeros((w.shape[0], stop - start), BF16))
        else:
            while stop < IN_PAD and _PERM[stop] == _PERM[stop - 1] + 1:
                stop += 1
            pieces.append(w[:, int(_PERM[start]):int(_PERM[stop - 1]) + 1].astype(BF16))
        start = stop
    return jnp.concatenate(pieces, axis=1)


def _pad_to(a, shape):
    return jnp.pad(a, [(0, s - d) for d, s in zip(a.shape, shape)])


def _pair_classes():
    table = np.zeros((N_EXPERTS, N_EXPERTS), np.int32)
    pairs = []
    for grp in range(N_EXPERT_GROUPS):
        for i in range(EXPERTS_PER_GROUP):
            for j in range(i + 1, EXPERTS_PER_GROUP):
                table[grp * EXPERTS_PER_GROUP + i, grp * EXPERTS_PER_GROUP + j] = len(pairs)
                pairs.append((grp * EXPERTS_PER_GROUP + i, grp * EXPERTS_PER_GROUP + j))
    return table, np.asarray(pairs, np.int32)


_PAIR_CLASS, _CLASS_EXPERTS = _pair_classes()
N_PAIR_CLASSES = len(_CLASS_EXPERTS)


def _route(s, router_bias, n_tok):
    s16 = s[:, :N_EXPERTS]
    s_sel = s16 + router_bias.astype(F32)

    def top2(a):
        i1 = jnp.argmax(a, axis=-1)
        rest = jnp.where(jnp.arange(a.shape[-1]) == i1[..., None], -jnp.inf, a)
        i2 = jnp.argmax(rest, axis=-1)
        return jnp.max(a, axis=-1) + jnp.max(rest, axis=-1), jnp.stack([i1, i2], axis=-1)

    group_score, _ = top2(s_sel.reshape(n_tok, N_EXPERT_GROUPS, EXPERTS_PER_GROUP))
    group = jnp.argmax(group_score, axis=-1)
    in_group = (jnp.arange(N_EXPERTS) // EXPERTS_PER_GROUP)[None, :] == group[:, None]
    _, e_idx = top2(jnp.where(in_group, s_sel, NEG_INF))
    s_top = jnp.take_along_axis(s16, e_idx, axis=-1)
    gate = s_top / jnp.sum(s_top, axis=-1, keepdims=True)
    swap = e_idx[:, 0] > e_idx[:, 1]
    e_lo = jnp.where(swap, e_idx[:, 1], e_idx[:, 0])
    e_hi = jnp.where(swap, e_idx[:, 0], e_idx[:, 1])
    gate_lohi = jnp.where(swap[:, None], gate[:, ::-1], gate)
    cls = jnp.asarray(_PAIR_CLASS)[e_lo, e_hi]
    onehot = (cls[:, None] == jnp.arange(N_PAIR_CLASSES)[None, :]).astype(jnp.int32)
    rank = jnp.take_along_axis(jnp.cumsum(onehot, axis=0) - onehot, cls[:, None], axis=1)[:, 0]
    counts = jnp.sum(onehot, axis=0)
    padded = (counts + MOE_BLK - 1) // MOE_BLK * MOE_BLK
    ends = jnp.cumsum(padded)
    dest = (ends - padded)[cls] + rank
    n_rows = n_tok + N_PAIR_CLASSES * MOE_BLK
    tok_of_row = jnp.zeros((n_rows,), jnp.int32).at[dest].set(jnp.arange(n_tok, dtype=jnp.int32))
    block_cls = jnp.minimum(jnp.searchsorted(ends, jnp.arange(n_rows // MOE_BLK) * MOE_BLK, side='right'),
                            N_PAIR_CLASSES - 1)
    meta = jnp.concatenate([jnp.asarray(_CLASS_EXPERTS[:, 0])[block_cls], jnp.asarray(_CLASS_EXPERTS[:, 1])[block_cls],
                            ends[-1:] // MOE_BLK]).astype(jnp.int32)
    return gate_lohi, dest, tok_of_row, meta


def kernel(x, mem, rel_table, router_w, router_bias, w_in, cmp_pe_k, cmp_pe_v, cmp_w1k, cmp_w2k, cmp_w1v, cmp_w2v, rwkv_mu, rwkv_w0, rwkv_w2, rwkv_a0, rwkv_a2, rwkv_g2, rwkv_kk, rwkv_ka, rwkv_rk, rwkv_gn_g, rwkv_gn_b, rwkv_v0, rwkv_v1, rwkv_v2, p_nsa, p_rwkv, w_out, ln1_g, ln1_b, xq_w, xk_w, xv_w, xo_w, ln2_g, ln2_b, moe_w_gate, moe_w_up, moe_w_down, ln3_g, ln3_b):
    bsz, t, d = x.shape
    n = bsz * t
    depth = w_in.shape[0]
    nc = t // CMP_STRIDE
    nsel = t // SEL_BLOCK
    g, hd = NSA_GROUPS, HEAD_DIM

    pc_tab, wb_tab, lt_tab, cq_tab = _bias_tables(rel_table, t)
    cmp_start = np.arange(nc) * CMP_STRIDE
    sel_start = np.arange(nsel) * SEL_BLOCK
    overlap = ((cmp_start[:, None] <= sel_start[None, :] + SEL_BLOCK - 1)
               & (cmp_start[:, None] + CMP_LEN - 1 >= sel_start[None, :]) & (cmp_start[:, None] < (nc - 1) * CMP_STRIDE))
    overlap = jnp.asarray(overlap.T, BF16)
    lane_head = np.arange(RWKV_WIDTH) // RWKV_HEAD
    hsum = jnp.asarray(lane_head[:, None] == lane_head[None, :], BF16)
    quad_head = np.arange(QUAD_LANES) // RWKV_HEAD
    ones_bd = jnp.asarray(quad_head[:, None] == quad_head[None, :], BF16)
    step = np.arange(SCAN_TT)[None, :, None]
    lane = np.arange(LANE)[None, None, :]
    sbh = np.arange(SCAN_TT // SUB_STEPS * QUAD)[:, None, None]
    place = jnp.asarray((lane // SUB_STEPS == sbh % QUAD) & (step == (sbh // QUAD) * SUB_STEPS + lane % SUB_STEPS), BF16)
    rw_f = _pad_to(router_w.astype(F32), (d, LANE))
    rw_hi = rw_f.astype(BF16)
    rw2 = jnp.stack([rw_hi, (rw_f - rw_hi.astype(F32)).astype(BF16)])

    xf = x.reshape(n, d)
    mem_f = mem.reshape(bsz * mem.shape[1], d)
    v_first = None
    for l in range(depth):
        w_in_l = _permute_in_proj(w_in[l])
        z = _matmul(xf, w_in_l, F32, 1024, 1024)

        kv0 = NSA_WIDTH
        zeros64 = jnp.zeros((d, LANE - hd), BF16)
        w_kv = jnp.stack([jnp.concatenate(
            [piece for i in range(KV_ARRAYS)
             for piece in (w_in[l][:, kv0 + i * NSA_KV + gi * hd:kv0 + i * NSA_KV + (gi + 1) * hd].astype(BF16), zeros64)],
            axis=1) for gi in range(g)])
        kc, vc, ks_aug, vs_aug, kw_pad, vw_aug = _kv_proj(xf, w_kv, bsz, t)

        def cmp_w(pe, w1, w2):
            return (pe.reshape(2, CMP_STRIDE * hd), w1.reshape(2, CMP_STRIDE * hd, CMP_HIDDEN).astype(BF16),
                    _pad_to(w2, (CMP_HIDDEN, LANE)).astype(BF16))

        chunks = lambda a: a.reshape(bsz, g, nc, CMP_STRIDE * hd)
        k_cmp = _compress(chunks(kc), *cmp_w(cmp_pe_k[l], cmp_w1k[l], cmp_w2k[l]))
        v_cmp = _compress(chunks(vc), *cmp_w(cmp_pe_v[l], cmp_w1v[l], cmp_w2v[l]))
        o_c, sel_mask = _cmp_branch(z, k_cmp, v_cmp, pc_tab, overlap, bsz, t)
        o_s = _sel_branch(z, ks_aug, vs_aug, sel_mask, lt_tab, cq_tab, bsz, t)
        o_w = _win_branch(z, kw_pad, vw_aug, wb_tab, bsz, t)

        mu = rwkv_mu[l]
        w3 = 3 * RWKV_WIDTH
        mu5 = jnp.stack([mu[0:512], mu[512:1024], mu[1024:1536], _pad_to(mu[w3:w3 + 128], (512,)),
                         _pad_to(mu[w3 + 128:w3 + 256], (512,))])
        vecs = jnp.stack([rwkv_w0[l], rwkv_a0[l], rwkv_kk[l], rwkv_ka[l], rwkv_rk[l].reshape(-1)])
        w2p = jnp.concatenate([rwkv_w2[l], jnp.zeros_like(rwkv_a2[l])], axis=0).astype(BF16)
        a2p = jnp.concatenate([jnp.zeros_like(rwkv_w2[l]), rwkv_a2[l]], axis=0).astype(BF16)
        vres = None
        if l > 0:
            vres = (v_first, rwkv_v0[l - 1][None, :], _pad_to(rwkv_v1[l - 1], (RWKV_WIDTH, LANE)).astype(BF16),
                    _pad_to(rwkv_v2[l - 1], (LANE, RWKV_WIDTH)).astype(BF16))
        r, w, k, v, kk, b, g_out, bonus, vt = _rwkv_prep(z, mu5, vecs, w2p, a2p, rwkv_g2[l].astype(BF16), hsum, vres,
                                                          bsz, t)
        if l == 0:
            v_first = v
        seq = lambda a: a.reshape(bsz, t, RWKV_WIDTH)
        o_rw = _rwkv_scan(seq(r), seq(w), seq(k), seq(kk), seq(b), vt, ones_bd, place, bsz, t)
        o_rw = o_rw.transpose(1, 0, 2, 3).reshape(n, RWKV_WIDTH)

        xf = _merge(xf, o_c, o_s, o_w, o_rw, bonus, g_out, z, jnp.stack([rwkv_gn_g[l], rwkv_gn_b[l]]), hsum,
                    p_nsa[l].astype(BF16), p_rwkv[l].astype(BF16), w_out[l].astype(BF16),
                    jnp.stack([ln1_g[l], ln1_b[l]]), bsz, t)

        mlen = mem.shape[1]
        mk = _matmul(mem_f, xk_w[l].astype(BF16), BF16, mlen, XATTN_WIDTH)
        mv = _matmul(mem_f, xv_w[l].astype(BF16), BF16, mlen, XATTN_WIDTH)
        kt = mk.reshape(bsz, mlen, XATTN_WIDTH).transpose(0, 2, 1)
        xf, scores = _xattn(xf, kt, mv.reshape(bsz, mlen, XATTN_WIDTH), xq_w[l].astype(BF16), xo_w[l].astype(BF16),
                            jnp.stack([ln2_g[l], ln2_b[l]]), rw2, bsz, t)

        gate, dest, tok_of_row, meta = _route(scores, router_bias, n)
        rows = jnp.take(xf.astype(BF16), tok_of_row, axis=0)
        y_rows = _experts(meta, rows, jnp.take(gate, tok_of_row, axis=0), moe_w_gate[l], moe_w_up[l], moe_w_down[l])
        xf = _combine(xf, jnp.take(y_rows, dest, axis=0), jnp.stack([ln3_g[l], ln3_b[l]]))
    return xf.reshape(bsz, t, d)
```

```python
import functools
import math

import numpy as np
import jax
import jax.numpy as jnp
from jax import lax
from jax.experimental import pallas as pl
from jax.experimental.pallas import tpu as pltpu

F32 = jnp.float32
BF16 = jnp.bfloat16

D_MODEL = 1024
DEPTH = 2
NSA_HEADS = 8
NSA_GROUPS = 2
NSA_HPG = NSA_HEADS // NSA_GROUPS
HEAD_DIM = 64
NSA_WIDTH = NSA_HEADS * HEAD_DIM
NSA_KV = NSA_GROUPS * HEAD_DIM
CMP_STRIDE = 16
CMP_LEN = 2 * CMP_STRIDE
CMP_HIDDEN = 256
SEL_BLOCK = 64
SEL_TOPN = 16
WINDOW = 512
Q_BLOCK = 128
SEL_FORCE = 1e9
RWKV_HEADS = 8
RWKV_HEAD = 64
RWKV_WIDTH = RWKV_HEADS * RWKV_HEAD
LORA_W = 64
LORA_A = 64
LORA_V = 32
LORA_G = 128
GN_EPS = 64e-5
REL_BUCKETS = 32
REL_MAX_DIST = 1024
XATTN_HEADS = 4
XATTN_HEAD = 128
XATTN_WIDTH = XATTN_HEADS * XATTN_HEAD
N_EXPERTS = 16
N_EXPERT_GROUPS = 4
EXPERTS_PER_GROUP = N_EXPERTS // N_EXPERT_GROUPS
TOP_K = 2
EXPERT_FF = 512
DN_ALPHA = (2 * DEPTH) ** 0.25
LN_EPS = 1e-5
NEG_INF = -1e30
NSA_COLS = NSA_WIDTH + 6 * NSA_KV + 3 * NSA_HEADS
RWKV_COLS = 3 * RWKV_WIDTH + LORA_W + LORA_A + LORA_G

LANE = 128
VMEM_LIMIT = 56 * 1024 * 1024

C_Q = 0
C_GN = 1024
C_GR = 2048
C_R = 3072
C_K = 3584
C_V = 4096
C_G3 = 4608
C_WA = 4864
C_ZG = 4992
IN_PAD = 5120
KV_ARRAYS = 6

MOE_BLK = 256
SCAN_TT = 128
FAR_BIAS_DIST = 1280


def _cparams(sem):
    return pltpu.CompilerParams(dimension_semantics=sem, vmem_limit_bytes=VMEM_LIMIT)


def _sigmoid(x):
    return 1.0 / (1.0 + jnp.exp(-x))


def _layer_norm(v, g, b):
    mu = jnp.mean(v, axis=-1, keepdims=True)
    d = v - mu
    var = jnp.mean(d * d, axis=-1, keepdims=True)
    return d * lax.rsqrt(var + LN_EPS) * g + b


def _dot(a, b):
    return jnp.dot(a, b, preferred_element_type=F32)


def _dot_nt(a, b):
    return lax.dot_general(a, b, (((1,), (1,)), ((), ())), preferred_element_type=F32)


def _dot_split(x, w):
    hi = x.astype(BF16)
    lo = (x - hi.astype(F32)).astype(BF16)
    return _dot(hi, w) + _dot(lo, w)


def _mm_kernel(x_ref, w_ref, o_ref, xb_ref):
    @pl.when(pl.program_id(1) == 0)
    def _():
        xb_ref[...] = x_ref[...].astype(BF16)

    o_ref[...] = _dot(xb_ref[...], w_ref[...]).astype(o_ref.dtype)


def _matmul(x, w, out_dtype, tm, tn):
    n, k = x.shape
    m = w.shape[1]
    return pl.pallas_call(
        _mm_kernel,
        grid=(n // tm, m // tn),
        in_specs=[pl.BlockSpec((tm, k), lambda i, j: (i, 0)),
                  pl.BlockSpec((k, tn), lambda i, j: (0, j))],
        out_specs=pl.BlockSpec((tm, tn), lambda i, j: (i, j)),
        out_shape=jax.ShapeDtypeStruct((n, m), out_dtype),
        scratch_shapes=[pltpu.VMEM((tm, k), BF16)],
        compiler_params=_cparams(("parallel", "arbitrary")),
        name="matmul",
    )(x, w)


def _kv_proj_kernel(tiles_per_seq, x_ref, w_ref, kc_ref, vc_ref, ks_ref, vs_ref, kw_ref, vw_ref, xb_ref):
    @pl.when(pl.program_id(1) == 0)
    def _():
        xb_ref[...] = x_ref[...].astype(BF16)

    y = _dot(xb_ref[...], w_ref[pl.program_id(1)])
    tm = y.shape[0]
    slot = lambda i: y[:, i * LANE:(i + 1) * LANE]
    lane = lax.broadcasted_iota(jnp.int32, (tm, LANE), 1)
    one_lane = jnp.where(lane == HEAD_DIM, 1.0, 0.0)
    t0 = lax.rem(pl.program_id(0), tiles_per_seq) * tm
    block = jnp.right_shift(t0 + lax.broadcasted_iota(jnp.int32, (tm, LANE), 0), int(math.log2(SEL_BLOCK)))
    kc_ref[0, 0] = slot(0)[:, :HEAD_DIM]
    vc_ref[0, 0] = slot(1)[:, :HEAD_DIM]
    ks_ref[0, 0] = jnp.concatenate([slot(2) + jnp.where(lane == HEAD_DIM + 1, 1.0, one_lane),
                                    jnp.where(lane == block, 1.0, 0.0)], axis=1).astype(BF16)
    vs_ref[0, 0] = (slot(3) + one_lane).astype(BF16)
    kw_ref[0, 0] = slot(4).astype(BF16)
    vw_ref[0, 0] = (slot(5) + one_lane).astype(BF16)


def _kv_proj(x, w_kv, bsz, t, tm=512):
    n, d = x.shape
    tps = t // tm
    g = NSA_GROUPS

    def out(width, dtype):
        return (pl.BlockSpec((1, 1, tm, width), lambda i, j: (i // tps, j, i % tps, 0)),
                jax.ShapeDtypeStruct((bsz, g, t, width), dtype))

    outs = [out(HEAD_DIM, F32), out(HEAD_DIM, F32), out(2 * LANE, BF16), out(LANE, BF16), out(LANE, BF16),
            out(LANE, BF16)]
    return pl.pallas_call(
        functools.partial(_kv_proj_kernel, tps),
        grid=(n // tm, g),
        in_specs=[pl.BlockSpec((tm, d), lambda i, j: (i, 0)),
                  pl.BlockSpec((g, d, KV_ARRAYS * LANE), lambda i, j: (0, 0, 0))],
        out_specs=[o[0] for o in outs],
        out_shape=[o[1] for o in outs],
        scratch_shapes=[pltpu.VMEM((tm, d), BF16)],
        compiler_params=_cparams(("parallel", "arbitrary")),
        name="nsa_kv_proj",
    )(x, w_kv)


def _compress_kernel(u_ref, pe_ref, w1_ref, w2_ref, o_ref):
    u = u_ref[0, 0]
    a = _dot((u + pe_ref[0:1, :]).astype(BF16), w1_ref[0])
    b = _dot((u + pe_ref[1:2, :]).astype(BF16), w1_ref[1])
    nc = u.shape[0]
    h = a + pltpu.roll(b, nc - 1, 0)
    h = 0.5 * h * (1.0 + jnp.tanh(math.sqrt(2.0 / math.pi) * (h + 0.044715 * (h * h * h))))
    y = _dot(h.astype(BF16), w2_ref[...])
    row = lax.broadcasted_iota(jnp.int32, y.shape, 0)
    o_ref[0, 0] = jnp.where(row < nc - 1, y, 0.0)


def _compress(u, pe2, w1, w2p):
    b, g, nc, _ = u.shape
    return pl.pallas_call(
        _compress_kernel,
        grid=(b, g),
        in_specs=[pl.BlockSpec((1, 1, nc, 1024), lambda i, j: (i, j, 0, 0)),
                  pl.BlockSpec((2, 1024), lambda i, j: (0, 0)),
                  pl.BlockSpec((2, 1024, CMP_HIDDEN), lambda i, j: (0, 0, 0)),
                  pl.BlockSpec((CMP_HIDDEN, LANE), lambda i, j: (0, 0))],
        out_specs=pl.BlockSpec((1, 1, nc, LANE), lambda i, j: (i, j, 0, 0)),
        out_shape=jax.ShapeDtypeStruct((b, g, nc, LANE), F32),
        compiler_params=_cparams(("parallel", "parallel")),
        name="nsa_compress",
    )(u, pe2, w1, w2p)


def _stack_heads(q_ref, row0=0):
    q = q_ref[row0:row0 + Q_BLOCK, :] * (HEAD_DIM ** -0.5)
    return jnp.concatenate([q[:, h * LANE:(h + 1) * LANE] for h in range(NSA_HPG)], axis=0).astype(BF16)


def _store_heads(o_ref, o, gates, branch, row0=0):
    lane = lax.broadcasted_iota(jnp.int32, (Q_BLOCK, LANE), 1)
    for pair in range(NSA_HPG // 2):
        halves = []
        for h in (2 * pair, 2 * pair + 1):
            c = branch * NSA_HPG + h
            halves.append(o[h * Q_BLOCK:(h + 1) * Q_BLOCK, :] * gates[:, c:c + 1])
        o_ref[0, pair, row0:row0 + Q_BLOCK, :] = jnp.where(lane < HEAD_DIM, halves[0],
                                                           pltpu.roll(halves[1], HEAD_DIM, 1)).astype(o_ref.dtype)


CMP_QB = 2


def _cmp_kernel(q_ref, zg_ref, kc_ref, vc_ref, pc0_ref, pc1_ref, ov_ref, o_ref, m_ref, *scratch):
    nc = kc_ref.shape[2]
    nsel = ov_ref.shape[0]
    kc = kc_ref[0, 0].astype(BF16)
    vc = vc_ref[0, 0].astype(BF16)
    gates = _sigmoid(zg_ref[...])
    blocks = [(qb, pl.program_id(2) * CMP_QB + qb, pc_ref, scratch[2 * qb], scratch[2 * qb + 1])
              for qb, pc_ref in enumerate((pc0_ref, pc1_ref))]
    for qb, c, _, p_s, _ in blocks:
        p_s[...] = _dot_nt(_stack_heads(q_ref, qb * Q_BLOCK), kc)
    for qb, c, pc_ref, p_s, pb_s in blocks:
        off = pl.multiple_of(lax.div((nc - 8) - 8 * c, LANE) * LANE, LANE)
        for r in range(NSA_HPG * Q_BLOCK // SEL_ROWS):
            rows = slice(r * SEL_ROWS, (r + 1) * SEL_ROWS)
            s = p_s[rows, :] + pc_ref[0, 0, rows, pl.ds(off, nc)]
            e = jnp.exp(s - jnp.max(s, axis=-1, keepdims=True))
            l = jnp.sum(e, axis=-1, keepdims=True)
            ql = (lax.broadcasted_iota(jnp.int32, (SEL_ROWS, 1), 0) + r * SEL_ROWS) & (Q_BLOCK - 1)
            p = e * jnp.where((c * Q_BLOCK + ql) >= (CMP_LEN - 1), 1.0 / l, 0.0)
            p_s[rows, :] = p
            pb_s[rows, :] = p.astype(BF16)
    vals = []
    jb = lax.broadcasted_iota(jnp.int32, (nsel, Q_BLOCK), 0)
    jbf = jb.astype(F32)
    for qb, c, _, p_s, pb_s in blocks:
        _store_heads(o_ref, _dot(pb_s[...], vc), gates[qb * Q_BLOCK:(qb + 1) * Q_BLOCK, :], 0, qb * Q_BLOCK)
        psum = ((p_s[0:Q_BLOCK, :] + p_s[Q_BLOCK:2 * Q_BLOCK, :])
                + (p_s[2 * Q_BLOCK:3 * Q_BLOCK, :] + p_s[3 * Q_BLOCK:4 * Q_BLOCK, :]))
        hi = psum.astype(BF16)
        lo = (psum - hi.astype(F32)).astype(BF16)
        imp = _dot_nt(ov_ref[...], hi) + _dot_nt(ov_ref[...], lo)
        tq = c * Q_BLOCK + lax.broadcasted_iota(jnp.int32, (nsel, Q_BLOCK), 1)
        jq = jnp.right_shift(tq, int(math.log2(SEL_BLOCK)))
        forced = (jb == 0) | (jb == jq) | (jb == jq - 1)
        vals.append(jnp.where(forced, SEL_FORCE, jnp.where(jb <= jq, imp, -SEL_FORCE)))
    sels = [jnp.zeros((nsel, Q_BLOCK), F32) for _ in blocks]
    for _ in range(SEL_TOPN):
        for qb in range(CMP_QB):
            mx = jnp.max(vals[qb], axis=0, keepdims=True)
            first = jnp.min(jnp.where(vals[qb] == mx, jbf, float(nsel)), axis=0, keepdims=True)
            hit = jbf == first
            sels[qb] = jnp.where(hit, jnp.where(mx >= 0.0, 1.0, 0.0), sels[qb])
            vals[qb] = jnp.where(hit, -3e38, vals[qb])
    for qb in range(CMP_QB):
        m_ref[0, 0, qb * Q_BLOCK:(qb + 1) * Q_BLOCK, :] = (sels[qb].T - 1.0) * (-NEG_INF)


def _cmp_branch(z, k_cmp, v_cmp, pc_tab, overlap, bsz, t):
    tq = CMP_QB * Q_BLOCK
    nq = t // tq
    nc = k_cmp.shape[2]
    nsel = overlap.shape[0]
    rows = NSA_HPG * Q_BLOCK

    def variant(qb):
        return pl.BlockSpec((1, 1, rows, 2 * nc - LANE),
                            lambda b, g, c: (g, lax.rem((nc - 8) - 8 * (CMP_QB * c + qb), LANE) // 8, 0, 0))

    return pl.pallas_call(
        _cmp_kernel,
        grid=(bsz, NSA_GROUPS, nq),
        in_specs=[pl.BlockSpec((tq, NSA_HPG * LANE), lambda b, g, c: (b * nq + c, g)),
                  pl.BlockSpec((tq, LANE), lambda b, g, c: (b * nq + c, C_G3 // LANE + g)),
                  pl.BlockSpec((1, 1, nc, LANE), lambda b, g, c: (b, g, 0, 0)),
                  pl.BlockSpec((1, 1, nc, LANE), lambda b, g, c: (b, g, 0, 0)),
                  variant(0), variant(1),
                  pl.BlockSpec((nsel, nc), lambda b, g, c: (0, 0))],
        out_specs=[pl.BlockSpec((1, NSA_HPG // 2, tq, LANE), lambda b, g, c: (b, g, c, 0)),
                   pl.BlockSpec((1, 1, tq, nsel), lambda b, g, c: (b, g, c, 0))],
        out_shape=[jax.ShapeDtypeStruct((bsz, NSA_HEADS // 2, t, LANE), BF16),
                   jax.ShapeDtypeStruct((bsz, NSA_GROUPS, t, nsel), F32)],
        scratch_shapes=[pltpu.VMEM((rows, nc), F32), pltpu.VMEM((rows, nc), BF16)] * CMP_QB,
        compiler_params=_cparams(("parallel", "parallel", "arbitrary")),
        name="nsa_cmp_select",
    )(z, z, k_cmp, v_cmp, pc_tab, pc_tab, overlap)


WIN_QB = 2


def _win_kernel(q_ref, zg_ref, k_ref, v_ref, wb_ref, o_ref, *scratch):
    nvar = wb_ref.shape[1]
    gates = _sigmoid(zg_ref[...])
    blocks = []
    for qb in range(WIN_QB):
        c = pl.program_id(2) * WIN_QB + qb
        start = pl.multiple_of(jnp.maximum(c * Q_BLOCK - WINDOW, 0), Q_BLOCK)
        blocks.append((qb, jnp.minimum(c, nvar - 1), pl.ds(start, WINDOW + Q_BLOCK), scratch[2 * qb], scratch[2 * qb + 1]))
    for qb, _, keys, s_s, _ in blocks:
        s_s[...] = _dot_nt(_stack_heads(q_ref, qb * Q_BLOCK), k_ref[0, 0, keys, :])
    for qb, var, _, s_s, p_s in blocks:
        for r in range(NSA_HPG * Q_BLOCK // SEL_ROWS):
            rows = slice(r * SEL_ROWS, (r + 1) * SEL_ROWS)
            s = s_s[rows, :] + wb_ref[0, var, rows, :]
            p_s[rows, :] = jnp.exp(s - jnp.max(s, axis=-1, keepdims=True)).astype(BF16)
    for qb, _, keys, _, p_s in blocks:
        o = _dot(p_s[...], v_ref[0, 0, keys, :])
        o = o * (1.0 / o[:, HEAD_DIM:HEAD_DIM + 1])
        _store_heads(o_ref, o, gates[qb * Q_BLOCK:(qb + 1) * Q_BLOCK, :], 2, qb * Q_BLOCK)


def _win_branch(z, kw, vw, wb_tab, bsz, t):
    tq = WIN_QB * Q_BLOCK
    nq = t // tq
    rows, width = NSA_HPG * Q_BLOCK, WINDOW + Q_BLOCK
    return pl.pallas_call(
        _win_kernel,
        grid=(bsz, NSA_GROUPS, nq),
        in_specs=[pl.BlockSpec((tq, NSA_HPG * LANE), lambda b, g, c: (b * nq + c, g)),
                  pl.BlockSpec((tq, LANE), lambda b, g, c: (b * nq + c, C_G3 // LANE + g)),
                  pl.BlockSpec((1, 1, t, LANE), lambda b, g, c: (b, g, 0, 0)),
                  pl.BlockSpec((1, 1, t, LANE), lambda b, g, c: (b, g, 0, 0)),
                  pl.BlockSpec((1,) + wb_tab.shape[1:], lambda b, g, c: (g, 0, 0, 0))],
        out_specs=pl.BlockSpec((1, NSA_HPG // 2, tq, LANE), lambda b, g, c: (b, g, c, 0)),
        out_shape=jax.ShapeDtypeStruct((bsz, NSA_HEADS // 2, t, LANE), BF16),
        scratch_shapes=[pltpu.VMEM((rows, width), F32), pltpu.VMEM((rows, width), BF16)] * WIN_QB,
        compiler_params=_cparams(("parallel", "parallel", "arbitrary")),
        name="nsa_window",
    )(z, z, kw, vw, wb_tab)


SEL_TK = 512


SEL_ROWS = 64
SEL_NEAR = FAR_BIAS_DIST + Q_BLOCK


def _sel_kernel(q_ref, zg_ref, k_ref, v_ref, mt_ref, lt_ref, cq_ref, o_ref, m_s, acc_s, s0_s, s1_s, p0_s, p1_s,
                a0_s, a1_s):
    c = pl.program_id(2)
    q = q_ref[...] * (HEAD_DIM ** -0.5)
    qs = jnp.concatenate([q[:, h * LANE:(h + 1) * LANE] for h in range(NSA_HPG)], axis=0) + cq_ref[0]
    mt = mt_ref[0, 0]
    q_aug = jnp.concatenate([qs, jnp.concatenate([mt] * NSA_HPG, axis=0)], axis=1).astype(BF16)
    m_s[...] = jnp.full(m_s.shape, NEG_INF, F32)
    acc_s[...] = jnp.zeros(acc_s.shape, F32)
    ncol = SEL_TK // LANE
    n_tiles = lax.div(c, SEL_TK // Q_BLOCK) + 1

    def key_rows(i):
        return pl.ds(pl.multiple_of(jnp.minimum(i, n_tiles - 1) * SEL_TK, SEL_TK), SEL_TK)

    def scores(i, dst):
        dst[...] = _dot_nt(q_aug, k_ref[0, 0, key_rows(i), :])

    def tile(i, cur, nxt, p_buf, a_buf):
        scores(i + 1, nxt)
        off = jnp.where(i < n_tiles, jnp.maximum(SEL_NEAR + SEL_TK - (c * Q_BLOCK - i * SEL_TK), 0),
                        SEL_NEAR + SEL_TK + Q_BLOCK)
        off = pl.multiple_of(off, LANE)
        for r in range(NSA_HPG * Q_BLOCK // SEL_ROWS):
            rows = slice(r * SEL_ROWS, (r + 1) * SEL_ROWS)
            s = cur[rows, :] + lt_ref[0, rows, pl.ds(off, SEL_TK)]
            cols = [s[:, j * LANE:(j + 1) * LANE] for j in range(ncol)]
            mx = functools.reduce(jnp.maximum, cols)
            m_old = m_s[rows, :]
            m_new = jnp.maximum(m_old, jnp.max(mx, axis=-1, keepdims=True))
            m_s[rows, :] = m_new
            a_buf[rows, :] = jnp.exp(m_old - m_new)
            p_buf[rows, :] = jnp.concatenate([jnp.exp(col - m_new) for col in cols], axis=1).astype(BF16)
        acc_s[...] = a_buf[...] * acc_s[...] + _dot(p_buf[...], v_ref[0, 0, key_rows(i), :])

    scores(0, s0_s)

    def pair(j, carry):
        tile(2 * j, s0_s, s1_s, p0_s, a0_s)
        tile(2 * j + 1, s1_s, s0_s, p1_s, a1_s)
        return carry

    lax.fori_loop(0, lax.div(n_tiles + 1, 2), pair, 0)
    acc = acc_s[...]
    o = acc * (1.0 / acc[:, HEAD_DIM:HEAD_DIM + 1])
    _store_heads(o_ref, o, _sigmoid(zg_ref[...]), 1)


def _sel_branch(z, ks_aug, vs, mt, lt_tab, cq_tab, bsz, t):
    nq = t // Q_BLOCK
    nsel = mt.shape[-1]
    rows = NSA_HPG * Q_BLOCK
    return pl.pallas_call(
        _sel_kernel,
        grid=(bsz, NSA_GROUPS, nq),
        in_specs=[pl.BlockSpec((Q_BLOCK, NSA_HPG * LANE), lambda b, g, c: (b * nq + c, g)),
                  pl.BlockSpec((Q_BLOCK, LANE), lambda b, g, c: (b * nq + c, C_G3 // LANE + g)),
                  pl.BlockSpec((1, 1, t, 2 * LANE), lambda b, g, c: (b, g, 0, 0)),
                  pl.BlockSpec((1, 1, t, LANE), lambda b, g, c: (b, g, 0, 0)),
                  pl.BlockSpec((1, 1, Q_BLOCK, nsel), lambda b, g, c: (b, g, c, 0)),
                  pl.BlockSpec((1, rows, lt_tab.shape[2]), lambda b, g, c: (g, 0, 0)),
                  pl.BlockSpec((1, rows, LANE), lambda b, g, c: (g, 0, 0))],
        out_specs=pl.BlockSpec((1, NSA_HPG // 2, Q_BLOCK, LANE), lambda b, g, c: (b, g, c, 0)),
        out_shape=jax.ShapeDtypeStruct((bsz, NSA_HEADS // 2, t, LANE), BF16),
        scratch_shapes=[pltpu.VMEM((rows, LANE), F32),
                        pltpu.VMEM((rows, LANE), F32),
                        pltpu.VMEM((rows, SEL_TK), F32),
                        pltpu.VMEM((rows, SEL_TK), F32),
                        pltpu.VMEM((rows, SEL_TK), BF16),
                        pltpu.VMEM((rows, SEL_TK), BF16),
                        pltpu.VMEM((rows, LANE), F32),
                        pltpu.VMEM((rows, LANE), F32)],
        compiler_params=_cparams(("parallel", "parallel", "arbitrary")),
        name="nsa_selected",
    )(z, z, ks_aug, vs, mt, lt_tab, cq_tab)


def _shift_mix(z_ref, prev_ref, mu, first):
    z = z_ref[...]
    prev = jnp.where(first, 0.0, prev_ref[7:8, :])
    row = lax.broadcasted_iota(jnp.int32, z.shape, 0)
    zp = jnp.where(row == 0, prev, pltpu.roll(z, 1, 0))
    return z + mu * (zp - z)


def _rwkv_prep_kernel(has_vres, tiles_per_seq, *refs):
    (zr, zk, zv, zwa, zg, pr, pk, pv, pwa, pg, mu_ref, vec_ref, w2_ref, a2_ref, g2_ref, hsum_ref) = refs[:16]
    pos = 16
    if has_vres:
        vf_ref, v0_ref, v1_ref, v2_ref = refs[pos:pos + 4]
        pos += 4
    r_o, w_o, k_o, v_o, kk_o, b_o, g_o, bonus_o, vt_o = refs[pos:]
    first = lax.rem(pl.program_id(0), tiles_per_seq) == 0
    r = _shift_mix(zr, pr, mu_ref[0:1, :], first)
    k = _shift_mix(zk, pk, mu_ref[1:2, :], first)
    v = _shift_mix(zv, pv, mu_ref[2:3, :], first)
    wa = _shift_mix(zwa, pwa, mu_ref[3:4, 0:LANE], first)
    zg_s = _shift_mix(zg, pg, mu_ref[4:5, 0:LANE], first)
    w0, a0, k_k, k_a, r_k = (vec_ref[i:i + 1, :] for i in range(5))
    if has_vres:
        lora = _dot(_dot(v.astype(BF16), v1_ref[...]).astype(BF16), v2_ref[...])
        v = v + (vf_ref[...] - v) * _sigmoid(v0_ref[...] + lora)
    u = w0 + _dot(jnp.tanh(wa).astype(BF16), w2_ref[...])
    decay = jnp.exp(-math.exp(-0.5) * _sigmoid(u))
    a = _sigmoid(a0 + _dot(wa.astype(BF16), a2_ref[...]))
    g = _dot(_sigmoid(zg_s).astype(BF16), g2_ref[...])
    kk = k * k_k
    kk = kk / jnp.maximum(jnp.sqrt(_dot_split(kk * kk, hsum_ref[...])), 1e-12)
    k = k * (1.0 + (a - 1.0) * k_a)
    r_o[...] = r
    w_o[...] = decay
    k_o[...] = k.astype(k_o.dtype)
    v_o[...] = v
    kk_o[...] = kk
    b_o[...] = kk * a
    g_o[...] = g.astype(g_o.dtype)
    bonus_o[...] = (_dot_split(r * k * r_k, hsum_ref[...]) * v).astype(bonus_o.dtype)
    vt_o[0] = v.T.astype(BF16)


def _rwkv_prep(z, mu5, vecs, w2p, a2p, g2, hsum, vres, bsz, t, tm=256):
    n = bsz * t
    nt = n // tm
    tps = t // tm
    w = RWKV_WIDTH

    def cur(width, col):
        return pl.BlockSpec((tm, width), lambda i: (i, col // width))

    def prev(width, col):
        return pl.BlockSpec((8, width), lambda i: (jnp.maximum(i * (tm // 8) - 1, 0), col // width))

    def full(shape):
        return pl.BlockSpec(shape, lambda i: (0,) * len(shape))

    cols = [(w, C_R), (w, C_K), (w, C_V), (LANE, C_WA), (LANE, C_ZG)]
    in_specs = [cur(*c) for c in cols] + [prev(*c) for c in cols]
    in_specs += [full(mu5.shape), full(vecs.shape), full(w2p.shape), full(a2p.shape), full(g2.shape), full(hsum.shape)]
    args = [z] * 10 + [mu5, vecs, w2p, a2p, g2, hsum]
    if vres is not None:
        v_first, v0, v1p, v2p = vres
        in_specs += [pl.BlockSpec((tm, w), lambda i: (i, 0)), full(v0.shape), full(v1p.shape), full(v2p.shape)]
        args += [v_first, v0, v1p, v2p]
    tok = pl.BlockSpec((tm, w), lambda i: (i, 0))
    out_specs = [tok] * 8 + [pl.BlockSpec((1, w, tm), lambda i: (i // tps, 0, i % tps))]
    tok_dtypes = (F32, F32, BF16, F32, F32, F32, BF16, BF16)
    out_shape = [jax.ShapeDtypeStruct((n, w), dt) for dt in tok_dtypes] + [jax.ShapeDtypeStruct((bsz, w, t), BF16)]
    return pl.pallas_call(
        functools.partial(_rwkv_prep_kernel, vres is not None, tps),
        grid=(nt,),
        in_specs=in_specs,
        out_specs=out_specs,
        out_shape=out_shape,
        compiler_params=_cparams(("parallel",)),
        name="rwkv_prep",
    )(*args)


QUAD = 4
QUAD_LANES = QUAD * RWKV_HEAD
SUB_STEPS = LANE // QUAD


def _scan_kernel(r_ref, w_ref, k_ref, kk_ref, b_ref, vt_ref, ones_ref, place_ref, o_ref, s_ref, vq_ref, kq_ref, vk_ref):
    bsz = r_ref.shape[0]
    tt = r_ref.shape[1]
    nh = RWKV_HEAD
    quads = [(bi, hf) for bi in range(bsz) for hf in range(RWKV_HEADS // QUAD)]

    @pl.when(pl.program_id(0) == 0)
    def _():
        s_ref[...] = jnp.zeros(s_ref.shape, F32)

    sub_shift, head_shift = int(math.log2(SUB_STEPS)), int(math.log2(nh))
    sub_lane = lax.broadcasted_iota(jnp.int32, (nh, LANE), 1) & (SUB_STEPS - 1)
    k_row_head = jnp.right_shift(lax.broadcasted_iota(jnp.int32, (LANE, QUAD_LANES), 0), sub_shift)
    k_lane_head = jnp.right_shift(lax.broadcasted_iota(jnp.int32, (LANE, QUAD_LANES), 1), head_shift)
    r_mask = (lax.broadcasted_iota(jnp.int32, (16, QUAD_LANES), 0)
              == jnp.right_shift(lax.broadcasted_iota(jnp.int32, (16, QUAD_LANES), 1), head_shift))
    ones_bd = ones_ref[...]

    def sub_block(sb, carry):
        s0 = pl.multiple_of(sb * SUB_STEPS, SUB_STEPS)
        for q, (bi, hf) in enumerate(quads):
            ln = slice(hf * QUAD_LANES, (hf + 1) * QUAD_LANES)
            vq = _dot(vt_ref[bi, (QUAD * hf) * nh:(QUAD * hf + 1) * nh, :], place_ref[sb * QUAD])
            for h in range(1, QUAD):
                vq = vq + _dot(vt_ref[bi, (QUAD * hf + h) * nh:(QUAD * hf + h + 1) * nh, :], place_ref[sb * QUAD + h])
            vq_ref[q] = vq.astype(BF16)
            k_sub = k_ref[bi, pl.ds(s0, SUB_STEPS), ln].astype(F32)
            kq_ref[q] = jnp.where(k_row_head == k_lane_head, jnp.concatenate([k_sub] * QUAD, axis=0), 0.0).astype(BF16)

        def group(tg, carry):
            t0 = pl.multiple_of(s0 + tg * 8, 8)
            rows = []
            for bi, hf in quads:
                ln = slice(hf * QUAD_LANES, (hf + 1) * QUAD_LANES)
                rows.append(tuple(ref[bi, pl.ds(t0, 8), ln] for ref in (w_ref, kk_ref, b_ref, r_ref)))
            onehots = [jnp.where(sub_lane == tg * 8 + u, 1.0, 0.0).astype(BF16) for u in range(8)]
            for q in range(len(quads)):
                vq = vq_ref[q]
                vk_ref[q] = _dot(jnp.concatenate([vq * oh for oh in onehots], axis=0), kq_ref[q])
            state = [s_ref[q] for q in range(len(quads))]
            for u in range(8):
                sk = _dot(jnp.concatenate([(state[q] * rows[q][1][u:u + 1, :]).astype(BF16)
                                           for q in range(len(quads))], axis=0), ones_bd)
                for q, (bi, hf) in enumerate(quads):
                    w8, _, b8, r8 = rows[q]
                    state[q] = (state[q] * w8[u:u + 1, :] - sk[q * nh:(q + 1) * nh, :] * b8[u:u + 1, :]
                                + vk_ref[q, u * nh:(u + 1) * nh, :])
                    r_lhs = jnp.where(r_mask, r8[u:u + 1, :], 0.0).astype(BF16)
                    o = _dot_nt(r_lhs, state[q].astype(BF16))
                    o_ref[t0 + u, bi, QUAD * hf:QUAD * (hf + 1), :] = o[0:QUAD, :]
            for q in range(len(quads)):
                s_ref[q] = state[q]
            return carry

        return lax.fori_loop(0, SUB_STEPS // 8, group, carry)

    lax.fori_loop(0, tt // SUB_STEPS, sub_block, 0)


def _rwkv_scan(r, w, k, kk, b, vt, ones_bd, place, bsz, t):
    tt = SCAN_TT
    wd = RWKV_WIDTH
    nquad = bsz * RWKV_HEADS // QUAD
    tok = pl.BlockSpec((bsz, tt, wd), lambda i: (0, i, 0))
    return pl.pallas_call(
        _scan_kernel,
        grid=(t // tt,),
        in_specs=[tok, tok, tok, tok, tok,
                  pl.BlockSpec((bsz, wd, tt), lambda i: (0, 0, i)),
                  pl.BlockSpec(ones_bd.shape, lambda i: (0, 0)),
                  pl.BlockSpec(place.shape, lambda i: (0, 0, 0))],
        out_specs=pl.BlockSpec((tt, bsz, RWKV_HEADS, RWKV_HEAD), lambda i: (i, 0, 0, 0)),
        out_shape=jax.ShapeDtypeStruct((t, bsz, RWKV_HEADS, RWKV_HEAD), F32),
        scratch_shapes=[pltpu.VMEM((nquad, RWKV_HEAD, QUAD_LANES), F32),
                        pltpu.VMEM((nquad, RWKV_HEAD, LANE), BF16),
                        pltpu.VMEM((nquad, LANE, QUAD_LANES), BF16),
                        pltpu.VMEM((nquad, 8 * RWKV_HEAD, QUAD_LANES), F32)],
        compiler_params=_cparams(("arbitrary",)),
        name="rwkv_scan",
    )(r, w, k, kk, b, vt, ones_bd, place)


def _merge_kernel(x_ref, oc_ref, os_ref, ow_ref, orw_ref, bonus_ref, g_ref, zgn_ref, zgr_ref, gn_ref, hsum_ref,
                  pn_ref, pr_ref, wo_ref, ln_ref, o_ref):
    o_n = jnp.concatenate([oc_ref[0, p].astype(F32) + os_ref[0, p].astype(F32) + ow_ref[0, p].astype(F32)
                           for p in range(NSA_HEADS // 2)], axis=1)
    y_n = _dot(o_n.astype(BF16), pn_ref[...])
    o = orw_ref[...]
    mu = _dot_split(o, hsum_ref[...]) * (1.0 / RWKV_HEAD)
    d = o - mu
    var = _dot_split(d * d, hsum_ref[...]) * (1.0 / RWKV_HEAD)
    o = d * lax.rsqrt(var + GN_EPS) * gn_ref[0:1, :] + gn_ref[1:2, :]
    o = ((o + bonus_ref[...]) * g_ref[...]).astype(BF16)
    y1 = _sigmoid(zgn_ref[...]) * y_n + _sigmoid(zgr_ref[...]) * _dot(o, pr_ref[...])
    y = _dot(y1.astype(BF16), wo_ref[...])
    o_ref[...] = _layer_norm(DN_ALPHA * x_ref[...] + y, ln_ref[0:1, :], ln_ref[1:2, :])


def _merge(x, o_c, o_s, o_w, o_rw, bonus, g, z, gn, hsum, p_nsa, p_rwkv, w_out, ln, bsz, t, tm=256):
    n = bsz * t
    tps = t // tm
    d = D_MODEL
    w = RWKV_WIDTH
    head = pl.BlockSpec((1, NSA_HEADS // 2, tm, LANE), lambda i: (i // tps, 0, i % tps, 0))
    tok = pl.BlockSpec((tm, w), lambda i: (i, 0))

    def full(a):
        return pl.BlockSpec(a.shape, lambda i: (0,) * a.ndim)

    return pl.pallas_call(
        _merge_kernel,
        grid=(n // tm,),
        in_specs=[pl.BlockSpec((tm, d), lambda i: (i, 0)), head, head, head, tok, tok, tok,
                  pl.BlockSpec((tm, d), lambda i: (i, C_GN // d)),
                  pl.BlockSpec((tm, d), lambda i: (i, C_GR // d)),
                  full(gn), full(hsum), full(p_nsa), full(p_rwkv), full(w_out), full(ln)],
        out_specs=pl.BlockSpec((tm, d), lambda i: (i, 0)),
        out_shape=jax.ShapeDtypeStruct((n, d), F32),
        compiler_params=_cparams(("parallel",)),
        name="mix_merge_ln",
    )(x, o_c, o_s, o_w, o_rw, bonus, g, z, z, gn, hsum, p_nsa, p_rwkv, w_out, ln)


def _xattn_kernel(x_ref, kt_ref, v_ref, wq_ref, wo_ref, ln_ref, rw_ref, o_ref, ob_ref, s_ref):
    x = x_ref[...]
    q = _dot(x.astype(BF16), wq_ref[...]).astype(BF16)
    outs = []
    for h in range(XATTN_HEADS):
        hs = slice(h * XATTN_HEAD, (h + 1) * XATTN_HEAD)
        s = _dot(q[:, hs], kt_ref[0, hs, :]) * (XATTN_HEAD ** -0.5)
        m = jnp.max(s, axis=-1, keepdims=True)
        p = jnp.exp(s - m)
        l = jnp.sum(p, axis=-1, keepdims=True)
        outs.append(_dot(p.astype(BF16), v_ref[0, :, hs]) * (1.0 / l))
    o = jnp.concatenate(outs, axis=1).astype(BF16)
    x2 = _layer_norm(DN_ALPHA * x + _dot(o, wo_ref[...]), ln_ref[0:1, :], ln_ref[1:2, :])
    o_ref[...] = x2
    hi = x2.astype(BF16)
    ob_ref[...] = hi
    lo = (x2 - hi.astype(F32)).astype(BF16)
    logits = _dot(hi, rw_ref[0]) + (_dot(lo, rw_ref[0]) + _dot(hi, rw_ref[1]))
    s_ref[...] = _sigmoid(logits)


def _xattn(x, kt, v, wq, wo, ln, rw2, bsz, t, tm=256):
    n = bsz * t
    tps = t // tm
    d = D_MODEL

    def full(a):
        return pl.BlockSpec(a.shape, lambda i: (0,) * a.ndim)

    return pl.pallas_call(
        _xattn_kernel,
        grid=(n // tm,),
        in_specs=[pl.BlockSpec((tm, d), lambda i: (i, 0)),
                  pl.BlockSpec((1,) + kt.shape[1:], lambda i: (i // tps, 0, 0)),
                  pl.BlockSpec((1,) + v.shape[1:], lambda i: (i // tps, 0, 0)),
                  full(wq), full(wo), full(ln), full(rw2)],
        out_specs=[pl.BlockSpec((tm, d), lambda i: (i, 0)), pl.BlockSpec((tm, d), lambda i: (i, 0)),
                   pl.BlockSpec((tm, LANE), lambda i: (i, 0))],
        out_shape=[jax.ShapeDtypeStruct((n, d), F32), jax.ShapeDtypeStruct((n, d), BF16),
                   jax.ShapeDtypeStruct((n, LANE), F32)],
        compiler_params=_cparams(("parallel",)),
        name="xattn_ln_router",
    )(x, kt, v, wq, wo, ln, rw2)


def _expert_kernel(meta_ref, x_ref, gate_ref, *refs):
    w_refs, o_ref, w_scr = refs[:6], refs[6], refs[7:]
    i = pl.program_id(0)
    nb = pl.num_programs(0)
    prev = jnp.maximum(i - 1, 0)

    for side in range(2):
        @pl.when((i == 0) | (meta_ref[side * nb + i] != meta_ref[side * nb + prev]))
        def _():
            for j in range(3):
                w_scr[3 * side + j][...] = w_refs[3 * side + j][0, 0].astype(BF16)

    n_used = meta_ref[2 * nb]

    @pl.when(i < n_used)
    def _():
        x = x_ref[...]
        y = jnp.zeros(o_ref.shape, F32)
        for side in range(2):
            wg, wu, wd = w_scr[3 * side:3 * side + 3]
            hg = _dot(x, wg[...])
            h = hg * _sigmoid(hg) * _dot(x, wu[...])
            y = y + gate_ref[:, side:side + 1] * _dot(h.astype(BF16), wd[...])
        o_ref[...] = y.astype(o_ref.dtype)

    @pl.when(i >= n_used)
    def _():
        o_ref[...] = jnp.zeros(o_ref.shape, o_ref.dtype)


def _experts(meta, rows, gates, w_gate, w_up, w_down, layer):
    n_rows, d = rows.shape
    ff = w_gate.shape[3]
    nb = n_rows // MOE_BLK

    def weight(shape, side):
        return pl.BlockSpec((1, 1) + shape, lambda i, m: (layer, m[side * nb + i], 0, 0))

    return pl.pallas_call(
        _expert_kernel,
        grid_spec=pltpu.PrefetchScalarGridSpec(
            num_scalar_prefetch=1,
            grid=(nb,),
            in_specs=[pl.BlockSpec((MOE_BLK, d), lambda i, m: (i, 0)),
                      pl.BlockSpec((MOE_BLK, TOP_K), lambda i, m: (i, 0))]
                     + [weight(s, side) for side in range(2) for s in ((d, ff), (d, ff), (ff, d))],
            out_specs=pl.BlockSpec((MOE_BLK, d), lambda i, m: (i, 0)),
            scratch_shapes=[pltpu.VMEM(s, BF16) for _ in range(2) for s in ((d, ff), (d, ff), (ff, d))]),
        out_shape=jax.ShapeDtypeStruct((n_rows, d), BF16),
        compiler_params=_cparams(("arbitrary",)),
        name="moe_experts",
    )(meta, rows, gates, w_gate, w_up, w_down, w_gate, w_up, w_down)


def _combine_kernel(x_ref, y_ref, ln_ref, o_ref):
    o_ref[...] = _layer_norm(DN_ALPHA * x_ref[...] + y_ref[...], ln_ref[0:1, :], ln_ref[1:2, :])


def _combine(x, y, ln, tm=512):
    n, d = x.shape
    tok = pl.BlockSpec((tm, d), lambda i: (i, 0))
    return pl.pallas_call(
        _combine_kernel,
        grid=(n // tm,),
        in_specs=[tok, tok, pl.BlockSpec(ln.shape, lambda i: (0, 0))],
        out_specs=tok,
        out_shape=jax.ShapeDtypeStruct((n, d), F32),
        compiler_params=_cparams(("parallel",)),
        name="moe_combine_ln",
    )(x, y, ln)


def _t5_bucket(dist):
    n = jnp.maximum(dist, 0)
    max_exact = REL_BUCKETS // 2
    log_ratio = jnp.log(jnp.maximum(n, 1).astype(F32) / max_exact) / math.log(REL_MAX_DIST / max_exact)
    large = jnp.minimum(max_exact + (log_ratio * (REL_BUCKETS - max_exact)).astype(jnp.int32), REL_BUCKETS - 1)
    return jnp.where(n < max_exact, n, large)


def _bias_tables(rel_table, t):
    nc = t // CMP_STRIDE
    f = rel_table[_t5_bucket(jnp.arange(t + 2048))].astype(F32).T
    ql = jnp.arange(Q_BLOCK)[:, None]

    chunks_per_q = Q_BLOCK // CMP_STRIDE
    r = np.arange(CMP_STRIDE)[:, None]
    m = np.arange(2 * nc + chunks_per_q - 1)[None, :]
    dist_c = CMP_STRIDE * (nc - 8 + chunks_per_q - 1 - m) + r - (CMP_LEN - 1)
    small = jnp.where((dist_c >= 0)[None], jnp.take(f, np.clip(dist_c, 0, f.shape[-1] - 1), axis=1), NEG_INF)
    pc = jnp.stack([small[:, :, chunks_per_q - 1 - a:chunks_per_q - 1 - a + 2 * nc] for a in range(chunks_per_q)], axis=1)
    pc = pc.reshape(NSA_GROUPS, NSA_HPG * Q_BLOCK, 2 * nc)
    pc = jnp.stack([pc[:, :, s:s + 2 * nc - LANE] for s in range(0, LANE, 8)], axis=1)
    nvar = WINDOW // Q_BLOCK + 1

    def toeplitz(base, width, hi, shift):
        period = width + Q_BLOCK
        m = np.arange(period)
        dist = base - np.where(m < width, m, m - period)
        row = jnp.take(f, np.clip(dist, 0, f.shape[-1] - 1), axis=1) - shift
        row = jnp.where(((dist >= 0) & (dist < hi))[None], row, NEG_INF)
        flat = jnp.tile(row, (1, Q_BLOCK))[:, :Q_BLOCK * (period - 1)]
        return flat.reshape(NSA_HEADS, Q_BLOCK, period - 1)[:, :, :width]

    wb = jnp.stack([toeplitz(Q_BLOCK * v, WINDOW + Q_BLOCK, WINDOW, 0.0) for v in range(nvar)], axis=1)
    wb = wb.reshape(NSA_GROUPS, NSA_HPG, nvar, Q_BLOCK, -1).transpose(0, 2, 1, 3, 4)
    wb = wb.reshape(NSA_GROUPS, nvar, NSA_HPG * Q_BLOCK, -1)
    far = rel_table[REL_BUCKETS - 1].astype(F32)
    lt = toeplitz(SEL_NEAR + SEL_TK, SEL_NEAR + 2 * SEL_TK + Q_BLOCK, f.shape[-1], far[:, None])
    lt = lt.reshape(NSA_GROUPS, NSA_HPG * Q_BLOCK, -1)
    far_hi = far.astype(BF16).astype(F32)
    lane = jnp.arange(LANE)[None, :]
    cq = jnp.where(lane == HEAD_DIM, far_hi[:, None], jnp.where(lane == HEAD_DIM + 1, (far - far_hi)[:, None], 0.0))
    cq = jnp.broadcast_to(cq[:, None, :], (NSA_HEADS, Q_BLOCK, LANE)).reshape(NSA_GROUPS, NSA_HPG * Q_BLOCK, LANE)
    return pc, wb, lt, cq


def _in_proj_perm():
    perm = np.full((IN_PAD,), -1, np.int64)
    for h in range(NSA_HEADS):
        perm[C_Q + h * LANE:C_Q + h * LANE + HEAD_DIM] = np.arange(HEAD_DIM) + h * HEAD_DIM
    rw0 = NSA_COLS
    g0 = NSA_COLS + RWKV_COLS
    perm[C_GN:C_GN + D_MODEL] = g0 + np.arange(D_MODEL)
    perm[C_GR:C_GR + D_MODEL] = g0 + D_MODEL + np.arange(D_MODEL)
    perm[C_R:C_R + RWKV_WIDTH] = rw0 + np.arange(RWKV_WIDTH)
    perm[C_K:C_K + RWKV_WIDTH] = rw0 + RWKV_WIDTH + np.arange(RWKV_WIDTH)
    perm[C_V:C_V + RWKV_WIDTH] = rw0 + 2 * RWKV_WIDTH + np.arange(RWKV_WIDTH)
    for g in range(NSA_GROUPS):
        for br in range(3):
            for h in range(NSA_HPG):
                perm[C_G3 + g * LANE + br * NSA_HPG + h] = NSA_WIDTH + 6 * NSA_KV + (g * NSA_HPG + h) * 3 + br
    perm[C_WA:C_WA + LORA_W + LORA_A] = rw0 + 3 * RWKV_WIDTH + np.arange(LORA_W + LORA_A)
    perm[C_ZG:C_ZG + LORA_G] = rw0 + 3 * RWKV_WIDTH + LORA_W + LORA_A + np.arange(LORA_G)
    return perm


_PERM = _in_proj_perm()


def _permute_in_proj(w):
    pieces = []
    start = 0
    while start < IN_PAD:
        stop = start + 1
        if _PERM[start] < 0:
            while stop < IN_PAD and _PERM[stop] < 0:
                stop += 1
            pieces.append(jnp.zeros((w.shape[0], stop - start), BF16))
        else:
            while stop < IN_PAD and _PERM[stop] == _PERM[stop - 1] + 1:
                stop += 1
            pieces.append(w[:, int(_PERM[start]):int(_PERM[stop - 1]) + 1].astype(BF16))
        start = stop
    return jnp.concatenate(pieces, axis=1)


def _pad_to(a, shape):
    return jnp.pad(a, [(0, s - d) for d, s in zip(a.shape, shape)])


_CLASS_EXPERTS = np.asarray([(grp * EXPERTS_PER_GROUP + i, grp * EXPERTS_PER_GROUP + j)
                             for grp in range(N_EXPERT_GROUPS)
                             for i in range(EXPERTS_PER_GROUP) for j in range(i + 1, EXPERTS_PER_GROUP)], np.int32)
N_PAIR_CLASSES = len(_CLASS_EXPERTS)


def _route(s, router_bias, n_tok):
    s16 = s[:, :N_EXPERTS]
    s_sel = s16 + router_bias.astype(F32)

    def top2(a):
        i1 = jnp.argmax(a, axis=-1)
        rest = jnp.where(jnp.arange(a.shape[-1]) == i1[..., None], -jnp.inf, a)
        i2 = jnp.argmax(rest, axis=-1)
        return jnp.max(a, axis=-1) + jnp.max(rest, axis=-1), jnp.stack([i1, i2], axis=-1)

    group_score, _ = top2(s_sel.reshape(n_tok, N_EXPERT_GROUPS, EXPERTS_PER_GROUP))
    group = jnp.argmax(group_score, axis=-1)
    in_group = (jnp.arange(N_EXPERTS) // EXPERTS_PER_GROUP)[None, :] == group[:, None]
    _, e_idx = top2(jnp.where(in_group, s_sel, NEG_INF))
    picked = jnp.arange(N_EXPERTS)[None, None, :] == e_idx[:, :, None]
    s_top = jnp.sum(jnp.where(picked, s16[:, None, :], 0.0), axis=-1)
    gate = s_top / jnp.sum(s_top, axis=-1, keepdims=True)
    swap = e_idx[:, 0] > e_idx[:, 1]
    e_lo = jnp.where(swap, e_idx[:, 1], e_idx[:, 0])
    e_hi = jnp.where(swap, e_idx[:, 0], e_idx[:, 1])
    gate_lohi = jnp.where(swap[:, None], gate[:, ::-1], gate)
    l_lo, l_hi = e_lo % EXPERTS_PER_GROUP, e_hi % EXPERTS_PER_GROUP
    pairs_per_group = N_PAIR_CLASSES // N_EXPERT_GROUPS
    cls = ((e_lo // EXPERTS_PER_GROUP) * pairs_per_group
           + (l_lo * (2 * EXPERTS_PER_GROUP - 1 - l_lo)) // 2 + (l_hi - l_lo - 1))
    onehot = (cls[:, None] == jnp.arange(N_PAIR_CLASSES)[None, :]).astype(jnp.int32)
    counts = jnp.sum(onehot, axis=0)
    padded = (counts + MOE_BLK - 1) // MOE_BLK * MOE_BLK
    ends = jnp.cumsum(padded)
    before = jnp.cumsum(onehot, axis=0) - onehot + (ends - padded)[None, :]
    dest = jnp.sum(onehot * before, axis=1)
    n_rows = n_tok + N_PAIR_CLASSES * MOE_BLK
    tok_of_row = jnp.zeros((n_rows,), jnp.int32).at[dest].set(jnp.arange(n_tok, dtype=jnp.int32))
    block_start = jnp.arange(n_rows // MOE_BLK) * MOE_BLK
    block_cls = jnp.minimum(jnp.sum((ends[None, :] <= block_start[:, None]).astype(jnp.int32), axis=1),
                            N_PAIR_CLASSES - 1)
    block_onehot = (block_cls[:, None] == jnp.arange(N_PAIR_CLASSES)[None, :]).astype(jnp.int32)
    block_experts = jnp.sum(block_onehot[:, :, None] * jnp.asarray(_CLASS_EXPERTS)[None], axis=1)
    meta = jnp.concatenate([block_experts[:, 0], block_experts[:, 1], ends[-1:] // MOE_BLK]).astype(jnp.int32)
    return gate_lohi, dest, tok_of_row, meta


def kernel(x, mem, rel_table, router_w, router_bias, w_in, cmp_pe_k, cmp_pe_v, cmp_w1k, cmp_w2k, cmp_w1v, cmp_w2v, rwkv_mu, rwkv_w0, rwkv_w2, rwkv_a0, rwkv_a2, rwkv_g2, rwkv_kk, rwkv_ka, rwkv_rk, rwkv_gn_g, rwkv_gn_b, rwkv_v0, rwkv_v1, rwkv_v2, p_nsa, p_rwkv, w_out, ln1_g, ln1_b, xq_w, xk_w, xv_w, xo_w, ln2_g, ln2_b, moe_w_gate, moe_w_up, moe_w_down, ln3_g, ln3_b):
    bsz, t, d = x.shape
    n = bsz * t
    depth = w_in.shape[0]
    nc = t // CMP_STRIDE
    nsel = t // SEL_BLOCK
    g, hd = NSA_GROUPS, HEAD_DIM

    pc_tab, wb_tab, lt_tab, cq_tab = _bias_tables(rel_table, t)
    cmp_start = np.arange(nc) * CMP_STRIDE
    sel_start = np.arange(nsel) * SEL_BLOCK
    overlap = ((cmp_start[:, None] <= sel_start[None, :] + SEL_BLOCK - 1)
               & (cmp_start[:, None] + CMP_LEN - 1 >= sel_start[None, :]) & (cmp_start[:, None] < (nc - 1) * CMP_STRIDE))
    overlap = jnp.asarray(overlap.T, BF16)
    lane_head = np.arange(RWKV_WIDTH) // RWKV_HEAD
    hsum = jnp.asarray(lane_head[:, None] == lane_head[None, :], BF16)
    quad_head = np.arange(QUAD_LANES) // RWKV_HEAD
    ones_bd = jnp.asarray(quad_head[:, None] == quad_head[None, :], BF16)
    step = np.arange(SCAN_TT)[None, :, None]
    lane = np.arange(LANE)[None, None, :]
    sbh = np.arange(SCAN_TT // SUB_STEPS * QUAD)[:, None, None]
    place = jnp.asarray((lane // SUB_STEPS == sbh % QUAD) & (step == (sbh // QUAD) * SUB_STEPS + lane % SUB_STEPS), BF16)
    rw_f = _pad_to(router_w.astype(F32), (d, LANE))
    rw_hi = rw_f.astype(BF16)
    rw2 = jnp.stack([rw_hi, (rw_f - rw_hi.astype(F32)).astype(BF16)])

    xf = x.reshape(n, d)
    mem_f = mem.reshape(bsz * mem.shape[1], d)
    v_first = None
    for l in range(depth):
        w_in_l = _permute_in_proj(w_in[l])
        z = _matmul(xf, w_in_l, F32, 1024, 1024)

        kv0 = NSA_WIDTH
        zeros64 = jnp.zeros((d, LANE - hd), BF16)
        w_kv = jnp.stack([jnp.concatenate(
            [piece for i in range(KV_ARRAYS)
             for piece in (w_in[l][:, kv0 + i * NSA_KV + gi * hd:kv0 + i * NSA_KV + (gi + 1) * hd].astype(BF16), zeros64)],
            axis=1) for gi in range(g)])
        kc, vc, ks_aug, vs_aug, kw_pad, vw_aug = _kv_proj(xf, w_kv, bsz, t)

        def cmp_w(pe, w1, w2):
            return (pe.reshape(2, CMP_STRIDE * hd), w1.reshape(2, CMP_STRIDE * hd, CMP_HIDDEN).astype(BF16),
                    _pad_to(w2, (CMP_HIDDEN, LANE)).astype(BF16))

        chunks = lambda a: a.reshape(bsz, g, nc, CMP_STRIDE * hd)
        k_cmp = _compress(chunks(kc), *cmp_w(cmp_pe_k[l], cmp_w1k[l], cmp_w2k[l]))
        v_cmp = _compress(chunks(vc), *cmp_w(cmp_pe_v[l], cmp_w1v[l], cmp_w2v[l]))
        o_c, sel_mask = _cmp_branch(z, k_cmp, v_cmp, pc_tab, overlap, bsz, t)
        o_s = _sel_branch(z, ks_aug, vs_aug, sel_mask, lt_tab, cq_tab, bsz, t)
        o_w = _win_branch(z, kw_pad, vw_aug, wb_tab, bsz, t)

        mu = rwkv_mu[l]
        w3 = 3 * RWKV_WIDTH
        mu5 = jnp.stack([mu[0:512], mu[512:1024], mu[1024:1536], _pad_to(mu[w3:w3 + 128], (512,)),
                         _pad_to(mu[w3 + 128:w3 + 256], (512,))])
        vecs = jnp.stack([rwkv_w0[l], rwkv_a0[l], rwkv_kk[l], rwkv_ka[l], rwkv_rk[l].reshape(-1)])
        w2p = jnp.concatenate([rwkv_w2[l], jnp.zeros_like(rwkv_a2[l])], axis=0).astype(BF16)
        a2p = jnp.concatenate([jnp.zeros_like(rwkv_w2[l]), rwkv_a2[l]], axis=0).astype(BF16)
        vres = None
        if l > 0:
            vres = (v_first, rwkv_v0[l - 1][None, :], _pad_to(rwkv_v1[l - 1], (RWKV_WIDTH, LANE)).astype(BF16),
                    _pad_to(rwkv_v2[l - 1], (LANE, RWKV_WIDTH)).astype(BF16))
        r, w, k, v, kk, b, g_out, bonus, vt = _rwkv_prep(z, mu5, vecs, w2p, a2p, rwkv_g2[l].astype(BF16), hsum, vres,
                                                          bsz, t)
        if l == 0:
            v_first = v
        seq = lambda a: a.reshape(bsz, t, RWKV_WIDTH)
        o_rw = _rwkv_scan(seq(r), seq(w), seq(k), seq(kk), seq(b), vt, ones_bd, place, bsz, t)
        o_rw = o_rw.transpose(1, 0, 2, 3).reshape(n, RWKV_WIDTH)

        xf = _merge(xf, o_c, o_s, o_w, o_rw, bonus, g_out, z, jnp.stack([rwkv_gn_g[l], rwkv_gn_b[l]]), hsum,
                    p_nsa[l].astype(BF16), p_rwkv[l].astype(BF16), w_out[l].astype(BF16),
                    jnp.stack([ln1_g[l], ln1_b[l]]), bsz, t)

        mlen = mem.shape[1]
        mk = _matmul(mem_f, xk_w[l].astype(BF16), BF16, mlen, XATTN_WIDTH)
        mv = _matmul(mem_f, xv_w[l].astype(BF16), BF16, mlen, XATTN_WIDTH)
        kt = mk.reshape(bsz, mlen, XATTN_WIDTH).transpose(0, 2, 1)
        xf, xf_b, scores = _xattn(xf, kt, mv.reshape(bsz, mlen, XATTN_WIDTH), xq_w[l].astype(BF16), xo_w[l].astype(BF16),
                            jnp.stack([ln2_g[l], ln2_b[l]]), rw2, bsz, t)

        gate, dest, tok_of_row, meta = _route(scores, router_bias, n)
        rows = jnp.take(xf_b, tok_of_row, axis=0)
        y_rows = _experts(meta, rows, jnp.take(gate, tok_of_row, axis=0), moe_w_gate, moe_w_up, moe_w_down, l)
        xf = _combine(xf, jnp.take(y_rows, dest, axis=0), jnp.stack([ln3_g[l], ln3_b[l]]))
    return xf.reshape(bsz, t, d)
```

```python
import functools
import math

import numpy as np
import jax
import jax.numpy as jnp
from jax import lax
from jax.experimental import pallas as pl
from jax.experimental.pallas import tpu as pltpu

F32 = jnp.float32
BF16 = jnp.bfloat16

D_MODEL = 1024
DEPTH = 2
NSA_HEADS = 8
NSA_GROUPS = 2
NSA_HPG = NSA_HEADS // NSA_GROUPS
HEAD_DIM = 64
NSA_WIDTH = NSA_HEADS * HEAD_DIM
NSA_KV = NSA_GROUPS * HEAD_DIM
CMP_STRIDE = 16
CMP_LEN = 2 * CMP_STRIDE
CMP_HIDDEN = 256
SEL_BLOCK = 64
SEL_TOPN = 16
WINDOW = 512
Q_BLOCK = 128
SEL_FORCE = 1e9
RWKV_HEADS = 8
RWKV_HEAD = 64
RWKV_WIDTH = RWKV_HEADS * RWKV_HEAD
LORA_W = 64
LORA_A = 64
LORA_V = 32
LORA_G = 128
GN_EPS = 64e-5
REL_BUCKETS = 32
REL_MAX_DIST = 1024
XATTN_HEADS = 4
XATTN_HEAD = 128
XATTN_WIDTH = XATTN_HEADS * XATTN_HEAD
N_EXPERTS = 16
N_EXPERT_GROUPS = 4
EXPERTS_PER_GROUP = N_EXPERTS // N_EXPERT_GROUPS
TOP_K = 2
EXPERT_FF = 512
DN_ALPHA = (2 * DEPTH) ** 0.25
LN_EPS = 1e-5
NEG_INF = -1e30
NSA_COLS = NSA_WIDTH + 6 * NSA_KV + 3 * NSA_HEADS
RWKV_COLS = 3 * RWKV_WIDTH + LORA_W + LORA_A + LORA_G

LANE = 128
VMEM_LIMIT = 56 * 1024 * 1024

C_Q = 0
C_GN = 1024
C_GR = 2048
C_R = 3072
C_K = 3584
C_V = 4096
C_G3 = 4608
C_WA = 4864
C_ZG = 4992
IN_PAD = 5120
KV_ARRAYS = 6

MOE_BLK = 256
SCAN_TT = 128
FAR_BIAS_DIST = 1280


def _cparams(sem):
    return pltpu.CompilerParams(dimension_semantics=sem, vmem_limit_bytes=VMEM_LIMIT)


def _sigmoid(x):
    return 1.0 / (1.0 + jnp.exp(-x))


def _layer_norm(v, g, b):
    mu = jnp.mean(v, axis=-1, keepdims=True)
    d = v - mu
    var = jnp.mean(d * d, axis=-1, keepdims=True)
    return d * lax.rsqrt(var + LN_EPS) * g + b


def _dot(a, b):
    return jnp.dot(a, b, preferred_element_type=F32)


def _dot_nt(a, b):
    return lax.dot_general(a, b, (((1,), (1,)), ((), ())), preferred_element_type=F32)


def _dot_split(x, w):
    hi = x.astype(BF16)
    lo = (x - hi.astype(F32)).astype(BF16)
    return _dot(hi, w) + _dot(lo, w)


def _mm_kernel(x_ref, w_ref, o_ref, xb_ref):
    @pl.when(pl.program_id(1) == 0)
    def _():
        xb_ref[...] = x_ref[...].astype(BF16)

    o_ref[...] = _dot(xb_ref[...], w_ref[...]).astype(o_ref.dtype)


def _matmul(x, w, out_dtype, tm, tn):
    n, k = x.shape
    m = w.shape[1]
    return pl.pallas_call(
        _mm_kernel,
        grid=(n // tm, m // tn),
        in_specs=[pl.BlockSpec((tm, k), lambda i, j: (i, 0)),
                  pl.BlockSpec((k, tn), lambda i, j: (0, j))],
        out_specs=pl.BlockSpec((tm, tn), lambda i, j: (i, j)),
        out_shape=jax.ShapeDtypeStruct((n, m), out_dtype),
        scratch_shapes=[pltpu.VMEM((tm, k), BF16)],
        compiler_params=_cparams(("parallel", "arbitrary")),
        name="matmul",
    )(x, w)


def _kv_proj_kernel(tiles_per_seq, x_ref, w_ref, kc_ref, vc_ref, ks_ref, vs_ref, kw_ref, vw_ref, xb_ref):
    @pl.when(pl.program_id(1) == 0)
    def _():
        xb_ref[...] = x_ref[...].astype(BF16)

    y = _dot(xb_ref[...], w_ref[pl.program_id(1)])
    tm = y.shape[0]
    slot = lambda i: y[:, i * LANE:(i + 1) * LANE]
    lane = lax.broadcasted_iota(jnp.int32, (tm, LANE), 1)
    one_lane = jnp.where(lane == HEAD_DIM, 1.0, 0.0)
    t0 = lax.rem(pl.program_id(0), tiles_per_seq) * tm
    block = jnp.right_shift(t0 + lax.broadcasted_iota(jnp.int32, (tm, LANE), 0), int(math.log2(SEL_BLOCK)))
    kc_ref[0, 0] = slot(0)[:, :HEAD_DIM]
    vc_ref[0, 0] = slot(1)[:, :HEAD_DIM]
    ks_ref[0, 0] = jnp.concatenate([slot(2) + jnp.where(lane == HEAD_DIM + 1, 1.0, one_lane),
                                    jnp.where(lane == block, 1.0, 0.0)], axis=1).astype(BF16)
    vs_ref[0, 0] = (slot(3) + one_lane).astype(BF16)
    kw_ref[0, 0] = slot(4).astype(BF16)
    vw_ref[0, 0] = (slot(5) + one_lane).astype(BF16)


def _kv_proj(x, w_kv, bsz, t, tm=512):
    n, d = x.shape
    tps = t // tm
    g = NSA_GROUPS

    def out(width, dtype):
        return (pl.BlockSpec((1, 1, tm, width), lambda i, j: (i // tps, j, i % tps, 0)),
                jax.ShapeDtypeStruct((bsz, g, t, width), dtype))

    outs = [out(HEAD_DIM, F32), out(HEAD_DIM, F32), out(2 * LANE, BF16), out(LANE, BF16), out(LANE, BF16),
            out(LANE, BF16)]
    return pl.pallas_call(
        functools.partial(_kv_proj_kernel, tps),
        grid=(n // tm, g),
        in_specs=[pl.BlockSpec((tm, d), lambda i, j: (i, 0)),
                  pl.BlockSpec((g, d, KV_ARRAYS * LANE), lambda i, j: (0, 0, 0))],
        out_specs=[o[0] for o in outs],
        out_shape=[o[1] for o in outs],
        scratch_shapes=[pltpu.VMEM((tm, d), BF16)],
        compiler_params=_cparams(("parallel", "arbitrary")),
        name="nsa_kv_proj",
    )(x, w_kv)


def _compress_kernel(u_ref, pe_ref, w1_ref, w2_ref, o_ref):
    u = u_ref[0, 0]
    a = _dot((u + pe_ref[0:1, :]).astype(BF16), w1_ref[0])
    b = _dot((u + pe_ref[1:2, :]).astype(BF16), w1_ref[1])
    nc = u.shape[0]
    h = a + pltpu.roll(b, nc - 1, 0)
    h = 0.5 * h * (1.0 + jnp.tanh(math.sqrt(2.0 / math.pi) * (h + 0.044715 * (h * h * h))))
    y = _dot(h.astype(BF16), w2_ref[...])
    row = lax.broadcasted_iota(jnp.int32, y.shape, 0)
    o_ref[0, 0] = jnp.where(row < nc - 1, y, 0.0)


def _compress(u, pe2, w1, w2p):
    b, g, nc, _ = u.shape
    return pl.pallas_call(
        _compress_kernel,
        grid=(b, g),
        in_specs=[pl.BlockSpec((1, 1, nc, 1024), lambda i, j: (i, j, 0, 0)),
                  pl.BlockSpec((2, 1024), lambda i, j: (0, 0)),
                  pl.BlockSpec((2, 1024, CMP_HIDDEN), lambda i, j: (0, 0, 0)),
                  pl.BlockSpec((CMP_HIDDEN, LANE), lambda i, j: (0, 0))],
        out_specs=pl.BlockSpec((1, 1, nc, LANE), lambda i, j: (i, j, 0, 0)),
        out_shape=jax.ShapeDtypeStruct((b, g, nc, LANE), F32),
        compiler_params=_cparams(("parallel", "parallel")),
        name="nsa_compress",
    )(u, pe2, w1, w2p)


def _stack_heads(q_ref, row0=0):
    q = q_ref[row0:row0 + Q_BLOCK, :] * (HEAD_DIM ** -0.5)
    return jnp.concatenate([q[:, h * LANE:(h + 1) * LANE] for h in range(NSA_HPG)], axis=0).astype(BF16)


def _store_heads(o_ref, o, gates, branch, row0=0):
    lane = lax.broadcasted_iota(jnp.int32, (Q_BLOCK, LANE), 1)
    for pair in range(NSA_HPG // 2):
        halves = []
        for h in (2 * pair, 2 * pair + 1):
            c = branch * NSA_HPG + h
            halves.append(o[h * Q_BLOCK:(h + 1) * Q_BLOCK, :] * gates[:, c:c + 1])
        o_ref[0, pair, row0:row0 + Q_BLOCK, :] = jnp.where(lane < HEAD_DIM, halves[0],
                                                           pltpu.roll(halves[1], HEAD_DIM, 1)).astype(o_ref.dtype)


CMP_QB = 2


def _cmp_kernel(q_ref, zg_ref, kc_ref, vc_ref, pc0_ref, pc1_ref, ov_ref, o_ref, m_ref, *scratch):
    nc = kc_ref.shape[2]
    nsel = ov_ref.shape[0]
    kc = kc_ref[0, 0].astype(BF16)
    vc = vc_ref[0, 0].astype(BF16)
    gates = _sigmoid(zg_ref[...])
    blocks = [(qb, pl.program_id(2) * CMP_QB + qb, pc_ref, scratch[2 * qb], scratch[2 * qb + 1])
              for qb, pc_ref in enumerate((pc0_ref, pc1_ref))]
    for qb, c, _, p_s, _ in blocks:
        p_s[...] = _dot_nt(_stack_heads(q_ref, qb * Q_BLOCK), kc)
    for qb, c, pc_ref, p_s, pb_s in blocks:
        off = pl.multiple_of(lax.div((nc - 8) - 8 * c, LANE) * LANE, LANE)
        for r in range(NSA_HPG * Q_BLOCK // SEL_ROWS):
            rows = slice(r * SEL_ROWS, (r + 1) * SEL_ROWS)
            s = p_s[rows, :] + pc_ref[0, 0, rows, pl.ds(off, nc)]
            e = jnp.exp(s - jnp.max(s, axis=-1, keepdims=True))
            l = jnp.sum(e, axis=-1, keepdims=True)
            ql = (lax.broadcasted_iota(jnp.int32, (SEL_ROWS, 1), 0) + r * SEL_ROWS) & (Q_BLOCK - 1)
            p = e * jnp.where((c * Q_BLOCK + ql) >= (CMP_LEN - 1), 1.0 / l, 0.0)
            p_s[rows, :] = p
            pb_s[rows, :] = p.astype(BF16)
    vals = []
    jb = lax.broadcasted_iota(jnp.int32, (nsel, Q_BLOCK), 0)
    jbf = jb.astype(F32)
    for qb, c, _, p_s, pb_s in blocks:
        _store_heads(o_ref, _dot(pb_s[...], vc), gates[qb * Q_BLOCK:(qb + 1) * Q_BLOCK, :], 0, qb * Q_BLOCK)
        psum = ((p_s[0:Q_BLOCK, :] + p_s[Q_BLOCK:2 * Q_BLOCK, :])
                + (p_s[2 * Q_BLOCK:3 * Q_BLOCK, :] + p_s[3 * Q_BLOCK:4 * Q_BLOCK, :]))
        hi = psum.astype(BF16)
        lo = (psum - hi.astype(F32)).astype(BF16)
        imp = _dot_nt(ov_ref[...], hi) + _dot_nt(ov_ref[...], lo)
        tq = c * Q_BLOCK + lax.broadcasted_iota(jnp.int32, (nsel, Q_BLOCK), 1)
        jq = jnp.right_shift(tq, int(math.log2(SEL_BLOCK)))
        forced = (jb == 0) | (jb == jq) | (jb == jq - 1)
        vals.append(jnp.where(forced, SEL_FORCE, jnp.where(jb <= jq, imp, -SEL_FORCE)))
    sels = [jnp.zeros((nsel, Q_BLOCK), F32) for _ in blocks]
    for _ in range(SEL_TOPN):
        for qb in range(CMP_QB):
            mx = jnp.max(vals[qb], axis=0, keepdims=True)
            first = jnp.min(jnp.where(vals[qb] == mx, jbf, float(nsel)), axis=0, keepdims=True)
            hit = jbf == first
            sels[qb] = jnp.where(hit, jnp.where(mx >= 0.0, 1.0, 0.0), sels[qb])
            vals[qb] = jnp.where(hit, -3e38, vals[qb])
    for qb in range(CMP_QB):
        m_ref[0, 0, qb * Q_BLOCK:(qb + 1) * Q_BLOCK, :] = (sels[qb].T - 1.0) * (-NEG_INF)


def _cmp_branch(z, k_cmp, v_cmp, pc_tab, overlap, bsz, t):
    tq = CMP_QB * Q_BLOCK
    nq = t // tq
    nc = k_cmp.shape[2]
    nsel = overlap.shape[0]
    rows = NSA_HPG * Q_BLOCK

    def variant(qb):
        return pl.BlockSpec((1, 1, rows, 2 * nc - LANE),
                            lambda b, g, c: (g, lax.rem((nc - 8) - 8 * (CMP_QB * c + qb), LANE) // 8, 0, 0))

    return pl.pallas_call(
        _cmp_kernel,
        grid=(bsz, NSA_GROUPS, nq),
        in_specs=[pl.BlockSpec((tq, NSA_HPG * LANE), lambda b, g, c: (b * nq + c, g)),
                  pl.BlockSpec((tq, LANE), lambda b, g, c: (b * nq + c, C_G3 // LANE + g)),
                  pl.BlockSpec((1, 1, nc, LANE), lambda b, g, c: (b, g, 0, 0)),
                  pl.BlockSpec((1, 1, nc, LANE), lambda b, g, c: (b, g, 0, 0)),
                  variant(0), variant(1),
                  pl.BlockSpec((nsel, nc), lambda b, g, c: (0, 0))],
        out_specs=[pl.BlockSpec((1, NSA_HPG // 2, tq, LANE), lambda b, g, c: (b, g, c, 0)),
                   pl.BlockSpec((1, 1, tq, nsel), lambda b, g, c: (b, g, c, 0))],
        out_shape=[jax.ShapeDtypeStruct((bsz, NSA_HEADS // 2, t, LANE), BF16),
                   jax.ShapeDtypeStruct((bsz, NSA_GROUPS, t, nsel), F32)],
        scratch_shapes=[pltpu.VMEM((rows, nc), F32), pltpu.VMEM((rows, nc), BF16)] * CMP_QB,
        compiler_params=_cparams(("parallel", "parallel", "arbitrary")),
        name="nsa_cmp_select",
    )(z, z, k_cmp, v_cmp, pc_tab, pc_tab, overlap)


WIN_QB = 2


def _win_kernel(q_ref, zg_ref, k_ref, v_ref, wb_ref, o_ref, *scratch):
    nvar = wb_ref.shape[1]
    gates = _sigmoid(zg_ref[...])
    blocks = []
    for qb in range(WIN_QB):
        c = pl.program_id(2) * WIN_QB + qb
        start = pl.multiple_of(jnp.maximum(c * Q_BLOCK - WINDOW, 0), Q_BLOCK)
        blocks.append((qb, jnp.minimum(c, nvar - 1), pl.ds(start, WINDOW + Q_BLOCK), scratch[2 * qb], scratch[2 * qb + 1]))
    for qb, _, keys, s_s, _ in blocks:
        s_s[...] = _dot_nt(_stack_heads(q_ref, qb * Q_BLOCK), k_ref[0, 0, keys, :])
    for qb, var, _, s_s, p_s in blocks:
        for r in range(NSA_HPG * Q_BLOCK // SEL_ROWS):
            rows = slice(r * SEL_ROWS, (r + 1) * SEL_ROWS)
            s = s_s[rows, :] + wb_ref[0, var, rows, :]
            p_s[rows, :] = jnp.exp(s - jnp.max(s, axis=-1, keepdims=True)).astype(BF16)
    for qb, _, keys, _, p_s in blocks:
        o = _dot(p_s[...], v_ref[0, 0, keys, :])
        o = o * (1.0 / o[:, HEAD_DIM:HEAD_DIM + 1])
        _store_heads(o_ref, o, gates[qb * Q_BLOCK:(qb + 1) * Q_BLOCK, :], 2, qb * Q_BLOCK)


def _win_branch(z, kw, vw, wb_tab, bsz, t):
    tq = WIN_QB * Q_BLOCK
    nq = t // tq
    rows, width = NSA_HPG * Q_BLOCK, WINDOW + Q_BLOCK
    return pl.pallas_call(
        _win_kernel,
        grid=(bsz, NSA_GROUPS, nq),
        in_specs=[pl.BlockSpec((tq, NSA_HPG * LANE), lambda b, g, c: (b * nq + c, g)),
                  pl.BlockSpec((tq, LANE), lambda b, g, c: (b * nq + c, C_G3 // LANE + g)),
                  pl.BlockSpec((1, 1, t, LANE), lambda b, g, c: (b, g, 0, 0)),
                  pl.BlockSpec((1, 1, t, LANE), lambda b, g, c: (b, g, 0, 0)),
                  pl.BlockSpec((1,) + wb_tab.shape[1:], lambda b, g, c: (g, 0, 0, 0))],
        out_specs=pl.BlockSpec((1, NSA_HPG // 2, tq, LANE), lambda b, g, c: (b, g, c, 0)),
        out_shape=jax.ShapeDtypeStruct((bsz, NSA_HEADS // 2, t, LANE), BF16),
        scratch_shapes=[pltpu.VMEM((rows, width), F32), pltpu.VMEM((rows, width), BF16)] * WIN_QB,
        compiler_params=_cparams(("parallel", "parallel", "arbitrary")),
        name="nsa_window",
    )(z, z, kw, vw, wb_tab)


SEL_TK = 512


SEL_ROWS = 64
SEL_NEAR = FAR_BIAS_DIST + Q_BLOCK


def _sel_kernel(q_ref, zg_ref, k_ref, v_ref, mt_ref, lt_ref, cq_ref, o_ref, m_s, acc_s, s0_s, s1_s, p0_s, p1_s,
                a0_s, a1_s):
    c = pl.program_id(2)
    q = q_ref[...] * (HEAD_DIM ** -0.5)
    qs = jnp.concatenate([q[:, h * LANE:(h + 1) * LANE] for h in range(NSA_HPG)], axis=0) + cq_ref[0]
    mt = mt_ref[0, 0]
    q_aug = jnp.concatenate([qs, jnp.concatenate([mt] * NSA_HPG, axis=0)], axis=1).astype(BF16)
    m_s[...] = jnp.full(m_s.shape, NEG_INF, F32)
    acc_s[...] = jnp.zeros(acc_s.shape, F32)
    ncol = SEL_TK // LANE
    n_tiles = lax.div(c, SEL_TK // Q_BLOCK) + 1

    def key_rows(i):
        return pl.ds(pl.multiple_of(jnp.minimum(i, n_tiles - 1) * SEL_TK, SEL_TK), SEL_TK)

    def scores(i, dst):
        dst[...] = _dot_nt(q_aug, k_ref[0, 0, key_rows(i), :])

    def tile(i, cur, nxt, p_buf, a_buf):
        scores(i + 1, nxt)
        off = jnp.where(i < n_tiles, jnp.maximum(SEL_NEAR + SEL_TK - (c * Q_BLOCK - i * SEL_TK), 0),
                        SEL_NEAR + SEL_TK + Q_BLOCK)
        off = pl.multiple_of(off, LANE)
        for r in range(NSA_HPG * Q_BLOCK // SEL_ROWS):
            rows = slice(r * SEL_ROWS, (r + 1) * SEL_ROWS)
            s = cur[rows, :] + lt_ref[0, rows, pl.ds(off, SEL_TK)]
            cols = [s[:, j * LANE:(j + 1) * LANE] for j in range(ncol)]
            mx = functools.reduce(jnp.maximum, cols)
            m_old = m_s[rows, :]
            m_new = jnp.maximum(m_old, jnp.max(mx, axis=-1, keepdims=True))
            m_s[rows, :] = m_new
            a_buf[rows, :] = jnp.exp(m_old - m_new)
            p_buf[rows, :] = jnp.concatenate([jnp.exp(col - m_new) for col in cols], axis=1).astype(BF16)
        acc_s[...] = a_buf[...] * acc_s[...] + _dot(p_buf[...], v_ref[0, 0, key_rows(i), :])

    scores(0, s0_s)

    def pair(j, carry):
        tile(2 * j, s0_s, s1_s, p0_s, a0_s)
        tile(2 * j + 1, s1_s, s0_s, p1_s, a1_s)
        return carry

    lax.fori_loop(0, lax.div(n_tiles + 1, 2), pair, 0)
    acc = acc_s[...]
    o = acc * (1.0 / acc[:, HEAD_DIM:HEAD_DIM + 1])
    _store_heads(o_ref, o, _sigmoid(zg_ref[...]), 1)


def _sel_branch(z, ks_aug, vs, mt, lt_tab, cq_tab, bsz, t):
    nq = t // Q_BLOCK
    nsel = mt.shape[-1]
    rows = NSA_HPG * Q_BLOCK
    return pl.pallas_call(
        _sel_kernel,
        grid=(bsz, NSA_GROUPS, nq),
        in_specs=[pl.BlockSpec((Q_BLOCK, NSA_HPG * LANE), lambda b, g, c: (b * nq + c, g)),
                  pl.BlockSpec((Q_BLOCK, LANE), lambda b, g, c: (b * nq + c, C_G3 // LANE + g)),
                  pl.BlockSpec((1, 1, t, 2 * LANE), lambda b, g, c: (b, g, 0, 0)),
                  pl.BlockSpec((1, 1, t, LANE), lambda b, g, c: (b, g, 0, 0)),
                  pl.BlockSpec((1, 1, Q_BLOCK, nsel), lambda b, g, c: (b, g, c, 0)),
                  pl.BlockSpec((1, rows, lt_tab.shape[2]), lambda b, g, c: (g, 0, 0)),
                  pl.BlockSpec((1, rows, LANE), lambda b, g, c: (g, 0, 0))],
        out_specs=pl.BlockSpec((1, NSA_HPG // 2, Q_BLOCK, LANE), lambda b, g, c: (b, g, c, 0)),
        out_shape=jax.ShapeDtypeStruct((bsz, NSA_HEADS // 2, t, LANE), BF16),
        scratch_shapes=[pltpu.VMEM((rows, LANE), F32),
                        pltpu.VMEM((rows, LANE), F32),
                        pltpu.VMEM((rows, SEL_TK), F32),
                        pltpu.VMEM((rows, SEL_TK), F32),
                        pltpu.VMEM((rows, SEL_TK), BF16),
                        pltpu.VMEM((rows, SEL_TK), BF16),
                        pltpu.VMEM((rows, LANE), F32),
                        pltpu.VMEM((rows, LANE), F32)],
        compiler_params=_cparams(("parallel", "parallel", "arbitrary")),
        name="nsa_selected",
    )(z, z, ks_aug, vs, mt, lt_tab, cq_tab)


def _shift_mix(z_ref, prev_ref, mu, first):
    z = z_ref[...]
    prev = jnp.where(first, 0.0, prev_ref[7:8, :])
    row = lax.broadcasted_iota(jnp.int32, z.shape, 0)
    zp = jnp.where(row == 0, prev, pltpu.roll(z, 1, 0))
    return z + mu * (zp - z)


def _rwkv_prep_kernel(has_vres, tiles_per_seq, *refs):
    (zr, zk, zv, zwa, zg, pr, pk, pv, pwa, pg, mu_ref, vec_ref, w2_ref, a2_ref, g2_ref, hsum_ref) = refs[:16]
    pos = 16
    if has_vres:
        vf_ref, v0_ref, v1_ref, v2_ref = refs[pos:pos + 4]
        pos += 4
    r_o, w_o, k_o, v_o, kk_o, b_o, g_o, bonus_o, vt_o = refs[pos:]
    first = lax.rem(pl.program_id(0), tiles_per_seq) == 0
    r = _shift_mix(zr, pr, mu_ref[0:1, :], first)
    k = _shift_mix(zk, pk, mu_ref[1:2, :], first)
    v = _shift_mix(zv, pv, mu_ref[2:3, :], first)
    wa = _shift_mix(zwa, pwa, mu_ref[3:4, 0:LANE], first)
    zg_s = _shift_mix(zg, pg, mu_ref[4:5, 0:LANE], first)
    w0, a0, k_k, k_a, r_k = (vec_ref[i:i + 1, :] for i in range(5))
    if has_vres:
        lora = _dot(_dot(v.astype(BF16), v1_ref[...]).astype(BF16), v2_ref[...])
        v = v + (vf_ref[...] - v) * _sigmoid(v0_ref[...] + lora)
    u = w0 + _dot(jnp.tanh(wa).astype(BF16), w2_ref[...])
    decay = jnp.exp(-math.exp(-0.5) * _sigmoid(u))
    a = _sigmoid(a0 + _dot(wa.astype(BF16), a2_ref[...]))
    g = _dot(_sigmoid(zg_s).astype(BF16), g2_ref[...])
    kk = k * k_k
    kk = kk / jnp.maximum(jnp.sqrt(_dot_split(kk * kk, hsum_ref[...])), 1e-12)
    k = k * (1.0 + (a - 1.0) * k_a)
    r_o[...] = r
    w_o[...] = decay
    k_o[...] = k.astype(k_o.dtype)
    v_o[...] = v
    kk_o[...] = kk
    b_o[...] = kk * a
    g_o[...] = g.astype(g_o.dtype)
    bonus_o[...] = (_dot_split(r * k * r_k, hsum_ref[...]) * v).astype(bonus_o.dtype)
    vt_o[0] = v.T.astype(BF16)


def _rwkv_prep(z, mu5, vecs, w2p, a2p, g2, hsum, vres, bsz, t, tm=256):
    n = bsz * t
    nt = n // tm
    tps = t // tm
    w = RWKV_WIDTH

    def cur(width, col):
        return pl.BlockSpec((tm, width), lambda i: (i, col // width))

    def prev(width, col):
        return pl.BlockSpec((8, width), lambda i: (jnp.maximum(i * (tm // 8) - 1, 0), col // width))

    def full(shape):
        return pl.BlockSpec(shape, lambda i: (0,) * len(shape))

    cols = [(w, C_R), (w, C_K), (w, C_V), (LANE, C_WA), (LANE, C_ZG)]
    in_specs = [cur(*c) for c in cols] + [prev(*c) for c in cols]
    in_specs += [full(mu5.shape), full(vecs.shape), full(w2p.shape), full(a2p.shape), full(g2.shape), full(hsum.shape)]
    args = [z] * 10 + [mu5, vecs, w2p, a2p, g2, hsum]
    if vres is not None:
        v_first, v0, v1p, v2p = vres
        in_specs += [pl.BlockSpec((tm, w), lambda i: (i, 0)), full(v0.shape), full(v1p.shape), full(v2p.shape)]
        args += [v_first, v0, v1p, v2p]
    tok = pl.BlockSpec((tm, w), lambda i: (i, 0))
    out_specs = [tok] * 8 + [pl.BlockSpec((1, w, tm), lambda i: (i // tps, 0, i % tps))]
    tok_dtypes = (F32, F32, BF16, F32, F32, F32, BF16, BF16)
    out_shape = [jax.ShapeDtypeStruct((n, w), dt) for dt in tok_dtypes] + [jax.ShapeDtypeStruct((bsz, w, t), BF16)]
    return pl.pallas_call(
        functools.partial(_rwkv_prep_kernel, vres is not None, tps),
        grid=(nt,),
        in_specs=in_specs,
        out_specs=out_specs,
        out_shape=out_shape,
        compiler_params=_cparams(("parallel",)),
        name="rwkv_prep",
    )(*args)


QUAD = 4
QUAD_LANES = QUAD * RWKV_HEAD
SUB_STEPS = LANE // QUAD


def _scan_kernel(r_ref, w_ref, k_ref, kk_ref, b_ref, vt_ref, ones_ref, place_ref, o_ref, s_ref, vq_ref, kq_ref, vk0_ref,
                 vk1_ref):
    bsz = r_ref.shape[0]
    tt = r_ref.shape[1]
    nh = RWKV_HEAD
    quads = [(bi, hf) for bi in range(bsz) for hf in range(RWKV_HEADS // QUAD)]

    @pl.when(pl.program_id(0) == 0)
    def _():
        s_ref[...] = jnp.zeros(s_ref.shape, F32)

    sub_shift, head_shift = int(math.log2(SUB_STEPS)), int(math.log2(nh))
    sub_lane = lax.broadcasted_iota(jnp.int32, (nh, LANE), 1) & (SUB_STEPS - 1)
    k_row_head = jnp.right_shift(lax.broadcasted_iota(jnp.int32, (LANE, QUAD_LANES), 0), sub_shift)
    k_lane_head = jnp.right_shift(lax.broadcasted_iota(jnp.int32, (LANE, QUAD_LANES), 1), head_shift)
    r_mask = (lax.broadcasted_iota(jnp.int32, (16, QUAD_LANES), 0)
              == jnp.right_shift(lax.broadcasted_iota(jnp.int32, (16, QUAD_LANES), 1), head_shift))
    ones_bd = ones_ref[...]

    def sub_block(sb, carry):
        s0 = pl.multiple_of(sb * SUB_STEPS, SUB_STEPS)
        for q, (bi, hf) in enumerate(quads):
            ln = slice(hf * QUAD_LANES, (hf + 1) * QUAD_LANES)
            vq = _dot(vt_ref[bi, (QUAD * hf) * nh:(QUAD * hf + 1) * nh, :], place_ref[sb * QUAD])
            for h in range(1, QUAD):
                vq = vq + _dot(vt_ref[bi, (QUAD * hf + h) * nh:(QUAD * hf + h + 1) * nh, :], place_ref[sb * QUAD + h])
            vq_ref[q] = vq.astype(BF16)
            k_sub = k_ref[bi, pl.ds(s0, SUB_STEPS), ln].astype(F32)
            kq_ref[q] = jnp.where(k_row_head == k_lane_head, jnp.concatenate([k_sub] * QUAD, axis=0), 0.0).astype(BF16)

        def group(tg, vk_ref, carry):
            t0 = pl.multiple_of(s0 + tg * 8, 8)
            rows = []
            for bi, hf in quads:
                ln = slice(hf * QUAD_LANES, (hf + 1) * QUAD_LANES)
                rows.append(tuple(ref[bi, pl.ds(t0, 8), ln] for ref in (w_ref, kk_ref, b_ref, r_ref)))
            onehots = [jnp.where(sub_lane == tg * 8 + u, 1.0, 0.0).astype(BF16) for u in range(8)]
            for q in range(len(quads)):
                vq = vq_ref[q]
                vk_ref[q] = _dot(jnp.concatenate([vq * oh for oh in onehots], axis=0), kq_ref[q])
            state = [s_ref[q] for q in range(len(quads))]
            for u in range(8):
                sk = _dot(jnp.concatenate([(state[q] * rows[q][1][u:u + 1, :]).astype(BF16)
                                           for q in range(len(quads))], axis=0), ones_bd)
                for q, (bi, hf) in enumerate(quads):
                    w8, _, b8, r8 = rows[q]
                    state[q] = (state[q] * w8[u:u + 1, :] - sk[q * nh:(q + 1) * nh, :] * b8[u:u + 1, :]
                                + vk_ref[q, u * nh:(u + 1) * nh, :])
                    r_lhs = jnp.where(r_mask, r8[u:u + 1, :], 0.0).astype(BF16)
                    o = _dot_nt(r_lhs, state[q].astype(BF16))
                    o_ref[t0 + u, bi, QUAD * hf:QUAD * (hf + 1), :] = o[0:QUAD, :]
            for q in range(len(quads)):
                s_ref[q] = state[q]
            return carry

        for tg in range(SUB_STEPS // 8):
            carry = group(tg, (vk0_ref, vk1_ref)[tg % 2], carry)
        return carry

    lax.fori_loop(0, tt // SUB_STEPS, sub_block, 0)


def _rwkv_scan(r, w, k, kk, b, vt, ones_bd, place, bsz, t):
    tt = SCAN_TT
    wd = RWKV_WIDTH
    nquad = bsz * RWKV_HEADS // QUAD
    tok = pl.BlockSpec((bsz, tt, wd), lambda i: (0, i, 0))
    return pl.pallas_call(
        _scan_kernel,
        grid=(t // tt,),
        in_specs=[tok, tok, tok, tok, tok,
                  pl.BlockSpec((bsz, wd, tt), lambda i: (0, 0, i)),
                  pl.BlockSpec(ones_bd.shape, lambda i: (0, 0)),
                  pl.BlockSpec(place.shape, lambda i: (0, 0, 0))],
        out_specs=pl.BlockSpec((tt, bsz, RWKV_HEADS, RWKV_HEAD), lambda i: (i, 0, 0, 0)),
        out_shape=jax.ShapeDtypeStruct((t, bsz, RWKV_HEADS, RWKV_HEAD), F32),
        scratch_shapes=[pltpu.VMEM((nquad, RWKV_HEAD, QUAD_LANES), F32),
                        pltpu.VMEM((nquad, RWKV_HEAD, LANE), BF16),
                        pltpu.VMEM((nquad, LANE, QUAD_LANES), BF16),
                        pltpu.VMEM((nquad, 8 * RWKV_HEAD, QUAD_LANES), F32),
                        pltpu.VMEM((nquad, 8 * RWKV_HEAD, QUAD_LANES), F32)],
        compiler_params=_cparams(("arbitrary",)),
        name="rwkv_scan",
    )(r, w, k, kk, b, vt, ones_bd, place)


def _merge_kernel(x_ref, oc_ref, os_ref, ow_ref, orw_ref, bonus_ref, g_ref, zgn_ref, zgr_ref, gn_ref, hsum_ref,
                  pn_ref, pr_ref, wo_ref, ln_ref, o_ref):
    o_n = jnp.concatenate([oc_ref[0, p].astype(F32) + os_ref[0, p].astype(F32) + ow_ref[0, p].astype(F32)
                           for p in range(NSA_HEADS // 2)], axis=1)
    y_n = _dot(o_n.astype(BF16), pn_ref[...])
    o = orw_ref[...]
    mu = _dot_split(o, hsum_ref[...]) * (1.0 / RWKV_HEAD)
    d = o - mu
    var = _dot_split(d * d, hsum_ref[...]) * (1.0 / RWKV_HEAD)
    o = d * lax.rsqrt(var + GN_EPS) * gn_ref[0:1, :] + gn_ref[1:2, :]
    o = ((o + bonus_ref[...]) * g_ref[...]).astype(BF16)
    y1 = _sigmoid(zgn_ref[...]) * y_n + _sigmoid(zgr_ref[...]) * _dot(o, pr_ref[...])
    y = _dot(y1.astype(BF16), wo_ref[...])
    o_ref[...] = _layer_norm(DN_ALPHA * x_ref[...] + y, ln_ref[0:1, :], ln_ref[1:2, :])


def _merge(x, o_c, o_s, o_w, o_rw, bonus, g, z, gn, hsum, p_nsa, p_rwkv, w_out, ln, bsz, t, tm=256):
    n = bsz * t
    tps = t // tm
    d = D_MODEL
    w = RWKV_WIDTH
    head = pl.BlockSpec((1, NSA_HEADS // 2, tm, LANE), lambda i: (i // tps, 0, i % tps, 0))
    tok = pl.BlockSpec((tm, w), lambda i: (i, 0))

    def full(a):
        return pl.BlockSpec(a.shape, lambda i: (0,) * a.ndim)

    return pl.pallas_call(
        _merge_kernel,
        grid=(n // tm,),
        in_specs=[pl.BlockSpec((tm, d), lambda i: (i, 0)), head, head, head, tok, tok, tok,
                  pl.BlockSpec((tm, d), lambda i: (i, C_GN // d)),
                  pl.BlockSpec((tm, d), lambda i: (i, C_GR // d)),
                  full(gn), full(hsum), full(p_nsa), full(p_rwkv), full(w_out), full(ln)],
        out_specs=pl.BlockSpec((tm, d), lambda i: (i, 0)),
        out_shape=jax.ShapeDtypeStruct((n, d), F32),
        compiler_params=_cparams(("parallel",)),
        name="mix_merge_ln",
    )(x, o_c, o_s, o_w, o_rw, bonus, g, z, z, gn, hsum, p_nsa, p_rwkv, w_out, ln)


def _xattn_kernel(x_ref, kt_ref, v_ref, wq_ref, wo_ref, ln_ref, rw_ref, o_ref, ob_ref, s_ref):
    x = x_ref[...]
    q = _dot(x.astype(BF16), wq_ref[...]).astype(BF16)
    outs = []
    for h in range(XATTN_HEADS):
        hs = slice(h * XATTN_HEAD, (h + 1) * XATTN_HEAD)
        s = _dot(q[:, hs], kt_ref[0, hs, :]) * (XATTN_HEAD ** -0.5)
        m = jnp.max(s, axis=-1, keepdims=True)
        p = jnp.exp(s - m)
        l = jnp.sum(p, axis=-1, keepdims=True)
        outs.append(_dot(p.astype(BF16), v_ref[0, :, hs]) * (1.0 / l))
    o = jnp.concatenate(outs, axis=1).astype(BF16)
    x2 = _layer_norm(DN_ALPHA * x + _dot(o, wo_ref[...]), ln_ref[0:1, :], ln_ref[1:2, :])
    o_ref[...] = x2
    hi = x2.astype(BF16)
    ob_ref[...] = hi
    lo = (x2 - hi.astype(F32)).astype(BF16)
    logits = _dot(hi, rw_ref[0]) + (_dot(lo, rw_ref[0]) + _dot(hi, rw_ref[1]))
    s_ref[...] = _sigmoid(logits)


def _xattn(x, kt, v, wq, wo, ln, rw2, bsz, t, tm=256):
    n = bsz * t
    tps = t // tm
    d = D_MODEL

    def full(a):
        return pl.BlockSpec(a.shape, lambda i: (0,) * a.ndim)

    return pl.pallas_call(
        _xattn_kernel,
        grid=(n // tm,),
        in_specs=[pl.BlockSpec((tm, d), lambda i: (i, 0)),
                  pl.BlockSpec((1,) + kt.shape[1:], lambda i: (i // tps, 0, 0)),
                  pl.BlockSpec((1,) + v.shape[1:], lambda i: (i // tps, 0, 0)),
                  full(wq), full(wo), full(ln), full(rw2)],
        out_specs=[pl.BlockSpec((tm, d), lambda i: (i, 0)), pl.BlockSpec((tm, d), lambda i: (i, 0)),
                   pl.BlockSpec((tm, LANE), lambda i: (i, 0))],
        out_shape=[jax.ShapeDtypeStruct((n, d), F32), jax.ShapeDtypeStruct((n, d), BF16),
                   jax.ShapeDtypeStruct((n, LANE), F32)],
        compiler_params=_cparams(("parallel",)),
        name="xattn_ln_router",
    )(x, kt, v, wq, wo, ln, rw2)


def _expert_kernel(meta_ref, x_ref, gate_ref, *refs):
    w_refs, o_ref, w_scr = refs[:6], refs[6], refs[7:]
    i = pl.program_id(0)
    nb = pl.num_programs(0)
    prev = jnp.maximum(i - 1, 0)

    for side in range(2):
        @pl.when((i == 0) | (meta_ref[side * nb + i] != meta_ref[side * nb + prev]))
        def _():
            for j in range(3):
                w_scr[3 * side + j][...] = w_refs[3 * side + j][0, 0].astype(BF16)

    n_used = meta_ref[2 * nb]

    @pl.when(i < n_used)
    def _():
        x = x_ref[...]
        y = jnp.zeros(o_ref.shape, F32)
        for side in range(2):
            wg, wu, wd = w_scr[3 * side:3 * side + 3]
            hg = _dot(x, wg[...])
            h = hg * _sigmoid(hg) * _dot(x, wu[...])
            y = y + gate_ref[:, side:side + 1] * _dot(h.astype(BF16), wd[...])
        o_ref[...] = y.astype(o_ref.dtype)

    @pl.when(i >= n_used)
    def _():
        o_ref[...] = jnp.zeros(o_ref.shape, o_ref.dtype)


def _experts(meta, rows, gates, w_gate, w_up, w_down, layer):
    n_rows, d = rows.shape
    ff = w_gate.shape[3]
    nb = n_rows // MOE_BLK

    def weight(shape, side):
        return pl.BlockSpec((1, 1) + shape, lambda i, m: (layer, m[side * nb + i], 0, 0))

    return pl.pallas_call(
        _expert_kernel,
        grid_spec=pltpu.PrefetchScalarGridSpec(
            num_scalar_prefetch=1,
            grid=(nb,),
            in_specs=[pl.BlockSpec((MOE_BLK, d), lambda i, m: (i, 0)),
                      pl.BlockSpec((MOE_BLK, TOP_K), lambda i, m: (i, 0))]
                     + [weight(s, side) for side in range(2) for s in ((d, ff), (d, ff), (ff, d))],
            out_specs=pl.BlockSpec((MOE_BLK, d), lambda i, m: (i, 0)),
            scratch_shapes=[pltpu.VMEM(s, BF16) for _ in range(2) for s in ((d, ff), (d, ff), (ff, d))]),
        out_shape=jax.ShapeDtypeStruct((n_rows, d), BF16),
        compiler_params=_cparams(("arbitrary",)),
        name="moe_experts",
    )(meta, rows, gates, w_gate, w_up, w_down, w_gate, w_up, w_down)


def _combine_kernel(x_ref, y_ref, ln_ref, o_ref):
    o_ref[...] = _layer_norm(DN_ALPHA * x_ref[...] + y_ref[...], ln_ref[0:1, :], ln_ref[1:2, :])


def _combine(x, y, ln, tm=512):
    n, d = x.shape
    tok = pl.BlockSpec((tm, d), lambda i: (i, 0))
    return pl.pallas_call(
        _combine_kernel,
        grid=(n // tm,),
        in_specs=[tok, tok, pl.BlockSpec(ln.shape, lambda i: (0, 0))],
        out_specs=tok,
        out_shape=jax.ShapeDtypeStruct((n, d), F32),
        compiler_params=_cparams(("parallel",)),
        name="moe_combine_ln",
    )(x, y, ln)


def _t5_bucket(dist):
    n = jnp.maximum(dist, 0)
    max_exact = REL_BUCKETS // 2
    log_ratio = jnp.log(jnp.maximum(n, 1).astype(F32) / max_exact) / math.log(REL_MAX_DIST / max_exact)
    large = jnp.minimum(max_exact + (log_ratio * (REL_BUCKETS - max_exact)).astype(jnp.int32), REL_BUCKETS - 1)
    return jnp.where(n < max_exact, n, large)


def _bias_tables(rel_table, t):
    nc = t // CMP_STRIDE
    f = rel_table[_t5_bucket(jnp.arange(t + 2048))].astype(F32).T
    ql = jnp.arange(Q_BLOCK)[:, None]

    chunks_per_q = Q_BLOCK // CMP_STRIDE
    r = np.arange(CMP_STRIDE)[:, None]
    m = np.arange(2 * nc + chunks_per_q - 1)[None, :]
    dist_c = CMP_STRIDE * (nc - 8 + chunks_per_q - 1 - m) + r - (CMP_LEN - 1)
    small = jnp.where((dist_c >= 0)[None], jnp.take(f, np.clip(dist_c, 0, f.shape[-1] - 1), axis=1), NEG_INF)
    pc = jnp.stack([small[:, :, chunks_per_q - 1 - a:chunks_per_q - 1 - a + 2 * nc] for a in range(chunks_per_q)], axis=1)
    pc = pc.reshape(NSA_GROUPS, NSA_HPG * Q_BLOCK, 2 * nc)
    pc = jnp.stack([pc[:, :, s:s + 2 * nc - LANE] for s in range(0, LANE, 8)], axis=1)
    nvar = WINDOW // Q_BLOCK + 1

    def toeplitz(base, width, hi, shift):
        period = width + Q_BLOCK
        m = np.arange(period)
        dist = base - np.where(m < width, m, m - period)
        row = jnp.take(f, np.clip(dist, 0, f.shape[-1] - 1), axis=1) - shift
        row = jnp.where(((dist >= 0) & (dist < hi))[None], row, NEG_INF)
        flat = jnp.tile(row, (1, Q_BLOCK))[:, :Q_BLOCK * (period - 1)]
        return flat.reshape(NSA_HEADS, Q_BLOCK, period - 1)[:, :, :width]

    wb = jnp.stack([toeplitz(Q_BLOCK * v, WINDOW + Q_BLOCK, WINDOW, 0.0) for v in range(nvar)], axis=1)
    wb = wb.reshape(NSA_GROUPS, NSA_HPG, nvar, Q_BLOCK, -1).transpose(0, 2, 1, 3, 4)
    wb = wb.reshape(NSA_GROUPS, nvar, NSA_HPG * Q_BLOCK, -1)
    far = rel_table[REL_BUCKETS - 1].astype(F32)
    lt = toeplitz(SEL_NEAR + SEL_TK, SEL_NEAR + 2 * SEL_TK + Q_BLOCK, f.shape[-1], far[:, None])
    lt = lt.reshape(NSA_GROUPS, NSA_HPG * Q_BLOCK, -1)
    far_hi = far.astype(BF16).astype(F32)
    lane = jnp.arange(LANE)[None, :]
    cq = jnp.where(lane == HEAD_DIM, far_hi[:, None], jnp.where(lane == HEAD_DIM + 1, (far - far_hi)[:, None], 0.0))
    cq = jnp.broadcast_to(cq[:, None, :], (NSA_HEADS, Q_BLOCK, LANE)).reshape(NSA_GROUPS, NSA_HPG * Q_BLOCK, LANE)
    return pc, wb, lt, cq


def _in_proj_perm():
    perm = np.full((IN_PAD,), -1, np.int64)
    for h in range(NSA_HEADS):
        perm[C_Q + h * LANE:C_Q + h * LANE + HEAD_DIM] = np.arange(HEAD_DIM) + h * HEAD_DIM
    rw0 = NSA_COLS
    g0 = NSA_COLS + RWKV_COLS
    perm[C_GN:C_GN + D_MODEL] = g0 + np.arange(D_MODEL)
    perm[C_GR:C_GR + D_MODEL] = g0 + D_MODEL + np.arange(D_MODEL)
    perm[C_R:C_R + RWKV_WIDTH] = rw0 + np.arange(RWKV_WIDTH)
    perm[C_K:C_K + RWKV_WIDTH] = rw0 + RWKV_WIDTH + np.arange(RWKV_WIDTH)
    perm[C_V:C_V + RWKV_WIDTH] = rw0 + 2 * RWKV_WIDTH + np.arange(RWKV_WIDTH)
    for g in range(NSA_GROUPS):
        for br in range(3):
            for h in range(NSA_HPG):
                perm[C_G3 + g * LANE + br * NSA_HPG + h] = NSA_WIDTH + 6 * NSA_KV + (g * NSA_HPG + h) * 3 + br
    perm[C_WA:C_WA + LORA_W + LORA_A] = rw0 + 3 * RWKV_WIDTH + np.arange(LORA_W + LORA_A)
    perm[C_ZG:C_ZG + LORA_G] = rw0 + 3 * RWKV_WIDTH + LORA_W + LORA_A + np.arange(LORA_G)
    return perm


_PERM = _in_proj_perm()


def _permute_in_proj(w):
    pieces = []
    start = 0
    while start < IN_PAD:
        stop = start + 1
        if _PERM[start] < 0:
            while stop < IN_PAD and _PERM[stop] < 0:
                stop += 1
            pieces.append(jnp.zeros((w.shape[0], stop - start), BF16))
        else:
            while stop < IN_PAD and _PERM[stop] == _PERM[stop - 1] + 1:
                stop += 1
            pieces.append(w[:, int(_PERM[start]):int(_PERM[stop - 1]) + 1].astype(BF16))
        start = stop
    return jnp.concatenate(pieces, axis=1)


def _pad_to(a, shape):
    return jnp.pad(a, [(0, s - d) for d, s in zip(a.shape, shape)])


_CLASS_EXPERTS = np.asarray([(grp * EXPERTS_PER_GROUP + i, grp * EXPERTS_PER_GROUP + j)
                             for grp in range(N_EXPERT_GROUPS)
                             for i in range(EXPERTS_PER_GROUP) for j in range(i + 1, EXPERTS_PER_GROUP)], np.int32)
N_PAIR_CLASSES = len(_CLASS_EXPERTS)


def _route(s, router_bias, n_tok):
    s16 = s[:, :N_EXPERTS]
    s_sel = s16 + router_bias.astype(F32)

    def top2(a):
        i1 = jnp.argmax(a, axis=-1)
        rest = jnp.where(jnp.arange(a.shape[-1]) == i1[..., None], -jnp.inf, a)
        i2 = jnp.argmax(rest, axis=-1)
        return jnp.max(a, axis=-1) + jnp.max(rest, axis=-1), jnp.stack([i1, i2], axis=-1)

    group_score, _ = top2(s_sel.reshape(n_tok, N_EXPERT_GROUPS, EXPERTS_PER_GROUP))
    group = jnp.argmax(group_score, axis=-1)
    in_group = (jnp.arange(N_EXPERTS) // EXPERTS_PER_GROUP)[None, :] == group[:, None]
    _, e_idx = top2(jnp.where(in_group, s_sel, NEG_INF))
    picked = jnp.arange(N_EXPERTS)[None, None, :] == e_idx[:, :, None]
    s_top = jnp.sum(jnp.where(picked, s16[:, None, :], 0.0), axis=-1)
    gate = s_top / jnp.sum(s_top, axis=-1, keepdims=True)
    swap = e_idx[:, 0] > e_idx[:, 1]
    e_lo = jnp.where(swap, e_idx[:, 1], e_idx[:, 0])
    e_hi = jnp.where(swap, e_idx[:, 0], e_idx[:, 1])
    gate_lohi = jnp.where(swap[:, None], gate[:, ::-1], gate)
    l_lo, l_hi = e_lo % EXPERTS_PER_GROUP, e_hi % EXPERTS_PER_GROUP
    pairs_per_group = N_PAIR_CLASSES // N_EXPERT_GROUPS
    cls = ((e_lo // EXPERTS_PER_GROUP) * pairs_per_group
           + (l_lo * (2 * EXPERTS_PER_GROUP - 1 - l_lo)) // 2 + (l_hi - l_lo - 1))
    onehot = (cls[:, None] == jnp.arange(N_PAIR_CLASSES)[None, :]).astype(jnp.int32)
    counts = jnp.sum(onehot, axis=0)
    padded = (counts + MOE_BLK - 1) // MOE_BLK * MOE_BLK
    ends = jnp.cumsum(padded)
    before = jnp.cumsum(onehot, axis=0) - onehot + (ends - padded)[None, :]
    dest = jnp.sum(onehot * before, axis=1)
    n_rows = n_tok + N_PAIR_CLASSES * MOE_BLK
    tok_of_row = jnp.zeros((n_rows,), jnp.int32).at[dest].set(jnp.arange(n_tok, dtype=jnp.int32))
    block_start = jnp.arange(n_rows // MOE_BLK) * MOE_BLK
    block_cls = jnp.minimum(jnp.sum((ends[None, :] <= block_start[:, None]).astype(jnp.int32), axis=1),
                            N_PAIR_CLASSES - 1)
    block_onehot = (block_cls[:, None] == jnp.arange(N_PAIR_CLASSES)[None, :]).astype(jnp.int32)
    block_experts = jnp.sum(block_onehot[:, :, None] * jnp.asarray(_CLASS_EXPERTS)[None], axis=1)
    meta = jnp.concatenate([block_experts[:, 0], block_experts[:, 1], ends[-1:] // MOE_BLK]).astype(jnp.int32)
    return gate_lohi, dest, tok_of_row, meta


def kernel(x, mem, rel_table, router_w, router_bias, w_in, cmp_pe_k, cmp_pe_v, cmp_w1k, cmp_w2k, cmp_w1v, cmp_w2v, rwkv_mu, rwkv_w0, rwkv_w2, rwkv_a0, rwkv_a2, rwkv_g2, rwkv_kk, rwkv_ka, rwkv_rk, rwkv_gn_g, rwkv_gn_b, rwkv_v0, rwkv_v1, rwkv_v2, p_nsa, p_rwkv, w_out, ln1_g, ln1_b, xq_w, xk_w, xv_w, xo_w, ln2_g, ln2_b, moe_w_gate, moe_w_up, moe_w_down, ln3_g, ln3_b):
    bsz, t, d = x.shape
    n = bsz * t
    depth = w_in.shape[0]
    nc = t // CMP_STRIDE
    nsel = t // SEL_BLOCK
    g, hd = NSA_GROUPS, HEAD_DIM

    pc_tab, wb_tab, lt_tab, cq_tab = _bias_tables(rel_table, t)
    cmp_start = np.arange(nc) * CMP_STRIDE
    sel_start = np.arange(nsel) * SEL_BLOCK
    overlap = ((cmp_start[:, None] <= sel_start[None, :] + SEL_BLOCK - 1)
               & (cmp_start[:, None] + CMP_LEN - 1 >= sel_start[None, :]) & (cmp_start[:, None] < (nc - 1) * CMP_STRIDE))
    overlap = jnp.asarray(overlap.T, BF16)
    lane_head = np.arange(RWKV_WIDTH) // RWKV_HEAD
    hsum = jnp.asarray(lane_head[:, None] == lane_head[None, :], BF16)
    quad_head = np.arange(QUAD_LANES) // RWKV_HEAD
    ones_bd = jnp.asarray(quad_head[:, None] == quad_head[None, :], BF16)
    step = np.arange(SCAN_TT)[None, :, None]
    lane = np.arange(LANE)[None, None, :]
    sbh = np.arange(SCAN_TT // SUB_STEPS * QUAD)[:, None, None]
    place = jnp.asarray((lane // SUB_STEPS == sbh % QUAD) & (step == (sbh // QUAD) * SUB_STEPS + lane % SUB_STEPS), BF16)
    rw_f = _pad_to(router_w.astype(F32), (d, LANE))
    rw_hi = rw_f.astype(BF16)
    rw2 = jnp.stack([rw_hi, (rw_f - rw_hi.astype(F32)).astype(BF16)])

    xf = x.reshape(n, d)
    mem_f = mem.reshape(bsz * mem.shape[1], d)
    v_first = None
    for l in range(depth):
        w_in_l = _permute_in_proj(w_in[l])
        z = _matmul(xf, w_in_l, F32, 1024, 1024)

        kv0 = NSA_WIDTH
        zeros64 = jnp.zeros((d, LANE - hd), BF16)
        w_kv = jnp.stack([jnp.concatenate(
            [piece for i in range(KV_ARRAYS)
             for piece in (w_in[l][:, kv0 + i * NSA_KV + gi * hd:kv0 + i * NSA_KV + (gi + 1) * hd].astype(BF16), zeros64)],
            axis=1) for gi in range(g)])
        kc, vc, ks_aug, vs_aug, kw_pad, vw_aug = _kv_proj(xf, w_kv, bsz, t)

        def cmp_w(pe, w1, w2):
            return (pe.reshape(2, CMP_STRIDE * hd), w1.reshape(2, CMP_STRIDE * hd, CMP_HIDDEN).astype(BF16),
                    _pad_to(w2, (CMP_HIDDEN, LANE)).astype(BF16))

        chunks = lambda a: a.reshape(bsz, g, nc, CMP_STRIDE * hd)
        k_cmp = _compress(chunks(kc), *cmp_w(cmp_pe_k[l], cmp_w1k[l], cmp_w2k[l]))
        v_cmp = _compress(chunks(vc), *cmp_w(cmp_pe_v[l], cmp_w1v[l], cmp_w2v[l]))
        o_c, sel_mask = _cmp_branch(z, k_cmp, v_cmp, pc_tab, overlap, bsz, t)
        o_s = _sel_branch(z, ks_aug, vs_aug, sel_mask, lt_tab, cq_tab, bsz, t)
        o_w = _win_branch(z, kw_pad, vw_aug, wb_tab, bsz, t)

        mu = rwkv_mu[l]
        w3 = 3 * RWKV_WIDTH
        mu5 = jnp.stack([mu[0:512], mu[512:1024], mu[1024:1536], _pad_to(mu[w3:w3 + 128], (512,)),
                         _pad_to(mu[w3 + 128:w3 + 256], (512,))])
        vecs = jnp.stack([rwkv_w0[l], rwkv_a0[l], rwkv_kk[l], rwkv_ka[l], rwkv_rk[l].reshape(-1)])
        w2p = jnp.concatenate([rwkv_w2[l], jnp.zeros_like(rwkv_a2[l])], axis=0).astype(BF16)
        a2p = jnp.concatenate([jnp.zeros_like(rwkv_w2[l]), rwkv_a2[l]], axis=0).astype(BF16)
        vres = None
        if l > 0:
            vres = (v_first, rwkv_v0[l - 1][None, :], _pad_to(rwkv_v1[l - 1], (RWKV_WIDTH, LANE)).astype(BF16),
                    _pad_to(rwkv_v2[l - 1], (LANE, RWKV_WIDTH)).astype(BF16))
        r, w, k, v, kk, b, g_out, bonus, vt = _rwkv_prep(z, mu5, vecs, w2p, a2p, rwkv_g2[l].astype(BF16), hsum, vres,
                                                          bsz, t)
        if l == 0:
            v_first = v
        seq = lambda a: a.reshape(bsz, t, RWKV_WIDTH)
        o_rw = _rwkv_scan(seq(r), seq(w), seq(k), seq(kk), seq(b), vt, ones_bd, place, bsz, t)
        o_rw = o_rw.transpose(1, 0, 2, 3).reshape(n, RWKV_WIDTH)

        xf = _merge(xf, o_c, o_s, o_w, o_rw, bonus, g_out, z, jnp.stack([rwkv_gn_g[l], rwkv_gn_b[l]]), hsum,
                    p_nsa[l].astype(BF16), p_rwkv[l].astype(BF16), w_out[l].astype(BF16),
                    jnp.stack([ln1_g[l], ln1_b[l]]), bsz, t)

        mlen = mem.shape[1]
        mk = _matmul(mem_f, xk_w[l].astype(BF16), BF16, mlen, XATTN_WIDTH)
        mv = _matmul(mem_f, xv_w[l].astype(BF16), BF16, mlen, XATTN_WIDTH)
        kt = mk.reshape(bsz, mlen, XATTN_WIDTH).transpose(0, 2, 1)
        xf, xf_b, scores = _xattn(xf, kt, mv.reshape(bsz, mlen, XATTN_WIDTH), xq_w[l].astype(BF16), xo_w[l].astype(BF16),
                            jnp.stack([ln2_g[l], ln2_b[l]]), rw2, bsz, t)

        gate, dest, tok_of_row, meta = _route(scores, router_bias, n)
        rows = jnp.take(xf_b, tok_of_row, axis=0)
        y_rows = _experts(meta, rows, jnp.take(gate, tok_of_row, axis=0), moe_w_gate, moe_w_up, moe_w_down, l)
        xf = _combine(xf, jnp.take(y_rows, dest, axis=0), jnp.stack([ln3_g[l], ln3_b[l]]))
    return xf.reshape(bsz, t, d)
```

```python
import functools
import math

import numpy as np
import jax
import jax.numpy as jnp
from jax import lax
from jax.experimental import pallas as pl
from jax.experimental.pallas import tpu as pltpu

F32 = jnp.float32
BF16 = jnp.bfloat16

D_MODEL = 1024
DEPTH = 2
NSA_HEADS = 8
NSA_GROUPS = 2
NSA_HPG = NSA_HEADS // NSA_GROUPS
HEAD_DIM = 64
NSA_WIDTH = NSA_HEADS * HEAD_DIM
NSA_KV = NSA_GROUPS * HEAD_DIM
CMP_STRIDE = 16
CMP_LEN = 2 * CMP_STRIDE
CMP_HIDDEN = 256
SEL_BLOCK = 64
SEL_TOPN = 16
WINDOW = 512
Q_BLOCK = 128
SEL_FORCE = 1e9
RWKV_HEADS = 8
RWKV_HEAD = 64
RWKV_WIDTH = RWKV_HEADS * RWKV_HEAD
LORA_W = 64
LORA_A = 64
LORA_V = 32
LORA_G = 128
GN_EPS = 64e-5
REL_BUCKETS = 32
REL_MAX_DIST = 1024
XATTN_HEADS = 4
XATTN_HEAD = 128
XATTN_WIDTH = XATTN_HEADS * XATTN_HEAD
N_EXPERTS = 16
N_EXPERT_GROUPS = 4
EXPERTS_PER_GROUP = N_EXPERTS // N_EXPERT_GROUPS
TOP_K = 2
EXPERT_FF = 512
DN_ALPHA = (2 * DEPTH) ** 0.25
LN_EPS = 1e-5
NEG_INF = -1e30
NSA_COLS = NSA_WIDTH + 6 * NSA_KV + 3 * NSA_HEADS
RWKV_COLS = 3 * RWKV_WIDTH + LORA_W + LORA_A + LORA_G

LANE = 128
VMEM_LIMIT = 56 * 1024 * 1024

C_Q = 0
C_GN = 1024
C_GR = 2048
C_R = 3072
C_K = 3584
C_V = 4096
C_G3 = 4608
C_WA = 4864
C_ZG = 4992
IN_PAD = 5120
KV_ARRAYS = 6

MOE_BLK = 256
SCAN_TT = 128
FAR_BIAS_DIST = 1280


def _cparams(sem):
    return pltpu.CompilerParams(dimension_semantics=sem, vmem_limit_bytes=VMEM_LIMIT)


def _sigmoid(x):
    return 1.0 / (1.0 + jnp.exp(-x))


def _layer_norm(v, g, b):
    mu = jnp.mean(v, axis=-1, keepdims=True)
    d = v - mu
    var = jnp.mean(d * d, axis=-1, keepdims=True)
    return d * lax.rsqrt(var + LN_EPS) * g + b


def _dot(a, b):
    return jnp.dot(a, b, preferred_element_type=F32)


def _dot_nt(a, b):
    return lax.dot_general(a, b, (((1,), (1,)), ((), ())), preferred_element_type=F32)


def _dot_split(x, w):
    hi = x.astype(BF16)
    lo = (x - hi.astype(F32)).astype(BF16)
    return _dot(hi, w) + _dot(lo, w)


def _mm_kernel(x_ref, w_ref, o_ref, xb_ref):
    @pl.when(pl.program_id(1) == 0)
    def _():
        xb_ref[...] = x_ref[...].astype(BF16)

    o_ref[...] = _dot(xb_ref[...], w_ref[...]).astype(o_ref.dtype)


def _matmul(x, w, out_dtype, tm, tn):
    n, k = x.shape
    m = w.shape[1]
    return pl.pallas_call(
        _mm_kernel,
        grid=(n // tm, m // tn),
        in_specs=[pl.BlockSpec((tm, k), lambda i, j: (i, 0)),
                  pl.BlockSpec((k, tn), lambda i, j: (0, j))],
        out_specs=pl.BlockSpec((tm, tn), lambda i, j: (i, j)),
        out_shape=jax.ShapeDtypeStruct((n, m), out_dtype),
        scratch_shapes=[pltpu.VMEM((tm, k), BF16)],
        compiler_params=_cparams(("parallel", "arbitrary")),
        name="matmul",
    )(x, w)


def _kv_proj_kernel(tiles_per_seq, x_ref, w_ref, kc_ref, vc_ref, ks_ref, vs_ref, kw_ref, vw_ref, xb_ref):
    @pl.when(pl.program_id(1) == 0)
    def _():
        xb_ref[...] = x_ref[...].astype(BF16)

    y = _dot(xb_ref[...], w_ref[pl.program_id(1)])
    tm = y.shape[0]
    slot = lambda i: y[:, i * LANE:(i + 1) * LANE]
    lane = lax.broadcasted_iota(jnp.int32, (tm, LANE), 1)
    one_lane = jnp.where(lane == HEAD_DIM, 1.0, 0.0)
    t0 = lax.rem(pl.program_id(0), tiles_per_seq) * tm
    block = jnp.right_shift(t0 + lax.broadcasted_iota(jnp.int32, (tm, LANE), 0), int(math.log2(SEL_BLOCK)))
    kc_ref[0, 0] = slot(0)[:, :HEAD_DIM]
    vc_ref[0, 0] = slot(1)[:, :HEAD_DIM]
    ks_ref[0, 0] = jnp.concatenate([slot(2) + jnp.where(lane == HEAD_DIM + 1, 1.0, one_lane),
                                    jnp.where(lane == block, 1.0, 0.0)], axis=1).astype(BF16)
    vs_ref[0, 0] = (slot(3) + one_lane).astype(BF16)
    kw_ref[0, 0] = slot(4).astype(BF16)
    vw_ref[0, 0] = (slot(5) + one_lane).astype(BF16)


def _kv_proj(x, w_kv, bsz, t, tm=512):
    n, d = x.shape
    tps = t // tm
    g = NSA_GROUPS

    def out(width, dtype):
        return (pl.BlockSpec((1, 1, tm, width), lambda i, j: (i // tps, j, i % tps, 0)),
                jax.ShapeDtypeStruct((bsz, g, t, width), dtype))

    outs = [out(HEAD_DIM, F32), out(HEAD_DIM, F32), out(2 * LANE, BF16), out(LANE, BF16), out(LANE, BF16),
            out(LANE, BF16)]
    return pl.pallas_call(
        functools.partial(_kv_proj_kernel, tps),
        grid=(n // tm, g),
        in_specs=[pl.BlockSpec((tm, d), lambda i, j: (i, 0)),
                  pl.BlockSpec((g, d, KV_ARRAYS * LANE), lambda i, j: (0, 0, 0))],
        out_specs=[o[0] for o in outs],
        out_shape=[o[1] for o in outs],
        scratch_shapes=[pltpu.VMEM((tm, d), BF16)],
        compiler_params=_cparams(("parallel", "arbitrary")),
        name="nsa_kv_proj",
    )(x, w_kv)


def _compress_kernel(u_ref, pe_ref, w1_ref, w2_ref, o_ref):
    u = u_ref[0, 0]
    a = _dot((u + pe_ref[0:1, :]).astype(BF16), w1_ref[0])
    b = _dot((u + pe_ref[1:2, :]).astype(BF16), w1_ref[1])
    nc = u.shape[0]
    h = a + pltpu.roll(b, nc - 1, 0)
    h = 0.5 * h * (1.0 + jnp.tanh(math.sqrt(2.0 / math.pi) * (h + 0.044715 * (h * h * h))))
    y = _dot(h.astype(BF16), w2_ref[...])
    row = lax.broadcasted_iota(jnp.int32, y.shape, 0)
    o_ref[0, 0] = jnp.where(row < nc - 1, y, 0.0)


def _compress(u, pe2, w1, w2p):
    b, g, nc, _ = u.shape
    return pl.pallas_call(
        _compress_kernel,
        grid=(b, g),
        in_specs=[pl.BlockSpec((1, 1, nc, 1024), lambda i, j: (i, j, 0, 0)),
                  pl.BlockSpec((2, 1024), lambda i, j: (0, 0)),
                  pl.BlockSpec((2, 1024, CMP_HIDDEN), lambda i, j: (0, 0, 0)),
                  pl.BlockSpec((CMP_HIDDEN, LANE), lambda i, j: (0, 0))],
        out_specs=pl.BlockSpec((1, 1, nc, LANE), lambda i, j: (i, j, 0, 0)),
        out_shape=jax.ShapeDtypeStruct((b, g, nc, LANE), F32),
        compiler_params=_cparams(("parallel", "parallel")),
        name="nsa_compress",
    )(u, pe2, w1, w2p)


def _stack_heads(q_ref, row0=0):
    q = q_ref[row0:row0 + Q_BLOCK, :] * (HEAD_DIM ** -0.5)
    return jnp.concatenate([q[:, h * LANE:(h + 1) * LANE] for h in range(NSA_HPG)], axis=0).astype(BF16)


def _store_heads(o_ref, o, gates, branch, row0=0):
    lane = lax.broadcasted_iota(jnp.int32, (Q_BLOCK, LANE), 1)
    for pair in range(NSA_HPG // 2):
        halves = []
        for h in (2 * pair, 2 * pair + 1):
            c = branch * NSA_HPG + h
            halves.append(o[h * Q_BLOCK:(h + 1) * Q_BLOCK, :] * gates[:, c:c + 1])
        o_ref[0, pair, row0:row0 + Q_BLOCK, :] = jnp.where(lane < HEAD_DIM, halves[0],
                                                           pltpu.roll(halves[1], HEAD_DIM, 1)).astype(o_ref.dtype)


CMP_QB = 2


def _cmp_kernel(q_ref, zg_ref, kc_ref, vc_ref, pc0_ref, pc1_ref, ov_ref, o_ref, m_ref, *scratch):
    nc = kc_ref.shape[2]
    nsel = ov_ref.shape[0]
    kc = kc_ref[0, 0].astype(BF16)
    vc = vc_ref[0, 0].astype(BF16)
    gates = _sigmoid(zg_ref[...])
    blocks = [(qb, pl.program_id(2) * CMP_QB + qb, pc_ref, scratch[2 * qb], scratch[2 * qb + 1])
              for qb, pc_ref in enumerate((pc0_ref, pc1_ref))]
    for qb, c, _, p_s, _ in blocks:
        p_s[...] = _dot_nt(_stack_heads(q_ref, qb * Q_BLOCK), kc)
    for qb, c, pc_ref, p_s, pb_s in blocks:
        off = pl.multiple_of(lax.div((nc - 8) - 8 * c, LANE) * LANE, LANE)
        for r in range(NSA_HPG * Q_BLOCK // SEL_ROWS):
            rows = slice(r * SEL_ROWS, (r + 1) * SEL_ROWS)
            s = p_s[rows, :] + pc_ref[0, 0, rows, pl.ds(off, nc)]
            e = jnp.exp(s - jnp.max(s, axis=-1, keepdims=True))
            l = jnp.sum(e, axis=-1, keepdims=True)
            ql = (lax.broadcasted_iota(jnp.int32, (SEL_ROWS, 1), 0) + r * SEL_ROWS) & (Q_BLOCK - 1)
            p = e * jnp.where((c * Q_BLOCK + ql) >= (CMP_LEN - 1), 1.0 / l, 0.0)
            p_s[rows, :] = p
            pb_s[rows, :] = p.astype(BF16)
    vals = []
    jb = lax.broadcasted_iota(jnp.int32, (nsel, Q_BLOCK), 0)
    jbf = jb.astype(F32)
    for qb, c, _, p_s, pb_s in blocks:
        _store_heads(o_ref, _dot(pb_s[...], vc), gates[qb * Q_BLOCK:(qb + 1) * Q_BLOCK, :], 0, qb * Q_BLOCK)
        psum = ((p_s[0:Q_BLOCK, :] + p_s[Q_BLOCK:2 * Q_BLOCK, :])
                + (p_s[2 * Q_BLOCK:3 * Q_BLOCK, :] + p_s[3 * Q_BLOCK:4 * Q_BLOCK, :]))
        hi = psum.astype(BF16)
        lo = (psum - hi.astype(F32)).astype(BF16)
        imp = _dot_nt(ov_ref[...], hi) + _dot_nt(ov_ref[...], lo)
        tq = c * Q_BLOCK + lax.broadcasted_iota(jnp.int32, (nsel, Q_BLOCK), 1)
        jq = jnp.right_shift(tq, int(math.log2(SEL_BLOCK)))
        forced = (jb == 0) | (jb == jq) | (jb == jq - 1)
        vals.append(jnp.where(forced, SEL_FORCE, jnp.where(jb <= jq, imp, -SEL_FORCE)))
    sels = [jnp.zeros((nsel, Q_BLOCK), F32) for _ in blocks]
    for _ in range(SEL_TOPN):
        for qb in range(CMP_QB):
            mx = jnp.max(vals[qb], axis=0, keepdims=True)
            first = jnp.min(jnp.where(vals[qb] == mx, jbf, float(nsel)), axis=0, keepdims=True)
            hit = jbf == first
            sels[qb] = jnp.where(hit, jnp.where(mx >= 0.0, 1.0, 0.0), sels[qb])
            vals[qb] = jnp.where(hit, -3e38, vals[qb])
    for qb in range(CMP_QB):
        m_ref[0, 0, qb * Q_BLOCK:(qb + 1) * Q_BLOCK, :] = (sels[qb].T - 1.0) * (-NEG_INF)


def _cmp_branch(z, k_cmp, v_cmp, pc_tab, overlap, bsz, t):
    tq = CMP_QB * Q_BLOCK
    nq = t // tq
    nc = k_cmp.shape[2]
    nsel = overlap.shape[0]
    rows = NSA_HPG * Q_BLOCK

    def variant(qb):
        return pl.BlockSpec((1, 1, rows, 2 * nc - LANE),
                            lambda b, g, c: (g, lax.rem((nc - 8) - 8 * (CMP_QB * c + qb), LANE) // 8, 0, 0))

    return pl.pallas_call(
        _cmp_kernel,
        grid=(bsz, NSA_GROUPS, nq),
        in_specs=[pl.BlockSpec((tq, NSA_HPG * LANE), lambda b, g, c: (b * nq + c, g)),
                  pl.BlockSpec((tq, LANE), lambda b, g, c: (b * nq + c, C_G3 // LANE + g)),
                  pl.BlockSpec((1, 1, nc, LANE), lambda b, g, c: (b, g, 0, 0)),
                  pl.BlockSpec((1, 1, nc, LANE), lambda b, g, c: (b, g, 0, 0)),
                  variant(0), variant(1),
                  pl.BlockSpec((nsel, nc), lambda b, g, c: (0, 0))],
        out_specs=[pl.BlockSpec((1, NSA_HPG // 2, tq, LANE), lambda b, g, c: (b, g, c, 0)),
                   pl.BlockSpec((1, 1, tq, nsel), lambda b, g, c: (b, g, c, 0))],
        out_shape=[jax.ShapeDtypeStruct((bsz, NSA_HEADS // 2, t, LANE), BF16),
                   jax.ShapeDtypeStruct((bsz, NSA_GROUPS, t, nsel), F32)],
        scratch_shapes=[pltpu.VMEM((rows, nc), F32), pltpu.VMEM((rows, nc), BF16)] * CMP_QB,
        compiler_params=_cparams(("parallel", "parallel", "arbitrary")),
        name="nsa_cmp_select",
    )(z, z, k_cmp, v_cmp, pc_tab, pc_tab, overlap)


WIN_QB = 2


def _win_kernel(q_ref, zg_ref, k_ref, v_ref, wb_ref, o_ref, *scratch):
    nvar = wb_ref.shape[1]
    gates = _sigmoid(zg_ref[...])
    blocks = []
    for qb in range(WIN_QB):
        c = pl.program_id(2) * WIN_QB + qb
        start = pl.multiple_of(jnp.maximum(c * Q_BLOCK - WINDOW, 0), Q_BLOCK)
        blocks.append((qb, jnp.minimum(c, nvar - 1), pl.ds(start, WINDOW + Q_BLOCK), scratch[2 * qb], scratch[2 * qb + 1]))
    for qb, _, keys, s_s, _ in blocks:
        s_s[...] = _dot_nt(_stack_heads(q_ref, qb * Q_BLOCK), k_ref[0, 0, keys, :])
    for qb, var, _, s_s, p_s in blocks:
        for r in range(NSA_HPG * Q_BLOCK // SEL_ROWS):
            rows = slice(r * SEL_ROWS, (r + 1) * SEL_ROWS)
            s = s_s[rows, :] + wb_ref[0, var, rows, :]
            p_s[rows, :] = jnp.exp(s - jnp.max(s, axis=-1, keepdims=True)).astype(BF16)
    for qb, _, keys, _, p_s in blocks:
        o = _dot(p_s[...], v_ref[0, 0, keys, :])
        o = o * (1.0 / o[:, HEAD_DIM:HEAD_DIM + 1])
        _store_heads(o_ref, o, gates[qb * Q_BLOCK:(qb + 1) * Q_BLOCK, :], 2, qb * Q_BLOCK)


def _win_branch(z, kw, vw, wb_tab, bsz, t):
    tq = WIN_QB * Q_BLOCK
    nq = t // tq
    rows, width = NSA_HPG * Q_BLOCK, WINDOW + Q_BLOCK
    return pl.pallas_call(
        _win_kernel,
        grid=(bsz, NSA_GROUPS, nq),
        in_specs=[pl.BlockSpec((tq, NSA_HPG * LANE), lambda b, g, c: (b * nq + c, g)),
                  pl.BlockSpec((tq, LANE), lambda b, g, c: (b * nq + c, C_G3 // LANE + g)),
                  pl.BlockSpec((1, 1, t, LANE), lambda b, g, c: (b, g, 0, 0)),
                  pl.BlockSpec((1, 1, t, LANE), lambda b, g, c: (b, g, 0, 0)),
                  pl.BlockSpec((1,) + wb_tab.shape[1:], lambda b, g, c: (g, 0, 0, 0))],
        out_specs=pl.BlockSpec((1, NSA_HPG // 2, tq, LANE), lambda b, g, c: (b, g, c, 0)),
        out_shape=jax.ShapeDtypeStruct((bsz, NSA_HEADS // 2, t, LANE), BF16),
        scratch_shapes=[pltpu.VMEM((rows, width), F32), pltpu.VMEM((rows, width), BF16)] * WIN_QB,
        compiler_params=_cparams(("parallel", "parallel", "arbitrary")),
        name="nsa_window",
    )(z, z, kw, vw, wb_tab)


SEL_TK = 512


SEL_ROWS = 64
SEL_QB = 2
SEL_NEAR = FAR_BIAS_DIST + Q_BLOCK


def _sel_kernel(q_ref, zg_ref, k_ref, v_ref, mt_ref, lt_ref, cq_ref, o_ref, m_s, acc_s, s0_s, s1_s, p0_s, p1_s,
                a0_s, a1_s):
    rows_per_block = NSA_HPG * Q_BLOCK
    c0 = pl.program_id(2) * SEL_QB
    q = q_ref[...] * (HEAD_DIM ** -0.5)
    parts = []
    for qb in range(SEL_QB):
        qrows = slice(qb * Q_BLOCK, (qb + 1) * Q_BLOCK)
        qs = jnp.concatenate([q[qrows, h * LANE:(h + 1) * LANE] for h in range(NSA_HPG)], axis=0) + cq_ref[0]
        parts.append(jnp.concatenate([qs, jnp.concatenate([mt_ref[0, 0, qrows, :]] * NSA_HPG, axis=0)], axis=1))
    q_aug = jnp.concatenate(parts, axis=0).astype(BF16)
    m_s[...] = jnp.full(m_s.shape, NEG_INF, F32)
    acc_s[...] = jnp.zeros(acc_s.shape, F32)
    ncol = SEL_TK // LANE
    n_tiles = lax.div(c0 + SEL_QB - 1, SEL_TK // Q_BLOCK) + 1

    def key_rows(i):
        return pl.ds(pl.multiple_of(jnp.minimum(i, n_tiles - 1) * SEL_TK, SEL_TK), SEL_TK)

    def scores(i, dst):
        dst[...] = _dot_nt(q_aug, k_ref[0, 0, key_rows(i), :])

    def tile(i, cur, nxt, p_buf, a_buf):
        scores(i + 1, nxt)
        for qb in range(SEL_QB):
            off = jnp.where(i < n_tiles, jnp.maximum(SEL_NEAR + SEL_TK - ((c0 + qb) * Q_BLOCK - i * SEL_TK), 0),
                            SEL_NEAR + SEL_TK + Q_BLOCK)
            off = pl.multiple_of(off, LANE)
            for r in range(rows_per_block // SEL_ROWS):
                trows = slice(r * SEL_ROWS, (r + 1) * SEL_ROWS)
                rows = slice(qb * rows_per_block + r * SEL_ROWS, qb * rows_per_block + (r + 1) * SEL_ROWS)
                s = cur[rows, :] + lt_ref[0, trows, pl.ds(off, SEL_TK)]
                cols = [s[:, j * LANE:(j + 1) * LANE] for j in range(ncol)]
                mx = functools.reduce(jnp.maximum, cols)
                m_old = m_s[rows, :]
                m_new = jnp.maximum(m_old, jnp.max(mx, axis=-1, keepdims=True))
                m_s[rows, :] = m_new
                a_buf[rows, :] = jnp.exp(m_old - m_new)
                p_buf[rows, :] = jnp.concatenate([jnp.exp(col - m_new) for col in cols], axis=1).astype(BF16)
        acc_s[...] = a_buf[...] * acc_s[...] + _dot(p_buf[...], v_ref[0, 0, key_rows(i), :])

    scores(0, s0_s)

    def pair(j, carry):
        tile(2 * j, s0_s, s1_s, p0_s, a0_s)
        tile(2 * j + 1, s1_s, s0_s, p1_s, a1_s)
        return carry

    lax.fori_loop(0, lax.div(n_tiles + 1, 2), pair, 0)
    acc = acc_s[...]
    o = acc * (1.0 / acc[:, HEAD_DIM:HEAD_DIM + 1])
    gates = _sigmoid(zg_ref[...])
    for qb in range(SEL_QB):
        _store_heads(o_ref, o[qb * rows_per_block:(qb + 1) * rows_per_block, :],
                     gates[qb * Q_BLOCK:(qb + 1) * Q_BLOCK, :], 1, qb * Q_BLOCK)


def _sel_branch(z, ks_aug, vs, mt, lt_tab, cq_tab, bsz, t):
    tq = SEL_QB * Q_BLOCK
    nq = t // tq
    nsel = mt.shape[-1]
    trows = NSA_HPG * Q_BLOCK
    rows = SEL_QB * trows
    return pl.pallas_call(
        _sel_kernel,
        grid=(bsz, NSA_GROUPS, nq),
        in_specs=[pl.BlockSpec((tq, NSA_HPG * LANE), lambda b, g, c: (b * nq + c, g)),
                  pl.BlockSpec((tq, LANE), lambda b, g, c: (b * nq + c, C_G3 // LANE + g)),
                  pl.BlockSpec((1, 1, t, 2 * LANE), lambda b, g, c: (b, g, 0, 0)),
                  pl.BlockSpec((1, 1, t, LANE), lambda b, g, c: (b, g, 0, 0)),
                  pl.BlockSpec((1, 1, tq, nsel), lambda b, g, c: (b, g, c, 0)),
                  pl.BlockSpec((1, trows, lt_tab.shape[2]), lambda b, g, c: (g, 0, 0)),
                  pl.BlockSpec((1, trows, LANE), lambda b, g, c: (g, 0, 0))],
        out_specs=pl.BlockSpec((1, NSA_HPG // 2, tq, LANE), lambda b, g, c: (b, g, c, 0)),
        out_shape=jax.ShapeDtypeStruct((bsz, NSA_HEADS // 2, t, LANE), BF16),
        scratch_shapes=[pltpu.VMEM((rows, LANE), F32),
                        pltpu.VMEM((rows, LANE), F32),
                        pltpu.VMEM((rows, SEL_TK), F32),
                        pltpu.VMEM((rows, SEL_TK), F32),
                        pltpu.VMEM((rows, SEL_TK), BF16),
                        pltpu.VMEM((rows, SEL_TK), BF16),
                        pltpu.VMEM((rows, LANE), F32),
                        pltpu.VMEM((rows, LANE), F32)],
        compiler_params=_cparams(("parallel", "parallel", "arbitrary")),
        name="nsa_selected",
    )(z, z, ks_aug, vs, mt, lt_tab, cq_tab)


def _shift_mix(z_ref, prev_ref, mu, first):
    z = z_ref[...]
    prev = jnp.where(first, 0.0, prev_ref[7:8, :])
    row = lax.broadcasted_iota(jnp.int32, z.shape, 0)
    zp = jnp.where(row == 0, prev, pltpu.roll(z, 1, 0))
    return z + mu * (zp - z)


def _rwkv_prep_kernel(has_vres, tiles_per_seq, *refs):
    (zr, zk, zv, zwa, zg, pr, pk, pv, pwa, pg, mu_ref, vec_ref, w2_ref, a2_ref, g2_ref, hsum_ref) = refs[:16]
    pos = 16
    if has_vres:
        vf_ref, v0_ref, v1_ref, v2_ref = refs[pos:pos + 4]
        pos += 4
    r_o, w_o, k_o, v_o, kk_o, b_o, g_o, bonus_o, vt_o = refs[pos:]
    first = lax.rem(pl.program_id(0), tiles_per_seq) == 0
    r = _shift_mix(zr, pr, mu_ref[0:1, :], first)
    k = _shift_mix(zk, pk, mu_ref[1:2, :], first)
    v = _shift_mix(zv, pv, mu_ref[2:3, :], first)
    wa = _shift_mix(zwa, pwa, mu_ref[3:4, 0:LANE], first)
    zg_s = _shift_mix(zg, pg, mu_ref[4:5, 0:LANE], first)
    w0, a0, k_k, k_a, r_k = (vec_ref[i:i + 1, :] for i in range(5))
    if has_vres:
        lora = _dot(_dot(v.astype(BF16), v1_ref[...]).astype(BF16), v2_ref[...])
        v = v + (vf_ref[...] - v) * _sigmoid(v0_ref[...] + lora)
    u = w0 + _dot(jnp.tanh(wa).astype(BF16), w2_ref[...])
    decay = jnp.exp(-math.exp(-0.5) * _sigmoid(u))
    a = _sigmoid(a0 + _dot(wa.astype(BF16), a2_ref[...]))
    g = _dot(_sigmoid(zg_s).astype(BF16), g2_ref[...])
    kk = k * k_k
    kk = kk / jnp.maximum(jnp.sqrt(_dot_split(kk * kk, hsum_ref[...])), 1e-12)
    k = k * (1.0 + (a - 1.0) * k_a)
    r_o[...] = r
    w_o[...] = decay
    k_o[...] = k.astype(k_o.dtype)
    v_o[...] = v
    kk_o[...] = kk
    b_o[...] = kk * a
    g_o[...] = g.astype(g_o.dtype)
    bonus_o[...] = (_dot_split(r * k * r_k, hsum_ref[...]) * v).astype(bonus_o.dtype)
    vt_o[0] = v.T.astype(BF16)


def _rwkv_prep(z, mu5, vecs, w2p, a2p, g2, hsum, vres, bsz, t, tm=256):
    n = bsz * t
    nt = n // tm
    tps = t // tm
    w = RWKV_WIDTH

    def cur(width, col):
        return pl.BlockSpec((tm, width), lambda i: (i, col // width))

    def prev(width, col):
        return pl.BlockSpec((8, width), lambda i: (jnp.maximum(i * (tm // 8) - 1, 0), col // width))

    def full(shape):
        return pl.BlockSpec(shape, lambda i: (0,) * len(shape))

    cols = [(w, C_R), (w, C_K), (w, C_V), (LANE, C_WA), (LANE, C_ZG)]
    in_specs = [cur(*c) for c in cols] + [prev(*c) for c in cols]
    in_specs += [full(mu5.shape), full(vecs.shape), full(w2p.shape), full(a2p.shape), full(g2.shape), full(hsum.shape)]
    args = [z] * 10 + [mu5, vecs, w2p, a2p, g2, hsum]
    if vres is not None:
        v_first, v0, v1p, v2p = vres
        in_specs += [pl.BlockSpec((tm, w), lambda i: (i, 0)), full(v0.shape), full(v1p.shape), full(v2p.shape)]
        args += [v_first, v0, v1p, v2p]
    tok = pl.BlockSpec((tm, w), lambda i: (i, 0))
    out_specs = [tok] * 8 + [pl.BlockSpec((1, w, tm), lambda i: (i // tps, 0, i % tps))]
    tok_dtypes = (F32, F32, BF16, F32, F32, F32, BF16, BF16)
    out_shape = [jax.ShapeDtypeStruct((n, w), dt) for dt in tok_dtypes] + [jax.ShapeDtypeStruct((bsz, w, t), BF16)]
    return pl.pallas_call(
        functools.partial(_rwkv_prep_kernel, vres is not None, tps),
        grid=(nt,),
        in_specs=in_specs,
        out_specs=out_specs,
        out_shape=out_shape,
        compiler_params=_cparams(("parallel",)),
        name="rwkv_prep",
    )(*args)


QUAD = 4
QUAD_LANES = QUAD * RWKV_HEAD
SUB_STEPS = LANE // QUAD


def _scan_kernel(r_ref, w_ref, k_ref, kk_ref, b_ref, vt_ref, ones_ref, place_ref, o_ref, s_ref, vq_ref, kq_ref, vk0_ref,
                 vk1_ref):
    bsz = r_ref.shape[0]
    tt = r_ref.shape[1]
    nh = RWKV_HEAD
    quads = [(bi, hf) for bi in range(bsz) for hf in range(RWKV_HEADS // QUAD)]

    @pl.when(pl.program_id(0) == 0)
    def _():
        s_ref[...] = jnp.zeros(s_ref.shape, F32)

    sub_shift, head_shift = int(math.log2(SUB_STEPS)), int(math.log2(nh))
    sub_lane = lax.broadcasted_iota(jnp.int32, (nh, LANE), 1) & (SUB_STEPS - 1)
    k_row_head = jnp.right_shift(lax.broadcasted_iota(jnp.int32, (LANE, QUAD_LANES), 0), sub_shift)
    k_lane_head = jnp.right_shift(lax.broadcasted_iota(jnp.int32, (LANE, QUAD_LANES), 1), head_shift)
    r_mask = (lax.broadcasted_iota(jnp.int32, (16, QUAD_LANES), 0)
              == jnp.right_shift(lax.broadcasted_iota(jnp.int32, (16, QUAD_LANES), 1), head_shift))
    ones_bd = ones_ref[...]

    def sub_block(sb, carry):
        s0 = pl.multiple_of(sb * SUB_STEPS, SUB_STEPS)
        for q, (bi, hf) in enumerate(quads):
            ln = slice(hf * QUAD_LANES, (hf + 1) * QUAD_LANES)
            vq = _dot(vt_ref[bi, (QUAD * hf) * nh:(QUAD * hf + 1) * nh, :], place_ref[sb * QUAD])
            for h in range(1, QUAD):
                vq = vq + _dot(vt_ref[bi, (QUAD * hf + h) * nh:(QUAD * hf + h + 1) * nh, :], place_ref[sb * QUAD + h])
            vq_ref[q] = vq.astype(BF16)
            k_sub = k_ref[bi, pl.ds(s0, SUB_STEPS), ln].astype(F32)
            kq_ref[q] = jnp.where(k_row_head == k_lane_head, jnp.concatenate([k_sub] * QUAD, axis=0), 0.0).astype(BF16)

        def group(tg, vk_ref, carry):
            t0 = pl.multiple_of(s0 + tg * 8, 8)
            rows = []
            for bi, hf in quads:
                ln = slice(hf * QUAD_LANES, (hf + 1) * QUAD_LANES)
                rows.append(tuple(ref[bi, pl.ds(t0, 8), ln] for ref in (w_ref, kk_ref, b_ref, r_ref)))
            onehots = [jnp.where(sub_lane == tg * 8 + u, 1.0, 0.0).astype(BF16) for u in range(8)]
            for q in range(len(quads)):
                vq = vq_ref[q]
                vk_ref[q] = _dot(jnp.concatenate([vq * oh for oh in onehots], axis=0), kq_ref[q])
            state = [s_ref[q] for q in range(len(quads))]
            for u in range(8):
                sk = _dot(jnp.concatenate([(state[q] * rows[q][1][u:u + 1, :]).astype(BF16)
                                           for q in range(len(quads))], axis=0), ones_bd)
                for q, (bi, hf) in enumerate(quads):
                    w8, _, b8, r8 = rows[q]
                    state[q] = (state[q] * w8[u:u + 1, :] - sk[q * nh:(q + 1) * nh, :] * b8[u:u + 1, :]
                                + vk_ref[q, u * nh:(u + 1) * nh, :])
                    r_lhs = jnp.where(r_mask, r8[u:u + 1, :], 0.0).astype(BF16)
                    o = _dot_nt(r_lhs, state[q].astype(BF16))
                    o_ref[t0 + u, bi, QUAD * hf:QUAD * (hf + 1), :] = o[0:QUAD, :]
            for q in range(len(quads)):
                s_ref[q] = state[q]
            return carry

        for tg in range(SUB_STEPS // 8):
            carry = group(tg, (vk0_ref, vk1_ref)[tg % 2], carry)
        return carry

    lax.fori_loop(0, tt // SUB_STEPS, sub_block, 0)


def _rwkv_scan(r, w, k, kk, b, vt, ones_bd, place, bsz, t):
    tt = SCAN_TT
    wd = RWKV_WIDTH
    nquad = bsz * RWKV_HEADS // QUAD
    tok = pl.BlockSpec((bsz, tt, wd), lambda i: (0, i, 0))
    return pl.pallas_call(
        _scan_kernel,
        grid=(t // tt,),
        in_specs=[tok, tok, tok, tok, tok,
                  pl.BlockSpec((bsz, wd, tt), lambda i: (0, 0, i)),
                  pl.BlockSpec(ones_bd.shape, lambda i: (0, 0)),
                  pl.BlockSpec(place.shape, lambda i: (0, 0, 0))],
        out_specs=pl.BlockSpec((tt, bsz, RWKV_HEADS, RWKV_HEAD), lambda i: (i, 0, 0, 0)),
        out_shape=jax.ShapeDtypeStruct((t, bsz, RWKV_HEADS, RWKV_HEAD), F32),
        scratch_shapes=[pltpu.VMEM((nquad, RWKV_HEAD, QUAD_LANES), F32),
                        pltpu.VMEM((nquad, RWKV_HEAD, LANE), BF16),
                        pltpu.VMEM((nquad, LANE, QUAD_LANES), BF16),
                        pltpu.VMEM((nquad, 8 * RWKV_HEAD, QUAD_LANES), F32),
                        pltpu.VMEM((nquad, 8 * RWKV_HEAD, QUAD_LANES), F32)],
        compiler_params=_cparams(("arbitrary",)),
        name="rwkv_scan",
    )(r, w, k, kk, b, vt, ones_bd, place)


def _merge_kernel(x_ref, oc_ref, os_ref, ow_ref, orw_ref, bonus_ref, g_ref, zgn_ref, zgr_ref, gn_ref, hsum_ref,
                  pn_ref, pr_ref, wo_ref, ln_ref, o_ref):
    o_n = jnp.concatenate([oc_ref[0, p].astype(F32) + os_ref[0, p].astype(F32) + ow_ref[0, p].astype(F32)
                           for p in range(NSA_HEADS // 2)], axis=1)
    y_n = _dot(o_n.astype(BF16), pn_ref[...])
    o = orw_ref[...]
    mu = _dot_split(o, hsum_ref[...]) * (1.0 / RWKV_HEAD)
    d = o - mu
    var = _dot_split(d * d, hsum_ref[...]) * (1.0 / RWKV_HEAD)
    o = d * lax.rsqrt(var + GN_EPS) * gn_ref[0:1, :] + gn_ref[1:2, :]
    o = ((o + bonus_ref[...]) * g_ref[...]).astype(BF16)
    y1 = _sigmoid(zgn_ref[...]) * y_n + _sigmoid(zgr_ref[...]) * _dot(o, pr_ref[...])
    y = _dot(y1.astype(BF16), wo_ref[...])
    o_ref[...] = _layer_norm(DN_ALPHA * x_ref[...] + y, ln_ref[0:1, :], ln_ref[1:2, :])


def _merge(x, o_c, o_s, o_w, o_rw, bonus, g, z, gn, hsum, p_nsa, p_rwkv, w_out, ln, bsz, t, tm=256):
    n = bsz * t
    tps = t // tm
    d = D_MODEL
    w = RWKV_WIDTH
    head = pl.BlockSpec((1, NSA_HEADS // 2, tm, LANE), lambda i: (i // tps, 0, i % tps, 0))
    tok = pl.BlockSpec((tm, w), lambda i: (i, 0))

    def full(a):
        return pl.BlockSpec(a.shape, lambda i: (0,) * a.ndim)

    return pl.pallas_call(
        _merge_kernel,
        grid=(n // tm,),
        in_specs=[pl.BlockSpec((tm, d), lambda i: (i, 0)), head, head, head, tok, tok, tok,
                  pl.BlockSpec((tm, d), lambda i: (i, C_GN // d)),
                  pl.BlockSpec((tm, d), lambda i: (i, C_GR // d)),
                  full(gn), full(hsum), full(p_nsa), full(p_rwkv), full(w_out), full(ln)],
        out_specs=pl.BlockSpec((tm, d), lambda i: (i, 0)),
        out_shape=jax.ShapeDtypeStruct((n, d), F32),
        compiler_params=_cparams(("parallel",)),
        name="mix_merge_ln",
    )(x, o_c, o_s, o_w, o_rw, bonus, g, z, z, gn, hsum, p_nsa, p_rwkv, w_out, ln)


def _xattn_kernel(x_ref, kt_ref, v_ref, wq_ref, wo_ref, ln_ref, rw_ref, o_ref, ob_ref, s_ref):
    x = x_ref[...]
    q = _dot(x.astype(BF16), wq_ref[...]).astype(BF16)
    outs = []
    for h in range(XATTN_HEADS):
        hs = slice(h * XATTN_HEAD, (h + 1) * XATTN_HEAD)
        s = _dot(q[:, hs], kt_ref[0, hs, :]) * (XATTN_HEAD ** -0.5)
        m = jnp.max(s, axis=-1, keepdims=True)
        p = jnp.exp(s - m)
        l = jnp.sum(p, axis=-1, keepdims=True)
        outs.append(_dot(p.astype(BF16), v_ref[0, :, hs]) * (1.0 / l))
    o = jnp.concatenate(outs, axis=1).astype(BF16)
    x2 = _layer_norm(DN_ALPHA * x + _dot(o, wo_ref[...]), ln_ref[0:1, :], ln_ref[1:2, :])
    o_ref[...] = x2
    hi = x2.astype(BF16)
    ob_ref[...] = hi
    lo = (x2 - hi.astype(F32)).astype(BF16)
    logits = _dot(hi, rw_ref[0]) + (_dot(lo, rw_ref[0]) + _dot(hi, rw_ref[1]))
    s_ref[...] = _sigmoid(logits)


def _xattn(x, kt, v, wq, wo, ln, rw2, bsz, t, tm=256):
    n = bsz * t
    tps = t // tm
    d = D_MODEL

    def full(a):
        return pl.BlockSpec(a.shape, lambda i: (0,) * a.ndim)

    return pl.pallas_call(
        _xattn_kernel,
        grid=(n // tm,),
        in_specs=[pl.BlockSpec((tm, d), lambda i: (i, 0)),
                  pl.BlockSpec((1,) + kt.shape[1:], lambda i: (i // tps, 0, 0)),
                  pl.BlockSpec((1,) + v.shape[1:], lambda i: (i // tps, 0, 0)),
                  full(wq), full(wo), full(ln), full(rw2)],
        out_specs=[pl.BlockSpec((tm, d), lambda i: (i, 0)), pl.BlockSpec((tm, d), lambda i: (i, 0)),
                   pl.BlockSpec((tm, LANE), lambda i: (i, 0))],
        out_shape=[jax.ShapeDtypeStruct((n, d), F32), jax.ShapeDtypeStruct((n, d), BF16),
                   jax.ShapeDtypeStruct((n, LANE), F32)],
        compiler_params=_cparams(("parallel",)),
        name="xattn_ln_router",
    )(x, kt, v, wq, wo, ln, rw2)


def _expert_kernel(meta_ref, x_ref, gate_ref, *refs):
    w_refs, o_ref, w_scr = refs[:6], refs[6], refs[7:]
    i = pl.program_id(0)
    nb = pl.num_programs(0)
    prev = jnp.maximum(i - 1, 0)

    for side in range(2):
        @pl.when((i == 0) | (meta_ref[side * nb + i] != meta_ref[side * nb + prev]))
        def _():
            for j in range(3):
                w_scr[3 * side + j][...] = w_refs[3 * side + j][0, 0].astype(BF16)

    n_used = meta_ref[2 * nb]

    @pl.when(i < n_used)
    def _():
        x = x_ref[...]
        y = jnp.zeros(o_ref.shape, F32)
        for side in range(2):
            wg, wu, wd = w_scr[3 * side:3 * side + 3]
            hg = _dot(x, wg[...])
            h = hg * _sigmoid(hg) * _dot(x, wu[...])
            y = y + gate_ref[:, side:side + 1] * _dot(h.astype(BF16), wd[...])
        o_ref[...] = y.astype(o_ref.dtype)

    @pl.when(i >= n_used)
    def _():
        o_ref[...] = jnp.zeros(o_ref.shape, o_ref.dtype)


def _experts(meta, rows, gates, w_gate, w_up, w_down, layer):
    n_rows, d = rows.shape
    ff = w_gate.shape[3]
    nb = n_rows // MOE_BLK

    def weight(shape, side):
        return pl.BlockSpec((1, 1) + shape, lambda i, m: (layer, m[side * nb + i], 0, 0))

    return pl.pallas_call(
        _expert_kernel,
        grid_spec=pltpu.PrefetchScalarGridSpec(
            num_scalar_prefetch=1,
            grid=(nb,),
            in_specs=[pl.BlockSpec((MOE_BLK, d), lambda i, m: (i, 0)),
                      pl.BlockSpec((MOE_BLK, TOP_K), lambda i, m: (i, 0))]
                     + [weight(s, side) for side in range(2) for s in ((d, ff), (d, ff), (ff, d))],
            out_specs=pl.BlockSpec((MOE_BLK, d), lambda i, m: (i, 0)),
            scratch_shapes=[pltpu.VMEM(s, BF16) for _ in range(2) for s in ((d, ff), (d, ff), (ff, d))]),
        out_shape=jax.ShapeDtypeStruct((n_rows, d), BF16),
        compiler_params=_cparams(("arbitrary",)),
        name="moe_experts",
    )(meta, rows, gates, w_gate, w_up, w_down, w_gate, w_up, w_down)


def _combine_kernel(x_ref, y_ref, ln_ref, o_ref):
    o_ref[...] = _layer_norm(DN_ALPHA * x_ref[...] + y_ref[...], ln_ref[0:1, :], ln_ref[1:2, :])


def _combine(x, y, ln, tm=512):
    n, d = x.shape
    tok = pl.BlockSpec((tm, d), lambda i: (i, 0))
    return pl.pallas_call(
        _combine_kernel,
        grid=(n // tm,),
        in_specs=[tok, tok, pl.BlockSpec(ln.shape, lambda i: (0, 0))],
        out_specs=tok,
        out_shape=jax.ShapeDtypeStruct((n, d), F32),
        compiler_params=_cparams(("parallel",)),
        name="moe_combine_ln",
    )(x, y, ln)


def _t5_bucket(dist):
    n = jnp.maximum(dist, 0)
    max_exact = REL_BUCKETS // 2
    log_ratio = jnp.log(jnp.maximum(n, 1).astype(F32) / max_exact) / math.log(REL_MAX_DIST / max_exact)
    large = jnp.minimum(max_exact + (log_ratio * (REL_BUCKETS - max_exact)).astype(jnp.int32), REL_BUCKETS - 1)
    return jnp.where(n < max_exact, n, large)


def _bias_tables(rel_table, t):
    nc = t // CMP_STRIDE
    f = rel_table[_t5_bucket(jnp.arange(t + 2048))].astype(F32).T
    ql = jnp.arange(Q_BLOCK)[:, None]

    chunks_per_q = Q_BLOCK // CMP_STRIDE
    r = np.arange(CMP_STRIDE)[:, None]
    m = np.arange(2 * nc + chunks_per_q - 1)[None, :]
    dist_c = CMP_STRIDE * (nc - 8 + chunks_per_q - 1 - m) + r - (CMP_LEN - 1)
    small = jnp.where((dist_c >= 0)[None], jnp.take(f, np.clip(dist_c, 0, f.shape[-1] - 1), axis=1), NEG_INF)
    pc = jnp.stack([small[:, :, chunks_per_q - 1 - a:chunks_per_q - 1 - a + 2 * nc] for a in range(chunks_per_q)], axis=1)
    pc = pc.reshape(NSA_GROUPS, NSA_HPG * Q_BLOCK, 2 * nc)
    pc = jnp.stack([pc[:, :, s:s + 2 * nc - LANE] for s in range(0, LANE, 8)], axis=1)
    nvar = WINDOW // Q_BLOCK + 1

    def toeplitz(base, width, hi, shift):
        period = width + Q_BLOCK
        m = np.arange(period)
        dist = base - np.where(m < width, m, m - period)
        row = jnp.take(f, np.clip(dist, 0, f.shape[-1] - 1), axis=1) - shift
        row = jnp.where(((dist >= 0) & (dist < hi))[None], row, NEG_INF)
        flat = jnp.tile(row, (1, Q_BLOCK))[:, :Q_BLOCK * (period - 1)]
        return flat.reshape(NSA_HEADS, Q_BLOCK, period - 1)[:, :, :width]

    wb = jnp.stack([toeplitz(Q_BLOCK * v, WINDOW + Q_BLOCK, WINDOW, 0.0) for v in range(nvar)], axis=1)
    wb = wb.reshape(NSA_GROUPS, NSA_HPG, nvar, Q_BLOCK, -1).transpose(0, 2, 1, 3, 4)
    wb = wb.reshape(NSA_GROUPS, nvar, NSA_HPG * Q_BLOCK, -1)
    far = rel_table[REL_BUCKETS - 1].astype(F32)
    lt = toeplitz(SEL_NEAR + SEL_TK, SEL_NEAR + 2 * SEL_TK + Q_BLOCK, f.shape[-1], far[:, None])
    lt = lt.reshape(NSA_GROUPS, NSA_HPG * Q_BLOCK, -1)
    far_hi = far.astype(BF16).astype(F32)
    lane = jnp.arange(LANE)[None, :]
    cq = jnp.where(lane == HEAD_DIM, far_hi[:, None], jnp.where(lane == HEAD_DIM + 1, (far - far_hi)[:, None], 0.0))
    cq = jnp.broadcast_to(cq[:, None, :], (NSA_HEADS, Q_BLOCK, LANE)).reshape(NSA_GROUPS, NSA_HPG * Q_BLOCK, LANE)
    return pc, wb, lt, cq


def _in_proj_perm():
    perm = np.full((IN_PAD,), -1, np.int64)
    for h in range(NSA_HEADS):
        perm[C_Q + h * LANE:C_Q + h * LANE + HEAD_DIM] = np.arange(HEAD_DIM) + h * HEAD_DIM
    rw0 = NSA_COLS
    g0 = NSA_COLS + RWKV_COLS
    perm[C_GN:C_GN + D_MODEL] = g0 + np.arange(D_MODEL)
    perm[C_GR:C_GR + D_MODEL] = g0 + D_MODEL + np.arange(D_MODEL)
    perm[C_R:C_R + RWKV_WIDTH] = rw0 + np.arange(RWKV_WIDTH)
    perm[C_K:C_K + RWKV_WIDTH] = rw0 + RWKV_WIDTH + np.arange(RWKV_WIDTH)
    perm[C_V:C_V + RWKV_WIDTH] = rw0 + 2 * RWKV_WIDTH + np.arange(RWKV_WIDTH)
    for g in range(NSA_GROUPS):
        for br in range(3):
            for h in range(NSA_HPG):
                perm[C_G3 + g * LANE + br * NSA_HPG + h] = NSA_WIDTH + 6 * NSA_KV + (g * NSA_HPG + h) * 3 + br
    perm[C_WA:C_WA + LORA_W + LORA_A] = rw0 + 3 * RWKV_WIDTH + np.arange(LORA_W + LORA_A)
    perm[C_ZG:C_ZG + LORA_G] = rw0 + 3 * RWKV_WIDTH + LORA_W + LORA_A + np.arange(LORA_G)
    return perm


_PERM = _in_proj_perm()


def _permute_in_proj(w):
    pieces = []
    start = 0
    while start < IN_PAD:
        stop = start + 1
        if _PERM[start] < 0:
            while stop < IN_PAD and _PERM[stop] < 0:
                stop += 1
            pieces.append(jnp.zeros((w.shape[0], stop - start), BF16))
        else:
            while stop < IN_PAD and _PERM[stop] == _PERM[stop - 1] + 1:
                stop += 1
            pieces.append(w[:, int(_PERM[start]):int(_PERM[stop - 1]) + 1].astype(BF16))
        start = stop
    return jnp.concatenate(pieces, axis=1)


def _pad_to(a, shape):
    return jnp.pad(a, [(0, s - d) for d, s in zip(a.shape, shape)])


_CLASS_EXPERTS = np.asarray([(grp * EXPERTS_PER_GROUP + i, grp * EXPERTS_PER_GROUP + j)
                             for grp in range(N_EXPERT_GROUPS)
                             for i in range(EXPERTS_PER_GROUP) for j in range(i + 1, EXPERTS_PER_GROUP)], np.int32)
N_PAIR_CLASSES = len(_CLASS_EXPERTS)


def _route(s, router_bias, n_tok):
    s16 = s[:, :N_EXPERTS]
    s_sel = s16 + router_bias.astype(F32)

    def top2(a):
        i1 = jnp.argmax(a, axis=-1)
        rest = jnp.where(jnp.arange(a.shape[-1]) == i1[..., None], -jnp.inf, a)
        i2 = jnp.argmax(rest, axis=-1)
        return jnp.max(a, axis=-1) + jnp.max(rest, axis=-1), jnp.stack([i1, i2], axis=-1)

    group_score, _ = top2(s_sel.reshape(n_tok, N_EXPERT_GROUPS, EXPERTS_PER_GROUP))
    group = jnp.argmax(group_score, axis=-1)
    in_group = (jnp.arange(N_EXPERTS) // EXPERTS_PER_GROUP)[None, :] == group[:, None]
    _, e_idx = top2(jnp.where(in_group, s_sel, NEG_INF))
    picked = jnp.arange(N_EXPERTS)[None, None, :] == e_idx[:, :, None]
    s_top = jnp.sum(jnp.where(picked, s16[:, None, :], 0.0), axis=-1)
    gate = s_top / jnp.sum(s_top, axis=-1, keepdims=True)
    swap = e_idx[:, 0] > e_idx[:, 1]
    e_lo = jnp.where(swap, e_idx[:, 1], e_idx[:, 0])
    e_hi = jnp.where(swap, e_idx[:, 0], e_idx[:, 1])
    gate_lohi = jnp.where(swap[:, None], gate[:, ::-1], gate)
    l_lo, l_hi = e_lo % EXPERTS_PER_GROUP, e_hi % EXPERTS_PER_GROUP
    pairs_per_group = N_PAIR_CLASSES // N_EXPERT_GROUPS
    cls = ((e_lo // EXPERTS_PER_GROUP) * pairs_per_group
           + (l_lo * (2 * EXPERTS_PER_GROUP - 1 - l_lo)) // 2 + (l_hi - l_lo - 1))
    onehot = (cls[:, None] == jnp.arange(N_PAIR_CLASSES)[None, :]).astype(jnp.int32)
    counts = jnp.sum(onehot, axis=0)
    padded = (counts + MOE_BLK - 1) // MOE_BLK * MOE_BLK
    ends = jnp.cumsum(padded)
    before = jnp.cumsum(onehot, axis=0) - onehot + (ends - padded)[None, :]
    dest = jnp.sum(onehot * before, axis=1)
    n_rows = n_tok + N_PAIR_CLASSES * MOE_BLK
    tok_of_row = jnp.zeros((n_rows,), jnp.int32).at[dest].set(jnp.arange(n_tok, dtype=jnp.int32))
    block_start = jnp.arange(n_rows // MOE_BLK) * MOE_BLK
    block_cls = jnp.minimum(jnp.sum((ends[None, :] <= block_start[:, None]).astype(jnp.int32), axis=1),
                            N_PAIR_CLASSES - 1)
    block_onehot = (block_cls[:, None] == jnp.arange(N_PAIR_CLASSES)[None, :]).astype(jnp.int32)
    block_experts = jnp.sum(block_onehot[:, :, None] * jnp.asarray(_CLASS_EXPERTS)[None], axis=1)
    meta = jnp.concatenate([block_experts[:, 0], block_experts[:, 1], ends[-1:] // MOE_BLK]).astype(jnp.int32)
    return gate_lohi, dest, tok_of_row, meta


def kernel(x, mem, rel_table, router_w, router_bias, w_in, cmp_pe_k, cmp_pe_v, cmp_w1k, cmp_w2k, cmp_w1v, cmp_w2v, rwkv_mu, rwkv_w0, rwkv_w2, rwkv_a0, rwkv_a2, rwkv_g2, rwkv_kk, rwkv_ka, rwkv_rk, rwkv_gn_g, rwkv_gn_b, rwkv_v0, rwkv_v1, rwkv_v2, p_nsa, p_rwkv, w_out, ln1_g, ln1_b, xq_w, xk_w, xv_w, xo_w, ln2_g, ln2_b, moe_w_gate, moe_w_up, moe_w_down, ln3_g, ln3_b):
    bsz, t, d = x.shape
    n = bsz * t
    depth = w_in.shape[0]
    nc = t // CMP_STRIDE
    nsel = t // SEL_BLOCK
    g, hd = NSA_GROUPS, HEAD_DIM

    pc_tab, wb_tab, lt_tab, cq_tab = _bias_tables(rel_table, t)
    cmp_start = np.arange(nc) * CMP_STRIDE
    sel_start = np.arange(nsel) * SEL_BLOCK
    overlap = ((cmp_start[:, None] <= sel_start[None, :] + SEL_BLOCK - 1)
               & (cmp_start[:, None] + CMP_LEN - 1 >= sel_start[None, :]) & (cmp_start[:, None] < (nc - 1) * CMP_STRIDE))
    overlap = jnp.asarray(overlap.T, BF16)
    lane_head = np.arange(RWKV_WIDTH) // RWKV_HEAD
    hsum = jnp.asarray(lane_head[:, None] == lane_head[None, :], BF16)
    quad_head = np.arange(QUAD_LANES) // RWKV_HEAD
    ones_bd = jnp.asarray(quad_head[:, None] == quad_head[None, :], BF16)
    step = np.arange(SCAN_TT)[None, :, None]
    lane = np.arange(LANE)[None, None, :]
    sbh = np.arange(SCAN_TT // SUB_STEPS * QUAD)[:, None, None]
    place = jnp.asarray((lane // SUB_STEPS == sbh % QUAD) & (step == (sbh // QUAD) * SUB_STEPS + lane % SUB_STEPS), BF16)
    rw_f = _pad_to(router_w.astype(F32), (d, LANE))
    rw_hi = rw_f.astype(BF16)
    rw2 = jnp.stack([rw_hi, (rw_f - rw_hi.astype(F32)).astype(BF16)])

    xf = x.reshape(n, d)
    mem_f = mem.reshape(bsz * mem.shape[1], d)
    v_first = None
    for l in range(depth):
        w_in_l = _permute_in_proj(w_in[l])
        z = _matmul(xf, w_in_l, F32, 1024, 1024)

        kv0 = NSA_WIDTH
        zeros64 = jnp.zeros((d, LANE - hd), BF16)
        w_kv = jnp.stack([jnp.concatenate(
            [piece for i in range(KV_ARRAYS)
             for piece in (w_in[l][:, kv0 + i * NSA_KV + gi * hd:kv0 + i * NSA_KV + (gi + 1) * hd].astype(BF16), zeros64)],
            axis=1) for gi in range(g)])
        kc, vc, ks_aug, vs_aug, kw_pad, vw_aug = _kv_proj(xf, w_kv, bsz, t)

        def cmp_w(pe, w1, w2):
            return (pe.reshape(2, CMP_STRIDE * hd), w1.reshape(2, CMP_STRIDE * hd, CMP_HIDDEN).astype(BF16),
                    _pad_to(w2, (CMP_HIDDEN, LANE)).astype(BF16))

        chunks = lambda a: a.reshape(bsz, g, nc, CMP_STRIDE * hd)
        k_cmp = _compress(chunks(kc), *cmp_w(cmp_pe_k[l], cmp_w1k[l], cmp_w2k[l]))
        v_cmp = _compress(chunks(vc), *cmp_w(cmp_pe_v[l], cmp_w1v[l], cmp_w2v[l]))
        o_c, sel_mask = _cmp_branch(z, k_cmp, v_cmp, pc_tab, overlap, bsz, t)
        o_s = _sel_branch(z, ks_aug, vs_aug, sel_mask, lt_tab, cq_tab, bsz, t)
        o_w = _win_branch(z, kw_pad, vw_aug, wb_tab, bsz, t)

        mu = rwkv_mu[l]
        w3 = 3 * RWKV_WIDTH
        mu5 = jnp.stack([mu[0:512], mu[512:1024], mu[1024:1536], _pad_to(mu[w3:w3 + 128], (512,)),
                         _pad_to(mu[w3 + 128:w3 + 256], (512,))])
        vecs = jnp.stack([rwkv_w0[l], rwkv_a0[l], rwkv_kk[l], rwkv_ka[l], rwkv_rk[l].reshape(-1)])
        w2p = jnp.concatenate([rwkv_w2[l], jnp.zeros_like(rwkv_a2[l])], axis=0).astype(BF16)
        a2p = jnp.concatenate([jnp.zeros_like(rwkv_w2[l]), rwkv_a2[l]], axis=0).astype(BF16)
        vres = None
        if l > 0:
            vres = (v_first, rwkv_v0[l - 1][None, :], _pad_to(rwkv_v1[l - 1], (RWKV_WIDTH, LANE)).astype(BF16),
                    _pad_to(rwkv_v2[l - 1], (LANE, RWKV_WIDTH)).astype(BF16))
        r, w, k, v, kk, b, g_out, bonus, vt = _rwkv_prep(z, mu5, vecs, w2p, a2p, rwkv_g2[l].astype(BF16), hsum, vres,
                                                          bsz, t)
        if l == 0:
            v_first = v
        seq = lambda a: a.reshape(bsz, t, RWKV_WIDTH)
        o_rw = _rwkv_scan(seq(r), seq(w), seq(k), seq(kk), seq(b), vt, ones_bd, place, bsz, t)
        o_rw = o_rw.transpose(1, 0, 2, 3).reshape(n, RWKV_WIDTH)

        xf = _merge(xf, o_c, o_s, o_w, o_rw, bonus, g_out, z, jnp.stack([rwkv_gn_g[l], rwkv_gn_b[l]]), hsum,
                    p_nsa[l].astype(BF16), p_rwkv[l].astype(BF16), w_out[l].astype(BF16),
                    jnp.stack([ln1_g[l], ln1_b[l]]), bsz, t)

        mlen = mem.shape[1]
        mk = _matmul(mem_f, xk_w[l].astype(BF16), BF16, mlen, XATTN_WIDTH)
        mv = _matmul(mem_f, xv_w[l].astype(BF16), BF16, mlen, XATTN_WIDTH)
        kt = mk.reshape(bsz, mlen, XATTN_WIDTH).transpose(0, 2, 1)
        xf, xf_b, scores = _xattn(xf, kt, mv.reshape(bsz, mlen, XATTN_WIDTH), xq_w[l].astype(BF16), xo_w[l].astype(BF16),
                            jnp.stack([ln2_g[l], ln2_b[l]]), rw2, bsz, t)

        gate, dest, tok_of_row, meta = _route(scores, router_bias, n)
        rows = jnp.take(xf_b, tok_of_row, axis=0)
        y_rows = _experts(meta, rows, jnp.take(gate, tok_of_row, axis=0), moe_w_gate, moe_w_up, moe_w_down, l)
        xf = _combine(xf, jnp.take(y_rows, dest, axis=0), jnp.stack([ln3_g[l], ln3_b[l]]))
    return xf.reshape(bsz, t, d)
```

```python
import functools
import math

import numpy as np
import jax
import jax.numpy as jnp
from jax import lax
from jax.experimental import pallas as pl
from jax.experimental.pallas import tpu as pltpu

F32 = jnp.float32
BF16 = jnp.bfloat16

D_MODEL = 1024
DEPTH = 2
NSA_HEADS = 8
NSA_GROUPS = 2
NSA_HPG = NSA_HEADS // NSA_GROUPS
HEAD_DIM = 64
NSA_WIDTH = NSA_HEADS * HEAD_DIM
NSA_KV = NSA_GROUPS * HEAD_DIM
CMP_STRIDE = 16
CMP_LEN = 2 * CMP_STRIDE
CMP_HIDDEN = 256
SEL_BLOCK = 64
SEL_TOPN = 16
WINDOW = 512
Q_BLOCK = 128
SEL_FORCE = 1e9
RWKV_HEADS = 8
RWKV_HEAD = 64
RWKV_WIDTH = RWKV_HEADS * RWKV_HEAD
LORA_W = 64
LORA_A = 64
LORA_V = 32
LORA_G = 128
GN_EPS = 64e-5
REL_BUCKETS = 32
REL_MAX_DIST = 1024
XATTN_HEADS = 4
XATTN_HEAD = 128
XATTN_WIDTH = XATTN_HEADS * XATTN_HEAD
N_EXPERTS = 16
N_EXPERT_GROUPS = 4
EXPERTS_PER_GROUP = N_EXPERTS // N_EXPERT_GROUPS
TOP_K = 2
EXPERT_FF = 512
DN_ALPHA = (2 * DEPTH) ** 0.25
LN_EPS = 1e-5
NEG_INF = -1e30
NSA_COLS = NSA_WIDTH + 6 * NSA_KV + 3 * NSA_HEADS
RWKV_COLS = 3 * RWKV_WIDTH + LORA_W + LORA_A + LORA_G

LANE = 128
VMEM_LIMIT = 56 * 1024 * 1024

C_Q = 0
C_GN = 1024
C_GR = 2048
C_R = 3072
C_K = 3584
C_V = 4096
C_G3 = 4608
C_WA = 4864
C_ZG = 4992
IN_PAD = 5120
KV_ARRAYS = 6

MOE_BLK = 256
SCAN_TT = 128
FAR_BIAS_DIST = 1280


def _cparams(sem):
    return pltpu.CompilerParams(dimension_semantics=sem, vmem_limit_bytes=VMEM_LIMIT)


def _sigmoid(x):
    return 1.0 / (1.0 + jnp.exp(-x))


def _layer_norm(v, g, b):
    mu = jnp.mean(v, axis=-1, keepdims=True)
    d = v - mu
    var = jnp.mean(d * d, axis=-1, keepdims=True)
    return d * lax.rsqrt(var + LN_EPS) * g + b


def _dot(a, b):
    return jnp.dot(a, b, preferred_element_type=F32)


def _dot_nt(a, b):
    return lax.dot_general(a, b, (((1,), (1,)), ((), ())), preferred_element_type=F32)


def _dot_split(x, w):
    hi = x.astype(BF16)
    lo = (x - hi.astype(F32)).astype(BF16)
    return _dot(hi, w) + _dot(lo, w)


def _mm_kernel(x_ref, w_ref, o_ref, xb_ref):
    @pl.when(pl.program_id(1) == 0)
    def _():
        xb_ref[...] = x_ref[...].astype(BF16)

    o_ref[...] = _dot(xb_ref[...], w_ref[...]).astype(o_ref.dtype)


def _matmul(x, w, out_dtype, tm, tn):
    n, k = x.shape
    m = w.shape[1]
    return pl.pallas_call(
        _mm_kernel,
        grid=(n // tm, m // tn),
        in_specs=[pl.BlockSpec((tm, k), lambda i, j: (i, 0)),
                  pl.BlockSpec((k, tn), lambda i, j: (0, j))],
        out_specs=pl.BlockSpec((tm, tn), lambda i, j: (i, j)),
        out_shape=jax.ShapeDtypeStruct((n, m), out_dtype),
        scratch_shapes=[pltpu.VMEM((tm, k), BF16)],
        compiler_params=_cparams(("parallel", "arbitrary")),
        name="matmul",
    )(x, w)


def _kv_proj_kernel(tiles_per_seq, x_ref, w_ref, kc_ref, vc_ref, ks_ref, vs_ref, kw_ref, vw_ref, xb_ref):
    @pl.when(pl.program_id(1) == 0)
    def _():
        xb_ref[...] = x_ref[...].astype(BF16)

    y = _dot(xb_ref[...], w_ref[pl.program_id(1)])
    tm = y.shape[0]
    slot = lambda i: y[:, i * LANE:(i + 1) * LANE]
    lane = lax.broadcasted_iota(jnp.int32, (tm, LANE), 1)
    one_lane = jnp.where(lane == HEAD_DIM, 1.0, 0.0)
    t0 = lax.rem(pl.program_id(0), tiles_per_seq) * tm
    block = jnp.right_shift(t0 + lax.broadcasted_iota(jnp.int32, (tm, LANE), 0), int(math.log2(SEL_BLOCK)))
    kc_ref[0, 0] = slot(0)[:, :HEAD_DIM]
    vc_ref[0, 0] = slot(1)[:, :HEAD_DIM]
    ks_ref[0, 0] = jnp.concatenate([slot(2) + jnp.where(lane == HEAD_DIM + 1, 1.0, one_lane),
                                    jnp.where(lane == block, 1.0, 0.0)], axis=1).astype(BF16)
    vs_ref[0, 0] = (slot(3) + one_lane).astype(BF16)
    kw_ref[0, 0] = slot(4).astype(BF16)
    vw_ref[0, 0] = (slot(5) + one_lane).astype(BF16)


def _kv_proj(x, w_kv, bsz, t, tm=512):
    n, d = x.shape
    tps = t // tm
    g = NSA_GROUPS

    def out(width, dtype):
        return (pl.BlockSpec((1, 1, tm, width), lambda i, j: (i // tps, j, i % tps, 0)),
                jax.ShapeDtypeStruct((bsz, g, t, width), dtype))

    outs = [out(HEAD_DIM, F32), out(HEAD_DIM, F32), out(2 * LANE, BF16), out(LANE, BF16), out(LANE, BF16),
            out(LANE, BF16)]
    return pl.pallas_call(
        functools.partial(_kv_proj_kernel, tps),
        grid=(n // tm, g),
        in_specs=[pl.BlockSpec((tm, d), lambda i, j: (i, 0)),
                  pl.BlockSpec((g, d, KV_ARRAYS * LANE), lambda i, j: (0, 0, 0))],
        out_specs=[o[0] for o in outs],
        out_shape=[o[1] for o in outs],
        scratch_shapes=[pltpu.VMEM((tm, d), BF16)],
        compiler_params=_cparams(("parallel", "arbitrary")),
        name="nsa_kv_proj",
    )(x, w_kv)


def _compress_kernel(u_ref, pe_ref, w1_ref, w2_ref, o_ref):
    u = u_ref[0, 0]
    a = _dot((u + pe_ref[0:1, :]).astype(BF16), w1_ref[0])
    b = _dot((u + pe_ref[1:2, :]).astype(BF16), w1_ref[1])
    nc = u.shape[0]
    h = a + pltpu.roll(b, nc - 1, 0)
    h = 0.5 * h * (1.0 + jnp.tanh(math.sqrt(2.0 / math.pi) * (h + 0.044715 * (h * h * h))))
    y = _dot(h.astype(BF16), w2_ref[...])
    row = lax.broadcasted_iota(jnp.int32, y.shape, 0)
    o_ref[0, 0] = jnp.where(row < nc - 1, y, 0.0)


def _compress(u, pe2, w1, w2p):
    b, g, nc, _ = u.shape
    return pl.pallas_call(
        _compress_kernel,
        grid=(b, g),
        in_specs=[pl.BlockSpec((1, 1, nc, 1024), lambda i, j: (i, j, 0, 0)),
                  pl.BlockSpec((2, 1024), lambda i, j: (0, 0)),
                  pl.BlockSpec((2, 1024, CMP_HIDDEN), lambda i, j: (0, 0, 0)),
                  pl.BlockSpec((CMP_HIDDEN, LANE), lambda i, j: (0, 0))],
        out_specs=pl.BlockSpec((1, 1, nc, LANE), lambda i, j: (i, j, 0, 0)),
        out_shape=jax.ShapeDtypeStruct((b, g, nc, LANE), F32),
        compiler_params=_cparams(("parallel", "parallel")),
        name="nsa_compress",
    )(u, pe2, w1, w2p)


def _stack_heads(q_ref, row0=0):
    q = q_ref[row0:row0 + Q_BLOCK, :] * (HEAD_DIM ** -0.5)
    return jnp.concatenate([q[:, h * LANE:(h + 1) * LANE] for h in range(NSA_HPG)], axis=0).astype(BF16)


def _store_heads(o_ref, o, gates, branch, row0=0):
    lane = lax.broadcasted_iota(jnp.int32, (Q_BLOCK, LANE), 1)
    for pair in range(NSA_HPG // 2):
        halves = []
        for h in (2 * pair, 2 * pair + 1):
            c = branch * NSA_HPG + h
            halves.append(o[h * Q_BLOCK:(h + 1) * Q_BLOCK, :] * gates[:, c:c + 1])
        o_ref[0, pair, row0:row0 + Q_BLOCK, :] = jnp.where(lane < HEAD_DIM, halves[0],
                                                           pltpu.roll(halves[1], HEAD_DIM, 1)).astype(o_ref.dtype)


CMP_QB = 2


def _cmp_kernel(q_ref, zg_ref, kc_ref, vc_ref, pc0_ref, pc1_ref, ov_ref, o_ref, m_ref, *scratch):
    nc = kc_ref.shape[2]
    nsel = ov_ref.shape[0]
    kc = kc_ref[0, 0].astype(BF16)
    vc = vc_ref[0, 0].astype(BF16)
    gates = _sigmoid(zg_ref[...])
    blocks = [(qb, pl.program_id(2) * CMP_QB + qb, pc_ref, scratch[2 * qb], scratch[2 * qb + 1])
              for qb, pc_ref in enumerate((pc0_ref, pc1_ref))]
    for qb, c, _, p_s, _ in blocks:
        p_s[...] = _dot_nt(_stack_heads(q_ref, qb * Q_BLOCK), kc)
    for qb, c, pc_ref, p_s, pb_s in blocks:
        off = pl.multiple_of(lax.div((nc - 8) - 8 * c, LANE) * LANE, LANE)
        for r in range(NSA_HPG * Q_BLOCK // SEL_ROWS):
            rows = slice(r * SEL_ROWS, (r + 1) * SEL_ROWS)
            s = p_s[rows, :] + pc_ref[0, 0, rows, pl.ds(off, nc)]
            e = jnp.exp(s - jnp.max(s, axis=-1, keepdims=True))
            l = jnp.sum(e, axis=-1, keepdims=True)
            ql = (lax.broadcasted_iota(jnp.int32, (SEL_ROWS, 1), 0) + r * SEL_ROWS) & (Q_BLOCK - 1)
            p = e * jnp.where((c * Q_BLOCK + ql) >= (CMP_LEN - 1), 1.0 / l, 0.0)
            p_s[rows, :] = p
            pb_s[rows, :] = p.astype(BF16)
    vals = []
    jb = lax.broadcasted_iota(jnp.int32, (nsel, Q_BLOCK), 0)
    jbf = jb.astype(F32)
    for qb, c, _, p_s, pb_s in blocks:
        _store_heads(o_ref, _dot(pb_s[...], vc), gates[qb * Q_BLOCK:(qb + 1) * Q_BLOCK, :], 0, qb * Q_BLOCK)
        psum = ((p_s[0:Q_BLOCK, :] + p_s[Q_BLOCK:2 * Q_BLOCK, :])
                + (p_s[2 * Q_BLOCK:3 * Q_BLOCK, :] + p_s[3 * Q_BLOCK:4 * Q_BLOCK, :]))
        hi = psum.astype(BF16)
        lo = (psum - hi.astype(F32)).astype(BF16)
        imp = _dot_nt(ov_ref[...], hi) + _dot_nt(ov_ref[...], lo)
        tq = c * Q_BLOCK + lax.broadcasted_iota(jnp.int32, (nsel, Q_BLOCK), 1)
        jq = jnp.right_shift(tq, int(math.log2(SEL_BLOCK)))
        forced = (jb == 0) | (jb == jq) | (jb == jq - 1)
        vals.append(jnp.where(forced, SEL_FORCE, jnp.where(jb <= jq, imp, -SEL_FORCE)))
    sels = [jnp.zeros((nsel, Q_BLOCK), F32) for _ in blocks]
    for _ in range(SEL_TOPN):
        for qb in range(CMP_QB):
            mx = jnp.max(vals[qb], axis=0, keepdims=True)
            first = jnp.min(jnp.where(vals[qb] == mx, jbf, float(nsel)), axis=0, keepdims=True)
            hit = jbf == first
            sels[qb] = jnp.where(hit, jnp.where(mx >= 0.0, 1.0, 0.0), sels[qb])
            vals[qb] = jnp.where(hit, -3e38, vals[qb])
    for qb in range(CMP_QB):
        m_ref[0, 0, qb * Q_BLOCK:(qb + 1) * Q_BLOCK, :] = (sels[qb].T - 1.0) * (-NEG_INF)


def _cmp_branch(z, k_cmp, v_cmp, pc_tab, overlap, bsz, t):
    tq = CMP_QB * Q_BLOCK
    nq = t // tq
    nc = k_cmp.shape[2]
    nsel = overlap.shape[0]
    rows = NSA_HPG * Q_BLOCK

    def variant(qb):
        return pl.BlockSpec((1, 1, rows, 2 * nc - LANE),
                            lambda b, g, c: (g, lax.rem((nc - 8) - 8 * (CMP_QB * c + qb), LANE) // 8, 0, 0))

    return pl.pallas_call(
        _cmp_kernel,
        grid=(bsz, NSA_GROUPS, nq),
        in_specs=[pl.BlockSpec((tq, NSA_HPG * LANE), lambda b, g, c: (b * nq + c, g)),
                  pl.BlockSpec((tq, LANE), lambda b, g, c: (b * nq + c, C_G3 // LANE + g)),
                  pl.BlockSpec((1, 1, nc, LANE), lambda b, g, c: (b, g, 0, 0)),
                  pl.BlockSpec((1, 1, nc, LANE), lambda b, g, c: (b, g, 0, 0)),
                  variant(0), variant(1),
                  pl.BlockSpec((nsel, nc), lambda b, g, c: (0, 0))],
        out_specs=[pl.BlockSpec((1, NSA_HPG // 2, tq, LANE), lambda b, g, c: (b, g, c, 0)),
                   pl.BlockSpec((1, 1, tq, nsel), lambda b, g, c: (b, g, c, 0))],
        out_shape=[jax.ShapeDtypeStruct((bsz, NSA_HEADS // 2, t, LANE), BF16),
                   jax.ShapeDtypeStruct((bsz, NSA_GROUPS, t, nsel), F32)],
        scratch_shapes=[pltpu.VMEM((rows, nc), F32), pltpu.VMEM((rows, nc), BF16)] * CMP_QB,
        compiler_params=_cparams(("parallel", "parallel", "arbitrary")),
        name="nsa_cmp_select",
    )(z, z, k_cmp, v_cmp, pc_tab, pc_tab, overlap)


WIN_QB = 2


def _win_kernel(q_ref, zg_ref, k_ref, v_ref, wb_ref, o_ref, *scratch):
    nvar = wb_ref.shape[1]
    gates = _sigmoid(zg_ref[...])
    blocks = []
    for qb in range(WIN_QB):
        c = pl.program_id(2) * WIN_QB + qb
        start = pl.multiple_of(jnp.maximum(c * Q_BLOCK - WINDOW, 0), Q_BLOCK)
        blocks.append((qb, jnp.minimum(c, nvar - 1), pl.ds(start, WINDOW + Q_BLOCK), scratch[2 * qb], scratch[2 * qb + 1]))
    for qb, _, keys, s_s, _ in blocks:
        s_s[...] = _dot_nt(_stack_heads(q_ref, qb * Q_BLOCK), k_ref[0, 0, keys, :])
    for qb, var, _, s_s, p_s in blocks:
        for r in range(NSA_HPG * Q_BLOCK // SEL_ROWS):
            rows = slice(r * SEL_ROWS, (r + 1) * SEL_ROWS)
            s = s_s[rows, :] + wb_ref[0, var, rows, :]
            p_s[rows, :] = jnp.exp(s - jnp.max(s, axis=-1, keepdims=True)).astype(BF16)
    for qb, _, keys, _, p_s in blocks:
        o = _dot(p_s[...], v_ref[0, 0, keys, :])
        o = o * (1.0 / o[:, HEAD_DIM:HEAD_DIM + 1])
        _store_heads(o_ref, o, gates[qb * Q_BLOCK:(qb + 1) * Q_BLOCK, :], 2, qb * Q_BLOCK)


def _win_branch(z, kw, vw, wb_tab, bsz, t):
    tq = WIN_QB * Q_BLOCK
    nq = t // tq
    rows, width = NSA_HPG * Q_BLOCK, WINDOW + Q_BLOCK
    return pl.pallas_call(
        _win_kernel,
        grid=(bsz, NSA_GROUPS, nq),
        in_specs=[pl.BlockSpec((tq, NSA_HPG * LANE), lambda b, g, c: (b * nq + c, g)),
                  pl.BlockSpec((tq, LANE), lambda b, g, c: (b * nq + c, C_G3 // LANE + g)),
                  pl.BlockSpec((1, 1, t, LANE), lambda b, g, c: (b, g, 0, 0)),
                  pl.BlockSpec((1, 1, t, LANE), lambda b, g, c: (b, g, 0, 0)),
                  pl.BlockSpec((1,) + wb_tab.shape[1:], lambda b, g, c: (g, 0, 0, 0))],
        out_specs=pl.BlockSpec((1, NSA_HPG // 2, tq, LANE), lambda b, g, c: (b, g, c, 0)),
        out_shape=jax.ShapeDtypeStruct((bsz, NSA_HEADS // 2, t, LANE), BF16),
        scratch_shapes=[pltpu.VMEM((rows, width), F32), pltpu.VMEM((rows, width), BF16)] * WIN_QB,
        compiler_params=_cparams(("parallel", "parallel", "arbitrary")),
        name="nsa_window",
    )(z, z, kw, vw, wb_tab)


SEL_TK = 512


SEL_ROWS = 64
SEL_QB = 4
SEL_NEAR = FAR_BIAS_DIST + Q_BLOCK


def _sel_kernel(q_ref, zg_ref, k_ref, v_ref, mt_ref, lt_ref, cq_ref, o_ref, m_s, acc_s, s0_s, s1_s, p0_s, p1_s,
                a0_s, a1_s):
    rows_per_block = NSA_HPG * Q_BLOCK
    c0 = pl.program_id(2) * SEL_QB
    q = q_ref[...] * (HEAD_DIM ** -0.5)
    parts = []
    for qb in range(SEL_QB):
        qrows = slice(qb * Q_BLOCK, (qb + 1) * Q_BLOCK)
        qs = jnp.concatenate([q[qrows, h * LANE:(h + 1) * LANE] for h in range(NSA_HPG)], axis=0) + cq_ref[0]
        parts.append(jnp.concatenate([qs, jnp.concatenate([mt_ref[0, 0, qrows, :]] * NSA_HPG, axis=0)], axis=1))
    q_aug = jnp.concatenate(parts, axis=0).astype(BF16)
    m_s[...] = jnp.full(m_s.shape, NEG_INF, F32)
    acc_s[...] = jnp.zeros(acc_s.shape, F32)
    ncol = SEL_TK // LANE
    n_tiles = lax.div(c0 + SEL_QB - 1, SEL_TK // Q_BLOCK) + 1

    def key_rows(i):
        return pl.ds(pl.multiple_of(jnp.minimum(i, n_tiles - 1) * SEL_TK, SEL_TK), SEL_TK)

    def scores(i, dst):
        dst[...] = _dot_nt(q_aug, k_ref[0, 0, key_rows(i), :])

    def tile(i, cur, nxt, p_buf, a_buf):
        scores(i + 1, nxt)
        for qb in range(SEL_QB):
            off = jnp.where(i < n_tiles, jnp.maximum(SEL_NEAR + SEL_TK - ((c0 + qb) * Q_BLOCK - i * SEL_TK), 0),
                            SEL_NEAR + SEL_TK + Q_BLOCK)
            off = pl.multiple_of(off, LANE)
            for r in range(rows_per_block // SEL_ROWS):
                trows = slice(r * SEL_ROWS, (r + 1) * SEL_ROWS)
                rows = slice(qb * rows_per_block + r * SEL_ROWS, qb * rows_per_block + (r + 1) * SEL_ROWS)
                s = cur[rows, :] + lt_ref[0, trows, pl.ds(off, SEL_TK)]
                cols = [s[:, j * LANE:(j + 1) * LANE] for j in range(ncol)]
                mx = functools.reduce(jnp.maximum, cols)
                m_old = m_s[rows, :]
                m_new = jnp.maximum(m_old, jnp.max(mx, axis=-1, keepdims=True))
                m_s[rows, :] = m_new
                a_buf[rows, :] = jnp.exp(m_old - m_new)
                p_buf[rows, :] = jnp.concatenate([jnp.exp(col - m_new) for col in cols], axis=1).astype(BF16)
        acc_s[...] = a_buf[...] * acc_s[...] + _dot(p_buf[...], v_ref[0, 0, key_rows(i), :])

    scores(0, s0_s)

    def pair(j, carry):
        tile(2 * j, s0_s, s1_s, p0_s, a0_s)
        tile(2 * j + 1, s1_s, s0_s, p1_s, a1_s)
        return carry

    lax.fori_loop(0, lax.div(n_tiles + 1, 2), pair, 0)
    acc = acc_s[...]
    o = acc * (1.0 / acc[:, HEAD_DIM:HEAD_DIM + 1])
    gates = _sigmoid(zg_ref[...])
    for qb in range(SEL_QB):
        _store_heads(o_ref, o[qb * rows_per_block:(qb + 1) * rows_per_block, :],
                     gates[qb * Q_BLOCK:(qb + 1) * Q_BLOCK, :], 1, qb * Q_BLOCK)


def _sel_branch(z, ks_aug, vs, mt, lt_tab, cq_tab, bsz, t):
    tq = SEL_QB * Q_BLOCK
    nq = t // tq
    nsel = mt.shape[-1]
    trows = NSA_HPG * Q_BLOCK
    rows = SEL_QB * trows
    return pl.pallas_call(
        _sel_kernel,
        grid=(bsz, NSA_GROUPS, nq),
        in_specs=[pl.BlockSpec((tq, NSA_HPG * LANE), lambda b, g, c: (b * nq + c, g)),
                  pl.BlockSpec((tq, LANE), lambda b, g, c: (b * nq + c, C_G3 // LANE + g)),
                  pl.BlockSpec((1, 1, t, 2 * LANE), lambda b, g, c: (b, g, 0, 0)),
                  pl.BlockSpec((1, 1, t, LANE), lambda b, g, c: (b, g, 0, 0)),
                  pl.BlockSpec((1, 1, tq, nsel), lambda b, g, c: (b, g, c, 0)),
                  pl.BlockSpec((1, trows, lt_tab.shape[2]), lambda b, g, c: (g, 0, 0)),
                  pl.BlockSpec((1, trows, LANE), lambda b, g, c: (g, 0, 0))],
        out_specs=pl.BlockSpec((1, NSA_HPG // 2, tq, LANE), lambda b, g, c: (b, g, c, 0)),
        out_shape=jax.ShapeDtypeStruct((bsz, NSA_HEADS // 2, t, LANE), BF16),
        scratch_shapes=[pltpu.VMEM((rows, LANE), F32),
                        pltpu.VMEM((rows, LANE), F32),
                        pltpu.VMEM((rows, SEL_TK), F32),
                        pltpu.VMEM((rows, SEL_TK), F32),
                        pltpu.VMEM((rows, SEL_TK), BF16),
                        pltpu.VMEM((rows, SEL_TK), BF16),
                        pltpu.VMEM((rows, LANE), F32),
                        pltpu.VMEM((rows, LANE), F32)],
        compiler_params=_cparams(("parallel", "parallel", "arbitrary")),
        name="nsa_selected",
    )(z, z, ks_aug, vs, mt, lt_tab, cq_tab)


def _shift_mix(z_ref, prev_ref, mu, first):
    z = z_ref[...]
    prev = jnp.where(first, 0.0, prev_ref[7:8, :])
    row = lax.broadcasted_iota(jnp.int32, z.shape, 0)
    zp = jnp.where(row == 0, prev, pltpu.roll(z, 1, 0))
    return z + mu * (zp - z)


def _rwkv_prep_kernel(has_vres, tiles_per_seq, *refs):
    (zr, zk, zv, zwa, zg, pr, pk, pv, pwa, pg, mu_ref, vec_ref, w2_ref, a2_ref, g2_ref, hsum_ref) = refs[:16]
    pos = 16
    if has_vres:
        vf_ref, v0_ref, v1_ref, v2_ref = refs[pos:pos + 4]
        pos += 4
    r_o, w_o, k_o, v_o, kk_o, b_o, g_o, bonus_o, vt_o = refs[pos:]
    first = lax.rem(pl.program_id(0), tiles_per_seq) == 0
    r = _shift_mix(zr, pr, mu_ref[0:1, :], first)
    k = _shift_mix(zk, pk, mu_ref[1:2, :], first)
    v = _shift_mix(zv, pv, mu_ref[2:3, :], first)
    wa = _shift_mix(zwa, pwa, mu_ref[3:4, 0:LANE], first)
    zg_s = _shift_mix(zg, pg, mu_ref[4:5, 0:LANE], first)
    w0, a0, k_k, k_a, r_k = (vec_ref[i:i + 1, :] for i in range(5))
    if has_vres:
        lora = _dot(_dot(v.astype(BF16), v1_ref[...]).astype(BF16), v2_ref[...])
        v = v + (vf_ref[...] - v) * _sigmoid(v0_ref[...] + lora)
    u = w0 + _dot(jnp.tanh(wa).astype(BF16), w2_ref[...])
    decay = jnp.exp(-math.exp(-0.5) * _sigmoid(u))
    a = _sigmoid(a0 + _dot(wa.astype(BF16), a2_ref[...]))
    g = _dot(_sigmoid(zg_s).astype(BF16), g2_ref[...])
    kk = k * k_k
    kk = kk / jnp.maximum(jnp.sqrt(_dot_split(kk * kk, hsum_ref[...])), 1e-12)
    k = k * (1.0 + (a - 1.0) * k_a)
    r_o[...] = r
    w_o[...] = decay
    k_o[...] = k.astype(k_o.dtype)
    v_o[...] = v
    kk_o[...] = kk
    b_o[...] = kk * a
    g_o[...] = g.astype(g_o.dtype)
    bonus_o[...] = (_dot_split(r * k * r_k, hsum_ref[...]) * v).astype(bonus_o.dtype)
    vt_o[0] = v.T.astype(BF16)


def _rwkv_prep(z, mu5, vecs, w2p, a2p, g2, hsum, vres, bsz, t, tm=256):
    n = bsz * t
    nt = n // tm
    tps = t // tm
    w = RWKV_WIDTH

    def cur(width, col):
        return pl.BlockSpec((tm, width), lambda i: (i, col // width))

    def prev(width, col):
        return pl.BlockSpec((8, width), lambda i: (jnp.maximum(i * (tm // 8) - 1, 0), col // width))

    def full(shape):
        return pl.BlockSpec(shape, lambda i: (0,) * len(shape))

    cols = [(w, C_R), (w, C_K), (w, C_V), (LANE, C_WA), (LANE, C_ZG)]
    in_specs = [cur(*c) for c in cols] + [prev(*c) for c in cols]
    in_specs += [full(mu5.shape), full(vecs.shape), full(w2p.shape), full(a2p.shape), full(g2.shape), full(hsum.shape)]
    args = [z] * 10 + [mu5, vecs, w2p, a2p, g2, hsum]
    if vres is not None:
        v_first, v0, v1p, v2p = vres
        in_specs += [pl.BlockSpec((tm, w), lambda i: (i, 0)), full(v0.shape), full(v1p.shape), full(v2p.shape)]
        args += [v_first, v0, v1p, v2p]
    tok = pl.BlockSpec((tm, w), lambda i: (i, 0))
    out_specs = [tok] * 8 + [pl.BlockSpec((1, w, tm), lambda i: (i // tps, 0, i % tps))]
    tok_dtypes = (F32, F32, BF16, F32, F32, F32, BF16, BF16)
    out_shape = [jax.ShapeDtypeStruct((n, w), dt) for dt in tok_dtypes] + [jax.ShapeDtypeStruct((bsz, w, t), BF16)]
    return pl.pallas_call(
        functools.partial(_rwkv_prep_kernel, vres is not None, tps),
        grid=(nt,),
        in_specs=in_specs,
        out_specs=out_specs,
        out_shape=out_shape,
        compiler_params=_cparams(("parallel",)),
        name="rwkv_prep",
    )(*args)


QUAD = 4
QUAD_LANES = QUAD * RWKV_HEAD
SUB_STEPS = LANE // QUAD


def _scan_kernel(r_ref, w_ref, k_ref, kk_ref, b_ref, vt_ref, ones_ref, place_ref, o_ref, s_ref, vq_ref, kq_ref, vk0_ref,
                 vk1_ref):
    bsz = r_ref.shape[0]
    tt = r_ref.shape[1]
    nh = RWKV_HEAD
    quads = [(bi, hf) for bi in range(bsz) for hf in range(RWKV_HEADS // QUAD)]

    @pl.when(pl.program_id(0) == 0)
    def _():
        s_ref[...] = jnp.zeros(s_ref.shape, F32)

    sub_shift, head_shift = int(math.log2(SUB_STEPS)), int(math.log2(nh))
    sub_lane = lax.broadcasted_iota(jnp.int32, (nh, LANE), 1) & (SUB_STEPS - 1)
    k_row_head = jnp.right_shift(lax.broadcasted_iota(jnp.int32, (LANE, QUAD_LANES), 0), sub_shift)
    k_lane_head = jnp.right_shift(lax.broadcasted_iota(jnp.int32, (LANE, QUAD_LANES), 1), head_shift)
    r_mask = (lax.broadcasted_iota(jnp.int32, (16, QUAD_LANES), 0)
              == jnp.right_shift(lax.broadcasted_iota(jnp.int32, (16, QUAD_LANES), 1), head_shift))
    ones_bd = ones_ref[...]

    def sub_block(sb, carry):
        s0 = pl.multiple_of(sb * SUB_STEPS, SUB_STEPS)
        for q, (bi, hf) in enumerate(quads):
            ln = slice(hf * QUAD_LANES, (hf + 1) * QUAD_LANES)
            vq = _dot(vt_ref[bi, (QUAD * hf) * nh:(QUAD * hf + 1) * nh, :], place_ref[sb * QUAD])
            for h in range(1, QUAD):
                vq = vq + _dot(vt_ref[bi, (QUAD * hf + h) * nh:(QUAD * hf + h + 1) * nh, :], place_ref[sb * QUAD + h])
            vq_ref[q] = vq.astype(BF16)
            k_sub = k_ref[bi, pl.ds(s0, SUB_STEPS), ln].astype(F32)
            kq_ref[q] = jnp.where(k_row_head == k_lane_head, jnp.concatenate([k_sub] * QUAD, axis=0), 0.0).astype(BF16)

        def group(tg, vk_ref, carry):
            t0 = pl.multiple_of(s0 + tg * 8, 8)
            rows = []
            for bi, hf in quads:
                ln = slice(hf * QUAD_LANES, (hf + 1) * QUAD_LANES)
                rows.append(tuple(ref[bi, pl.ds(t0, 8), ln] for ref in (w_ref, kk_ref, b_ref, r_ref)))
            onehots = [jnp.where(sub_lane == tg * 8 + u, 1.0, 0.0).astype(BF16) for u in range(8)]
            for q in range(len(quads)):
                vq = vq_ref[q]
                vk_ref[q] = _dot(jnp.concatenate([vq * oh for oh in onehots], axis=0), kq_ref[q])
            state = [s_ref[q] for q in range(len(quads))]
            for u in range(8):
                sk = _dot(jnp.concatenate([(state[q] * rows[q][1][u:u + 1, :]).astype(BF16)
                                           for q in range(len(quads))], axis=0), ones_bd)
                for q, (bi, hf) in enumerate(quads):
                    w8, _, b8, r8 = rows[q]
                    state[q] = (state[q] * w8[u:u + 1, :] - sk[q * nh:(q + 1) * nh, :] * b8[u:u + 1, :]
                                + vk_ref[q, u * nh:(u + 1) * nh, :])
                    r_lhs = jnp.where(r_mask, r8[u:u + 1, :], 0.0).astype(BF16)
                    o = _dot_nt(r_lhs, state[q].astype(BF16))
                    o_ref[t0 + u, bi, QUAD * hf:QUAD * (hf + 1), :] = o[0:QUAD, :]
            for q in range(len(quads)):
                s_ref[q] = state[q]
            return carry

        for tg in range(SUB_STEPS // 8):
            carry = group(tg, (vk0_ref, vk1_ref)[tg % 2], carry)
        return carry

    lax.fori_loop(0, tt // SUB_STEPS, sub_block, 0)


def _rwkv_scan(r, w, k, kk, b, vt, ones_bd, place, bsz, t):
    tt = SCAN_TT
    wd = RWKV_WIDTH
    nquad = bsz * RWKV_HEADS // QUAD
    tok = pl.BlockSpec((bsz, tt, wd), lambda i: (0, i, 0))
    return pl.pallas_call(
        _scan_kernel,
        grid=(t // tt,),
        in_specs=[tok, tok, tok, tok, tok,
                  pl.BlockSpec((bsz, wd, tt), lambda i: (0, 0, i)),
                  pl.BlockSpec(ones_bd.shape, lambda i: (0, 0)),
                  pl.BlockSpec(place.shape, lambda i: (0, 0, 0))],
        out_specs=pl.BlockSpec((tt, bsz, RWKV_HEADS, RWKV_HEAD), lambda i: (i, 0, 0, 0)),
        out_shape=jax.ShapeDtypeStruct((t, bsz, RWKV_HEADS, RWKV_HEAD), F32),
        scratch_shapes=[pltpu.VMEM((nquad, RWKV_HEAD, QUAD_LANES), F32),
                        pltpu.VMEM((nquad, RWKV_HEAD, LANE), BF16),
                        pltpu.VMEM((nquad, LANE, QUAD_LANES), BF16),
                        pltpu.VMEM((nquad, 8 * RWKV_HEAD, QUAD_LANES), F32),
                        pltpu.VMEM((nquad, 8 * RWKV_HEAD, QUAD_LANES), F32)],
        compiler_params=_cparams(("arbitrary",)),
        name="rwkv_scan",
    )(r, w, k, kk, b, vt, ones_bd, place)


def _merge_kernel(x_ref, oc_ref, os_ref, ow_ref, orw_ref, bonus_ref, g_ref, zgn_ref, zgr_ref, gn_ref, hsum_ref,
                  pn_ref, pr_ref, wo_ref, ln_ref, o_ref):
    o_n = jnp.concatenate([oc_ref[0, p].astype(F32) + os_ref[0, p].astype(F32) + ow_ref[0, p].astype(F32)
                           for p in range(NSA_HEADS // 2)], axis=1)
    y_n = _dot(o_n.astype(BF16), pn_ref[...])
    o = orw_ref[...]
    mu = _dot_split(o, hsum_ref[...]) * (1.0 / RWKV_HEAD)
    d = o - mu
    var = _dot_split(d * d, hsum_ref[...]) * (1.0 / RWKV_HEAD)
    o = d * lax.rsqrt(var + GN_EPS) * gn_ref[0:1, :] + gn_ref[1:2, :]
    o = ((o + bonus_ref[...]) * g_ref[...]).astype(BF16)
    y1 = _sigmoid(zgn_ref[...]) * y_n + _sigmoid(zgr_ref[...]) * _dot(o, pr_ref[...])
    y = _dot(y1.astype(BF16), wo_ref[...])
    o_ref[...] = _layer_norm(DN_ALPHA * x_ref[...] + y, ln_ref[0:1, :], ln_ref[1:2, :])


def _merge(x, o_c, o_s, o_w, o_rw, bonus, g, z, gn, hsum, p_nsa, p_rwkv, w_out, ln, bsz, t, tm=256):
    n = bsz * t
    tps = t // tm
    d = D_MODEL
    w = RWKV_WIDTH
    head = pl.BlockSpec((1, NSA_HEADS // 2, tm, LANE), lambda i: (i // tps, 0, i % tps, 0))
    tok = pl.BlockSpec((tm, w), lambda i: (i, 0))

    def full(a):
        return pl.BlockSpec(a.shape, lambda i: (0,) * a.ndim)

    return pl.pallas_call(
        _merge_kernel,
        grid=(n // tm,),
        in_specs=[pl.BlockSpec((tm, d), lambda i: (i, 0)), head, head, head, tok, tok, tok,
                  pl.BlockSpec((tm, d), lambda i: (i, C_GN // d)),
                  pl.BlockSpec((tm, d), lambda i: (i, C_GR // d)),
                  full(gn), full(hsum), full(p_nsa), full(p_rwkv), full(w_out), full(ln)],
        out_specs=pl.BlockSpec((tm, d), lambda i: (i, 0)),
        out_shape=jax.ShapeDtypeStruct((n, d), F32),
        compiler_params=_cparams(("parallel",)),
        name="mix_merge_ln",
    )(x, o_c, o_s, o_w, o_rw, bonus, g, z, z, gn, hsum, p_nsa, p_rwkv, w_out, ln)


def _xattn_kernel(x_ref, kt_ref, v_ref, wq_ref, wo_ref, ln_ref, rw_ref, o_ref, ob_ref, s_ref):
    x = x_ref[...]
    q = _dot(x.astype(BF16), wq_ref[...]).astype(BF16)
    outs = []
    for h in range(XATTN_HEADS):
        hs = slice(h * XATTN_HEAD, (h + 1) * XATTN_HEAD)
        s = _dot(q[:, hs], kt_ref[0, hs, :]) * (XATTN_HEAD ** -0.5)
        m = jnp.max(s, axis=-1, keepdims=True)
        p = jnp.exp(s - m)
        l = jnp.sum(p, axis=-1, keepdims=True)
        outs.append(_dot(p.astype(BF16), v_ref[0, :, hs]) * (1.0 / l))
    o = jnp.concatenate(outs, axis=1).astype(BF16)
    x2 = _layer_norm(DN_ALPHA * x + _dot(o, wo_ref[...]), ln_ref[0:1, :], ln_ref[1:2, :])
    o_ref[...] = x2
    hi = x2.astype(BF16)
    ob_ref[...] = hi
    lo = (x2 - hi.astype(F32)).astype(BF16)
    logits = _dot(hi, rw_ref[0]) + (_dot(lo, rw_ref[0]) + _dot(hi, rw_ref[1]))
    s_ref[...] = _sigmoid(logits)


def _xattn(x, kt, v, wq, wo, ln, rw2, bsz, t, tm=256):
    n = bsz * t
    tps = t // tm
    d = D_MODEL

    def full(a):
        return pl.BlockSpec(a.shape, lambda i: (0,) * a.ndim)

    return pl.pallas_call(
        _xattn_kernel,
        grid=(n // tm,),
        in_specs=[pl.BlockSpec((tm, d), lambda i: (i, 0)),
                  pl.BlockSpec((1,) + kt.shape[1:], lambda i: (i // tps, 0, 0)),
                  pl.BlockSpec((1,) + v.shape[1:], lambda i: (i // tps, 0, 0)),
                  full(wq), full(wo), full(ln), full(rw2)],
        out_specs=[pl.BlockSpec((tm, d), lambda i: (i, 0)), pl.BlockSpec((tm, d), lambda i: (i, 0)),
                   pl.BlockSpec((tm, LANE), lambda i: (i, 0))],
        out_shape=[jax.ShapeDtypeStruct((n, d), F32), jax.ShapeDtypeStruct((n, d), BF16),
                   jax.ShapeDtypeStruct((n, LANE), F32)],
        compiler_params=_cparams(("parallel",)),
        name="xattn_ln_router",
    )(x, kt, v, wq, wo, ln, rw2)


def _expert_kernel(meta_ref, x_ref, gate_ref, *refs):
    w_refs, o_ref, w_scr = refs[:6], refs[6], refs[7:]
    i = pl.program_id(0)
    nb = pl.num_programs(0)
    prev = jnp.maximum(i - 1, 0)

    for side in range(2):
        @pl.when((i == 0) | (meta_ref[side * nb + i] != meta_ref[side * nb + prev]))
        def _():
            for j in range(3):
                w_scr[3 * side + j][...] = w_refs[3 * side + j][0, 0].astype(BF16)

    n_used = meta_ref[2 * nb]

    @pl.when(i < n_used)
    def _():
        x = x_ref[...]
        y = jnp.zeros(o_ref.shape, F32)
        for side in range(2):
            wg, wu, wd = w_scr[3 * side:3 * side + 3]
            hg = _dot(x, wg[...])
            h = hg * _sigmoid(hg) * _dot(x, wu[...])
            y = y + gate_ref[:, side:side + 1] * _dot(h.astype(BF16), wd[...])
        o_ref[...] = y.astype(o_ref.dtype)

    @pl.when(i >= n_used)
    def _():
        o_ref[...] = jnp.zeros(o_ref.shape, o_ref.dtype)


def _experts(meta, rows, gates, w_gate, w_up, w_down, layer):
    n_rows, d = rows.shape
    ff = w_gate.shape[3]
    nb = n_rows // MOE_BLK

    def weight(shape, side):
        return pl.BlockSpec((1, 1) + shape, lambda i, m: (layer, m[side * nb + i], 0, 0))

    return pl.pallas_call(
        _expert_kernel,
        grid_spec=pltpu.PrefetchScalarGridSpec(
            num_scalar_prefetch=1,
            grid=(nb,),
            in_specs=[pl.BlockSpec((MOE_BLK, d), lambda i, m: (i, 0)),
                      pl.BlockSpec((MOE_BLK, TOP_K), lambda i, m: (i, 0))]
                     + [weight(s, side) for side in range(2) for s in ((d, ff), (d, ff), (ff, d))],
            out_specs=pl.BlockSpec((MOE_BLK, d), lambda i, m: (i, 0)),
            scratch_shapes=[pltpu.VMEM(s, BF16) for _ in range(2) for s in ((d, ff), (d, ff), (ff, d))]),
        out_shape=jax.ShapeDtypeStruct((n_rows, d), BF16),
        compiler_params=_cparams(("arbitrary",)),
        name="moe_experts",
    )(meta, rows, gates, w_gate, w_up, w_down, w_gate, w_up, w_down)


def _combine_kernel(x_ref, y_ref, ln_ref, o_ref):
    o_ref[...] = _layer_norm(DN_ALPHA * x_ref[...] + y_ref[...], ln_ref[0:1, :], ln_ref[1:2, :])


def _combine(x, y, ln, tm=512):
    n, d = x.shape
    tok = pl.BlockSpec((tm, d), lambda i: (i, 0))
    return pl.pallas_call(
        _combine_kernel,
        grid=(n // tm,),
        in_specs=[tok, tok, pl.BlockSpec(ln.shape, lambda i: (0, 0))],
        out_specs=tok,
        out_shape=jax.ShapeDtypeStruct((n, d), F32),
        compiler_params=_cparams(("parallel",)),
        name="moe_combine_ln",
    )(x, y, ln)


def _t5_bucket(dist):
    n = jnp.maximum(dist, 0)
    max_exact = REL_BUCKETS // 2
    log_ratio = jnp.log(jnp.maximum(n, 1).astype(F32) / max_exact) / math.log(REL_MAX_DIST / max_exact)
    large = jnp.minimum(max_exact + (log_ratio * (REL_BUCKETS - max_exact)).astype(jnp.int32), REL_BUCKETS - 1)
    return jnp.where(n < max_exact, n, large)


def _bias_tables(rel_table, t):
    nc = t // CMP_STRIDE
    f = rel_table[_t5_bucket(jnp.arange(t + 2048))].astype(F32).T
    ql = jnp.arange(Q_BLOCK)[:, None]

    chunks_per_q = Q_BLOCK // CMP_STRIDE
    r = np.arange(CMP_STRIDE)[:, None]
    m = np.arange(2 * nc + chunks_per_q - 1)[None, :]
    dist_c = CMP_STRIDE * (nc - 8 + chunks_per_q - 1 - m) + r - (CMP_LEN - 1)
    small = jnp.where((dist_c >= 0)[None], jnp.take(f, np.clip(dist_c, 0, f.shape[-1] - 1), axis=1), NEG_INF)
    pc = jnp.stack([small[:, :, chunks_per_q - 1 - a:chunks_per_q - 1 - a + 2 * nc] for a in range(chunks_per_q)], axis=1)
    pc = pc.reshape(NSA_GROUPS, NSA_HPG * Q_BLOCK, 2 * nc)
    pc = jnp.stack([pc[:, :, s:s + 2 * nc - LANE] for s in range(0, LANE, 8)], axis=1)
    nvar = WINDOW // Q_BLOCK + 1

    def toeplitz(base, width, hi, shift):
        period = width + Q_BLOCK
        m = np.arange(period)
        dist = base - np.where(m < width, m, m - period)
        row = jnp.take(f, np.clip(dist, 0, f.shape[-1] - 1), axis=1) - shift
        row = jnp.where(((dist >= 0) & (dist < hi))[None], row, NEG_INF)
        flat = jnp.tile(row, (1, Q_BLOCK))[:, :Q_BLOCK * (period - 1)]
        return flat.reshape(NSA_HEADS, Q_BLOCK, period - 1)[:, :, :width]

    wb = jnp.stack([toeplitz(Q_BLOCK * v, WINDOW + Q_BLOCK, WINDOW, 0.0) for v in range(nvar)], axis=1)
    wb = wb.reshape(NSA_GROUPS, NSA_HPG, nvar, Q_BLOCK, -1).transpose(0, 2, 1, 3, 4)
    wb = wb.reshape(NSA_GROUPS, nvar, NSA_HPG * Q_BLOCK, -1)
    far = rel_table[REL_BUCKETS - 1].astype(F32)
    lt = toeplitz(SEL_NEAR + SEL_TK, SEL_NEAR + 2 * SEL_TK + Q_BLOCK, f.shape[-1], far[:, None])
    lt = lt.reshape(NSA_GROUPS, NSA_HPG * Q_BLOCK, -1)
    far_hi = far.astype(BF16).astype(F32)
    lane = jnp.arange(LANE)[None, :]
    cq = jnp.where(lane == HEAD_DIM, far_hi[:, None], jnp.where(lane == HEAD_DIM + 1, (far - far_hi)[:, None], 0.0))
    cq = jnp.broadcast_to(cq[:, None, :], (NSA_HEADS, Q_BLOCK, LANE)).reshape(NSA_GROUPS, NSA_HPG * Q_BLOCK, LANE)
    return pc, wb, lt, cq


def _in_proj_perm():
    perm = np.full((IN_PAD,), -1, np.int64)
    for h in range(NSA_HEADS):
        perm[C_Q + h * LANE:C_Q + h * LANE + HEAD_DIM] = np.arange(HEAD_DIM) + h * HEAD_DIM
    rw0 = NSA_COLS
    g0 = NSA_COLS + RWKV_COLS
    perm[C_GN:C_GN + D_MODEL] = g0 + np.arange(D_MODEL)
    perm[C_GR:C_GR + D_MODEL] = g0 + D_MODEL + np.arange(D_MODEL)
    perm[C_R:C_R + RWKV_WIDTH] = rw0 + np.arange(RWKV_WIDTH)
    perm[C_K:C_K + RWKV_WIDTH] = rw0 + RWKV_WIDTH + np.arange(RWKV_WIDTH)
    perm[C_V:C_V + RWKV_WIDTH] = rw0 + 2 * RWKV_WIDTH + np.arange(RWKV_WIDTH)
    for g in range(NSA_GROUPS):
        for br in range(3):
            for h in range(NSA_HPG):
                perm[C_G3 + g * LANE + br * NSA_HPG + h] = NSA_WIDTH + 6 * NSA_KV + (g * NSA_HPG + h) * 3 + br
    perm[C_WA:C_WA + LORA_W + LORA_A] = rw0 + 3 * RWKV_WIDTH + np.arange(LORA_W + LORA_A)
    perm[C_ZG:C_ZG + LORA_G] = rw0 + 3 * RWKV_WIDTH + LORA_W + LORA_A + np.arange(LORA_G)
    return perm


_PERM = _in_proj_perm()


def _permute_in_proj(w):
    pieces = []
    start = 0
    while start < IN_PAD:
        stop = start + 1
        if _PERM[start] < 0:
            while stop < IN_PAD and _PERM[stop] < 0:
                stop += 1
            pieces.append(jnp.zeros((w.shape[0], stop - start), BF16))
        else:
            while stop < IN_PAD and _PERM[stop] == _PERM[stop - 1] + 1:
                stop += 1
            pieces.append(w[:, int(_PERM[start]):int(_PERM[stop - 1]) + 1].astype(BF16))
        start = stop
    return jnp.concatenate(pieces, axis=1)


def _pad_to(a, shape):
    return jnp.pad(a, [(0, s - d) for d, s in zip(a.shape, shape)])


_CLASS_EXPERTS = np.asarray([(grp * EXPERTS_PER_GROUP + i, grp * EXPERTS_PER_GROUP + j)
                             for grp in range(N_EXPERT_GROUPS)
                             for i in range(EXPERTS_PER_GROUP) for j in range(i + 1, EXPERTS_PER_GROUP)], np.int32)
N_PAIR_CLASSES = len(_CLASS_EXPERTS)


def _route(s, router_bias, n_tok):
    s16 = s[:, :N_EXPERTS]
    s_sel = s16 + router_bias.astype(F32)

    def top2(a):
        i1 = jnp.argmax(a, axis=-1)
        rest = jnp.where(jnp.arange(a.shape[-1]) == i1[..., None], -jnp.inf, a)
        i2 = jnp.argmax(rest, axis=-1)
        return jnp.max(a, axis=-1) + jnp.max(rest, axis=-1), jnp.stack([i1, i2], axis=-1)

    group_score, _ = top2(s_sel.reshape(n_tok, N_EXPERT_GROUPS, EXPERTS_PER_GROUP))
    group = jnp.argmax(group_score, axis=-1)
    in_group = (jnp.arange(N_EXPERTS) // EXPERTS_PER_GROUP)[None, :] == group[:, None]
    _, e_idx = top2(jnp.where(in_group, s_sel, NEG_INF))
    picked = jnp.arange(N_EXPERTS)[None, None, :] == e_idx[:, :, None]
    s_top = jnp.sum(jnp.where(picked, s16[:, None, :], 0.0), axis=-1)
    gate = s_top / jnp.sum(s_top, axis=-1, keepdims=True)
    swap = e_idx[:, 0] > e_idx[:, 1]
    e_lo = jnp.where(swap, e_idx[:, 1], e_idx[:, 0])
    e_hi = jnp.where(swap, e_idx[:, 0], e_idx[:, 1])
    gate_lohi = jnp.where(swap[:, None], gate[:, ::-1], gate)
    l_lo, l_hi = e_lo % EXPERTS_PER_GROUP, e_hi % EXPERTS_PER_GROUP
    pairs_per_group = N_PAIR_CLASSES // N_EXPERT_GROUPS
    cls = ((e_lo // EXPERTS_PER_GROUP) * pairs_per_group
           + (l_lo * (2 * EXPERTS_PER_GROUP - 1 - l_lo)) // 2 + (l_hi - l_lo - 1))
    onehot = (cls[:, None] == jnp.arange(N_PAIR_CLASSES)[None, :]).astype(jnp.int32)
    counts = jnp.sum(onehot, axis=0)
    padded = (counts + MOE_BLK - 1) // MOE_BLK * MOE_BLK
    ends = jnp.cumsum(padded)
    before = jnp.cumsum(onehot, axis=0) - onehot + (ends - padded)[None, :]
    dest = jnp.sum(onehot * before, axis=1)
    n_rows = n_tok + N_PAIR_CLASSES * MOE_BLK
    tok_of_row = jnp.zeros((n_rows,), jnp.int32).at[dest].set(jnp.arange(n_tok, dtype=jnp.int32))
    block_start = jnp.arange(n_rows // MOE_BLK) * MOE_BLK
    block_cls = jnp.minimum(jnp.sum((ends[None, :] <= block_start[:, None]).astype(jnp.int32), axis=1),
                            N_PAIR_CLASSES - 1)
    block_onehot = (block_cls[:, None] == jnp.arange(N_PAIR_CLASSES)[None, :]).astype(jnp.int32)
    block_experts = jnp.sum(block_onehot[:, :, None] * jnp.asarray(_CLASS_EXPERTS)[None], axis=1)
    meta = jnp.concatenate([block_experts[:, 0], block_experts[:, 1], ends[-1:] // MOE_BLK]).astype(jnp.int32)
    return gate_lohi, dest, tok_of_row, meta


def kernel(x, mem, rel_table, router_w, router_bias, w_in, cmp_pe_k, cmp_pe_v, cmp_w1k, cmp_w2k, cmp_w1v, cmp_w2v, rwkv_mu, rwkv_w0, rwkv_w2, rwkv_a0, rwkv_a2, rwkv_g2, rwkv_kk, rwkv_ka, rwkv_rk, rwkv_gn_g, rwkv_gn_b, rwkv_v0, rwkv_v1, rwkv_v2, p_nsa, p_rwkv, w_out, ln1_g, ln1_b, xq_w, xk_w, xv_w, xo_w, ln2_g, ln2_b, moe_w_gate, moe_w_up, moe_w_down, ln3_g, ln3_b):
    bsz, t, d = x.shape
    n = bsz * t
    depth = w_in.shape[0]
    nc = t // CMP_STRIDE
    nsel = t // SEL_BLOCK
    g, hd = NSA_GROUPS, HEAD_DIM

    pc_tab, wb_tab, lt_tab, cq_tab = _bias_tables(rel_table, t)
    cmp_start = np.arange(nc) * CMP_STRIDE
    sel_start = np.arange(nsel) * SEL_BLOCK
    overlap = ((cmp_start[:, None] <= sel_start[None, :] + SEL_BLOCK - 1)
               & (cmp_start[:, None] + CMP_LEN - 1 >= sel_start[None, :]) & (cmp_start[:, None] < (nc - 1) * CMP_STRIDE))
    overlap = jnp.asarray(overlap.T, BF16)
    lane_head = np.arange(RWKV_WIDTH) // RWKV_HEAD
    hsum = jnp.asarray(lane_head[:, None] == lane_head[None, :], BF16)
    quad_head = np.arange(QUAD_LANES) // RWKV_HEAD
    ones_bd = jnp.asarray(quad_head[:, None] == quad_head[None, :], BF16)
    step = np.arange(SCAN_TT)[None, :, None]
    lane = np.arange(LANE)[None, None, :]
    sbh = np.arange(SCAN_TT // SUB_STEPS * QUAD)[:, None, None]
    place = jnp.asarray((lane // SUB_STEPS == sbh % QUAD) & (step == (sbh // QUAD) * SUB_STEPS + lane % SUB_STEPS), BF16)
    rw_f = _pad_to(router_w.astype(F32), (d, LANE))
    rw_hi = rw_f.astype(BF16)
    rw2 = jnp.stack([rw_hi, (rw_f - rw_hi.astype(F32)).astype(BF16)])

    xf = x.reshape(n, d)
    mem_f = mem.reshape(bsz * mem.shape[1], d)
    v_first = None
    for l in range(depth):
        w_in_l = _permute_in_proj(w_in[l])
        z = _matmul(xf, w_in_l, F32, 1024, 1024)

        kv0 = NSA_WIDTH
        zeros64 = jnp.zeros((d, LANE - hd), BF16)
        w_kv = jnp.stack([jnp.concatenate(
            [piece for i in range(KV_ARRAYS)
             for piece in (w_in[l][:, kv0 + i * NSA_KV + gi * hd:kv0 + i * NSA_KV + (gi + 1) * hd].astype(BF16), zeros64)],
            axis=1) for gi in range(g)])
        kc, vc, ks_aug, vs_aug, kw_pad, vw_aug = _kv_proj(xf, w_kv, bsz, t)

        def cmp_w(pe, w1, w2):
            return (pe.reshape(2, CMP_STRIDE * hd), w1.reshape(2, CMP_STRIDE * hd, CMP_HIDDEN).astype(BF16),
                    _pad_to(w2, (CMP_HIDDEN, LANE)).astype(BF16))

        chunks = lambda a: a.reshape(bsz, g, nc, CMP_STRIDE * hd)
        k_cmp = _compress(chunks(kc), *cmp_w(cmp_pe_k[l], cmp_w1k[l], cmp_w2k[l]))
        v_cmp = _compress(chunks(vc), *cmp_w(cmp_pe_v[l], cmp_w1v[l], cmp_w2v[l]))
        o_c, sel_mask = _cmp_branch(z, k_cmp, v_cmp, pc_tab, overlap, bsz, t)
        o_s = _sel_branch(z, ks_aug, vs_aug, sel_mask, lt_tab, cq_tab, bsz, t)
        o_w = _win_branch(z, kw_pad, vw_aug, wb_tab, bsz, t)

        mu = rwkv_mu[l]
        w3 = 3 * RWKV_WIDTH
        mu5 = jnp.stack([mu[0:512], mu[512:1024], mu[1024:1536], _pad_to(mu[w3:w3 + 128], (512,)),
                         _pad_to(mu[w3 + 128:w3 + 256], (512,))])
        vecs = jnp.stack([rwkv_w0[l], rwkv_a0[l], rwkv_kk[l], rwkv_ka[l], rwkv_rk[l].reshape(-1)])
        w2p = jnp.concatenate([rwkv_w2[l], jnp.zeros_like(rwkv_a2[l])], axis=0).astype(BF16)
        a2p = jnp.concatenate([jnp.zeros_like(rwkv_w2[l]), rwkv_a2[l]], axis=0).astype(BF16)
        vres = None
        if l > 0:
            vres = (v_first, rwkv_v0[l - 1][None, :], _pad_to(rwkv_v1[l - 1], (RWKV_WIDTH, LANE)).astype(BF16),
                    _pad_to(rwkv_v2[l - 1], (LANE, RWKV_WIDTH)).astype(BF16))
        r, w, k, v, kk, b, g_out, bonus, vt = _rwkv_prep(z, mu5, vecs, w2p, a2p, rwkv_g2[l].astype(BF16), hsum, vres,
                                                          bsz, t)
        if l == 0:
            v_first = v
        seq = lambda a: a.reshape(bsz, t, RWKV_WIDTH)
        o_rw = _rwkv_scan(seq(r), seq(w), seq(k), seq(kk), seq(b), vt, ones_bd, place, bsz, t)
        o_rw = o_rw.transpose(1, 0, 2, 3).reshape(n, RWKV_WIDTH)

        xf = _merge(xf, o_c, o_s, o_w, o_rw, bonus, g_out, z, jnp.stack([rwkv_gn_g[l], rwkv_gn_b[l]]), hsum,
                    p_nsa[l].astype(BF16), p_rwkv[l].astype(BF16), w_out[l].astype(BF16),
                    jnp.stack([ln1_g[l], ln1_b[l]]), bsz, t)

        mlen = mem.shape[1]
        mk = _matmul(mem_f, xk_w[l].astype(BF16), BF16, mlen, XATTN_WIDTH)
        mv = _matmul(mem_f, xv_w[l].astype(BF16), BF16, mlen, XATTN_WIDTH)
        kt = mk.reshape(bsz, mlen, XATTN_WIDTH).transpose(0, 2, 1)
        xf, xf_b, scores = _xattn(xf, kt, mv.reshape(bsz, mlen, XATTN_WIDTH), xq_w[l].astype(BF16), xo_w[l].astype(BF16),
                            jnp.stack([ln2_g[l], ln2_b[l]]), rw2, bsz, t)

        gate, dest, tok_of_row, meta = _route(scores, router_bias, n)
        rows = jnp.take(xf_b, tok_of_row, axis=0)
        y_rows = _experts(meta, rows, jnp.take(gate, tok_of_row, axis=0), moe_w_gate, moe_w_up, moe_w_down, l)
        xf = _combine(xf, jnp.take(y_rows, dest, axis=0), jnp.stack([ln3_g[l], ln3_b[l]]))
    return xf.reshape(bsz, t, d)
```

```python
import functools
import math

import numpy as np
import jax
import jax.numpy as jnp
from jax import lax
from jax.experimental import pallas as pl
from jax.experimental.pallas import tpu as pltpu

F32 = jnp.float32
BF16 = jnp.bfloat16

D_MODEL = 1024
DEPTH = 2
NSA_HEADS = 8
NSA_GROUPS = 2
NSA_HPG = NSA_HEADS // NSA_GROUPS
HEAD_DIM = 64
NSA_WIDTH = NSA_HEADS * HEAD_DIM
NSA_KV = NSA_GROUPS * HEAD_DIM
CMP_STRIDE = 16
CMP_LEN = 2 * CMP_STRIDE
CMP_HIDDEN = 256
SEL_BLOCK = 64
SEL_TOPN = 16
WINDOW = 512
Q_BLOCK = 128
SEL_FORCE = 1e9
RWKV_HEADS = 8
RWKV_HEAD = 64
RWKV_WIDTH = RWKV_HEADS * RWKV_HEAD
LORA_W = 64
LORA_A = 64
LORA_V = 32
LORA_G = 128
GN_EPS = 64e-5
REL_BUCKETS = 32
REL_MAX_DIST = 1024
XATTN_HEADS = 4
XATTN_HEAD = 128
XATTN_WIDTH = XATTN_HEADS * XATTN_HEAD
N_EXPERTS = 16
N_EXPERT_GROUPS = 4
EXPERTS_PER_GROUP = N_EXPERTS // N_EXPERT_GROUPS
TOP_K = 2
EXPERT_FF = 512
DN_ALPHA = (2 * DEPTH) ** 0.25
LN_EPS = 1e-5
NEG_INF = -1e30
NSA_COLS = NSA_WIDTH + 6 * NSA_KV + 3 * NSA_HEADS
RWKV_COLS = 3 * RWKV_WIDTH + LORA_W + LORA_A + LORA_G

LANE = 128
VMEM_LIMIT = 56 * 1024 * 1024

C_Q = 0
C_GN = 1024
C_GR = 2048
C_R = 3072
C_K = 3584
C_V = 4096
C_G3 = 4608
C_WA = 4864
C_ZG = 4992
IN_PAD = 5120
KV_ARRAYS = 6

MOE_BLK = 256
SCAN_TT = 128
FAR_BIAS_DIST = 1280


def _cparams(sem):
    return pltpu.CompilerParams(dimension_semantics=sem, vmem_limit_bytes=VMEM_LIMIT)


def _sigmoid(x):
    return 1.0 / (1.0 + jnp.exp(-x))


def _layer_norm(v, g, b):
    mu = jnp.mean(v, axis=-1, keepdims=True)
    d = v - mu
    var = jnp.mean(d * d, axis=-1, keepdims=True)
    return d * lax.rsqrt(var + LN_EPS) * g + b


def _dot(a, b):
    return jnp.dot(a, b, preferred_element_type=F32)


def _dot_nt(a, b):
    return lax.dot_general(a, b, (((1,), (1,)), ((), ())), preferred_element_type=F32)


def _dot_split(x, w):
    hi = x.astype(BF16)
    lo = (x - hi.astype(F32)).astype(BF16)
    return _dot(hi, w) + _dot(lo, w)


def _mm_kernel(x_ref, w_ref, o_ref, xb_ref):
    @pl.when(pl.program_id(1) == 0)
    def _():
        xb_ref[...] = x_ref[...].astype(BF16)

    o_ref[...] = _dot(xb_ref[...], w_ref[...]).astype(o_ref.dtype)


def _matmul(x, w, out_dtype, tm, tn):
    n, k = x.shape
    m = w.shape[1]
    return pl.pallas_call(
        _mm_kernel,
        grid=(n // tm, m // tn),
        in_specs=[pl.BlockSpec((tm, k), lambda i, j: (i, 0)),
                  pl.BlockSpec((k, tn), lambda i, j: (0, j))],
        out_specs=pl.BlockSpec((tm, tn), lambda i, j: (i, j)),
        out_shape=jax.ShapeDtypeStruct((n, m), out_dtype),
        scratch_shapes=[pltpu.VMEM((tm, k), BF16)],
        compiler_params=_cparams(("parallel", "arbitrary")),
        name="matmul",
    )(x, w)


def _kv_proj_kernel(tiles_per_seq, x_ref, w_ref, kc_ref, vc_ref, ks_ref, vs_ref, kw_ref, vw_ref, xb_ref):
    @pl.when(pl.program_id(1) == 0)
    def _():
        xb_ref[...] = x_ref[...].astype(BF16)

    y = _dot(xb_ref[...], w_ref[pl.program_id(1)])
    tm = y.shape[0]
    slot = lambda i: y[:, i * LANE:(i + 1) * LANE]
    lane = lax.broadcasted_iota(jnp.int32, (tm, LANE), 1)
    one_lane = jnp.where(lane == HEAD_DIM, 1.0, 0.0)
    t0 = lax.rem(pl.program_id(0), tiles_per_seq) * tm
    block = jnp.right_shift(t0 + lax.broadcasted_iota(jnp.int32, (tm, LANE), 0), int(math.log2(SEL_BLOCK)))
    kc_ref[0, 0] = slot(0)[:, :HEAD_DIM]
    vc_ref[0, 0] = slot(1)[:, :HEAD_DIM]
    ks_ref[0, 0] = jnp.concatenate([slot(2) + jnp.where(lane == HEAD_DIM + 1, 1.0, one_lane),
                                    jnp.where(lane == block, 1.0, 0.0)], axis=1).astype(BF16)
    vs_ref[0, 0] = (slot(3) + one_lane).astype(BF16)
    kw_ref[0, 0] = slot(4).astype(BF16)
    vw_ref[0, 0] = (slot(5) + one_lane).astype(BF16)


def _kv_proj(x, w_kv, bsz, t, tm=512):
    n, d = x.shape
    tps = t // tm
    g = NSA_GROUPS

    def out(width, dtype):
        return (pl.BlockSpec((1, 1, tm, width), lambda i, j: (i // tps, j, i % tps, 0)),
                jax.ShapeDtypeStruct((bsz, g, t, width), dtype))

    outs = [out(HEAD_DIM, F32), out(HEAD_DIM, F32), out(2 * LANE, BF16), out(LANE, BF16), out(LANE, BF16),
            out(LANE, BF16)]
    return pl.pallas_call(
        functools.partial(_kv_proj_kernel, tps),
        grid=(n // tm, g),
        in_specs=[pl.BlockSpec((tm, d), lambda i, j: (i, 0)),
                  pl.BlockSpec((g, d, KV_ARRAYS * LANE), lambda i, j: (0, 0, 0))],
        out_specs=[o[0] for o in outs],
        out_shape=[o[1] for o in outs],
        scratch_shapes=[pltpu.VMEM((tm, d), BF16)],
        compiler_params=_cparams(("parallel", "arbitrary")),
        name="nsa_kv_proj",
    )(x, w_kv)


def _compress_kernel(u_ref, pe_ref, w1_ref, w2_ref, o_ref):
    u = u_ref[0, 0]
    a = _dot((u + pe_ref[0:1, :]).astype(BF16), w1_ref[0])
    b = _dot((u + pe_ref[1:2, :]).astype(BF16), w1_ref[1])
    nc = u.shape[0]
    h = a + pltpu.roll(b, nc - 1, 0)
    h = 0.5 * h * (1.0 + jnp.tanh(math.sqrt(2.0 / math.pi) * (h + 0.044715 * (h * h * h))))
    y = _dot(h.astype(BF16), w2_ref[...])
    row = lax.broadcasted_iota(jnp.int32, y.shape, 0)
    o_ref[0, 0] = jnp.where(row < nc - 1, y, 0.0)


def _compress(u, pe2, w1, w2p):
    b, g, nc, _ = u.shape
    return pl.pallas_call(
        _compress_kernel,
        grid=(b, g),
        in_specs=[pl.BlockSpec((1, 1, nc, 1024), lambda i, j: (i, j, 0, 0)),
                  pl.BlockSpec((2, 1024), lambda i, j: (0, 0)),
                  pl.BlockSpec((2, 1024, CMP_HIDDEN), lambda i, j: (0, 0, 0)),
                  pl.BlockSpec((CMP_HIDDEN, LANE), lambda i, j: (0, 0))],
        out_specs=pl.BlockSpec((1, 1, nc, LANE), lambda i, j: (i, j, 0, 0)),
        out_shape=jax.ShapeDtypeStruct((b, g, nc, LANE), F32),
        compiler_params=_cparams(("parallel", "parallel")),
        name="nsa_compress",
    )(u, pe2, w1, w2p)


def _stack_heads(q_ref, row0=0):
    q = q_ref[row0:row0 + Q_BLOCK, :] * (HEAD_DIM ** -0.5)
    return jnp.concatenate([q[:, h * LANE:(h + 1) * LANE] for h in range(NSA_HPG)], axis=0).astype(BF16)


def _store_heads(o_ref, o, gates, branch, row0=0):
    lane = lax.broadcasted_iota(jnp.int32, (Q_BLOCK, LANE), 1)
    for pair in range(NSA_HPG // 2):
        halves = []
        for h in (2 * pair, 2 * pair + 1):
            c = branch * NSA_HPG + h
            halves.append(o[h * Q_BLOCK:(h + 1) * Q_BLOCK, :] * gates[:, c:c + 1])
        o_ref[0, pair, row0:row0 + Q_BLOCK, :] = jnp.where(lane < HEAD_DIM, halves[0],
                                                           pltpu.roll(halves[1], HEAD_DIM, 1)).astype(o_ref.dtype)


CMP_QB = 4


def _cmp_kernel(q_ref, zg_ref, kc_ref, vc_ref, *refs):
    pc_refs, (ov_ref, o_ref, m_ref), scratch = refs[:CMP_QB], refs[CMP_QB:CMP_QB + 3], refs[CMP_QB + 3:]
    nc = kc_ref.shape[2]
    nsel = ov_ref.shape[0]
    kc = kc_ref[0, 0].astype(BF16)
    vc = vc_ref[0, 0].astype(BF16)
    gates = _sigmoid(zg_ref[...])
    blocks = [(qb, pl.program_id(2) * CMP_QB + qb, pc_ref, scratch[2 * qb], scratch[2 * qb + 1])
              for qb, pc_ref in enumerate(pc_refs)]
    for qb, c, _, p_s, _ in blocks:
        p_s[...] = _dot_nt(_stack_heads(q_ref, qb * Q_BLOCK), kc)
    for qb, c, pc_ref, p_s, pb_s in blocks:
        off = pl.multiple_of(lax.div((nc - 8) - 8 * c, LANE) * LANE, LANE)
        for r in range(NSA_HPG * Q_BLOCK // SEL_ROWS):
            rows = slice(r * SEL_ROWS, (r + 1) * SEL_ROWS)
            s = p_s[rows, :] + pc_ref[0, 0, rows, pl.ds(off, nc)]
            e = jnp.exp(s - jnp.max(s, axis=-1, keepdims=True))
            l = jnp.sum(e, axis=-1, keepdims=True)
            ql = (lax.broadcasted_iota(jnp.int32, (SEL_ROWS, 1), 0) + r * SEL_ROWS) & (Q_BLOCK - 1)
            p = e * jnp.where((c * Q_BLOCK + ql) >= (CMP_LEN - 1), 1.0 / l, 0.0)
            p_s[rows, :] = p
            pb_s[rows, :] = p.astype(BF16)
    vals = []
    jb = lax.broadcasted_iota(jnp.int32, (nsel, Q_BLOCK), 0)
    jbf = jb.astype(F32)
    for qb, c, _, p_s, pb_s in blocks:
        _store_heads(o_ref, _dot(pb_s[...], vc), gates[qb * Q_BLOCK:(qb + 1) * Q_BLOCK, :], 0, qb * Q_BLOCK)
        psum = ((p_s[0:Q_BLOCK, :] + p_s[Q_BLOCK:2 * Q_BLOCK, :])
                + (p_s[2 * Q_BLOCK:3 * Q_BLOCK, :] + p_s[3 * Q_BLOCK:4 * Q_BLOCK, :]))
        hi = psum.astype(BF16)
        lo = (psum - hi.astype(F32)).astype(BF16)
        imp = _dot_nt(ov_ref[...], hi) + _dot_nt(ov_ref[...], lo)
        tq = c * Q_BLOCK + lax.broadcasted_iota(jnp.int32, (nsel, Q_BLOCK), 1)
        jq = jnp.right_shift(tq, int(math.log2(SEL_BLOCK)))
        forced = (jb == 0) | (jb == jq) | (jb == jq - 1)
        vals.append(jnp.where(forced, SEL_FORCE, jnp.where(jb <= jq, imp, -SEL_FORCE)))
    sels = [jnp.zeros((nsel, Q_BLOCK), F32) for _ in blocks]
    for _ in range(SEL_TOPN):
        for qb in range(CMP_QB):
            mx = jnp.max(vals[qb], axis=0, keepdims=True)
            first = jnp.min(jnp.where(vals[qb] == mx, jbf, float(nsel)), axis=0, keepdims=True)
            hit = jbf == first
            sels[qb] = jnp.where(hit, jnp.where(mx >= 0.0, 1.0, 0.0), sels[qb])
            vals[qb] = jnp.where(hit, -3e38, vals[qb])
    for qb in range(CMP_QB):
        m_ref[0, 0, qb * Q_BLOCK:(qb + 1) * Q_BLOCK, :] = (sels[qb].T - 1.0) * (-NEG_INF)


def _cmp_branch(z, k_cmp, v_cmp, pc_tab, overlap, bsz, t):
    tq = CMP_QB * Q_BLOCK
    nq = t // tq
    nc = k_cmp.shape[2]
    nsel = overlap.shape[0]
    rows = NSA_HPG * Q_BLOCK

    def variant(qb):
        return pl.BlockSpec((1, 1, rows, 2 * nc - LANE),
                            lambda b, g, c: (g, lax.rem((nc - 8) - 8 * (CMP_QB * c + qb), LANE) // 8, 0, 0))

    return pl.pallas_call(
        _cmp_kernel,
        grid=(bsz, NSA_GROUPS, nq),
        in_specs=[pl.BlockSpec((tq, NSA_HPG * LANE), lambda b, g, c: (b * nq + c, g)),
                  pl.BlockSpec((tq, LANE), lambda b, g, c: (b * nq + c, C_G3 // LANE + g)),
                  pl.BlockSpec((1, 1, nc, LANE), lambda b, g, c: (b, g, 0, 0)),
                  pl.BlockSpec((1, 1, nc, LANE), lambda b, g, c: (b, g, 0, 0)),
                  *[variant(qb) for qb in range(CMP_QB)],
                  pl.BlockSpec((nsel, nc), lambda b, g, c: (0, 0))],
        out_specs=[pl.BlockSpec((1, NSA_HPG // 2, tq, LANE), lambda b, g, c: (b, g, c, 0)),
                   pl.BlockSpec((1, 1, tq, nsel), lambda b, g, c: (b, g, c, 0))],
        out_shape=[jax.ShapeDtypeStruct((bsz, NSA_HEADS // 2, t, LANE), BF16),
                   jax.ShapeDtypeStruct((bsz, NSA_GROUPS, t, nsel), F32)],
        scratch_shapes=[pltpu.VMEM((rows, nc), F32), pltpu.VMEM((rows, nc), BF16)] * CMP_QB,
        compiler_params=_cparams(("parallel", "parallel", "arbitrary")),
        name="nsa_cmp_select",
    )(z, z, k_cmp, v_cmp, *[pc_tab] * CMP_QB, overlap)


WIN_QB = 4


def _win_kernel(q_ref, zg_ref, k_ref, v_ref, wb_ref, o_ref, *scratch):
    nvar = wb_ref.shape[1]
    gates = _sigmoid(zg_ref[...])
    blocks = []
    for qb in range(WIN_QB):
        c = pl.program_id(2) * WIN_QB + qb
        start = pl.multiple_of(jnp.maximum(c * Q_BLOCK - WINDOW, 0), Q_BLOCK)
        blocks.append((qb, jnp.minimum(c, nvar - 1), pl.ds(start, WINDOW + Q_BLOCK), scratch[2 * qb], scratch[2 * qb + 1]))
    for qb, _, keys, s_s, _ in blocks:
        s_s[...] = _dot_nt(_stack_heads(q_ref, qb * Q_BLOCK), k_ref[0, 0, keys, :])
    for qb, var, _, s_s, p_s in blocks:
        for r in range(NSA_HPG * Q_BLOCK // SEL_ROWS):
            rows = slice(r * SEL_ROWS, (r + 1) * SEL_ROWS)
            s = s_s[rows, :] + wb_ref[0, var, rows, :]
            p_s[rows, :] = jnp.exp(s - jnp.max(s, axis=-1, keepdims=True)).astype(BF16)
    for qb, _, keys, _, p_s in blocks:
        o = _dot(p_s[...], v_ref[0, 0, keys, :])
        o = o * (1.0 / o[:, HEAD_DIM:HEAD_DIM + 1])
        _store_heads(o_ref, o, gates[qb * Q_BLOCK:(qb + 1) * Q_BLOCK, :], 2, qb * Q_BLOCK)


def _win_branch(z, kw, vw, wb_tab, bsz, t):
    tq = WIN_QB * Q_BLOCK
    nq = t // tq
    rows, width = NSA_HPG * Q_BLOCK, WINDOW + Q_BLOCK
    return pl.pallas_call(
        _win_kernel,
        grid=(bsz, NSA_GROUPS, nq),
        in_specs=[pl.BlockSpec((tq, NSA_HPG * LANE), lambda b, g, c: (b * nq + c, g)),
                  pl.BlockSpec((tq, LANE), lambda b, g, c: (b * nq + c, C_G3 // LANE + g)),
                  pl.BlockSpec((1, 1, t, LANE), lambda b, g, c: (b, g, 0, 0)),
                  pl.BlockSpec((1, 1, t, LANE), lambda b, g, c: (b, g, 0, 0)),
                  pl.BlockSpec((1,) + wb_tab.shape[1:], lambda b, g, c: (g, 0, 0, 0))],
        out_specs=pl.BlockSpec((1, NSA_HPG // 2, tq, LANE), lambda b, g, c: (b, g, c, 0)),
        out_shape=jax.ShapeDtypeStruct((bsz, NSA_HEADS // 2, t, LANE), BF16),
        scratch_shapes=[pltpu.VMEM((rows, width), F32), pltpu.VMEM((rows, width), BF16)] * WIN_QB,
        compiler_params=_cparams(("parallel", "parallel", "arbitrary")),
        name="nsa_window",
    )(z, z, kw, vw, wb_tab)


SEL_TK = 512


SEL_ROWS = 64
SEL_QB = 4
SEL_NEAR = FAR_BIAS_DIST + Q_BLOCK


def _sel_kernel(q_ref, zg_ref, k_ref, v_ref, mt_ref, lt_ref, cq_ref, o_ref, m_s, acc_s, s0_s, s1_s, p0_s, p1_s,
                a0_s, a1_s):
    rows_per_block = NSA_HPG * Q_BLOCK
    c0 = pl.program_id(2) * SEL_QB
    q = q_ref[...] * (HEAD_DIM ** -0.5)
    parts = []
    for qb in range(SEL_QB):
        qrows = slice(qb * Q_BLOCK, (qb + 1) * Q_BLOCK)
        qs = jnp.concatenate([q[qrows, h * LANE:(h + 1) * LANE] for h in range(NSA_HPG)], axis=0) + cq_ref[0]
        parts.append(jnp.concatenate([qs, jnp.concatenate([mt_ref[0, 0, qrows, :]] * NSA_HPG, axis=0)], axis=1))
    q_aug = jnp.concatenate(parts, axis=0).astype(BF16)
    m_s[...] = jnp.full(m_s.shape, NEG_INF, F32)
    acc_s[...] = jnp.zeros(acc_s.shape, F32)
    ncol = SEL_TK // LANE
    n_tiles = lax.div(c0 + SEL_QB - 1, SEL_TK // Q_BLOCK) + 1

    def key_rows(i):
        return pl.ds(pl.multiple_of(jnp.minimum(i, n_tiles - 1) * SEL_TK, SEL_TK), SEL_TK)

    def scores(i, dst):
        dst[...] = _dot_nt(q_aug, k_ref[0, 0, key_rows(i), :])

    def tile(i, cur, nxt, p_buf, a_buf):
        scores(i + 1, nxt)
        for qb in range(SEL_QB):
            off = jnp.where(i < n_tiles, jnp.maximum(SEL_NEAR + SEL_TK - ((c0 + qb) * Q_BLOCK - i * SEL_TK), 0),
                            SEL_NEAR + SEL_TK + Q_BLOCK)
            off = pl.multiple_of(off, LANE)
            for r in range(rows_per_block // SEL_ROWS):
                trows = slice(r * SEL_ROWS, (r + 1) * SEL_ROWS)
                rows = slice(qb * rows_per_block + r * SEL_ROWS, qb * rows_per_block + (r + 1) * SEL_ROWS)
                s = cur[rows, :] + lt_ref[0, trows, pl.ds(off, SEL_TK)]
                cols = [s[:, j * LANE:(j + 1) * LANE] for j in range(ncol)]
                mx = functools.reduce(jnp.maximum, cols)
                m_old = m_s[rows, :]
                m_new = jnp.maximum(m_old, jnp.max(mx, axis=-1, keepdims=True))
                m_s[rows, :] = m_new
                a_buf[rows, :] = jnp.exp(m_old - m_new)
                p_buf[rows, :] = jnp.concatenate([jnp.exp(col - m_new) for col in cols], axis=1).astype(BF16)
        acc_s[...] = a_buf[...] * acc_s[...] + _dot(p_buf[...], v_ref[0, 0, key_rows(i), :])

    scores(0, s0_s)

    def pair(j, carry):
        tile(2 * j, s0_s, s1_s, p0_s, a0_s)
        tile(2 * j + 1, s1_s, s0_s, p1_s, a1_s)
        return carry

    lax.fori_loop(0, lax.div(n_tiles + 1, 2), pair, 0)
    acc = acc_s[...]
    o = acc * (1.0 / acc[:, HEAD_DIM:HEAD_DIM + 1])
    gates = _sigmoid(zg_ref[...])
    for qb in range(SEL_QB):
        _store_heads(o_ref, o[qb * rows_per_block:(qb + 1) * rows_per_block, :],
                     gates[qb * Q_BLOCK:(qb + 1) * Q_BLOCK, :], 1, qb * Q_BLOCK)


def _sel_branch(z, ks_aug, vs, mt, lt_tab, cq_tab, bsz, t):
    tq = SEL_QB * Q_BLOCK
    nq = t // tq
    nsel = mt.shape[-1]
    trows = NSA_HPG * Q_BLOCK
    rows = SEL_QB * trows
    return pl.pallas_call(
        _sel_kernel,
        grid=(bsz, NSA_GROUPS, nq),
        in_specs=[pl.BlockSpec((tq, NSA_HPG * LANE), lambda b, g, c: (b * nq + c, g)),
                  pl.BlockSpec((tq, LANE), lambda b, g, c: (b * nq + c, C_G3 // LANE + g)),
                  pl.BlockSpec((1, 1, t, 2 * LANE), lambda b, g, c: (b, g, 0, 0)),
                  pl.BlockSpec((1, 1, t, LANE), lambda b, g, c: (b, g, 0, 0)),
                  pl.BlockSpec((1, 1, tq, nsel), lambda b, g, c: (b, g, c, 0)),
                  pl.BlockSpec((1, trows, lt_tab.shape[2]), lambda b, g, c: (g, 0, 0)),
                  pl.BlockSpec((1, trows, LANE), lambda b, g, c: (g, 0, 0))],
        out_specs=pl.BlockSpec((1, NSA_HPG // 2, tq, LANE), lambda b, g, c: (b, g, c, 0)),
        out_shape=jax.ShapeDtypeStruct((bsz, NSA_HEADS // 2, t, LANE), BF16),
        scratch_shapes=[pltpu.VMEM((rows, LANE), F32),
                        pltpu.VMEM((rows, LANE), F32),
                        pltpu.VMEM((rows, SEL_TK), F32),
                        pltpu.VMEM((rows, SEL_TK), F32),
                        pltpu.VMEM((rows, SEL_TK), BF16),
                        pltpu.VMEM((rows, SEL_TK), BF16),
                        pltpu.VMEM((rows, LANE), F32),
                        pltpu.VMEM((rows, LANE), F32)],
        compiler_params=_cparams(("parallel", "parallel", "arbitrary")),
        name="nsa_selected",
    )(z, z, ks_aug, vs, mt, lt_tab, cq_tab)


def _shift_mix(z_ref, prev_ref, mu, first):
    z = z_ref[...]
    prev = jnp.where(first, 0.0, prev_ref[7:8, :])
    row = lax.broadcasted_iota(jnp.int32, z.shape, 0)
    zp = jnp.where(row == 0, prev, pltpu.roll(z, 1, 0))
    return z + mu * (zp - z)


def _rwkv_prep_kernel(has_vres, tiles_per_seq, *refs):
    (zr, zk, zv, zwa, zg, pr, pk, pv, pwa, pg, mu_ref, vec_ref, w2_ref, a2_ref, g2_ref, hsum_ref) = refs[:16]
    pos = 16
    if has_vres:
        vf_ref, v0_ref, v1_ref, v2_ref = refs[pos:pos + 4]
        pos += 4
    r_o, w_o, k_o, v_o, kk_o, b_o, g_o, bonus_o, vt_o = refs[pos:]
    first = lax.rem(pl.program_id(0), tiles_per_seq) == 0
    r = _shift_mix(zr, pr, mu_ref[0:1, :], first)
    k = _shift_mix(zk, pk, mu_ref[1:2, :], first)
    v = _shift_mix(zv, pv, mu_ref[2:3, :], first)
    wa = _shift_mix(zwa, pwa, mu_ref[3:4, 0:LANE], first)
    zg_s = _shift_mix(zg, pg, mu_ref[4:5, 0:LANE], first)
    w0, a0, k_k, k_a, r_k = (vec_ref[i:i + 1, :] for i in range(5))
    if has_vres:
        lora = _dot(_dot(v.astype(BF16), v1_ref[...]).astype(BF16), v2_ref[...])
        v = v + (vf_ref[...] - v) * _sigmoid(v0_ref[...] + lora)
    u = w0 + _dot(jnp.tanh(wa).astype(BF16), w2_ref[...])
    decay = jnp.exp(-math.exp(-0.5) * _sigmoid(u))
    a = _sigmoid(a0 + _dot(wa.astype(BF16), a2_ref[...]))
    g = _dot(_sigmoid(zg_s).astype(BF16), g2_ref[...])
    kk = k * k_k
    kk = kk / jnp.maximum(jnp.sqrt(_dot_split(kk * kk, hsum_ref[...])), 1e-12)
    k = k * (1.0 + (a - 1.0) * k_a)
    r_o[...] = r
    w_o[...] = decay
    k_o[...] = k.astype(k_o.dtype)
    v_o[...] = v
    kk_o[...] = kk
    b_o[...] = kk * a
    g_o[...] = g.astype(g_o.dtype)
    bonus_o[...] = (_dot_split(r * k * r_k, hsum_ref[...]) * v).astype(bonus_o.dtype)
    vt_o[0] = v.T.astype(BF16)


def _rwkv_prep(z, mu5, vecs, w2p, a2p, g2, hsum, vres, bsz, t, tm=256):
    n = bsz * t
    nt = n // tm
    tps = t // tm
    w = RWKV_WIDTH

    def cur(width, col):
        return pl.BlockSpec((tm, width), lambda i: (i, col // width))

    def prev(width, col):
        return pl.BlockSpec((8, width), lambda i: (jnp.maximum(i * (tm // 8) - 1, 0), col // width))

    def full(shape):
        return pl.BlockSpec(shape, lambda i: (0,) * len(shape))

    cols = [(w, C_R), (w, C_K), (w, C_V), (LANE, C_WA), (LANE, C_ZG)]
    in_specs = [cur(*c) for c in cols] + [prev(*c) for c in cols]
    in_specs += [full(mu5.shape), full(vecs.shape), full(w2p.shape), full(a2p.shape), full(g2.shape), full(hsum.shape)]
    args = [z] * 10 + [mu5, vecs, w2p, a2p, g2, hsum]
    if vres is not None:
        v_first, v0, v1p, v2p = vres
        in_specs += [pl.BlockSpec((tm, w), lambda i: (i, 0)), full(v0.shape), full(v1p.shape), full(v2p.shape)]
        args += [v_first, v0, v1p, v2p]
    tok = pl.BlockSpec((tm, w), lambda i: (i, 0))
    out_specs = [tok] * 8 + [pl.BlockSpec((1, w, tm), lambda i: (i // tps, 0, i % tps))]
    tok_dtypes = (F32, F32, BF16, F32, F32, F32, BF16, BF16)
    out_shape = [jax.ShapeDtypeStruct((n, w), dt) for dt in tok_dtypes] + [jax.ShapeDtypeStruct((bsz, w, t), BF16)]
    return pl.pallas_call(
        functools.partial(_rwkv_prep_kernel, vres is not None, tps),
        grid=(nt,),
        in_specs=in_specs,
        out_specs=out_specs,
        out_shape=out_shape,
        compiler_params=_cparams(("parallel",)),
        name="rwkv_prep",
    )(*args)


QUAD = 4
QUAD_LANES = QUAD * RWKV_HEAD
SUB_STEPS = LANE // QUAD


def _scan_kernel(r_ref, w_ref, k_ref, kk_ref, b_ref, vt_ref, ones_ref, place_ref, o_ref, s_ref, vq_ref, kq_ref, vk0_ref,
                 vk1_ref):
    bsz = r_ref.shape[0]
    tt = r_ref.shape[1]
    nh = RWKV_HEAD
    quads = [(bi, hf) for bi in range(bsz) for hf in range(RWKV_HEADS // QUAD)]

    @pl.when(pl.program_id(0) == 0)
    def _():
        s_ref[...] = jnp.zeros(s_ref.shape, F32)

    sub_shift, head_shift = int(math.log2(SUB_STEPS)), int(math.log2(nh))
    sub_lane = lax.broadcasted_iota(jnp.int32, (nh, LANE), 1) & (SUB_STEPS - 1)
    k_row_head = jnp.right_shift(lax.broadcasted_iota(jnp.int32, (LANE, QUAD_LANES), 0), sub_shift)
    k_lane_head = jnp.right_shift(lax.broadcasted_iota(jnp.int32, (LANE, QUAD_LANES), 1), head_shift)
    r_mask = (lax.broadcasted_iota(jnp.int32, (16, QUAD_LANES), 0)
              == jnp.right_shift(lax.broadcasted_iota(jnp.int32, (16, QUAD_LANES), 1), head_shift))
    ones_bd = ones_ref[...]

    def sub_block(sb, carry):
        s0 = pl.multiple_of(sb * SUB_STEPS, SUB_STEPS)
        for q, (bi, hf) in enumerate(quads):
            ln = slice(hf * QUAD_LANES, (hf + 1) * QUAD_LANES)
            vq = _dot(vt_ref[bi, (QUAD * hf) * nh:(QUAD * hf + 1) * nh, :], place_ref[sb * QUAD])
            for h in range(1, QUAD):
                vq = vq + _dot(vt_ref[bi, (QUAD * hf + h) * nh:(QUAD * hf + h + 1) * nh, :], place_ref[sb * QUAD + h])
            vq_ref[q] = vq.astype(BF16)
            k_sub = k_ref[bi, pl.ds(s0, SUB_STEPS), ln].astype(F32)
            kq_ref[q] = jnp.where(k_row_head == k_lane_head, jnp.concatenate([k_sub] * QUAD, axis=0), 0.0).astype(BF16)

        def group(tg, vk_ref, carry):
            t0 = pl.multiple_of(s0 + tg * 8, 8)
            rows = []
            for bi, hf in quads:
                ln = slice(hf * QUAD_LANES, (hf + 1) * QUAD_LANES)
                rows.append(tuple(ref[bi, pl.ds(t0, 8), ln] for ref in (w_ref, kk_ref, b_ref, r_ref)))
            onehots = [jnp.where(sub_lane == tg * 8 + u, 1.0, 0.0).astype(BF16) for u in range(8)]
            for q in range(len(quads)):
                vq = vq_ref[q]
                vk_ref[q] = _dot(jnp.concatenate([vq * oh for oh in onehots], axis=0), kq_ref[q])
            state = [s_ref[q] for q in range(len(quads))]
            for u in range(8):
                sk = _dot(jnp.concatenate([(state[q] * rows[q][1][u:u + 1, :]).astype(BF16)
                                           for q in range(len(quads))], axis=0), ones_bd)
                for q, (bi, hf) in enumerate(quads):
                    w8, _, b8, r8 = rows[q]
                    state[q] = (state[q] * w8[u:u + 1, :] - sk[q * nh:(q + 1) * nh, :] * b8[u:u + 1, :]
                                + vk_ref[q, u * nh:(u + 1) * nh, :])
                    r_lhs = jnp.where(r_mask, r8[u:u + 1, :], 0.0).astype(BF16)
                    o = _dot_nt(r_lhs, state[q].astype(BF16))
                    o_ref[t0 + u, bi, QUAD * hf:QUAD * (hf + 1), :] = o[0:QUAD, :]
            for q in range(len(quads)):
                s_ref[q] = state[q]
            return carry

        for tg in range(SUB_STEPS // 8):
            carry = group(tg, (vk0_ref, vk1_ref)[tg % 2], carry)
        return carry

    lax.fori_loop(0, tt // SUB_STEPS, sub_block, 0)


def _rwkv_scan(r, w, k, kk, b, vt, ones_bd, place, bsz, t):
    tt = SCAN_TT
    wd = RWKV_WIDTH
    nquad = bsz * RWKV_HEADS // QUAD
    tok = pl.BlockSpec((bsz, tt, wd), lambda i: (0, i, 0))
    return pl.pallas_call(
        _scan_kernel,
        grid=(t // tt,),
        in_specs=[tok, tok, tok, tok, tok,
                  pl.BlockSpec((bsz, wd, tt), lambda i: (0, 0, i)),
                  pl.BlockSpec(ones_bd.shape, lambda i: (0, 0)),
                  pl.BlockSpec(place.shape, lambda i: (0, 0, 0))],
        out_specs=pl.BlockSpec((tt, bsz, RWKV_HEADS, RWKV_HEAD), lambda i: (i, 0, 0, 0)),
        out_shape=jax.ShapeDtypeStruct((t, bsz, RWKV_HEADS, RWKV_HEAD), F32),
        scratch_shapes=[pltpu.VMEM((nquad, RWKV_HEAD, QUAD_LANES), F32),
                        pltpu.VMEM((nquad, RWKV_HEAD, LANE), BF16),
                        pltpu.VMEM((nquad, LANE, QUAD_LANES), BF16),
                        pltpu.VMEM((nquad, 8 * RWKV_HEAD, QUAD_LANES), F32),
                        pltpu.VMEM((nquad, 8 * RWKV_HEAD, QUAD_LANES), F32)],
        compiler_params=_cparams(("arbitrary",)),
        name="rwkv_scan",
    )(r, w, k, kk, b, vt, ones_bd, place)


def _merge_kernel(x_ref, oc_ref, os_ref, ow_ref, orw_ref, bonus_ref, g_ref, zgn_ref, zgr_ref, gn_ref, hsum_ref,
                  pn_ref, pr_ref, wo_ref, ln_ref, o_ref):
    o_n = jnp.concatenate([oc_ref[0, p].astype(F32) + os_ref[0, p].astype(F32) + ow_ref[0, p].astype(F32)
                           for p in range(NSA_HEADS // 2)], axis=1)
    y_n = _dot(o_n.astype(BF16), pn_ref[...])
    o = orw_ref[...]
    mu = _dot_split(o, hsum_ref[...]) * (1.0 / RWKV_HEAD)
    d = o - mu
    var = _dot_split(d * d, hsum_ref[...]) * (1.0 / RWKV_HEAD)
    o = d * lax.rsqrt(var + GN_EPS) * gn_ref[0:1, :] + gn_ref[1:2, :]
    o = ((o + bonus_ref[...]) * g_ref[...]).astype(BF16)
    y1 = _sigmoid(zgn_ref[...]) * y_n + _sigmoid(zgr_ref[...]) * _dot(o, pr_ref[...])
    y = _dot(y1.astype(BF16), wo_ref[...])
    o_ref[...] = _layer_norm(DN_ALPHA * x_ref[...] + y, ln_ref[0:1, :], ln_ref[1:2, :])


def _merge(x, o_c, o_s, o_w, o_rw, bonus, g, z, gn, hsum, p_nsa, p_rwkv, w_out, ln, bsz, t, tm=256):
    n = bsz * t
    tps = t // tm
    d = D_MODEL
    w = RWKV_WIDTH
    head = pl.BlockSpec((1, NSA_HEADS // 2, tm, LANE), lambda i: (i // tps, 0, i % tps, 0))
    tok = pl.BlockSpec((tm, w), lambda i: (i, 0))

    def full(a):
        return pl.BlockSpec(a.shape, lambda i: (0,) * a.ndim)

    return pl.pallas_call(
        _merge_kernel,
        grid=(n // tm,),
        in_specs=[pl.BlockSpec((tm, d), lambda i: (i, 0)), head, head, head, tok, tok, tok,
                  pl.BlockSpec((tm, d), lambda i: (i, C_GN // d)),
                  pl.BlockSpec((tm, d), lambda i: (i, C_GR // d)),
                  full(gn), full(hsum), full(p_nsa), full(p_rwkv), full(w_out), full(ln)],
        out_specs=pl.BlockSpec((tm, d), lambda i: (i, 0)),
        out_shape=jax.ShapeDtypeStruct((n, d), F32),
        compiler_params=_cparams(("parallel",)),
        name="mix_merge_ln",
    )(x, o_c, o_s, o_w, o_rw, bonus, g, z, z, gn, hsum, p_nsa, p_rwkv, w_out, ln)


def _xattn_kernel(x_ref, kt_ref, v_ref, wq_ref, wo_ref, ln_ref, rw_ref, o_ref, ob_ref, s_ref):
    x = x_ref[...]
    q = _dot(x.astype(BF16), wq_ref[...]).astype(BF16)
    outs = []
    for h in range(XATTN_HEADS):
        hs = slice(h * XATTN_HEAD, (h + 1) * XATTN_HEAD)
        s = _dot(q[:, hs], kt_ref[0, hs, :]) * (XATTN_HEAD ** -0.5)
        m = jnp.max(s, axis=-1, keepdims=True)
        p = jnp.exp(s - m)
        l = jnp.sum(p, axis=-1, keepdims=True)
        outs.append(_dot(p.astype(BF16), v_ref[0, :, hs]) * (1.0 / l))
    o = jnp.concatenate(outs, axis=1).astype(BF16)
    x2 = _layer_norm(DN_ALPHA * x + _dot(o, wo_ref[...]), ln_ref[0:1, :], ln_ref[1:2, :])
    o_ref[...] = x2
    hi = x2.astype(BF16)
    ob_ref[...] = hi
    lo = (x2 - hi.astype(F32)).astype(BF16)
    logits = _dot(hi, rw_ref[0]) + (_dot(lo, rw_ref[0]) + _dot(hi, rw_ref[1]))
    s_ref[...] = _sigmoid(logits)


def _xattn(x, kt, v, wq, wo, ln, rw2, bsz, t, tm=256):
    n = bsz * t
    tps = t // tm
    d = D_MODEL

    def full(a):
        return pl.BlockSpec(a.shape, lambda i: (0,) * a.ndim)

    return pl.pallas_call(
        _xattn_kernel,
        grid=(n // tm,),
        in_specs=[pl.BlockSpec((tm, d), lambda i: (i, 0)),
                  pl.BlockSpec((1,) + kt.shape[1:], lambda i: (i // tps, 0, 0)),
                  pl.BlockSpec((1,) + v.shape[1:], lambda i: (i // tps, 0, 0)),
                  full(wq), full(wo), full(ln), full(rw2)],
        out_specs=[pl.BlockSpec((tm, d), lambda i: (i, 0)), pl.BlockSpec((tm, d), lambda i: (i, 0)),
                   pl.BlockSpec((tm, LANE), lambda i: (i, 0))],
        out_shape=[jax.ShapeDtypeStruct((n, d), F32), jax.ShapeDtypeStruct((n, d), BF16),
                   jax.ShapeDtypeStruct((n, LANE), F32)],
        compiler_params=_cparams(("parallel",)),
        name="xattn_ln_router",
    )(x, kt, v, wq, wo, ln, rw2)


def _expert_kernel(meta_ref, x_ref, gate_ref, *refs):
    w_refs, o_ref, w_scr = refs[:6], refs[6], refs[7:]
    i = pl.program_id(0)
    nb = pl.num_programs(0)
    prev = jnp.maximum(i - 1, 0)

    for side in range(2):
        @pl.when((i == 0) | (meta_ref[side * nb + i] != meta_ref[side * nb + prev]))
        def _():
            for j in range(3):
                w_scr[3 * side + j][...] = w_refs[3 * side + j][0, 0].astype(BF16)

    n_used = meta_ref[2 * nb]

    @pl.when(i < n_used)
    def _():
        x = x_ref[...]
        y = jnp.zeros(o_ref.shape, F32)
        for side in range(2):
            wg, wu, wd = w_scr[3 * side:3 * side + 3]
            hg = _dot(x, wg[...])
            h = hg * _sigmoid(hg) * _dot(x, wu[...])
            y = y + gate_ref[:, side:side + 1] * _dot(h.astype(BF16), wd[...])
        o_ref[...] = y.astype(o_ref.dtype)

    @pl.when(i >= n_used)
    def _():
        o_ref[...] = jnp.zeros(o_ref.shape, o_ref.dtype)


def _experts(meta, rows, gates, w_gate, w_up, w_down, layer):
    n_rows, d = rows.shape
    ff = w_gate.shape[3]
    nb = n_rows // MOE_BLK

    def weight(shape, side):
        return pl.BlockSpec((1, 1) + shape, lambda i, m: (layer, m[side * nb + i], 0, 0))

    return pl.pallas_call(
        _expert_kernel,
        grid_spec=pltpu.PrefetchScalarGridSpec(
            num_scalar_prefetch=1,
            grid=(nb,),
            in_specs=[pl.BlockSpec((MOE_BLK, d), lambda i, m: (i, 0)),
                      pl.BlockSpec((MOE_BLK, TOP_K), lambda i, m: (i, 0))]
                     + [weight(s, side) for side in range(2) for s in ((d, ff), (d, ff), (ff, d))],
            out_specs=pl.BlockSpec((MOE_BLK, d), lambda i, m: (i, 0)),
            scratch_shapes=[pltpu.VMEM(s, BF16) for _ in range(2) for s in ((d, ff), (d, ff), (ff, d))]),
        out_shape=jax.ShapeDtypeStruct((n_rows, d), BF16),
        compiler_params=_cparams(("arbitrary",)),
        name="moe_experts",
    )(meta, rows, gates, w_gate, w_up, w_down, w_gate, w_up, w_down)


def _combine_kernel(x_ref, y_ref, ln_ref, o_ref):
    o_ref[...] = _layer_norm(DN_ALPHA * x_ref[...] + y_ref[...], ln_ref[0:1, :], ln_ref[1:2, :])


def _combine(x, y, ln, tm=512):
    n, d = x.shape
    tok = pl.BlockSpec((tm, d), lambda i: (i, 0))
    return pl.pallas_call(
        _combine_kernel,
        grid=(n // tm,),
        in_specs=[tok, tok, pl.BlockSpec(ln.shape, lambda i: (0, 0))],
        out_specs=tok,
        out_shape=jax.ShapeDtypeStruct((n, d), F32),
        compiler_params=_cparams(("parallel",)),
        name="moe_combine_ln",
    )(x, y, ln)


def _t5_bucket(dist):
    n = jnp.maximum(dist, 0)
    max_exact = REL_BUCKETS // 2
    log_ratio = jnp.log(jnp.maximum(n, 1).astype(F32) / max_exact) / math.log(REL_MAX_DIST / max_exact)
    large = jnp.minimum(max_exact + (log_ratio * (REL_BUCKETS - max_exact)).astype(jnp.int32), REL_BUCKETS - 1)
    return jnp.where(n < max_exact, n, large)


def _bias_tables(rel_table, t):
    nc = t // CMP_STRIDE
    f = rel_table[_t5_bucket(jnp.arange(t + 2048))].astype(F32).T
    ql = jnp.arange(Q_BLOCK)[:, None]

    chunks_per_q = Q_BLOCK // CMP_STRIDE
    r = np.arange(CMP_STRIDE)[:, None]
    m = np.arange(2 * nc + chunks_per_q - 1)[None, :]
    dist_c = CMP_STRIDE * (nc - 8 + chunks_per_q - 1 - m) + r - (CMP_LEN - 1)
    small = jnp.where((dist_c >= 0)[None], jnp.take(f, np.clip(dist_c, 0, f.shape[-1] - 1), axis=1), NEG_INF)
    pc = jnp.stack([small[:, :, chunks_per_q - 1 - a:chunks_per_q - 1 - a + 2 * nc] for a in range(chunks_per_q)], axis=1)
    pc = pc.reshape(NSA_GROUPS, NSA_HPG * Q_BLOCK, 2 * nc)
    pc = jnp.stack([pc[:, :, s:s + 2 * nc - LANE] for s in range(0, LANE, 8)], axis=1)
    nvar = WINDOW // Q_BLOCK + 1

    def toeplitz(base, width, hi, shift):
        period = width + Q_BLOCK
        m = np.arange(period)
        dist = base - np.where(m < width, m, m - period)
        row = jnp.take(f, np.clip(dist, 0, f.shape[-1] - 1), axis=1) - shift
        row = jnp.where(((dist >= 0) & (dist < hi))[None], row, NEG_INF)
        flat = jnp.tile(row, (1, Q_BLOCK))[:, :Q_BLOCK * (period - 1)]
        return flat.reshape(NSA_HEADS, Q_BLOCK, period - 1)[:, :, :width]

    wb = jnp.stack([toeplitz(Q_BLOCK * v, WINDOW + Q_BLOCK, WINDOW, 0.0) for v in range(nvar)], axis=1)
    wb = wb.reshape(NSA_GROUPS, NSA_HPG, nvar, Q_BLOCK, -1).transpose(0, 2, 1, 3, 4)
    wb = wb.reshape(NSA_GROUPS, nvar, NSA_HPG * Q_BLOCK, -1)
    far = rel_table[REL_BUCKETS - 1].astype(F32)
    lt = toeplitz(SEL_NEAR + SEL_TK, SEL_NEAR + 2 * SEL_TK + Q_BLOCK, f.shape[-1], far[:, None])
    lt = lt.reshape(NSA_GROUPS, NSA_HPG * Q_BLOCK, -1)
    far_hi = far.astype(BF16).astype(F32)
    lane = jnp.arange(LANE)[None, :]
    cq = jnp.where(lane == HEAD_DIM, far_hi[:, None], jnp.where(lane == HEAD_DIM + 1, (far - far_hi)[:, None], 0.0))
    cq = jnp.broadcast_to(cq[:, None, :], (NSA_HEADS, Q_BLOCK, LANE)).reshape(NSA_GROUPS, NSA_HPG * Q_BLOCK, LANE)
    return pc, wb, lt, cq


def _in_proj_perm():
    perm = np.full((IN_PAD,), -1, np.int64)
    for h in range(NSA_HEADS):
        perm[C_Q + h * LANE:C_Q + h * LANE + HEAD_DIM] = np.arange(HEAD_DIM) + h * HEAD_DIM
    rw0 = NSA_COLS
    g0 = NSA_COLS + RWKV_COLS
    perm[C_GN:C_GN + D_MODEL] = g0 + np.arange(D_MODEL)
    perm[C_GR:C_GR + D_MODEL] = g0 + D_MODEL + np.arange(D_MODEL)
    perm[C_R:C_R + RWKV_WIDTH] = rw0 + np.arange(RWKV_WIDTH)
    perm[C_K:C_K + RWKV_WIDTH] = rw0 + RWKV_WIDTH + np.arange(RWKV_WIDTH)
    perm[C_V:C_V + RWKV_WIDTH] = rw0 + 2 * RWKV_WIDTH + np.arange(RWKV_WIDTH)
    for g in range(NSA_GROUPS):
        for br in range(3):
            for h in range(NSA_HPG):
                perm[C_G3 + g * LANE + br * NSA_HPG + h] = NSA_WIDTH + 6 * NSA_KV + (g * NSA_HPG + h) * 3 + br
    perm[C_WA:C_WA + LORA_W + LORA_A] = rw0 + 3 * RWKV_WIDTH + np.arange(LORA_W + LORA_A)
    perm[C_ZG:C_ZG + LORA_G] = rw0 + 3 * RWKV_WIDTH + LORA_W + LORA_A + np.arange(LORA_G)
    return perm


_PERM = _in_proj_perm()


def _permute_in_proj(w):
    pieces = []
    start = 0
    while start < IN_PAD:
        stop = start + 1
        if _PERM[start] < 0:
            while stop < IN_PAD and _PERM[stop] < 0:
                stop += 1
            pieces.append(jnp.zeros((w.shape[0], stop - start), BF16))
        else:
            while stop < IN_PAD and _PERM[stop] == _PERM[stop - 1] + 1:
                stop += 1
            pieces.append(w[:, int(_PERM[start]):int(_PERM[stop - 1]) + 1].astype(BF16))
        start = stop
    return jnp.concatenate(pieces, axis=1)


def _pad_to(a, shape):
    return jnp.pad(a, [(0, s - d) for d, s in zip(a.shape, shape)])


_CLASS_EXPERTS = np.asarray([(grp * EXPERTS_PER_GROUP + i, grp * EXPERTS_PER_GROUP + j)
                             for grp in range(N_EXPERT_GROUPS)
                             for i in range(EXPERTS_PER_GROUP) for j in range(i + 1, EXPERTS_PER_GROUP)], np.int32)
N_PAIR_CLASSES = len(_CLASS_EXPERTS)


def _route(s, router_bias, n_tok):
    s16 = s[:, :N_EXPERTS]
    s_sel = s16 + router_bias.astype(F32)

    def top2(a):
        i1 = jnp.argmax(a, axis=-1)
        rest = jnp.where(jnp.arange(a.shape[-1]) == i1[..., None], -jnp.inf, a)
        i2 = jnp.argmax(rest, axis=-1)
        return jnp.max(a, axis=-1) + jnp.max(rest, axis=-1), jnp.stack([i1, i2], axis=-1)

    group_score, _ = top2(s_sel.reshape(n_tok, N_EXPERT_GROUPS, EXPERTS_PER_GROUP))
    group = jnp.argmax(group_score, axis=-1)
    in_group = (jnp.arange(N_EXPERTS) // EXPERTS_PER_GROUP)[None, :] == group[:, None]
    _, e_idx = top2(jnp.where(in_group, s_sel, NEG_INF))
    picked = jnp.arange(N_EXPERTS)[None, None, :] == e_idx[:, :, None]
    s_top = jnp.sum(jnp.where(picked, s16[:, None, :], 0.0), axis=-1)
    gate = s_top / jnp.sum(s_top, axis=-1, keepdims=True)
    swap = e_idx[:, 0] > e_idx[:, 1]
    e_lo = jnp.where(swap, e_idx[:, 1], e_idx[:, 0])
    e_hi = jnp.where(swap, e_idx[:, 0], e_idx[:, 1])
    gate_lohi = jnp.where(swap[:, None], gate[:, ::-1], gate)
    l_lo, l_hi = e_lo % EXPERTS_PER_GROUP, e_hi % EXPERTS_PER_GROUP
    pairs_per_group = N_PAIR_CLASSES // N_EXPERT_GROUPS
    cls = ((e_lo // EXPERTS_PER_GROUP) * pairs_per_group
           + (l_lo * (2 * EXPERTS_PER_GROUP - 1 - l_lo)) // 2 + (l_hi - l_lo - 1))
    onehot = (cls[:, None] == jnp.arange(N_PAIR_CLASSES)[None, :]).astype(jnp.int32)
    counts = jnp.sum(onehot, axis=0)
    padded = (counts + MOE_BLK - 1) // MOE_BLK * MOE_BLK
    ends = jnp.cumsum(padded)
    before = jnp.cumsum(onehot, axis=0) - onehot + (ends - padded)[None, :]
    dest = jnp.sum(onehot * before, axis=1)
    n_rows = n_tok + N_PAIR_CLASSES * MOE_BLK
    tok_of_row = jnp.zeros((n_rows,), jnp.int32).at[dest].set(jnp.arange(n_tok, dtype=jnp.int32))
    block_start = jnp.arange(n_rows // MOE_BLK) * MOE_BLK
    block_cls = jnp.minimum(jnp.sum((ends[None, :] <= block_start[:, None]).astype(jnp.int32), axis=1),
                            N_PAIR_CLASSES - 1)
    block_onehot = (block_cls[:, None] == jnp.arange(N_PAIR_CLASSES)[None, :]).astype(jnp.int32)
    block_experts = jnp.sum(block_onehot[:, :, None] * jnp.asarray(_CLASS_EXPERTS)[None], axis=1)
    meta = jnp.concatenate([block_experts[:, 0], block_experts[:, 1], ends[-1:] // MOE_BLK]).astype(jnp.int32)
    return gate_lohi, dest, tok_of_row, meta


def kernel(x, mem, rel_table, router_w, router_bias, w_in, cmp_pe_k, cmp_pe_v, cmp_w1k, cmp_w2k, cmp_w1v, cmp_w2v, rwkv_mu, rwkv_w0, rwkv_w2, rwkv_a0, rwkv_a2, rwkv_g2, rwkv_kk, rwkv_ka, rwkv_rk, rwkv_gn_g, rwkv_gn_b, rwkv_v0, rwkv_v1, rwkv_v2, p_nsa, p_rwkv, w_out, ln1_g, ln1_b, xq_w, xk_w, xv_w, xo_w, ln2_g, ln2_b, moe_w_gate, moe_w_up, moe_w_down, ln3_g, ln3_b):
    bsz, t, d = x.shape
    n = bsz * t
    depth = w_in.shape[0]
    nc = t // CMP_STRIDE
    nsel = t // SEL_BLOCK
    g, hd = NSA_GROUPS, HEAD_DIM

    pc_tab, wb_tab, lt_tab, cq_tab = _bias_tables(rel_table, t)
    cmp_start = np.arange(nc) * CMP_STRIDE
    sel_start = np.arange(nsel) * SEL_BLOCK
    overlap = ((cmp_start[:, None] <= sel_start[None, :] + SEL_BLOCK - 1)
               & (cmp_start[:, None] + CMP_LEN - 1 >= sel_start[None, :]) & (cmp_start[:, None] < (nc - 1) * CMP_STRIDE))
    overlap = jnp.asarray(overlap.T, BF16)
    lane_head = np.arange(RWKV_WIDTH) // RWKV_HEAD
    hsum = jnp.asarray(lane_head[:, None] == lane_head[None, :], BF16)
    quad_head = np.arange(QUAD_LANES) // RWKV_HEAD
    ones_bd = jnp.asarray(quad_head[:, None] == quad_head[None, :], BF16)
    step = np.arange(SCAN_TT)[None, :, None]
    lane = np.arange(LANE)[None, None, :]
    sbh = np.arange(SCAN_TT // SUB_STEPS * QUAD)[:, None, None]
    place = jnp.asarray((lane // SUB_STEPS == sbh % QUAD) & (step == (sbh // QUAD) * SUB_STEPS + lane % SUB_STEPS), BF16)
    rw_f = _pad_to(router_w.astype(F32), (d, LANE))
    rw_hi = rw_f.astype(BF16)
    rw2 = jnp.stack([rw_hi, (rw_f - rw_hi.astype(F32)).astype(BF16)])

    xf = x.reshape(n, d)
    mem_f = mem.reshape(bsz * mem.shape[1], d)
    v_first = None
    for l in range(depth):
        w_in_l = _permute_in_proj(w_in[l])
        z = _matmul(xf, w_in_l, F32, 1024, 1024)

        kv0 = NSA_WIDTH
        zeros64 = jnp.zeros((d, LANE - hd), BF16)
        w_kv = jnp.stack([jnp.concatenate(
            [piece for i in range(KV_ARRAYS)
             for piece in (w_in[l][:, kv0 + i * NSA_KV + gi * hd:kv0 + i * NSA_KV + (gi + 1) * hd].astype(BF16), zeros64)],
            axis=1) for gi in range(g)])
        kc, vc, ks_aug, vs_aug, kw_pad, vw_aug = _kv_proj(xf, w_kv, bsz, t)

        def cmp_w(pe, w1, w2):
            return (pe.reshape(2, CMP_STRIDE * hd), w1.reshape(2, CMP_STRIDE * hd, CMP_HIDDEN).astype(BF16),
                    _pad_to(w2, (CMP_HIDDEN, LANE)).astype(BF16))

        chunks = lambda a: a.reshape(bsz, g, nc, CMP_STRIDE * hd)
        k_cmp = _compress(chunks(kc), *cmp_w(cmp_pe_k[l], cmp_w1k[l], cmp_w2k[l]))
        v_cmp = _compress(chunks(vc), *cmp_w(cmp_pe_v[l], cmp_w1v[l], cmp_w2v[l]))
        o_c, sel_mask = _cmp_branch(z, k_cmp, v_cmp, pc_tab, overlap, bsz, t)
        o_s = _sel_branch(z, ks_aug, vs_aug, sel_mask, lt_tab, cq_tab, bsz, t)
        o_w = _win_branch(z, kw_pad, vw_aug, wb_tab, bsz, t)

        mu = rwkv_mu[l]
        w3 = 3 * RWKV_WIDTH
        mu5 = jnp.stack([mu[0:512], mu[512:1024], mu[1024:1536], _pad_to(mu[w3:w3 + 128], (512,)),
                         _pad_to(mu[w3 + 128:w3 + 256], (512,))])
        vecs = jnp.stack([rwkv_w0[l], rwkv_a0[l], rwkv_kk[l], rwkv_ka[l], rwkv_rk[l].reshape(-1)])
        w2p = jnp.concatenate([rwkv_w2[l], jnp.zeros_like(rwkv_a2[l])], axis=0).astype(BF16)
        a2p = jnp.concatenate([jnp.zeros_like(rwkv_w2[l]), rwkv_a2[l]], axis=0).astype(BF16)
        vres = None
        if l > 0:
            vres = (v_first, rwkv_v0[l - 1][None, :], _pad_to(rwkv_v1[l - 1], (RWKV_WIDTH, LANE)).astype(BF16),
                    _pad_to(rwkv_v2[l - 1], (LANE, RWKV_WIDTH)).astype(BF16))
        r, w, k, v, kk, b, g_out, bonus, vt = _rwkv_prep(z, mu5, vecs, w2p, a2p, rwkv_g2[l].astype(BF16), hsum, vres,
                                                          bsz, t)
        if l == 0:
            v_first = v
        seq = lambda a: a.reshape(bsz, t, RWKV_WIDTH)
        o_rw = _rwkv_scan(seq(r), seq(w), seq(k), seq(kk), seq(b), vt, ones_bd, place, bsz, t)
        o_rw = o_rw.transpose(1, 0, 2, 3).reshape(n, RWKV_WIDTH)

        xf = _merge(xf, o_c, o_s, o_w, o_rw, bonus, g_out, z, jnp.stack([rwkv_gn_g[l], rwkv_gn_b[l]]), hsum,
                    p_nsa[l].astype(BF16), p_rwkv[l].astype(BF16), w_out[l].astype(BF16),
                    jnp.stack([ln1_g[l], ln1_b[l]]), bsz, t)

        mlen = mem.shape[1]
        mk = _matmul(mem_f, xk_w[l].astype(BF16), BF16, mlen, XATTN_WIDTH)
        mv = _matmul(mem_f, xv_w[l].astype(BF16), BF16, mlen, XATTN_WIDTH)
        kt = mk.reshape(bsz, mlen, XATTN_WIDTH).transpose(0, 2, 1)
        xf, xf_b, scores = _xattn(xf, kt, mv.reshape(bsz, mlen, XATTN_WIDTH), xq_w[l].astype(BF16), xo_w[l].astype(BF16),
                            jnp.stack([ln2_g[l], ln2_b[l]]), rw2, bsz, t)

        gate, dest, tok_of_row, meta = _route(scores, router_bias, n)
        rows = jnp.take(xf_b, tok_of_row, axis=0)
        y_rows = _experts(meta, rows, jnp.take(gate, tok_of_row, axis=0), moe_w_gate, moe_w_up, moe_w_down, l)
        xf = _combine(xf, jnp.take(y_rows, dest, axis=0), jnp.stack([ln3_g[l], ln3_b[l]]))
    return xf.reshape(bsz, t, d)
```

```python
import functools
import math

import numpy as np
import jax
import jax.numpy as jnp
from jax import lax
from jax.experimental import pallas as pl
from jax.experimental.pallas import tpu as pltpu

F32 = jnp.float32
BF16 = jnp.bfloat16

D_MODEL = 1024
DEPTH = 2
NSA_HEADS = 8
NSA_GROUPS = 2
NSA_HPG = NSA_HEADS // NSA_GROUPS
HEAD_DIM = 64
NSA_WIDTH = NSA_HEADS * HEAD_DIM
NSA_KV = NSA_GROUPS * HEAD_DIM
CMP_STRIDE = 16
CMP_LEN = 2 * CMP_STRIDE
CMP_HIDDEN = 256
SEL_BLOCK = 64
SEL_TOPN = 16
WINDOW = 512
Q_BLOCK = 128
SEL_FORCE = 1e9
RWKV_HEADS = 8
RWKV_HEAD = 64
RWKV_WIDTH = RWKV_HEADS * RWKV_HEAD
LORA_W = 64
LORA_A = 64
LORA_V = 32
LORA_G = 128
GN_EPS = 64e-5
REL_BUCKETS = 32
REL_MAX_DIST = 1024
XATTN_HEADS = 4
XATTN_HEAD = 128
XATTN_WIDTH = XATTN_HEADS * XATTN_HEAD
N_EXPERTS = 16
N_EXPERT_GROUPS = 4
EXPERTS_PER_GROUP = N_EXPERTS // N_EXPERT_GROUPS
TOP_K = 2
EXPERT_FF = 512
DN_ALPHA = (2 * DEPTH) ** 0.25
LN_EPS = 1e-5
NEG_INF = -1e30
NSA_COLS = NSA_WIDTH + 6 * NSA_KV + 3 * NSA_HEADS
RWKV_COLS = 3 * RWKV_WIDTH + LORA_W + LORA_A + LORA_G

LANE = 128
VMEM_LIMIT = 56 * 1024 * 1024

C_Q = 0
C_GN = 1024
C_GR = 2048
C_R = 3072
C_K = 3584
C_V = 4096
C_G3 = 4608
C_WA = 4864
C_ZG = 4992
IN_PAD = 5120
KV_ARRAYS = 6

MOE_BLK = 256
SCAN_TT = 128
FAR_BIAS_DIST = 1280


def _cparams(sem):
    return pltpu.CompilerParams(dimension_semantics=sem, vmem_limit_bytes=VMEM_LIMIT)


def _sigmoid(x):
    return 1.0 / (1.0 + jnp.exp(-x))


def _layer_norm(v, g, b):
    mu = jnp.mean(v, axis=-1, keepdims=True)
    d = v - mu
    var = jnp.mean(d * d, axis=-1, keepdims=True)
    return d * lax.rsqrt(var + LN_EPS) * g + b


def _dot(a, b):
    return jnp.dot(a, b, preferred_element_type=F32)


def _dot_nt(a, b):
    return lax.dot_general(a, b, (((1,), (1,)), ((), ())), preferred_element_type=F32)


def _dot_split(x, w):
    hi = x.astype(BF16)
    lo = (x - hi.astype(F32)).astype(BF16)
    return _dot(hi, w) + _dot(lo, w)


def _mm_kernel(x_ref, w_ref, o_ref, xb_ref):
    @pl.when(pl.program_id(1) == 0)
    def _():
        xb_ref[...] = x_ref[...].astype(BF16)

    o_ref[...] = _dot(xb_ref[...], w_ref[...]).astype(o_ref.dtype)


def _matmul(x, w, out_dtype, tm, tn):
    n, k = x.shape
    m = w.shape[1]
    return pl.pallas_call(
        _mm_kernel,
        grid=(n // tm, m // tn),
        in_specs=[pl.BlockSpec((tm, k), lambda i, j: (i, 0)),
                  pl.BlockSpec((k, tn), lambda i, j: (0, j))],
        out_specs=pl.BlockSpec((tm, tn), lambda i, j: (i, j)),
        out_shape=jax.ShapeDtypeStruct((n, m), out_dtype),
        scratch_shapes=[pltpu.VMEM((tm, k), BF16)],
        compiler_params=_cparams(("parallel", "arbitrary")),
        name="matmul",
    )(x, w)


def _kv_proj_kernel(tiles_per_seq, x_ref, w_ref, kc_ref, vc_ref, ks_ref, vs_ref, kw_ref, vw_ref, xb_ref):
    @pl.when(pl.program_id(1) == 0)
    def _():
        xb_ref[...] = x_ref[...].astype(BF16)

    y = _dot(xb_ref[...], w_ref[pl.program_id(1)])
    tm = y.shape[0]
    slot = lambda i: y[:, i * LANE:(i + 1) * LANE]
    lane = lax.broadcasted_iota(jnp.int32, (tm, LANE), 1)
    one_lane = jnp.where(lane == HEAD_DIM, 1.0, 0.0)
    t0 = lax.rem(pl.program_id(0), tiles_per_seq) * tm
    block = jnp.right_shift(t0 + lax.broadcasted_iota(jnp.int32, (tm, LANE), 0), int(math.log2(SEL_BLOCK)))
    kc_ref[0, 0] = slot(0)[:, :HEAD_DIM]
    vc_ref[0, 0] = slot(1)[:, :HEAD_DIM]
    ks_ref[0, 0] = jnp.concatenate([slot(2) + jnp.where(lane == HEAD_DIM + 1, 1.0, one_lane),
                                    jnp.where(lane == block, 1.0, 0.0)], axis=1).astype(BF16)
    vs_ref[0, 0] = (slot(3) + one_lane).astype(BF16)
    kw_ref[0, 0] = slot(4).astype(BF16)
    vw_ref[0, 0] = (slot(5) + one_lane).astype(BF16)


def _kv_proj(x, w_kv, bsz, t, tm=512):
    n, d = x.shape
    tps = t // tm
    g = NSA_GROUPS

    def out(width, dtype):
        return (pl.BlockSpec((1, 1, tm, width), lambda i, j: (i // tps, j, i % tps, 0)),
                jax.ShapeDtypeStruct((bsz, g, t, width), dtype))

    outs = [out(HEAD_DIM, F32), out(HEAD_DIM, F32), out(2 * LANE, BF16), out(LANE, BF16), out(LANE, BF16),
            out(LANE, BF16)]
    return pl.pallas_call(
        functools.partial(_kv_proj_kernel, tps),
        grid=(n // tm, g),
        in_specs=[pl.BlockSpec((tm, d), lambda i, j: (i, 0)),
                  pl.BlockSpec((g, d, KV_ARRAYS * LANE), lambda i, j: (0, 0, 0))],
        out_specs=[o[0] for o in outs],
        out_shape=[o[1] for o in outs],
        scratch_shapes=[pltpu.VMEM((tm, d), BF16)],
        compiler_params=_cparams(("parallel", "arbitrary")),
        name="nsa_kv_proj",
    )(x, w_kv)


def _compress_kernel(u_ref, pe_ref, w1_ref, w2_ref, o_ref):
    u = u_ref[0, 0]
    a = _dot((u + pe_ref[0:1, :]).astype(BF16), w1_ref[0])
    b = _dot((u + pe_ref[1:2, :]).astype(BF16), w1_ref[1])
    nc = u.shape[0]
    h = a + pltpu.roll(b, nc - 1, 0)
    h = 0.5 * h * (1.0 + jnp.tanh(math.sqrt(2.0 / math.pi) * (h + 0.044715 * (h * h * h))))
    y = _dot(h.astype(BF16), w2_ref[...])
    row = lax.broadcasted_iota(jnp.int32, y.shape, 0)
    o_ref[0, 0] = jnp.where(row < nc - 1, y, 0.0)


def _compress(u, pe2, w1, w2p):
    b, g, nc, _ = u.shape
    return pl.pallas_call(
        _compress_kernel,
        grid=(b, g),
        in_specs=[pl.BlockSpec((1, 1, nc, 1024), lambda i, j: (i, j, 0, 0)),
                  pl.BlockSpec((2, 1024), lambda i, j: (0, 0)),
                  pl.BlockSpec((2, 1024, CMP_HIDDEN), lambda i, j: (0, 0, 0)),
                  pl.BlockSpec((CMP_HIDDEN, LANE), lambda i, j: (0, 0))],
        out_specs=pl.BlockSpec((1, 1, nc, LANE), lambda i, j: (i, j, 0, 0)),
        out_shape=jax.ShapeDtypeStruct((b, g, nc, LANE), F32),
        compiler_params=_cparams(("parallel", "parallel")),
        name="nsa_compress",
    )(u, pe2, w1, w2p)


def _stack_heads(q_ref, row0=0):
    q = q_ref[row0:row0 + Q_BLOCK, :] * (HEAD_DIM ** -0.5)
    return jnp.concatenate([q[:, h * LANE:(h + 1) * LANE] for h in range(NSA_HPG)], axis=0).astype(BF16)


def _store_heads(o_ref, o, gates, branch, row0=0):
    lane = lax.broadcasted_iota(jnp.int32, (Q_BLOCK, LANE), 1)
    for pair in range(NSA_HPG // 2):
        halves = []
        for h in (2 * pair, 2 * pair + 1):
            c = branch * NSA_HPG + h
            halves.append(o[h * Q_BLOCK:(h + 1) * Q_BLOCK, :] * gates[:, c:c + 1])
        o_ref[0, pair, row0:row0 + Q_BLOCK, :] = jnp.where(lane < HEAD_DIM, halves[0],
                                                           pltpu.roll(halves[1], HEAD_DIM, 1)).astype(o_ref.dtype)


CMP_QB = 4


def _cmp_kernel(q_ref, zg_ref, kc_ref, vc_ref, *refs):
    pc_refs, (ov_ref, o_ref, m_ref), scratch = refs[:CMP_QB], refs[CMP_QB:CMP_QB + 3], refs[CMP_QB + 3:]
    nc = kc_ref.shape[2]
    nsel = ov_ref.shape[0]
    kc = kc_ref[0, 0].astype(BF16)
    vc = vc_ref[0, 0].astype(BF16)
    gates = _sigmoid(zg_ref[...])
    blocks = [(qb, pl.program_id(2) * CMP_QB + qb, pc_ref, scratch[2 * qb], scratch[2 * qb + 1])
              for qb, pc_ref in enumerate(pc_refs)]
    for qb, c, _, p_s, _ in blocks:
        p_s[...] = _dot_nt(_stack_heads(q_ref, qb * Q_BLOCK), kc)
    for qb, c, pc_ref, p_s, pb_s in blocks:
        off = pl.multiple_of(lax.div((nc - 8) - 8 * c, LANE) * LANE, LANE)
        for r in range(NSA_HPG * Q_BLOCK // SEL_ROWS):
            rows = slice(r * SEL_ROWS, (r + 1) * SEL_ROWS)
            s = p_s[rows, :] + pc_ref[0, 0, rows, pl.ds(off, nc)]
            e = jnp.exp(s - jnp.max(s, axis=-1, keepdims=True))
            l = jnp.sum(e, axis=-1, keepdims=True)
            ql = (lax.broadcasted_iota(jnp.int32, (SEL_ROWS, 1), 0) + r * SEL_ROWS) & (Q_BLOCK - 1)
            p = e * jnp.where((c * Q_BLOCK + ql) >= (CMP_LEN - 1), 1.0 / l, 0.0)
            p_s[rows, :] = p
            pb_s[rows, :] = p.astype(BF16)
    vals = []
    jb = lax.broadcasted_iota(jnp.int32, (nsel, Q_BLOCK), 0)
    jbf = jb.astype(F32)
    for qb, c, _, p_s, pb_s in blocks:
        _store_heads(o_ref, _dot(pb_s[...], vc), gates[qb * Q_BLOCK:(qb + 1) * Q_BLOCK, :], 0, qb * Q_BLOCK)
        psum = ((p_s[0:Q_BLOCK, :] + p_s[Q_BLOCK:2 * Q_BLOCK, :])
                + (p_s[2 * Q_BLOCK:3 * Q_BLOCK, :] + p_s[3 * Q_BLOCK:4 * Q_BLOCK, :]))
        hi = psum.astype(BF16)
        lo = (psum - hi.astype(F32)).astype(BF16)
        imp = _dot_nt(ov_ref[...], hi) + _dot_nt(ov_ref[...], lo)
        tq = c * Q_BLOCK + lax.broadcasted_iota(jnp.int32, (nsel, Q_BLOCK), 1)
        jq = jnp.right_shift(tq, int(math.log2(SEL_BLOCK)))
        forced = (jb == 0) | (jb == jq) | (jb == jq - 1)
        vals.append(jnp.where(forced, SEL_FORCE, jnp.where(jb <= jq, imp, -SEL_FORCE)))
    sels = [jnp.zeros((nsel, Q_BLOCK), F32) for _ in blocks]
    for _ in range(SEL_TOPN):
        for qb in range(CMP_QB):
            mx = jnp.max(vals[qb], axis=0, keepdims=True)
            first = jnp.min(jnp.where(vals[qb] == mx, jbf, float(nsel)), axis=0, keepdims=True)
            hit = jbf == first
            sels[qb] = jnp.where(hit, jnp.where(mx >= 0.0, 1.0, 0.0), sels[qb])
            vals[qb] = jnp.where(hit, -3e38, vals[qb])
    for qb in range(CMP_QB):
        m_ref[0, 0, qb * Q_BLOCK:(qb + 1) * Q_BLOCK, :] = (sels[qb].T - 1.0) * (-NEG_INF)


def _cmp_branch(z, k_cmp, v_cmp, pc_tab, overlap, bsz, t):
    tq = CMP_QB * Q_BLOCK
    nq = t // tq
    nc = k_cmp.shape[2]
    nsel = overlap.shape[0]
    rows = NSA_HPG * Q_BLOCK

    def variant(qb):
        return pl.BlockSpec((1, 1, rows, 2 * nc - LANE),
                            lambda b, g, c: (g, lax.rem((nc - 8) - 8 * (CMP_QB * c + qb), LANE) // 8, 0, 0))

    return pl.pallas_call(
        _cmp_kernel,
        grid=(bsz, NSA_GROUPS, nq),
        in_specs=[pl.BlockSpec((tq, NSA_HPG * LANE), lambda b, g, c: (b * nq + c, g)),
                  pl.BlockSpec((tq, LANE), lambda b, g, c: (b * nq + c, C_G3 // LANE + g)),
                  pl.BlockSpec((1, 1, nc, LANE), lambda b, g, c: (b, g, 0, 0)),
                  pl.BlockSpec((1, 1, nc, LANE), lambda b, g, c: (b, g, 0, 0)),
                  *[variant(qb) for qb in range(CMP_QB)],
                  pl.BlockSpec((nsel, nc), lambda b, g, c: (0, 0))],
        out_specs=[pl.BlockSpec((1, NSA_HPG // 2, tq, LANE), lambda b, g, c: (b, g, c, 0)),
                   pl.BlockSpec((1, 1, tq, nsel), lambda b, g, c: (b, g, c, 0))],
        out_shape=[jax.ShapeDtypeStruct((bsz, NSA_HEADS // 2, t, LANE), BF16),
                   jax.ShapeDtypeStruct((bsz, NSA_GROUPS, t, nsel), F32)],
        scratch_shapes=[pltpu.VMEM((rows, nc), F32), pltpu.VMEM((rows, nc), BF16)] * CMP_QB,
        compiler_params=_cparams(("parallel", "parallel", "arbitrary")),
        name="nsa_cmp_select",
    )(z, z, k_cmp, v_cmp, *[pc_tab] * CMP_QB, overlap)


WIN_QB = 4


def _win_kernel(q_ref, zg_ref, k_ref, v_ref, wb_ref, o_ref, *scratch):
    nvar = wb_ref.shape[1]
    gates = _sigmoid(zg_ref[...])
    blocks = []
    for qb in range(WIN_QB):
        c = pl.program_id(2) * WIN_QB + qb
        start = pl.multiple_of(jnp.maximum(c * Q_BLOCK - WINDOW, 0), Q_BLOCK)
        blocks.append((qb, jnp.minimum(c, nvar - 1), pl.ds(start, WINDOW + Q_BLOCK), scratch[2 * qb], scratch[2 * qb + 1]))
    for qb, _, keys, s_s, _ in blocks:
        s_s[...] = _dot_nt(_stack_heads(q_ref, qb * Q_BLOCK), k_ref[0, 0, keys, :])
    for qb, var, _, s_s, p_s in blocks:
        for r in range(NSA_HPG * Q_BLOCK // SEL_ROWS):
            rows = slice(r * SEL_ROWS, (r + 1) * SEL_ROWS)
            s = s_s[rows, :] + wb_ref[0, var, rows, :]
            p_s[rows, :] = jnp.exp(s - jnp.max(s, axis=-1, keepdims=True)).astype(BF16)
    for qb, _, keys, _, p_s in blocks:
        o = _dot(p_s[...], v_ref[0, 0, keys, :])
        o = o * (1.0 / o[:, HEAD_DIM:HEAD_DIM + 1])
        _store_heads(o_ref, o, gates[qb * Q_BLOCK:(qb + 1) * Q_BLOCK, :], 2, qb * Q_BLOCK)


def _win_branch(z, kw, vw, wb_tab, bsz, t):
    tq = WIN_QB * Q_BLOCK
    nq = t // tq
    rows, width = NSA_HPG * Q_BLOCK, WINDOW + Q_BLOCK
    return pl.pallas_call(
        _win_kernel,
        grid=(bsz, NSA_GROUPS, nq),
        in_specs=[pl.BlockSpec((tq, NSA_HPG * LANE), lambda b, g, c: (b * nq + c, g)),
                  pl.BlockSpec((tq, LANE), lambda b, g, c: (b * nq + c, C_G3 // LANE + g)),
                  pl.BlockSpec((1, 1, t, LANE), lambda b, g, c: (b, g, 0, 0)),
                  pl.BlockSpec((1, 1, t, LANE), lambda b, g, c: (b, g, 0, 0)),
                  pl.BlockSpec((1,) + wb_tab.shape[1:], lambda b, g, c: (g, 0, 0, 0))],
        out_specs=pl.BlockSpec((1, NSA_HPG // 2, tq, LANE), lambda b, g, c: (b, g, c, 0)),
        out_shape=jax.ShapeDtypeStruct((bsz, NSA_HEADS // 2, t, LANE), BF16),
        scratch_shapes=[pltpu.VMEM((rows, width), F32), pltpu.VMEM((rows, width), BF16)] * WIN_QB,
        compiler_params=_cparams(("parallel", "parallel", "arbitrary")),
        name="nsa_window",
    )(z, z, kw, vw, wb_tab)


SEL_TK = 512


SEL_ROWS = 64
SEL_QB = 4
SEL_NEAR = FAR_BIAS_DIST + Q_BLOCK


def _sel_kernel(q_ref, zg_ref, k_ref, v_ref, mt_ref, lt_ref, cq_ref, o_ref, m_s, acc_s, s0_s, s1_s, p0_s, p1_s,
                a0_s, a1_s):
    rows_per_block = NSA_HPG * Q_BLOCK
    c0 = pl.program_id(2) * SEL_QB
    q = q_ref[...] * (HEAD_DIM ** -0.5)
    parts = []
    for qb in range(SEL_QB):
        qrows = slice(qb * Q_BLOCK, (qb + 1) * Q_BLOCK)
        qs = jnp.concatenate([q[qrows, h * LANE:(h + 1) * LANE] for h in range(NSA_HPG)], axis=0) + cq_ref[0]
        parts.append(jnp.concatenate([qs, jnp.concatenate([mt_ref[0, 0, qrows, :]] * NSA_HPG, axis=0)], axis=1))
    q_aug = jnp.concatenate(parts, axis=0).astype(BF16)
    m_s[...] = jnp.full(m_s.shape, NEG_INF, F32)
    acc_s[...] = jnp.zeros(acc_s.shape, F32)
    ncol = SEL_TK // LANE
    n_tiles = lax.div(c0 + SEL_QB - 1, SEL_TK // Q_BLOCK) + 1

    def key_rows(i):
        return pl.ds(pl.multiple_of(jnp.minimum(i, n_tiles - 1) * SEL_TK, SEL_TK), SEL_TK)

    def scores(i, dst):
        dst[...] = _dot_nt(q_aug, k_ref[0, 0, key_rows(i), :])

    def tile(i, cur, nxt, p_buf, a_buf):
        scores(i + 1, nxt)
        for qb in range(SEL_QB):
            off = jnp.where(i < n_tiles, jnp.maximum(SEL_NEAR + SEL_TK - ((c0 + qb) * Q_BLOCK - i * SEL_TK), 0),
                            SEL_NEAR + SEL_TK + Q_BLOCK)
            off = pl.multiple_of(off, LANE)
            for r in range(rows_per_block // SEL_ROWS):
                trows = slice(r * SEL_ROWS, (r + 1) * SEL_ROWS)
                rows = slice(qb * rows_per_block + r * SEL_ROWS, qb * rows_per_block + (r + 1) * SEL_ROWS)
                s = cur[rows, :] + lt_ref[0, trows, pl.ds(off, SEL_TK)]
                cols = [s[:, j * LANE:(j + 1) * LANE] for j in range(ncol)]
                mx = functools.reduce(jnp.maximum, cols)
                m_old = m_s[rows, :]
                m_new = jnp.maximum(m_old, jnp.max(mx, axis=-1, keepdims=True))
                m_s[rows, :] = m_new
                a_buf[rows, :] = jnp.exp(m_old - m_new)
                p_buf[rows, :] = jnp.concatenate([jnp.exp(col - m_new) for col in cols], axis=1).astype(BF16)
        acc_s[...] = a_buf[...] * acc_s[...] + _dot(p_buf[...], v_ref[0, 0, key_rows(i), :])

    scores(0, s0_s)

    def pair(j, carry):
        tile(2 * j, s0_s, s1_s, p0_s, a0_s)
        tile(2 * j + 1, s1_s, s0_s, p1_s, a1_s)
        return carry

    lax.fori_loop(0, lax.div(n_tiles + 1, 2), pair, 0)
    acc = acc_s[...]
    o = acc * (1.0 / acc[:, HEAD_DIM:HEAD_DIM + 1])
    gates = _sigmoid(zg_ref[...])
    for qb in range(SEL_QB):
        _store_heads(o_ref, o[qb * rows_per_block:(qb + 1) * rows_per_block, :],
                     gates[qb * Q_BLOCK:(qb + 1) * Q_BLOCK, :], 1, qb * Q_BLOCK)


def _sel_branch(z, ks_aug, vs, mt, lt_tab, cq_tab, bsz, t):
    tq = SEL_QB * Q_BLOCK
    nq = t // tq
    nsel = mt.shape[-1]
    trows = NSA_HPG * Q_BLOCK
    rows = SEL_QB * trows
    return pl.pallas_call(
        _sel_kernel,
        grid=(bsz, NSA_GROUPS, nq),
        in_specs=[pl.BlockSpec((tq, NSA_HPG * LANE), lambda b, g, c: (b * nq + c, g)),
                  pl.BlockSpec((tq, LANE), lambda b, g, c: (b * nq + c, C_G3 // LANE + g)),
                  pl.BlockSpec((1, 1, t, 2 * LANE), lambda b, g, c: (b, g, 0, 0)),
                  pl.BlockSpec((1, 1, t, LANE), lambda b, g, c: (b, g, 0, 0)),
                  pl.BlockSpec((1, 1, tq, nsel), lambda b, g, c: (b, g, c, 0)),
                  pl.BlockSpec((1, trows, lt_tab.shape[2]), lambda b, g, c: (g, 0, 0)),
                  pl.BlockSpec((1, trows, LANE), lambda b, g, c: (g, 0, 0))],
        out_specs=pl.BlockSpec((1, NSA_HPG // 2, tq, LANE), lambda b, g, c: (b, g, c, 0)),
        out_shape=jax.ShapeDtypeStruct((bsz, NSA_HEADS // 2, t, LANE), BF16),
        scratch_shapes=[pltpu.VMEM((rows, LANE), F32),
                        pltpu.VMEM((rows, LANE), F32),
                        pltpu.VMEM((rows, SEL_TK), F32),
                        pltpu.VMEM((rows, SEL_TK), F32),
                        pltpu.VMEM((rows, SEL_TK), BF16),
                        pltpu.VMEM((rows, SEL_TK), BF16),
                        pltpu.VMEM((rows, LANE), F32),
                        pltpu.VMEM((rows, LANE), F32)],
        compiler_params=_cparams(("parallel", "parallel", "arbitrary")),
        name="nsa_selected",
    )(z, z, ks_aug, vs, mt, lt_tab, cq_tab)


def _shift_mix(z_ref, prev_ref, mu, first):
    z = z_ref[...]
    prev = jnp.where(first, 0.0, prev_ref[7:8, :])
    row = lax.broadcasted_iota(jnp.int32, z.shape, 0)
    zp = jnp.where(row == 0, prev, pltpu.roll(z, 1, 0))
    return z + mu * (zp - z)


def _rwkv_prep_kernel(has_vres, tiles_per_seq, *refs):
    (zr, zk, zv, zwa, zg, pr, pk, pv, pwa, pg, mu_ref, vec_ref, w2_ref, a2_ref, g2_ref, hsum_ref) = refs[:16]
    pos = 16
    if has_vres:
        vf_ref, v0_ref, v1_ref, v2_ref = refs[pos:pos + 4]
        pos += 4
    r_o, w_o, k_o, v_o, kk_o, b_o, g_o, bonus_o, vt_o = refs[pos:]
    first = lax.rem(pl.program_id(0), tiles_per_seq) == 0
    r = _shift_mix(zr, pr, mu_ref[0:1, :], first)
    k = _shift_mix(zk, pk, mu_ref[1:2, :], first)
    v = _shift_mix(zv, pv, mu_ref[2:3, :], first)
    wa = _shift_mix(zwa, pwa, mu_ref[3:4, 0:LANE], first)
    zg_s = _shift_mix(zg, pg, mu_ref[4:5, 0:LANE], first)
    w0, a0, k_k, k_a, r_k = (vec_ref[i:i + 1, :] for i in range(5))
    if has_vres:
        lora = _dot(_dot(v.astype(BF16), v1_ref[...]).astype(BF16), v2_ref[...])
        v = v + (vf_ref[...] - v) * _sigmoid(v0_ref[...] + lora)
    u = w0 + _dot(jnp.tanh(wa).astype(BF16), w2_ref[...])
    decay = jnp.exp(-math.exp(-0.5) * _sigmoid(u))
    a = _sigmoid(a0 + _dot(wa.astype(BF16), a2_ref[...]))
    g = _dot(_sigmoid(zg_s).astype(BF16), g2_ref[...])
    kk = k * k_k
    kk = kk / jnp.maximum(jnp.sqrt(_dot_split(kk * kk, hsum_ref[...])), 1e-12)
    k = k * (1.0 + (a - 1.0) * k_a)
    r_o[...] = r
    w_o[...] = decay
    k_o[...] = k.astype(k_o.dtype)
    v_o[...] = v
    kk_o[...] = kk
    b_o[...] = kk * a
    g_o[...] = g.astype(g_o.dtype)
    bonus_o[...] = (_dot_split(r * k * r_k, hsum_ref[...]) * v).astype(bonus_o.dtype)
    vt_o[0] = v.T.astype(BF16)


def _rwkv_prep(z, mu5, vecs, w2p, a2p, g2, hsum, vres, bsz, t, tm=512):
    n = bsz * t
    nt = n // tm
    tps = t // tm
    w = RWKV_WIDTH

    def cur(width, col):
        return pl.BlockSpec((tm, width), lambda i: (i, col // width))

    def prev(width, col):
        return pl.BlockSpec((8, width), lambda i: (jnp.maximum(i * (tm // 8) - 1, 0), col // width))

    def full(shape):
        return pl.BlockSpec(shape, lambda i: (0,) * len(shape))

    cols = [(w, C_R), (w, C_K), (w, C_V), (LANE, C_WA), (LANE, C_ZG)]
    in_specs = [cur(*c) for c in cols] + [prev(*c) for c in cols]
    in_specs += [full(mu5.shape), full(vecs.shape), full(w2p.shape), full(a2p.shape), full(g2.shape), full(hsum.shape)]
    args = [z] * 10 + [mu5, vecs, w2p, a2p, g2, hsum]
    if vres is not None:
        v_first, v0, v1p, v2p = vres
        in_specs += [pl.BlockSpec((tm, w), lambda i: (i, 0)), full(v0.shape), full(v1p.shape), full(v2p.shape)]
        args += [v_first, v0, v1p, v2p]
    tok = pl.BlockSpec((tm, w), lambda i: (i, 0))
    out_specs = [tok] * 8 + [pl.BlockSpec((1, w, tm), lambda i: (i // tps, 0, i % tps))]
    tok_dtypes = (F32, F32, BF16, F32, F32, F32, BF16, BF16)
    out_shape = [jax.ShapeDtypeStruct((n, w), dt) for dt in tok_dtypes] + [jax.ShapeDtypeStruct((bsz, w, t), BF16)]
    return pl.pallas_call(
        functools.partial(_rwkv_prep_kernel, vres is not None, tps),
        grid=(nt,),
        in_specs=in_specs,
        out_specs=out_specs,
        out_shape=out_shape,
        compiler_params=_cparams(("parallel",)),
        name="rwkv_prep",
    )(*args)


QUAD = 4
QUAD_LANES = QUAD * RWKV_HEAD
SUB_STEPS = LANE // QUAD


def _scan_kernel(r_ref, w_ref, k_ref, kk_ref, b_ref, vt_ref, ones_ref, place_ref, o_ref, s_ref, vq_ref, kq_ref, vk0_ref,
                 vk1_ref):
    bsz = r_ref.shape[0]
    tt = r_ref.shape[1]
    nh = RWKV_HEAD
    quads = [(bi, hf) for bi in range(bsz) for hf in range(RWKV_HEADS // QUAD)]

    @pl.when(pl.program_id(0) == 0)
    def _():
        s_ref[...] = jnp.zeros(s_ref.shape, F32)

    sub_shift, head_shift = int(math.log2(SUB_STEPS)), int(math.log2(nh))
    sub_lane = lax.broadcasted_iota(jnp.int32, (nh, LANE), 1) & (SUB_STEPS - 1)
    k_row_head = jnp.right_shift(lax.broadcasted_iota(jnp.int32, (LANE, QUAD_LANES), 0), sub_shift)
    k_lane_head = jnp.right_shift(lax.broadcasted_iota(jnp.int32, (LANE, QUAD_LANES), 1), head_shift)
    r_mask = (lax.broadcasted_iota(jnp.int32, (16, QUAD_LANES), 0)
              == jnp.right_shift(lax.broadcasted_iota(jnp.int32, (16, QUAD_LANES), 1), head_shift))
    ones_bd = ones_ref[...]

    def sub_block(sb, carry):
        s0 = pl.multiple_of(sb * SUB_STEPS, SUB_STEPS)
        for q, (bi, hf) in enumerate(quads):
            ln = slice(hf * QUAD_LANES, (hf + 1) * QUAD_LANES)
            vq = _dot(vt_ref[bi, (QUAD * hf) * nh:(QUAD * hf + 1) * nh, :], place_ref[sb * QUAD])
            for h in range(1, QUAD):
                vq = vq + _dot(vt_ref[bi, (QUAD * hf + h) * nh:(QUAD * hf + h + 1) * nh, :], place_ref[sb * QUAD + h])
            vq_ref[q] = vq.astype(BF16)
            k_sub = k_ref[bi, pl.ds(s0, SUB_STEPS), ln].astype(F32)
            kq_ref[q] = jnp.where(k_row_head == k_lane_head, jnp.concatenate([k_sub] * QUAD, axis=0), 0.0).astype(BF16)

        def group(tg, vk_ref, carry):
            t0 = pl.multiple_of(s0 + tg * 8, 8)
            rows = []
            for bi, hf in quads:
                ln = slice(hf * QUAD_LANES, (hf + 1) * QUAD_LANES)
                rows.append(tuple(ref[bi, pl.ds(t0, 8), ln] for ref in (w_ref, kk_ref, b_ref, r_ref)))
            onehots = [jnp.where(sub_lane == tg * 8 + u, 1.0, 0.0).astype(BF16) for u in range(8)]
            for q in range(len(quads)):
                vq = vq_ref[q]
                vk_ref[q] = _dot(jnp.concatenate([vq * oh for oh in onehots], axis=0), kq_ref[q])
            state = [s_ref[q] for q in range(len(quads))]
            for u in range(8):
                sk = _dot(jnp.concatenate([(state[q] * rows[q][1][u:u + 1, :]).astype(BF16)
                                           for q in range(len(quads))], axis=0), ones_bd)
                for q, (bi, hf) in enumerate(quads):
                    w8, _, b8, r8 = rows[q]
                    state[q] = (state[q] * w8[u:u + 1, :] - sk[q * nh:(q + 1) * nh, :] * b8[u:u + 1, :]
                                + vk_ref[q, u * nh:(u + 1) * nh, :])
                    r_lhs = jnp.where(r_mask, r8[u:u + 1, :], 0.0).astype(BF16)
                    o = _dot_nt(r_lhs, state[q].astype(BF16))
                    o_ref[t0 + u, bi, QUAD * hf:QUAD * (hf + 1), :] = o[0:QUAD, :]
            for q in range(len(quads)):
                s_ref[q] = state[q]
            return carry

        for tg in range(SUB_STEPS // 8):
            carry = group(tg, (vk0_ref, vk1_ref)[tg % 2], carry)
        return carry

    lax.fori_loop(0, tt // SUB_STEPS, sub_block, 0)


def _rwkv_scan(r, w, k, kk, b, vt, ones_bd, place, bsz, t):
    tt = SCAN_TT
    wd = RWKV_WIDTH
    nquad = bsz * RWKV_HEADS // QUAD
    tok = pl.BlockSpec((bsz, tt, wd), lambda i: (0, i, 0))
    return pl.pallas_call(
        _scan_kernel,
        grid=(t // tt,),
        in_specs=[tok, tok, tok, tok, tok,
                  pl.BlockSpec((bsz, wd, tt), lambda i: (0, 0, i)),
                  pl.BlockSpec(ones_bd.shape, lambda i: (0, 0)),
                  pl.BlockSpec(place.shape, lambda i: (0, 0, 0))],
        out_specs=pl.BlockSpec((tt, bsz, RWKV_HEADS, RWKV_HEAD), lambda i: (i, 0, 0, 0)),
        out_shape=jax.ShapeDtypeStruct((t, bsz, RWKV_HEADS, RWKV_HEAD), F32),
        scratch_shapes=[pltpu.VMEM((nquad, RWKV_HEAD, QUAD_LANES), F32),
                        pltpu.VMEM((nquad, RWKV_HEAD, LANE), BF16),
                        pltpu.VMEM((nquad, LANE, QUAD_LANES), BF16),
                        pltpu.VMEM((nquad, 8 * RWKV_HEAD, QUAD_LANES), F32),
                        pltpu.VMEM((nquad, 8 * RWKV_HEAD, QUAD_LANES), F32)],
        compiler_params=_cparams(("arbitrary",)),
        name="rwkv_scan",
    )(r, w, k, kk, b, vt, ones_bd, place)


def _merge_kernel(x_ref, oc_ref, os_ref, ow_ref, orw_ref, bonus_ref, g_ref, zgn_ref, zgr_ref, gn_ref, hsum_ref,
                  pn_ref, pr_ref, wo_ref, ln_ref, o_ref):
    o_n = jnp.concatenate([oc_ref[0, p].astype(F32) + os_ref[0, p].astype(F32) + ow_ref[0, p].astype(F32)
                           for p in range(NSA_HEADS // 2)], axis=1)
    y_n = _dot(o_n.astype(BF16), pn_ref[...])
    o = orw_ref[...]
    mu = _dot_split(o, hsum_ref[...]) * (1.0 / RWKV_HEAD)
    d = o - mu
    var = _dot_split(d * d, hsum_ref[...]) * (1.0 / RWKV_HEAD)
    o = d * lax.rsqrt(var + GN_EPS) * gn_ref[0:1, :] + gn_ref[1:2, :]
    o = ((o + bonus_ref[...]) * g_ref[...]).astype(BF16)
    y1 = _sigmoid(zgn_ref[...]) * y_n + _sigmoid(zgr_ref[...]) * _dot(o, pr_ref[...])
    y = _dot(y1.astype(BF16), wo_ref[...])
    o_ref[...] = _layer_norm(DN_ALPHA * x_ref[...] + y, ln_ref[0:1, :], ln_ref[1:2, :])


def _merge(x, o_c, o_s, o_w, o_rw, bonus, g, z, gn, hsum, p_nsa, p_rwkv, w_out, ln, bsz, t, tm=512):
    n = bsz * t
    tps = t // tm
    d = D_MODEL
    w = RWKV_WIDTH
    head = pl.BlockSpec((1, NSA_HEADS // 2, tm, LANE), lambda i: (i // tps, 0, i % tps, 0))
    tok = pl.BlockSpec((tm, w), lambda i: (i, 0))

    def full(a):
        return pl.BlockSpec(a.shape, lambda i: (0,) * a.ndim)

    return pl.pallas_call(
        _merge_kernel,
        grid=(n // tm,),
        in_specs=[pl.BlockSpec((tm, d), lambda i: (i, 0)), head, head, head, tok, tok, tok,
                  pl.BlockSpec((tm, d), lambda i: (i, C_GN // d)),
                  pl.BlockSpec((tm, d), lambda i: (i, C_GR // d)),
                  full(gn), full(hsum), full(p_nsa), full(p_rwkv), full(w_out), full(ln)],
        out_specs=pl.BlockSpec((tm, d), lambda i: (i, 0)),
        out_shape=jax.ShapeDtypeStruct((n, d), F32),
        compiler_params=_cparams(("parallel",)),
        name="mix_merge_ln",
    )(x, o_c, o_s, o_w, o_rw, bonus, g, z, z, gn, hsum, p_nsa, p_rwkv, w_out, ln)


def _xattn_kernel(x_ref, kt_ref, v_ref, wq_ref, wo_ref, ln_ref, rw_ref, o_ref, ob_ref, s_ref):
    x = x_ref[...]
    q = _dot(x.astype(BF16), wq_ref[...]).astype(BF16)
    outs = []
    for h in range(XATTN_HEADS):
        hs = slice(h * XATTN_HEAD, (h + 1) * XATTN_HEAD)
        s = _dot(q[:, hs], kt_ref[0, hs, :]) * (XATTN_HEAD ** -0.5)
        m = jnp.max(s, axis=-1, keepdims=True)
        p = jnp.exp(s - m)
        l = jnp.sum(p, axis=-1, keepdims=True)
        outs.append(_dot(p.astype(BF16), v_ref[0, :, hs]) * (1.0 / l))
    o = jnp.concatenate(outs, axis=1).astype(BF16)
    x2 = _layer_norm(DN_ALPHA * x + _dot(o, wo_ref[...]), ln_ref[0:1, :], ln_ref[1:2, :])
    o_ref[...] = x2
    hi = x2.astype(BF16)
    ob_ref[...] = hi
    lo = (x2 - hi.astype(F32)).astype(BF16)
    logits = _dot(hi, rw_ref[0]) + (_dot(lo, rw_ref[0]) + _dot(hi, rw_ref[1]))
    s_ref[...] = _sigmoid(logits)


def _xattn(x, kt, v, wq, wo, ln, rw2, bsz, t, tm=256):
    n = bsz * t
    tps = t // tm
    d = D_MODEL

    def full(a):
        return pl.BlockSpec(a.shape, lambda i: (0,) * a.ndim)

    return pl.pallas_call(
        _xattn_kernel,
        grid=(n // tm,),
        in_specs=[pl.BlockSpec((tm, d), lambda i: (i, 0)),
                  pl.BlockSpec((1,) + kt.shape[1:], lambda i: (i // tps, 0, 0)),
                  pl.BlockSpec((1,) + v.shape[1:], lambda i: (i // tps, 0, 0)),
                  full(wq), full(wo), full(ln), full(rw2)],
        out_specs=[pl.BlockSpec((tm, d), lambda i: (i, 0)), pl.BlockSpec((tm, d), lambda i: (i, 0)),
                   pl.BlockSpec((tm, LANE), lambda i: (i, 0))],
        out_shape=[jax.ShapeDtypeStruct((n, d), F32), jax.ShapeDtypeStruct((n, d), BF16),
                   jax.ShapeDtypeStruct((n, LANE), F32)],
        compiler_params=_cparams(("parallel",)),
        name="xattn_ln_router",
    )(x, kt, v, wq, wo, ln, rw2)


def _expert_kernel(meta_ref, x_ref, gate_ref, *refs):
    w_refs, o_ref, w_scr = refs[:6], refs[6], refs[7:]
    i = pl.program_id(0)
    nb = pl.num_programs(0)
    prev = jnp.maximum(i - 1, 0)

    for side in range(2):
        @pl.when((i == 0) | (meta_ref[side * nb + i] != meta_ref[side * nb + prev]))
        def _():
            for j in range(3):
                w_scr[3 * side + j][...] = w_refs[3 * side + j][0, 0].astype(BF16)

    n_used = meta_ref[2 * nb]

    @pl.when(i < n_used)
    def _():
        x = x_ref[...]
        y = jnp.zeros(o_ref.shape, F32)
        for side in range(2):
            wg, wu, wd = w_scr[3 * side:3 * side + 3]
            hg = _dot(x, wg[...])
            h = hg * _sigmoid(hg) * _dot(x, wu[...])
            y = y + gate_ref[:, side:side + 1] * _dot(h.astype(BF16), wd[...])
        o_ref[...] = y.astype(o_ref.dtype)

    @pl.when(i >= n_used)
    def _():
        o_ref[...] = jnp.zeros(o_ref.shape, o_ref.dtype)


def _experts(meta, rows, gates, w_gate, w_up, w_down, layer):
    n_rows, d = rows.shape
    ff = w_gate.shape[3]
    nb = n_rows // MOE_BLK

    def weight(shape, side):
        return pl.BlockSpec((1, 1) + shape, lambda i, m: (layer, m[side * nb + i], 0, 0))

    return pl.pallas_call(
        _expert_kernel,
        grid_spec=pltpu.PrefetchScalarGridSpec(
            num_scalar_prefetch=1,
            grid=(nb,),
            in_specs=[pl.BlockSpec((MOE_BLK, d), lambda i, m: (i, 0)),
                      pl.BlockSpec((MOE_BLK, TOP_K), lambda i, m: (i, 0))]
                     + [weight(s, side) for side in range(2) for s in ((d, ff), (d, ff), (ff, d))],
            out_specs=pl.BlockSpec((MOE_BLK, d), lambda i, m: (i, 0)),
            scratch_shapes=[pltpu.VMEM(s, BF16) for _ in range(2) for s in ((d, ff), (d, ff), (ff, d))]),
        out_shape=jax.ShapeDtypeStruct((n_rows, d), BF16),
        compiler_params=_cparams(("arbitrary",)),
        name="moe_experts",
    )(meta, rows, gates, w_gate, w_up, w_down, w_gate, w_up, w_down)


def _combine_kernel(x_ref, y_ref, ln_ref, o_ref):
    o_ref[...] = _layer_norm(DN_ALPHA * x_ref[...] + y_ref[...], ln_ref[0:1, :], ln_ref[1:2, :])


def _combine(x, y, ln, tm=512):
    n, d = x.shape
    tok = pl.BlockSpec((tm, d), lambda i: (i, 0))
    return pl.pallas_call(
        _combine_kernel,
        grid=(n // tm,),
        in_specs=[tok, tok, pl.BlockSpec(ln.shape, lambda i: (0, 0))],
        out_specs=tok,
        out_shape=jax.ShapeDtypeStruct((n, d), F32),
        compiler_params=_cparams(("parallel",)),
        name="moe_combine_ln",
    )(x, y, ln)


def _t5_bucket(dist):
    n = jnp.maximum(dist, 0)
    max_exact = REL_BUCKETS // 2
    log_ratio = jnp.log(jnp.maximum(n, 1).astype(F32) / max_exact) / math.log(REL_MAX_DIST / max_exact)
    large = jnp.minimum(max_exact + (log_ratio * (REL_BUCKETS - max_exact)).astype(jnp.int32), REL_BUCKETS - 1)
    return jnp.where(n < max_exact, n, large)


def _bias_tables(rel_table, t):
    nc = t // CMP_STRIDE
    f = rel_table[_t5_bucket(jnp.arange(t + 2048))].astype(F32).T
    ql = jnp.arange(Q_BLOCK)[:, None]

    chunks_per_q = Q_BLOCK // CMP_STRIDE
    r = np.arange(CMP_STRIDE)[:, None]
    m = np.arange(2 * nc + chunks_per_q - 1)[None, :]
    dist_c = CMP_STRIDE * (nc - 8 + chunks_per_q - 1 - m) + r - (CMP_LEN - 1)
    small = jnp.where((dist_c >= 0)[None], jnp.take(f, np.clip(dist_c, 0, f.shape[-1] - 1), axis=1), NEG_INF)
    pc = jnp.stack([small[:, :, chunks_per_q - 1 - a:chunks_per_q - 1 - a + 2 * nc] for a in range(chunks_per_q)], axis=1)
    pc = pc.reshape(NSA_GROUPS, NSA_HPG * Q_BLOCK, 2 * nc)
    pc = jnp.stack([pc[:, :, s:s + 2 * nc - LANE] for s in range(0, LANE, 8)], axis=1)
    nvar = WINDOW // Q_BLOCK + 1

    def toeplitz(base, width, hi, shift):
        period = width + Q_BLOCK
        m = np.arange(period)
        dist = base - np.where(m < width, m, m - period)
        row = jnp.take(f, np.clip(dist, 0, f.shape[-1] - 1), axis=1) - shift
        row = jnp.where(((dist >= 0) & (dist < hi))[None], row, NEG_INF)
        flat = jnp.tile(row, (1, Q_BLOCK))[:, :Q_BLOCK * (period - 1)]
        return flat.reshape(NSA_HEADS, Q_BLOCK, period - 1)[:, :, :width]

    wb = jnp.stack([toeplitz(Q_BLOCK * v, WINDOW + Q_BLOCK, WINDOW, 0.0) for v in range(nvar)], axis=1)
    wb = wb.reshape(NSA_GROUPS, NSA_HPG, nvar, Q_BLOCK, -1).transpose(0, 2, 1, 3, 4)
    wb = wb.reshape(NSA_GROUPS, nvar, NSA_HPG * Q_BLOCK, -1)
    far = rel_table[REL_BUCKETS - 1].astype(F32)
    lt = toeplitz(SEL_NEAR + SEL_TK, SEL_NEAR + 2 * SEL_TK + Q_BLOCK, f.shape[-1], far[:, None])
    lt = lt.reshape(NSA_GROUPS, NSA_HPG * Q_BLOCK, -1)
    far_hi = far.astype(BF16).astype(F32)
    lane = jnp.arange(LANE)[None, :]
    cq = jnp.where(lane == HEAD_DIM, far_hi[:, None], jnp.where(lane == HEAD_DIM + 1, (far - far_hi)[:, None], 0.0))
    cq = jnp.broadcast_to(cq[:, None, :], (NSA_HEADS, Q_BLOCK, LANE)).reshape(NSA_GROUPS, NSA_HPG * Q_BLOCK, LANE)
    return pc, wb, lt, cq


def _in_proj_perm():
    perm = np.full((IN_PAD,), -1, np.int64)
    for h in range(NSA_HEADS):
        perm[C_Q + h * LANE:C_Q + h * LANE + HEAD_DIM] = np.arange(HEAD_DIM) + h * HEAD_DIM
    rw0 = NSA_COLS
    g0 = NSA_COLS + RWKV_COLS
    perm[C_GN:C_GN + D_MODEL] = g0 + np.arange(D_MODEL)
    perm[C_GR:C_GR + D_MODEL] = g0 + D_MODEL + np.arange(D_MODEL)
    perm[C_R:C_R + RWKV_WIDTH] = rw0 + np.arange(RWKV_WIDTH)
    perm[C_K:C_K + RWKV_WIDTH] = rw0 + RWKV_WIDTH + np.arange(RWKV_WIDTH)
    perm[C_V:C_V + RWKV_WIDTH] = rw0 + 2 * RWKV_WIDTH + np.arange(RWKV_WIDTH)
    for g in range(NSA_GROUPS):
        for br in range(3):
            for h in range(NSA_HPG):
                perm[C_G3 + g * LANE + br * NSA_HPG + h] = NSA_WIDTH + 6 * NSA_KV + (g * NSA_HPG + h) * 3 + br
    perm[C_WA:C_WA + LORA_W + LORA_A] = rw0 + 3 * RWKV_WIDTH + np.arange(LORA_W + LORA_A)
    perm[C_ZG:C_ZG + LORA_G] = rw0 + 3 * RWKV_WIDTH + LORA_W + LORA_A + np.arange(LORA_G)
    return perm


_PERM = _in_proj_perm()


def _permute_in_proj(w):
    pieces = []
    start = 0
    while start < IN_PAD:
        stop = start + 1
        if _PERM[start] < 0:
            while stop < IN_PAD and _PERM[stop] < 0:
                stop += 1
            pieces.append(jnp.zeros((w.shape[0], stop - start), BF16))
        else:
            while stop < IN_PAD and _PERM[stop] == _PERM[stop - 1] + 1:
                stop += 1
            pieces.append(w[:, int(_PERM[start]):int(_PERM[stop - 1]) + 1].astype(BF16))
        start = stop
    return jnp.concatenate(pieces, axis=1)


def _pad_to(a, shape):
    return jnp.pad(a, [(0, s - d) for d, s in zip(a.shape, shape)])


_CLASS_EXPERTS = np.asarray([(grp * EXPERTS_PER_GROUP + i, grp * EXPERTS_PER_GROUP + j)
                             for grp in range(N_EXPERT_GROUPS)
                             for i in range(EXPERTS_PER_GROUP) for j in range(i + 1, EXPERTS_PER_GROUP)], np.int32)
N_PAIR_CLASSES = len(_CLASS_EXPERTS)


def _route(s, router_bias, n_tok):
    s16 = s[:, :N_EXPERTS]
    s_sel = s16 + router_bias.astype(F32)

    def top2(a):
        i1 = jnp.argmax(a, axis=-1)
        rest = jnp.where(jnp.arange(a.shape[-1]) == i1[..., None], -jnp.inf, a)
        i2 = jnp.argmax(rest, axis=-1)
        return jnp.max(a, axis=-1) + jnp.max(rest, axis=-1), jnp.stack([i1, i2], axis=-1)

    group_score, _ = top2(s_sel.reshape(n_tok, N_EXPERT_GROUPS, EXPERTS_PER_GROUP))
    group = jnp.argmax(group_score, axis=-1)
    in_group = (jnp.arange(N_EXPERTS) // EXPERTS_PER_GROUP)[None, :] == group[:, None]
    _, e_idx = top2(jnp.where(in_group, s_sel, NEG_INF))
    picked = jnp.arange(N_EXPERTS)[None, None, :] == e_idx[:, :, None]
    s_top = jnp.sum(jnp.where(picked, s16[:, None, :], 0.0), axis=-1)
    gate = s_top / jnp.sum(s_top, axis=-1, keepdims=True)
    swap = e_idx[:, 0] > e_idx[:, 1]
    e_lo = jnp.where(swap, e_idx[:, 1], e_idx[:, 0])
    e_hi = jnp.where(swap, e_idx[:, 0], e_idx[:, 1])
    gate_lohi = jnp.where(swap[:, None], gate[:, ::-1], gate)
    l_lo, l_hi = e_lo % EXPERTS_PER_GROUP, e_hi % EXPERTS_PER_GROUP
    pairs_per_group = N_PAIR_CLASSES // N_EXPERT_GROUPS
    cls = ((e_lo // EXPERTS_PER_GROUP) * pairs_per_group
           + (l_lo * (2 * EXPERTS_PER_GROUP - 1 - l_lo)) // 2 + (l_hi - l_lo - 1))
    onehot = (cls[:, None] == jnp.arange(N_PAIR_CLASSES)[None, :]).astype(jnp.int32)
    counts = jnp.sum(onehot, axis=0)
    padded = (counts + MOE_BLK - 1) // MOE_BLK * MOE_BLK
    ends = jnp.cumsum(padded)
    before = jnp.cumsum(onehot, axis=0) - onehot + (ends - padded)[None, :]
    dest = jnp.sum(onehot * before, axis=1)
    n_rows = n_tok + N_PAIR_CLASSES * MOE_BLK
    tok_of_row = jnp.zeros((n_rows,), jnp.int32).at[dest].set(jnp.arange(n_tok, dtype=jnp.int32))
    block_start = jnp.arange(n_rows // MOE_BLK) * MOE_BLK
    block_cls = jnp.minimum(jnp.sum((ends[None, :] <= block_start[:, None]).astype(jnp.int32), axis=1),
                            N_PAIR_CLASSES - 1)
    block_onehot = (block_cls[:, None] == jnp.arange(N_PAIR_CLASSES)[None, :]).astype(jnp.int32)
    block_experts = jnp.sum(block_onehot[:, :, None] * jnp.asarray(_CLASS_EXPERTS)[None], axis=1)
    meta = jnp.concatenate([block_experts[:, 0], block_experts[:, 1], ends[-1:] // MOE_BLK]).astype(jnp.int32)
    return gate_lohi, dest, tok_of_row, meta


def kernel(x, mem, rel_table, router_w, router_bias, w_in, cmp_pe_k, cmp_pe_v, cmp_w1k, cmp_w2k, cmp_w1v, cmp_w2v, rwkv_mu, rwkv_w0, rwkv_w2, rwkv_a0, rwkv_a2, rwkv_g2, rwkv_kk, rwkv_ka, rwkv_rk, rwkv_gn_g, rwkv_gn_b, rwkv_v0, rwkv_v1, rwkv_v2, p_nsa, p_rwkv, w_out, ln1_g, ln1_b, xq_w, xk_w, xv_w, xo_w, ln2_g, ln2_b, moe_w_gate, moe_w_up, moe_w_down, ln3_g, ln3_b):
    bsz, t, d = x.shape
    n = bsz * t
    depth = w_in.shape[0]
    nc = t // CMP_STRIDE
    nsel = t // SEL_BLOCK
    g, hd = NSA_GROUPS, HEAD_DIM

    pc_tab, wb_tab, lt_tab, cq_tab = _bias_tables(rel_table, t)
    cmp_start = np.arange(nc) * CMP_STRIDE
    sel_start = np.arange(nsel) * SEL_BLOCK
    overlap = ((cmp_start[:, None] <= sel_start[None, :] + SEL_BLOCK - 1)
               & (cmp_start[:, None] + CMP_LEN - 1 >= sel_start[None, :]) & (cmp_start[:, None] < (nc - 1) * CMP_STRIDE))
    overlap = jnp.asarray(overlap.T, BF16)
    lane_head = np.arange(RWKV_WIDTH) // RWKV_HEAD
    hsum = jnp.asarray(lane_head[:, None] == lane_head[None, :], BF16)
    quad_head = np.arange(QUAD_LANES) // RWKV_HEAD
    ones_bd = jnp.asarray(quad_head[:, None] == quad_head[None, :], BF16)
    step = np.arange(SCAN_TT)[None, :, None]
    lane = np.arange(LANE)[None, None, :]
    sbh = np.arange(SCAN_TT // SUB_STEPS * QUAD)[:, None, None]
    place = jnp.asarray((lane // SUB_STEPS == sbh % QUAD) & (step == (sbh // QUAD) * SUB_STEPS + lane % SUB_STEPS), BF16)
    rw_f = _pad_to(router_w.astype(F32), (d, LANE))
    rw_hi = rw_f.astype(BF16)
    rw2 = jnp.stack([rw_hi, (rw_f - rw_hi.astype(F32)).astype(BF16)])

    xf = x.reshape(n, d)
    mem_f = mem.reshape(bsz * mem.shape[1], d)
    v_first = None
    for l in range(depth):
        w_in_l = _permute_in_proj(w_in[l])
        z = _matmul(xf, w_in_l, F32, 1024, 1024)

        kv0 = NSA_WIDTH
        zeros64 = jnp.zeros((d, LANE - hd), BF16)
        w_kv = jnp.stack([jnp.concatenate(
            [piece for i in range(KV_ARRAYS)
             for piece in (w_in[l][:, kv0 + i * NSA_KV + gi * hd:kv0 + i * NSA_KV + (gi + 1) * hd].astype(BF16), zeros64)],
            axis=1) for gi in range(g)])
        kc, vc, ks_aug, vs_aug, kw_pad, vw_aug = _kv_proj(xf, w_kv, bsz, t)

        def cmp_w(pe, w1, w2):
            return (pe.reshape(2, CMP_STRIDE * hd), w1.reshape(2, CMP_STRIDE * hd, CMP_HIDDEN).astype(BF16),
                    _pad_to(w2, (CMP_HIDDEN, LANE)).astype(BF16))

        chunks = lambda a: a.reshape(bsz, g, nc, CMP_STRIDE * hd)
        k_cmp = _compress(chunks(kc), *cmp_w(cmp_pe_k[l], cmp_w1k[l], cmp_w2k[l]))
        v_cmp = _compress(chunks(vc), *cmp_w(cmp_pe_v[l], cmp_w1v[l], cmp_w2v[l]))
        o_c, sel_mask = _cmp_branch(z, k_cmp, v_cmp, pc_tab, overlap, bsz, t)
        o_s = _sel_branch(z, ks_aug, vs_aug, sel_mask, lt_tab, cq_tab, bsz, t)
        o_w = _win_branch(z, kw_pad, vw_aug, wb_tab, bsz, t)

        mu = rwkv_mu[l]
        w3 = 3 * RWKV_WIDTH
        mu5 = jnp.stack([mu[0:512], mu[512:1024], mu[1024:1536], _pad_to(mu[w3:w3 + 128], (512,)),
                         _pad_to(mu[w3 + 128:w3 + 256], (512,))])
        vecs = jnp.stack([rwkv_w0[l], rwkv_a0[l], rwkv_kk[l], rwkv_ka[l], rwkv_rk[l].reshape(-1)])
        w2p = jnp.concatenate([rwkv_w2[l], jnp.zeros_like(rwkv_a2[l])], axis=0).astype(BF16)
        a2p = jnp.concatenate([jnp.zeros_like(rwkv_w2[l]), rwkv_a2[l]], axis=0).astype(BF16)
        vres = None
        if l > 0:
            vres = (v_first, rwkv_v0[l - 1][None, :], _pad_to(rwkv_v1[l - 1], (RWKV_WIDTH, LANE)).astype(BF16),
                    _pad_to(rwkv_v2[l - 1], (LANE, RWKV_WIDTH)).astype(BF16))
        r, w, k, v, kk, b, g_out, bonus, vt = _rwkv_prep(z, mu5, vecs, w2p, a2p, rwkv_g2[l].astype(BF16), hsum, vres,
                                                          bsz, t)
        if l == 0:
            v_first = v
        seq = lambda a: a.reshape(bsz, t, RWKV_WIDTH)
        o_rw = _rwkv_scan(seq(r), seq(w), seq(k), seq(kk), seq(b), vt, ones_bd, place, bsz, t)
        o_rw = o_rw.transpose(1, 0, 2, 3).reshape(n, RWKV_WIDTH)

        xf = _merge(xf, o_c, o_s, o_w, o_rw, bonus, g_out, z, jnp.stack([rwkv_gn_g[l], rwkv_gn_b[l]]), hsum,
                    p_nsa[l].astype(BF16), p_rwkv[l].astype(BF16), w_out[l].astype(BF16),
                    jnp.stack([ln1_g[l], ln1_b[l]]), bsz, t)

        mlen = mem.shape[1]
        mk = _matmul(mem_f, xk_w[l].astype(BF16), BF16, mlen, XATTN_WIDTH)
        mv = _matmul(mem_f, xv_w[l].astype(BF16), BF16, mlen, XATTN_WIDTH)
        kt = mk.reshape(bsz, mlen, XATTN_WIDTH).transpose(0, 2, 1)
        xf, xf_b, scores = _xattn(xf, kt, mv.reshape(bsz, mlen, XATTN_WIDTH), xq_w[l].astype(BF16), xo_w[l].astype(BF16),
                            jnp.stack([ln2_g[l], ln2_b[l]]), rw2, bsz, t)

        gate, dest, tok_of_row, meta = _route(scores, router_bias, n)
        rows = jnp.take(xf_b, tok_of_row, axis=0)
        y_rows = _experts(meta, rows, jnp.take(gate, tok_of_row, axis=0), moe_w_gate, moe_w_up, moe_w_down, l)
        xf = _combine(xf, jnp.take(y_rows, dest, axis=0), jnp.stack([ln3_g[l], ln3_b[l]]))
    return xf.reshape(bsz, t, d)
```

```python
import functools
import math

import numpy as np
import jax
import jax.numpy as jnp
from jax import lax
from jax.experimental import pallas as pl
from jax.experimental.pallas import tpu as pltpu

F32 = jnp.float32
BF16 = jnp.bfloat16

D_MODEL = 1024
DEPTH = 2
NSA_HEADS = 8
NSA_GROUPS = 2
NSA_HPG = NSA_HEADS // NSA_GROUPS
HEAD_DIM = 64
NSA_WIDTH = NSA_HEADS * HEAD_DIM
NSA_KV = NSA_GROUPS * HEAD_DIM
CMP_STRIDE = 16
CMP_LEN = 2 * CMP_STRIDE
CMP_HIDDEN = 256
SEL_BLOCK = 64
SEL_TOPN = 16
WINDOW = 512
Q_BLOCK = 128
SEL_FORCE = 1e9
RWKV_HEADS = 8
RWKV_HEAD = 64
RWKV_WIDTH = RWKV_HEADS * RWKV_HEAD
LORA_W = 64
LORA_A = 64
LORA_V = 32
LORA_G = 128
GN_EPS = 64e-5
REL_BUCKETS = 32
REL_MAX_DIST = 1024
XATTN_HEADS = 4
XATTN_HEAD = 128
XATTN_WIDTH = XATTN_HEADS * XATTN_HEAD
N_EXPERTS = 16
N_EXPERT_GROUPS = 4
EXPERTS_PER_GROUP = N_EXPERTS // N_EXPERT_GROUPS
TOP_K = 2
EXPERT_FF = 512
DN_ALPHA = (2 * DEPTH) ** 0.25
LN_EPS = 1e-5
NEG_INF = -1e30
NSA_COLS = NSA_WIDTH + 6 * NSA_KV + 3 * NSA_HEADS
RWKV_COLS = 3 * RWKV_WIDTH + LORA_W + LORA_A + LORA_G

LANE = 128
VMEM_LIMIT = 56 * 1024 * 1024

C_Q = 0
C_GN = 1024
C_GR = 2048
C_R = 3072
C_K = 3584
C_V = 4096
C_G3 = 4608
C_WA = 4864
C_ZG = 4992
IN_PAD = 5120
KV_ARRAYS = 6

MOE_BLK = 256
SCAN_TT = 128
FAR_BIAS_DIST = 1280


def _cparams(sem):
    return pltpu.CompilerParams(dimension_semantics=sem, vmem_limit_bytes=VMEM_LIMIT)


def _sigmoid(x):
    return 1.0 / (1.0 + jnp.exp(-x))


def _layer_norm(v, g, b):
    mu = jnp.mean(v, axis=-1, keepdims=True)
    d = v - mu
    var = jnp.mean(d * d, axis=-1, keepdims=True)
    return d * lax.rsqrt(var + LN_EPS) * g + b


def _dot(a, b):
    return jnp.dot(a, b, preferred_element_type=F32)


def _dot_nt(a, b):
    return lax.dot_general(a, b, (((1,), (1,)), ((), ())), preferred_element_type=F32)


def _dot_split(x, w):
    hi = x.astype(BF16)
    lo = (x - hi.astype(F32)).astype(BF16)
    return _dot(hi, w) + _dot(lo, w)


def _mm_kernel(x_ref, w_ref, o_ref, xb_ref):
    @pl.when(pl.program_id(1) == 0)
    def _():
        xb_ref[...] = x_ref[...].astype(BF16)

    o_ref[...] = _dot(xb_ref[...], w_ref[...]).astype(o_ref.dtype)


def _matmul(x, w, out_dtype, tm, tn):
    n, k = x.shape
    m = w.shape[1]
    return pl.pallas_call(
        _mm_kernel,
        grid=(n // tm, m // tn),
        in_specs=[pl.BlockSpec((tm, k), lambda i, j: (i, 0)),
                  pl.BlockSpec((k, tn), lambda i, j: (0, j))],
        out_specs=pl.BlockSpec((tm, tn), lambda i, j: (i, j)),
        out_shape=jax.ShapeDtypeStruct((n, m), out_dtype),
        scratch_shapes=[pltpu.VMEM((tm, k), BF16)],
        compiler_params=_cparams(("parallel", "arbitrary")),
        name="matmul",
    )(x, w)


def _kv_proj_kernel(tiles_per_seq, x_ref, w_ref, kc_ref, vc_ref, ks_ref, vs_ref, kw_ref, vw_ref, xb_ref):
    @pl.when(pl.program_id(1) == 0)
    def _():
        xb_ref[...] = x_ref[...].astype(BF16)

    y = _dot(xb_ref[...], w_ref[pl.program_id(1)])
    tm = y.shape[0]
    slot = lambda i: y[:, i * LANE:(i + 1) * LANE]
    lane = lax.broadcasted_iota(jnp.int32, (tm, LANE), 1)
    one_lane = jnp.where(lane == HEAD_DIM, 1.0, 0.0)
    t0 = lax.rem(pl.program_id(0), tiles_per_seq) * tm
    block = jnp.right_shift(t0 + lax.broadcasted_iota(jnp.int32, (tm, LANE), 0), int(math.log2(SEL_BLOCK)))
    kc_ref[0, 0] = slot(0)[:, :HEAD_DIM]
    vc_ref[0, 0] = slot(1)[:, :HEAD_DIM]
    ks_ref[0, 0] = jnp.concatenate([slot(2) + jnp.where(lane == HEAD_DIM + 1, 1.0, one_lane),
                                    jnp.where(lane == block, 1.0, 0.0)], axis=1).astype(BF16)
    vs_ref[0, 0] = (slot(3) + one_lane).astype(BF16)
    kw_ref[0, 0] = slot(4).astype(BF16)
    vw_ref[0, 0] = (slot(5) + one_lane).astype(BF16)


def _kv_proj(x, w_kv, bsz, t, tm=512):
    n, d = x.shape
    tps = t // tm
    g = NSA_GROUPS

    def out(width, dtype):
        return (pl.BlockSpec((1, 1, tm, width), lambda i, j: (i // tps, j, i % tps, 0)),
                jax.ShapeDtypeStruct((bsz, g, t, width), dtype))

    outs = [out(HEAD_DIM, F32), out(HEAD_DIM, F32), out(2 * LANE, BF16), out(LANE, BF16), out(LANE, BF16),
            out(LANE, BF16)]
    return pl.pallas_call(
        functools.partial(_kv_proj_kernel, tps),
        grid=(n // tm, g),
        in_specs=[pl.BlockSpec((tm, d), lambda i, j: (i, 0)),
                  pl.BlockSpec((g, d, KV_ARRAYS * LANE), lambda i, j: (0, 0, 0))],
        out_specs=[o[0] for o in outs],
        out_shape=[o[1] for o in outs],
        scratch_shapes=[pltpu.VMEM((tm, d), BF16)],
        compiler_params=_cparams(("parallel", "arbitrary")),
        name="nsa_kv_proj",
    )(x, w_kv)


def _compress_kernel(u_ref, pe_ref, w1_ref, w2_ref, o_ref):
    u = u_ref[0, 0]
    a = _dot((u + pe_ref[0:1, :]).astype(BF16), w1_ref[0])
    b = _dot((u + pe_ref[1:2, :]).astype(BF16), w1_ref[1])
    nc = u.shape[0]
    h = a + pltpu.roll(b, nc - 1, 0)
    h = 0.5 * h * (1.0 + jnp.tanh(math.sqrt(2.0 / math.pi) * (h + 0.044715 * (h * h * h))))
    y = _dot(h.astype(BF16), w2_ref[...])
    row = lax.broadcasted_iota(jnp.int32, y.shape, 0)
    o_ref[0, 0] = jnp.where(row < nc - 1, y, 0.0)


def _compress(u, pe2, w1, w2p):
    b, g, nc, _ = u.shape
    return pl.pallas_call(
        _compress_kernel,
        grid=(b, g),
        in_specs=[pl.BlockSpec((1, 1, nc, 1024), lambda i, j: (i, j, 0, 0)),
                  pl.BlockSpec((2, 1024), lambda i, j: (0, 0)),
                  pl.BlockSpec((2, 1024, CMP_HIDDEN), lambda i, j: (0, 0, 0)),
                  pl.BlockSpec((CMP_HIDDEN, LANE), lambda i, j: (0, 0))],
        out_specs=pl.BlockSpec((1, 1, nc, LANE), lambda i, j: (i, j, 0, 0)),
        out_shape=jax.ShapeDtypeStruct((b, g, nc, LANE), F32),
        compiler_params=_cparams(("parallel", "parallel")),
        name="nsa_compress",
    )(u, pe2, w1, w2p)


def _stack_heads(q_ref, row0=0):
    q = q_ref[row0:row0 + Q_BLOCK, :] * (HEAD_DIM ** -0.5)
    return jnp.concatenate([q[:, h * LANE:(h + 1) * LANE] for h in range(NSA_HPG)], axis=0).astype(BF16)


def _store_heads(o_ref, o, gates, branch, row0=0):
    lane = lax.broadcasted_iota(jnp.int32, (Q_BLOCK, LANE), 1)
    for pair in range(NSA_HPG // 2):
        halves = []
        for h in (2 * pair, 2 * pair + 1):
            c = branch * NSA_HPG + h
            halves.append(o[h * Q_BLOCK:(h + 1) * Q_BLOCK, :] * gates[:, c:c + 1])
        o_ref[0, pair, row0:row0 + Q_BLOCK, :] = jnp.where(lane < HEAD_DIM, halves[0],
                                                           pltpu.roll(halves[1], HEAD_DIM, 1)).astype(o_ref.dtype)


CMP_QB = 4


def _cmp_kernel(q_ref, zg_ref, kc_ref, vc_ref, *refs):
    pc_refs, (ov_ref, o_ref, m_ref), scratch = refs[:CMP_QB], refs[CMP_QB:CMP_QB + 3], refs[CMP_QB + 3:]
    nc = kc_ref.shape[2]
    nsel = ov_ref.shape[0]
    kc = kc_ref[0, 0].astype(BF16)
    vc = vc_ref[0, 0].astype(BF16)
    gates = _sigmoid(zg_ref[...])
    blocks = [(qb, pl.program_id(2) * CMP_QB + qb, pc_ref, scratch[2 * qb], scratch[2 * qb + 1])
              for qb, pc_ref in enumerate(pc_refs)]
    for qb, c, _, p_s, _ in blocks:
        p_s[...] = _dot_nt(_stack_heads(q_ref, qb * Q_BLOCK), kc)
    for qb, c, pc_ref, p_s, pb_s in blocks:
        off = pl.multiple_of(lax.div((nc - 8) - 8 * c, LANE) * LANE, LANE)
        for r in range(NSA_HPG * Q_BLOCK // SEL_ROWS):
            rows = slice(r * SEL_ROWS, (r + 1) * SEL_ROWS)
            s = p_s[rows, :] + pc_ref[0, 0, rows, pl.ds(off, nc)]
            e = jnp.exp(s - jnp.max(s, axis=-1, keepdims=True))
            l = jnp.sum(e, axis=-1, keepdims=True)
            ql = (lax.broadcasted_iota(jnp.int32, (SEL_ROWS, 1), 0) + r * SEL_ROWS) & (Q_BLOCK - 1)
            p = e * jnp.where((c * Q_BLOCK + ql) >= (CMP_LEN - 1), 1.0 / l, 0.0)
            p_s[rows, :] = p
            pb_s[rows, :] = p.astype(BF16)
    vals = []
    jb = lax.broadcasted_iota(jnp.int32, (nsel, Q_BLOCK), 0)
    jbf = jb.astype(F32)
    for qb, c, _, p_s, pb_s in blocks:
        _store_heads(o_ref, _dot(pb_s[...], vc), gates[qb * Q_BLOCK:(qb + 1) * Q_BLOCK, :], 0, qb * Q_BLOCK)
        psum = ((p_s[0:Q_BLOCK, :] + p_s[Q_BLOCK:2 * Q_BLOCK, :])
                + (p_s[2 * Q_BLOCK:3 * Q_BLOCK, :] + p_s[3 * Q_BLOCK:4 * Q_BLOCK, :]))
        hi = psum.astype(BF16)
        lo = (psum - hi.astype(F32)).astype(BF16)
        imp = _dot_nt(ov_ref[...], hi) + _dot_nt(ov_ref[...], lo)
        tq = c * Q_BLOCK + lax.broadcasted_iota(jnp.int32, (nsel, Q_BLOCK), 1)
        jq = jnp.right_shift(tq, int(math.log2(SEL_BLOCK)))
        forced = (jb == 0) | (jb == jq) | (jb == jq - 1)
        vals.append(jnp.where(forced, SEL_FORCE, jnp.where(jb <= jq, imp, -SEL_FORCE)))
    sels = [jnp.zeros((nsel, Q_BLOCK), F32) for _ in blocks]
    for _ in range(SEL_TOPN):
        for qb in range(CMP_QB):
            mx = jnp.max(vals[qb], axis=0, keepdims=True)
            first = jnp.min(jnp.where(vals[qb] == mx, jbf, float(nsel)), axis=0, keepdims=True)
            hit = jbf == first
            sels[qb] = jnp.where(hit, jnp.where(mx >= 0.0, 1.0, 0.0), sels[qb])
            vals[qb] = jnp.where(hit, -3e38, vals[qb])
    for qb in range(CMP_QB):
        m_ref[0, 0, qb * Q_BLOCK:(qb + 1) * Q_BLOCK, :] = (sels[qb].T - 1.0) * (-NEG_INF)


def _cmp_branch(z, k_cmp, v_cmp, pc_tab, overlap, bsz, t):
    tq = CMP_QB * Q_BLOCK
    nq = t // tq
    nc = k_cmp.shape[2]
    nsel = overlap.shape[0]
    rows = NSA_HPG * Q_BLOCK

    def variant(qb):
        return pl.BlockSpec((1, 1, rows, 2 * nc - LANE),
                            lambda b, g, c: (g, lax.rem((nc - 8) - 8 * (CMP_QB * c + qb), LANE) // 8, 0, 0))

    return pl.pallas_call(
        _cmp_kernel,
        grid=(bsz, NSA_GROUPS, nq),
        in_specs=[pl.BlockSpec((tq, NSA_HPG * LANE), lambda b, g, c: (b * nq + c, g)),
                  pl.BlockSpec((tq, LANE), lambda b, g, c: (b * nq + c, C_G3 // LANE + g)),
                  pl.BlockSpec((1, 1, nc, LANE), lambda b, g, c: (b, g, 0, 0)),
                  pl.BlockSpec((1, 1, nc, LANE), lambda b, g, c: (b, g, 0, 0)),
                  *[variant(qb) for qb in range(CMP_QB)],
                  pl.BlockSpec((nsel, nc), lambda b, g, c: (0, 0))],
        out_specs=[pl.BlockSpec((1, NSA_HPG // 2, tq, LANE), lambda b, g, c: (b, g, c, 0)),
                   pl.BlockSpec((1, 1, tq, nsel), lambda b, g, c: (b, g, c, 0))],
        out_shape=[jax.ShapeDtypeStruct((bsz, NSA_HEADS // 2, t, LANE), BF16),
                   jax.ShapeDtypeStruct((bsz, NSA_GROUPS, t, nsel), F32)],
        scratch_shapes=[pltpu.VMEM((rows, nc), F32), pltpu.VMEM((rows, nc), BF16)] * CMP_QB,
        compiler_params=_cparams(("parallel", "parallel", "arbitrary")),
        name="nsa_cmp_select",
    )(z, z, k_cmp, v_cmp, *[pc_tab] * CMP_QB, overlap)


WIN_QB = 4


def _win_kernel(q_ref, zg_ref, k_ref, v_ref, wb_ref, o_ref, *scratch):
    nvar = wb_ref.shape[1]
    gates = _sigmoid(zg_ref[...])
    blocks = []
    for qb in range(WIN_QB):
        c = pl.program_id(2) * WIN_QB + qb
        start = pl.multiple_of(jnp.maximum(c * Q_BLOCK - WINDOW, 0), Q_BLOCK)
        blocks.append((qb, jnp.minimum(c, nvar - 1), pl.ds(start, WINDOW + Q_BLOCK), scratch[2 * qb], scratch[2 * qb + 1]))
    for qb, _, keys, s_s, _ in blocks:
        s_s[...] = _dot_nt(_stack_heads(q_ref, qb * Q_BLOCK), k_ref[0, 0, keys, :])
    for qb, var, _, s_s, p_s in blocks:
        for r in range(NSA_HPG * Q_BLOCK // SEL_ROWS):
            rows = slice(r * SEL_ROWS, (r + 1) * SEL_ROWS)
            s = s_s[rows, :] + wb_ref[0, var, rows, :]
            p_s[rows, :] = jnp.exp(s - jnp.max(s, axis=-1, keepdims=True)).astype(BF16)
    for qb, _, keys, _, p_s in blocks:
        o = _dot(p_s[...], v_ref[0, 0, keys, :])
        o = o * (1.0 / o[:, HEAD_DIM:HEAD_DIM + 1])
        _store_heads(o_ref, o, gates[qb * Q_BLOCK:(qb + 1) * Q_BLOCK, :], 2, qb * Q_BLOCK)


def _win_branch(z, kw, vw, wb_tab, bsz, t):
    tq = WIN_QB * Q_BLOCK
    nq = t // tq
    rows, width = NSA_HPG * Q_BLOCK, WINDOW + Q_BLOCK
    return pl.pallas_call(
        _win_kernel,
        grid=(bsz, NSA_GROUPS, nq),
        in_specs=[pl.BlockSpec((tq, NSA_HPG * LANE), lambda b, g, c: (b * nq + c, g)),
                  pl.BlockSpec((tq, LANE), lambda b, g, c: (b * nq + c, C_G3 // LANE + g)),
                  pl.BlockSpec((1, 1, t, LANE), lambda b, g, c: (b, g, 0, 0)),
                  pl.BlockSpec((1, 1, t, LANE), lambda b, g, c: (b, g, 0, 0)),
                  pl.BlockSpec((1,) + wb_tab.shape[1:], lambda b, g, c: (g, 0, 0, 0))],
        out_specs=pl.BlockSpec((1, NSA_HPG // 2, tq, LANE), lambda b, g, c: (b, g, c, 0)),
        out_shape=jax.ShapeDtypeStruct((bsz, NSA_HEADS // 2, t, LANE), BF16),
        scratch_shapes=[pltpu.VMEM((rows, width), F32), pltpu.VMEM((rows, width), BF16)] * WIN_QB,
        compiler_params=_cparams(("parallel", "parallel", "arbitrary")),
        name="nsa_window",
    )(z, z, kw, vw, wb_tab)


SEL_TK = 512


SEL_ROWS = 64
SEL_QB = 4
SEL_NEAR = FAR_BIAS_DIST + Q_BLOCK


def _sel_kernel(q_ref, zg_ref, k_ref, v_ref, mt_ref, lt_ref, cq_ref, o_ref, m_s, acc_s, s0_s, s1_s, p0_s, p1_s,
                a0_s, a1_s):
    rows_per_block = NSA_HPG * Q_BLOCK
    c0 = pl.program_id(2) * SEL_QB
    q = q_ref[...] * (HEAD_DIM ** -0.5)
    parts = []
    for qb in range(SEL_QB):
        qrows = slice(qb * Q_BLOCK, (qb + 1) * Q_BLOCK)
        qs = jnp.concatenate([q[qrows, h * LANE:(h + 1) * LANE] for h in range(NSA_HPG)], axis=0) + cq_ref[0]
        parts.append(jnp.concatenate([qs, jnp.concatenate([mt_ref[0, 0, qrows, :]] * NSA_HPG, axis=0)], axis=1))
    q_aug = jnp.concatenate(parts, axis=0).astype(BF16)
    m_s[...] = jnp.full(m_s.shape, NEG_INF, F32)
    acc_s[...] = jnp.zeros(acc_s.shape, F32)
    ncol = SEL_TK // LANE
    n_tiles = lax.div(c0 + SEL_QB - 1, SEL_TK // Q_BLOCK) + 1

    def key_rows(i):
        return pl.ds(pl.multiple_of(jnp.minimum(i, n_tiles - 1) * SEL_TK, SEL_TK), SEL_TK)

    def scores(i, dst):
        dst[...] = _dot_nt(q_aug, k_ref[0, 0, key_rows(i), :])

    def tile(i, cur, nxt, p_buf, a_buf):
        scores(i + 1, nxt)
        for qb in range(SEL_QB):
            off = jnp.where(i < n_tiles, jnp.maximum(SEL_NEAR + SEL_TK - ((c0 + qb) * Q_BLOCK - i * SEL_TK), 0),
                            SEL_NEAR + SEL_TK + Q_BLOCK)
            off = pl.multiple_of(off, LANE)
            for r in range(rows_per_block // SEL_ROWS):
                trows = slice(r * SEL_ROWS, (r + 1) * SEL_ROWS)
                rows = slice(qb * rows_per_block + r * SEL_ROWS, qb * rows_per_block + (r + 1) * SEL_ROWS)
                s = cur[rows, :] + lt_ref[0, trows, pl.ds(off, SEL_TK)]
                cols = [s[:, j * LANE:(j + 1) * LANE] for j in range(ncol)]
                mx = functools.reduce(jnp.maximum, cols)
                m_old = m_s[rows, :]
                m_new = jnp.maximum(m_old, jnp.max(mx, axis=-1, keepdims=True))
                m_s[rows, :] = m_new
                a_buf[rows, :] = jnp.exp(m_old - m_new)
                p_buf[rows, :] = jnp.concatenate([jnp.exp(col - m_new) for col in cols], axis=1).astype(BF16)
        acc_s[...] = a_buf[...] * acc_s[...] + _dot(p_buf[...], v_ref[0, 0, key_rows(i), :])

    scores(0, s0_s)

    def pair(j, carry):
        tile(2 * j, s0_s, s1_s, p0_s, a0_s)
        tile(2 * j + 1, s1_s, s0_s, p1_s, a1_s)
        return carry

    lax.fori_loop(0, lax.div(n_tiles + 1, 2), pair, 0)
    acc = acc_s[...]
    o = acc * (1.0 / acc[:, HEAD_DIM:HEAD_DIM + 1])
    gates = _sigmoid(zg_ref[...])
    for qb in range(SEL_QB):
        _store_heads(o_ref, o[qb * rows_per_block:(qb + 1) * rows_per_block, :],
                     gates[qb * Q_BLOCK:(qb + 1) * Q_BLOCK, :], 1, qb * Q_BLOCK)


def _sel_branch(z, ks_aug, vs, mt, lt_tab, cq_tab, bsz, t):
    tq = SEL_QB * Q_BLOCK
    nq = t // tq
    nsel = mt.shape[-1]
    trows = NSA_HPG * Q_BLOCK
    rows = SEL_QB * trows
    return pl.pallas_call(
        _sel_kernel,
        grid=(bsz, NSA_GROUPS, nq),
        in_specs=[pl.BlockSpec((tq, NSA_HPG * LANE), lambda b, g, c: (b * nq + c, g)),
                  pl.BlockSpec((tq, LANE), lambda b, g, c: (b * nq + c, C_G3 // LANE + g)),
                  pl.BlockSpec((1, 1, t, 2 * LANE), lambda b, g, c: (b, g, 0, 0)),
                  pl.BlockSpec((1, 1, t, LANE), lambda b, g, c: (b, g, 0, 0)),
                  pl.BlockSpec((1, 1, tq, nsel), lambda b, g, c: (b, g, c, 0)),
                  pl.BlockSpec((1, trows, lt_tab.shape[2]), lambda b, g, c: (g, 0, 0)),
                  pl.BlockSpec((1, trows, LANE), lambda b, g, c: (g, 0, 0))],
        out_specs=pl.BlockSpec((1, NSA_HPG // 2, tq, LANE), lambda b, g, c: (b, g, c, 0)),
        out_shape=jax.ShapeDtypeStruct((bsz, NSA_HEADS // 2, t, LANE), BF16),
        scratch_shapes=[pltpu.VMEM((rows, LANE), F32),
                        pltpu.VMEM((rows, LANE), F32),
                        pltpu.VMEM((rows, SEL_TK), F32),
                        pltpu.VMEM((rows, SEL_TK), F32),
                        pltpu.VMEM((rows, SEL_TK), BF16),
                        pltpu.VMEM((rows, SEL_TK), BF16),
                        pltpu.VMEM((rows, LANE), F32),
                        pltpu.VMEM((rows, LANE), F32)],
        compiler_params=_cparams(("parallel", "parallel", "arbitrary")),
        name="nsa_selected",
    )(z, z, ks_aug, vs, mt, lt_tab, cq_tab)


def _shift_mix(z_ref, prev_ref, mu, first):
    z = z_ref[...]
    prev = jnp.where(first, 0.0, prev_ref[7:8, :])
    row = lax.broadcasted_iota(jnp.int32, z.shape, 0)
    zp = jnp.where(row == 0, prev, pltpu.roll(z, 1, 0))
    return z + mu * (zp - z)


def _rwkv_prep_kernel(has_vres, tiles_per_seq, *refs):
    (zr, zk, zv, zwa, zg, pr, pk, pv, pwa, pg, mu_ref, vec_ref, w2_ref, a2_ref, g2_ref, hsum_ref) = refs[:16]
    pos = 16
    if has_vres:
        vf_ref, v0_ref, v1_ref, v2_ref = refs[pos:pos + 4]
        pos += 4
    r_o, w_o, k_o, v_o, kk_o, b_o, g_o, bonus_o, vt_o = refs[pos:]
    first = lax.rem(pl.program_id(0), tiles_per_seq) == 0
    r = _shift_mix(zr, pr, mu_ref[0:1, :], first)
    k = _shift_mix(zk, pk, mu_ref[1:2, :], first)
    v = _shift_mix(zv, pv, mu_ref[2:3, :], first)
    wa = _shift_mix(zwa, pwa, mu_ref[3:4, 0:LANE], first)
    zg_s = _shift_mix(zg, pg, mu_ref[4:5, 0:LANE], first)
    w0, a0, k_k, k_a, r_k = (vec_ref[i:i + 1, :] for i in range(5))
    if has_vres:
        lora = _dot(_dot(v.astype(BF16), v1_ref[...]).astype(BF16), v2_ref[...])
        v = v + (vf_ref[...] - v) * _sigmoid(v0_ref[...] + lora)
    u = w0 + _dot(jnp.tanh(wa).astype(BF16), w2_ref[...])
    decay = jnp.exp(-math.exp(-0.5) * _sigmoid(u))
    a = _sigmoid(a0 + _dot(wa.astype(BF16), a2_ref[...]))
    g = _dot(_sigmoid(zg_s).astype(BF16), g2_ref[...])
    kk = k * k_k
    kk = kk / jnp.maximum(jnp.sqrt(_dot_split(kk * kk, hsum_ref[...])), 1e-12)
    k = k * (1.0 + (a - 1.0) * k_a)
    r_o[...] = r
    w_o[...] = decay
    k_o[...] = k.astype(k_o.dtype)
    v_o[...] = v
    kk_o[...] = kk
    b_o[...] = kk * a
    g_o[...] = g.astype(g_o.dtype)
    bonus_o[...] = (_dot_split(r * k * r_k, hsum_ref[...]) * v).astype(bonus_o.dtype)
    vt_o[0] = v.T.astype(BF16)


def _rwkv_prep(z, mu5, vecs, w2p, a2p, g2, hsum, vres, bsz, t, tm=512):
    n = bsz * t
    nt = n // tm
    tps = t // tm
    w = RWKV_WIDTH

    def cur(width, col):
        return pl.BlockSpec((tm, width), lambda i: (i, col // width))

    def prev(width, col):
        return pl.BlockSpec((8, width), lambda i: (jnp.maximum(i * (tm // 8) - 1, 0), col // width))

    def full(shape):
        return pl.BlockSpec(shape, lambda i: (0,) * len(shape))

    cols = [(w, C_R), (w, C_K), (w, C_V), (LANE, C_WA), (LANE, C_ZG)]
    in_specs = [cur(*c) for c in cols] + [prev(*c) for c in cols]
    in_specs += [full(mu5.shape), full(vecs.shape), full(w2p.shape), full(a2p.shape), full(g2.shape), full(hsum.shape)]
    args = [z] * 10 + [mu5, vecs, w2p, a2p, g2, hsum]
    if vres is not None:
        v_first, v0, v1p, v2p = vres
        in_specs += [pl.BlockSpec((tm, w), lambda i: (i, 0)), full(v0.shape), full(v1p.shape), full(v2p.shape)]
        args += [v_first, v0, v1p, v2p]
    tok = pl.BlockSpec((tm, w), lambda i: (i, 0))
    out_specs = [tok] * 8 + [pl.BlockSpec((1, w, tm), lambda i: (i // tps, 0, i % tps))]
    tok_dtypes = (F32, F32, BF16, F32, F32, F32, BF16, BF16)
    out_shape = [jax.ShapeDtypeStruct((n, w), dt) for dt in tok_dtypes] + [jax.ShapeDtypeStruct((bsz, w, t), BF16)]
    return pl.pallas_call(
        functools.partial(_rwkv_prep_kernel, vres is not None, tps),
        grid=(nt,),
        in_specs=in_specs,
        out_specs=out_specs,
        out_shape=out_shape,
        compiler_params=_cparams(("parallel",)),
        name="rwkv_prep",
    )(*args)


QUAD = 4
QUAD_LANES = QUAD * RWKV_HEAD
SUB_STEPS = LANE // QUAD


def _scan_kernel(r_ref, w_ref, k_ref, kk_ref, b_ref, vt_ref, ones_ref, place_ref, o_ref, s_ref, vq_ref, kq_ref, vk0_ref,
                 vk1_ref):
    bsz = r_ref.shape[0]
    tt = r_ref.shape[1]
    nh = RWKV_HEAD
    quads = [(bi, hf) for bi in range(bsz) for hf in range(RWKV_HEADS // QUAD)]

    @pl.when(pl.program_id(0) == 0)
    def _():
        s_ref[...] = jnp.zeros(s_ref.shape, F32)

    sub_shift, head_shift = int(math.log2(SUB_STEPS)), int(math.log2(nh))
    sub_lane = lax.broadcasted_iota(jnp.int32, (nh, LANE), 1) & (SUB_STEPS - 1)
    k_row_head = jnp.right_shift(lax.broadcasted_iota(jnp.int32, (LANE, QUAD_LANES), 0), sub_shift)
    k_lane_head = jnp.right_shift(lax.broadcasted_iota(jnp.int32, (LANE, QUAD_LANES), 1), head_shift)
    r_mask = (lax.broadcasted_iota(jnp.int32, (16, QUAD_LANES), 0)
              == jnp.right_shift(lax.broadcasted_iota(jnp.int32, (16, QUAD_LANES), 1), head_shift))
    ones_bd = ones_ref[...]

    def sub_block(sb, carry):
        s0 = pl.multiple_of(sb * SUB_STEPS, SUB_STEPS)
        for q, (bi, hf) in enumerate(quads):
            ln = slice(hf * QUAD_LANES, (hf + 1) * QUAD_LANES)
            vq = _dot(vt_ref[bi, (QUAD * hf) * nh:(QUAD * hf + 1) * nh, :], place_ref[sb * QUAD])
            for h in range(1, QUAD):
                vq = vq + _dot(vt_ref[bi, (QUAD * hf + h) * nh:(QUAD * hf + h + 1) * nh, :], place_ref[sb * QUAD + h])
            vq_ref[q] = vq.astype(BF16)
            k_sub = k_ref[bi, pl.ds(s0, SUB_STEPS), ln].astype(F32)
            kq_ref[q] = jnp.where(k_row_head == k_lane_head, jnp.concatenate([k_sub] * QUAD, axis=0), 0.0).astype(BF16)

        def group(tg, vk_ref, carry):
            t0 = pl.multiple_of(s0 + tg * 8, 8)
            rows = []
            for bi, hf in quads:
                ln = slice(hf * QUAD_LANES, (hf + 1) * QUAD_LANES)
                rows.append(tuple(ref[bi, pl.ds(t0, 8), ln] for ref in (w_ref, kk_ref, b_ref, r_ref)))
            onehots = [jnp.where(sub_lane == tg * 8 + u, 1.0, 0.0).astype(BF16) for u in range(8)]
            for q in range(len(quads)):
                vq = vq_ref[q]
                vk_ref[q] = _dot(jnp.concatenate([vq * oh for oh in onehots], axis=0), kq_ref[q])
            state = [s_ref[q] for q in range(len(quads))]
            for u in range(8):
                sk = _dot(jnp.concatenate([(state[q] * rows[q][1][u:u + 1, :]).astype(BF16)
                                           for q in range(len(quads))], axis=0), ones_bd)
                for q, (bi, hf) in enumerate(quads):
                    w8, _, b8, r8 = rows[q]
                    state[q] = (state[q] * w8[u:u + 1, :] - sk[q * nh:(q + 1) * nh, :] * b8[u:u + 1, :]
                                + vk_ref[q, u * nh:(u + 1) * nh, :])
                    r_lhs = jnp.where(r_mask, r8[u:u + 1, :], 0.0).astype(BF16)
                    o = _dot_nt(r_lhs, state[q].astype(BF16))
                    o_ref[t0 + u, bi, QUAD * hf:QUAD * (hf + 1), :] = o[0:QUAD, :]
            for q in range(len(quads)):
                s_ref[q] = state[q]
            return carry

        for tg in range(SUB_STEPS // 8):
            carry = group(tg, (vk0_ref, vk1_ref)[tg % 2], carry)
        return carry

    lax.fori_loop(0, tt // SUB_STEPS, sub_block, 0)


def _rwkv_scan(r, w, k, kk, b, vt, ones_bd, place, bsz, t):
    tt = SCAN_TT
    wd = RWKV_WIDTH
    nquad = bsz * RWKV_HEADS // QUAD
    tok = pl.BlockSpec((bsz, tt, wd), lambda i: (0, i, 0))
    return pl.pallas_call(
        _scan_kernel,
        grid=(t // tt,),
        in_specs=[tok, tok, tok, tok, tok,
                  pl.BlockSpec((bsz, wd, tt), lambda i: (0, 0, i)),
                  pl.BlockSpec(ones_bd.shape, lambda i: (0, 0)),
                  pl.BlockSpec(place.shape, lambda i: (0, 0, 0))],
        out_specs=pl.BlockSpec((tt, bsz, RWKV_HEADS, RWKV_HEAD), lambda i: (i, 0, 0, 0)),
        out_shape=jax.ShapeDtypeStruct((t, bsz, RWKV_HEADS, RWKV_HEAD), F32),
        scratch_shapes=[pltpu.VMEM((nquad, RWKV_HEAD, QUAD_LANES), F32),
                        pltpu.VMEM((nquad, RWKV_HEAD, LANE), BF16),
                        pltpu.VMEM((nquad, LANE, QUAD_LANES), BF16),
                        pltpu.VMEM((nquad, 8 * RWKV_HEAD, QUAD_LANES), F32),
                        pltpu.VMEM((nquad, 8 * RWKV_HEAD, QUAD_LANES), F32)],
        compiler_params=_cparams(("arbitrary",)),
        name="rwkv_scan",
    )(r, w, k, kk, b, vt, ones_bd, place)


def _merge_kernel(x_ref, oc_ref, os_ref, ow_ref, orw_ref, bonus_ref, g_ref, zgn_ref, zgr_ref, gn_ref, hsum_ref,
                  pn_ref, pr_ref, wo_ref, ln_ref, o_ref):
    o_n = jnp.concatenate([oc_ref[0, p].astype(F32) + os_ref[0, p].astype(F32) + ow_ref[0, p].astype(F32)
                           for p in range(NSA_HEADS // 2)], axis=1)
    y_n = _dot(o_n.astype(BF16), pn_ref[...])
    o = orw_ref[...]
    mu = _dot_split(o, hsum_ref[...]) * (1.0 / RWKV_HEAD)
    d = o - mu
    var = _dot_split(d * d, hsum_ref[...]) * (1.0 / RWKV_HEAD)
    o = d * lax.rsqrt(var + GN_EPS) * gn_ref[0:1, :] + gn_ref[1:2, :]
    o = ((o + bonus_ref[...]) * g_ref[...]).astype(BF16)
    y1 = _sigmoid(zgn_ref[...]) * y_n + _sigmoid(zgr_ref[...]) * _dot(o, pr_ref[...])
    y = _dot(y1.astype(BF16), wo_ref[...])
    o_ref[...] = _layer_norm(DN_ALPHA * x_ref[...] + y, ln_ref[0:1, :], ln_ref[1:2, :])


def _merge(x, o_c, o_s, o_w, o_rw, bonus, g, z, gn, hsum, p_nsa, p_rwkv, w_out, ln, bsz, t, tm=512):
    n = bsz * t
    tps = t // tm
    d = D_MODEL
    w = RWKV_WIDTH
    head = pl.BlockSpec((1, NSA_HEADS // 2, tm, LANE), lambda i: (i // tps, 0, i % tps, 0))
    tok = pl.BlockSpec((tm, w), lambda i: (i, 0))

    def full(a):
        return pl.BlockSpec(a.shape, lambda i: (0,) * a.ndim)

    return pl.pallas_call(
        _merge_kernel,
        grid=(n // tm,),
        in_specs=[pl.BlockSpec((tm, d), lambda i: (i, 0)), head, head, head, tok, tok, tok,
                  pl.BlockSpec((tm, d), lambda i: (i, C_GN // d)),
                  pl.BlockSpec((tm, d), lambda i: (i, C_GR // d)),
                  full(gn), full(hsum), full(p_nsa), full(p_rwkv), full(w_out), full(ln)],
        out_specs=pl.BlockSpec((tm, d), lambda i: (i, 0)),
        out_shape=jax.ShapeDtypeStruct((n, d), F32),
        compiler_params=_cparams(("parallel",)),
        name="mix_merge_ln",
    )(x, o_c, o_s, o_w, o_rw, bonus, g, z, z, gn, hsum, p_nsa, p_rwkv, w_out, ln)


def _xattn_kernel(x_ref, kt_ref, v_ref, wq_ref, wo_ref, ln_ref, rw_ref, o_ref, ob_ref, s_ref):
    x = x_ref[...]
    q = _dot(x.astype(BF16), wq_ref[...]).astype(BF16)
    outs = []
    for h in range(XATTN_HEADS):
        hs = slice(h * XATTN_HEAD, (h + 1) * XATTN_HEAD)
        s = _dot(q[:, hs], kt_ref[0, hs, :]) * (XATTN_HEAD ** -0.5)
        m = jnp.max(s, axis=-1, keepdims=True)
        p = jnp.exp(s - m)
        l = jnp.sum(p, axis=-1, keepdims=True)
        outs.append(_dot(p.astype(BF16), v_ref[0, :, hs]) * (1.0 / l))
    o = jnp.concatenate(outs, axis=1).astype(BF16)
    x2 = _layer_norm(DN_ALPHA * x + _dot(o, wo_ref[...]), ln_ref[0:1, :], ln_ref[1:2, :])
    o_ref[...] = x2
    hi = x2.astype(BF16)
    ob_ref[...] = hi
    lo = (x2 - hi.astype(F32)).astype(BF16)
    logits = _dot(hi, rw_ref[0]) + (_dot(lo, rw_ref[0]) + _dot(hi, rw_ref[1]))
    s_ref[...] = _sigmoid(logits)


def _xattn(x, kt, v, wq, wo, ln, rw2, bsz, t, tm=256):
    n = bsz * t
    tps = t // tm
    d = D_MODEL

    def full(a):
        return pl.BlockSpec(a.shape, lambda i: (0,) * a.ndim)

    return pl.pallas_call(
        _xattn_kernel,
        grid=(n // tm,),
        in_specs=[pl.BlockSpec((tm, d), lambda i: (i, 0)),
                  pl.BlockSpec((1,) + kt.shape[1:], lambda i: (i // tps, 0, 0)),
                  pl.BlockSpec((1,) + v.shape[1:], lambda i: (i // tps, 0, 0)),
                  full(wq), full(wo), full(ln), full(rw2)],
        out_specs=[pl.BlockSpec((tm, d), lambda i: (i, 0)), pl.BlockSpec((tm, d), lambda i: (i, 0)),
                   pl.BlockSpec((tm, LANE), lambda i: (i, 0))],
        out_shape=[jax.ShapeDtypeStruct((n, d), F32), jax.ShapeDtypeStruct((n, d), BF16),
                   jax.ShapeDtypeStruct((n, LANE), F32)],
        compiler_params=_cparams(("parallel",)),
        name="xattn_ln_router",
    )(x, kt, v, wq, wo, ln, rw2)


def _expert_kernel(meta_ref, x_ref, gate_ref, *refs):
    w_refs, o_ref, w_scr = refs[:6], refs[6], refs[7:]
    i = pl.program_id(0)
    nb = pl.num_programs(0)
    prev = jnp.maximum(i - 1, 0)

    for side in range(2):
        @pl.when((i == 0) | (meta_ref[side * nb + i] != meta_ref[side * nb + prev]))
        def _():
            for j in range(3):
                w_scr[3 * side + j][...] = w_refs[3 * side + j][0, 0].astype(BF16)

    n_used = meta_ref[2 * nb]

    @pl.when(i < n_used)
    def _():
        x = x_ref[...]
        y = jnp.zeros(o_ref.shape, F32)
        for side in range(2):
            wg, wu, wd = w_scr[3 * side:3 * side + 3]
            hg = _dot(x, wg[...])
            h = hg * _sigmoid(hg) * _dot(x, wu[...])
            y = y + gate_ref[:, side:side + 1] * _dot(h.astype(BF16), wd[...])
        o_ref[...] = y.astype(o_ref.dtype)

    @pl.when(i >= n_used)
    def _():
        o_ref[...] = jnp.zeros(o_ref.shape, o_ref.dtype)


def _experts(meta, rows, gates, w_gate, w_up, w_down, layer):
    n_rows, d = rows.shape
    ff = w_gate.shape[3]
    nb = n_rows // MOE_BLK

    def weight(shape, side):
        return pl.BlockSpec((1, 1) + shape, lambda i, m: (layer, m[side * nb + i], 0, 0))

    return pl.pallas_call(
        _expert_kernel,
        grid_spec=pltpu.PrefetchScalarGridSpec(
            num_scalar_prefetch=1,
            grid=(nb,),
            in_specs=[pl.BlockSpec((MOE_BLK, d), lambda i, m: (i, 0)),
                      pl.BlockSpec((MOE_BLK, TOP_K), lambda i, m: (i, 0))]
                     + [weight(s, side) for side in range(2) for s in ((d, ff), (d, ff), (ff, d))],
            out_specs=pl.BlockSpec((MOE_BLK, d), lambda i, m: (i, 0)),
            scratch_shapes=[pltpu.VMEM(s, BF16) for _ in range(2) for s in ((d, ff), (d, ff), (ff, d))]),
        out_shape=jax.ShapeDtypeStruct((n_rows, d), BF16),
        compiler_params=_cparams(("arbitrary",)),
        name="moe_experts",
    )(meta, rows, gates, w_gate, w_up, w_down, w_gate, w_up, w_down)


def _combine_kernel(x_ref, y_ref, ln_ref, o_ref):
    o_ref[...] = _layer_norm(DN_ALPHA * x_ref[...] + y_ref[...], ln_ref[0:1, :], ln_ref[1:2, :])


def _combine(x, y, ln, tm=512):
    n, d = x.shape
    tok = pl.BlockSpec((tm, d), lambda i: (i, 0))
    return pl.pallas_call(
        _combine_kernel,
        grid=(n // tm,),
        in_specs=[tok, tok, pl.BlockSpec(ln.shape, lambda i: (0, 0))],
        out_specs=tok,
        out_shape=jax.ShapeDtypeStruct((n, d), F32),
        compiler_params=_cparams(("parallel",)),
        name="moe_combine_ln",
    )(x, y, ln)


def _t5_bucket(dist):
    n = jnp.maximum(dist, 0)
    max_exact = REL_BUCKETS // 2
    log_ratio = jnp.log(jnp.maximum(n, 1).astype(F32) / max_exact) / math.log(REL_MAX_DIST / max_exact)
    large = jnp.minimum(max_exact + (log_ratio * (REL_BUCKETS - max_exact)).astype(jnp.int32), REL_BUCKETS - 1)
    return jnp.where(n < max_exact, n, large)


def _bias_tables(rel_table, t):
    nc = t // CMP_STRIDE
    n_dist = t + 2048
    onehot = (_t5_bucket(jnp.arange(n_dist))[:, None] == jnp.arange(REL_BUCKETS)[None, :]).astype(F32)
    f = jnp.dot(onehot, rel_table.astype(F32), precision=lax.Precision.HIGHEST).T
    pad = n_dist
    f_ext = jnp.concatenate([jnp.full((NSA_HEADS, pad), NEG_INF, F32), f], axis=1)

    def at(first, step, count):
        last = first + step * (count - 1)
        assert step < 0 and last + pad >= 0 and first < n_dist
        return lax.slice(f_ext, (0, last + pad), (NSA_HEADS, first + pad + 1), (1, -step))[:, ::-1]

    chunks_per_q = Q_BLOCK // CMP_STRIDE
    n_m = 2 * nc + chunks_per_q - 1
    small = jnp.stack([at(CMP_STRIDE * (nc - 8 + chunks_per_q - 1) + r - (CMP_LEN - 1), -CMP_STRIDE, n_m)
                       for r in range(CMP_STRIDE)], axis=1)
    pc = jnp.stack([small[:, :, chunks_per_q - 1 - a:chunks_per_q - 1 - a + 2 * nc] for a in range(chunks_per_q)], axis=1)
    pc = pc.reshape(NSA_GROUPS, NSA_HPG * Q_BLOCK, 2 * nc)
    pc = jnp.stack([pc[:, :, s:s + 2 * nc - LANE] for s in range(0, LANE, 8)], axis=1)
    nvar = WINDOW // Q_BLOCK + 1

    def toeplitz(base, width, hi, shift):
        period = width + Q_BLOCK
        row = jnp.concatenate([at(base, -1, width), at(base + Q_BLOCK, -1, Q_BLOCK)], axis=1) - shift
        m = np.arange(period)
        dist = base - np.where(m < width, m, m - period)
        row = jnp.where((dist < hi)[None], row, NEG_INF)
        flat = jnp.tile(row, (1, Q_BLOCK))[:, :Q_BLOCK * (period - 1)]
        return flat.reshape(NSA_HEADS, Q_BLOCK, period - 1)[:, :, :width]

    wb = jnp.stack([toeplitz(Q_BLOCK * v, WINDOW + Q_BLOCK, WINDOW, 0.0) for v in range(nvar)], axis=1)
    wb = wb.reshape(NSA_GROUPS, NSA_HPG, nvar, Q_BLOCK, -1).transpose(0, 2, 1, 3, 4)
    wb = wb.reshape(NSA_GROUPS, nvar, NSA_HPG * Q_BLOCK, -1)
    far = rel_table[REL_BUCKETS - 1].astype(F32)
    lt = toeplitz(SEL_NEAR + SEL_TK, SEL_NEAR + 2 * SEL_TK + Q_BLOCK, n_dist, far[:, None])
    lt = lt.reshape(NSA_GROUPS, NSA_HPG * Q_BLOCK, -1)
    far_hi = far.astype(BF16).astype(F32)
    lane = jnp.arange(LANE)[None, :]
    cq = jnp.where(lane == HEAD_DIM, far_hi[:, None], jnp.where(lane == HEAD_DIM + 1, (far - far_hi)[:, None], 0.0))
    cq = jnp.broadcast_to(cq[:, None, :], (NSA_HEADS, Q_BLOCK, LANE)).reshape(NSA_GROUPS, NSA_HPG * Q_BLOCK, LANE)
    return pc, wb, lt, cq


def _in_proj_perm():
    perm = np.full((IN_PAD,), -1, np.int64)
    for h in range(NSA_HEADS):
        perm[C_Q + h * LANE:C_Q + h * LANE + HEAD_DIM] = np.arange(HEAD_DIM) + h * HEAD_DIM
    rw0 = NSA_COLS
    g0 = NSA_COLS + RWKV_COLS
    perm[C_GN:C_GN + D_MODEL] = g0 + np.arange(D_MODEL)
    perm[C_GR:C_GR + D_MODEL] = g0 + D_MODEL + np.arange(D_MODEL)
    perm[C_R:C_R + RWKV_WIDTH] = rw0 + np.arange(RWKV_WIDTH)
    perm[C_K:C_K + RWKV_WIDTH] = rw0 + RWKV_WIDTH + np.arange(RWKV_WIDTH)
    perm[C_V:C_V + RWKV_WIDTH] = rw0 + 2 * RWKV_WIDTH + np.arange(RWKV_WIDTH)
    for g in range(NSA_GROUPS):
        for br in range(3):
            for h in range(NSA_HPG):
                perm[C_G3 + g * LANE + br * NSA_HPG + h] = NSA_WIDTH + 6 * NSA_KV + (g * NSA_HPG + h) * 3 + br
    perm[C_WA:C_WA + LORA_W + LORA_A] = rw0 + 3 * RWKV_WIDTH + np.arange(LORA_W + LORA_A)
    perm[C_ZG:C_ZG + LORA_G] = rw0 + 3 * RWKV_WIDTH + LORA_W + LORA_A + np.arange(LORA_G)
    return perm


_PERM = _in_proj_perm()


def _permute_in_proj(w):
    pieces = []
    start = 0
    while start < IN_PAD:
        stop = start + 1
        if _PERM[start] < 0:
            while stop < IN_PAD and _PERM[stop] < 0:
                stop += 1
            pieces.append(jnp.zeros((w.shape[0], stop - start), BF16))
        else:
            while stop < IN_PAD and _PERM[stop] == _PERM[stop - 1] + 1:
                stop += 1
            pieces.append(w[:, int(_PERM[start]):int(_PERM[stop - 1]) + 1].astype(BF16))
        start = stop
    return jnp.concatenate(pieces, axis=1)


def _pad_to(a, shape):
    return jnp.pad(a, [(0, s - d) for d, s in zip(a.shape, shape)])


_CLASS_EXPERTS = np.asarray([(grp * EXPERTS_PER_GROUP + i, grp * EXPERTS_PER_GROUP + j)
                             for grp in range(N_EXPERT_GROUPS)
                             for i in range(EXPERTS_PER_GROUP) for j in range(i + 1, EXPERTS_PER_GROUP)], np.int32)
N_PAIR_CLASSES = len(_CLASS_EXPERTS)


def _route(s, router_bias, n_tok):
    s16 = s[:, :N_EXPERTS]
    s_sel = s16 + router_bias.astype(F32)

    def top2(a):
        i1 = jnp.argmax(a, axis=-1)
        rest = jnp.where(jnp.arange(a.shape[-1]) == i1[..., None], -jnp.inf, a)
        i2 = jnp.argmax(rest, axis=-1)
        return jnp.max(a, axis=-1) + jnp.max(rest, axis=-1), jnp.stack([i1, i2], axis=-1)

    group_score, _ = top2(s_sel.reshape(n_tok, N_EXPERT_GROUPS, EXPERTS_PER_GROUP))
    group = jnp.argmax(group_score, axis=-1)
    in_group = (jnp.arange(N_EXPERTS) // EXPERTS_PER_GROUP)[None, :] == group[:, None]
    _, e_idx = top2(jnp.where(in_group, s_sel, NEG_INF))
    picked = jnp.arange(N_EXPERTS)[None, None, :] == e_idx[:, :, None]
    s_top = jnp.sum(jnp.where(picked, s16[:, None, :], 0.0), axis=-1)
    gate = s_top / jnp.sum(s_top, axis=-1, keepdims=True)
    swap = e_idx[:, 0] > e_idx[:, 1]
    e_lo = jnp.where(swap, e_idx[:, 1], e_idx[:, 0])
    e_hi = jnp.where(swap, e_idx[:, 0], e_idx[:, 1])
    gate_lohi = jnp.where(swap[:, None], gate[:, ::-1], gate)
    l_lo, l_hi = e_lo % EXPERTS_PER_GROUP, e_hi % EXPERTS_PER_GROUP
    pairs_per_group = N_PAIR_CLASSES // N_EXPERT_GROUPS
    cls = ((e_lo // EXPERTS_PER_GROUP) * pairs_per_group
           + (l_lo * (2 * EXPERTS_PER_GROUP - 1 - l_lo)) // 2 + (l_hi - l_lo - 1))
    onehot = (cls[:, None] == jnp.arange(N_PAIR_CLASSES)[None, :]).astype(jnp.int32)
    counts = jnp.sum(onehot, axis=0)
    padded = (counts + MOE_BLK - 1) // MOE_BLK * MOE_BLK
    ends = jnp.cumsum(padded)
    before = jnp.cumsum(onehot, axis=0) - onehot + (ends - padded)[None, :]
    dest = jnp.sum(onehot * before, axis=1)
    n_rows = n_tok + N_PAIR_CLASSES * MOE_BLK
    tok_of_row = jnp.zeros((n_rows,), jnp.int32).at[dest].set(jnp.arange(n_tok, dtype=jnp.int32))
    block_start = jnp.arange(n_rows // MOE_BLK) * MOE_BLK
    block_cls = jnp.minimum(jnp.sum((ends[None, :] <= block_start[:, None]).astype(jnp.int32), axis=1),
                            N_PAIR_CLASSES - 1)
    block_onehot = (block_cls[:, None] == jnp.arange(N_PAIR_CLASSES)[None, :]).astype(jnp.int32)
    block_experts = jnp.sum(block_onehot[:, :, None] * jnp.asarray(_CLASS_EXPERTS)[None], axis=1)
    meta = jnp.concatenate([block_experts[:, 0], block_experts[:, 1], ends[-1:] // MOE_BLK]).astype(jnp.int32)
    return gate_lohi, dest, tok_of_row, meta


def kernel(x, mem, rel_table, router_w, router_bias, w_in, cmp_pe_k, cmp_pe_v, cmp_w1k, cmp_w2k, cmp_w1v, cmp_w2v, rwkv_mu, rwkv_w0, rwkv_w2, rwkv_a0, rwkv_a2, rwkv_g2, rwkv_kk, rwkv_ka, rwkv_rk, rwkv_gn_g, rwkv_gn_b, rwkv_v0, rwkv_v1, rwkv_v2, p_nsa, p_rwkv, w_out, ln1_g, ln1_b, xq_w, xk_w, xv_w, xo_w, ln2_g, ln2_b, moe_w_gate, moe_w_up, moe_w_down, ln3_g, ln3_b):
    bsz, t, d = x.shape
    n = bsz * t
    depth = w_in.shape[0]
    nc = t // CMP_STRIDE
    nsel = t // SEL_BLOCK
    g, hd = NSA_GROUPS, HEAD_DIM

    pc_tab, wb_tab, lt_tab, cq_tab = _bias_tables(rel_table, t)
    cmp_start = np.arange(nc) * CMP_STRIDE
    sel_start = np.arange(nsel) * SEL_BLOCK
    overlap = ((cmp_start[:, None] <= sel_start[None, :] + SEL_BLOCK - 1)
               & (cmp_start[:, None] + CMP_LEN - 1 >= sel_start[None, :]) & (cmp_start[:, None] < (nc - 1) * CMP_STRIDE))
    overlap = jnp.asarray(overlap.T, BF16)
    lane_head = np.arange(RWKV_WIDTH) // RWKV_HEAD
    hsum = jnp.asarray(lane_head[:, None] == lane_head[None, :], BF16)
    quad_head = np.arange(QUAD_LANES) // RWKV_HEAD
    ones_bd = jnp.asarray(quad_head[:, None] == quad_head[None, :], BF16)
    step = np.arange(SCAN_TT)[None, :, None]
    lane = np.arange(LANE)[None, None, :]
    sbh = np.arange(SCAN_TT // SUB_STEPS * QUAD)[:, None, None]
    place = jnp.asarray((lane // SUB_STEPS == sbh % QUAD) & (step == (sbh // QUAD) * SUB_STEPS + lane % SUB_STEPS), BF16)
    rw_f = _pad_to(router_w.astype(F32), (d, LANE))
    rw_hi = rw_f.astype(BF16)
    rw2 = jnp.stack([rw_hi, (rw_f - rw_hi.astype(F32)).astype(BF16)])

    xf = x.reshape(n, d)
    mem_f = mem.reshape(bsz * mem.shape[1], d)
    v_first = None
    for l in range(depth):
        w_in_l = _permute_in_proj(w_in[l])
        z = _matmul(xf, w_in_l, F32, 1024, 1024)

        kv0 = NSA_WIDTH
        zeros64 = jnp.zeros((d, LANE - hd), BF16)
        w_kv = jnp.stack([jnp.concatenate(
            [piece for i in range(KV_ARRAYS)
             for piece in (w_in[l][:, kv0 + i * NSA_KV + gi * hd:kv0 + i * NSA_KV + (gi + 1) * hd].astype(BF16), zeros64)],
            axis=1) for gi in range(g)])
        kc, vc, ks_aug, vs_aug, kw_pad, vw_aug = _kv_proj(xf, w_kv, bsz, t)

        def cmp_w(pe, w1, w2):
            return (pe.reshape(2, CMP_STRIDE * hd), w1.reshape(2, CMP_STRIDE * hd, CMP_HIDDEN).astype(BF16),
                    _pad_to(w2, (CMP_HIDDEN, LANE)).astype(BF16))

        chunks = lambda a: a.reshape(bsz, g, nc, CMP_STRIDE * hd)
        k_cmp = _compress(chunks(kc), *cmp_w(cmp_pe_k[l], cmp_w1k[l], cmp_w2k[l]))
        v_cmp = _compress(chunks(vc), *cmp_w(cmp_pe_v[l], cmp_w1v[l], cmp_w2v[l]))
        o_c, sel_mask = _cmp_branch(z, k_cmp, v_cmp, pc_tab, overlap, bsz, t)
        o_s = _sel_branch(z, ks_aug, vs_aug, sel_mask, lt_tab, cq_tab, bsz, t)
        o_w = _win_branch(z, kw_pad, vw_aug, wb_tab, bsz, t)

        mu = rwkv_mu[l]
        w3 = 3 * RWKV_WIDTH
        mu5 = jnp.stack([mu[0:512], mu[512:1024], mu[1024:1536], _pad_to(mu[w3:w3 + 128], (512,)),
                         _pad_to(mu[w3 + 128:w3 + 256], (512,))])
        vecs = jnp.stack([rwkv_w0[l], rwkv_a0[l], rwkv_kk[l], rwkv_ka[l], rwkv_rk[l].reshape(-1)])
        w2p = jnp.concatenate([rwkv_w2[l], jnp.zeros_like(rwkv_a2[l])], axis=0).astype(BF16)
        a2p = jnp.concatenate([jnp.zeros_like(rwkv_w2[l]), rwkv_a2[l]], axis=0).astype(BF16)
        vres = None
        if l > 0:
            vres = (v_first, rwkv_v0[l - 1][None, :], _pad_to(rwkv_v1[l - 1], (RWKV_WIDTH, LANE)).astype(BF16),
                    _pad_to(rwkv_v2[l - 1], (LANE, RWKV_WIDTH)).astype(BF16))
        r, w, k, v, kk, b, g_out, bonus, vt = _rwkv_prep(z, mu5, vecs, w2p, a2p, rwkv_g2[l].astype(BF16), hsum, vres,
                                                          bsz, t)
        if l == 0:
            v_first = v
        seq = lambda a: a.reshape(bsz, t, RWKV_WIDTH)
        o_rw = _rwkv_scan(seq(r), seq(w), seq(k), seq(kk), seq(b), vt, ones_bd, place, bsz, t)
        o_rw = o_rw.transpose(1, 0, 2, 3).reshape(n, RWKV_WIDTH)

        xf = _merge(xf, o_c, o_s, o_w, o_rw, bonus, g_out, z, jnp.stack([rwkv_gn_g[l], rwkv_gn_b[l]]), hsum,
                    p_nsa[l].astype(BF16), p_rwkv[l].astype(BF16), w_out[l].astype(BF16),
                    jnp.stack([ln1_g[l], ln1_b[l]]), bsz, t)

        mlen = mem.shape[1]
        mk = _matmul(mem_f, xk_w[l].astype(BF16), BF16, mlen, XATTN_WIDTH)
        mv = _matmul(mem_f, xv_w[l].astype(BF16), BF16, mlen, XATTN_WIDTH)
        kt = mk.reshape(bsz, mlen, XATTN_WIDTH).transpose(0, 2, 1)
        xf, xf_b, scores = _xattn(xf, kt, mv.reshape(bsz, mlen, XATTN_WIDTH), xq_w[l].astype(BF16), xo_w[l].astype(BF16),
                            jnp.stack([ln2_g[l], ln2_b[l]]), rw2, bsz, t)

        gate, dest, tok_of_row, meta = _route(scores, router_bias, n)
        rows = jnp.take(xf_b, tok_of_row, axis=0)
        y_rows = _experts(meta, rows, jnp.take(gate, tok_of_row, axis=0), moe_w_gate, moe_w_up, moe_w_down, l)
        xf = _combine(xf, jnp.take(y_rows, dest, axis=0), jnp.stack([ln3_g[l], ln3_b[l]]))
    return xf.reshape(bsz, t, d)
```

```python
import functools
import math

import numpy as np
import jax
import jax.numpy as jnp
from jax import lax
from jax.experimental import pallas as pl
from jax.experimental.pallas import tpu as pltpu

F32 = jnp.float32
BF16 = jnp.bfloat16

D_MODEL = 1024
DEPTH = 2
NSA_HEADS = 8
NSA_GROUPS = 2
NSA_HPG = NSA_HEADS // NSA_GROUPS
HEAD_DIM = 64
NSA_WIDTH = NSA_HEADS * HEAD_DIM
NSA_KV = NSA_GROUPS * HEAD_DIM
CMP_STRIDE = 16
CMP_LEN = 2 * CMP_STRIDE
CMP_HIDDEN = 256
SEL_BLOCK = 64
SEL_TOPN = 16
WINDOW = 512
Q_BLOCK = 128
SEL_FORCE = 1e9
RWKV_HEADS = 8
RWKV_HEAD = 64
RWKV_WIDTH = RWKV_HEADS * RWKV_HEAD
LORA_W = 64
LORA_A = 64
LORA_V = 32
LORA_G = 128
GN_EPS = 64e-5
REL_BUCKETS = 32
REL_MAX_DIST = 1024
XATTN_HEADS = 4
XATTN_HEAD = 128
XATTN_WIDTH = XATTN_HEADS * XATTN_HEAD
N_EXPERTS = 16
N_EXPERT_GROUPS = 4
EXPERTS_PER_GROUP = N_EXPERTS // N_EXPERT_GROUPS
TOP_K = 2
EXPERT_FF = 512
DN_ALPHA = (2 * DEPTH) ** 0.25
LN_EPS = 1e-5
NEG_INF = -1e30
NSA_COLS = NSA_WIDTH + 6 * NSA_KV + 3 * NSA_HEADS
RWKV_COLS = 3 * RWKV_WIDTH + LORA_W + LORA_A + LORA_G

LANE = 128
VMEM_LIMIT = 56 * 1024 * 1024

C_Q = 0
C_GN = 1024
C_GR = 2048
C_R = 3072
C_K = 3584
C_V = 4096
C_G3 = 4608
C_WA = 4864
C_ZG = 4992
IN_PAD = 5120
C_SPLIT = C_R
KV_ARRAYS = 6

MOE_BLK = 256
SCAN_TT = 128
FAR_BIAS_DIST = 1280


def _cparams(sem):
    return pltpu.CompilerParams(dimension_semantics=sem, vmem_limit_bytes=VMEM_LIMIT)


def _sigmoid(x):
    return 1.0 / (1.0 + jnp.exp(-x))


def _layer_norm(v, g, b):
    mu = jnp.mean(v, axis=-1, keepdims=True)
    d = v - mu
    var = jnp.mean(d * d, axis=-1, keepdims=True)
    return d * lax.rsqrt(var + LN_EPS) * g + b


def _dot(a, b):
    return jnp.dot(a, b, preferred_element_type=F32)


def _dot_nt(a, b):
    return lax.dot_general(a, b, (((1,), (1,)), ((), ())), preferred_element_type=F32)


def _dot_split(x, w):
    hi = x.astype(BF16)
    lo = (x - hi.astype(F32)).astype(BF16)
    return _dot(hi, w) + _dot(lo, w)


def _mm_kernel(x_ref, w_ref, o_ref, xb_ref):
    @pl.when(pl.program_id(1) == 0)
    def _():
        xb_ref[...] = x_ref[...].astype(BF16)

    o_ref[...] = _dot(xb_ref[...], w_ref[...]).astype(o_ref.dtype)


def _matmul(x, w, out_dtype, tm, tn):
    n, k = x.shape
    m = w.shape[1]
    return pl.pallas_call(
        _mm_kernel,
        grid=(n // tm, m // tn),
        in_specs=[pl.BlockSpec((tm, k), lambda i, j: (i, 0)),
                  pl.BlockSpec((k, tn), lambda i, j: (0, j))],
        out_specs=pl.BlockSpec((tm, tn), lambda i, j: (i, j)),
        out_shape=jax.ShapeDtypeStruct((n, m), out_dtype),
        scratch_shapes=[pltpu.VMEM((tm, k), BF16)],
        compiler_params=_cparams(("parallel", "arbitrary")),
        name="matmul",
    )(x, w)


IN_TN = 1024


def _in_proj_kernel(x_ref, w_ref, oa_ref, ob_ref, xb_ref):
    j = pl.program_id(1)

    @pl.when(j == 0)
    def _():
        xb_ref[...] = x_ref[...].astype(BF16)

    @pl.when(j < C_SPLIT // IN_TN)
    def _():
        oa_ref[...] = _dot(xb_ref[...], w_ref[...]).astype(oa_ref.dtype)

    @pl.when(j >= C_SPLIT // IN_TN)
    def _():
        ob_ref[...] = _dot(xb_ref[...], w_ref[...])


def _in_proj(x, w, tm=1024):
    n, k = x.shape
    na = C_SPLIT // IN_TN
    return pl.pallas_call(
        _in_proj_kernel,
        grid=(n // tm, IN_PAD // IN_TN),
        in_specs=[pl.BlockSpec((tm, k), lambda i, j: (i, 0)),
                  pl.BlockSpec((k, IN_TN), lambda i, j: (0, j))],
        out_specs=[pl.BlockSpec((tm, IN_TN), lambda i, j: (i, jnp.minimum(j, na - 1))),
                   pl.BlockSpec((tm, IN_TN), lambda i, j: (i, jnp.maximum(j - na, 0)))],
        out_shape=[jax.ShapeDtypeStruct((n, C_SPLIT), BF16), jax.ShapeDtypeStruct((n, IN_PAD - C_SPLIT), F32)],
        scratch_shapes=[pltpu.VMEM((tm, k), BF16)],
        compiler_params=_cparams(("parallel", "arbitrary")),
        name="in_proj",
    )(x, w)


def _kv_proj_kernel(tiles_per_seq, x_ref, w_ref, kc_ref, vc_ref, ks_ref, vs_ref, kw_ref, vw_ref, xb_ref):
    @pl.when(pl.program_id(1) == 0)
    def _():
        xb_ref[...] = x_ref[...].astype(BF16)

    y = _dot(xb_ref[...], w_ref[pl.program_id(1)])
    tm = y.shape[0]
    slot = lambda i: y[:, i * LANE:(i + 1) * LANE]
    lane = lax.broadcasted_iota(jnp.int32, (tm, LANE), 1)
    one_lane = jnp.where(lane == HEAD_DIM, 1.0, 0.0)
    t0 = lax.rem(pl.program_id(0), tiles_per_seq) * tm
    block = jnp.right_shift(t0 + lax.broadcasted_iota(jnp.int32, (tm, LANE), 0), int(math.log2(SEL_BLOCK)))
    kc_ref[0, 0] = slot(0)[:, :HEAD_DIM]
    vc_ref[0, 0] = slot(1)[:, :HEAD_DIM]
    ks_ref[0, 0] = jnp.concatenate([slot(2) + jnp.where(lane == HEAD_DIM + 1, 1.0, one_lane),
                                    jnp.where(lane == block, 1.0, 0.0)], axis=1).astype(BF16)
    vs_ref[0, 0] = (slot(3) + one_lane).astype(BF16)
    kw_ref[0, 0] = slot(4).astype(BF16)
    vw_ref[0, 0] = (slot(5) + one_lane).astype(BF16)


def _kv_proj(x, w_kv, bsz, t, tm=512):
    n, d = x.shape
    tps = t // tm
    g = NSA_GROUPS

    def out(width, dtype):
        return (pl.BlockSpec((1, 1, tm, width), lambda i, j: (i // tps, j, i % tps, 0)),
                jax.ShapeDtypeStruct((bsz, g, t, width), dtype))

    outs = [out(HEAD_DIM, F32), out(HEAD_DIM, F32), out(2 * LANE, BF16), out(LANE, BF16), out(LANE, BF16),
            out(LANE, BF16)]
    return pl.pallas_call(
        functools.partial(_kv_proj_kernel, tps),
        grid=(n // tm, g),
        in_specs=[pl.BlockSpec((tm, d), lambda i, j: (i, 0)),
                  pl.BlockSpec((g, d, KV_ARRAYS * LANE), lambda i, j: (0, 0, 0))],
        out_specs=[o[0] for o in outs],
        out_shape=[o[1] for o in outs],
        scratch_shapes=[pltpu.VMEM((tm, d), BF16)],
        compiler_params=_cparams(("parallel", "arbitrary")),
        name="nsa_kv_proj",
    )(x, w_kv)


def _compress_kernel(u_ref, pe_ref, w1_ref, w2_ref, o_ref):
    u = u_ref[0, 0]
    a = _dot((u + pe_ref[0:1, :]).astype(BF16), w1_ref[0])
    b = _dot((u + pe_ref[1:2, :]).astype(BF16), w1_ref[1])
    nc = u.shape[0]
    h = a + pltpu.roll(b, nc - 1, 0)
    h = 0.5 * h * (1.0 + jnp.tanh(math.sqrt(2.0 / math.pi) * (h + 0.044715 * (h * h * h))))
    y = _dot(h.astype(BF16), w2_ref[...])
    row = lax.broadcasted_iota(jnp.int32, y.shape, 0)
    o_ref[0, 0] = jnp.where(row < nc - 1, y, 0.0)


def _compress(u, pe2, w1, w2p):
    b, g, nc, _ = u.shape
    return pl.pallas_call(
        _compress_kernel,
        grid=(b, g),
        in_specs=[pl.BlockSpec((1, 1, nc, 1024), lambda i, j: (i, j, 0, 0)),
                  pl.BlockSpec((2, 1024), lambda i, j: (0, 0)),
                  pl.BlockSpec((2, 1024, CMP_HIDDEN), lambda i, j: (0, 0, 0)),
                  pl.BlockSpec((CMP_HIDDEN, LANE), lambda i, j: (0, 0))],
        out_specs=pl.BlockSpec((1, 1, nc, LANE), lambda i, j: (i, j, 0, 0)),
        out_shape=jax.ShapeDtypeStruct((b, g, nc, LANE), F32),
        compiler_params=_cparams(("parallel", "parallel")),
        name="nsa_compress",
    )(u, pe2, w1, w2p)


def _stack_heads(q_ref, row0=0):
    q = q_ref[row0:row0 + Q_BLOCK, :] * (HEAD_DIM ** -0.5)
    return jnp.concatenate([q[:, h * LANE:(h + 1) * LANE] for h in range(NSA_HPG)], axis=0).astype(BF16)


def _store_heads(o_ref, o, gates, branch, row0=0):
    lane = lax.broadcasted_iota(jnp.int32, (Q_BLOCK, LANE), 1)
    for pair in range(NSA_HPG // 2):
        halves = []
        for h in (2 * pair, 2 * pair + 1):
            c = branch * NSA_HPG + h
            halves.append(o[h * Q_BLOCK:(h + 1) * Q_BLOCK, :] * gates[:, c:c + 1])
        o_ref[0, pair, row0:row0 + Q_BLOCK, :] = jnp.where(lane < HEAD_DIM, halves[0],
                                                           pltpu.roll(halves[1], HEAD_DIM, 1)).astype(o_ref.dtype)


CMP_QB = 4


def _cmp_kernel(q_ref, zg_ref, kc_ref, vc_ref, *refs):
    pc_refs, (ov_ref, o_ref, m_ref), scratch = refs[:CMP_QB], refs[CMP_QB:CMP_QB + 3], refs[CMP_QB + 3:]
    nc = kc_ref.shape[2]
    nsel = ov_ref.shape[0]
    kc = kc_ref[0, 0].astype(BF16)
    vc = vc_ref[0, 0].astype(BF16)
    gates = _sigmoid(zg_ref[...])
    blocks = [(qb, pl.program_id(2) * CMP_QB + qb, pc_ref, scratch[2 * qb], scratch[2 * qb + 1])
              for qb, pc_ref in enumerate(pc_refs)]
    for qb, c, _, p_s, _ in blocks:
        p_s[...] = _dot_nt(_stack_heads(q_ref, qb * Q_BLOCK), kc)
    for qb, c, pc_ref, p_s, pb_s in blocks:
        off = pl.multiple_of(lax.div((nc - 8) - 8 * c, LANE) * LANE, LANE)
        for r in range(NSA_HPG * Q_BLOCK // SEL_ROWS):
            rows = slice(r * SEL_ROWS, (r + 1) * SEL_ROWS)
            s = p_s[rows, :] + pc_ref[0, 0, rows, pl.ds(off, nc)]
            e = jnp.exp(s - jnp.max(s, axis=-1, keepdims=True))
            l = jnp.sum(e, axis=-1, keepdims=True)
            ql = (lax.broadcasted_iota(jnp.int32, (SEL_ROWS, 1), 0) + r * SEL_ROWS) & (Q_BLOCK - 1)
            p = e * jnp.where((c * Q_BLOCK + ql) >= (CMP_LEN - 1), 1.0 / l, 0.0)
            p_s[rows, :] = p
            pb_s[rows, :] = p.astype(BF16)
    vals, sels = [], []
    jb = lax.broadcasted_iota(jnp.int32, (nsel, Q_BLOCK), 0)
    jbf = jb.astype(F32)
    for qb, c, _, p_s, pb_s in blocks:
        _store_heads(o_ref, _dot(pb_s[...], vc), gates[qb * Q_BLOCK:(qb + 1) * Q_BLOCK, :], 0, qb * Q_BLOCK)
        psum = ((p_s[0:Q_BLOCK, :] + p_s[Q_BLOCK:2 * Q_BLOCK, :])
                + (p_s[2 * Q_BLOCK:3 * Q_BLOCK, :] + p_s[3 * Q_BLOCK:4 * Q_BLOCK, :]))
        hi = psum.astype(BF16)
        lo = (psum - hi.astype(F32)).astype(BF16)
        imp = _dot_nt(ov_ref[...], hi) + _dot_nt(ov_ref[...], lo)
        tq = c * Q_BLOCK + lax.broadcasted_iota(jnp.int32, (nsel, Q_BLOCK), 1)
        jq = jnp.right_shift(tq, int(math.log2(SEL_BLOCK)))
        forced = (jb == 0) | (jb == jq) | (jb == jq - 1)
        vals.append(jnp.where(forced, -3e38, jnp.where(jb <= jq, imp, -SEL_FORCE)))
        sels.append(jnp.where(forced, 1.0, 0.0))
    for _ in range(SEL_TOPN - 3):
        for qb in range(CMP_QB):
            mx = jnp.max(vals[qb], axis=0, keepdims=True)
            first = jnp.min(jnp.where(vals[qb] == mx, jbf, float(nsel)), axis=0, keepdims=True)
            hit = jbf == first
            sels[qb] = jnp.where(hit, jnp.where(mx >= 0.0, 1.0, 0.0), sels[qb])
            vals[qb] = jnp.where(hit, -3e38, vals[qb])
    for qb in range(CMP_QB):
        m_ref[0, 0, qb * Q_BLOCK:(qb + 1) * Q_BLOCK, :] = (sels[qb].T - 1.0) * (-NEG_INF)


def _cmp_branch(za, zb, k_cmp, v_cmp, pc_tab, overlap, bsz, t):
    tq = CMP_QB * Q_BLOCK
    nq = t // tq
    nc = k_cmp.shape[2]
    nsel = overlap.shape[0]
    rows = NSA_HPG * Q_BLOCK

    def variant(qb):
        return pl.BlockSpec((1, 1, rows, 2 * nc - LANE),
                            lambda b, g, c: (g, lax.rem((nc - 8) - 8 * (CMP_QB * c + qb), LANE) // 8, 0, 0))

    return pl.pallas_call(
        _cmp_kernel,
        grid=(bsz, NSA_GROUPS, nq),
        in_specs=[pl.BlockSpec((tq, NSA_HPG * LANE), lambda b, g, c: (b * nq + c, g)),
                  pl.BlockSpec((tq, LANE), lambda b, g, c: (b * nq + c, (C_G3 - C_SPLIT) // LANE + g)),
                  pl.BlockSpec((1, 1, nc, LANE), lambda b, g, c: (b, g, 0, 0)),
                  pl.BlockSpec((1, 1, nc, LANE), lambda b, g, c: (b, g, 0, 0)),
                  *[variant(qb) for qb in range(CMP_QB)],
                  pl.BlockSpec((nsel, nc), lambda b, g, c: (0, 0))],
        out_specs=[pl.BlockSpec((1, NSA_HPG // 2, tq, LANE), lambda b, g, c: (b, g, c, 0)),
                   pl.BlockSpec((1, 1, tq, nsel), lambda b, g, c: (b, g, c, 0))],
        out_shape=[jax.ShapeDtypeStruct((bsz, NSA_HEADS // 2, t, LANE), BF16),
                   jax.ShapeDtypeStruct((bsz, NSA_GROUPS, t, nsel), F32)],
        scratch_shapes=[pltpu.VMEM((rows, nc), F32), pltpu.VMEM((rows, nc), BF16)] * CMP_QB,
        compiler_params=_cparams(("parallel", "parallel", "arbitrary")),
        name="nsa_cmp_select",
    )(za, zb, k_cmp, v_cmp, *[pc_tab] * CMP_QB, overlap)


WIN_QB = 4


def _win_kernel(q_ref, zg_ref, k_ref, v_ref, wb_ref, o_ref, *scratch):
    nvar = wb_ref.shape[1]
    gates = _sigmoid(zg_ref[...])
    blocks = []
    for qb in range(WIN_QB):
        c = pl.program_id(2) * WIN_QB + qb
        start = pl.multiple_of(jnp.maximum(c * Q_BLOCK - WINDOW, 0), Q_BLOCK)
        blocks.append((qb, jnp.minimum(c, nvar - 1), pl.ds(start, WINDOW + Q_BLOCK), scratch[2 * qb], scratch[2 * qb + 1]))
    for qb, _, keys, s_s, _ in blocks:
        s_s[...] = _dot_nt(_stack_heads(q_ref, qb * Q_BLOCK), k_ref[0, 0, keys, :])
    for qb, var, _, s_s, p_s in blocks:
        for r in range(NSA_HPG * Q_BLOCK // SEL_ROWS):
            rows = slice(r * SEL_ROWS, (r + 1) * SEL_ROWS)
            s = s_s[rows, :] + wb_ref[0, var, rows, :]
            p_s[rows, :] = jnp.exp(s - jnp.max(s, axis=-1, keepdims=True)).astype(BF16)
    for qb, _, keys, _, p_s in blocks:
        o = _dot(p_s[...], v_ref[0, 0, keys, :])
        o = o * (1.0 / o[:, HEAD_DIM:HEAD_DIM + 1])
        _store_heads(o_ref, o, gates[qb * Q_BLOCK:(qb + 1) * Q_BLOCK, :], 2, qb * Q_BLOCK)


def _win_branch(za, zb, kw, vw, wb_tab, bsz, t):
    tq = WIN_QB * Q_BLOCK
    nq = t // tq
    rows, width = NSA_HPG * Q_BLOCK, WINDOW + Q_BLOCK
    return pl.pallas_call(
        _win_kernel,
        grid=(bsz, NSA_GROUPS, nq),
        in_specs=[pl.BlockSpec((tq, NSA_HPG * LANE), lambda b, g, c: (b * nq + c, g)),
                  pl.BlockSpec((tq, LANE), lambda b, g, c: (b * nq + c, (C_G3 - C_SPLIT) // LANE + g)),
                  pl.BlockSpec((1, 1, t, LANE), lambda b, g, c: (b, g, 0, 0)),
                  pl.BlockSpec((1, 1, t, LANE), lambda b, g, c: (b, g, 0, 0)),
                  pl.BlockSpec((1,) + wb_tab.shape[1:], lambda b, g, c: (g, 0, 0, 0))],
        out_specs=pl.BlockSpec((1, NSA_HPG // 2, tq, LANE), lambda b, g, c: (b, g, c, 0)),
        out_shape=jax.ShapeDtypeStruct((bsz, NSA_HEADS // 2, t, LANE), BF16),
        scratch_shapes=[pltpu.VMEM((rows, width), F32), pltpu.VMEM((rows, width), BF16)] * WIN_QB,
        compiler_params=_cparams(("parallel", "parallel", "arbitrary")),
        name="nsa_window",
    )(za, zb, kw, vw, wb_tab)


SEL_TK = 512


SEL_ROWS = 64
SEL_QB = 4
SEL_NEAR = FAR_BIAS_DIST + Q_BLOCK


def _sel_kernel(q_ref, zg_ref, k_ref, v_ref, mt_ref, lt_ref, cq_ref, o_ref, m_s, acc_s, s0_s, s1_s, p0_s, p1_s,
                a0_s, a1_s):
    rows_per_block = NSA_HPG * Q_BLOCK
    c0 = pl.program_id(2) * SEL_QB
    q = q_ref[...] * (HEAD_DIM ** -0.5)
    parts = []
    for qb in range(SEL_QB):
        qrows = slice(qb * Q_BLOCK, (qb + 1) * Q_BLOCK)
        qs = jnp.concatenate([q[qrows, h * LANE:(h + 1) * LANE] for h in range(NSA_HPG)], axis=0) + cq_ref[0]
        parts.append(jnp.concatenate([qs, jnp.concatenate([mt_ref[0, 0, qrows, :]] * NSA_HPG, axis=0)], axis=1))
    q_aug = jnp.concatenate(parts, axis=0).astype(BF16)
    m_s[...] = jnp.full(m_s.shape, NEG_INF, F32)
    acc_s[...] = jnp.zeros(acc_s.shape, F32)
    ncol = SEL_TK // LANE
    n_tiles = lax.div(c0 + SEL_QB - 1, SEL_TK // Q_BLOCK) + 1

    def key_rows(i):
        return pl.ds(pl.multiple_of(jnp.minimum(i, n_tiles - 1) * SEL_TK, SEL_TK), SEL_TK)

    def scores(i, dst):
        dst[...] = _dot_nt(q_aug, k_ref[0, 0, key_rows(i), :])

    def tile(i, cur, nxt, p_buf, a_buf):
        scores(i + 1, nxt)
        for qb in range(SEL_QB):
            off = jnp.where(i < n_tiles, jnp.maximum(SEL_NEAR + SEL_TK - ((c0 + qb) * Q_BLOCK - i * SEL_TK), 0),
                            SEL_NEAR + SEL_TK + Q_BLOCK)
            off = pl.multiple_of(off, LANE)
            for r in range(rows_per_block // SEL_ROWS):
                trows = slice(r * SEL_ROWS, (r + 1) * SEL_ROWS)
                rows = slice(qb * rows_per_block + r * SEL_ROWS, qb * rows_per_block + (r + 1) * SEL_ROWS)
                s = cur[rows, :] + lt_ref[0, trows, pl.ds(off, SEL_TK)]
                cols = [s[:, j * LANE:(j + 1) * LANE] for j in range(ncol)]
                mx = functools.reduce(jnp.maximum, cols)
                m_old = m_s[rows, :]
                m_new = jnp.maximum(m_old, jnp.max(mx, axis=-1, keepdims=True))
                m_s[rows, :] = m_new
                a_buf[rows, :] = jnp.exp(m_old - m_new)
                p_buf[rows, :] = jnp.concatenate([jnp.exp(col - m_new) for col in cols], axis=1).astype(BF16)
        acc_s[...] = a_buf[...] * acc_s[...] + _dot(p_buf[...], v_ref[0, 0, key_rows(i), :])

    scores(0, s0_s)

    def pair(j, carry):
        tile(2 * j, s0_s, s1_s, p0_s, a0_s)
        tile(2 * j + 1, s1_s, s0_s, p1_s, a1_s)
        return carry

    lax.fori_loop(0, lax.div(n_tiles + 1, 2), pair, 0)
    acc = acc_s[...]
    o = acc * (1.0 / acc[:, HEAD_DIM:HEAD_DIM + 1])
    gates = _sigmoid(zg_ref[...])
    for qb in range(SEL_QB):
        _store_heads(o_ref, o[qb * rows_per_block:(qb + 1) * rows_per_block, :],
                     gates[qb * Q_BLOCK:(qb + 1) * Q_BLOCK, :], 1, qb * Q_BLOCK)


def _sel_branch(za, zb, ks_aug, vs, mt, lt_tab, cq_tab, bsz, t):
    tq = SEL_QB * Q_BLOCK
    nq = t // tq
    nsel = mt.shape[-1]
    trows = NSA_HPG * Q_BLOCK
    rows = SEL_QB * trows
    return pl.pallas_call(
        _sel_kernel,
        grid=(bsz, NSA_GROUPS, nq),
        in_specs=[pl.BlockSpec((tq, NSA_HPG * LANE), lambda b, g, c: (b * nq + c, g)),
                  pl.BlockSpec((tq, LANE), lambda b, g, c: (b * nq + c, (C_G3 - C_SPLIT) // LANE + g)),
                  pl.BlockSpec((1, 1, t, 2 * LANE), lambda b, g, c: (b, g, 0, 0)),
                  pl.BlockSpec((1, 1, t, LANE), lambda b, g, c: (b, g, 0, 0)),
                  pl.BlockSpec((1, 1, tq, nsel), lambda b, g, c: (b, g, c, 0)),
                  pl.BlockSpec((1, trows, lt_tab.shape[2]), lambda b, g, c: (g, 0, 0)),
                  pl.BlockSpec((1, trows, LANE), lambda b, g, c: (g, 0, 0))],
        out_specs=pl.BlockSpec((1, NSA_HPG // 2, tq, LANE), lambda b, g, c: (b, g, c, 0)),
        out_shape=jax.ShapeDtypeStruct((bsz, NSA_HEADS // 2, t, LANE), BF16),
        scratch_shapes=[pltpu.VMEM((rows, LANE), F32),
                        pltpu.VMEM((rows, LANE), F32),
                        pltpu.VMEM((rows, SEL_TK), F32),
                        pltpu.VMEM((rows, SEL_TK), F32),
                        pltpu.VMEM((rows, SEL_TK), BF16),
                        pltpu.VMEM((rows, SEL_TK), BF16),
                        pltpu.VMEM((rows, LANE), F32),
                        pltpu.VMEM((rows, LANE), F32)],
        compiler_params=_cparams(("parallel", "parallel", "arbitrary")),
        name="nsa_selected",
    )(za, zb, ks_aug, vs, mt, lt_tab, cq_tab)


def _shift_mix(z_ref, prev_ref, mu, first):
    z = z_ref[...]
    prev = jnp.where(first, 0.0, prev_ref[7:8, :])
    row = lax.broadcasted_iota(jnp.int32, z.shape, 0)
    zp = jnp.where(row == 0, prev, pltpu.roll(z, 1, 0))
    return z + mu * (zp - z)


def _rwkv_prep_kernel(has_vres, tiles_per_seq, *refs):
    (zr, zk, zv, zwa, zg, pr, pk, pv, pwa, pg, mu_ref, vec_ref, w2_ref, a2_ref, g2_ref, hsum_ref) = refs[:16]
    pos = 16
    if has_vres:
        vf_ref, v0_ref, v1_ref, v2_ref = refs[pos:pos + 4]
        pos += 4
    r_o, w_o, k_o, v_o, kk_o, b_o, g_o, bonus_o, vt_o = refs[pos:]
    first = lax.rem(pl.program_id(0), tiles_per_seq) == 0
    r = _shift_mix(zr, pr, mu_ref[0:1, :], first)
    k = _shift_mix(zk, pk, mu_ref[1:2, :], first)
    v = _shift_mix(zv, pv, mu_ref[2:3, :], first)
    wa = _shift_mix(zwa, pwa, mu_ref[3:4, 0:LANE], first)
    zg_s = _shift_mix(zg, pg, mu_ref[4:5, 0:LANE], first)
    w0, a0, k_k, k_a, r_k = (vec_ref[i:i + 1, :] for i in range(5))
    if has_vres:
        lora = _dot(_dot(v.astype(BF16), v1_ref[...]).astype(BF16), v2_ref[...])
        v = v + (vf_ref[...] - v) * _sigmoid(v0_ref[...] + lora)
    u = w0 + _dot(jnp.tanh(wa).astype(BF16), w2_ref[...])
    decay = jnp.exp(-math.exp(-0.5) * _sigmoid(u))
    a = _sigmoid(a0 + _dot(wa.astype(BF16), a2_ref[...]))
    g = _dot(_sigmoid(zg_s).astype(BF16), g2_ref[...])
    kk = k * k_k
    kk = kk / jnp.maximum(jnp.sqrt(_dot_split(kk * kk, hsum_ref[...])), 1e-12)
    k = k * (1.0 + (a - 1.0) * k_a)
    r_o[...] = r
    w_o[...] = decay
    k_o[...] = k.astype(k_o.dtype)
    v_o[...] = v
    kk_o[...] = kk
    b_o[...] = kk * a
    g_o[...] = g.astype(g_o.dtype)
    bonus_o[...] = (_dot_split(r * k * r_k, hsum_ref[...]) * v).astype(bonus_o.dtype)
    vt_o[0] = v.T.astype(BF16)


def _rwkv_prep(z, mu5, vecs, w2p, a2p, g2, hsum, vres, bsz, t, tm=512):
    n = bsz * t
    nt = n // tm
    tps = t // tm
    w = RWKV_WIDTH

    def cur(width, col):
        return pl.BlockSpec((tm, width), lambda i: (i, col // width))

    def prev(width, col):
        return pl.BlockSpec((8, width), lambda i: (jnp.maximum(i * (tm // 8) - 1, 0), col // width))

    def full(shape):
        return pl.BlockSpec(shape, lambda i: (0,) * len(shape))

    cols = [(w, C_R - C_SPLIT), (w, C_K - C_SPLIT), (w, C_V - C_SPLIT), (LANE, C_WA - C_SPLIT), (LANE, C_ZG - C_SPLIT)]
    in_specs = [cur(*c) for c in cols] + [prev(*c) for c in cols]
    in_specs += [full(mu5.shape), full(vecs.shape), full(w2p.shape), full(a2p.shape), full(g2.shape), full(hsum.shape)]
    args = [z] * 10 + [mu5, vecs, w2p, a2p, g2, hsum]
    if vres is not None:
        v_first, v0, v1p, v2p = vres
        in_specs += [pl.BlockSpec((tm, w), lambda i: (i, 0)), full(v0.shape), full(v1p.shape), full(v2p.shape)]
        args += [v_first, v0, v1p, v2p]
    tok = pl.BlockSpec((tm, w), lambda i: (i, 0))
    out_specs = [tok] * 8 + [pl.BlockSpec((1, w, tm), lambda i: (i // tps, 0, i % tps))]
    tok_dtypes = (F32, F32, BF16, F32, F32, F32, BF16, BF16)
    out_shape = [jax.ShapeDtypeStruct((n, w), dt) for dt in tok_dtypes] + [jax.ShapeDtypeStruct((bsz, w, t), BF16)]
    return pl.pallas_call(
        functools.partial(_rwkv_prep_kernel, vres is not None, tps),
        grid=(nt,),
        in_specs=in_specs,
        out_specs=out_specs,
        out_shape=out_shape,
        compiler_params=_cparams(("parallel",)),
        name="rwkv_prep",
    )(*args)


QUAD = 4
QUAD_LANES = QUAD * RWKV_HEAD
SUB_STEPS = LANE // QUAD


def _scan_kernel(r_ref, w_ref, k_ref, kk_ref, b_ref, vt_ref, ones_ref, place_ref, o_ref, s_ref, vq_ref, kq_ref, vk0_ref,
                 vk1_ref):
    bsz = r_ref.shape[0]
    tt = r_ref.shape[1]
    nh = RWKV_HEAD
    quads = [(bi, hf) for bi in range(bsz) for hf in range(RWKV_HEADS // QUAD)]

    @pl.when(pl.program_id(0) == 0)
    def _():
        s_ref[...] = jnp.zeros(s_ref.shape, F32)

    sub_shift, head_shift = int(math.log2(SUB_STEPS)), int(math.log2(nh))
    sub_lane = lax.broadcasted_iota(jnp.int32, (nh, LANE), 1) & (SUB_STEPS - 1)
    k_row_head = jnp.right_shift(lax.broadcasted_iota(jnp.int32, (LANE, QUAD_LANES), 0), sub_shift)
    k_lane_head = jnp.right_shift(lax.broadcasted_iota(jnp.int32, (LANE, QUAD_LANES), 1), head_shift)
    r_mask = (lax.broadcasted_iota(jnp.int32, (16, QUAD_LANES), 0)
              == jnp.right_shift(lax.broadcasted_iota(jnp.int32, (16, QUAD_LANES), 1), head_shift))
    ones_bd = ones_ref[...]

    def sub_block(sb, carry):
        s0 = pl.multiple_of(sb * SUB_STEPS, SUB_STEPS)
        for q, (bi, hf) in enumerate(quads):
            ln = slice(hf * QUAD_LANES, (hf + 1) * QUAD_LANES)
            vq = _dot(vt_ref[bi, (QUAD * hf) * nh:(QUAD * hf + 1) * nh, :], place_ref[sb * QUAD])
            for h in range(1, QUAD):
                vq = vq + _dot(vt_ref[bi, (QUAD * hf + h) * nh:(QUAD * hf + h + 1) * nh, :], place_ref[sb * QUAD + h])
            vq_ref[q] = vq.astype(BF16)
            k_sub = k_ref[bi, pl.ds(s0, SUB_STEPS), ln].astype(F32)
            kq_ref[q] = jnp.where(k_row_head == k_lane_head, jnp.concatenate([k_sub] * QUAD, axis=0), 0.0).astype(BF16)

        def group(tg, vk_ref, carry):
            t0 = pl.multiple_of(s0 + tg * 8, 8)
            rows = []
            for bi, hf in quads:
                ln = slice(hf * QUAD_LANES, (hf + 1) * QUAD_LANES)
                rows.append(tuple(ref[bi, pl.ds(t0, 8), ln] for ref in (w_ref, kk_ref, b_ref, r_ref)))
            onehots = [jnp.where(sub_lane == tg * 8 + u, 1.0, 0.0).astype(BF16) for u in range(8)]
            for q in range(len(quads)):
                vq = vq_ref[q]
                vk_ref[q] = _dot(jnp.concatenate([vq * oh for oh in onehots], axis=0), kq_ref[q])
            state = [s_ref[q] for q in range(len(quads))]
            for u in range(8):
                sk = _dot(jnp.concatenate([(state[q] * rows[q][1][u:u + 1, :]).astype(BF16)
                                           for q in range(len(quads))], axis=0), ones_bd)
                for q, (bi, hf) in enumerate(quads):
                    w8, _, b8, r8 = rows[q]
                    state[q] = (state[q] * w8[u:u + 1, :] - sk[q * nh:(q + 1) * nh, :] * b8[u:u + 1, :]
                                + vk_ref[q, u * nh:(u + 1) * nh, :])
                    r_lhs = jnp.where(r_mask, r8[u:u + 1, :], 0.0).astype(BF16)
                    o = _dot_nt(r_lhs, state[q].astype(BF16))
                    o_ref[t0 + u, bi, QUAD * hf:QUAD * (hf + 1), :] = o[0:QUAD, :]
            for q in range(len(quads)):
                s_ref[q] = state[q]
            return carry

        for tg in range(SUB_STEPS // 8):
            carry = group(tg, (vk0_ref, vk1_ref)[tg % 2], carry)
        return carry

    lax.fori_loop(0, tt // SUB_STEPS, sub_block, 0)


def _rwkv_scan(r, w, k, kk, b, vt, ones_bd, place, bsz, t):
    tt = SCAN_TT
    wd = RWKV_WIDTH
    nquad = bsz * RWKV_HEADS // QUAD
    tok = pl.BlockSpec((bsz, tt, wd), lambda i: (0, i, 0))
    return pl.pallas_call(
        _scan_kernel,
        grid=(t // tt,),
        in_specs=[tok, tok, tok, tok, tok,
                  pl.BlockSpec((bsz, wd, tt), lambda i: (0, 0, i)),
                  pl.BlockSpec(ones_bd.shape, lambda i: (0, 0)),
                  pl.BlockSpec(place.shape, lambda i: (0, 0, 0))],
        out_specs=pl.BlockSpec((tt, bsz, RWKV_HEADS, RWKV_HEAD), lambda i: (i, 0, 0, 0)),
        out_shape=jax.ShapeDtypeStruct((t, bsz, RWKV_HEADS, RWKV_HEAD), F32),
        scratch_shapes=[pltpu.VMEM((nquad, RWKV_HEAD, QUAD_LANES), F32),
                        pltpu.VMEM((nquad, RWKV_HEAD, LANE), BF16),
                        pltpu.VMEM((nquad, LANE, QUAD_LANES), BF16),
                        pltpu.VMEM((nquad, 8 * RWKV_HEAD, QUAD_LANES), F32),
                        pltpu.VMEM((nquad, 8 * RWKV_HEAD, QUAD_LANES), F32)],
        compiler_params=_cparams(("arbitrary",)),
        name="rwkv_scan",
    )(r, w, k, kk, b, vt, ones_bd, place)


def _merge_kernel(x_ref, oc_ref, os_ref, ow_ref, orw_ref, bonus_ref, g_ref, zgn_ref, zgr_ref, gn_ref, hsum_ref,
                  pn_ref, pr_ref, wo_ref, ln_ref, o_ref):
    o_n = jnp.concatenate([oc_ref[0, p].astype(F32) + os_ref[0, p].astype(F32) + ow_ref[0, p].astype(F32)
                           for p in range(NSA_HEADS // 2)], axis=1)
    y_n = _dot(o_n.astype(BF16), pn_ref[...])
    o = orw_ref[...]
    mu = _dot_split(o, hsum_ref[...]) * (1.0 / RWKV_HEAD)
    d = o - mu
    var = _dot_split(d * d, hsum_ref[...]) * (1.0 / RWKV_HEAD)
    o = d * lax.rsqrt(var + GN_EPS) * gn_ref[0:1, :] + gn_ref[1:2, :]
    o = ((o + bonus_ref[...]) * g_ref[...]).astype(BF16)
    y1 = _sigmoid(zgn_ref[...].astype(F32)) * y_n + _sigmoid(zgr_ref[...].astype(F32)) * _dot(o, pr_ref[...])
    y = _dot(y1.astype(BF16), wo_ref[...])
    o_ref[...] = _layer_norm(DN_ALPHA * x_ref[...] + y, ln_ref[0:1, :], ln_ref[1:2, :])


def _merge(x, o_c, o_s, o_w, o_rw, bonus, g, z, gn, hsum, p_nsa, p_rwkv, w_out, ln, bsz, t, tm=512):
    n = bsz * t
    tps = t // tm
    d = D_MODEL
    w = RWKV_WIDTH
    head = pl.BlockSpec((1, NSA_HEADS // 2, tm, LANE), lambda i: (i // tps, 0, i % tps, 0))
    tok = pl.BlockSpec((tm, w), lambda i: (i, 0))

    def full(a):
        return pl.BlockSpec(a.shape, lambda i: (0,) * a.ndim)

    return pl.pallas_call(
        _merge_kernel,
        grid=(n // tm,),
        in_specs=[pl.BlockSpec((tm, d), lambda i: (i, 0)), head, head, head, tok, tok, tok,
                  pl.BlockSpec((tm, d), lambda i: (i, C_GN // d)),
                  pl.BlockSpec((tm, d), lambda i: (i, C_GR // d)),
                  full(gn), full(hsum), full(p_nsa), full(p_rwkv), full(w_out), full(ln)],
        out_specs=pl.BlockSpec((tm, d), lambda i: (i, 0)),
        out_shape=jax.ShapeDtypeStruct((n, d), F32),
        compiler_params=_cparams(("parallel",)),
        name="mix_merge_ln",
    )(x, o_c, o_s, o_w, o_rw, bonus, g, z, z, gn, hsum, p_nsa, p_rwkv, w_out, ln)


def _xattn_kernel(x_ref, kt_ref, v_ref, wq_ref, wo_ref, ln_ref, rw_ref, o_ref, ob_ref, s_ref):
    x = x_ref[...]
    q = _dot(x.astype(BF16), wq_ref[...]).astype(BF16)
    outs = []
    for h in range(XATTN_HEADS):
        hs = slice(h * XATTN_HEAD, (h + 1) * XATTN_HEAD)
        s = _dot(q[:, hs], kt_ref[0, hs, :]) * (XATTN_HEAD ** -0.5)
        m = jnp.max(s, axis=-1, keepdims=True)
        p = jnp.exp(s - m)
        l = jnp.sum(p, axis=-1, keepdims=True)
        outs.append(_dot(p.astype(BF16), v_ref[0, :, hs]) * (1.0 / l))
    o = jnp.concatenate(outs, axis=1).astype(BF16)
    x2 = _layer_norm(DN_ALPHA * x + _dot(o, wo_ref[...]), ln_ref[0:1, :], ln_ref[1:2, :])
    o_ref[...] = x2
    hi = x2.astype(BF16)
    ob_ref[...] = hi
    lo = (x2 - hi.astype(F32)).astype(BF16)
    logits = _dot(hi, rw_ref[0]) + (_dot(lo, rw_ref[0]) + _dot(hi, rw_ref[1]))
    s_ref[...] = _sigmoid(logits)


def _xattn(x, kt, v, wq, wo, ln, rw2, bsz, t, tm=256):
    n = bsz * t
    tps = t // tm
    d = D_MODEL

    def full(a):
        return pl.BlockSpec(a.shape, lambda i: (0,) * a.ndim)

    return pl.pallas_call(
        _xattn_kernel,
        grid=(n // tm,),
        in_specs=[pl.BlockSpec((tm, d), lambda i: (i, 0)),
                  pl.BlockSpec((1,) + kt.shape[1:], lambda i: (i // tps, 0, 0)),
                  pl.BlockSpec((1,) + v.shape[1:], lambda i: (i // tps, 0, 0)),
                  full(wq), full(wo), full(ln), full(rw2)],
        out_specs=[pl.BlockSpec((tm, d), lambda i: (i, 0)), pl.BlockSpec((tm, d), lambda i: (i, 0)),
                   pl.BlockSpec((tm, LANE), lambda i: (i, 0))],
        out_shape=[jax.ShapeDtypeStruct((n, d), F32), jax.ShapeDtypeStruct((n, d), BF16),
                   jax.ShapeDtypeStruct((n, LANE), F32)],
        compiler_params=_cparams(("parallel",)),
        name="xattn_ln_router",
    )(x, kt, v, wq, wo, ln, rw2)


def _expert_kernel(meta_ref, x_ref, gate_ref, *refs):
    w_refs, o_ref, w_scr = refs[:6], refs[6], refs[7:]
    i = pl.program_id(0)
    nb = pl.num_programs(0)
    prev = jnp.maximum(i - 1, 0)

    for side in range(2):
        @pl.when((i == 0) | (meta_ref[side * nb + i] != meta_ref[side * nb + prev]))
        def _():
            for j in range(3):
                w_scr[3 * side + j][...] = w_refs[3 * side + j][0, 0].astype(BF16)

    n_used = meta_ref[2 * nb]

    @pl.when(i < n_used)
    def _():
        x = x_ref[...]
        y = jnp.zeros(o_ref.shape, F32)
        for side in range(2):
            wg, wu, wd = w_scr[3 * side:3 * side + 3]
            hg = _dot(x, wg[...])
            h = hg * _sigmoid(hg) * _dot(x, wu[...])
            y = y + gate_ref[:, side:side + 1] * _dot(h.astype(BF16), wd[...])
        o_ref[...] = y.astype(o_ref.dtype)

    @pl.when(i >= n_used)
    def _():
        o_ref[...] = jnp.zeros(o_ref.shape, o_ref.dtype)


def _experts(meta, rows, gates, w_gate, w_up, w_down, layer):
    n_rows, d = rows.shape
    ff = w_gate.shape[3]
    nb = n_rows // MOE_BLK

    def weight(shape, side):
        return pl.BlockSpec((1, 1) + shape, lambda i, m: (layer, m[side * nb + i], 0, 0))

    return pl.pallas_call(
        _expert_kernel,
        grid_spec=pltpu.PrefetchScalarGridSpec(
            num_scalar_prefetch=1,
            grid=(nb,),
            in_specs=[pl.BlockSpec((MOE_BLK, d), lambda i, m: (i, 0)),
                      pl.BlockSpec((MOE_BLK, TOP_K), lambda i, m: (i, 0))]
                     + [weight(s, side) for side in range(2) for s in ((d, ff), (d, ff), (ff, d))],
            out_specs=pl.BlockSpec((MOE_BLK, d), lambda i, m: (i, 0)),
            scratch_shapes=[pltpu.VMEM(s, BF16) for _ in range(2) for s in ((d, ff), (d, ff), (ff, d))]),
        out_shape=jax.ShapeDtypeStruct((n_rows, d), BF16),
        compiler_params=_cparams(("arbitrary",)),
        name="moe_experts",
    )(meta, rows, gates, w_gate, w_up, w_down, w_gate, w_up, w_down)


def _combine_kernel(x_ref, y_ref, ln_ref, o_ref):
    o_ref[...] = _layer_norm(DN_ALPHA * x_ref[...] + y_ref[...], ln_ref[0:1, :], ln_ref[1:2, :])


def _combine(x, y, ln, tm=512):
    n, d = x.shape
    tok = pl.BlockSpec((tm, d), lambda i: (i, 0))
    return pl.pallas_call(
        _combine_kernel,
        grid=(n // tm,),
        in_specs=[tok, tok, pl.BlockSpec(ln.shape, lambda i: (0, 0))],
        out_specs=tok,
        out_shape=jax.ShapeDtypeStruct((n, d), F32),
        compiler_params=_cparams(("parallel",)),
        name="moe_combine_ln",
    )(x, y, ln)


def _t5_bucket(dist):
    n = jnp.maximum(dist, 0)
    max_exact = REL_BUCKETS // 2
    log_ratio = jnp.log(jnp.maximum(n, 1).astype(F32) / max_exact) / math.log(REL_MAX_DIST / max_exact)
    large = jnp.minimum(max_exact + (log_ratio * (REL_BUCKETS - max_exact)).astype(jnp.int32), REL_BUCKETS - 1)
    return jnp.where(n < max_exact, n, large)


def _bias_tables(rel_table, t):
    nc = t // CMP_STRIDE
    n_dist = t + 2048
    onehot = (_t5_bucket(jnp.arange(n_dist))[:, None] == jnp.arange(REL_BUCKETS)[None, :]).astype(F32)
    f = jnp.dot(onehot, rel_table.astype(F32), precision=lax.Precision.HIGHEST).T
    pad = n_dist
    f_ext = jnp.concatenate([jnp.full((NSA_HEADS, pad), NEG_INF, F32), f], axis=1)

    def at(first, step, count):
        last = first + step * (count - 1)
        assert step < 0 and last + pad >= 0 and first < n_dist
        return lax.slice(f_ext, (0, last + pad), (NSA_HEADS, first + pad + 1), (1, -step))[:, ::-1]

    chunks_per_q = Q_BLOCK // CMP_STRIDE
    n_m = 2 * nc + chunks_per_q - 1
    small = jnp.stack([at(CMP_STRIDE * (nc - 8 + chunks_per_q - 1) + r - (CMP_LEN - 1), -CMP_STRIDE, n_m)
                       for r in range(CMP_STRIDE)], axis=1)
    pc = jnp.stack([small[:, :, chunks_per_q - 1 - a:chunks_per_q - 1 - a + 2 * nc] for a in range(chunks_per_q)], axis=1)
    pc = pc.reshape(NSA_GROUPS, NSA_HPG * Q_BLOCK, 2 * nc)
    pc = jnp.stack([pc[:, :, s:s + 2 * nc - LANE] for s in range(0, LANE, 8)], axis=1)
    nvar = WINDOW // Q_BLOCK + 1

    def toeplitz(base, width, hi, shift):
        period = width + Q_BLOCK
        row = jnp.concatenate([at(base, -1, width), at(base + Q_BLOCK, -1, Q_BLOCK)], axis=1) - shift
        m = np.arange(period)
        dist = base - np.where(m < width, m, m - period)
        row = jnp.where((dist < hi)[None], row, NEG_INF)
        flat = jnp.tile(row, (1, Q_BLOCK))[:, :Q_BLOCK * (period - 1)]
        return flat.reshape(NSA_HEADS, Q_BLOCK, period - 1)[:, :, :width]

    wb = jnp.stack([toeplitz(Q_BLOCK * v, WINDOW + Q_BLOCK, WINDOW, 0.0) for v in range(nvar)], axis=1)
    wb = wb.reshape(NSA_GROUPS, NSA_HPG, nvar, Q_BLOCK, -1).transpose(0, 2, 1, 3, 4)
    wb = wb.reshape(NSA_GROUPS, nvar, NSA_HPG * Q_BLOCK, -1)
    far = rel_table[REL_BUCKETS - 1].astype(F32)
    lt = toeplitz(SEL_NEAR + SEL_TK, SEL_NEAR + 2 * SEL_TK + Q_BLOCK, n_dist, far[:, None])
    lt = lt.reshape(NSA_GROUPS, NSA_HPG * Q_BLOCK, -1)
    far_hi = far.astype(BF16).astype(F32)
    lane = jnp.arange(LANE)[None, :]
    cq = jnp.where(lane == HEAD_DIM, far_hi[:, None], jnp.where(lane == HEAD_DIM + 1, (far - far_hi)[:, None], 0.0))
    cq = jnp.broadcast_to(cq[:, None, :], (NSA_HEADS, Q_BLOCK, LANE)).reshape(NSA_GROUPS, NSA_HPG * Q_BLOCK, LANE)
    return pc, wb, lt, cq


def _in_proj_perm():
    perm = np.full((IN_PAD,), -1, np.int64)
    for h in range(NSA_HEADS):
        perm[C_Q + h * LANE:C_Q + h * LANE + HEAD_DIM] = np.arange(HEAD_DIM) + h * HEAD_DIM
    rw0 = NSA_COLS
    g0 = NSA_COLS + RWKV_COLS
    perm[C_GN:C_GN + D_MODEL] = g0 + np.arange(D_MODEL)
    perm[C_GR:C_GR + D_MODEL] = g0 + D_MODEL + np.arange(D_MODEL)
    perm[C_R:C_R + RWKV_WIDTH] = rw0 + np.arange(RWKV_WIDTH)
    perm[C_K:C_K + RWKV_WIDTH] = rw0 + RWKV_WIDTH + np.arange(RWKV_WIDTH)
    perm[C_V:C_V + RWKV_WIDTH] = rw0 + 2 * RWKV_WIDTH + np.arange(RWKV_WIDTH)
    for g in range(NSA_GROUPS):
        for br in range(3):
            for h in range(NSA_HPG):
                perm[C_G3 + g * LANE + br * NSA_HPG + h] = NSA_WIDTH + 6 * NSA_KV + (g * NSA_HPG + h) * 3 + br
    perm[C_WA:C_WA + LORA_W + LORA_A] = rw0 + 3 * RWKV_WIDTH + np.arange(LORA_W + LORA_A)
    perm[C_ZG:C_ZG + LORA_G] = rw0 + 3 * RWKV_WIDTH + LORA_W + LORA_A + np.arange(LORA_G)
    return perm


_PERM = _in_proj_perm()


def _permute_in_proj(w):
    pieces = []
    start = 0
    while start < IN_PAD:
        stop = start + 1
        if _PERM[start] < 0:
            while stop < IN_PAD and _PERM[stop] < 0:
                stop += 1
            pieces.append(jnp.zeros((w.shape[0], stop - start), BF16))
        else:
            while stop < IN_PAD and _PERM[stop] == _PERM[stop - 1] + 1:
                stop += 1
            pieces.append(w[:, int(_PERM[start]):int(_PERM[stop - 1]) + 1].astype(BF16))
        start = stop
    return jnp.concatenate(pieces, axis=1)


def _pad_to(a, shape):
    return jnp.pad(a, [(0, s - d) for d, s in zip(a.shape, shape)])


_CLASS_EXPERTS = np.asarray([(grp * EXPERTS_PER_GROUP + i, grp * EXPERTS_PER_GROUP + j)
                             for grp in range(N_EXPERT_GROUPS)
                             for i in range(EXPERTS_PER_GROUP) for j in range(i + 1, EXPERTS_PER_GROUP)], np.int32)
N_PAIR_CLASSES = len(_CLASS_EXPERTS)


def _route(s, router_bias, n_tok):
    s16 = s[:, :N_EXPERTS]
    s_sel = s16 + router_bias.astype(F32)

    def top2(a):
        i1 = jnp.argmax(a, axis=-1)
        rest = jnp.where(jnp.arange(a.shape[-1]) == i1[..., None], -jnp.inf, a)
        i2 = jnp.argmax(rest, axis=-1)
        return jnp.max(a, axis=-1) + jnp.max(rest, axis=-1), jnp.stack([i1, i2], axis=-1)

    group_score, _ = top2(s_sel.reshape(n_tok, N_EXPERT_GROUPS, EXPERTS_PER_GROUP))
    group = jnp.argmax(group_score, axis=-1)
    in_group = (jnp.arange(N_EXPERTS) // EXPERTS_PER_GROUP)[None, :] == group[:, None]
    _, e_idx = top2(jnp.where(in_group, s_sel, NEG_INF))
    picked = jnp.arange(N_EXPERTS)[None, None, :] == e_idx[:, :, None]
    s_top = jnp.sum(jnp.where(picked, s16[:, None, :], 0.0), axis=-1)
    gate = s_top / jnp.sum(s_top, axis=-1, keepdims=True)
    swap = e_idx[:, 0] > e_idx[:, 1]
    e_lo = jnp.where(swap, e_idx[:, 1], e_idx[:, 0])
    e_hi = jnp.where(swap, e_idx[:, 0], e_idx[:, 1])
    gate_lohi = jnp.where(swap[:, None], gate[:, ::-1], gate)
    l_lo, l_hi = e_lo % EXPERTS_PER_GROUP, e_hi % EXPERTS_PER_GROUP
    pairs_per_group = N_PAIR_CLASSES // N_EXPERT_GROUPS
    cls = ((e_lo // EXPERTS_PER_GROUP) * pairs_per_group
           + (l_lo * (2 * EXPERTS_PER_GROUP - 1 - l_lo)) // 2 + (l_hi - l_lo - 1))
    onehot = (cls[:, None] == jnp.arange(N_PAIR_CLASSES)[None, :]).astype(jnp.int32)
    counts = jnp.sum(onehot, axis=0)
    padded = (counts + MOE_BLK - 1) // MOE_BLK * MOE_BLK
    ends = jnp.cumsum(padded)
    before = jnp.cumsum(onehot, axis=0) - onehot + (ends - padded)[None, :]
    dest = jnp.sum(onehot * before, axis=1)
    n_rows = n_tok + N_PAIR_CLASSES * MOE_BLK
    tok_of_row = jnp.zeros((n_rows,), jnp.int32).at[dest].set(jnp.arange(n_tok, dtype=jnp.int32))
    block_start = jnp.arange(n_rows // MOE_BLK) * MOE_BLK
    block_cls = jnp.minimum(jnp.sum((ends[None, :] <= block_start[:, None]).astype(jnp.int32), axis=1),
                            N_PAIR_CLASSES - 1)
    block_onehot = (block_cls[:, None] == jnp.arange(N_PAIR_CLASSES)[None, :]).astype(jnp.int32)
    block_experts = jnp.sum(block_onehot[:, :, None] * jnp.asarray(_CLASS_EXPERTS)[None], axis=1)
    meta = jnp.concatenate([block_experts[:, 0], block_experts[:, 1], ends[-1:] // MOE_BLK]).astype(jnp.int32)
    return gate_lohi, dest, tok_of_row, meta


def kernel(x, mem, rel_table, router_w, router_bias, w_in, cmp_pe_k, cmp_pe_v, cmp_w1k, cmp_w2k, cmp_w1v, cmp_w2v, rwkv_mu, rwkv_w0, rwkv_w2, rwkv_a0, rwkv_a2, rwkv_g2, rwkv_kk, rwkv_ka, rwkv_rk, rwkv_gn_g, rwkv_gn_b, rwkv_v0, rwkv_v1, rwkv_v2, p_nsa, p_rwkv, w_out, ln1_g, ln1_b, xq_w, xk_w, xv_w, xo_w, ln2_g, ln2_b, moe_w_gate, moe_w_up, moe_w_down, ln3_g, ln3_b):
    bsz, t, d = x.shape
    n = bsz * t
    depth = w_in.shape[0]
    nc = t // CMP_STRIDE
    nsel = t // SEL_BLOCK
    g, hd = NSA_GROUPS, HEAD_DIM

    pc_tab, wb_tab, lt_tab, cq_tab = _bias_tables(rel_table, t)
    cmp_start = np.arange(nc) * CMP_STRIDE
    sel_start = np.arange(nsel) * SEL_BLOCK
    overlap = ((cmp_start[:, None] <= sel_start[None, :] + SEL_BLOCK - 1)
               & (cmp_start[:, None] + CMP_LEN - 1 >= sel_start[None, :]) & (cmp_start[:, None] < (nc - 1) * CMP_STRIDE))
    overlap = jnp.asarray(overlap.T, BF16)
    lane_head = np.arange(RWKV_WIDTH) // RWKV_HEAD
    hsum = jnp.asarray(lane_head[:, None] == lane_head[None, :], BF16)
    quad_head = np.arange(QUAD_LANES) // RWKV_HEAD
    ones_bd = jnp.asarray(quad_head[:, None] == quad_head[None, :], BF16)
    step = np.arange(SCAN_TT)[None, :, None]
    lane = np.arange(LANE)[None, None, :]
    sbh = np.arange(SCAN_TT // SUB_STEPS * QUAD)[:, None, None]
    place = jnp.asarray((lane // SUB_STEPS == sbh % QUAD) & (step == (sbh // QUAD) * SUB_STEPS + lane % SUB_STEPS), BF16)
    rw_f = _pad_to(router_w.astype(F32), (d, LANE))
    rw_hi = rw_f.astype(BF16)
    rw2 = jnp.stack([rw_hi, (rw_f - rw_hi.astype(F32)).astype(BF16)])

    xf = x.reshape(n, d)
    mem_f = mem.reshape(bsz * mem.shape[1], d)
    v_first = None
    for l in range(depth):
        w_in_l = _permute_in_proj(w_in[l])
        za, zb = _in_proj(xf, w_in_l)

        kv0 = NSA_WIDTH
        zeros64 = jnp.zeros((d, LANE - hd), BF16)
        w_kv = jnp.stack([jnp.concatenate(
            [piece for i in range(KV_ARRAYS)
             for piece in (w_in[l][:, kv0 + i * NSA_KV + gi * hd:kv0 + i * NSA_KV + (gi + 1) * hd].astype(BF16), zeros64)],
            axis=1) for gi in range(g)])
        kc, vc, ks_aug, vs_aug, kw_pad, vw_aug = _kv_proj(xf, w_kv, bsz, t)

        def cmp_w(pe, w1, w2):
            return (pe.reshape(2, CMP_STRIDE * hd), w1.reshape(2, CMP_STRIDE * hd, CMP_HIDDEN).astype(BF16),
                    _pad_to(w2, (CMP_HIDDEN, LANE)).astype(BF16))

        chunks = lambda a: a.reshape(bsz, g, nc, CMP_STRIDE * hd)
        k_cmp = _compress(chunks(kc), *cmp_w(cmp_pe_k[l], cmp_w1k[l], cmp_w2k[l]))
        v_cmp = _compress(chunks(vc), *cmp_w(cmp_pe_v[l], cmp_w1v[l], cmp_w2v[l]))
        o_c, sel_mask = _cmp_branch(za, zb, k_cmp, v_cmp, pc_tab, overlap, bsz, t)
        o_s = _sel_branch(za, zb, ks_aug, vs_aug, sel_mask, lt_tab, cq_tab, bsz, t)
        o_w = _win_branch(za, zb, kw_pad, vw_aug, wb_tab, bsz, t)

        mu = rwkv_mu[l]
        w3 = 3 * RWKV_WIDTH
        mu5 = jnp.stack([mu[0:512], mu[512:1024], mu[1024:1536], _pad_to(mu[w3:w3 + 128], (512,)),
                         _pad_to(mu[w3 + 128:w3 + 256], (512,))])
        vecs = jnp.stack([rwkv_w0[l], rwkv_a0[l], rwkv_kk[l], rwkv_ka[l], rwkv_rk[l].reshape(-1)])
        w2p = jnp.concatenate([rwkv_w2[l], jnp.zeros_like(rwkv_a2[l])], axis=0).astype(BF16)
        a2p = jnp.concatenate([jnp.zeros_like(rwkv_w2[l]), rwkv_a2[l]], axis=0).astype(BF16)
        vres = None
        if l > 0:
            vres = (v_first, rwkv_v0[l - 1][None, :], _pad_to(rwkv_v1[l - 1], (RWKV_WIDTH, LANE)).astype(BF16),
                    _pad_to(rwkv_v2[l - 1], (LANE, RWKV_WIDTH)).astype(BF16))
        r, w, k, v, kk, b, g_out, bonus, vt = _rwkv_prep(zb, mu5, vecs, w2p, a2p, rwkv_g2[l].astype(BF16), hsum, vres,
                                                          bsz, t)
        if l == 0:
            v_first = v
        seq = lambda a: a.reshape(bsz, t, RWKV_WIDTH)
        o_rw = _rwkv_scan(seq(r), seq(w), seq(k), seq(kk), seq(b), vt, ones_bd, place, bsz, t)
        o_rw = o_rw.transpose(1, 0, 2, 3).reshape(n, RWKV_WIDTH)

        xf = _merge(xf, o_c, o_s, o_w, o_rw, bonus, g_out, za, jnp.stack([rwkv_gn_g[l], rwkv_gn_b[l]]), hsum,
                    p_nsa[l].astype(BF16), p_rwkv[l].astype(BF16), w_out[l].astype(BF16),
                    jnp.stack([ln1_g[l], ln1_b[l]]), bsz, t)

        mlen = mem.shape[1]
        mk = _matmul(mem_f, xk_w[l].astype(BF16), BF16, mlen, XATTN_WIDTH)
        mv = _matmul(mem_f, xv_w[l].astype(BF16), BF16, mlen, XATTN_WIDTH)
        kt = mk.reshape(bsz, mlen, XATTN_WIDTH).transpose(0, 2, 1)
        xf, xf_b, scores = _xattn(xf, kt, mv.reshape(bsz, mlen, XATTN_WIDTH), xq_w[l].astype(BF16), xo_w[l].astype(BF16),
                            jnp.stack([ln2_g[l], ln2_b[l]]), rw2, bsz, t)

        gate, dest, tok_of_row, meta = _route(scores, router_bias, n)
        rows = jnp.take(xf_b, tok_of_row, axis=0)
        y_rows = _experts(meta, rows, jnp.take(gate, tok_of_row, axis=0), moe_w_gate, moe_w_up, moe_w_down, l)
        xf = _combine(xf, jnp.take(y_rows, dest, axis=0), jnp.stack([ln3_g[l], ln3_b[l]]))
    return xf.reshape(bsz, t, d)
```

```python
import functools
import math

import numpy as np
import jax
import jax.numpy as jnp
from jax import lax
from jax.experimental import pallas as pl
from jax.experimental.pallas import tpu as pltpu

F32 = jnp.float32
BF16 = jnp.bfloat16

D_MODEL = 1024
DEPTH = 2
NSA_HEADS = 8
NSA_GROUPS = 2
NSA_HPG = NSA_HEADS // NSA_GROUPS
HEAD_DIM = 64
NSA_WIDTH = NSA_HEADS * HEAD_DIM
NSA_KV = NSA_GROUPS * HEAD_DIM
CMP_STRIDE = 16
CMP_LEN = 2 * CMP_STRIDE
CMP_HIDDEN = 256
SEL_BLOCK = 64
SEL_TOPN = 16
WINDOW = 512
Q_BLOCK = 128
SEL_FORCE = 1e9
RWKV_HEADS = 8
RWKV_HEAD = 64
RWKV_WIDTH = RWKV_HEADS * RWKV_HEAD
LORA_W = 64
LORA_A = 64
LORA_V = 32
LORA_G = 128
GN_EPS = 64e-5
REL_BUCKETS = 32
REL_MAX_DIST = 1024
XATTN_HEADS = 4
XATTN_HEAD = 128
XATTN_WIDTH = XATTN_HEADS * XATTN_HEAD
N_EXPERTS = 16
N_EXPERT_GROUPS = 4
EXPERTS_PER_GROUP = N_EXPERTS // N_EXPERT_GROUPS
TOP_K = 2
EXPERT_FF = 512
DN_ALPHA = (2 * DEPTH) ** 0.25
LN_EPS = 1e-5
NEG_INF = -1e30
NSA_COLS = NSA_WIDTH + 6 * NSA_KV + 3 * NSA_HEADS
RWKV_COLS = 3 * RWKV_WIDTH + LORA_W + LORA_A + LORA_G

LANE = 128
VMEM_LIMIT = 56 * 1024 * 1024

C_Q = 0
C_GN = 1024
C_GR = 2048
C_R = 3072
C_K = 3584
C_V = 4096
C_G3 = 4608
C_WA = 4864
C_ZG = 4992
IN_PAD = 5120
C_SPLIT = C_R
KV_ARRAYS = 6

MOE_BLK = 256
SCAN_TT = 128
FAR_BIAS_DIST = 1280


def _cparams(sem):
    return pltpu.CompilerParams(dimension_semantics=sem, vmem_limit_bytes=VMEM_LIMIT)


def _sigmoid(x):
    return 1.0 / (1.0 + jnp.exp(-x))


def _layer_norm(v, g, b):
    mu = jnp.mean(v, axis=-1, keepdims=True)
    d = v - mu
    var = jnp.mean(d * d, axis=-1, keepdims=True)
    return d * lax.rsqrt(var + LN_EPS) * g + b


def _dot(a, b):
    return jnp.dot(a, b, preferred_element_type=F32)


def _dot_nt(a, b):
    return lax.dot_general(a, b, (((1,), (1,)), ((), ())), preferred_element_type=F32)


def _dot_split(x, w):
    hi = x.astype(BF16)
    lo = (x - hi.astype(F32)).astype(BF16)
    return _dot(hi, w) + _dot(lo, w)


def _mm_kernel(x_ref, w_ref, o_ref, xb_ref):
    @pl.when(pl.program_id(1) == 0)
    def _():
        xb_ref[...] = x_ref[...].astype(BF16)

    o_ref[...] = _dot(xb_ref[...], w_ref[...]).astype(o_ref.dtype)


def _matmul(x, w, out_dtype, tm, tn):
    n, k = x.shape
    m = w.shape[1]
    return pl.pallas_call(
        _mm_kernel,
        grid=(n // tm, m // tn),
        in_specs=[pl.BlockSpec((tm, k), lambda i, j: (i, 0)),
                  pl.BlockSpec((k, tn), lambda i, j: (0, j))],
        out_specs=pl.BlockSpec((tm, tn), lambda i, j: (i, j)),
        out_shape=jax.ShapeDtypeStruct((n, m), out_dtype),
        scratch_shapes=[pltpu.VMEM((tm, k), BF16)],
        compiler_params=_cparams(("parallel", "arbitrary")),
        name="matmul",
    )(x, w)


IN_TN = 1024


def _in_proj_kernel(x_ref, w_ref, oa_ref, ob_ref, xb_ref):
    j = pl.program_id(1)

    @pl.when(j == 0)
    def _():
        xb_ref[...] = x_ref[...].astype(BF16)

    @pl.when(j < C_SPLIT // IN_TN)
    def _():
        oa_ref[...] = _dot(xb_ref[...], w_ref[...]).astype(oa_ref.dtype)

    @pl.when(j >= C_SPLIT // IN_TN)
    def _():
        ob_ref[...] = _dot(xb_ref[...], w_ref[...])


def _in_proj(x, w, tm=1024):
    n, k = x.shape
    na = C_SPLIT // IN_TN
    return pl.pallas_call(
        _in_proj_kernel,
        grid=(n // tm, IN_PAD // IN_TN),
        in_specs=[pl.BlockSpec((tm, k), lambda i, j: (i, 0)),
                  pl.BlockSpec((k, IN_TN), lambda i, j: (0, j))],
        out_specs=[pl.BlockSpec((tm, IN_TN), lambda i, j: (i, jnp.minimum(j, na - 1))),
                   pl.BlockSpec((tm, IN_TN), lambda i, j: (i, jnp.maximum(j - na, 0)))],
        out_shape=[jax.ShapeDtypeStruct((n, C_SPLIT), BF16), jax.ShapeDtypeStruct((n, IN_PAD - C_SPLIT), F32)],
        scratch_shapes=[pltpu.VMEM((tm, k), BF16)],
        compiler_params=_cparams(("parallel", "arbitrary")),
        name="in_proj",
    )(x, w)


def _kv_proj_kernel(tiles_per_seq, x_ref, w_ref, kc_ref, vc_ref, ks_ref, vs_ref, kw_ref, vw_ref, xb_ref):
    @pl.when(pl.program_id(1) == 0)
    def _():
        xb_ref[...] = x_ref[...].astype(BF16)

    y = _dot(xb_ref[...], w_ref[pl.program_id(1)])
    tm = y.shape[0]
    slot = lambda i: y[:, i * LANE:(i + 1) * LANE]
    lane = lax.broadcasted_iota(jnp.int32, (tm, LANE), 1)
    one_lane = jnp.where(lane == HEAD_DIM, 1.0, 0.0)
    t0 = lax.rem(pl.program_id(0), tiles_per_seq) * tm
    block = jnp.right_shift(t0 + lax.broadcasted_iota(jnp.int32, (tm, LANE), 0), int(math.log2(SEL_BLOCK)))
    kc_ref[0, 0] = slot(0)[:, :HEAD_DIM]
    vc_ref[0, 0] = slot(1)[:, :HEAD_DIM]
    ks_ref[0, 0] = jnp.concatenate([slot(2) + jnp.where(lane == HEAD_DIM + 1, 1.0, one_lane),
                                    jnp.where(lane == block, 1.0, 0.0)], axis=1).astype(BF16)
    vs_ref[0, 0] = (slot(3) + one_lane).astype(BF16)
    kw_ref[0, 0] = slot(4).astype(BF16)
    vw_ref[0, 0] = (slot(5) + one_lane).astype(BF16)


def _kv_proj(x, w_kv, bsz, t, tm=512):
    n, d = x.shape
    tps = t // tm
    g = NSA_GROUPS

    def out(width, dtype):
        return (pl.BlockSpec((1, 1, tm, width), lambda i, j: (i // tps, j, i % tps, 0)),
                jax.ShapeDtypeStruct((bsz, g, t, width), dtype))

    outs = [out(HEAD_DIM, F32), out(HEAD_DIM, F32), out(2 * LANE, BF16), out(LANE, BF16), out(LANE, BF16),
            out(LANE, BF16)]
    return pl.pallas_call(
        functools.partial(_kv_proj_kernel, tps),
        grid=(n // tm, g),
        in_specs=[pl.BlockSpec((tm, d), lambda i, j: (i, 0)),
                  pl.BlockSpec((g, d, KV_ARRAYS * LANE), lambda i, j: (0, 0, 0))],
        out_specs=[o[0] for o in outs],
        out_shape=[o[1] for o in outs],
        scratch_shapes=[pltpu.VMEM((tm, d), BF16)],
        compiler_params=_cparams(("parallel", "arbitrary")),
        name="nsa_kv_proj",
    )(x, w_kv)


def _compress_kernel(u_ref, pe_ref, w1_ref, w2_ref, o_ref):
    u = u_ref[0, 0]
    a = _dot((u + pe_ref[0:1, :]).astype(BF16), w1_ref[0])
    b = _dot((u + pe_ref[1:2, :]).astype(BF16), w1_ref[1])
    nc = u.shape[0]
    h = a + pltpu.roll(b, nc - 1, 0)
    h = 0.5 * h * (1.0 + jnp.tanh(math.sqrt(2.0 / math.pi) * (h + 0.044715 * (h * h * h))))
    y = _dot(h.astype(BF16), w2_ref[...])
    row = lax.broadcasted_iota(jnp.int32, y.shape, 0)
    o_ref[0, 0] = jnp.where(row < nc - 1, y, 0.0)


def _compress(u, pe2, w1, w2p):
    b, g, nc, _ = u.shape
    return pl.pallas_call(
        _compress_kernel,
        grid=(b, g),
        in_specs=[pl.BlockSpec((1, 1, nc, 1024), lambda i, j: (i, j, 0, 0)),
                  pl.BlockSpec((2, 1024), lambda i, j: (0, 0)),
                  pl.BlockSpec((2, 1024, CMP_HIDDEN), lambda i, j: (0, 0, 0)),
                  pl.BlockSpec((CMP_HIDDEN, LANE), lambda i, j: (0, 0))],
        out_specs=pl.BlockSpec((1, 1, nc, LANE), lambda i, j: (i, j, 0, 0)),
        out_shape=jax.ShapeDtypeStruct((b, g, nc, LANE), F32),
        compiler_params=_cparams(("parallel", "parallel")),
        name="nsa_compress",
    )(u, pe2, w1, w2p)


def _stack_heads(q_ref, row0=0):
    q = q_ref[row0:row0 + Q_BLOCK, :] * (HEAD_DIM ** -0.5)
    return jnp.concatenate([q[:, h * LANE:(h + 1) * LANE] for h in range(NSA_HPG)], axis=0).astype(BF16)


def _store_heads(o_ref, o, gates, branch, row0=0):
    lane = lax.broadcasted_iota(jnp.int32, (Q_BLOCK, LANE), 1)
    for pair in range(NSA_HPG // 2):
        halves = []
        for h in (2 * pair, 2 * pair + 1):
            c = branch * NSA_HPG + h
            halves.append(o[h * Q_BLOCK:(h + 1) * Q_BLOCK, :] * gates[:, c:c + 1])
        o_ref[0, pair, row0:row0 + Q_BLOCK, :] = jnp.where(lane < HEAD_DIM, halves[0],
                                                           pltpu.roll(halves[1], HEAD_DIM, 1)).astype(o_ref.dtype)


CMP_QB = 4


def _cmp_kernel(q_ref, zg_ref, kc_ref, vc_ref, *refs):
    pc_refs, (ov_ref, o_ref, m_ref), scratch = refs[:CMP_QB], refs[CMP_QB:CMP_QB + 3], refs[CMP_QB + 3:]
    nc = kc_ref.shape[2]
    nsel = ov_ref.shape[0]
    kc = kc_ref[0, 0].astype(BF16)
    vc = vc_ref[0, 0].astype(BF16)
    gates = _sigmoid(zg_ref[...])
    blocks = [(qb, pl.program_id(2) * CMP_QB + qb, pc_ref, scratch[2 * qb], scratch[2 * qb + 1])
              for qb, pc_ref in enumerate(pc_refs)]
    for qb, c, _, p_s, _ in blocks:
        p_s[...] = _dot_nt(_stack_heads(q_ref, qb * Q_BLOCK), kc)
    for qb, c, pc_ref, p_s, pb_s in blocks:
        off = pl.multiple_of(lax.div((nc - 8) - 8 * c, LANE) * LANE, LANE)
        for r in range(NSA_HPG * Q_BLOCK // SEL_ROWS):
            rows = slice(r * SEL_ROWS, (r + 1) * SEL_ROWS)
            s = p_s[rows, :] + pc_ref[0, 0, rows, pl.ds(off, nc)]
            e = jnp.exp(s - jnp.max(s, axis=-1, keepdims=True))
            l = jnp.sum(e, axis=-1, keepdims=True)
            ql = (lax.broadcasted_iota(jnp.int32, (SEL_ROWS, 1), 0) + r * SEL_ROWS) & (Q_BLOCK - 1)
            p = e * jnp.where((c * Q_BLOCK + ql) >= (CMP_LEN - 1), 1.0 / l, 0.0)
            p_s[rows, :] = p
            pb_s[rows, :] = p.astype(BF16)
    vals, sels = [], []
    jb = lax.broadcasted_iota(jnp.int32, (nsel, Q_BLOCK), 0)
    jbf = jb.astype(F32)
    for qb, c, _, p_s, pb_s in blocks:
        _store_heads(o_ref, _dot(pb_s[...], vc), gates[qb * Q_BLOCK:(qb + 1) * Q_BLOCK, :], 0, qb * Q_BLOCK)
        psum = ((p_s[0:Q_BLOCK, :] + p_s[Q_BLOCK:2 * Q_BLOCK, :])
                + (p_s[2 * Q_BLOCK:3 * Q_BLOCK, :] + p_s[3 * Q_BLOCK:4 * Q_BLOCK, :]))
        hi = psum.astype(BF16)
        lo = (psum - hi.astype(F32)).astype(BF16)
        imp = _dot_nt(ov_ref[...], hi) + _dot_nt(ov_ref[...], lo)
        tq = c * Q_BLOCK + lax.broadcasted_iota(jnp.int32, (nsel, Q_BLOCK), 1)
        jq = jnp.right_shift(tq, int(math.log2(SEL_BLOCK)))
        forced = (jb == 0) | (jb == jq) | (jb == jq - 1)
        vals.append(jnp.where(forced, -3e38, jnp.where(jb <= jq, imp, -SEL_FORCE)))
        sels.append(jnp.where(forced, 1.0, 0.0))
    for _ in range(SEL_TOPN - 3):
        for qb in range(CMP_QB):
            mx = jnp.max(vals[qb], axis=0, keepdims=True)
            first = jnp.min(jnp.where(vals[qb] == mx, jbf, float(nsel)), axis=0, keepdims=True)
            hit = jbf == first
            sels[qb] = jnp.where(hit, jnp.where(mx >= 0.0, 1.0, 0.0), sels[qb])
            vals[qb] = jnp.where(hit, -3e38, vals[qb])
    for qb in range(CMP_QB):
        m_ref[0, 0, qb * Q_BLOCK:(qb + 1) * Q_BLOCK, :] = (sels[qb].T - 1.0) * (-NEG_INF)


def _cmp_branch(za, zb, k_cmp, v_cmp, pc_tab, overlap, bsz, t):
    tq = CMP_QB * Q_BLOCK
    nq = t // tq
    nc = k_cmp.shape[2]
    nsel = overlap.shape[0]
    rows = NSA_HPG * Q_BLOCK

    def variant(qb):
        return pl.BlockSpec((1, 1, rows, 2 * nc - LANE),
                            lambda b, g, c: (g, lax.rem((nc - 8) - 8 * (CMP_QB * c + qb), LANE) // 8, 0, 0))

    return pl.pallas_call(
        _cmp_kernel,
        grid=(bsz, NSA_GROUPS, nq),
        in_specs=[pl.BlockSpec((tq, NSA_HPG * LANE), lambda b, g, c: (b * nq + c, g)),
                  pl.BlockSpec((tq, LANE), lambda b, g, c: (b * nq + c, (C_G3 - C_SPLIT) // LANE + g)),
                  pl.BlockSpec((1, 1, nc, LANE), lambda b, g, c: (b, g, 0, 0)),
                  pl.BlockSpec((1, 1, nc, LANE), lambda b, g, c: (b, g, 0, 0)),
                  *[variant(qb) for qb in range(CMP_QB)],
                  pl.BlockSpec((nsel, nc), lambda b, g, c: (0, 0))],
        out_specs=[pl.BlockSpec((1, NSA_HPG // 2, tq, LANE), lambda b, g, c: (b, g, c, 0)),
                   pl.BlockSpec((1, 1, tq, nsel), lambda b, g, c: (b, g, c, 0))],
        out_shape=[jax.ShapeDtypeStruct((bsz, NSA_HEADS // 2, t, LANE), BF16),
                   jax.ShapeDtypeStruct((bsz, NSA_GROUPS, t, nsel), F32)],
        scratch_shapes=[pltpu.VMEM((rows, nc), F32), pltpu.VMEM((rows, nc), BF16)] * CMP_QB,
        compiler_params=_cparams(("parallel", "parallel", "arbitrary")),
        name="nsa_cmp_select",
    )(za, zb, k_cmp, v_cmp, *[pc_tab] * CMP_QB, overlap)


WIN_QB = 4


def _win_kernel(q_ref, zg_ref, k_ref, v_ref, wb_ref, o_ref, *scratch):
    nvar = wb_ref.shape[1]
    gates = _sigmoid(zg_ref[...])
    blocks = []
    for qb in range(WIN_QB):
        c = pl.program_id(2) * WIN_QB + qb
        start = pl.multiple_of(jnp.maximum(c * Q_BLOCK - WINDOW, 0), Q_BLOCK)
        blocks.append((qb, jnp.minimum(c, nvar - 1), pl.ds(start, WINDOW + Q_BLOCK), scratch[2 * qb], scratch[2 * qb + 1]))
    for qb, _, keys, s_s, _ in blocks:
        s_s[...] = _dot_nt(_stack_heads(q_ref, qb * Q_BLOCK), k_ref[0, 0, keys, :])
    for qb, var, _, s_s, p_s in blocks:
        for r in range(NSA_HPG * Q_BLOCK // SEL_ROWS):
            rows = slice(r * SEL_ROWS, (r + 1) * SEL_ROWS)
            s = s_s[rows, :] + wb_ref[0, var, rows, :]
            p_s[rows, :] = jnp.exp(s - jnp.max(s, axis=-1, keepdims=True)).astype(BF16)
    for qb, _, keys, _, p_s in blocks:
        o = _dot(p_s[...], v_ref[0, 0, keys, :])
        o = o * (1.0 / o[:, HEAD_DIM:HEAD_DIM + 1])
        _store_heads(o_ref, o, gates[qb * Q_BLOCK:(qb + 1) * Q_BLOCK, :], 2, qb * Q_BLOCK)


def _win_branch(za, zb, kw, vw, wb_tab, bsz, t):
    tq = WIN_QB * Q_BLOCK
    nq = t // tq
    rows, width = NSA_HPG * Q_BLOCK, WINDOW + Q_BLOCK
    return pl.pallas_call(
        _win_kernel,
        grid=(bsz, NSA_GROUPS, nq),
        in_specs=[pl.BlockSpec((tq, NSA_HPG * LANE), lambda b, g, c: (b * nq + c, g)),
                  pl.BlockSpec((tq, LANE), lambda b, g, c: (b * nq + c, (C_G3 - C_SPLIT) // LANE + g)),
                  pl.BlockSpec((1, 1, t, LANE), lambda b, g, c: (b, g, 0, 0)),
                  pl.BlockSpec((1, 1, t, LANE), lambda b, g, c: (b, g, 0, 0)),
                  pl.BlockSpec((1,) + wb_tab.shape[1:], lambda b, g, c: (g, 0, 0, 0))],
        out_specs=pl.BlockSpec((1, NSA_HPG // 2, tq, LANE), lambda b, g, c: (b, g, c, 0)),
        out_shape=jax.ShapeDtypeStruct((bsz, NSA_HEADS // 2, t, LANE), BF16),
        scratch_shapes=[pltpu.VMEM((rows, width), F32), pltpu.VMEM((rows, width), BF16)] * WIN_QB,
        compiler_params=_cparams(("parallel", "parallel", "arbitrary")),
        name="nsa_window",
    )(za, zb, kw, vw, wb_tab)


SEL_TK = 512


SEL_ROWS = 64
SEL_QB = 4
SEL_NEAR = FAR_BIAS_DIST + Q_BLOCK


def _sel_kernel(q_ref, zg_ref, k_ref, v_ref, mt_ref, lt_ref, cq_ref, o_ref, m_s, acc_s, s0_s, s1_s, p0_s, p1_s,
                a0_s, a1_s):
    rows_per_block = NSA_HPG * Q_BLOCK
    c0 = pl.program_id(2) * SEL_QB
    q = q_ref[...] * (HEAD_DIM ** -0.5)
    parts = []
    for qb in range(SEL_QB):
        qrows = slice(qb * Q_BLOCK, (qb + 1) * Q_BLOCK)
        qs = jnp.concatenate([q[qrows, h * LANE:(h + 1) * LANE] for h in range(NSA_HPG)], axis=0) + cq_ref[0]
        parts.append(jnp.concatenate([qs, jnp.concatenate([mt_ref[0, 0, qrows, :]] * NSA_HPG, axis=0)], axis=1))
    q_aug = jnp.concatenate(parts, axis=0).astype(BF16)
    m_s[...] = jnp.full(m_s.shape, NEG_INF, F32)
    acc_s[...] = jnp.zeros(acc_s.shape, F32)
    ncol = SEL_TK // LANE
    n_tiles = lax.div(c0 + SEL_QB - 1, SEL_TK // Q_BLOCK) + 1

    def key_rows(i):
        return pl.ds(pl.multiple_of(jnp.minimum(i, n_tiles - 1) * SEL_TK, SEL_TK), SEL_TK)

    def scores(i, dst):
        dst[...] = _dot_nt(q_aug, k_ref[0, 0, key_rows(i), :])

    def tile(i, cur, nxt, p_buf, a_buf):
        scores(i + 1, nxt)
        for qb in range(SEL_QB):
            off = jnp.where(i < n_tiles, jnp.maximum(SEL_NEAR + SEL_TK - ((c0 + qb) * Q_BLOCK - i * SEL_TK), 0),
                            SEL_NEAR + SEL_TK + Q_BLOCK)
            off = pl.multiple_of(off, LANE)
            for r in range(rows_per_block // SEL_ROWS):
                trows = slice(r * SEL_ROWS, (r + 1) * SEL_ROWS)
                rows = slice(qb * rows_per_block + r * SEL_ROWS, qb * rows_per_block + (r + 1) * SEL_ROWS)
                s = cur[rows, :] + lt_ref[0, trows, pl.ds(off, SEL_TK)]
                cols = [s[:, j * LANE:(j + 1) * LANE] for j in range(ncol)]
                mx = functools.reduce(jnp.maximum, cols)
                m_old = m_s[rows, :]
                m_new = jnp.maximum(m_old, jnp.max(mx, axis=-1, keepdims=True))
                m_s[rows, :] = m_new
                a_buf[rows, :] = jnp.exp(m_old - m_new)
                p_buf[rows, :] = jnp.concatenate([jnp.exp(col - m_new) for col in cols], axis=1).astype(BF16)
        acc_s[...] = a_buf[...] * acc_s[...] + _dot(p_buf[...], v_ref[0, 0, key_rows(i), :])

    scores(0, s0_s)

    def pair(j, carry):
        tile(2 * j, s0_s, s1_s, p0_s, a0_s)
        tile(2 * j + 1, s1_s, s0_s, p1_s, a1_s)
        return carry

    lax.fori_loop(0, lax.div(n_tiles + 1, 2), pair, 0)
    acc = acc_s[...]
    o = acc * (1.0 / acc[:, HEAD_DIM:HEAD_DIM + 1])
    gates = _sigmoid(zg_ref[...])
    for qb in range(SEL_QB):
        _store_heads(o_ref, o[qb * rows_per_block:(qb + 1) * rows_per_block, :],
                     gates[qb * Q_BLOCK:(qb + 1) * Q_BLOCK, :], 1, qb * Q_BLOCK)


def _sel_branch(za, zb, ks_aug, vs, mt, lt_tab, cq_tab, bsz, t):
    tq = SEL_QB * Q_BLOCK
    nq = t // tq
    nsel = mt.shape[-1]
    trows = NSA_HPG * Q_BLOCK
    rows = SEL_QB * trows
    return pl.pallas_call(
        _sel_kernel,
        grid=(bsz, NSA_GROUPS, nq),
        in_specs=[pl.BlockSpec((tq, NSA_HPG * LANE), lambda b, g, c: (b * nq + c, g)),
                  pl.BlockSpec((tq, LANE), lambda b, g, c: (b * nq + c, (C_G3 - C_SPLIT) // LANE + g)),
                  pl.BlockSpec((1, 1, t, 2 * LANE), lambda b, g, c: (b, g, 0, 0)),
                  pl.BlockSpec((1, 1, t, LANE), lambda b, g, c: (b, g, 0, 0)),
                  pl.BlockSpec((1, 1, tq, nsel), lambda b, g, c: (b, g, c, 0)),
                  pl.BlockSpec((1, trows, lt_tab.shape[2]), lambda b, g, c: (g, 0, 0)),
                  pl.BlockSpec((1, trows, LANE), lambda b, g, c: (g, 0, 0))],
        out_specs=pl.BlockSpec((1, NSA_HPG // 2, tq, LANE), lambda b, g, c: (b, g, c, 0)),
        out_shape=jax.ShapeDtypeStruct((bsz, NSA_HEADS // 2, t, LANE), BF16),
        scratch_shapes=[pltpu.VMEM((rows, LANE), F32),
                        pltpu.VMEM((rows, LANE), F32),
                        pltpu.VMEM((rows, SEL_TK), F32),
                        pltpu.VMEM((rows, SEL_TK), F32),
                        pltpu.VMEM((rows, SEL_TK), BF16),
                        pltpu.VMEM((rows, SEL_TK), BF16),
                        pltpu.VMEM((rows, LANE), F32),
                        pltpu.VMEM((rows, LANE), F32)],
        compiler_params=_cparams(("parallel", "parallel", "arbitrary")),
        name="nsa_selected",
    )(za, zb, ks_aug, vs, mt, lt_tab, cq_tab)


def _shift_mix(z_ref, prev_ref, mu, first):
    z = z_ref[...]
    prev = jnp.where(first, 0.0, prev_ref[7:8, :])
    row = lax.broadcasted_iota(jnp.int32, z.shape, 0)
    zp = jnp.where(row == 0, prev, pltpu.roll(z, 1, 0))
    return z + mu * (zp - z)


def _rwkv_prep_kernel(has_vres, tiles_per_seq, *refs):
    (zr, zk, zv, zwa, zg, pr, pk, pv, pwa, pg, mu_ref, vec_ref, w2_ref, a2_ref, g2_ref, hsum_ref) = refs[:16]
    pos = 16
    if has_vres:
        vf_ref, v0_ref, v1_ref, v2_ref = refs[pos:pos + 4]
        pos += 4
    r_o, w_o, k_o, v_o, kk_o, b_o, g_o, bonus_o, vt_o = refs[pos:]
    first = lax.rem(pl.program_id(0), tiles_per_seq) == 0
    r = _shift_mix(zr, pr, mu_ref[0:1, :], first)
    k = _shift_mix(zk, pk, mu_ref[1:2, :], first)
    v = _shift_mix(zv, pv, mu_ref[2:3, :], first)
    wa = _shift_mix(zwa, pwa, mu_ref[3:4, 0:LANE], first)
    zg_s = _shift_mix(zg, pg, mu_ref[4:5, 0:LANE], first)
    w0, a0, k_k, k_a, r_k = (vec_ref[i:i + 1, :] for i in range(5))
    if has_vres:
        lora = _dot(_dot(v.astype(BF16), v1_ref[...]).astype(BF16), v2_ref[...])
        v = v + (vf_ref[...] - v) * _sigmoid(v0_ref[...] + lora)
    u = w0 + _dot(jnp.tanh(wa).astype(BF16), w2_ref[...])
    decay = jnp.exp(-math.exp(-0.5) * _sigmoid(u))
    a = _sigmoid(a0 + _dot(wa.astype(BF16), a2_ref[...]))
    g = _dot(_sigmoid(zg_s).astype(BF16), g2_ref[...])
    kk = k * k_k
    kk = kk / jnp.maximum(jnp.sqrt(_dot_split(kk * kk, hsum_ref[...])), 1e-12)
    k = k * (1.0 + (a - 1.0) * k_a)
    r_o[...] = r
    w_o[...] = decay
    k_o[...] = k.astype(k_o.dtype)
    v_o[...] = v
    kk_o[...] = kk
    b_o[...] = kk * a
    g_o[...] = g.astype(g_o.dtype)
    bonus_o[...] = (_dot_split(r * k * r_k, hsum_ref[...]) * v).astype(bonus_o.dtype)
    vt_o[0] = v.T.astype(BF16)


def _rwkv_prep(z, mu5, vecs, w2p, a2p, g2, hsum, vres, bsz, t, tm=512):
    n = bsz * t
    nt = n // tm
    tps = t // tm
    w = RWKV_WIDTH

    def cur(width, col):
        return pl.BlockSpec((tm, width), lambda i: (i, col // width))

    def prev(width, col):
        return pl.BlockSpec((8, width), lambda i: (jnp.maximum(i * (tm // 8) - 1, 0), col // width))

    def full(shape):
        return pl.BlockSpec(shape, lambda i: (0,) * len(shape))

    cols = [(w, C_R - C_SPLIT), (w, C_K - C_SPLIT), (w, C_V - C_SPLIT), (LANE, C_WA - C_SPLIT), (LANE, C_ZG - C_SPLIT)]
    in_specs = [cur(*c) for c in cols] + [prev(*c) for c in cols]
    in_specs += [full(mu5.shape), full(vecs.shape), full(w2p.shape), full(a2p.shape), full(g2.shape), full(hsum.shape)]
    args = [z] * 10 + [mu5, vecs, w2p, a2p, g2, hsum]
    if vres is not None:
        v_first, v0, v1p, v2p = vres
        in_specs += [pl.BlockSpec((tm, w), lambda i: (i, 0)), full(v0.shape), full(v1p.shape), full(v2p.shape)]
        args += [v_first, v0, v1p, v2p]
    tok = pl.BlockSpec((tm, w), lambda i: (i, 0))
    out_specs = [tok] * 8 + [pl.BlockSpec((1, w, tm), lambda i: (i // tps, 0, i % tps))]
    tok_dtypes = (F32, F32, BF16, F32, F32, F32, BF16, BF16)
    out_shape = [jax.ShapeDtypeStruct((n, w), dt) for dt in tok_dtypes] + [jax.ShapeDtypeStruct((bsz, w, t), BF16)]
    return pl.pallas_call(
        functools.partial(_rwkv_prep_kernel, vres is not None, tps),
        grid=(nt,),
        in_specs=in_specs,
        out_specs=out_specs,
        out_shape=out_shape,
        compiler_params=_cparams(("parallel",)),
        name="rwkv_prep",
    )(*args)


QUAD = 4
QUAD_LANES = QUAD * RWKV_HEAD
SUB_STEPS = LANE // QUAD


def _scan_kernel(r_ref, w_ref, k_ref, kk_ref, b_ref, vt_ref, ones_ref, place_ref, o_ref, s_ref, vq_ref, kq_ref, vk0_ref,
                 vk1_ref):
    bsz = r_ref.shape[0]
    tt = r_ref.shape[1]
    nh = RWKV_HEAD
    quads = [(bi, hf) for bi in range(bsz) for hf in range(RWKV_HEADS // QUAD)]

    @pl.when(pl.program_id(0) == 0)
    def _():
        s_ref[...] = jnp.zeros(s_ref.shape, F32)

    sub_shift, head_shift = int(math.log2(SUB_STEPS)), int(math.log2(nh))
    sub_lane = lax.broadcasted_iota(jnp.int32, (nh, LANE), 1) & (SUB_STEPS - 1)
    k_row_head = jnp.right_shift(lax.broadcasted_iota(jnp.int32, (LANE, QUAD_LANES), 0), sub_shift)
    k_lane_head = jnp.right_shift(lax.broadcasted_iota(jnp.int32, (LANE, QUAD_LANES), 1), head_shift)
    r_mask = (lax.broadcasted_iota(jnp.int32, (16, QUAD_LANES), 0)
              == jnp.right_shift(lax.broadcasted_iota(jnp.int32, (16, QUAD_LANES), 1), head_shift))
    ones_bd = ones_ref[...]

    def sub_block(sb, carry):
        s0 = pl.multiple_of(sb * SUB_STEPS, SUB_STEPS)
        for q, (bi, hf) in enumerate(quads):
            ln = slice(hf * QUAD_LANES, (hf + 1) * QUAD_LANES)
            vq = _dot(vt_ref[bi, (QUAD * hf) * nh:(QUAD * hf + 1) * nh, :], place_ref[sb * QUAD])
            for h in range(1, QUAD):
                vq = vq + _dot(vt_ref[bi, (QUAD * hf + h) * nh:(QUAD * hf + h + 1) * nh, :], place_ref[sb * QUAD + h])
            vq_ref[q] = vq.astype(BF16)
            k_sub = k_ref[bi, pl.ds(s0, SUB_STEPS), ln].astype(F32)
            kq_ref[q] = jnp.where(k_row_head == k_lane_head, jnp.concatenate([k_sub] * QUAD, axis=0), 0.0).astype(BF16)

        def group(tg, vk_ref, carry):
            t0 = pl.multiple_of(s0 + tg * 8, 8)
            rows = []
            for bi, hf in quads:
                ln = slice(hf * QUAD_LANES, (hf + 1) * QUAD_LANES)
                rows.append(tuple(ref[bi, pl.ds(t0, 8), ln] for ref in (w_ref, kk_ref, b_ref, r_ref)))
            onehots = [jnp.where(sub_lane == tg * 8 + u, 1.0, 0.0).astype(BF16) for u in range(8)]
            for q in range(len(quads)):
                vq = vq_ref[q]
                vk_ref[q] = _dot(jnp.concatenate([vq * oh for oh in onehots], axis=0), kq_ref[q])
            state = [s_ref[q] for q in range(len(quads))]
            for u in range(8):
                sk = _dot(jnp.concatenate([(state[q] * rows[q][1][u:u + 1, :]).astype(BF16)
                                           for q in range(len(quads))], axis=0), ones_bd)
                for q, (bi, hf) in enumerate(quads):
                    w8, _, b8, r8 = rows[q]
                    state[q] = (state[q] * w8[u:u + 1, :] - sk[q * nh:(q + 1) * nh, :] * b8[u:u + 1, :]
                                + vk_ref[q, u * nh:(u + 1) * nh, :])
                    r_lhs = jnp.where(r_mask, r8[u:u + 1, :], 0.0).astype(BF16)
                    o = _dot_nt(r_lhs, state[q].astype(BF16))
                    o_ref[t0 + u, bi, QUAD * hf:QUAD * (hf + 1), :] = o[0:QUAD, :]
            for q in range(len(quads)):
                s_ref[q] = state[q]
            return carry

        for tg in range(SUB_STEPS // 8):
            carry = group(tg, (vk0_ref, vk1_ref)[tg % 2], carry)
        return carry

    lax.fori_loop(0, tt // SUB_STEPS, sub_block, 0)


def _rwkv_scan(r, w, k, kk, b, vt, ones_bd, place, bsz, t):
    tt = SCAN_TT
    wd = RWKV_WIDTH
    nquad = bsz * RWKV_HEADS // QUAD
    tok = pl.BlockSpec((bsz, tt, wd), lambda i: (0, i, 0))
    return pl.pallas_call(
        _scan_kernel,
        grid=(t // tt,),
        in_specs=[tok, tok, tok, tok, tok,
                  pl.BlockSpec((bsz, wd, tt), lambda i: (0, 0, i)),
                  pl.BlockSpec(ones_bd.shape, lambda i: (0, 0)),
                  pl.BlockSpec(place.shape, lambda i: (0, 0, 0))],
        out_specs=pl.BlockSpec((tt, bsz, RWKV_HEADS, RWKV_HEAD), lambda i: (i, 0, 0, 0)),
        out_shape=jax.ShapeDtypeStruct((t, bsz, RWKV_HEADS, RWKV_HEAD), F32),
        scratch_shapes=[pltpu.VMEM((nquad, RWKV_HEAD, QUAD_LANES), F32),
                        pltpu.VMEM((nquad, RWKV_HEAD, LANE), BF16),
                        pltpu.VMEM((nquad, LANE, QUAD_LANES), BF16),
                        pltpu.VMEM((nquad, 8 * RWKV_HEAD, QUAD_LANES), F32),
                        pltpu.VMEM((nquad, 8 * RWKV_HEAD, QUAD_LANES), F32)],
        compiler_params=_cparams(("arbitrary",)),
        name="rwkv_scan",
    )(r, w, k, kk, b, vt, ones_bd, place)


def _merge_kernel(x_ref, oc_ref, os_ref, ow_ref, orw_ref, bonus_ref, g_ref, zgn_ref, zgr_ref, gn_ref, hsum_ref,
                  pn_ref, pr_ref, wo_ref, ln_ref, o_ref):
    o_n = jnp.concatenate([oc_ref[0, p].astype(F32) + os_ref[0, p].astype(F32) + ow_ref[0, p].astype(F32)
                           for p in range(NSA_HEADS // 2)], axis=1)
    y_n = _dot(o_n.astype(BF16), pn_ref[...])
    o = orw_ref[...]
    mu = _dot(o.astype(BF16), hsum_ref[...]) * (1.0 / RWKV_HEAD)
    d = o - mu
    var = _dot((d * d).astype(BF16), hsum_ref[...]) * (1.0 / RWKV_HEAD)
    o = d * lax.rsqrt(var + GN_EPS) * gn_ref[0:1, :] + gn_ref[1:2, :]
    o = ((o + bonus_ref[...]) * g_ref[...]).astype(BF16)
    y1 = _sigmoid(zgn_ref[...].astype(F32)) * y_n + _sigmoid(zgr_ref[...].astype(F32)) * _dot(o, pr_ref[...])
    y = _dot(y1.astype(BF16), wo_ref[...])
    o_ref[...] = _layer_norm(DN_ALPHA * x_ref[...] + y, ln_ref[0:1, :], ln_ref[1:2, :])


def _merge(x, o_c, o_s, o_w, o_rw, bonus, g, z, gn, hsum, p_nsa, p_rwkv, w_out, ln, bsz, t, tm=512):
    n = bsz * t
    tps = t // tm
    d = D_MODEL
    w = RWKV_WIDTH
    head = pl.BlockSpec((1, NSA_HEADS // 2, tm, LANE), lambda i: (i // tps, 0, i % tps, 0))
    tok = pl.BlockSpec((tm, w), lambda i: (i, 0))

    def full(a):
        return pl.BlockSpec(a.shape, lambda i: (0,) * a.ndim)

    return pl.pallas_call(
        _merge_kernel,
        grid=(n // tm,),
        in_specs=[pl.BlockSpec((tm, d), lambda i: (i, 0)), head, head, head, tok, tok, tok,
                  pl.BlockSpec((tm, d), lambda i: (i, C_GN // d)),
                  pl.BlockSpec((tm, d), lambda i: (i, C_GR // d)),
                  full(gn), full(hsum), full(p_nsa), full(p_rwkv), full(w_out), full(ln)],
        out_specs=pl.BlockSpec((tm, d), lambda i: (i, 0)),
        out_shape=jax.ShapeDtypeStruct((n, d), F32),
        compiler_params=_cparams(("parallel",)),
        name="mix_merge_ln",
    )(x, o_c, o_s, o_w, o_rw, bonus, g, z, z, gn, hsum, p_nsa, p_rwkv, w_out, ln)


def _xattn_kernel(x_ref, kt_ref, v_ref, wq_ref, wo_ref, ln_ref, rw_ref, o_ref, ob_ref, s_ref):
    x = x_ref[...]
    q = _dot(x.astype(BF16), wq_ref[...]).astype(BF16)
    outs = []
    for h in range(XATTN_HEADS):
        hs = slice(h * XATTN_HEAD, (h + 1) * XATTN_HEAD)
        s = _dot(q[:, hs], kt_ref[0, hs, :]) * (XATTN_HEAD ** -0.5)
        m = jnp.max(s, axis=-1, keepdims=True)
        p = jnp.exp(s - m)
        l = jnp.sum(p, axis=-1, keepdims=True)
        outs.append(_dot(p.astype(BF16), v_ref[0, :, hs]) * (1.0 / l))
    o = jnp.concatenate(outs, axis=1).astype(BF16)
    x2 = _layer_norm(DN_ALPHA * x + _dot(o, wo_ref[...]), ln_ref[0:1, :], ln_ref[1:2, :])
    o_ref[...] = x2
    hi = x2.astype(BF16)
    ob_ref[...] = hi
    lo = (x2 - hi.astype(F32)).astype(BF16)
    logits = _dot(hi, rw_ref[0]) + (_dot(lo, rw_ref[0]) + _dot(hi, rw_ref[1]))
    s_ref[...] = _sigmoid(logits)


def _xattn(x, kt, v, wq, wo, ln, rw2, bsz, t, tm=256):
    n = bsz * t
    tps = t // tm
    d = D_MODEL

    def full(a):
        return pl.BlockSpec(a.shape, lambda i: (0,) * a.ndim)

    return pl.pallas_call(
        _xattn_kernel,
        grid=(n // tm,),
        in_specs=[pl.BlockSpec((tm, d), lambda i: (i, 0)),
                  pl.BlockSpec((1,) + kt.shape[1:], lambda i: (i // tps, 0, 0)),
                  pl.BlockSpec((1,) + v.shape[1:], lambda i: (i // tps, 0, 0)),
                  full(wq), full(wo), full(ln), full(rw2)],
        out_specs=[pl.BlockSpec((tm, d), lambda i: (i, 0)), pl.BlockSpec((tm, d), lambda i: (i, 0)),
                   pl.BlockSpec((tm, LANE), lambda i: (i, 0))],
        out_shape=[jax.ShapeDtypeStruct((n, d), F32), jax.ShapeDtypeStruct((n, d), BF16),
                   jax.ShapeDtypeStruct((n, LANE), F32)],
        compiler_params=_cparams(("parallel",)),
        name="xattn_ln_router",
    )(x, kt, v, wq, wo, ln, rw2)


def _expert_kernel(meta_ref, x_ref, gate_ref, *refs):
    w_refs, o_ref, w_scr = refs[:6], refs[6], refs[7:]
    i = pl.program_id(0)
    nb = pl.num_programs(0)
    prev = jnp.maximum(i - 1, 0)

    for side in range(2):
        @pl.when((i == 0) | (meta_ref[side * nb + i] != meta_ref[side * nb + prev]))
        def _():
            for j in range(3):
                w_scr[3 * side + j][...] = w_refs[3 * side + j][0, 0].astype(BF16)

    n_used = meta_ref[2 * nb]

    @pl.when(i < n_used)
    def _():
        x = x_ref[...]
        y = jnp.zeros(o_ref.shape, F32)
        for side in range(2):
            wg, wu, wd = w_scr[3 * side:3 * side + 3]
            hg = _dot(x, wg[...])
            h = hg * _sigmoid(hg) * _dot(x, wu[...])
            y = y + gate_ref[:, side:side + 1] * _dot(h.astype(BF16), wd[...])
        o_ref[...] = y.astype(o_ref.dtype)

    @pl.when(i >= n_used)
    def _():
        o_ref[...] = jnp.zeros(o_ref.shape, o_ref.dtype)


def _experts(meta, rows, gates, w_gate, w_up, w_down, layer):
    n_rows, d = rows.shape
    ff = w_gate.shape[3]
    nb = n_rows // MOE_BLK

    def weight(shape, side):
        return pl.BlockSpec((1, 1) + shape, lambda i, m: (layer, m[side * nb + i], 0, 0))

    return pl.pallas_call(
        _expert_kernel,
        grid_spec=pltpu.PrefetchScalarGridSpec(
            num_scalar_prefetch=1,
            grid=(nb,),
            in_specs=[pl.BlockSpec((MOE_BLK, d), lambda i, m: (i, 0)),
                      pl.BlockSpec((MOE_BLK, TOP_K), lambda i, m: (i, 0))]
                     + [weight(s, side) for side in range(2) for s in ((d, ff), (d, ff), (ff, d))],
            out_specs=pl.BlockSpec((MOE_BLK, d), lambda i, m: (i, 0)),
            scratch_shapes=[pltpu.VMEM(s, BF16) for _ in range(2) for s in ((d, ff), (d, ff), (ff, d))]),
        out_shape=jax.ShapeDtypeStruct((n_rows, d), BF16),
        compiler_params=_cparams(("arbitrary",)),
        name="moe_experts",
    )(meta, rows, gates, w_gate, w_up, w_down, w_gate, w_up, w_down)


def _combine_kernel(x_ref, y_ref, ln_ref, o_ref):
    o_ref[...] = _layer_norm(DN_ALPHA * x_ref[...] + y_ref[...], ln_ref[0:1, :], ln_ref[1:2, :])


def _combine(x, y, ln, tm=512):
    n, d = x.shape
    tok = pl.BlockSpec((tm, d), lambda i: (i, 0))
    return pl.pallas_call(
        _combine_kernel,
        grid=(n // tm,),
        in_specs=[tok, tok, pl.BlockSpec(ln.shape, lambda i: (0, 0))],
        out_specs=tok,
        out_shape=jax.ShapeDtypeStruct((n, d), F32),
        compiler_params=_cparams(("parallel",)),
        name="moe_combine_ln",
    )(x, y, ln)


def _t5_bucket(dist):
    n = jnp.maximum(dist, 0)
    max_exact = REL_BUCKETS // 2
    log_ratio = jnp.log(jnp.maximum(n, 1).astype(F32) / max_exact) / math.log(REL_MAX_DIST / max_exact)
    large = jnp.minimum(max_exact + (log_ratio * (REL_BUCKETS - max_exact)).astype(jnp.int32), REL_BUCKETS - 1)
    return jnp.where(n < max_exact, n, large)


def _bias_tables(rel_table, t):
    nc = t // CMP_STRIDE
    n_dist = t + 2048
    onehot = (_t5_bucket(jnp.arange(n_dist))[:, None] == jnp.arange(REL_BUCKETS)[None, :]).astype(F32)
    f = jnp.dot(onehot, rel_table.astype(F32), precision=lax.Precision.HIGHEST).T
    pad = n_dist
    f_ext = jnp.concatenate([jnp.full((NSA_HEADS, pad), NEG_INF, F32), f], axis=1)

    def at(first, step, count):
        last = first + step * (count - 1)
        assert step < 0 and last + pad >= 0 and first < n_dist
        return lax.slice(f_ext, (0, last + pad), (NSA_HEADS, first + pad + 1), (1, -step))[:, ::-1]

    chunks_per_q = Q_BLOCK // CMP_STRIDE
    n_m = 2 * nc + chunks_per_q - 1
    small = jnp.stack([at(CMP_STRIDE * (nc - 8 + chunks_per_q - 1) + r - (CMP_LEN - 1), -CMP_STRIDE, n_m)
                       for r in range(CMP_STRIDE)], axis=1)
    pc = jnp.stack([small[:, :, chunks_per_q - 1 - a:chunks_per_q - 1 - a + 2 * nc] for a in range(chunks_per_q)], axis=1)
    pc = pc.reshape(NSA_GROUPS, NSA_HPG * Q_BLOCK, 2 * nc)
    pc = jnp.stack([pc[:, :, s:s + 2 * nc - LANE] for s in range(0, LANE, 8)], axis=1)
    nvar = WINDOW // Q_BLOCK + 1

    def toeplitz(base, width, hi, shift):
        period = width + Q_BLOCK
        row = jnp.concatenate([at(base, -1, width), at(base + Q_BLOCK, -1, Q_BLOCK)], axis=1) - shift
        m = np.arange(period)
        dist = base - np.where(m < width, m, m - period)
        row = jnp.where((dist < hi)[None], row, NEG_INF)
        flat = jnp.tile(row, (1, Q_BLOCK))[:, :Q_BLOCK * (period - 1)]
        return flat.reshape(NSA_HEADS, Q_BLOCK, period - 1)[:, :, :width]

    wb = jnp.stack([toeplitz(Q_BLOCK * v, WINDOW + Q_BLOCK, WINDOW, 0.0) for v in range(nvar)], axis=1)
    wb = wb.reshape(NSA_GROUPS, NSA_HPG, nvar, Q_BLOCK, -1).transpose(0, 2, 1, 3, 4)
    wb = wb.reshape(NSA_GROUPS, nvar, NSA_HPG * Q_BLOCK, -1)
    far = rel_table[REL_BUCKETS - 1].astype(F32)
    lt = toeplitz(SEL_NEAR + SEL_TK, SEL_NEAR + 2 * SEL_TK + Q_BLOCK, n_dist, far[:, None])
    lt = lt.reshape(NSA_GROUPS, NSA_HPG * Q_BLOCK, -1)
    far_hi = far.astype(BF16).astype(F32)
    lane = jnp.arange(LANE)[None, :]
    cq = jnp.where(lane == HEAD_DIM, far_hi[:, None], jnp.where(lane == HEAD_DIM + 1, (far - far_hi)[:, None], 0.0))
    cq = jnp.broadcast_to(cq[:, None, :], (NSA_HEADS, Q_BLOCK, LANE)).reshape(NSA_GROUPS, NSA_HPG * Q_BLOCK, LANE)
    return pc, wb, lt, cq


def _in_proj_perm():
    perm = np.full((IN_PAD,), -1, np.int64)
    for h in range(NSA_HEADS):
        perm[C_Q + h * LANE:C_Q + h * LANE + HEAD_DIM] = np.arange(HEAD_DIM) + h * HEAD_DIM
    rw0 = NSA_COLS
    g0 = NSA_COLS + RWKV_COLS
    perm[C_GN:C_GN + D_MODEL] = g0 + np.arange(D_MODEL)
    perm[C_GR:C_GR + D_MODEL] = g0 + D_MODEL + np.arange(D_MODEL)
    perm[C_R:C_R + RWKV_WIDTH] = rw0 + np.arange(RWKV_WIDTH)
    perm[C_K:C_K + RWKV_WIDTH] = rw0 + RWKV_WIDTH + np.arange(RWKV_WIDTH)
    perm[C_V:C_V + RWKV_WIDTH] = rw0 + 2 * RWKV_WIDTH + np.arange(RWKV_WIDTH)
    for g in range(NSA_GROUPS):
        for br in range(3):
            for h in range(NSA_HPG):
                perm[C_G3 + g * LANE + br * NSA_HPG + h] = NSA_WIDTH + 6 * NSA_KV + (g * NSA_HPG + h) * 3 + br
    perm[C_WA:C_WA + LORA_W + LORA_A] = rw0 + 3 * RWKV_WIDTH + np.arange(LORA_W + LORA_A)
    perm[C_ZG:C_ZG + LORA_G] = rw0 + 3 * RWKV_WIDTH + LORA_W + LORA_A + np.arange(LORA_G)
    return perm


_PERM = _in_proj_perm()


def _permute_in_proj(w):
    pieces = []
    start = 0
    while start < IN_PAD:
        stop = start + 1
        if _PERM[start] < 0:
            while stop < IN_PAD and _PERM[stop] < 0:
                stop += 1
            pieces.append(jnp.zeros((w.shape[0], stop - start), BF16))
        else:
            while stop < IN_PAD and _PERM[stop] == _PERM[stop - 1] + 1:
                stop += 1
            pieces.append(w[:, int(_PERM[start]):int(_PERM[stop - 1]) + 1].astype(BF16))
        start = stop
    return jnp.concatenate(pieces, axis=1)


def _pad_to(a, shape):
    return jnp.pad(a, [(0, s - d) for d, s in zip(a.shape, shape)])


_CLASS_EXPERTS = np.asarray([(grp * EXPERTS_PER_GROUP + i, grp * EXPERTS_PER_GROUP + j)
                             for grp in range(N_EXPERT_GROUPS)
                             for i in range(EXPERTS_PER_GROUP) for j in range(i + 1, EXPERTS_PER_GROUP)], np.int32)
N_PAIR_CLASSES = len(_CLASS_EXPERTS)


def _route(s, router_bias, n_tok):
    s16 = s[:, :N_EXPERTS]
    s_sel = s16 + router_bias.astype(F32)

    def top2(a):
        i1 = jnp.argmax(a, axis=-1)
        rest = jnp.where(jnp.arange(a.shape[-1]) == i1[..., None], -jnp.inf, a)
        i2 = jnp.argmax(rest, axis=-1)
        return jnp.max(a, axis=-1) + jnp.max(rest, axis=-1), jnp.stack([i1, i2], axis=-1)

    group_score, _ = top2(s_sel.reshape(n_tok, N_EXPERT_GROUPS, EXPERTS_PER_GROUP))
    group = jnp.argmax(group_score, axis=-1)
    in_group = (jnp.arange(N_EXPERTS) // EXPERTS_PER_GROUP)[None, :] == group[:, None]
    _, e_idx = top2(jnp.where(in_group, s_sel, NEG_INF))
    picked = jnp.arange(N_EXPERTS)[None, None, :] == e_idx[:, :, None]
    s_top = jnp.sum(jnp.where(picked, s16[:, None, :], 0.0), axis=-1)
    gate = s_top / jnp.sum(s_top, axis=-1, keepdims=True)
    swap = e_idx[:, 0] > e_idx[:, 1]
    e_lo = jnp.where(swap, e_idx[:, 1], e_idx[:, 0])
    e_hi = jnp.where(swap, e_idx[:, 0], e_idx[:, 1])
    gate_lohi = jnp.where(swap[:, None], gate[:, ::-1], gate)
    l_lo, l_hi = e_lo % EXPERTS_PER_GROUP, e_hi % EXPERTS_PER_GROUP
    pairs_per_group = N_PAIR_CLASSES // N_EXPERT_GROUPS
    cls = ((e_lo // EXPERTS_PER_GROUP) * pairs_per_group
           + (l_lo * (2 * EXPERTS_PER_GROUP - 1 - l_lo)) // 2 + (l_hi - l_lo - 1))
    onehot = (cls[:, None] == jnp.arange(N_PAIR_CLASSES)[None, :]).astype(jnp.int32)
    counts = jnp.sum(onehot, axis=0)
    padded = (counts + MOE_BLK - 1) // MOE_BLK * MOE_BLK
    ends = jnp.cumsum(padded)
    before = jnp.cumsum(onehot, axis=0) - onehot + (ends - padded)[None, :]
    dest = jnp.sum(onehot * before, axis=1)
    n_rows = n_tok + N_PAIR_CLASSES * MOE_BLK
    tok_of_row = jnp.zeros((n_rows,), jnp.int32).at[dest].set(jnp.arange(n_tok, dtype=jnp.int32))
    block_start = jnp.arange(n_rows // MOE_BLK) * MOE_BLK
    block_cls = jnp.minimum(jnp.sum((ends[None, :] <= block_start[:, None]).astype(jnp.int32), axis=1),
                            N_PAIR_CLASSES - 1)
    block_onehot = (block_cls[:, None] == jnp.arange(N_PAIR_CLASSES)[None, :]).astype(jnp.int32)
    block_experts = jnp.sum(block_onehot[:, :, None] * jnp.asarray(_CLASS_EXPERTS)[None], axis=1)
    meta = jnp.concatenate([block_experts[:, 0], block_experts[:, 1], ends[-1:] // MOE_BLK]).astype(jnp.int32)
    return gate_lohi, dest, tok_of_row, meta


def kernel(x, mem, rel_table, router_w, router_bias, w_in, cmp_pe_k, cmp_pe_v, cmp_w1k, cmp_w2k, cmp_w1v, cmp_w2v, rwkv_mu, rwkv_w0, rwkv_w2, rwkv_a0, rwkv_a2, rwkv_g2, rwkv_kk, rwkv_ka, rwkv_rk, rwkv_gn_g, rwkv_gn_b, rwkv_v0, rwkv_v1, rwkv_v2, p_nsa, p_rwkv, w_out, ln1_g, ln1_b, xq_w, xk_w, xv_w, xo_w, ln2_g, ln2_b, moe_w_gate, moe_w_up, moe_w_down, ln3_g, ln3_b):
    bsz, t, d = x.shape
    n = bsz * t
    depth = w_in.shape[0]
    nc = t // CMP_STRIDE
    nsel = t // SEL_BLOCK
    g, hd = NSA_GROUPS, HEAD_DIM

    pc_tab, wb_tab, lt_tab, cq_tab = _bias_tables(rel_table, t)
    cmp_start = np.arange(nc) * CMP_STRIDE
    sel_start = np.arange(nsel) * SEL_BLOCK
    overlap = ((cmp_start[:, None] <= sel_start[None, :] + SEL_BLOCK - 1)
               & (cmp_start[:, None] + CMP_LEN - 1 >= sel_start[None, :]) & (cmp_start[:, None] < (nc - 1) * CMP_STRIDE))
    overlap = jnp.asarray(overlap.T, BF16)
    lane_head = np.arange(RWKV_WIDTH) // RWKV_HEAD
    hsum = jnp.asarray(lane_head[:, None] == lane_head[None, :], BF16)
    quad_head = np.arange(QUAD_LANES) // RWKV_HEAD
    ones_bd = jnp.asarray(quad_head[:, None] == quad_head[None, :], BF16)
    step = np.arange(SCAN_TT)[None, :, None]
    lane = np.arange(LANE)[None, None, :]
    sbh = np.arange(SCAN_TT // SUB_STEPS * QUAD)[:, None, None]
    place = jnp.asarray((lane // SUB_STEPS == sbh % QUAD) & (step == (sbh // QUAD) * SUB_STEPS + lane % SUB_STEPS), BF16)
    rw_f = _pad_to(router_w.astype(F32), (d, LANE))
    rw_hi = rw_f.astype(BF16)
    rw2 = jnp.stack([rw_hi, (rw_f - rw_hi.astype(F32)).astype(BF16)])

    xf = x.reshape(n, d)
    mem_f = mem.reshape(bsz * mem.shape[1], d)
    v_first = None
    for l in range(depth):
        w_in_l = _permute_in_proj(w_in[l])
        za, zb = _in_proj(xf, w_in_l)

        kv0 = NSA_WIDTH
        zeros64 = jnp.zeros((d, LANE - hd), BF16)
        w_kv = jnp.stack([jnp.concatenate(
            [piece for i in range(KV_ARRAYS)
             for piece in (w_in[l][:, kv0 + i * NSA_KV + gi * hd:kv0 + i * NSA_KV + (gi + 1) * hd].astype(BF16), zeros64)],
            axis=1) for gi in range(g)])
        kc, vc, ks_aug, vs_aug, kw_pad, vw_aug = _kv_proj(xf, w_kv, bsz, t)

        def cmp_w(pe, w1, w2):
            return (pe.reshape(2, CMP_STRIDE * hd), w1.reshape(2, CMP_STRIDE * hd, CMP_HIDDEN).astype(BF16),
                    _pad_to(w2, (CMP_HIDDEN, LANE)).astype(BF16))

        chunks = lambda a: a.reshape(bsz, g, nc, CMP_STRIDE * hd)
        k_cmp = _compress(chunks(kc), *cmp_w(cmp_pe_k[l], cmp_w1k[l], cmp_w2k[l]))
        v_cmp = _compress(chunks(vc), *cmp_w(cmp_pe_v[l], cmp_w1v[l], cmp_w2v[l]))
        o_c, sel_mask = _cmp_branch(za, zb, k_cmp, v_cmp, pc_tab, overlap, bsz, t)
        o_s = _sel_branch(za, zb, ks_aug, vs_aug, sel_mask, lt_tab, cq_tab, bsz, t)
        o_w = _win_branch(za, zb, kw_pad, vw_aug, wb_tab, bsz, t)

        mu = rwkv_mu[l]
        w3 = 3 * RWKV_WIDTH
        mu5 = jnp.stack([mu[0:512], mu[512:1024], mu[1024:1536], _pad_to(mu[w3:w3 + 128], (512,)),
                         _pad_to(mu[w3 + 128:w3 + 256], (512,))])
        vecs = jnp.stack([rwkv_w0[l], rwkv_a0[l], rwkv_kk[l], rwkv_ka[l], rwkv_rk[l].reshape(-1)])
        w2p = jnp.concatenate([rwkv_w2[l], jnp.zeros_like(rwkv_a2[l])], axis=0).astype(BF16)
        a2p = jnp.concatenate([jnp.zeros_like(rwkv_w2[l]), rwkv_a2[l]], axis=0).astype(BF16)
        vres = None
        if l > 0:
            vres = (v_first, rwkv_v0[l - 1][None, :], _pad_to(rwkv_v1[l - 1], (RWKV_WIDTH, LANE)).astype(BF16),
                    _pad_to(rwkv_v2[l - 1], (LANE, RWKV_WIDTH)).astype(BF16))
        r, w, k, v, kk, b, g_out, bonus, vt = _rwkv_prep(zb, mu5, vecs, w2p, a2p, rwkv_g2[l].astype(BF16), hsum, vres,
                                                          bsz, t)
        if l == 0:
            v_first = v
        seq = lambda a: a.reshape(bsz, t, RWKV_WIDTH)
        o_rw = _rwkv_scan(seq(r), seq(w), seq(k), seq(kk), seq(b), vt, ones_bd, place, bsz, t)
        o_rw = o_rw.transpose(1, 0, 2, 3).reshape(n, RWKV_WIDTH)

        xf = _merge(xf, o_c, o_s, o_w, o_rw, bonus, g_out, za, jnp.stack([rwkv_gn_g[l], rwkv_gn_b[l]]), hsum,
                    p_nsa[l].astype(BF16), p_rwkv[l].astype(BF16), w_out[l].astype(BF16),
                    jnp.stack([ln1_g[l], ln1_b[l]]), bsz, t)

        mlen = mem.shape[1]
        mk = _matmul(mem_f, xk_w[l].astype(BF16), BF16, mlen, XATTN_WIDTH)
        mv = _matmul(mem_f, xv_w[l].astype(BF16), BF16, mlen, XATTN_WIDTH)
        kt = mk.reshape(bsz, mlen, XATTN_WIDTH).transpose(0, 2, 1)
        xf, xf_b, scores = _xattn(xf, kt, mv.reshape(bsz, mlen, XATTN_WIDTH), xq_w[l].astype(BF16), xo_w[l].astype(BF16),
                            jnp.stack([ln2_g[l], ln2_b[l]]), rw2, bsz, t)

        gate, dest, tok_of_row, meta = _route(scores, router_bias, n)
        rows = jnp.take(xf_b, tok_of_row, axis=0)
        y_rows = _experts(meta, rows, jnp.take(gate, tok_of_row, axis=0), moe_w_gate, moe_w_up, moe_w_down, l)
        xf = _combine(xf, jnp.take(y_rows, dest, axis=0), jnp.stack([ln3_g[l], ln3_b[l]]))
    return xf.reshape(bsz, t, d)
```
